```python
import math
import jax, jax.numpy as jnp
from jax import lax
import numpy as np

D_MODEL = 1024
BATCH = 8
SEQ = 4096
DEPTH = 1

HEAD_DIM = 64
ATT_WIDTH = D_MODEL
CONV_WIDTH = D_MODEL
MIX_WIDTH = ATT_WIDTH + CONV_WIDTH
N_Q_HEADS = ATT_WIDTH // HEAD_DIM
N_KV_HEADS = 4
Q_PER_KV = N_Q_HEADS // N_KV_HEADS
KV_WIDTH = N_KV_HEADS * HEAD_DIM
N_CONV_GROUPS = CONV_WIDTH // HEAD_DIM
CONV_K = 31
DILATED_PATTERNS = ((128, 1), (512, 4), (2048, 16))
BLK = 128
NORM_EPS = 1e-6
LN_EPS = 1e-5
SPLIT_SIZES = (ATT_WIDTH, KV_WIDTH, KV_WIDTH, ATT_WIDTH, CONV_WIDTH, CONV_WIDTH, CONV_WIDTH)
IN_COLS = sum(SPLIT_SIZES)

kernel_name = "hybrid_dilated_attn_conformer_conv"


def rmsnorm(x, g):
    xf = x.astype(jnp.float32)
    y = xf * lax.rsqrt(jnp.mean(xf * xf, axis=-1, keepdims=True) + NORM_EPS)
    return (y * g.astype(jnp.float32)).astype(x.dtype)


def layernorm(x, g, b):
    xf = x.astype(jnp.float32)
    mu = jnp.mean(xf, axis=-1, keepdims=True)
    var = jnp.mean(jnp.square(xf - mu), axis=-1, keepdims=True)
    y = (xf - mu) * lax.rsqrt(var + LN_EPS)
    return (y * g.astype(jnp.float32) + b.astype(jnp.float32)).astype(x.dtype)


def alibi_slopes(n):
    return jnp.exp2(-8.0 * (jnp.arange(n, dtype=jnp.float32) + 1.0) / n)


def _to_blocks(t, dilation, n_blocks):
    b, s = t.shape[:2]
    rest = t.shape[2:]
    sub_len = s // dilation
    t = t.reshape((b, sub_len, dilation) + rest)
    t = jnp.moveaxis(t, 2, 1)
    t = jnp.pad(t, [(0, 0), (0, 0), (0, n_blocks * BLK - sub_len)] + [(0, 0)] * len(rest))
    return t.reshape((b, dilation, n_blocks, BLK) + rest)


def _from_blocks(t, seq):
    b, d, nb = t.shape[:3]
    rest = t.shape[4:]
    sub_len = seq // d
    t = t.reshape((b, d, nb * BLK) + rest)[:, :, :sub_len]
    t = jnp.moveaxis(t, 1, 2)
    return t.reshape((b, seq) + rest)


def _with_prev_block(t):
    prev = jnp.pad(t, [(0, 0), (0, 0), (1, 0)] + [(0, 0)] * (t.ndim - 3))[:, :, :-1]
    return jnp.concatenate([prev, t], axis=3)


def dilated_window_attention(q, k, v, slopes, window, dilation):
    seq = q.shape[1]
    sub_len = seq // dilation
    w = window // dilation
    nb = -(-sub_len // BLK)
    qb = _to_blocks(q, dilation, nb)
    kw = _with_prev_block(_to_blocks(k, dilation, nb))
    vw = _with_prev_block(_to_blocks(v, dilation, nb))
    s = jnp.einsum('brnqhgc,brnkhc->brnhgqk', qb, kw).astype(jnp.float32)
    qi = jnp.arange(BLK)[:, None]
    kj = jnp.arange(2 * BLK)[None, :]
    dist = BLK + qi - kj
    kpos = (jnp.arange(nb)[:, None, None] - 1) * BLK + kj
    valid = (dist >= 0) & (dist <= w) & (kpos >= 0)
    bias = -slopes[:, :, None, None] * (dist * dilation).astype(jnp.float32)
    s = jnp.where(valid[:, None, None], s + bias, -jnp.inf)
    m = jnp.max(s, axis=-1, keepdims=True)
    p = jnp.exp(s - m)
    l = jnp.sum(p, axis=-1)
    o = jnp.einsum('brnhgqk,brnkhc->brnqhgc', p, vw.astype(jnp.float32))
    l_q = jnp.moveaxis(l, -1, 3)
    o = o / l_q[..., None]
    lse = jnp.moveaxis(m[..., 0], -1, 3) + jnp.log(l_q)
    return _from_blocks(o, seq), _from_blocks(lse, seq)


def attention_branch(q, k, v, gate):
    b, s, _ = q.shape
    q = q.reshape(b, s, N_KV_HEADS, Q_PER_KV, HEAD_DIM) * (HEAD_DIM ** -0.5)
    k = k.reshape(b, s, N_KV_HEADS, HEAD_DIM)
    v = v.reshape(b, s, N_KV_HEADS, HEAD_DIM)
    slopes = alibi_slopes(N_Q_HEADS).reshape(N_KV_HEADS, Q_PER_KV)
    outs, lses = [], []
    for window, dilation in DILATED_PATTERNS:
        o, lse = dilated_window_attention(q, k, v, slopes, window, dilation)
        outs.append(o)
        lses.append(lse)
    wts = jax.nn.softmax(jnp.stack(lses, axis=0), axis=0)
    o = jnp.sum(wts[..., None] * jnp.stack(outs, axis=0), axis=0)
    o = o.reshape(b, s, ATT_WIDTH).astype(gate.dtype)
    return o * jax.nn.silu(gate)


def conv_branch(val, glu_gate, gate, conv_w, conv_b, ln_g, ln_b):
    h = val * jax.nn.sigmoid(glu_gate)
    h = lax.conv_general_dilated(
        h, conv_w.astype(h.dtype)[:, None, :], window_strides=(1,),
        padding=[(CONV_K - 1, 0)], dimension_numbers=('NWC', 'WIO', 'NWC'),
        feature_group_count=CONV_WIDTH) + conv_b.astype(h.dtype)
    h = layernorm(h, ln_g, ln_b)
    h = jax.nn.silu(h)
    return h * jax.nn.silu(gate)


def _fwd_setup_inputs(seed: int = 0) -> dict:
    key = jax.random.key(seed)
    ks = jax.random.split(key, 9)
    f32 = jnp.float32
    x = jax.random.normal(ks[0], (BATCH, SEQ, D_MODEL), f32)
    norm_g = 1.0 + 0.02 * jax.random.normal(ks[1], (DEPTH, D_MODEL), f32)
    w_in = jax.random.normal(ks[2], (DEPTH, D_MODEL, IN_COLS), f32) * D_MODEL ** -0.5
    conv_w = jax.random.normal(ks[3], (DEPTH, CONV_K, CONV_WIDTH), f32) * CONV_K ** -0.5
    conv_b = 0.02 * jax.random.normal(ks[4], (DEPTH, CONV_WIDTH), f32)
    conv_ln_g = 1.0 + 0.02 * jax.random.normal(ks[5], (DEPTH, CONV_WIDTH), f32)
    conv_ln_b = 0.02 * jax.random.normal(ks[6], (DEPTH, CONV_WIDTH), f32)
    w_out = jax.random.normal(ks[7], (DEPTH, MIX_WIDTH, D_MODEL), f32) * MIX_WIDTH ** -0.5
    final_norm_g = 1.0 + 0.02 * jax.random.normal(ks[8], (D_MODEL,), f32)
    return {"x": x, "norm_g": norm_g, "w_in": w_in, "conv_w": conv_w, "conv_b": conv_b,
            "conv_ln_g": conv_ln_g, "conv_ln_b": conv_ln_b, "w_out": w_out,
            "final_norm_g": final_norm_g}


def _fwd_reference(x, norm_g, w_in, conv_w, conv_b, conv_ln_g, conv_ln_b, w_out, final_norm_g):
    split_idx = list(np.cumsum(SPLIT_SIZES)[:-1])
    for layer in range(DEPTH):
        h = rmsnorm(x, norm_g[layer])
        proj = jnp.einsum('bsd,de->bse', h, w_in[layer])
        q, k, v, a_gate, c_val, c_glu, c_gate = jnp.split(proj, split_idx, axis=-1)
        y_att = attention_branch(q, k, v, a_gate)
        y_conv = conv_branch(c_val, c_glu, c_gate, conv_w[layer], conv_b[layer],
                             conv_ln_g[layer], conv_ln_b[layer])
        y = jnp.concatenate([y_att, y_conv], axis=-1)
        x = x + jnp.einsum('bse,ed->bsd', y, w_out[layer])
    return rmsnorm(x, final_norm_g)


import jax as _jax
import jax.numpy as _jnp

TWIN_FORMAT = 'train_step'
FWD_PARAMS = ['x', 'norm_g', 'w_in', 'conv_w', 'conv_b', 'conv_ln_g', 'conv_ln_b', 'w_out', 'final_norm_g']
TWIN_WEIGHTS = ['norm_g', 'w_in', 'conv_w', 'conv_b', 'conv_ln_g', 'conv_ln_b', 'w_out', 'final_norm_g']
TWIN_DIFF_INPUT = 'x'
TWIN_INPUTS = ['x', 'norm_g', 'w_in', 'conv_w', 'conv_b', 'conv_ln_g', 'conv_ln_b', 'w_out', 'final_norm_g', 'loss_target', 'm_norm_g', 'm_w_in', 'm_conv_w', 'm_conv_b', 'm_conv_ln_g', 'm_conv_ln_b', 'm_w_out', 'm_final_norm_g', 'v_norm_g', 'v_w_in', 'v_conv_w', 'v_conv_b', 'v_conv_ln_g', 'v_conv_ln_b', 'v_w_out', 'v_final_norm_g']
TWIN_OUTPUTS = ['loss', 'grad_x', 'grad_norm_g', 'grad_w_in', 'grad_conv_w', 'grad_conv_b', 'grad_conv_ln_g', 'grad_conv_ln_b', 'grad_w_out', 'grad_final_norm_g', 'delta_norm_g', 'delta_w_in', 'delta_conv_w', 'delta_conv_b', 'delta_conv_ln_g', 'delta_conv_ln_b', 'delta_w_out', 'delta_final_norm_g', 'new_m_norm_g', 'new_m_w_in', 'new_m_conv_w', 'new_m_conv_b', 'new_m_conv_ln_g', 'new_m_conv_ln_b', 'new_m_w_out', 'new_m_final_norm_g', 'new_v_norm_g', 'new_v_w_in', 'new_v_conv_w', 'new_v_conv_b', 'new_v_conv_ln_g', 'new_v_conv_ln_b', 'new_v_w_out', 'new_v_final_norm_g']
TWIN_LEAF_KINDS = {'loss': 'loss', 'grad_x': 'grad_x', 'grad_norm_g': 'grad_w', 'grad_w_in': 'grad_w', 'grad_conv_w': 'grad_w', 'grad_conv_b': 'grad_w', 'grad_conv_ln_g': 'grad_w', 'grad_conv_ln_b': 'grad_w', 'grad_w_out': 'grad_w', 'grad_final_norm_g': 'grad_w', 'delta_norm_g': 'delta_w', 'delta_w_in': 'delta_w', 'delta_conv_w': 'delta_w', 'delta_conv_b': 'delta_w', 'delta_conv_ln_g': 'delta_w', 'delta_conv_ln_b': 'delta_w', 'delta_w_out': 'delta_w', 'delta_final_norm_g': 'delta_w', 'new_m_norm_g': 'new_m', 'new_m_w_in': 'new_m', 'new_m_conv_w': 'new_m', 'new_m_conv_b': 'new_m', 'new_m_conv_ln_g': 'new_m', 'new_m_conv_ln_b': 'new_m', 'new_m_w_out': 'new_m', 'new_m_final_norm_g': 'new_m', 'new_v_norm_g': 'new_v', 'new_v_w_in': 'new_v', 'new_v_conv_w': 'new_v', 'new_v_conv_b': 'new_v', 'new_v_conv_ln_g': 'new_v', 'new_v_conv_ln_b': 'new_v', 'new_v_w_out': 'new_v', 'new_v_final_norm_g': 'new_v'}


def _forward(args):
    return _fwd_reference(*[args[k] for k in FWD_PARAMS])


def _output_shape():
    out = _jax.eval_shape(lambda: _forward(_fwd_setup_inputs(0)))
    return out.shape, out.dtype

N_MICROBATCH = 1
ADAM_LR = 0.001
ADAM_B1 = 0.9
ADAM_B2 = 0.999
ADAM_EPS = 1e-08
ADAM_WD = 0.01
ADAM_STEP = 10
PER_EXAMPLE_BATCH_AXIS = {'x': 0, 'loss_target': 0}
SHARED_INPUTS = []
_WEIGHT_DTYPES = {'norm_g': _jnp.float32, 'w_in': _jnp.float32, 'conv_w': _jnp.float32, 'conv_b': _jnp.float32, 'conv_ln_g': _jnp.float32, 'conv_ln_b': _jnp.float32, 'w_out': _jnp.float32, 'final_norm_g': _jnp.float32}
MOMENT_SCALE = {'norm_g': 8.767826e-02, 'w_in': 3.497028e-02, 'conv_w': 4.465014e-02, 'conv_b': 9.484781e-02, 'conv_ln_g': 5.279977e-02, 'conv_ln_b': 4.460821e-02, 'w_out': 5.196214e-02, 'final_norm_g': 3.196487e+01}


def _to_microbatches(a, axis):
    t = _jnp.moveaxis(a, axis, 0)
    t = t.reshape((N_MICROBATCH, t.shape[0] // N_MICROBATCH) + t.shape[1:])
    return _jnp.moveaxis(t, 1, axis + 1)


def setup_inputs(seed: int = 0) -> dict:
    inp = _fwd_setup_inputs(seed)
    key = _jax.random.fold_in(_jax.random.key(seed), 7919)
    shape, _ = _output_shape()
    out = dict(inp)
    out["loss_target"] = _jax.random.normal(_jax.random.fold_in(key, 0), shape, _jnp.float32)
    for i, name in enumerate(TWIN_WEIGHTS):
        w = inp[name].astype(_jnp.float32)
        if MOMENT_SCALE is None:
            s = _jnp.sqrt(_jnp.mean(_jnp.square(w)) + 1e-30)
        else:
            s = MOMENT_SCALE[name]
        km, kv = _jax.random.split(_jax.random.fold_in(key, i + 1))
        out[name] = w
        out["m_" + name] = s * _jax.random.normal(km, w.shape, _jnp.float32)
        out["v_" + name] = (s * s) * _jax.random.uniform(kv, w.shape, _jnp.float32, 0.5, 1.5)
    if N_MICROBATCH > 1:
        for name, axis in PER_EXAMPLE_BATCH_AXIS.items():
            out[name] = _to_microbatches(out[name], axis)
    return {'x': out['x'], 'norm_g': out['norm_g'], 'w_in': out['w_in'], 'conv_w': out['conv_w'], 'conv_b': out['conv_b'], 'conv_ln_g': out['conv_ln_g'], 'conv_ln_b': out['conv_ln_b'], 'w_out': out['w_out'], 'final_norm_g': out['final_norm_g'], 'loss_target': out['loss_target'], 'm_norm_g': out['m_norm_g'], 'm_w_in': out['m_w_in'], 'm_conv_w': out['m_conv_w'], 'm_conv_b': out['m_conv_b'], 'm_conv_ln_g': out['m_conv_ln_g'], 'm_conv_ln_b': out['m_conv_ln_b'], 'm_w_out': out['m_w_out'], 'm_final_norm_g': out['m_final_norm_g'], 'v_norm_g': out['v_norm_g'], 'v_w_in': out['v_w_in'], 'v_conv_w': out['v_conv_w'], 'v_conv_b': out['v_conv_b'], 'v_conv_ln_g': out['v_conv_ln_g'], 'v_conv_ln_b': out['v_conv_ln_b'], 'v_w_out': out['v_w_out'], 'v_final_norm_g': out['v_final_norm_g']}


def _loss(weights, diff, rest, loss_target):
    with _jax.named_scope("forward"):
        args = {**rest, TWIN_DIFF_INPUT: diff, **{k: w.astype(_WEIGHT_DTYPES[k]) for k, w in weights.items()}}
        y = _forward(args)
    with _jax.named_scope("loss_head"):
        err = _jnp.square(y.astype(_jnp.float32) - loss_target)
        return 0.5 * _jnp.sum(_jnp.mean(err, axis=-1)) if err.ndim else 0.5 * err


def _adamw(w, g, m, v):
    m = ADAM_B1 * m + (1.0 - ADAM_B1) * g
    v = ADAM_B2 * v + (1.0 - ADAM_B2) * _jnp.square(g)
    m_hat = m / (1.0 - ADAM_B1 ** ADAM_STEP)
    v_hat = v / (1.0 - ADAM_B2 ** ADAM_STEP)
    delta = -ADAM_LR * (m_hat / (_jnp.sqrt(v_hat) + ADAM_EPS) + ADAM_WD * w)
    return delta, m, v


def reference(x, norm_g, w_in, conv_w, conv_b, conv_ln_g, conv_ln_b, w_out, final_norm_g, loss_target, m_norm_g, m_w_in, m_conv_w, m_conv_b, m_conv_ln_g, m_conv_ln_b, m_w_out, m_final_norm_g, v_norm_g, v_w_in, v_conv_w, v_conv_b, v_conv_ln_g, v_conv_ln_b, v_w_out, v_final_norm_g):
    given = dict(x=x, norm_g=norm_g, w_in=w_in, conv_w=conv_w, conv_b=conv_b, conv_ln_g=conv_ln_g, conv_ln_b=conv_ln_b, w_out=w_out, final_norm_g=final_norm_g, loss_target=loss_target, m_norm_g=m_norm_g, m_w_in=m_w_in, m_conv_w=m_conv_w, m_conv_b=m_conv_b, m_conv_ln_g=m_conv_ln_g, m_conv_ln_b=m_conv_ln_b, m_w_out=m_w_out, m_final_norm_g=m_final_norm_g, v_norm_g=v_norm_g, v_w_in=v_w_in, v_conv_w=v_conv_w, v_conv_b=v_conv_b, v_conv_ln_g=v_conv_ln_g, v_conv_ln_b=v_conv_ln_b, v_w_out=v_w_out, v_final_norm_g=v_final_norm_g)
    weights = {n: given[n] for n in TWIN_WEIGHTS}
    shared = {n: given[n] for n in SHARED_INPUTS}
    per_example = {n: given[n] for n in ['x']}
    grad_fn = _jax.value_and_grad(_loss, argnums=(0, 1))

    def one_microbatch(ex, loss_target):
        ex = dict(ex)
        diff = ex.pop(TWIN_DIFF_INPUT)
        return grad_fn(weights, diff, {**shared, **ex}, loss_target)

    if N_MICROBATCH == 1:
        loss, (grad_w, grad_x) = one_microbatch(per_example, given["loss_target"])
    else:
        def body(carry, xs):
            loss_sum, grad_sum = carry
            l_k, (gw_k, gx_k) = one_microbatch(xs[0], xs[1])
            with _jax.named_scope("update"):
                return (loss_sum + l_k, _jax.tree.map(_jnp.add, grad_sum, gw_k)), gx_k

        init = (_jnp.zeros((), _jnp.float32), _jax.tree.map(_jnp.zeros_like, weights))
        (loss, grad_w), grad_x = _jax.lax.scan(body, init, (per_example, given["loss_target"]))
    with _jax.named_scope("update"):
        delta_w, new_m, new_v = {}, {}, {}
        for n in TWIN_WEIGHTS:
            delta_w[n], new_m[n], new_v[n] = _adamw(weights[n], grad_w[n], given["m_" + n], given["v_" + n])
    return (loss, grad_x, *[grad_w[n] for n in TWIN_WEIGHTS], *[delta_w[n] for n in TWIN_WEIGHTS],
            *[new_m[n] for n in TWIN_WEIGHTS], *[new_v[n] for n in TWIN_WEIGHTS])
```

```python
import functools

import jax
import jax.numpy as jnp
from jax import lax
from jax.experimental import pallas as pl
from jax.experimental.pallas import tpu as pltpu

F32 = jnp.float32
BF16 = jnp.bfloat16

HEAD_DIM = 64
N_KV_HEADS = 4
N_Q_HEADS = 16
ATT_W = 1024
KV_W = 256
CONV_K = 31
CONV_HALO = 32
PATTERNS = ((128, 1), (512, 4), (2048, 16))
BLK = 128
LANES = 128
NORM_EPS = 1e-6
LN_EPS = 1e-5
NEG = -1e30
N_DEV = 8
ADAM_LR, ADAM_B1, ADAM_B2, ADAM_EPS, ADAM_WD, ADAM_STEP = 0.001, 0.9, 0.999, 1e-08, 0.01, 10
VMEM_LIMIT = 48 * 1024 * 1024
SLOPES = tuple(2.0 ** (-8.0 * (h + 1) / N_Q_HEADS) for h in range(N_Q_HEADS))
MESH = pl.DeviceIdType.MESH


def _params(sem):
    return pltpu.CompilerParams(dimension_semantics=sem, vmem_limit_bytes=VMEM_LIMIT)


def _sigmoid(v):
    return 1.0 / (1.0 + jnp.exp(-v))


def _silu_and_grad(v):
    s = _sigmoid(v)
    return v * s, s * (1.0 + v * (1.0 - s))


def _exchange(arrays, scatter, name):
    na = len(arrays)
    out_shapes = []
    for a, s in zip(arrays, scatter):
        shp = a.shape if s else (N_DEV,) + a.shape
        out_shapes.append(jax.ShapeDtypeStruct(shp, a.dtype))

    def body(*refs):
        ins = refs[:na]
        outs = refs[na:2 * na]
        send_sems, recv_sems, loc_sems = refs[2 * na:]
        x, y, c = lax.axis_index("x"), lax.axis_index("y"), lax.axis_index("c")
        me = 4 * x + 2 * y + c

        def mine(a, dev):
            return ins[a].at[dev] if scatter[a] else ins[a]

        local = [pltpu.make_async_copy(mine(a, me), outs[a].at[me], loc_sems.at[a]) for a in range(na)]
        for cp in local:
            cp.start()
        sends = []
        for k in range(1, N_DEV):
            kx, ky, kc = (k >> 2) & 1, (k >> 1) & 1, k & 1
            px = 1 - x if kx else x
            py = 1 - y if ky else y
            pc = 1 - c if kc else c
            peer = 4 * px + 2 * py + pc
            for a in range(na):
                cp = pltpu.make_async_remote_copy(
                    src_ref=mine(a, peer), dst_ref=outs[a].at[me],
                    send_sem=send_sems.at[a, k - 1], recv_sem=recv_sems.at[a, k - 1],
                    device_id=(px, py, pc), device_id_type=MESH)
                cp.start()
                sends.append((cp, a, k, peer))
        for cp, a, k, peer in sends:
            pltpu.make_async_remote_copy(
                src_ref=mine(a, peer), dst_ref=outs[a].at[peer],
                send_sem=send_sems.at[a, k - 1], recv_sem=recv_sems.at[a, k - 1],
                device_id=(x, y, c), device_id_type=MESH).wait_recv()
        for cp, a, k, peer in sends:
            cp.wait_send()
        for cp in local:
            cp.wait()

    any_spec = pl.BlockSpec(memory_space=pl.ANY)
    return pl.pallas_call(
        body, name=name, out_shape=tuple(out_shapes),
        in_specs=[any_spec] * na, out_specs=tuple([any_spec] * na),
        scratch_shapes=[pltpu.SemaphoreType.DMA((na, N_DEV - 1)),
                        pltpu.SemaphoreType.DMA((na, N_DEV - 1)),
                        pltpu.SemaphoreType.DMA((na,))],
    )(*arrays)


def _inproj(x, g, w, col_block, ncols, out_dtype, scale, emit_h, name, tm=512, tn=512):
    S, D = x.shape

    def body(x_ref, g_ref, w_ref, *rest):
        if emit_h:
            o_ref, h_out, h_scr = rest
        else:
            o_ref, h_scr = rest

        @pl.when(pl.program_id(1) == 0)
        def _():
            xf = x_ref[...]
            r = lax.rsqrt(jnp.mean(xf * xf, axis=-1, keepdims=True) + NORM_EPS)
            h = (xf * r * g_ref[...]).astype(BF16)
            h_scr[...] = h
            if emit_h:
                h_out[...] = h

        acc = jnp.dot(h_scr[...], w_ref[...], preferred_element_type=F32)
        if scale != 1.0:
            acc = acc * scale
        o_ref[...] = acc.astype(out_dtype)

    out_shape = [jax.ShapeDtypeStruct((S, ncols), out_dtype)]
    out_specs = [pl.BlockSpec((tm, tn), lambda i, j: (i, j))]
    if emit_h:
        out_shape.append(jax.ShapeDtypeStruct((S, D), BF16))
        out_specs.append(pl.BlockSpec((tm, D), lambda i, j: (i, 0)))
    res = pl.pallas_call(
        body, name=name, grid=(S // tm, ncols // tn),
        in_specs=[pl.BlockSpec((tm, D), lambda i, j: (i, 0)),
                  pl.BlockSpec((1, D), lambda i, j: (0, 0)),
                  pl.BlockSpec((D, tn), lambda i, j: (0, col_block + j))],
        out_specs=tuple(out_specs), out_shape=tuple(out_shape),
        scratch_shapes=[pltpu.VMEM((tm, D), BF16)],
        compiler_params=_params(("parallel", "arbitrary")),
    )(x, g, w)
    return res if emit_h else res[0]


def _window_mask(n, dil):
    qi = lax.broadcasted_iota(jnp.int32, (BLK, 2 * BLK), 0)
    kj = lax.broadcasted_iota(jnp.int32, (BLK, 2 * BLK), 1)
    dist = BLK + qi - kj
    valid = (dist >= 0) & (dist <= BLK) & ((kj >= BLK) | (n > 0))
    negd = (dist * (-dil)).astype(F32)
    return valid, negd


def _head_operands(kv2, hk, lo_mask):
    half, pos = hk // 2, hk % 2
    out = []
    for base in (0, KV_W):
        t = kv2[:, base + half * LANES: base + (half + 1) * LANES].astype(F32)
        sw = pltpu.roll(t, HEAD_DIM, axis=1)
        at_lo, at_hi = (t, sw) if pos == 0 else (sw, t)
        out.append(jnp.where(lo_mask, at_lo, 0.0).astype(BF16))
        out.append(jnp.where(lo_mask, 0.0, at_hi).astype(BF16))
    return out


def _nt(a, b):
    return lax.dot_general(a, b, (((1,), (1,)), ((), ())), preferred_element_type=F32)


def _tn(a, b):
    return lax.dot_general(a, b, (((0,), (0,)), ((), ())), preferred_element_type=F32)


def _attn_fwd(q, kv, dil, name):
    S = q.shape[0]
    L = S // dil
    nb = L // BLK
    qv = q.reshape(L, dil * ATT_W)
    kvv = kv.reshape(L, dil * 2 * KV_W)

    def body(q_ref, kvc_ref, kvp_ref, o_ref, lse_ref):
        n = pl.program_id(1)
        valid, negd = _window_mask(n, dil)
        kv2 = jnp.concatenate([kvp_ref[...], kvc_ref[...]], axis=0)
        lo_mask = lax.broadcasted_iota(jnp.int32, (2 * BLK, LANES), 1) < HEAD_DIM
        lane = lax.broadcasted_iota(jnp.int32, (BLK, LANES), 1)
        stats = jnp.zeros((BLK, LANES), F32)
        for hk in range(N_KV_HEADS):
            k_lo, k_hi, v_lo, v_hi = _head_operands(kv2, hk, lo_mask)
            for jp in range(2):
                blk = hk * 2 + jp
                qp = q_ref[:, blk * LANES:(blk + 1) * LANES]
                o_pair = jnp.zeros((BLK, LANES), F32)
                for which, (kk, vv) in enumerate(((k_lo, v_lo), (k_hi, v_hi))):
                    h = 2 * blk + which
                    s = _nt(qp, kk) + SLOPES[h] * negd
                    s = jnp.where(valid, s, NEG)
                    m = jnp.max(s, axis=1, keepdims=True)
                    p = jnp.exp(s - m)
                    l = jnp.sum(p, axis=1, keepdims=True)
                    o_h = jnp.dot(p.astype(BF16), vv, preferred_element_type=F32)
                    o_pair = o_pair + o_h * (1.0 / l)
                    stats = jnp.where(lane == h, m + jnp.log(l), stats)
                o_ref[:, blk * LANES:(blk + 1) * LANES] = o_pair
        lse_ref[...] = stats

    o, lse = pl.pallas_call(
        body, name=name, grid=(dil, nb),
        in_specs=[pl.BlockSpec((BLK, ATT_W), lambda r, n: (n, r)),
                  pl.BlockSpec((BLK, 2 * KV_W), lambda r, n: (n, r)),
                  pl.BlockSpec((BLK, 2 * KV_W), lambda r, n: (jnp.maximum(n - 1, 0), r))],
        out_specs=(pl.BlockSpec((BLK, ATT_W), lambda r, n: (n, r)),
                   pl.BlockSpec((BLK, LANES), lambda r, n: (n, r))),
        out_shape=(jax.ShapeDtypeStruct((L, dil * ATT_W), F32),
                   jax.ShapeDtypeStruct((L, dil * LANES), F32)),
        compiler_params=_params(("parallel", "parallel")),
    )(qv, kvv, kvv)
    return o.reshape(S, ATT_W), lse.reshape(S, LANES)


def _attn_combine(os_, lses, gates, tm=256):
    S = os_[0].shape[0]
    npat = len(os_)

    def body(*refs):
        o_refs = refs[:npat]
        l_refs = refs[npat:2 * npat]
        gate_ref, o_out, lse_out, y_out = refs[2 * npat:]
        ls = [r[...] for r in l_refs]
        mx = functools.reduce(jnp.maximum, ls)
        es = [jnp.exp(l - mx) for l in ls]
        tot = functools.reduce(jnp.add, es)
        inv = 1.0 / tot
        ws = [e * inv for e in es]
        lse_out[...] = mx + jnp.log(tot)
        lo = lax.broadcasted_iota(jnp.int32, (tm, LANES), 1) < HEAD_DIM
        for blk in range(ATT_W // LANES):
            cols = slice(blk * LANES, (blk + 1) * LANES)
            acc = jnp.zeros((tm, LANES), F32)
            for p in range(npat):
                sc = jnp.where(lo, ws[p][:, 2 * blk:2 * blk + 1], ws[p][:, 2 * blk + 1:2 * blk + 2])
                acc = acc + sc * o_refs[p][:, cols]
            o_out[:, cols] = acc
            a = gate_ref[:, cols]
            y_out[:, cols] = (acc * (a * _sigmoid(a))).astype(BF16)

    row = lambda w: pl.BlockSpec((tm, w), lambda i: (i, 0))
    return pl.pallas_call(
        body, name="attn_combine", grid=(S // tm,),
        in_specs=[row(ATT_W)] * npat + [row(LANES)] * npat + [row(ATT_W)],
        out_specs=(row(ATT_W), row(LANES), row(ATT_W)),
        out_shape=(jax.ShapeDtypeStruct((S, ATT_W), F32), jax.ShapeDtypeStruct((S, LANES), F32),
                   jax.ShapeDtypeStruct((S, ATT_W), BF16)),
        compiler_params=_params(("parallel",)),
    )(*os_, *lses, gates)


def _conv_fwd(gates, conv_w, conv_b, ln_g, ln_b, tt=128):
    S = gates.shape[0]
    C = conv_w.shape[1]
    hb = tt // CONV_HALO

    def body(val_ref, glu_ref, hval_ref, hglu_ref, gate_ref, w_ref, b_ref, g_ref, beta_ref,
             conv_ref, y_ref, hbuf):
        i = pl.program_id(0)
        halo = hval_ref[...] * _sigmoid(hglu_ref[...])
        hbuf[0:CONV_HALO, :] = jnp.where(i > 0, halo, 0.0)
        hbuf[CONV_HALO:, :] = val_ref[...] * _sigmoid(glu_ref[...])
        for cb in range(C // LANES):
            cols = slice(cb * LANES, (cb + 1) * LANES)
            for rc in range(tt // 8):
                acc = jnp.zeros((8, LANES), F32)
                for j in range(CONV_K):
                    start = rc * 8 + CONV_HALO - (CONV_K - 1) + j
                    acc = acc + hbuf[start:start + 8, cols] * w_ref[j:j + 1, cols]
                conv_ref[rc * 8:(rc + 1) * 8, cols] = acc
        cv = conv_ref[...] + b_ref[...]
        conv_ref[...] = cv
        mu = jnp.mean(cv, axis=-1, keepdims=True)
        xc = cv - mu
        var = jnp.mean(xc * xc, axis=-1, keepdims=True)
        ln = xc * lax.rsqrt(var + LN_EPS) * g_ref[...] + beta_ref[...]
        gt = gate_ref[...]
        y_ref[...] = (ln * _sigmoid(ln) * (gt * _sigmoid(gt))).astype(BF16)

    vec = pl.BlockSpec((1, C), lambda i: (0, 0))
    return pl.pallas_call(
        body, name="conv_fwd", grid=(S // tt,),
        in_specs=[pl.BlockSpec((tt, C), lambda i: (i, 1)),
                  pl.BlockSpec((tt, C), lambda i: (i, 2)),
                  pl.BlockSpec((CONV_HALO, C), lambda i: (jnp.maximum(i * hb - 1, 0), 1)),
                  pl.BlockSpec((CONV_HALO, C), lambda i: (jnp.maximum(i * hb - 1, 0), 2)),
                  pl.BlockSpec((tt, C), lambda i: (i, 3)),
                  pl.BlockSpec((CONV_HALO, C), lambda i: (0, 0)), vec, vec, vec],
        out_specs=(pl.BlockSpec((tt, C), lambda i: (i, 0)), pl.BlockSpec((tt, C), lambda i: (i, 0))),
        out_shape=(jax.ShapeDtypeStruct((S, C), F32), jax.ShapeDtypeStruct((S, C), BF16)),
        scratch_shapes=[pltpu.VMEM((tt + CONV_HALO, C), F32)],
        compiler_params=_params(("parallel",)),
    )(gates, gates, gates, gates, gates, conv_w, conv_b, ln_g, ln_b)


def _outproj_loss(x, y_att, y_conv, w_out, gf, target, tm=512):
    S, D = x.shape
    E = y_att.shape[1]

    def body(x_ref, ya_ref, yc_ref, w_ref, gf_ref, t_ref, dx_ref, dxb_ref, loss_ref, ggf_ref):
        @pl.when(pl.program_id(0) == 0)
        def _():
            loss_ref[...] = jnp.zeros_like(loss_ref)
            ggf_ref[...] = jnp.zeros_like(ggf_ref)

        x2 = (x_ref[...] + jnp.dot(ya_ref[...], w_ref[0:E, :], preferred_element_type=F32)
              + jnp.dot(yc_ref[...], w_ref[E:, :], preferred_element_type=F32))
        r = lax.rsqrt(jnp.mean(x2 * x2, axis=-1, keepdims=True) + NORM_EPS)
        nrm = x2 * r
        gfv = gf_ref[...]
        err = nrm * gfv - t_ref[...]
        loss_ref[...] += jnp.sum(err * err, axis=0, keepdims=True)
        dout = err * (1.0 / D)
        ggf_ref[...] += jnp.sum(dout * nrm, axis=0, keepdims=True)
        dn = dout * gfv
        dx2 = r * (dn - nrm * jnp.mean(dn * nrm, axis=-1, keepdims=True))
        dx_ref[...] = dx2
        dxb_ref[...] = dx2.astype(BF16)

    row = lambda w: pl.BlockSpec((tm, w), lambda i: (i, 0))
    vec = pl.BlockSpec((1, D), lambda i: (0, 0))
    return pl.pallas_call(
        body, name="outproj_loss", grid=(S // tm,),
        in_specs=[row(D), row(E), row(E), pl.BlockSpec((2 * E, D), lambda i: (0, 0)), vec, row(D)],
        out_specs=(row(D), row(D), vec, vec),
        out_shape=(jax.ShapeDtypeStruct((S, D), F32), jax.ShapeDtypeStruct((S, D), BF16),
                   jax.ShapeDtypeStruct((1, D), F32), jax.ShapeDtypeStruct((1, D), F32)),
        compiler_params=_params(("arbitrary",)),
    )(x, y_att, y_conv, w_out, gf, target)


def _split3(v):
    hi = v.astype(BF16)
    r1 = v - hi.astype(F32)
    mid = r1.astype(BF16)
    lo = (r1 - mid.astype(F32)).astype(BF16)
    return hi, mid, lo


def _dy_att(dxb, w_out, gates, o, tm=512):
    S, D = dxb.shape
    E = ATT_W

    def body(dx_ref, w_ref, a_ref, o_ref, do_ref, da_ref, dl_ref):
        dya = _nt(dx_ref[...], w_ref[...])
        a = a_ref[...]
        ov = o_ref[...]
        sl, dsl = _silu_and_grad(a)
        d_o = dya * sl
        do_ref[...] = d_o.astype(BF16)
        da_ref[...] = (dya * ov * dsl).astype(BF16)
        ci = lax.broadcasted_iota(jnp.int32, (E, LANES), 0) // HEAD_DIM
        hi = lax.broadcasted_iota(jnp.int32, (E, LANES), 1)
        sel = jnp.where(ci == hi, 1.0, 0.0).astype(BF16)
        acc = jnp.zeros((tm, LANES), F32)
        for part in _split3(d_o * ov):
            acc = acc + jnp.dot(part, sel, preferred_element_type=F32)
        dl_ref[...] = acc

    row = lambda w: pl.BlockSpec((tm, w), lambda i: (i, 0))
    return pl.pallas_call(
        body, name="dy_att", grid=(S // tm,),
        in_specs=[row(D), pl.BlockSpec((E, D), lambda i: (0, 0)), row(E), row(E)],
        out_specs=(row(E), row(E), row(LANES)),
        out_shape=(jax.ShapeDtypeStruct((S, E), BF16), jax.ShapeDtypeStruct((S, E), BF16),
                   jax.ShapeDtypeStruct((S, LANES), F32)),
        compiler_params=_params(("parallel",)),
    )(dxb, w_out, gates, o)


def _dy_conv(dxb, w_out, gates, conv_out, ln_g, ln_b, tm=512):
    S, D = dxb.shape
    C = conv_out.shape[1]

    def body(dx_ref, w_ref, gate_ref, cv_ref, g_ref, beta_ref, dgate_ref, dconv_ref, gg_ref, gb_ref, gcb_ref):
        @pl.when(pl.program_id(0) == 0)
        def _():
            gg_ref[...] = jnp.zeros_like(gg_ref)
            gb_ref[...] = jnp.zeros_like(gb_ref)
            gcb_ref[...] = jnp.zeros_like(gcb_ref)

        dyc = _nt(dx_ref[...], w_ref[...])
        cv = cv_ref[...]
        mu = jnp.mean(cv, axis=-1, keepdims=True)
        xc = cv - mu
        rstd = lax.rsqrt(jnp.mean(xc * xc, axis=-1, keepdims=True) + LN_EPS)
        nrm = xc * rstd
        gv = g_ref[...]
        ln = nrm * gv + beta_ref[...]
        u, du = _silu_and_grad(ln)
        gt = gate_ref[...]
        g2, dg2 = _silu_and_grad(gt)
        dgate_ref[...] = (dyc * u * dg2).astype(BF16)
        d_ln = dyc * g2 * du
        gb_ref[...] += jnp.sum(d_ln, axis=0, keepdims=True)
        gg_ref[...] += jnp.sum(d_ln * nrm, axis=0, keepdims=True)
        dn = d_ln * gv
        d_conv = rstd * (dn - jnp.mean(dn, axis=-1, keepdims=True)
                         - nrm * jnp.mean(dn * nrm, axis=-1, keepdims=True))
        dconv_ref[...] = d_conv
        gcb_ref[...] += jnp.sum(d_conv, axis=0, keepdims=True)

    row = lambda w: pl.BlockSpec((tm, w), lambda i: (i, 0))
    vec = pl.BlockSpec((1, C), lambda i: (0, 0))
    return pl.pallas_call(
        body, name="dy_conv", grid=(S // tm,),
        in_specs=[row(D), pl.BlockSpec((C, D), lambda i: (1, 0)),
                  pl.BlockSpec((tm, C), lambda i: (i, 3)), row(C), vec, vec],
        out_specs=(row(C), row(C), vec, vec, vec),
        out_shape=(jax.ShapeDtypeStruct((S, C), BF16), jax.ShapeDtypeStruct((S, C), F32),
                   jax.ShapeDtypeStruct((1, C), F32), jax.ShapeDtypeStruct((1, C), F32),
                   jax.ShapeDtypeStruct((1, C), F32)),
        compiler_params=_params(("arbitrary",)),
    )(dxb, w_out, gates, conv_out, ln_g, ln_b)


def _conv_bwd(d_conv, gates, d_a_gate, d_c_gate, conv_w, tt=128):
    S, C = d_conv.shape
    hb = tt // CONV_HALO
    nt = S // tt

    def body(dc_ref, dnext_ref, val_ref, glu_ref, hval_ref, hglu_ref, da_ref, dg_ref, w_ref,
             out_ref, gw_ref, hbuf, dbuf, dhbuf):
        i = pl.program_id(0)

        @pl.when(i == 0)
        def _():
            gw_ref[...] = jnp.zeros_like(gw_ref)

        val = val_ref[...]
        sg = _sigmoid(glu_ref[...])
        halo = hval_ref[...] * _sigmoid(hglu_ref[...])
        hbuf[0:CONV_HALO, :] = jnp.where(i > 0, halo, 0.0)
        hbuf[CONV_HALO:, :] = val * sg
        dbuf[0:tt, :] = dc_ref[...]
        dbuf[tt:, :] = jnp.where(i < nt - 1, dnext_ref[...], 0.0)
        for cb in range(C // LANES):
            cols = slice(cb * LANES, (cb + 1) * LANES)
            gacc = [jnp.zeros((8, LANES), F32) for _ in range(CONV_K)]
            for rc in range(tt // 8):
                dcur = dbuf[rc * 8:(rc + 1) * 8, cols]
                acc = jnp.zeros((8, LANES), F32)
                for j in range(CONV_K):
                    ds_ = rc * 8 + (CONV_K - 1) - j
                    acc = acc + dbuf[ds_:ds_ + 8, cols] * w_ref[j:j + 1, cols]
                    hs = rc * 8 + CONV_HALO - (CONV_K - 1) + j
                    gacc[j] = gacc[j] + dcur * hbuf[hs:hs + 8, cols]
                dhbuf[rc * 8:(rc + 1) * 8, cols] = acc
            for j in range(CONV_K):
                gw_ref[j:j + 1, cols] += jnp.sum(gacc[j], axis=0, keepdims=True)
        d_h = dhbuf[...]
        out_ref[:, 0:C] = da_ref[...]
        out_ref[:, C:2 * C] = (d_h * sg).astype(BF16)
        out_ref[:, 2 * C:3 * C] = (d_h * val * sg * (1.0 - sg)).astype(BF16)
        out_ref[:, 3 * C:4 * C] = dg_ref[...]

    tile = lambda col: pl.BlockSpec((tt, C), lambda i: (i, col))
    return pl.pallas_call(
        body, name="conv_bwd", grid=(nt,),
        in_specs=[tile(0),
                  pl.BlockSpec((CONV_HALO, C), lambda i: (jnp.minimum((i + 1) * hb, S // CONV_HALO - 1), 0)),
                  tile(1), tile(2),
                  pl.BlockSpec((CONV_HALO, C), lambda i: (jnp.maximum(i * hb - 1, 0), 1)),
                  pl.BlockSpec((CONV_HALO, C), lambda i: (jnp.maximum(i * hb - 1, 0), 2)),
                  tile(0), tile(0),
                  pl.BlockSpec((CONV_HALO, C), lambda i: (0, 0))],
        out_specs=(pl.BlockSpec((tt, 4 * C), lambda i: (i, 0)),
                   pl.BlockSpec((CONV_HALO, C), lambda i: (0, 0))),
        out_shape=(jax.ShapeDtypeStruct((S, 4 * C), BF16), jax.ShapeDtypeStruct((CONV_HALO, C), F32)),
        scratch_shapes=[pltpu.VMEM((tt + CONV_HALO, C), F32), pltpu.VMEM((tt + CONV_HALO, C), F32),
                        pltpu.VMEM((tt, C), F32)],
        compiler_params=_params(("arbitrary",)),
    )(d_conv, d_conv, gates, gates, gates, gates, d_a_gate, d_c_gate, conv_w)


def _attn_bwd(q, kv, d_o, lse, delta, dil, prev, final, name):
    S = q.shape[0]
    L = S // dil
    nb = L // BLK
    out_dt = BF16 if final else F32
    have_prev = prev is not None
    view = lambda a, w: a.reshape(L, dil * w)

    def body(*refs):
        q_ref, do_ref, lse_ref, dl_ref, kvc_ref, kvp_ref = refs[:6]
        if have_prev:
            pdq_ref, pdkv_ref, dq_ref, dkv_ref, carry = refs[6:]
        else:
            dq_ref, dkv_ref, carry = refs[6:]
        n = pl.program_id(1)

        @pl.when(n == 0)
        def _():
            carry[...] = jnp.zeros_like(carry)

        @pl.when(n < nb)
        def _():
            valid, negd = _window_mask(n, dil)
            kv2 = jnp.concatenate([kvp_ref[...], kvc_ref[...]], axis=0)
            lo_mask = lax.broadcasted_iota(jnp.int32, (2 * BLK, LANES), 1) < HEAD_DIM
            halves = [jnp.zeros((2 * BLK, LANES), F32) for _ in range(4)]
            for hk in range(N_KV_HEADS):
                k_lo, k_hi, v_lo, v_hi = _head_operands(kv2, hk, lo_mask)
                dk_sum = jnp.zeros((2 * BLK, LANES), F32)
                dv_sum = jnp.zeros((2 * BLK, LANES), F32)
                for jp in range(2):
                    blk = hk * 2 + jp
                    cols = slice(blk * LANES, (blk + 1) * LANES)
                    qp = q_ref[:, cols]
                    dop = do_ref[:, cols]
                    dq_blk = jnp.zeros((BLK, LANES), F32)
                    dks, dvs = [], []
                    for which, (kk, vv) in enumerate(((k_lo, v_lo), (k_hi, v_hi))):
                        h = 2 * blk + which
                        s = _nt(qp, kk) + SLOPES[h] * negd
                        s = jnp.where(valid, s, NEG)
                        p = jnp.exp(s - lse_ref[:, h:h + 1])
                        dp = _nt(dop, vv)
                        ds = (p * (dp - dl_ref[:, h:h + 1])).astype(BF16)
                        dq_blk = dq_blk + jnp.dot(ds, kk, preferred_element_type=F32)
                        dks.append(_tn(ds, qp))
                        dvs.append(_tn(p.astype(BF16), dop))
                    dk_sum = dk_sum + jnp.where(lo_mask, dks[0], dks[1])
                    dv_sum = dv_sum + jnp.where(lo_mask, dvs[0], dvs[1])
                    if have_prev:
                        dq_blk = dq_blk + pdq_ref[:, cols]
                    if final:
                        dq_blk = dq_blk * (HEAD_DIM ** -0.5)
                    dq_ref[:, cols] = dq_blk.astype(out_dt)
                half, pos = hk // 2, hk % 2
                here = lo_mask if pos == 0 else jnp.logical_not(lo_mask)
                dk_tot = dk_sum + pltpu.roll(dk_sum, HEAD_DIM, axis=1)
                dv_tot = dv_sum + pltpu.roll(dv_sum, HEAD_DIM, axis=1)
                halves[half] = halves[half] + jnp.where(here, dk_tot, 0.0)
                halves[2 + half] = halves[2 + half] + jnp.where(here, dv_tot, 0.0)
            for b in range(4):
                cols = slice(b * LANES, (b + 1) * LANES)
                done = carry[:, cols] + halves[b][0:BLK, :]
                if have_prev:
                    done = done + pdkv_ref[:, cols]
                dkv_ref[:, cols] = done.astype(out_dt)
                carry[:, cols] = halves[b][BLK:, :]

        @pl.when(n == nb)
        def _():
            done = carry[...]
            if have_prev:
                done = done + pdkv_ref[...]
            dkv_ref[...] = done.astype(out_dt)

    cur = lambda r, n: (jnp.minimum(n, nb - 1), r)
    behind = lambda r, n: (jnp.maximum(n - 1, 0), r)
    in_specs = [pl.BlockSpec((BLK, ATT_W), cur), pl.BlockSpec((BLK, ATT_W), cur),
                pl.BlockSpec((BLK, LANES), cur), pl.BlockSpec((BLK, LANES), cur),
                pl.BlockSpec((BLK, 2 * KV_W), cur), pl.BlockSpec((BLK, 2 * KV_W), behind)]
    args = [view(q, ATT_W), view(d_o, ATT_W), view(lse, LANES), view(delta, LANES),
            view(kv, 2 * KV_W), view(kv, 2 * KV_W)]
    if have_prev:
        in_specs += [pl.BlockSpec((BLK, ATT_W), cur), pl.BlockSpec((BLK, 2 * KV_W), behind)]
        args += [view(prev[0], ATT_W), view(prev[1], 2 * KV_W)]
    dq, dkv = pl.pallas_call(
        body, name=name, grid=(dil, nb + 1),
        in_specs=in_specs,
        out_specs=(pl.BlockSpec((BLK, ATT_W), cur), pl.BlockSpec((BLK, 2 * KV_W), behind)),
        out_shape=(jax.ShapeDtypeStruct((L, dil * ATT_W), out_dt),
                   jax.ShapeDtypeStruct((L, dil * 2 * KV_W), out_dt)),
        scratch_shapes=[pltpu.VMEM((BLK, 2 * KV_W), F32)],
        compiler_params=_params(("parallel", "arbitrary")),
    )(*args)
    return dq.reshape(S, ATT_W), dkv.reshape(S, 2 * KV_W)


def _dh(dq, dkv, dgates, w_in, x, dx2, g, tm=512, tk=512):
    S, D = x.shape
    nq = dq.shape[1] // tk
    nkv = dkv.shape[1] // tk
    ng = dgates.shape[1] // tk
    nk = nq + nkv + ng

    def body(dq_ref, dkv_ref, dg_ref, w_ref, x_ref, dx2_ref, g_ref, gx_ref, gng_ref, acc):
        i, k = pl.program_id(0), pl.program_id(1)

        @pl.when((i == 0) & (k == 0))
        def _():
            gng_ref[...] = jnp.zeros_like(gng_ref)

        @pl.when(k == 0)
        def _():
            acc[...] = jnp.zeros_like(acc)

        @pl.when(k < nq)
        def _():
            acc[...] += _nt(dq_ref[...], w_ref[...])

        @pl.when((k >= nq) & (k < nq + nkv))
        def _():
            acc[...] += _nt(dkv_ref[...], w_ref[...])

        @pl.when(k >= nq + nkv)
        def _():
            acc[...] += _nt(dg_ref[...], w_ref[...])

        @pl.when(k == nk - 1)
        def _():
            dh = acc[...]
            xf = x_ref[...]
            r = lax.rsqrt(jnp.mean(xf * xf, axis=-1, keepdims=True) + NORM_EPS)
            nrm = xf * r
            gng_ref[...] += jnp.sum(dh * nrm, axis=0, keepdims=True)
            dn = dh * g_ref[...]
            gx_ref[...] = dx2_ref[...] + r * (dn - nrm * jnp.mean(dn * nrm, axis=-1, keepdims=True))

    row = pl.BlockSpec((tm, D), lambda i, k: (i, 0))
    vec = pl.BlockSpec((1, D), lambda i, k: (0, 0))
    return pl.pallas_call(
        body, name="dh", grid=(S // tm, nk),
        in_specs=[pl.BlockSpec((tm, tk), lambda i, k: (i, jnp.minimum(k, nq - 1))),
                  pl.BlockSpec((tm, tk), lambda i, k: (i, jnp.clip(k - nq, 0, nkv - 1))),
                  pl.BlockSpec((tm, tk), lambda i, k: (i, jnp.clip(k - nq - nkv, 0, ng - 1))),
                  pl.BlockSpec((D, tk), lambda i, k: (0, k)), row, row, vec],
        out_specs=(row, vec),
        out_shape=(jax.ShapeDtypeStruct((S, D), F32), jax.ShapeDtypeStruct((1, D), F32)),
        scratch_shapes=[pltpu.VMEM((tm, D), F32)],
        compiler_params=_params(("arbitrary", "arbitrary")),
    )(dq, dkv, dgates, w_in, x, dx2, g)


def _tn_matmul(a, b, name, tm=512, tn=512):
    M, K = a.shape
    N = b.shape[1]

    def body(a_ref, b_ref, o_ref):
        @pl.when(pl.program_id(1) == 0)
        def _():
            o_ref[...] = jnp.zeros_like(o_ref)

        o_ref[...] += _tn(a_ref[...], b_ref[...])

    return pl.pallas_call(
        body, name=name, grid=(N // tn, M // tm),
        in_specs=[pl.BlockSpec((tm, K), lambda j, m: (m, 0)), pl.BlockSpec((tm, tn), lambda j, m: (m, j))],
        out_specs=pl.BlockSpec((K, tn), lambda j, m: (0, j)),
        out_shape=jax.ShapeDtypeStruct((K, N), F32),
        compiler_params=_params(("parallel", "arbitrary")),
    )(a, b)


def _adamw(parts, w, m, v, name, tr=None):
    R, C = w.shape
    tr = R if tr is None else tr

    def body(p_ref, w_ref, m_ref, v_ref, g_out, d_out, m_out, v_out):
        g = p_ref[0].astype(F32)
        for dev in range(1, N_DEV):
            g = g + p_ref[dev].astype(F32)
        mn = ADAM_B1 * m_ref[...] + (1.0 - ADAM_B1) * g
        vn = ADAM_B2 * v_ref[...] + (1.0 - ADAM_B2) * (g * g)
        m_hat = mn / (1.0 - ADAM_B1 ** ADAM_STEP)
        v_hat = vn / (1.0 - ADAM_B2 ** ADAM_STEP)
        g_out[...] = g
        d_out[...] = -ADAM_LR * (m_hat / (jnp.sqrt(v_hat) + ADAM_EPS) + ADAM_WD * w_ref[...])
        m_out[...] = mn
        v_out[...] = vn

    blk = pl.BlockSpec((tr, C), lambda i: (i, 0))
    shp = jax.ShapeDtypeStruct((R, C), F32)
    return pl.pallas_call(
        body, name=name, grid=(R // tr,),
        in_specs=[pl.BlockSpec((N_DEV, tr, C), lambda i: (0, i, 0)), blk, blk, blk],
        out_specs=(blk, blk, blk, blk), out_shape=(shp, shp, shp, shp),
        compiler_params=_params(("parallel",)),
    )(parts, w, m, v)


def _local_step(x, target, norm_g, w_in, conv_w, conv_b, ln_g, ln_b, w_out, gf):
    n_qkv_blocks = (ATT_W + 2 * KV_W) // 512
    q, h = _inproj(x, norm_g, w_in, 0, ATT_W, BF16, HEAD_DIM ** -0.5, True, "inproj_q")
    kv = _inproj(x, norm_g, w_in, ATT_W // 512, 2 * KV_W, BF16, 1.0, False, "inproj_kv")
    gates = _inproj(x, norm_g, w_in, n_qkv_blocks, w_in.shape[1] - n_qkv_blocks * 512, F32, 1.0, False,
                    "inproj_gates")

    pats = [_attn_fwd(q, kv, dil, "attn_fwd_d%d" % dil) for _, dil in PATTERNS]
    o, lse, y_att = _attn_combine([p[0] for p in pats], [p[1] for p in pats], gates)
    conv_out, y_conv = _conv_fwd(gates, conv_w, conv_b, ln_g, ln_b)
    dx2, dxb, loss_cols, g_gf = _outproj_loss(x, y_att, y_conv, w_out, gf, target)

    d_o, d_a_gate, delta = _dy_att(dxb, w_out, gates, o)
    d_c_gate, d_conv, g_ln_g, g_ln_b, g_conv_b = _dy_conv(dxb, w_out, gates, conv_out, ln_g, ln_b)
    dgates, g_conv_w = _conv_bwd(d_conv, gates, d_a_gate, d_c_gate, conv_w)
    acc = None
    for idx, (_, dil) in enumerate(PATTERNS):
        acc = _attn_bwd(q, kv, d_o, lse, delta, dil, acc, idx == len(PATTERNS) - 1, "attn_bwd_d%d" % dil)
    dq, dkv = acc
    grad_x, g_norm_g = _dh(dq, dkv, dgates, w_in, x, dx2, norm_g)
    g_w_in = jnp.concatenate([_tn_matmul(h, dq, "gw_in_q"), _tn_matmul(h, dkv, "gw_in_kv"),
                              _tn_matmul(h, dgates, "gw_in_gates")], axis=1)
    g_w_out = jnp.concatenate([_tn_matmul(y_att, dxb, "gw_out_att"), _tn_matmul(y_conv, dxb, "gw_out_conv")],
                              axis=0)
    small = (g_norm_g, g_conv_b, g_ln_g, g_ln_b, g_gf, loss_cols)
    return grad_x, g_w_in, g_w_out, g_conv_w, small


def kernel(x, norm_g, w_in, conv_w, conv_b, conv_ln_g, conv_ln_b, w_out, final_norm_g, loss_target, m_norm_g, m_w_in, m_conv_w, m_conv_b, m_conv_ln_g, m_conv_ln_b, m_w_out, m_final_norm_g, v_norm_g, v_w_in, v_conv_w, v_conv_b, v_conv_ln_g, v_conv_ln_b, v_w_out, v_final_norm_g):
    S, D = x.shape[1], x.shape[2]
    win_sh, wout_sh, cw_sh = w_in[0], w_out[0], conv_w[0]
    cols_sh, rows_sh, ch_sh = win_sh.shape[1], wout_sh.shape[0], cw_sh.shape[1]

    win_all, wout_all, cw_all = _exchange(
        [win_sh.astype(BF16), wout_sh.astype(BF16), cw_sh], [False, False, False], "gather_weights")
    w_in_full = win_all.transpose(1, 0, 2).reshape(D, N_DEV * cols_sh)
    w_out_full = wout_all.reshape(N_DEV * rows_sh, D)
    conv_w_full = cw_all.transpose(1, 0, 2).reshape(CONV_K, N_DEV * ch_sh)
    conv_w_full = jnp.pad(conv_w_full, ((0, CONV_HALO - CONV_K), (0, 0)))
    gf = final_norm_g.reshape(1, D)

    grad_x, g_w_in, g_w_out, g_conv_w, small = _local_step(
        x[0], loss_target[0], norm_g, w_in_full, conv_w_full, conv_b, conv_ln_g, conv_ln_b, w_out_full, gf)

    win_pieces = g_w_in.reshape(D, N_DEV, cols_sh).transpose(1, 0, 2).astype(BF16)
    wout_pieces = g_w_out.reshape(N_DEV, rows_sh, D).astype(BF16)
    cw_pieces = g_conv_w[:CONV_K].reshape(CONV_K, N_DEV, ch_sh).transpose(1, 0, 2)
    small_pack = jnp.concatenate(list(small) + [jnp.zeros((2, D), F32)], axis=0)
    win_parts, wout_parts, cw_parts, small_parts = _exchange(
        [win_pieces, wout_pieces, cw_pieces, small_pack], [True, True, True, False], "reduce_grads")

    upd_win = _adamw(win_parts, win_sh, m_w_in[0], v_w_in[0], "adamw_w_in", tr=256)
    upd_wout = _adamw(wout_parts, wout_sh, m_w_out[0], v_w_out[0], "adamw_w_out", tr=128)
    upd_cw = _adamw(cw_parts, cw_sh, m_conv_w[0], v_conv_w[0], "adamw_conv_w")
    zeros3 = jnp.zeros((3, D), F32)
    stack = lambda a, b, c, d_, e: jnp.concatenate([a, b, c, d_, e.reshape(1, D), zeros3], axis=0)
    upd_small = _adamw(
        small_parts,
        stack(norm_g, conv_b, conv_ln_g, conv_ln_b, final_norm_g),
        stack(m_norm_g, m_conv_b, m_conv_ln_g, m_conv_ln_b, m_final_norm_g),
        stack(v_norm_g, v_conv_b, v_conv_ln_g, v_conv_ln_b, v_final_norm_g) + jnp.concatenate(
            [jnp.zeros((5, D), F32), jnp.ones((3, D), F32)], axis=0),
        "adamw_small")

    loss = 0.5 / D * jnp.sum(upd_small[0][5])

    def outputs(kind):
        sm = upd_small[kind]
        return [sm[0:1], upd_win[kind][None], upd_cw[kind][None], sm[1:2], sm[2:3], sm[3:4],
                upd_wout[kind][None], sm[4]]

    return (loss, grad_x[None], *outputs(0), *outputs(1), *outputs(2), *outputs(3))
```

```python
import functools

import jax
import jax.numpy as jnp
from jax import lax
from jax.experimental import pallas as pl
from jax.experimental.pallas import tpu as pltpu

F32 = jnp.float32
BF16 = jnp.bfloat16

HEAD_DIM = 64
N_KV_HEADS = 4
N_Q_HEADS = 16
ATT_W = 1024
KV_W = 256
CONV_K = 31
CONV_HALO = 32
PATTERNS = ((128, 1), (512, 4), (2048, 16))
BLK = 128
LANES = 128
NORM_EPS = 1e-6
LN_EPS = 1e-5
NEG = -1e30
N_DEV = 8
ADAM_LR, ADAM_B1, ADAM_B2, ADAM_EPS, ADAM_WD, ADAM_STEP = 0.001, 0.9, 0.999, 1e-08, 0.01, 10
VMEM_LIMIT = 48 * 1024 * 1024
SLOPES = tuple(2.0 ** (-8.0 * (h + 1) / N_Q_HEADS) for h in range(N_Q_HEADS))
MESH = pl.DeviceIdType.MESH


def _params(sem):
    return pltpu.CompilerParams(dimension_semantics=sem, vmem_limit_bytes=VMEM_LIMIT)


def _sigmoid(v):
    return 1.0 / (1.0 + jnp.exp(-v))


def _silu_and_grad(v):
    s = _sigmoid(v)
    return v * s, s * (1.0 + v * (1.0 - s))


ANY_SPEC = pl.BlockSpec(memory_space=pl.ANY)


def _mesh_pos():
    x, y, c = lax.axis_index("x"), lax.axis_index("y"), lax.axis_index("c")
    return x, y, c, 4 * x + 2 * y + c


def _flipped(k, x, y, c):
    px = 1 - x if k & 4 else x
    py = 1 - y if k & 2 else y
    pc = 1 - c if k & 1 else c
    return (px, py, pc), 4 * px + 2 * py + pc


class _Exchange:
    def __init__(self, arrays, dests):
        self.arrays, self.dests, self.n = list(arrays), list(dests), len(arrays)

    def out_shapes(self):
        return [jax.ShapeDtypeStruct((N_DEV,) + a.shape[-2:], a.dtype) for a in self.arrays]

    def sem_shapes(self):
        return [pltpu.SemaphoreType.DMA((self.n, N_DEV - 1)), pltpu.SemaphoreType.DMA((self.n, N_DEV - 1)),
                pltpu.SemaphoreType.DMA((self.n,))]

    def _when(self, a, dev, fn):
        if self.dests[a] is None:
            fn()
        else:
            lo, hi = self.dests[a]
            pl.when((dev >= lo) & (dev < hi))(fn)

    def _mine(self, ins, a, dev):
        return ins[a] if self.dests[a] is None else ins[a].at[dev - self.dests[a][0]]

    def _copy(self, ins, outs, sems, a, k, src_dev, slot, target):
        return pltpu.make_async_remote_copy(
            src_ref=self._mine(ins, a, src_dev), dst_ref=outs[a].at[slot],
            send_sem=sems[0].at[a, k - 1], recv_sem=sems[1].at[a, k - 1],
            device_id=target, device_id_type=MESH)

    def start(self, ins, outs, sems):
        x, y, c, me = _mesh_pos()
        for a in range(self.n):
            self._when(a, me, lambda a=a: pltpu.make_async_copy(
                self._mine(ins, a, me), outs[a].at[me], sems[2].at[a]).start())
            for k in range(1, N_DEV):
                target, peer = _flipped(k, x, y, c)
                self._when(a, peer, lambda a=a, k=k, target=target, peer=peer: self._copy(
                    ins, outs, sems, a, k, peer, me, target).start())

    def finish(self, ins, outs, sems):
        x, y, c, me = _mesh_pos()
        lo0 = [0 if d is None else d[0] for d in self.dests]
        for a in range(self.n):
            for k in range(1, N_DEV):
                target, peer = _flipped(k, x, y, c)
                self._when(a, me, lambda a=a, k=k, peer=peer: self._copy(
                    ins, outs, sems, a, k, lo0[a], peer, (x, y, c)).wait_recv())
            for k in range(1, N_DEV):
                target, peer = _flipped(k, x, y, c)
                self._when(a, peer, lambda a=a, k=k, target=target, peer=peer: self._copy(
                    ins, outs, sems, a, k, peer, me, target).wait_send())
            self._when(a, me, lambda a=a: pltpu.make_async_copy(
                self._mine(ins, a, me), outs[a].at[me], sems[2].at[a]).wait())


def _exchange(arrays, dests, name):
    ex = _Exchange(arrays, dests)
    na = ex.n

    def body(*refs):
        ins, outs, sems = refs[:na], refs[na:2 * na], refs[2 * na:]
        ex.start(ins, outs, sems)
        ex.finish(ins, outs, sems)

    return pl.pallas_call(
        body, name=name, out_shape=tuple(ex.out_shapes()),
        in_specs=[ANY_SPEC] * na, out_specs=tuple([ANY_SPEC] * na), scratch_shapes=ex.sem_shapes(),
    )(*arrays)


def _gather_two_level(arrays, name):
    na = len(arrays)

    def body(*refs):
        ins, outs = refs[:na], refs[na:2 * na]
        send_sems, recv_sems, loc_sems = refs[2 * na:]
        x, y, c, me = _mesh_pos()
        sibling = (x, y, 1 - c)
        chips = [(1 - x, y), (x, 1 - y), (1 - x, 1 - y)]

        def slot(px, py, pc):
            return 4 * px + 2 * py + pc

        def copy(a, k, src, block, to):
            return pltpu.make_async_remote_copy(
                src_ref=src, dst_ref=outs[a].at[slot(*block)], send_sem=send_sems.at[a, k], recv_sem=recv_sems.at[a, k],
                device_id=to, device_id_type=MESH)

        local = [pltpu.make_async_copy(ins[a], outs[a].at[me], loc_sems.at[a]) for a in range(na)]
        for cp in local:
            cp.start()
        started = []
        for a in range(na):
            started.append(copy(a, 0, ins[a], (x, y, c), sibling))
            started += [copy(a, 1 + j, ins[a], (x, y, c), (*chip, c)) for j, chip in enumerate(chips)]
        for cp in started:
            cp.start()
        for j, chip in enumerate(chips):
            for a in range(na):
                copy(a, 1 + j, ins[a], (*chip, c), (x, y, c)).wait_recv()
                fwd = copy(a, 4 + j, outs[a].at[slot(*chip, c)], (*chip, c), sibling)
                fwd.start()
                started.append(fwd)
        for a in range(na):
            copy(a, 0, ins[a], sibling, (x, y, c)).wait_recv()
            for j, chip in enumerate(chips):
                copy(a, 4 + j, ins[a], (*chip, 1 - c), (x, y, c)).wait_recv()
        for cp in started:
            cp.wait_send()
        for cp in local:
            cp.wait()

    return pl.pallas_call(
        body, name=name,
        out_shape=tuple(jax.ShapeDtypeStruct((N_DEV,) + a.shape, a.dtype) for a in arrays),
        in_specs=[ANY_SPEC] * na, out_specs=tuple([ANY_SPEC] * na),
        scratch_shapes=[pltpu.SemaphoreType.DMA((na, N_DEV - 1)), pltpu.SemaphoreType.DMA((na, N_DEV - 1)),
                        pltpu.SemaphoreType.DMA((na,))],
    )(*arrays)


def _inproj(x, g, w, col_block, ncols, out_dtype, scale, emit_h, name, tm=512, tn=512):
    S, D = x.shape

    def body(x_ref, g_ref, w_ref, *rest):
        if emit_h:
            o_ref, h_out, h_scr = rest
        else:
            o_ref, h_scr = rest

        @pl.when(pl.program_id(1) == 0)
        def _():
            xf = x_ref[...]
            r = lax.rsqrt(jnp.mean(xf * xf, axis=-1, keepdims=True) + NORM_EPS)
            h = (xf * r * g_ref[...]).astype(BF16)
            h_scr[...] = h
            if emit_h:
                h_out[...] = h

        acc = jnp.dot(h_scr[...], w_ref[...], preferred_element_type=F32)
        if scale != 1.0:
            acc = acc * scale
        o_ref[...] = acc.astype(out_dtype)

    out_shape = [jax.ShapeDtypeStruct((S, ncols), out_dtype)]
    out_specs = [pl.BlockSpec((tm, tn), lambda i, j: (i, j))]
    if emit_h:
        out_shape.append(jax.ShapeDtypeStruct((S, D), BF16))
        out_specs.append(pl.BlockSpec((tm, D), lambda i, j: (i, 0)))
    res = pl.pallas_call(
        body, name=name, grid=(S // tm, ncols // tn),
        in_specs=[pl.BlockSpec((tm, D), lambda i, j: (i, 0)),
                  pl.BlockSpec((1, D), lambda i, j: (0, 0)),
                  pl.BlockSpec((D, tn), lambda i, j: (0, col_block + j))],
        out_specs=tuple(out_specs), out_shape=tuple(out_shape),
        scratch_shapes=[pltpu.VMEM((tm, D), BF16)],
        compiler_params=_params(("parallel", "arbitrary")),
    )(x, g, w)
    return res if emit_h else res[0]


def _window_mask(n, dil):
    qi = lax.broadcasted_iota(jnp.int32, (BLK, 2 * BLK), 0)
    kj = lax.broadcasted_iota(jnp.int32, (BLK, 2 * BLK), 1)
    dist = BLK + qi - kj
    valid = (dist >= 0) & (dist <= BLK) & ((kj >= BLK) | (n > 0))
    negd = (dist * (-dil)).astype(F32)
    return valid, negd


def _head_operands(kv2, hk, lo_mask):
    half, pos = hk // 2, hk % 2
    out = []
    for base in (0, KV_W):
        t = kv2[:, base + half * LANES: base + (half + 1) * LANES].astype(F32)
        sw = pltpu.roll(t, HEAD_DIM, axis=1)
        at_lo, at_hi = (t, sw) if pos == 0 else (sw, t)
        out.append(jnp.where(lo_mask, at_lo, 0.0).astype(BF16))
        out.append(jnp.where(lo_mask, 0.0, at_hi).astype(BF16))
    return out


def _nt(a, b):
    return lax.dot_general(a, b, (((1,), (1,)), ((), ())), preferred_element_type=F32)


def _tn(a, b):
    return lax.dot_general(a, b, (((0,), (0,)), ((), ())), preferred_element_type=F32)


def _attn_fwd(q, kv, dil, name):
    S = q.shape[0]
    L = S // dil
    nb = L // BLK
    qv = q.reshape(L, dil * ATT_W)
    kvv = kv.reshape(L, dil * 2 * KV_W)

    def body(q_ref, kvc_ref, kvp_ref, o_ref, lse_ref):
        n = pl.program_id(1)
        valid, negd = _window_mask(n, dil)
        kv2 = jnp.concatenate([kvp_ref[...], kvc_ref[...]], axis=0)
        lo_mask = lax.broadcasted_iota(jnp.int32, (2 * BLK, LANES), 1) < HEAD_DIM
        lane = lax.broadcasted_iota(jnp.int32, (BLK, LANES), 1)
        stats = jnp.zeros((BLK, LANES), F32)
        for hk in range(N_KV_HEADS):
            k_lo, k_hi, v_lo, v_hi = _head_operands(kv2, hk, lo_mask)
            for jp in range(2):
                blk = hk * 2 + jp
                qp = q_ref[:, blk * LANES:(blk + 1) * LANES]
                o_pair = jnp.zeros((BLK, LANES), F32)
                for which, (kk, vv) in enumerate(((k_lo, v_lo), (k_hi, v_hi))):
                    h = 2 * blk + which
                    s = _nt(qp, kk) + SLOPES[h] * negd
                    s = jnp.where(valid, s, NEG)
                    m = jnp.max(s, axis=1, keepdims=True)
                    p = jnp.exp(s - m)
                    l = jnp.sum(p, axis=1, keepdims=True)
                    o_h = jnp.dot(p.astype(BF16), vv, preferred_element_type=F32)
                    o_pair = o_pair + o_h * (1.0 / l)
                    stats = jnp.where(lane == h, m + jnp.log(l), stats)
                o_ref[:, blk * LANES:(blk + 1) * LANES] = o_pair
        lse_ref[...] = stats

    o, lse = pl.pallas_call(
        body, name=name, grid=(dil, nb),
        in_specs=[pl.BlockSpec((BLK, ATT_W), lambda r, n: (n, r)),
                  pl.BlockSpec((BLK, 2 * KV_W), lambda r, n: (n, r)),
                  pl.BlockSpec((BLK, 2 * KV_W), lambda r, n: (jnp.maximum(n - 1, 0), r))],
        out_specs=(pl.BlockSpec((BLK, ATT_W), lambda r, n: (n, r)),
                   pl.BlockSpec((BLK, LANES), lambda r, n: (n, r))),
        out_shape=(jax.ShapeDtypeStruct((L, dil * ATT_W), F32),
                   jax.ShapeDtypeStruct((L, dil * LANES), F32)),
        compiler_params=_params(("parallel", "parallel")),
    )(qv, kvv, kvv)
    return o.reshape(S, ATT_W), lse.reshape(S, LANES)


def _attn_combine(os_, lses, gates, tm=256):
    S = os_[0].shape[0]
    npat = len(os_)

    def body(*refs):
        o_refs = refs[:npat]
        l_refs = refs[npat:2 * npat]
        gate_ref, o_out, lse_out, y_out = refs[2 * npat:]
        ls = [r[...] for r in l_refs]
        mx = functools.reduce(jnp.maximum, ls)
        es = [jnp.exp(l - mx) for l in ls]
        tot = functools.reduce(jnp.add, es)
        inv = 1.0 / tot
        ws = [e * inv for e in es]
        lse_out[...] = mx + jnp.log(tot)
        lo = lax.broadcasted_iota(jnp.int32, (tm, LANES), 1) < HEAD_DIM
        for blk in range(ATT_W // LANES):
            cols = slice(blk * LANES, (blk + 1) * LANES)
            acc = jnp.zeros((tm, LANES), F32)
            for p in range(npat):
                sc = jnp.where(lo, ws[p][:, 2 * blk:2 * blk + 1], ws[p][:, 2 * blk + 1:2 * blk + 2])
                acc = acc + sc * o_refs[p][:, cols]
            o_out[:, cols] = acc
            a = gate_ref[:, cols]
            y_out[:, cols] = (acc * (a * _sigmoid(a))).astype(BF16)

    row = lambda w: pl.BlockSpec((tm, w), lambda i: (i, 0))
    return pl.pallas_call(
        body, name="attn_combine", grid=(S // tm,),
        in_specs=[row(ATT_W)] * npat + [row(LANES)] * npat + [row(ATT_W)],
        out_specs=(row(ATT_W), row(LANES), row(ATT_W)),
        out_shape=(jax.ShapeDtypeStruct((S, ATT_W), F32), jax.ShapeDtypeStruct((S, LANES), F32),
                   jax.ShapeDtypeStruct((S, ATT_W), BF16)),
        compiler_params=_params(("parallel",)),
    )(*os_, *lses, gates)


def _shifted_copies(buf, phases):
    n = phases.shape[1]
    for b in range(1, 8):
        phases[b - 1] = buf[b:b + n, :]


def _window(buf, phases, start, cols):
    b = start % 8
    if b == 0:
        return buf[start:start + 8, cols]
    return phases[b - 1, start - b:start - b + 8, cols]


def _conv_fwd(gates, conv_w, conv_b, ln_g, ln_b, tt=128):
    S = gates.shape[0]
    C = conv_w.shape[1]
    hb = tt // CONV_HALO

    def body(val_ref, glu_ref, hval_ref, hglu_ref, gate_ref, w_ref, b_ref, g_ref, beta_ref,
             conv_ref, y_ref, hbuf, hph):
        i = pl.program_id(0)
        halo = hval_ref[...] * _sigmoid(hglu_ref[...])
        hbuf[0:CONV_HALO, :] = jnp.where(i > 0, halo, 0.0)
        hbuf[CONV_HALO:, :] = val_ref[...] * _sigmoid(glu_ref[...])
        _shifted_copies(hbuf, hph)
        for cb in range(C // LANES):
            cols = slice(cb * LANES, (cb + 1) * LANES)
            wj = [jnp.broadcast_to(w_ref[j:j + 1, cols], (8, LANES)) for j in range(CONV_K)]
            for rc in range(tt // 8):
                acc = jnp.zeros((8, LANES), F32)
                for j in range(CONV_K):
                    start = rc * 8 + CONV_HALO - (CONV_K - 1) + j
                    acc = acc + _window(hbuf, hph, start, cols) * wj[j]
                conv_ref[rc * 8:(rc + 1) * 8, cols] = acc
        cv = conv_ref[...] + b_ref[...]
        conv_ref[...] = cv
        mu = jnp.mean(cv, axis=-1, keepdims=True)
        xc = cv - mu
        var = jnp.mean(xc * xc, axis=-1, keepdims=True)
        ln = xc * lax.rsqrt(var + LN_EPS) * g_ref[...] + beta_ref[...]
        gt = gate_ref[...]
        y_ref[...] = (ln * _sigmoid(ln) * (gt * _sigmoid(gt))).astype(BF16)

    vec = pl.BlockSpec((1, C), lambda i: (0, 0))
    return pl.pallas_call(
        body, name="conv_fwd", grid=(S // tt,),
        in_specs=[pl.BlockSpec((tt, C), lambda i: (i, 1)),
                  pl.BlockSpec((tt, C), lambda i: (i, 2)),
                  pl.BlockSpec((CONV_HALO, C), lambda i: (jnp.maximum(i * hb - 1, 0), 1)),
                  pl.BlockSpec((CONV_HALO, C), lambda i: (jnp.maximum(i * hb - 1, 0), 2)),
                  pl.BlockSpec((tt, C), lambda i: (i, 3)),
                  pl.BlockSpec((CONV_HALO, C), lambda i: (0, 0)), vec, vec, vec],
        out_specs=(pl.BlockSpec((tt, C), lambda i: (i, 0)), pl.BlockSpec((tt, C), lambda i: (i, 0))),
        out_shape=(jax.ShapeDtypeStruct((S, C), F32), jax.ShapeDtypeStruct((S, C), BF16)),
        scratch_shapes=[pltpu.VMEM((tt + CONV_HALO, C), F32), pltpu.VMEM((7, tt + CONV_HALO - 8, C), F32)],
        compiler_params=_params(("parallel",)),
    )(gates, gates, gates, gates, gates, conv_w, conv_b, ln_g, ln_b)


def _outproj_loss(x, y_att, y_conv, w_out, gf, target, tm=512):
    S, D = x.shape
    E = y_att.shape[1]

    def body(x_ref, ya_ref, yc_ref, w_ref, gf_ref, t_ref, dx_ref, dxb_ref, loss_ref, ggf_ref):
        @pl.when(pl.program_id(0) == 0)
        def _():
            loss_ref[...] = jnp.zeros_like(loss_ref)
            ggf_ref[...] = jnp.zeros_like(ggf_ref)

        x2 = (x_ref[...] + jnp.dot(ya_ref[...], w_ref[0:E, :], preferred_element_type=F32)
              + jnp.dot(yc_ref[...], w_ref[E:, :], preferred_element_type=F32))
        r = lax.rsqrt(jnp.mean(x2 * x2, axis=-1, keepdims=True) + NORM_EPS)
        nrm = x2 * r
        gfv = gf_ref[...]
        err = nrm * gfv - t_ref[...]
        loss_ref[...] += jnp.sum(err * err, axis=0, keepdims=True)
        dout = err * (1.0 / D)
        ggf_ref[...] += jnp.sum(dout * nrm, axis=0, keepdims=True)
        dn = dout * gfv
        dx2 = r * (dn - nrm * jnp.mean(dn * nrm, axis=-1, keepdims=True))
        dx_ref[...] = dx2
        dxb_ref[...] = dx2.astype(BF16)

    row = lambda w: pl.BlockSpec((tm, w), lambda i: (i, 0))
    vec = pl.BlockSpec((1, D), lambda i: (0, 0))
    return pl.pallas_call(
        body, name="outproj_loss", grid=(S // tm,),
        in_specs=[row(D), row(E), row(E), pl.BlockSpec((2 * E, D), lambda i: (0, 0)), vec, row(D)],
        out_specs=(row(D), row(D), vec, vec),
        out_shape=(jax.ShapeDtypeStruct((S, D), F32), jax.ShapeDtypeStruct((S, D), BF16),
                   jax.ShapeDtypeStruct((1, D), F32), jax.ShapeDtypeStruct((1, D), F32)),
        compiler_params=_params(("arbitrary",)),
    )(x, y_att, y_conv, w_out, gf, target)


def _split3(v):
    hi = v.astype(BF16)
    r1 = v - hi.astype(F32)
    mid = r1.astype(BF16)
    lo = (r1 - mid.astype(F32)).astype(BF16)
    return hi, mid, lo


def _dy_att(dxb, w_out, gates, o, tm=512):
    S, D = dxb.shape
    E = ATT_W

    def body(dx_ref, w_ref, a_ref, o_ref, do_ref, da_ref, dl_ref):
        dya = _nt(dx_ref[...], w_ref[...])
        a = a_ref[...]
        ov = o_ref[...]
        sl, dsl = _silu_and_grad(a)
        d_o = dya * sl
        do_ref[...] = d_o.astype(BF16)
        da_ref[...] = (dya * ov * dsl).astype(BF16)
        ci = lax.broadcasted_iota(jnp.int32, (E, LANES), 0) // HEAD_DIM
        hi = lax.broadcasted_iota(jnp.int32, (E, LANES), 1)
        sel = jnp.where(ci == hi, 1.0, 0.0).astype(BF16)
        acc = jnp.zeros((tm, LANES), F32)
        for part in _split3(d_o * ov):
            acc = acc + jnp.dot(part, sel, preferred_element_type=F32)
        dl_ref[...] = acc

    row = lambda w: pl.BlockSpec((tm, w), lambda i: (i, 0))
    return pl.pallas_call(
        body, name="dy_att", grid=(S // tm,),
        in_specs=[row(D), pl.BlockSpec((E, D), lambda i: (0, 0)), row(E), row(E)],
        out_specs=(row(E), row(E), row(LANES)),
        out_shape=(jax.ShapeDtypeStruct((S, E), BF16), jax.ShapeDtypeStruct((S, E), BF16),
                   jax.ShapeDtypeStruct((S, LANES), F32)),
        compiler_params=_params(("parallel",)),
    )(dxb, w_out, gates, o)


def _dy_conv(dxb, w_out, gates, conv_out, ln_g, ln_b, tm=512):
    S, D = dxb.shape
    C = conv_out.shape[1]

    def body(dx_ref, w_ref, gate_ref, cv_ref, g_ref, beta_ref, dgate_ref, dconv_ref, gg_ref, gb_ref, gcb_ref):
        @pl.when(pl.program_id(0) == 0)
        def _():
            gg_ref[...] = jnp.zeros_like(gg_ref)
            gb_ref[...] = jnp.zeros_like(gb_ref)
            gcb_ref[...] = jnp.zeros_like(gcb_ref)

        dyc = _nt(dx_ref[...], w_ref[...])
        cv = cv_ref[...]
        mu = jnp.mean(cv, axis=-1, keepdims=True)
        xc = cv - mu
        rstd = lax.rsqrt(jnp.mean(xc * xc, axis=-1, keepdims=True) + LN_EPS)
        nrm = xc * rstd
        gv = g_ref[...]
        ln = nrm * gv + beta_ref[...]
        u, du = _silu_and_grad(ln)
        gt = gate_ref[...]
        g2, dg2 = _silu_and_grad(gt)
        dgate_ref[...] = (dyc * u * dg2).astype(BF16)
        d_ln = dyc * g2 * du
        gb_ref[...] += jnp.sum(d_ln, axis=0, keepdims=True)
        gg_ref[...] += jnp.sum(d_ln * nrm, axis=0, keepdims=True)
        dn = d_ln * gv
        d_conv = rstd * (dn - jnp.mean(dn, axis=-1, keepdims=True)
                         - nrm * jnp.mean(dn * nrm, axis=-1, keepdims=True))
        dconv_ref[...] = d_conv
        gcb_ref[...] += jnp.sum(d_conv, axis=0, keepdims=True)

    row = lambda w: pl.BlockSpec((tm, w), lambda i: (i, 0))
    vec = pl.BlockSpec((1, C), lambda i: (0, 0))
    return pl.pallas_call(
        body, name="dy_conv", grid=(S // tm,),
        in_specs=[row(D), pl.BlockSpec((C, D), lambda i: (1, 0)),
                  pl.BlockSpec((tm, C), lambda i: (i, 3)), row(C), vec, vec],
        out_specs=(row(C), row(C), vec, vec, vec),
        out_shape=(jax.ShapeDtypeStruct((S, C), BF16), jax.ShapeDtypeStruct((S, C), F32),
                   jax.ShapeDtypeStruct((1, C), F32), jax.ShapeDtypeStruct((1, C), F32),
                   jax.ShapeDtypeStruct((1, C), F32)),
        compiler_params=_params(("arbitrary",)),
    )(dxb, w_out, gates, conv_out, ln_g, ln_b)


def _conv_bwd(d_conv, gates, d_a_gate, d_c_gate, conv_w, tt=128):
    S, C = d_conv.shape
    hb = tt // CONV_HALO
    nt = S // tt

    def body(dc_ref, dnext_ref, val_ref, glu_ref, hval_ref, hglu_ref, da_ref, dg_ref, w_ref,
             out_ref, gw_ref, hbuf, dbuf, dhbuf, hph, dph):
        i = pl.program_id(0)

        @pl.when(i == 0)
        def _():
            gw_ref[...] = jnp.zeros_like(gw_ref)

        val = val_ref[...]
        sg = _sigmoid(glu_ref[...])
        halo = hval_ref[...] * _sigmoid(hglu_ref[...])
        hbuf[0:CONV_HALO, :] = jnp.where(i > 0, halo, 0.0)
        hbuf[CONV_HALO:, :] = val * sg
        dbuf[0:tt, :] = dc_ref[...]
        dbuf[tt:, :] = jnp.where(i < nt - 1, dnext_ref[...], 0.0)
        _shifted_copies(hbuf, hph)
        _shifted_copies(dbuf, dph)
        for cb in range(C // LANES):
            cols = slice(cb * LANES, (cb + 1) * LANES)
            wj = [jnp.broadcast_to(w_ref[j:j + 1, cols], (8, LANES)) for j in range(CONV_K)]
            for rc in range(tt // 8):
                acc = jnp.zeros((8, LANES), F32)
                for j in range(CONV_K):
                    acc = acc + _window(dbuf, dph, rc * 8 + (CONV_K - 1) - j, cols) * wj[j]
                dhbuf[rc * 8:(rc + 1) * 8, cols] = acc
            gacc = [jnp.zeros((8, LANES), F32) for _ in range(CONV_K)]
            for rc in range(tt // 8):
                dcur = dbuf[rc * 8:(rc + 1) * 8, cols]
                for j in range(CONV_K):
                    hs = rc * 8 + CONV_HALO - (CONV_K - 1) + j
                    gacc[j] = gacc[j] + dcur * _window(hbuf, hph, hs, cols)
            for j in range(CONV_K):
                gw_ref[j:j + 1, cols] += jnp.sum(gacc[j], axis=0, keepdims=True)
        d_h = dhbuf[...]
        out_ref[:, 0:C] = da_ref[...]
        out_ref[:, C:2 * C] = (d_h * sg).astype(BF16)
        out_ref[:, 2 * C:3 * C] = (d_h * val * sg * (1.0 - sg)).astype(BF16)
        out_ref[:, 3 * C:4 * C] = dg_ref[...]

    tile = lambda col: pl.BlockSpec((tt, C), lambda i: (i, col))
    return pl.pallas_call(
        body, name="conv_bwd", grid=(nt,),
        in_specs=[tile(0),
                  pl.BlockSpec((CONV_HALO, C), lambda i: (jnp.minimum((i + 1) * hb, S // CONV_HALO - 1), 0)),
                  tile(1), tile(2),
                  pl.BlockSpec((CONV_HALO, C), lambda i: (jnp.maximum(i * hb - 1, 0), 1)),
                  pl.BlockSpec((CONV_HALO, C), lambda i: (jnp.maximum(i * hb - 1, 0), 2)),
                  tile(0), tile(0),
                  pl.BlockSpec((CONV_HALO, C), lambda i: (0, 0))],
        out_specs=(pl.BlockSpec((tt, 4 * C), lambda i: (i, 0)),
                   pl.BlockSpec((CONV_HALO, C), lambda i: (0, 0))),
        out_shape=(jax.ShapeDtypeStruct((S, 4 * C), BF16), jax.ShapeDtypeStruct((CONV_HALO, C), F32)),
        scratch_shapes=[pltpu.VMEM((tt + CONV_HALO, C), F32), pltpu.VMEM((tt + CONV_HALO, C), F32),
                        pltpu.VMEM((tt, C), F32),
                        pltpu.VMEM((7, tt + CONV_HALO - 8, C), F32), pltpu.VMEM((7, tt + CONV_HALO - 8, C), F32)],
        compiler_params=_params(("arbitrary",)),
    )(d_conv, d_conv, gates, gates, gates, gates, d_a_gate, d_c_gate, conv_w)


def _attn_bwd(q, kv, d_o, lse, delta, dil, prev, final, name, hosted=None):
    S = q.shape[0]
    L = S // dil
    nb = L // BLK
    out_dt = BF16 if final else F32
    have_prev = prev is not None
    view = lambda a, w: a.reshape(L, dil * w)

    hn = hosted.n if hosted is not None else 0

    def body(*refs):
        refs = list(refs)
        q_ref, do_ref, lse_ref, dl_ref, kvc_ref, kvp_ref = refs[:6]
        del refs[:6]
        if have_prev:
            pdq_ref, pdkv_ref = refs[:2]
            del refs[:2]
        h_ins = refs[:hn]
        dq_ref, dkv_ref = refs[hn:hn + 2]
        h_outs = refs[hn + 2:2 * hn + 2]
        carry = refs[2 * hn + 2]
        h_sems = refs[2 * hn + 3:]
        n = pl.program_id(1)
        if hosted is not None:
            @pl.when((pl.program_id(0) == 0) & (n == 0))
            def _():
                hosted.start(h_ins, h_outs, h_sems)

        @pl.when(n == 0)
        def _():
            carry[...] = jnp.zeros_like(carry)

        @pl.when(n < nb)
        def _():
            valid, negd = _window_mask(n, dil)
            kv2 = jnp.concatenate([kvp_ref[...], kvc_ref[...]], axis=0)
            lo_mask = lax.broadcasted_iota(jnp.int32, (2 * BLK, LANES), 1) < HEAD_DIM
            halves = [jnp.zeros((2 * BLK, LANES), F32) for _ in range(4)]
            for hk in range(N_KV_HEADS):
                k_lo, k_hi, v_lo, v_hi = _head_operands(kv2, hk, lo_mask)
                dk_sum = jnp.zeros((2 * BLK, LANES), F32)
                dv_sum = jnp.zeros((2 * BLK, LANES), F32)
                for jp in range(2):
                    blk = hk * 2 + jp
                    cols = slice(blk * LANES, (blk + 1) * LANES)
                    qp = q_ref[:, cols]
                    dop = do_ref[:, cols]
                    dq_blk = jnp.zeros((BLK, LANES), F32)
                    dks, dvs = [], []
                    for which, (kk, vv) in enumerate(((k_lo, v_lo), (k_hi, v_hi))):
                        h = 2 * blk + which
                        s = _nt(qp, kk) + SLOPES[h] * negd
                        s = jnp.where(valid, s, NEG)
                        p = jnp.exp(s - lse_ref[:, h:h + 1])
                        dp = _nt(dop, vv)
                        ds = (p * (dp - dl_ref[:, h:h + 1])).astype(BF16)
                        dq_blk = dq_blk + jnp.dot(ds, kk, preferred_element_type=F32)
                        dks.append(_tn(ds, qp))
                        dvs.append(_tn(p.astype(BF16), dop))
                    dk_sum = dk_sum + jnp.where(lo_mask, dks[0], dks[1])
                    dv_sum = dv_sum + jnp.where(lo_mask, dvs[0], dvs[1])
                    if have_prev:
                        dq_blk = dq_blk + pdq_ref[:, cols]
                    if final:
                        dq_blk = dq_blk * (HEAD_DIM ** -0.5)
                    dq_ref[:, cols] = dq_blk.astype(out_dt)
                half, pos = hk // 2, hk % 2
                here = lo_mask if pos == 0 else jnp.logical_not(lo_mask)
                dk_tot = dk_sum + pltpu.roll(dk_sum, HEAD_DIM, axis=1)
                dv_tot = dv_sum + pltpu.roll(dv_sum, HEAD_DIM, axis=1)
                halves[half] = halves[half] + jnp.where(here, dk_tot, 0.0)
                halves[2 + half] = halves[2 + half] + jnp.where(here, dv_tot, 0.0)
            for b in range(4):
                cols = slice(b * LANES, (b + 1) * LANES)
                done = carry[:, cols] + halves[b][0:BLK, :]
                if have_prev:
                    done = done + pdkv_ref[:, cols]
                dkv_ref[:, cols] = done.astype(out_dt)
                carry[:, cols] = halves[b][BLK:, :]

        @pl.when(n == nb)
        def _():
            done = carry[...]
            if have_prev:
                done = done + pdkv_ref[...]
            dkv_ref[...] = done.astype(out_dt)

        if hosted is not None:
            @pl.when((pl.program_id(0) == dil - 1) & (n == nb))
            def _():
                hosted.finish(h_ins, h_outs, h_sems)

    cur = lambda r, n: (jnp.minimum(n, nb - 1), r)
    behind = lambda r, n: (jnp.maximum(n - 1, 0), r)
    in_specs = [pl.BlockSpec((BLK, ATT_W), cur), pl.BlockSpec((BLK, ATT_W), cur),
                pl.BlockSpec((BLK, LANES), cur), pl.BlockSpec((BLK, LANES), cur),
                pl.BlockSpec((BLK, 2 * KV_W), cur), pl.BlockSpec((BLK, 2 * KV_W), behind)]
    args = [view(q, ATT_W), view(d_o, ATT_W), view(lse, LANES), view(delta, LANES),
            view(kv, 2 * KV_W), view(kv, 2 * KV_W)]
    if have_prev:
        in_specs += [pl.BlockSpec((BLK, ATT_W), cur), pl.BlockSpec((BLK, 2 * KV_W), behind)]
        args += [view(prev[0], ATT_W), view(prev[1], 2 * KV_W)]
    out_specs = [pl.BlockSpec((BLK, ATT_W), cur), pl.BlockSpec((BLK, 2 * KV_W), behind)]
    out_shape = [jax.ShapeDtypeStruct((L, dil * ATT_W), out_dt), jax.ShapeDtypeStruct((L, dil * 2 * KV_W), out_dt)]
    scratch = [pltpu.VMEM((BLK, 2 * KV_W), F32)]
    if hosted is not None:
        in_specs += [ANY_SPEC] * hn
        args += hosted.arrays
        out_specs += [ANY_SPEC] * hn
        out_shape += hosted.out_shapes()
        scratch += hosted.sem_shapes()
    res = pl.pallas_call(
        body, name=name, grid=(dil, nb + 1),
        in_specs=in_specs, out_specs=tuple(out_specs), out_shape=tuple(out_shape), scratch_shapes=scratch,
        compiler_params=_params(("arbitrary", "arbitrary")),
    )(*args)
    return (res[0].reshape(S, ATT_W), res[1].reshape(S, 2 * KV_W)), list(res[2:])


def _dh(dq, dkv, dgates, w_in, x, dx2, g, hosted=None, tm=512, tk=512):
    S, D = x.shape
    nq = dq.shape[1] // tk
    nkv = dkv.shape[1] // tk
    ng = dgates.shape[1] // tk
    nk = nq + nkv + ng
    hn = hosted.n if hosted is not None else 0

    def body(*refs):
        dq_ref, dkv_ref, dg_ref, w_ref, x_ref, dx2_ref, g_ref = refs[:7]
        h_ins = refs[7:7 + hn]
        gx_ref, gng_ref = refs[7 + hn:9 + hn]
        h_outs = refs[9 + hn:9 + 2 * hn]
        acc = refs[9 + 2 * hn]
        h_sems = refs[10 + 2 * hn:]
        i, k = pl.program_id(0), pl.program_id(1)

        @pl.when((i == 0) & (k == 0))
        def _():
            gng_ref[...] = jnp.zeros_like(gng_ref)
            if hosted is not None:
                hosted.start(h_ins, h_outs, h_sems)

        @pl.when(k == 0)
        def _():
            acc[...] = jnp.zeros_like(acc)

        @pl.when(k < nq)
        def _():
            acc[...] += _nt(dq_ref[...], w_ref[...])

        @pl.when((k >= nq) & (k < nq + nkv))
        def _():
            acc[...] += _nt(dkv_ref[...], w_ref[...])

        @pl.when(k >= nq + nkv)
        def _():
            acc[...] += _nt(dg_ref[...], w_ref[...])

        @pl.when(k == nk - 1)
        def _():
            dh = acc[...]
            xf = x_ref[...]
            r = lax.rsqrt(jnp.mean(xf * xf, axis=-1, keepdims=True) + NORM_EPS)
            nrm = xf * r
            gng_ref[...] += jnp.sum(dh * nrm, axis=0, keepdims=True)
            dn = dh * g_ref[...]
            gx_ref[...] = dx2_ref[...] + r * (dn - nrm * jnp.mean(dn * nrm, axis=-1, keepdims=True))

        if hosted is not None:
            @pl.when((i == S // tm - 1) & (k == nk - 1))
            def _():
                hosted.finish(h_ins, h_outs, h_sems)

    row = pl.BlockSpec((tm, D), lambda i, k: (i, 0))
    vec = pl.BlockSpec((1, D), lambda i, k: (0, 0))
    in_specs = [pl.BlockSpec((tm, tk), lambda i, k: (i, jnp.minimum(k, nq - 1))),
                pl.BlockSpec((tm, tk), lambda i, k: (i, jnp.clip(k - nq, 0, nkv - 1))),
                pl.BlockSpec((tm, tk), lambda i, k: (i, jnp.clip(k - nq - nkv, 0, ng - 1))),
                pl.BlockSpec((D, tk), lambda i, k: (0, k)), row, row, vec]
    args = [dq, dkv, dgates, w_in, x, dx2, g]
    out_specs = [row, vec]
    out_shape = [jax.ShapeDtypeStruct((S, D), F32), jax.ShapeDtypeStruct((1, D), F32)]
    scratch = [pltpu.VMEM((tm, D), F32)]
    if hosted is not None:
        in_specs += [ANY_SPEC] * hn
        args += hosted.arrays
        out_specs += [ANY_SPEC] * hn
        out_shape += hosted.out_shapes()
        scratch += hosted.sem_shapes()
    res = pl.pallas_call(
        body, name="dh", grid=(S // tm, nk),
        in_specs=in_specs, out_specs=tuple(out_specs), out_shape=tuple(out_shape), scratch_shapes=scratch,
        compiler_params=_params(("arbitrary", "arbitrary")),
    )(*args)
    return res[0], res[1], list(res[2:])


def _tn_matmul(a, b, name, tm=512, tn=512):
    M, K = a.shape
    N = b.shape[1]

    def body(a_ref, b_ref, o_ref):
        @pl.when(pl.program_id(1) == 0)
        def _():
            o_ref[...] = jnp.zeros_like(o_ref)

        o_ref[...] += _tn(a_ref[...], b_ref[...])

    return pl.pallas_call(
        body, name=name, grid=(N // tn, M // tm),
        in_specs=[pl.BlockSpec((tm, K), lambda j, m: (m, 0)), pl.BlockSpec((tm, tn), lambda j, m: (m, j))],
        out_specs=pl.BlockSpec((K, tn), lambda j, m: (0, j)),
        out_shape=jax.ShapeDtypeStruct((K, N), F32),
        compiler_params=_params(("parallel", "arbitrary")),
    )(a, b)


def _adamw(parts, w, m, v, name, tr=None, split=None):
    R, C = w.shape
    tr = R if tr is None else tr
    parts = [parts] if split is None else list(parts)
    npar = len(parts)

    def total(p_ref):
        g = p_ref[0].astype(F32)
        for dev in range(1, N_DEV):
            g = g + p_ref[dev].astype(F32)
        return g

    def body(*refs):
        w_ref, m_ref, v_ref, g_out, d_out, m_out, v_out = refs[npar:]
        if split is None:
            g = total(refs[0])
        else:
            g = jnp.where(_mesh_pos()[3] < split, total(refs[0]), total(refs[1]))
        mn = ADAM_B1 * m_ref[...] + (1.0 - ADAM_B1) * g
        vn = ADAM_B2 * v_ref[...] + (1.0 - ADAM_B2) * (g * g)
        m_hat = mn / (1.0 - ADAM_B1 ** ADAM_STEP)
        v_hat = vn / (1.0 - ADAM_B2 ** ADAM_STEP)
        g_out[...] = g
        d_out[...] = -ADAM_LR * (m_hat / (jnp.sqrt(v_hat) + ADAM_EPS) + ADAM_WD * w_ref[...])
        m_out[...] = mn
        v_out[...] = vn

    blk = pl.BlockSpec((tr, C), lambda i: (i, 0))
    shp = jax.ShapeDtypeStruct((R, C), F32)
    return pl.pallas_call(
        body, name=name, grid=(R // tr,),
        in_specs=[pl.BlockSpec((N_DEV, tr, C), lambda i: (0, i, 0))] * npar + [blk, blk, blk],
        out_specs=(blk, blk, blk, blk), out_shape=(shp, shp, shp, shp),
        compiler_params=_params(("parallel",)),
    )(*parts, w, m, v)


def _local_step(x, target, norm_g, w_in, conv_w, conv_b, ln_g, ln_b, w_out, gf, early=None, late=None):
    n_qkv_blocks = (ATT_W + 2 * KV_W) // 512
    q, h = _inproj(x, norm_g, w_in, 0, ATT_W, BF16, HEAD_DIM ** -0.5, True, "inproj_q")
    kv = _inproj(x, norm_g, w_in, ATT_W // 512, 2 * KV_W, BF16, 1.0, False, "inproj_kv")
    gates = _inproj(x, norm_g, w_in, n_qkv_blocks, w_in.shape[1] - n_qkv_blocks * 512, F32, 1.0, False,
                    "inproj_gates")

    pats = [_attn_fwd(q, kv, dil, "attn_fwd_d%d" % dil) for _, dil in PATTERNS]
    o, lse, y_att = _attn_combine([p[0] for p in pats], [p[1] for p in pats], gates)
    conv_out, y_conv = _conv_fwd(gates, conv_w, conv_b, ln_g, ln_b)
    dx2, dxb, loss_cols, g_gf = _outproj_loss(x, y_att, y_conv, w_out, gf, target)

    d_o, d_a_gate, delta = _dy_att(dxb, w_out, gates, o)
    d_c_gate, d_conv, g_ln_g, g_ln_b, g_conv_b = _dy_conv(dxb, w_out, gates, conv_out, ln_g, ln_b)
    dgates, g_conv_w = _conv_bwd(d_conv, gates, d_a_gate, d_c_gate, conv_w)
    g_gates = _tn_matmul(h, dgates, "gw_in_gates")
    g_w_out = jnp.concatenate([_tn_matmul(y_att, dxb, "gw_out_att"), _tn_matmul(y_conv, dxb, "gw_out_conv")],
                              axis=0)
    acc, early_out = None, []
    for idx, (_, dil) in enumerate(PATTERNS):
        hosted = early(g_gates, g_w_out, g_conv_w) if (idx == 0 and early is not None) else None
        acc, outs = _attn_bwd(q, kv, d_o, lse, delta, dil, acc, idx == len(PATTERNS) - 1, "attn_bwd_d%d" % dil,
                              hosted)
        early_out += outs
    dq, dkv = acc
    g_q, g_kv = _tn_matmul(h, dq, "gw_in_q"), _tn_matmul(h, dkv, "gw_in_kv")
    grad_x, g_norm_g, late_out = _dh(dq, dkv, dgates, w_in, x, dx2, norm_g,
                                     late(g_q, g_kv, g_gates) if late is not None else None)
    small = (g_norm_g, g_conv_b, g_ln_g, g_ln_b, g_gf, loss_cols)
    return grad_x, (g_q, g_kv, g_gates), g_w_out, g_conv_w, small, early_out, late_out


def kernel(x, norm_g, w_in, conv_w, conv_b, conv_ln_g, conv_ln_b, w_out, final_norm_g, loss_target, m_norm_g, m_w_in, m_conv_w, m_conv_b, m_conv_ln_g, m_conv_ln_b, m_w_out, m_final_norm_g, v_norm_g, v_w_in, v_conv_w, v_conv_b, v_conv_ln_g, v_conv_ln_b, v_w_out, v_final_norm_g):
    S, D = x.shape[1], x.shape[2]
    win_sh, wout_sh, cw_sh = w_in[0], w_out[0], conv_w[0]
    cols_sh, rows_sh, ch_sh = win_sh.shape[1], wout_sh.shape[0], cw_sh.shape[1]

    win_all, wout_all, cw_all = _gather_two_level(
        [win_sh.astype(BF16), wout_sh.astype(BF16), cw_sh], "gather_weights")
    w_in_full = win_all.transpose(1, 0, 2).reshape(D, N_DEV * cols_sh)
    w_out_full = wout_all.reshape(N_DEV * rows_sh, D)
    conv_w_full = cw_all.transpose(1, 0, 2).reshape(CONV_K, N_DEV * ch_sh)
    conv_w_full = jnp.pad(conv_w_full, ((0, CONV_HALO - CONV_K), (0, 0)))
    gf = final_norm_g.reshape(1, D)

    att_cols = ATT_W + 2 * KV_W
    first = -(-att_cols // cols_sh)
    gate_off = first * cols_sh - att_cols

    def pieces(g, n):
        return g.reshape(D, n, cols_sh).transpose(1, 0, 2).astype(BF16)

    def early(g_gates, g_w_out, g_conv_w):
        return _Exchange(
            [pieces(g_gates[:, gate_off:], N_DEV - first), g_w_out.reshape(N_DEV, rows_sh, D).astype(BF16),
             g_conv_w[:CONV_K].reshape(CONV_K, N_DEV, ch_sh).transpose(1, 0, 2)],
            [(first, N_DEV), (0, N_DEV), (0, N_DEV)])

    def late(g_q, g_kv, g_gates):
        return _Exchange([pieces(jnp.concatenate([g_q, g_kv, g_gates[:, :gate_off]], axis=1), first)], [(0, first)])

    grad_x, _, _, _, small, early_out, late_out = _local_step(
        x[0], loss_target[0], norm_g, w_in_full, conv_w_full, conv_b, conv_ln_g, conv_ln_b, w_out_full, gf,
        early, late)
    win_parts_hi, wout_parts, cw_parts = early_out
    win_parts_lo, = late_out

    small_pack = jnp.concatenate(list(small) + [jnp.zeros((2, D), F32)], axis=0)
    small_parts, = _exchange([small_pack], [None], "gather_small")

    upd_win = _adamw((win_parts_lo, win_parts_hi), win_sh, m_w_in[0], v_w_in[0], "adamw_w_in", tr=256, split=first)
    upd_wout = _adamw(wout_parts, wout_sh, m_w_out[0], v_w_out[0], "adamw_w_out", tr=128)
    upd_cw = _adamw(cw_parts, cw_sh, m_conv_w[0], v_conv_w[0], "adamw_conv_w")
    zeros3 = jnp.zeros((3, D), F32)
    stack = lambda a, b, c, d_, e: jnp.concatenate([a, b, c, d_, e.reshape(1, D), zeros3], axis=0)
    upd_small = _adamw(
        small_parts,
        stack(norm_g, conv_b, conv_ln_g, conv_ln_b, final_norm_g),
        stack(m_norm_g, m_conv_b, m_conv_ln_g, m_conv_ln_b, m_final_norm_g),
        stack(v_norm_g, v_conv_b, v_conv_ln_g, v_conv_ln_b, v_final_norm_g) + jnp.concatenate(
            [jnp.zeros((5, D), F32), jnp.ones((3, D), F32)], axis=0),
        "adamw_small")

    loss = 0.5 / D * jnp.sum(upd_small[0][5])

    def outputs(kind):
        sm = upd_small[kind]
        return [sm[0:1], upd_win[kind][None], upd_cw[kind][None], sm[1:2], sm[2:3], sm[3:4],
                upd_wout[kind][None], sm[4]]

    return (loss, grad_x[None], *outputs(0), *outputs(1), *outputs(2), *outputs(3))
```

```python
import functools

import jax
import jax.numpy as jnp
from jax import lax
from jax.experimental import pallas as pl
from jax.experimental.pallas import tpu as pltpu

F32 = jnp.float32
BF16 = jnp.bfloat16

HEAD_DIM = 64
N_KV_HEADS = 4
N_Q_HEADS = 16
ATT_W = 1024
KV_W = 256
CONV_K = 31
CONV_HALO = 32
PATTERNS = ((128, 1), (512, 4), (2048, 16))
BLK = 128
LANES = 128
NORM_EPS = 1e-6
LN_EPS = 1e-5
NEG = -1e30
N_DEV = 8
ADAM_LR, ADAM_B1, ADAM_B2, ADAM_EPS, ADAM_WD, ADAM_STEP = 0.001, 0.9, 0.999, 1e-08, 0.01, 10
VMEM_LIMIT = 48 * 1024 * 1024
SLOPES = tuple(2.0 ** (-8.0 * (h + 1) / N_Q_HEADS) for h in range(N_Q_HEADS))
MESH = pl.DeviceIdType.MESH


def _params(sem):
    return pltpu.CompilerParams(dimension_semantics=sem, vmem_limit_bytes=VMEM_LIMIT)


def _sigmoid(v):
    return 1.0 / (1.0 + jnp.exp(-v))


def _silu_and_grad(v):
    s = _sigmoid(v)
    return v * s, s * (1.0 + v * (1.0 - s))


ANY_SPEC = pl.BlockSpec(memory_space=pl.ANY)


def _mesh_pos():
    x, y, c = lax.axis_index("x"), lax.axis_index("y"), lax.axis_index("c")
    return x, y, c, 4 * x + 2 * y + c


def _flipped(k, x, y, c):
    px = 1 - x if k & 4 else x
    py = 1 - y if k & 2 else y
    pc = 1 - c if k & 1 else c
    return (px, py, pc), 4 * px + 2 * py + pc


class _Exchange:
    def __init__(self, arrays, dests):
        self.arrays, self.dests, self.n = list(arrays), list(dests), len(arrays)

    def out_shapes(self):
        return [jax.ShapeDtypeStruct((N_DEV,) + a.shape[-2:], a.dtype) for a in self.arrays]

    def sem_shapes(self):
        return [pltpu.SemaphoreType.DMA((self.n, N_DEV - 1)), pltpu.SemaphoreType.DMA((self.n, N_DEV - 1)),
                pltpu.SemaphoreType.DMA((self.n,))]

    def _when(self, a, dev, fn):
        if self.dests[a] is None:
            fn()
        else:
            lo, hi = self.dests[a]
            pl.when((dev >= lo) & (dev < hi))(fn)

    def _mine(self, ins, a, dev):
        return ins[a] if self.dests[a] is None else ins[a].at[dev - self.dests[a][0]]

    def _copy(self, ins, outs, sems, a, k, src_dev, slot, target):
        return pltpu.make_async_remote_copy(
            src_ref=self._mine(ins, a, src_dev), dst_ref=outs[a].at[slot],
            send_sem=sems[0].at[a, k - 1], recv_sem=sems[1].at[a, k - 1],
            device_id=target, device_id_type=MESH)

    def start(self, ins, outs, sems):
        x, y, c, me = _mesh_pos()
        for a in range(self.n):
            self._when(a, me, lambda a=a: pltpu.make_async_copy(
                self._mine(ins, a, me), outs[a].at[me], sems[2].at[a]).start())
            for k in range(1, N_DEV):
                target, peer = _flipped(k, x, y, c)
                self._when(a, peer, lambda a=a, k=k, target=target, peer=peer: self._copy(
                    ins, outs, sems, a, k, peer, me, target).start())

    def finish(self, ins, outs, sems):
        x, y, c, me = _mesh_pos()
        lo0 = [0 if d is None else d[0] for d in self.dests]
        for a in range(self.n):
            for k in range(1, N_DEV):
                target, peer = _flipped(k, x, y, c)
                self._when(a, me, lambda a=a, k=k, peer=peer: self._copy(
                    ins, outs, sems, a, k, lo0[a], peer, (x, y, c)).wait_recv())
            for k in range(1, N_DEV):
                target, peer = _flipped(k, x, y, c)
                self._when(a, peer, lambda a=a, k=k, target=target, peer=peer: self._copy(
                    ins, outs, sems, a, k, peer, me, target).wait_send())
            self._when(a, me, lambda a=a: pltpu.make_async_copy(
                self._mine(ins, a, me), outs[a].at[me], sems[2].at[a]).wait())


def _exchange(arrays, dests, name):
    ex = _Exchange(arrays, dests)
    na = ex.n

    def body(*refs):
        ins, outs, sems = refs[:na], refs[na:2 * na], refs[2 * na:]
        ex.start(ins, outs, sems)
        ex.finish(ins, outs, sems)

    return pl.pallas_call(
        body, name=name, out_shape=tuple(ex.out_shapes()),
        in_specs=[ANY_SPEC] * na, out_specs=tuple([ANY_SPEC] * na), scratch_shapes=ex.sem_shapes(),
    )(*arrays)


def _gather_two_level(arrays, name):
    na = len(arrays)

    def body(*refs):
        ins, outs = refs[:na], refs[na:2 * na]
        send_sems, recv_sems, loc_sems = refs[2 * na:]
        x, y, c, me = _mesh_pos()
        sibling = (x, y, 1 - c)
        chips = [(1 - x, y), (x, 1 - y), (1 - x, 1 - y)]

        def slot(px, py, pc):
            return 4 * px + 2 * py + pc

        def copy(a, k, src, block, to):
            return pltpu.make_async_remote_copy(
                src_ref=src, dst_ref=outs[a].at[slot(*block)], send_sem=send_sems.at[a, k], recv_sem=recv_sems.at[a, k],
                device_id=to, device_id_type=MESH)

        local = [pltpu.make_async_copy(ins[a], outs[a].at[me], loc_sems.at[a]) for a in range(na)]
        for cp in local:
            cp.start()
        started = []
        for a in range(na):
            started.append(copy(a, 0, ins[a], (x, y, c), sibling))
            started += [copy(a, 1 + j, ins[a], (x, y, c), (*chip, c)) for j, chip in enumerate(chips)]
        for cp in started:
            cp.start()
        for j, chip in enumerate(chips):
            for a in range(na):
                copy(a, 1 + j, ins[a], (*chip, c), (x, y, c)).wait_recv()
                fwd = copy(a, 4 + j, outs[a].at[slot(*chip, c)], (*chip, c), sibling)
                fwd.start()
                started.append(fwd)
        for a in range(na):
            copy(a, 0, ins[a], sibling, (x, y, c)).wait_recv()
            for j, chip in enumerate(chips):
                copy(a, 4 + j, ins[a], (*chip, 1 - c), (x, y, c)).wait_recv()
        for cp in started:
            cp.wait_send()
        for cp in local:
            cp.wait()

    return pl.pallas_call(
        body, name=name,
        out_shape=tuple(jax.ShapeDtypeStruct((N_DEV,) + a.shape, a.dtype) for a in arrays),
        in_specs=[ANY_SPEC] * na, out_specs=tuple([ANY_SPEC] * na),
        scratch_shapes=[pltpu.SemaphoreType.DMA((na, N_DEV - 1)), pltpu.SemaphoreType.DMA((na, N_DEV - 1)),
                        pltpu.SemaphoreType.DMA((na,))],
    )(*arrays)


def _inproj(x, g, w, col_block, ncols, out_dtype, scale, emit_h, name, tm=1024, tn=512):
    S, D = x.shape

    def body(x_ref, g_ref, w_ref, *rest):
        if emit_h:
            o_ref, h_out, h_scr = rest
        else:
            o_ref, h_scr = rest

        @pl.when(pl.program_id(1) == 0)
        def _():
            xf = x_ref[...]
            r = lax.rsqrt(jnp.mean(xf * xf, axis=-1, keepdims=True) + NORM_EPS)
            h = (xf * r * g_ref[...]).astype(BF16)
            h_scr[...] = h
            if emit_h:
                h_out[...] = h

        acc = jnp.dot(h_scr[...], w_ref[...], preferred_element_type=F32)
        if scale != 1.0:
            acc = acc * scale
        o_ref[...] = acc.astype(out_dtype)

    out_shape = [jax.ShapeDtypeStruct((S, ncols), out_dtype)]
    out_specs = [pl.BlockSpec((tm, tn), lambda i, j: (i, j))]
    if emit_h:
        out_shape.append(jax.ShapeDtypeStruct((S, D), BF16))
        out_specs.append(pl.BlockSpec((tm, D), lambda i, j: (i, 0)))
    res = pl.pallas_call(
        body, name=name, grid=(S // tm, ncols // tn),
        in_specs=[pl.BlockSpec((tm, D), lambda i, j: (i, 0)),
                  pl.BlockSpec((1, D), lambda i, j: (0, 0)),
                  pl.BlockSpec((D, tn), lambda i, j: (0, col_block + j))],
        out_specs=tuple(out_specs), out_shape=tuple(out_shape),
        scratch_shapes=[pltpu.VMEM((tm, D), BF16)],
        compiler_params=_params(("parallel", "arbitrary")),
    )(x, g, w)
    return res if emit_h else res[0]


def _window_mask(n, dil):
    qi = lax.broadcasted_iota(jnp.int32, (BLK, 2 * BLK), 0)
    kj = lax.broadcasted_iota(jnp.int32, (BLK, 2 * BLK), 1)
    dist = BLK + qi - kj
    valid = (dist >= 0) & (dist <= BLK) & ((kj >= BLK) | (n > 0))
    negd = (dist * (-dil)).astype(F32)
    return valid, negd


def _head_operands(kv2, hk, lo_mask):
    half, pos = hk // 2, hk % 2
    out = []
    for base in (0, KV_W):
        t = kv2[:, base + half * LANES: base + (half + 1) * LANES].astype(F32)
        sw = pltpu.roll(t, HEAD_DIM, axis=1)
        at_lo, at_hi = (t, sw) if pos == 0 else (sw, t)
        out.append(jnp.where(lo_mask, at_lo, 0.0).astype(BF16))
        out.append(jnp.where(lo_mask, 0.0, at_hi).astype(BF16))
    return out


def _nt(a, b):
    return lax.dot_general(a, b, (((1,), (1,)), ((), ())), preferred_element_type=F32)


def _tn(a, b):
    return lax.dot_general(a, b, (((0,), (0,)), ((), ())), preferred_element_type=F32)


def _attn_fwd(q, kv, dil, name):
    S = q.shape[0]
    L = S // dil
    nb = L // BLK
    qv = q.reshape(L, dil * ATT_W)
    kvv = kv.reshape(L, dil * 2 * KV_W)

    def body(q_ref, kvc_ref, kvp_ref, o_ref, lse_ref):
        n = pl.program_id(1)
        valid, negd = _window_mask(n, dil)
        valid2 = jnp.concatenate([valid, valid], axis=0)
        negd2 = jnp.concatenate([negd, negd], axis=0)
        kv2 = jnp.concatenate([kvp_ref[...], kvc_ref[...]], axis=0)
        lo_mask = lax.broadcasted_iota(jnp.int32, (2 * BLK, LANES), 1) < HEAD_DIM
        top = lax.broadcasted_iota(jnp.int32, (2 * BLK, 1), 0) < BLK
        lane = lax.broadcasted_iota(jnp.int32, (BLK, LANES), 1)
        stats = jnp.zeros((BLK, LANES), F32)
        for hk in range(N_KV_HEADS):
            k_lo, k_hi, v_lo, v_hi = _head_operands(kv2, hk, lo_mask)
            cols = [slice(b * LANES, (b + 1) * LANES) for b in (2 * hk, 2 * hk + 1)]
            q2 = jnp.concatenate([q_ref[:, cols[0]], q_ref[:, cols[1]]], axis=0)
            o2 = jnp.zeros((2 * BLK, LANES), F32)
            for which, (kk, vv) in enumerate(((k_lo, v_lo), (k_hi, v_hi))):
                h0, h1 = 4 * hk + which, 4 * hk + 2 + which
                s = _nt(q2, kk) + jnp.where(top, SLOPES[h0], SLOPES[h1]) * negd2
                s = jnp.where(valid2, s, NEG)
                m = jnp.max(s, axis=1, keepdims=True)
                p = jnp.exp(s - m)
                l = jnp.sum(p, axis=1, keepdims=True)
                o2 = o2 + jnp.dot(p.astype(BF16), vv, preferred_element_type=F32) * (1.0 / l)
                lse = m + jnp.log(l)
                stats = jnp.where(lane == h0, lse[0:BLK], stats)
                stats = jnp.where(lane == h1, lse[BLK:], stats)
            o_ref[:, cols[0]] = o2[0:BLK]
            o_ref[:, cols[1]] = o2[BLK:]
        lse_ref[...] = stats

    o, lse = pl.pallas_call(
        body, name=name, grid=(dil, nb),
        in_specs=[pl.BlockSpec((BLK, ATT_W), lambda r, n: (n, r)),
                  pl.BlockSpec((BLK, 2 * KV_W), lambda r, n: (n, r)),
                  pl.BlockSpec((BLK, 2 * KV_W), lambda r, n: (jnp.maximum(n - 1, 0), r))],
        out_specs=(pl.BlockSpec((BLK, ATT_W), lambda r, n: (n, r)),
                   pl.BlockSpec((BLK, LANES), lambda r, n: (n, r))),
        out_shape=(jax.ShapeDtypeStruct((L, dil * ATT_W), F32),
                   jax.ShapeDtypeStruct((L, dil * LANES), F32)),
        compiler_params=_params(("parallel", "parallel")),
    )(qv, kvv, kvv)
    return o.reshape(S, ATT_W), lse.reshape(S, LANES)


def _attn_combine(os_, lses, gates, tm=256):
    S = os_[0].shape[0]
    npat = len(os_)

    def body(*refs):
        o_refs = refs[:npat]
        l_refs = refs[npat:2 * npat]
        gate_ref, o_out, lse_out, y_out = refs[2 * npat:]
        ls = [r[...] for r in l_refs]
        mx = functools.reduce(jnp.maximum, ls)
        es = [jnp.exp(l - mx) for l in ls]
        tot = functools.reduce(jnp.add, es)
        inv = 1.0 / tot
        ws = [e * inv for e in es]
        lse_out[...] = mx + jnp.log(tot)
        lo = lax.broadcasted_iota(jnp.int32, (tm, LANES), 1) < HEAD_DIM
        for blk in range(ATT_W // LANES):
            cols = slice(blk * LANES, (blk + 1) * LANES)
            acc = jnp.zeros((tm, LANES), F32)
            for p in range(npat):
                sc = jnp.where(lo, ws[p][:, 2 * blk:2 * blk + 1], ws[p][:, 2 * blk + 1:2 * blk + 2])
                acc = acc + sc * o_refs[p][:, cols]
            o_out[:, cols] = acc
            a = gate_ref[:, cols]
            y_out[:, cols] = (acc * (a * _sigmoid(a))).astype(BF16)

    row = lambda w: pl.BlockSpec((tm, w), lambda i: (i, 0))
    return pl.pallas_call(
        body, name="attn_combine", grid=(S // tm,),
        in_specs=[row(ATT_W)] * npat + [row(LANES)] * npat + [row(ATT_W)],
        out_specs=(row(ATT_W), row(LANES), row(ATT_W)),
        out_shape=(jax.ShapeDtypeStruct((S, ATT_W), F32), jax.ShapeDtypeStruct((S, LANES), F32),
                   jax.ShapeDtypeStruct((S, ATT_W), BF16)),
        compiler_params=_params(("parallel",)),
    )(*os_, *lses, gates)


def _shifted_copies(buf, phases):
    n = phases.shape[1]
    for b in range(1, 8):
        phases[b - 1] = buf[b:b + n, :]


def _window(buf, phases, start, cols):
    b = start % 8
    if b == 0:
        return buf[start:start + 8, cols]
    return phases[b - 1, start - b:start - b + 8, cols]


def _conv_fwd(gates, conv_w, conv_b, ln_g, ln_b, tt=128):
    S = gates.shape[0]
    C = conv_w.shape[1]
    hb = tt // CONV_HALO

    def body(val_ref, glu_ref, hval_ref, hglu_ref, gate_ref, w_ref, b_ref, g_ref, beta_ref,
             conv_ref, y_ref, hbuf, hph):
        i = pl.program_id(0)
        halo = hval_ref[...] * _sigmoid(hglu_ref[...])
        hbuf[0:CONV_HALO, :] = jnp.where(i > 0, halo, 0.0)
        hbuf[CONV_HALO:, :] = val_ref[...] * _sigmoid(glu_ref[...])
        _shifted_copies(hbuf, hph)
        for cb in range(C // LANES):
            cols = slice(cb * LANES, (cb + 1) * LANES)
            wj = [jnp.broadcast_to(w_ref[j:j + 1, cols], (8, LANES)) for j in range(CONV_K)]
            for rc in range(tt // 8):
                acc = jnp.zeros((8, LANES), F32)
                for j in range(CONV_K):
                    start = rc * 8 + CONV_HALO - (CONV_K - 1) + j
                    acc = acc + _window(hbuf, hph, start, cols) * wj[j]
                conv_ref[rc * 8:(rc + 1) * 8, cols] = acc
        cv = conv_ref[...] + b_ref[...]
        conv_ref[...] = cv
        mu = jnp.mean(cv, axis=-1, keepdims=True)
        xc = cv - mu
        var = jnp.mean(xc * xc, axis=-1, keepdims=True)
        ln = xc * lax.rsqrt(var + LN_EPS) * g_ref[...] + beta_ref[...]
        gt = gate_ref[...]
        y_ref[...] = (ln * _sigmoid(ln) * (gt * _sigmoid(gt))).astype(BF16)

    vec = pl.BlockSpec((1, C), lambda i: (0, 0))
    return pl.pallas_call(
        body, name="conv_fwd", grid=(S // tt,),
        in_specs=[pl.BlockSpec((tt, C), lambda i: (i, 1)),
                  pl.BlockSpec((tt, C), lambda i: (i, 2)),
                  pl.BlockSpec((CONV_HALO, C), lambda i: (jnp.maximum(i * hb - 1, 0), 1)),
                  pl.BlockSpec((CONV_HALO, C), lambda i: (jnp.maximum(i * hb - 1, 0), 2)),
                  pl.BlockSpec((tt, C), lambda i: (i, 3)),
                  pl.BlockSpec((CONV_HALO, C), lambda i: (0, 0)), vec, vec, vec],
        out_specs=(pl.BlockSpec((tt, C), lambda i: (i, 0)), pl.BlockSpec((tt, C), lambda i: (i, 0))),
        out_shape=(jax.ShapeDtypeStruct((S, C), F32), jax.ShapeDtypeStruct((S, C), BF16)),
        scratch_shapes=[pltpu.VMEM((tt + CONV_HALO, C), F32), pltpu.VMEM((7, tt + CONV_HALO - 8, C), F32)],
        compiler_params=_params(("parallel",)),
    )(gates, gates, gates, gates, gates, conv_w, conv_b, ln_g, ln_b)


def _outproj_loss(x, y_att, y_conv, w_out, gf, target, tm=512):
    S, D = x.shape
    E = y_att.shape[1]

    def body(x_ref, ya_ref, yc_ref, w_ref, gf_ref, t_ref, dx_ref, dxb_ref, loss_ref, ggf_ref):
        @pl.when(pl.program_id(0) == 0)
        def _():
            loss_ref[...] = jnp.zeros_like(loss_ref)
            ggf_ref[...] = jnp.zeros_like(ggf_ref)

        x2 = (x_ref[...] + jnp.dot(ya_ref[...], w_ref[0:E, :], preferred_element_type=F32)
              + jnp.dot(yc_ref[...], w_ref[E:, :], preferred_element_type=F32))
        r = lax.rsqrt(jnp.mean(x2 * x2, axis=-1, keepdims=True) + NORM_EPS)
        nrm = x2 * r
        gfv = gf_ref[...]
        err = nrm * gfv - t_ref[...]
        loss_ref[...] += jnp.sum(err * err, axis=0, keepdims=True)
        dout = err * (1.0 / D)
        ggf_ref[...] += jnp.sum(dout * nrm, axis=0, keepdims=True)
        dn = dout * gfv
        dx2 = r * (dn - nrm * jnp.mean(dn * nrm, axis=-1, keepdims=True))
        dx_ref[...] = dx2
        dxb_ref[...] = dx2.astype(BF16)

    row = lambda w: pl.BlockSpec((tm, w), lambda i: (i, 0))
    vec = pl.BlockSpec((1, D), lambda i: (0, 0))
    return pl.pallas_call(
        body, name="outproj_loss", grid=(S // tm,),
        in_specs=[row(D), row(E), row(E), pl.BlockSpec((2 * E, D), lambda i: (0, 0)), vec, row(D)],
        out_specs=(row(D), row(D), vec, vec),
        out_shape=(jax.ShapeDtypeStruct((S, D), F32), jax.ShapeDtypeStruct((S, D), BF16),
                   jax.ShapeDtypeStruct((1, D), F32), jax.ShapeDtypeStruct((1, D), F32)),
        compiler_params=_params(("arbitrary",)),
    )(x, y_att, y_conv, w_out, gf, target)


def _split3(v):
    hi = v.astype(BF16)
    r1 = v - hi.astype(F32)
    mid = r1.astype(BF16)
    lo = (r1 - mid.astype(F32)).astype(BF16)
    return hi, mid, lo


def _dy_att(dxb, w_out, gates, o, tm=512):
    S, D = dxb.shape
    E = ATT_W

    def body(dx_ref, w_ref, a_ref, o_ref, do_ref, da_ref, dl_ref):
        dya = _nt(dx_ref[...], w_ref[...])
        a = a_ref[...]
        ov = o_ref[...]
        sl, dsl = _silu_and_grad(a)
        d_o = dya * sl
        do_ref[...] = d_o.astype(BF16)
        da_ref[...] = (dya * ov * dsl).astype(BF16)
        ci = lax.broadcasted_iota(jnp.int32, (E, LANES), 0) // HEAD_DIM
        hi = lax.broadcasted_iota(jnp.int32, (E, LANES), 1)
        sel = jnp.where(ci == hi, 1.0, 0.0).astype(BF16)
        acc = jnp.zeros((tm, LANES), F32)
        for part in _split3(d_o * ov):
            acc = acc + jnp.dot(part, sel, preferred_element_type=F32)
        dl_ref[...] = acc

    row = lambda w: pl.BlockSpec((tm, w), lambda i: (i, 0))
    return pl.pallas_call(
        body, name="dy_att", grid=(S // tm,),
        in_specs=[row(D), pl.BlockSpec((E, D), lambda i: (0, 0)), row(E), row(E)],
        out_specs=(row(E), row(E), row(LANES)),
        out_shape=(jax.ShapeDtypeStruct((S, E), BF16), jax.ShapeDtypeStruct((S, E), BF16),
                   jax.ShapeDtypeStruct((S, LANES), F32)),
        compiler_params=_params(("parallel",)),
    )(dxb, w_out, gates, o)


def _dy_conv(dxb, w_out, gates, conv_out, ln_g, ln_b, tm=512):
    S, D = dxb.shape
    C = conv_out.shape[1]

    def body(dx_ref, w_ref, gate_ref, cv_ref, g_ref, beta_ref, dgate_ref, dconv_ref, gg_ref, gb_ref, gcb_ref):
        @pl.when(pl.program_id(0) == 0)
        def _():
            gg_ref[...] = jnp.zeros_like(gg_ref)
            gb_ref[...] = jnp.zeros_like(gb_ref)
            gcb_ref[...] = jnp.zeros_like(gcb_ref)

        dyc = _nt(dx_ref[...], w_ref[...])
        cv = cv_ref[...]
        mu = jnp.mean(cv, axis=-1, keepdims=True)
        xc = cv - mu
        rstd = lax.rsqrt(jnp.mean(xc * xc, axis=-1, keepdims=True) + LN_EPS)
        nrm = xc * rstd
        gv = g_ref[...]
        ln = nrm * gv + beta_ref[...]
        u, du = _silu_and_grad(ln)
        gt = gate_ref[...]
        g2, dg2 = _silu_and_grad(gt)
        dgate_ref[...] = (dyc * u * dg2).astype(BF16)
        d_ln = dyc * g2 * du
        gb_ref[...] += jnp.sum(d_ln, axis=0, keepdims=True)
        gg_ref[...] += jnp.sum(d_ln * nrm, axis=0, keepdims=True)
        dn = d_ln * gv
        d_conv = rstd * (dn - jnp.mean(dn, axis=-1, keepdims=True)
                         - nrm * jnp.mean(dn * nrm, axis=-1, keepdims=True))
        dconv_ref[...] = d_conv
        gcb_ref[...] += jnp.sum(d_conv, axis=0, keepdims=True)

    row = lambda w: pl.BlockSpec((tm, w), lambda i: (i, 0))
    vec = pl.BlockSpec((1, C), lambda i: (0, 0))
    return pl.pallas_call(
        body, name="dy_conv", grid=(S // tm,),
        in_specs=[row(D), pl.BlockSpec((C, D), lambda i: (1, 0)),
                  pl.BlockSpec((tm, C), lambda i: (i, 3)), row(C), vec, vec],
        out_specs=(row(C), row(C), vec, vec, vec),
        out_shape=(jax.ShapeDtypeStruct((S, C), BF16), jax.ShapeDtypeStruct((S, C), F32),
                   jax.ShapeDtypeStruct((1, C), F32), jax.ShapeDtypeStruct((1, C), F32),
                   jax.ShapeDtypeStruct((1, C), F32)),
        compiler_params=_params(("arbitrary",)),
    )(dxb, w_out, gates, conv_out, ln_g, ln_b)


def _conv_bwd(d_conv, gates, d_a_gate, d_c_gate, conv_w, tt=128):
    S, C = d_conv.shape
    hb = tt // CONV_HALO
    nt = S // tt

    def body(dc_ref, dnext_ref, val_ref, glu_ref, hval_ref, hglu_ref, da_ref, dg_ref, w_ref,
             out_ref, gw_ref, hbuf, dbuf, dhbuf, hph, dph):
        i = pl.program_id(0)

        @pl.when(i == 0)
        def _():
            gw_ref[...] = jnp.zeros_like(gw_ref)

        val = val_ref[...]
        sg = _sigmoid(glu_ref[...])
        halo = hval_ref[...] * _sigmoid(hglu_ref[...])
        hbuf[0:CONV_HALO, :] = jnp.where(i > 0, halo, 0.0)
        hbuf[CONV_HALO:, :] = val * sg
        dbuf[0:tt, :] = dc_ref[...]
        dbuf[tt:, :] = jnp.where(i < nt - 1, dnext_ref[...], 0.0)
        _shifted_copies(hbuf, hph)
        _shifted_copies(dbuf, dph)
        for cb in range(C // LANES):
            cols = slice(cb * LANES, (cb + 1) * LANES)
            wj = [jnp.broadcast_to(w_ref[j:j + 1, cols], (8, LANES)) for j in range(CONV_K)]
            for rc in range(tt // 8):
                acc = jnp.zeros((8, LANES), F32)
                for j in range(CONV_K):
                    acc = acc + _window(dbuf, dph, rc * 8 + (CONV_K - 1) - j, cols) * wj[j]
                dhbuf[rc * 8:(rc + 1) * 8, cols] = acc
            gacc = [jnp.zeros((8, LANES), F32) for _ in range(CONV_K)]
            for rc in range(tt // 8):
                dcur = dbuf[rc * 8:(rc + 1) * 8, cols]
                for j in range(CONV_K):
                    hs = rc * 8 + CONV_HALO - (CONV_K - 1) + j
                    gacc[j] = gacc[j] + dcur * _window(hbuf, hph, hs, cols)
            for j in range(CONV_K):
                gw_ref[j:j + 1, cols] += jnp.sum(gacc[j], axis=0, keepdims=True)
        d_h = dhbuf[...]
        out_ref[:, 0:C] = da_ref[...]
        out_ref[:, C:2 * C] = (d_h * sg).astype(BF16)
        out_ref[:, 2 * C:3 * C] = (d_h * val * sg * (1.0 - sg)).astype(BF16)
        out_ref[:, 3 * C:4 * C] = dg_ref[...]

    tile = lambda col: pl.BlockSpec((tt, C), lambda i: (i, col))
    return pl.pallas_call(
        body, name="conv_bwd", grid=(nt,),
        in_specs=[tile(0),
                  pl.BlockSpec((CONV_HALO, C), lambda i: (jnp.minimum((i + 1) * hb, S // CONV_HALO - 1), 0)),
                  tile(1), tile(2),
                  pl.BlockSpec((CONV_HALO, C), lambda i: (jnp.maximum(i * hb - 1, 0), 1)),
                  pl.BlockSpec((CONV_HALO, C), lambda i: (jnp.maximum(i * hb - 1, 0), 2)),
                  tile(0), tile(0),
                  pl.BlockSpec((CONV_HALO, C), lambda i: (0, 0))],
        out_specs=(pl.BlockSpec((tt, 4 * C), lambda i: (i, 0)),
                   pl.BlockSpec((CONV_HALO, C), lambda i: (0, 0))),
        out_shape=(jax.ShapeDtypeStruct((S, 4 * C), BF16), jax.ShapeDtypeStruct((CONV_HALO, C), F32)),
        scratch_shapes=[pltpu.VMEM((tt + CONV_HALO, C), F32), pltpu.VMEM((tt + CONV_HALO, C), F32),
                        pltpu.VMEM((tt, C), F32),
                        pltpu.VMEM((7, tt + CONV_HALO - 8, C), F32), pltpu.VMEM((7, tt + CONV_HALO - 8, C), F32)],
        compiler_params=_params(("arbitrary",)),
    )(d_conv, d_conv, gates, gates, gates, gates, d_a_gate, d_c_gate, conv_w)


def _attn_bwd(q, kv, d_o, lse, delta, dil, prev, final, name, hosted=None):
    S = q.shape[0]
    L = S // dil
    nb = L // BLK
    out_dt = BF16 if final else F32
    have_prev = prev is not None
    view = lambda a, w: a.reshape(L, dil * w)

    hn = hosted.n if hosted is not None else 0

    def body(*refs):
        refs = list(refs)
        q_ref, do_ref, lse_ref, dl_ref, kvc_ref, kvp_ref = refs[:6]
        del refs[:6]
        if have_prev:
            pdq_ref, pdkv_ref = refs[:2]
            del refs[:2]
        h_ins = refs[:hn]
        dq_ref, dkv_ref = refs[hn:hn + 2]
        h_outs = refs[hn + 2:2 * hn + 2]
        carry = refs[2 * hn + 2]
        h_sems = refs[2 * hn + 3:]
        n = pl.program_id(1)
        if hosted is not None:
            @pl.when((pl.program_id(0) == 0) & (n == 0))
            def _():
                hosted.start(h_ins, h_outs, h_sems)

        @pl.when(n == 0)
        def _():
            carry[...] = jnp.zeros_like(carry)

        @pl.when(n < nb)
        def _():
            valid, negd = _window_mask(n, dil)
            valid2 = jnp.concatenate([valid, valid], axis=0)
            negd2 = jnp.concatenate([negd, negd], axis=0)
            kv2 = jnp.concatenate([kvp_ref[...], kvc_ref[...]], axis=0)
            lo_mask = lax.broadcasted_iota(jnp.int32, (2 * BLK, LANES), 1) < HEAD_DIM
            top = lax.broadcasted_iota(jnp.int32, (2 * BLK, 1), 0) < BLK
            halves = [jnp.zeros((2 * BLK, LANES), F32) for _ in range(4)]
            for hk in range(N_KV_HEADS):
                k_lo, k_hi, v_lo, v_hi = _head_operands(kv2, hk, lo_mask)
                cols = [slice(b * LANES, (b + 1) * LANES) for b in (2 * hk, 2 * hk + 1)]
                q2 = jnp.concatenate([q_ref[:, cols[0]], q_ref[:, cols[1]]], axis=0)
                do2 = jnp.concatenate([do_ref[:, cols[0]], do_ref[:, cols[1]]], axis=0)
                dq2 = jnp.zeros((2 * BLK, LANES), F32)
                dks, dvs = [], []
                for which, (kk, vv) in enumerate(((k_lo, v_lo), (k_hi, v_hi))):
                    h0, h1 = 4 * hk + which, 4 * hk + 2 + which
                    s = _nt(q2, kk) + jnp.where(top, SLOPES[h0], SLOPES[h1]) * negd2
                    s = jnp.where(valid2, s, NEG)
                    lse2 = jnp.concatenate([lse_ref[:, h0:h0 + 1], lse_ref[:, h1:h1 + 1]], axis=0)
                    dl2 = jnp.concatenate([dl_ref[:, h0:h0 + 1], dl_ref[:, h1:h1 + 1]], axis=0)
                    p = jnp.exp(s - lse2)
                    ds = (p * (_nt(do2, vv) - dl2)).astype(BF16)
                    dq2 = dq2 + jnp.dot(ds, kk, preferred_element_type=F32)
                    dks.append(_tn(ds, q2))
                    dvs.append(_tn(p.astype(BF16), do2))
                dk_sum = jnp.where(lo_mask, dks[0], dks[1])
                dv_sum = jnp.where(lo_mask, dvs[0], dvs[1])
                for jp in range(2):
                    dq_blk = dq2[jp * BLK:(jp + 1) * BLK]
                    if have_prev:
                        dq_blk = dq_blk + pdq_ref[:, cols[jp]]
                    if final:
                        dq_blk = dq_blk * (HEAD_DIM ** -0.5)
                    dq_ref[:, cols[jp]] = dq_blk.astype(out_dt)
                half, pos = hk // 2, hk % 2
                here = lo_mask if pos == 0 else jnp.logical_not(lo_mask)
                dk_tot = dk_sum + pltpu.roll(dk_sum, HEAD_DIM, axis=1)
                dv_tot = dv_sum + pltpu.roll(dv_sum, HEAD_DIM, axis=1)
                halves[half] = halves[half] + jnp.where(here, dk_tot, 0.0)
                halves[2 + half] = halves[2 + half] + jnp.where(here, dv_tot, 0.0)
            for b in range(4):
                cols = slice(b * LANES, (b + 1) * LANES)
                done = carry[:, cols] + halves[b][0:BLK, :]
                if have_prev:
                    done = done + pdkv_ref[:, cols]
                dkv_ref[:, cols] = done.astype(out_dt)
                carry[:, cols] = halves[b][BLK:, :]

        @pl.when(n == nb)
        def _():
            done = carry[...]
            if have_prev:
                done = done + pdkv_ref[...]
            dkv_ref[...] = done.astype(out_dt)

        if hosted is not None:
            @pl.when((pl.program_id(0) == dil - 1) & (n == nb))
            def _():
                hosted.finish(h_ins, h_outs, h_sems)

    cur = lambda r, n: (jnp.minimum(n, nb - 1), r)
    behind = lambda r, n: (jnp.maximum(n - 1, 0), r)
    in_specs = [pl.BlockSpec((BLK, ATT_W), cur), pl.BlockSpec((BLK, ATT_W), cur),
                pl.BlockSpec((BLK, LANES), cur), pl.BlockSpec((BLK, LANES), cur),
                pl.BlockSpec((BLK, 2 * KV_W), cur), pl.BlockSpec((BLK, 2 * KV_W), behind)]
    args = [view(q, ATT_W), view(d_o, ATT_W), view(lse, LANES), view(delta, LANES),
            view(kv, 2 * KV_W), view(kv, 2 * KV_W)]
    if have_prev:
        in_specs += [pl.BlockSpec((BLK, ATT_W), cur), pl.BlockSpec((BLK, 2 * KV_W), behind)]
        args += [view(prev[0], ATT_W), view(prev[1], 2 * KV_W)]
    out_specs = [pl.BlockSpec((BLK, ATT_W), cur), pl.BlockSpec((BLK, 2 * KV_W), behind)]
    out_shape = [jax.ShapeDtypeStruct((L, dil * ATT_W), out_dt), jax.ShapeDtypeStruct((L, dil * 2 * KV_W), out_dt)]
    scratch = [pltpu.VMEM((BLK, 2 * KV_W), F32)]
    if hosted is not None:
        in_specs += [ANY_SPEC] * hn
        args += hosted.arrays
        out_specs += [ANY_SPEC] * hn
        out_shape += hosted.out_shapes()
        scratch += hosted.sem_shapes()
    res = pl.pallas_call(
        body, name=name, grid=(dil, nb + 1),
        in_specs=in_specs, out_specs=tuple(out_specs), out_shape=tuple(out_shape), scratch_shapes=scratch,
        compiler_params=_params(("arbitrary", "arbitrary")),
    )(*args)
    return (res[0].reshape(S, ATT_W), res[1].reshape(S, 2 * KV_W)), list(res[2:])


def _dh(dq, dkv, dgates, w_in, x, dx2, g, hosted=None, tm=1024, tk=512):
    S, D = x.shape
    nq = dq.shape[1] // tk
    nkv = dkv.shape[1] // tk
    ng = dgates.shape[1] // tk
    nk = nq + nkv + ng
    hn = hosted.n if hosted is not None else 0

    def body(*refs):
        dq_ref, dkv_ref, dg_ref, w_ref, x_ref, dx2_ref, g_ref = refs[:7]
        h_ins = refs[7:7 + hn]
        gx_ref, gng_ref = refs[7 + hn:9 + hn]
        h_outs = refs[9 + hn:9 + 2 * hn]
        acc = refs[9 + 2 * hn]
        h_sems = refs[10 + 2 * hn:]
        i, k = pl.program_id(0), pl.program_id(1)

        @pl.when((i == 0) & (k == 0))
        def _():
            gng_ref[...] = jnp.zeros_like(gng_ref)
            if hosted is not None:
                hosted.start(h_ins, h_outs, h_sems)

        @pl.when(k == 0)
        def _():
            acc[...] = jnp.zeros_like(acc)

        @pl.when(k < nq)
        def _():
            acc[...] += _nt(dq_ref[...], w_ref[...])

        @pl.when((k >= nq) & (k < nq + nkv))
        def _():
            acc[...] += _nt(dkv_ref[...], w_ref[...])

        @pl.when(k >= nq + nkv)
        def _():
            acc[...] += _nt(dg_ref[...], w_ref[...])

        @pl.when(k == nk - 1)
        def _():
            dh = acc[...]
            xf = x_ref[...]
            r = lax.rsqrt(jnp.mean(xf * xf, axis=-1, keepdims=True) + NORM_EPS)
            nrm = xf * r
            gng_ref[...] += jnp.sum(dh * nrm, axis=0, keepdims=True)
            dn = dh * g_ref[...]
            gx_ref[...] = dx2_ref[...] + r * (dn - nrm * jnp.mean(dn * nrm, axis=-1, keepdims=True))

        if hosted is not None:
            @pl.when((i == S // tm - 1) & (k == nk - 1))
            def _():
                hosted.finish(h_ins, h_outs, h_sems)

    row = pl.BlockSpec((tm, D), lambda i, k: (i, 0))
    vec = pl.BlockSpec((1, D), lambda i, k: (0, 0))
    in_specs = [pl.BlockSpec((tm, tk), lambda i, k: (i, jnp.minimum(k, nq - 1))),
                pl.BlockSpec((tm, tk), lambda i, k: (i, jnp.clip(k - nq, 0, nkv - 1))),
                pl.BlockSpec((tm, tk), lambda i, k: (i, jnp.clip(k - nq - nkv, 0, ng - 1))),
                pl.BlockSpec((D, tk), lambda i, k: (0, k)), row, row, vec]
    args = [dq, dkv, dgates, w_in, x, dx2, g]
    out_specs = [row, vec]
    out_shape = [jax.ShapeDtypeStruct((S, D), F32), jax.ShapeDtypeStruct((1, D), F32)]
    scratch = [pltpu.VMEM((tm, D), F32)]
    if hosted is not None:
        in_specs += [ANY_SPEC] * hn
        args += hosted.arrays
        out_specs += [ANY_SPEC] * hn
        out_shape += hosted.out_shapes()
        scratch += hosted.sem_shapes()
    res = pl.pallas_call(
        body, name="dh", grid=(S // tm, nk),
        in_specs=in_specs, out_specs=tuple(out_specs), out_shape=tuple(out_shape), scratch_shapes=scratch,
        compiler_params=_params(("arbitrary", "arbitrary")),
    )(*args)
    return res[0], res[1], list(res[2:])


def _tn_matmul(a, b, name, tm=512):
    M, K = a.shape
    N = b.shape[1]
    tn = min(N, 1024)

    def body(a_ref, b_ref, o_ref):
        @pl.when(pl.program_id(1) == 0)
        def _():
            o_ref[...] = jnp.zeros_like(o_ref)

        o_ref[...] += _tn(a_ref[...], b_ref[...])

    return pl.pallas_call(
        body, name=name, grid=(N // tn, M // tm),
        in_specs=[pl.BlockSpec((tm, K), lambda j, m: (m, 0)), pl.BlockSpec((tm, tn), lambda j, m: (m, j))],
        out_specs=pl.BlockSpec((K, tn), lambda j, m: (0, j)),
        out_shape=jax.ShapeDtypeStruct((K, N), F32),
        compiler_params=_params(("parallel", "arbitrary")),
    )(a, b)


def _adamw(parts, w, m, v, name, tr=None, split=None):
    R, C = w.shape
    tr = R if tr is None else tr
    parts = [parts] if split is None else list(parts)
    npar = len(parts)

    def total(p_ref):
        g = p_ref[0].astype(F32)
        for dev in range(1, N_DEV):
            g = g + p_ref[dev].astype(F32)
        return g

    def body(*refs):
        w_ref, m_ref, v_ref, g_out, d_out, m_out, v_out = refs[npar:]
        if split is None:
            g = total(refs[0])
        else:
            g = jnp.where(_mesh_pos()[3] < split, total(refs[0]), total(refs[1]))
        mn = ADAM_B1 * m_ref[...] + (1.0 - ADAM_B1) * g
        vn = ADAM_B2 * v_ref[...] + (1.0 - ADAM_B2) * (g * g)
        m_hat = mn / (1.0 - ADAM_B1 ** ADAM_STEP)
        v_hat = vn / (1.0 - ADAM_B2 ** ADAM_STEP)
        g_out[...] = g
        d_out[...] = -ADAM_LR * (m_hat / (jnp.sqrt(v_hat) + ADAM_EPS) + ADAM_WD * w_ref[...])
        m_out[...] = mn
        v_out[...] = vn

    blk = pl.BlockSpec((tr, C), lambda i: (i, 0))
    shp = jax.ShapeDtypeStruct((R, C), F32)
    return pl.pallas_call(
        body, name=name, grid=(R // tr,),
        in_specs=[pl.BlockSpec((N_DEV, tr, C), lambda i: (0, i, 0))] * npar + [blk, blk, blk],
        out_specs=(blk, blk, blk, blk), out_shape=(shp, shp, shp, shp),
        compiler_params=_params(("parallel",)),
    )(*parts, w, m, v)


def _local_step(x, target, norm_g, w_in, conv_w, conv_b, ln_g, ln_b, w_out, gf, early=None, late=None):
    n_qkv_blocks = (ATT_W + 2 * KV_W) // 512
    q, h = _inproj(x, norm_g, w_in, 0, ATT_W, BF16, HEAD_DIM ** -0.5, True, "inproj_q")
    kv = _inproj(x, norm_g, w_in, ATT_W // 512, 2 * KV_W, BF16, 1.0, False, "inproj_kv")
    gates = _inproj(x, norm_g, w_in, n_qkv_blocks, w_in.shape[1] - n_qkv_blocks * 512, F32, 1.0, False,
                    "inproj_gates")

    pats = [_attn_fwd(q, kv, dil, "attn_fwd_d%d" % dil) for _, dil in PATTERNS]
    o, lse, y_att = _attn_combine([p[0] for p in pats], [p[1] for p in pats], gates)
    conv_out, y_conv = _conv_fwd(gates, conv_w, conv_b, ln_g, ln_b)
    dx2, dxb, loss_cols, g_gf = _outproj_loss(x, y_att, y_conv, w_out, gf, target)

    d_o, d_a_gate, delta = _dy_att(dxb, w_out, gates, o)
    d_c_gate, d_conv, g_ln_g, g_ln_b, g_conv_b = _dy_conv(dxb, w_out, gates, conv_out, ln_g, ln_b)
    dgates, g_conv_w = _conv_bwd(d_conv, gates, d_a_gate, d_c_gate, conv_w)
    g_gates = _tn_matmul(h, dgates, "gw_in_gates")
    g_w_out = jnp.concatenate([_tn_matmul(y_att, dxb, "gw_out_att"), _tn_matmul(y_conv, dxb, "gw_out_conv")],
                              axis=0)
    acc, early_out = None, []
    for idx, (_, dil) in enumerate(PATTERNS):
        hosted = early(g_gates, g_w_out, g_conv_w) if (idx == 0 and early is not None) else None
        acc, outs = _attn_bwd(q, kv, d_o, lse, delta, dil, acc, idx == len(PATTERNS) - 1, "attn_bwd_d%d" % dil,
                              hosted)
        early_out += outs
    dq, dkv = acc
    g_q, g_kv = _tn_matmul(h, dq, "gw_in_q"), _tn_matmul(h, dkv, "gw_in_kv")
    grad_x, g_norm_g, late_out = _dh(dq, dkv, dgates, w_in, x, dx2, norm_g,
                                     late(g_q, g_kv, g_gates) if late is not None else None)
    small = (g_norm_g, g_conv_b, g_ln_g, g_ln_b, g_gf, loss_cols)
    return grad_x, (g_q, g_kv, g_gates), g_w_out, g_conv_w, small, early_out, late_out


def kernel(x, norm_g, w_in, conv_w, conv_b, conv_ln_g, conv_ln_b, w_out, final_norm_g, loss_target, m_norm_g, m_w_in, m_conv_w, m_conv_b, m_conv_ln_g, m_conv_ln_b, m_w_out, m_final_norm_g, v_norm_g, v_w_in, v_conv_w, v_conv_b, v_conv_ln_g, v_conv_ln_b, v_w_out, v_final_norm_g):
    S, D = x.shape[1], x.shape[2]
    win_sh, wout_sh, cw_sh = w_in[0], w_out[0], conv_w[0]
    cols_sh, rows_sh, ch_sh = win_sh.shape[1], wout_sh.shape[0], cw_sh.shape[1]

    win_all, wout_all, cw_all = _gather_two_level(
        [win_sh.astype(BF16), wout_sh.astype(BF16), cw_sh], "gather_weights")
    w_in_full = win_all.transpose(1, 0, 2).reshape(D, N_DEV * cols_sh)
    w_out_full = wout_all.reshape(N_DEV * rows_sh, D)
    conv_w_full = cw_all.transpose(1, 0, 2).reshape(CONV_K, N_DEV * ch_sh)
    conv_w_full = jnp.pad(conv_w_full, ((0, CONV_HALO - CONV_K), (0, 0)))
    gf = final_norm_g.reshape(1, D)

    att_cols = ATT_W + 2 * KV_W
    first = -(-att_cols // cols_sh)
    gate_off = first * cols_sh - att_cols

    def pieces(g, n):
        return g.reshape(D, n, cols_sh).transpose(1, 0, 2).astype(BF16)

    def early(g_gates, g_w_out, g_conv_w):
        return _Exchange(
            [pieces(g_gates[:, gate_off:], N_DEV - first), g_w_out.reshape(N_DEV, rows_sh, D).astype(BF16),
             g_conv_w[:CONV_K].reshape(CONV_K, N_DEV, ch_sh).transpose(1, 0, 2)],
            [(first, N_DEV), (0, N_DEV), (0, N_DEV)])

    def late(g_q, g_kv, g_gates):
        return _Exchange([pieces(jnp.concatenate([g_q, g_kv, g_gates[:, :gate_off]], axis=1), first)], [(0, first)])

    grad_x, _, _, _, small, early_out, late_out = _local_step(
        x[0], loss_target[0], norm_g, w_in_full, conv_w_full, conv_b, conv_ln_g, conv_ln_b, w_out_full, gf,
        early, late)
    win_parts_hi, wout_parts, cw_parts = early_out
    win_parts_lo, = late_out

    small_pack = jnp.concatenate(list(small) + [jnp.zeros((2, D), F32)], axis=0)
    small_parts, = _exchange([small_pack], [None], "gather_small")

    upd_win = _adamw((win_parts_lo, win_parts_hi), win_sh, m_w_in[0], v_w_in[0], "adamw_w_in", tr=256, split=first)
    upd_wout = _adamw(wout_parts, wout_sh, m_w_out[0], v_w_out[0], "adamw_w_out", tr=128)
    upd_cw = _adamw(cw_parts, cw_sh, m_conv_w[0], v_conv_w[0], "adamw_conv_w")
    zeros3 = jnp.zeros((3, D), F32)
    stack = lambda a, b, c, d_, e: jnp.concatenate([a, b, c, d_, e.reshape(1, D), zeros3], axis=0)
    upd_small = _adamw(
        small_parts,
        stack(norm_g, conv_b, conv_ln_g, conv_ln_b, final_norm_g),
        stack(m_norm_g, m_conv_b, m_conv_ln_g, m_conv_ln_b, m_final_norm_g),
        stack(v_norm_g, v_conv_b, v_conv_ln_g, v_conv_ln_b, v_final_norm_g) + jnp.concatenate(
            [jnp.zeros((5, D), F32), jnp.ones((3, D), F32)], axis=0),
        "adamw_small")

    loss = 0.5 / D * jnp.sum(upd_small[0][5])

    def outputs(kind):
        sm = upd_small[kind]
        return [sm[0:1], upd_win[kind][None], upd_cw[kind][None], sm[1:2], sm[2:3], sm[3:4],
                upd_wout[kind][None], sm[4]]

    return (loss, grad_x[None], *outputs(0), *outputs(1), *outputs(2), *outputs(3))
```

```python
import functools

import jax
import jax.numpy as jnp
from jax import lax
from jax.experimental import pallas as pl
from jax.experimental.pallas import tpu as pltpu

F32 = jnp.float32
BF16 = jnp.bfloat16

HEAD_DIM = 64
N_KV_HEADS = 4
N_Q_HEADS = 16
ATT_W = 1024
KV_W = 256
CONV_K = 31
CONV_HALO = 32
PATTERNS = ((128, 1), (512, 4), (2048, 16))
BLK = 128
LANES = 128
NORM_EPS = 1e-6
LN_EPS = 1e-5
NEG = -1e30
N_DEV = 8
ADAM_LR, ADAM_B1, ADAM_B2, ADAM_EPS, ADAM_WD, ADAM_STEP = 0.001, 0.9, 0.999, 1e-08, 0.01, 10
VMEM_LIMIT = 48 * 1024 * 1024
SLOPES = tuple(2.0 ** (-8.0 * (h + 1) / N_Q_HEADS) for h in range(N_Q_HEADS))
MESH = pl.DeviceIdType.MESH


def _params(sem):
    return pltpu.CompilerParams(dimension_semantics=sem, vmem_limit_bytes=VMEM_LIMIT)


def _sigmoid(v):
    return 1.0 / (1.0 + jnp.exp(-v))


def _silu_and_grad(v):
    s = _sigmoid(v)
    return v * s, s * (1.0 + v * (1.0 - s))


ANY_SPEC = pl.BlockSpec(memory_space=pl.ANY)


def _mesh_pos():
    x, y, c = lax.axis_index("x"), lax.axis_index("y"), lax.axis_index("c")
    return x, y, c, 4 * x + 2 * y + c


def _flipped(k, x, y, c):
    px = 1 - x if k & 4 else x
    py = 1 - y if k & 2 else y
    pc = 1 - c if k & 1 else c
    return (px, py, pc), 4 * px + 2 * py + pc


class _Exchange:
    def __init__(self, arrays, dests):
        self.arrays, self.dests, self.n = list(arrays), list(dests), len(arrays)

    def out_shapes(self):
        return [jax.ShapeDtypeStruct((N_DEV,) + a.shape[-2:], a.dtype) for a in self.arrays]

    def sem_shapes(self):
        return [pltpu.SemaphoreType.DMA((self.n, N_DEV - 1)), pltpu.SemaphoreType.DMA((self.n, N_DEV - 1)),
                pltpu.SemaphoreType.DMA((self.n,))]

    def _when(self, a, dev, fn):
        if self.dests[a] is None:
            fn()
        else:
            lo, hi = self.dests[a]
            pl.when((dev >= lo) & (dev < hi))(fn)

    def _mine(self, ins, a, dev):
        return ins[a] if self.dests[a] is None else ins[a].at[dev - self.dests[a][0]]

    def _copy(self, ins, outs, sems, a, k, src_dev, slot, target):
        return pltpu.make_async_remote_copy(
            src_ref=self._mine(ins, a, src_dev), dst_ref=outs[a].at[slot],
            send_sem=sems[0].at[a, k - 1], recv_sem=sems[1].at[a, k - 1],
            device_id=target, device_id_type=MESH)

    def start(self, ins, outs, sems):
        x, y, c, me = _mesh_pos()
        for a in range(self.n):
            self._when(a, me, lambda a=a: pltpu.make_async_copy(
                self._mine(ins, a, me), outs[a].at[me], sems[2].at[a]).start())
            for k in range(1, N_DEV):
                target, peer = _flipped(k, x, y, c)
                self._when(a, peer, lambda a=a, k=k, target=target, peer=peer: self._copy(
                    ins, outs, sems, a, k, peer, me, target).start())

    def finish(self, ins, outs, sems):
        x, y, c, me = _mesh_pos()
        lo0 = [0 if d is None else d[0] for d in self.dests]
        for a in range(self.n):
            for k in range(1, N_DEV):
                target, peer = _flipped(k, x, y, c)
                self._when(a, me, lambda a=a, k=k, peer=peer: self._copy(
                    ins, outs, sems, a, k, lo0[a], peer, (x, y, c)).wait_recv())
            for k in range(1, N_DEV):
                target, peer = _flipped(k, x, y, c)
                self._when(a, peer, lambda a=a, k=k, target=target, peer=peer: self._copy(
                    ins, outs, sems, a, k, peer, me, target).wait_send())
            self._when(a, me, lambda a=a: pltpu.make_async_copy(
                self._mine(ins, a, me), outs[a].at[me], sems[2].at[a]).wait())


def _exchange(arrays, dests, name):
    ex = _Exchange(arrays, dests)
    na = ex.n

    def body(*refs):
        ins, outs, sems = refs[:na], refs[na:2 * na], refs[2 * na:]
        ex.start(ins, outs, sems)
        ex.finish(ins, outs, sems)

    return pl.pallas_call(
        body, name=name, out_shape=tuple(ex.out_shapes()),
        in_specs=[ANY_SPEC] * na, out_specs=tuple([ANY_SPEC] * na), scratch_shapes=ex.sem_shapes(),
    )(*arrays)


def _gather_two_level(arrays, name):
    na = len(arrays)

    def body(*refs):
        ins, outs = refs[:na], refs[na:2 * na]
        send_sems, recv_sems, loc_sems = refs[2 * na:]
        x, y, c, me = _mesh_pos()
        sibling = (x, y, 1 - c)
        chips = [(1 - x, y), (x, 1 - y), (1 - x, 1 - y)]

        def slot(px, py, pc):
            return 4 * px + 2 * py + pc

        def copy(a, k, src, block, to):
            return pltpu.make_async_remote_copy(
                src_ref=src, dst_ref=outs[a].at[slot(*block)], send_sem=send_sems.at[a, k], recv_sem=recv_sems.at[a, k],
                device_id=to, device_id_type=MESH)

        local = [pltpu.make_async_copy(ins[a], outs[a].at[me], loc_sems.at[a]) for a in range(na)]
        for cp in local:
            cp.start()
        started = []
        for a in range(na):
            started.append(copy(a, 0, ins[a], (x, y, c), sibling))
            started += [copy(a, 1 + j, ins[a], (x, y, c), (*chip, c)) for j, chip in enumerate(chips)]
        for cp in started:
            cp.start()
        for j, chip in enumerate(chips):
            for a in range(na):
                copy(a, 1 + j, ins[a], (*chip, c), (x, y, c)).wait_recv()
                fwd = copy(a, 4 + j, outs[a].at[slot(*chip, c)], (*chip, c), sibling)
                fwd.start()
                started.append(fwd)
        for a in range(na):
            copy(a, 0, ins[a], sibling, (x, y, c)).wait_recv()
            for j, chip in enumerate(chips):
                copy(a, 4 + j, ins[a], (*chip, 1 - c), (x, y, c)).wait_recv()
        for cp in started:
            cp.wait_send()
        for cp in local:
            cp.wait()

    return pl.pallas_call(
        body, name=name,
        out_shape=tuple(jax.ShapeDtypeStruct((N_DEV,) + a.shape, a.dtype) for a in arrays),
        in_specs=[ANY_SPEC] * na, out_specs=tuple([ANY_SPEC] * na),
        scratch_shapes=[pltpu.SemaphoreType.DMA((na, N_DEV - 1)), pltpu.SemaphoreType.DMA((na, N_DEV - 1)),
                        pltpu.SemaphoreType.DMA((na,))],
    )(*arrays)


CHUNK = 128
RESIDUES = 16
PER_RES = CHUNK // RESIDUES


def _perm_rows(tile, inverse):
    a = lax.broadcasted_iota(jnp.int32, (CHUNK, CHUNK), 0)
    b = lax.broadcasted_iota(jnp.int32, (CHUNK, CHUNK), 1)
    if inverse:
        a, b = b, a
    p = jnp.where(a == PER_RES * (b % RESIDUES) + b // RESIDUES, 1.0, 0.0).astype(BF16)
    parts = [jnp.dot(p, tile[c * CHUNK:(c + 1) * CHUNK], preferred_element_type=F32)
             for c in range(tile.shape[0] // CHUNK)]
    return jnp.concatenate(parts, axis=0).astype(BF16)


class _Rows:
    def __init__(self, dil, S):
        nc = S // CHUNK
        self.dil = dil
        if dil == 1:
            self.view, self.block, self.nb = (nc, CHUNK), (None, CHUNK), nc
            self.index = lambda r, b: (b, 0, 0)
        elif dil == 4:
            self.view, self.block, self.nb = (nc, 4, 4, PER_RES), (4, 4, None, PER_RES), nc // 4
            self.index = lambda r, b: (b, 0, r, 0, 0)
        elif dil == RESIDUES:
            self.view, self.block, self.nb = (nc, RESIDUES, PER_RES), (RESIDUES, None, PER_RES), nc // RESIDUES
            self.index = lambda r, b: (b, r, 0, 0)
        else:
            raise NotImplementedError(dil)

    def of(self, a):
        return a.reshape(self.view + (a.shape[-1],))

    def spec(self, width, which_block):
        return pl.BlockSpec(self.block + (width,), lambda r, n: self.index(r, which_block(n)))

    def pos(self, row):
        if self.dil == 1:
            return (row % PER_RES) * RESIDUES + row // PER_RES
        if self.dil == 4:
            return (row // 32) * 32 + (row % PER_RES) * 4 + (row % 32) // PER_RES
        return row


def _ld(ref, cols=slice(None)):
    v = ref[(slice(None),) * (len(ref.shape) - 1) + (cols,)]
    return v.reshape(BLK, v.shape[-1])


def _st(ref, val, cols=slice(None)):
    ref[(slice(None),) * (len(ref.shape) - 1) + (cols,)] = val.reshape(ref.shape[:-1] + (val.shape[-1],))


def _inproj(x, g, w, col_block, ncols, out_dtype, scale, emit_h, perm, name, tm=1024, tn=512):
    S, D = x.shape

    def body(x_ref, g_ref, w_ref, *rest):
        if emit_h:
            o_ref, h_out, h_scr = rest
        else:
            o_ref, h_scr = rest

        @pl.when(pl.program_id(1) == 0)
        def _():
            xf = x_ref[...]
            r = lax.rsqrt(jnp.mean(xf * xf, axis=-1, keepdims=True) + NORM_EPS)
            h = (xf * r * g_ref[...]).astype(BF16)
            if perm:
                h = _perm_rows(h, False)
            h_scr[...] = h
            if emit_h:
                h_out[...] = h

        acc = jnp.dot(h_scr[...], w_ref[...], preferred_element_type=F32)
        if scale != 1.0:
            acc = acc * scale
        o_ref[...] = acc.astype(out_dtype)

    out_shape = [jax.ShapeDtypeStruct((S, ncols), out_dtype)]
    out_specs = [pl.BlockSpec((tm, tn), lambda i, j: (i, j))]
    if emit_h:
        out_shape.append(jax.ShapeDtypeStruct((S, D), BF16))
        out_specs.append(pl.BlockSpec((tm, D), lambda i, j: (i, 0)))
    res = pl.pallas_call(
        body, name=name, grid=(S // tm, ncols // tn),
        in_specs=[pl.BlockSpec((tm, D), lambda i, j: (i, 0)),
                  pl.BlockSpec((1, D), lambda i, j: (0, 0)),
                  pl.BlockSpec((D, tn), lambda i, j: (0, col_block + j))],
        out_specs=tuple(out_specs), out_shape=tuple(out_shape),
        scratch_shapes=[pltpu.VMEM((tm, D), BF16)],
        compiler_params=_params(("parallel", "arbitrary")),
    )(x, g, w)
    return res if emit_h else res[0]


def _window_mask(n, rows):
    qi = lax.broadcasted_iota(jnp.int32, (BLK, 2 * BLK), 0)
    kj = lax.broadcasted_iota(jnp.int32, (BLK, 2 * BLK), 1)
    dist = rows.pos(qi) - rows.pos(kj % BLK) + jnp.where(kj < BLK, BLK, 0)
    valid = (dist >= 0) & (dist <= BLK) & ((kj >= BLK) | (n > 0))
    negd = (dist * (-rows.dil)).astype(F32)
    return valid, negd


def _head_operands(kv2, hk, lo_mask):
    half, pos = hk // 2, hk % 2
    out = []
    for base in (0, KV_W):
        t = kv2[:, base + half * LANES: base + (half + 1) * LANES]
        sw = pltpu.roll(t, HEAD_DIM, axis=1)
        at_lo, at_hi = (t, sw) if pos == 0 else (sw, t)
        out.append(jnp.where(lo_mask, at_lo, 0.0).astype(BF16))
        out.append(jnp.where(lo_mask, 0.0, at_hi).astype(BF16))
    return out


def _nt(a, b):
    return lax.dot_general(a, b, (((1,), (1,)), ((), ())), preferred_element_type=F32)


def _tn(a, b):
    return lax.dot_general(a, b, (((0,), (0,)), ((), ())), preferred_element_type=F32)


def _attn_fwd(q, kv, dil, name):
    S = q.shape[0]
    rows = _Rows(dil, S)
    nb = rows.nb

    def body(q_ref, kvc_ref, kvp_ref, o_ref, lse_ref):
        n = pl.program_id(1)
        valid, negd = _window_mask(n, rows)
        valid2 = jnp.concatenate([valid, valid], axis=0)
        negd2 = jnp.concatenate([negd, negd], axis=0)
        kv2 = jnp.concatenate([_ld(kvp_ref), _ld(kvc_ref)], axis=0)
        lo_mask = lax.broadcasted_iota(jnp.int32, (2 * BLK, LANES), 1) < HEAD_DIM
        top = lax.broadcasted_iota(jnp.int32, (2 * BLK, 1), 0) < BLK
        lane = lax.broadcasted_iota(jnp.int32, (BLK, LANES), 1)
        stats = jnp.zeros((BLK, LANES), F32)
        for hk in range(N_KV_HEADS):
            k_lo, k_hi, v_lo, v_hi = _head_operands(kv2, hk, lo_mask)
            cols = [slice(b * LANES, (b + 1) * LANES) for b in (2 * hk, 2 * hk + 1)]
            q2 = jnp.concatenate([_ld(q_ref, cols[0]), _ld(q_ref, cols[1])], axis=0).astype(BF16)
            o2 = jnp.zeros((2 * BLK, LANES), F32)
            for which, (kk, vv) in enumerate(((k_lo, v_lo), (k_hi, v_hi))):
                h0, h1 = 4 * hk + which, 4 * hk + 2 + which
                s = _nt(q2, kk) + jnp.where(top, SLOPES[h0], SLOPES[h1]) * negd2
                s = jnp.where(valid2, s, NEG)
                m = jnp.max(s, axis=1, keepdims=True)
                p = jnp.exp(s - m)
                l = jnp.sum(p, axis=1, keepdims=True)
                o2 = o2 + jnp.dot(p.astype(BF16), vv, preferred_element_type=F32) * (1.0 / l)
                lse = m + jnp.log(l)
                stats = jnp.where(lane == h0, lse[0:BLK], stats)
                stats = jnp.where(lane == h1, lse[BLK:], stats)
            _st(o_ref, o2[0:BLK], cols[0])
            _st(o_ref, o2[BLK:], cols[1])
        _st(lse_ref, stats)

    here = lambda n: n
    before = lambda n: jnp.maximum(n - 1, 0)
    o, lse = pl.pallas_call(
        body, name=name, grid=(dil, nb),
        in_specs=[rows.spec(ATT_W, here), rows.spec(2 * KV_W, here), rows.spec(2 * KV_W, before)],
        out_specs=(rows.spec(ATT_W, here), rows.spec(LANES, here)),
        out_shape=(jax.ShapeDtypeStruct(rows.view + (ATT_W,), F32),
                   jax.ShapeDtypeStruct(rows.view + (LANES,), F32)),
        compiler_params=_params(("parallel", "parallel")),
    )(rows.of(q), rows.of(kv), rows.of(kv))
    return o.reshape(S, ATT_W), lse.reshape(S, LANES)


def _attn_combine(os_, lses, gates, tm=256):
    S = os_[0].shape[0]
    npat = len(os_)

    def body(*refs):
        o_refs = refs[:npat]
        l_refs = refs[npat:2 * npat]
        gate_ref, o_out, lse_out, y_out = refs[2 * npat:]
        ls = [r[...] for r in l_refs]
        mx = functools.reduce(jnp.maximum, ls)
        es = [jnp.exp(l - mx) for l in ls]
        tot = functools.reduce(jnp.add, es)
        inv = 1.0 / tot
        ws = [e * inv for e in es]
        lse_out[...] = mx + jnp.log(tot)
        lo = lax.broadcasted_iota(jnp.int32, (tm, LANES), 1) < HEAD_DIM
        for blk in range(ATT_W // LANES):
            cols = slice(blk * LANES, (blk + 1) * LANES)
            acc = jnp.zeros((tm, LANES), F32)
            for p in range(npat):
                sc = jnp.where(lo, ws[p][:, 2 * blk:2 * blk + 1], ws[p][:, 2 * blk + 1:2 * blk + 2])
                acc = acc + sc * o_refs[p][:, cols]
            o_out[:, cols] = acc
            a = gate_ref[:, cols]
            y_out[:, cols] = (acc * (a * _sigmoid(a))).astype(BF16)

    row = lambda w: pl.BlockSpec((tm, w), lambda i: (i, 0))
    return pl.pallas_call(
        body, name="attn_combine", grid=(S // tm,),
        in_specs=[row(ATT_W)] * npat + [row(LANES)] * npat + [row(ATT_W)],
        out_specs=(row(ATT_W), row(LANES), row(ATT_W)),
        out_shape=(jax.ShapeDtypeStruct((S, ATT_W), F32), jax.ShapeDtypeStruct((S, LANES), F32),
                   jax.ShapeDtypeStruct((S, ATT_W), BF16)),
        compiler_params=_params(("parallel",)),
    )(*os_, *lses, gates)


def _shifted_copies(buf, phases):
    n = phases.shape[1]
    for b in range(1, 8):
        phases[b - 1] = buf[b:b + n, :]


def _window(buf, phases, start, cols):
    b = start % 8
    if b == 0:
        return buf[start:start + 8, cols]
    return phases[b - 1, start - b:start - b + 8, cols]


def _conv_fwd(gates, conv_w, conv_b, ln_g, ln_b, tt=128):
    S = gates.shape[0]
    C = conv_w.shape[1]
    hb = tt // CONV_HALO

    def body(val_ref, glu_ref, hval_ref, hglu_ref, gate_ref, w_ref, b_ref, g_ref, beta_ref,
             conv_ref, y_ref, hbuf, hph):
        i = pl.program_id(0)
        halo = hval_ref[...] * _sigmoid(hglu_ref[...])
        hbuf[0:CONV_HALO, :] = jnp.where(i > 0, halo, 0.0)
        hbuf[CONV_HALO:, :] = val_ref[...] * _sigmoid(glu_ref[...])
        _shifted_copies(hbuf, hph)
        for cb in range(C // LANES):
            cols = slice(cb * LANES, (cb + 1) * LANES)
            wj = [jnp.broadcast_to(w_ref[j:j + 1, cols], (8, LANES)) for j in range(CONV_K)]
            for rc in range(tt // 8):
                acc = jnp.zeros((8, LANES), F32)
                for j in range(CONV_K):
                    start = rc * 8 + CONV_HALO - (CONV_K - 1) + j
                    acc = acc + _window(hbuf, hph, start, cols) * wj[j]
                conv_ref[rc * 8:(rc + 1) * 8, cols] = acc
        cv = conv_ref[...] + b_ref[...]
        conv_ref[...] = cv
        mu = jnp.mean(cv, axis=-1, keepdims=True)
        xc = cv - mu
        var = jnp.mean(xc * xc, axis=-1, keepdims=True)
        ln = xc * lax.rsqrt(var + LN_EPS) * g_ref[...] + beta_ref[...]
        gt = gate_ref[...]
        y_ref[...] = (ln * _sigmoid(ln) * (gt * _sigmoid(gt))).astype(BF16)

    vec = pl.BlockSpec((1, C), lambda i: (0, 0))
    return pl.pallas_call(
        body, name="conv_fwd", grid=(S // tt,),
        in_specs=[pl.BlockSpec((tt, C), lambda i: (i, 0)),
                  pl.BlockSpec((tt, C), lambda i: (i, 1)),
                  pl.BlockSpec((CONV_HALO, C), lambda i: (jnp.maximum(i * hb - 1, 0), 0)),
                  pl.BlockSpec((CONV_HALO, C), lambda i: (jnp.maximum(i * hb - 1, 0), 1)),
                  pl.BlockSpec((tt, C), lambda i: (i, 2)),
                  pl.BlockSpec((CONV_HALO, C), lambda i: (0, 0)), vec, vec, vec],
        out_specs=(pl.BlockSpec((tt, C), lambda i: (i, 0)), pl.BlockSpec((tt, C), lambda i: (i, 0))),
        out_shape=(jax.ShapeDtypeStruct((S, C), F32), jax.ShapeDtypeStruct((S, C), BF16)),
        scratch_shapes=[pltpu.VMEM((tt + CONV_HALO, C), F32), pltpu.VMEM((7, tt + CONV_HALO - 8, C), F32)],
        compiler_params=_params(("parallel",)),
    )(gates, gates, gates, gates, gates, conv_w, conv_b, ln_g, ln_b)


def _outproj_loss(x, y_att, y_conv, w_out, gf, target, tm=512):
    S, D = x.shape
    E = y_att.shape[1]

    def body(x_ref, ya_ref, yc_ref, w_ref, gf_ref, t_ref, dx_ref, dxb_ref, loss_ref, ggf_ref):
        @pl.when(pl.program_id(0) == 0)
        def _():
            loss_ref[...] = jnp.zeros_like(loss_ref)
            ggf_ref[...] = jnp.zeros_like(ggf_ref)

        x2 = (x_ref[...] + jnp.dot(_perm_rows(ya_ref[...], True), w_ref[0:E, :], preferred_element_type=F32)
              + jnp.dot(yc_ref[...], w_ref[E:, :], preferred_element_type=F32))
        r = lax.rsqrt(jnp.mean(x2 * x2, axis=-1, keepdims=True) + NORM_EPS)
        nrm = x2 * r
        gfv = gf_ref[...]
        err = nrm * gfv - t_ref[...]
        loss_ref[...] += jnp.sum(err * err, axis=0, keepdims=True)
        dout = err * (1.0 / D)
        ggf_ref[...] += jnp.sum(dout * nrm, axis=0, keepdims=True)
        dn = dout * gfv
        dx2 = r * (dn - nrm * jnp.mean(dn * nrm, axis=-1, keepdims=True))
        dx_ref[...] = dx2
        dxb_ref[...] = dx2.astype(BF16)

    row = lambda w: pl.BlockSpec((tm, w), lambda i: (i, 0))
    vec = pl.BlockSpec((1, D), lambda i: (0, 0))
    return pl.pallas_call(
        body, name="outproj_loss", grid=(S // tm,),
        in_specs=[row(D), row(E), row(E), pl.BlockSpec((2 * E, D), lambda i: (0, 0)), vec, row(D)],
        out_specs=(row(D), row(D), vec, vec),
        out_shape=(jax.ShapeDtypeStruct((S, D), F32), jax.ShapeDtypeStruct((S, D), BF16),
                   jax.ShapeDtypeStruct((1, D), F32), jax.ShapeDtypeStruct((1, D), F32)),
        compiler_params=_params(("arbitrary",)),
    )(x, y_att, y_conv, w_out, gf, target)


def _split3(v):
    hi = v.astype(BF16)
    r1 = v - hi.astype(F32)
    mid = r1.astype(BF16)
    lo = (r1 - mid.astype(F32)).astype(BF16)
    return hi, mid, lo


def _dy_att(dxb, w_out, gates, o, tm=512):
    S, D = dxb.shape
    E = ATT_W

    def body(dx_ref, w_ref, a_ref, o_ref, do_ref, da_ref, dl_ref, dxr_ref):
        dxr = _perm_rows(dx_ref[...], False)
        dxr_ref[...] = dxr
        dya = _nt(dxr, w_ref[...])
        a = a_ref[...]
        ov = o_ref[...]
        sl, dsl = _silu_and_grad(a)
        d_o = dya * sl
        do_ref[...] = d_o
        da_ref[...] = (dya * ov * dsl).astype(BF16)
        ci = lax.broadcasted_iota(jnp.int32, (E, LANES), 0) // HEAD_DIM
        hi = lax.broadcasted_iota(jnp.int32, (E, LANES), 1)
        sel = jnp.where(ci == hi, 1.0, 0.0).astype(BF16)
        acc = jnp.zeros((tm, LANES), F32)
        for part in _split3(d_o * ov):
            acc = acc + jnp.dot(part, sel, preferred_element_type=F32)
        dl_ref[...] = acc

    row = lambda w: pl.BlockSpec((tm, w), lambda i: (i, 0))
    return pl.pallas_call(
        body, name="dy_att", grid=(S // tm,),
        in_specs=[row(D), pl.BlockSpec((E, D), lambda i: (0, 0)), row(E), row(E)],
        out_specs=(row(E), row(E), row(LANES), row(D)),
        out_shape=(jax.ShapeDtypeStruct((S, E), F32), jax.ShapeDtypeStruct((S, E), BF16),
                   jax.ShapeDtypeStruct((S, LANES), F32), jax.ShapeDtypeStruct((S, D), BF16)),
        compiler_params=_params(("parallel",)),
    )(dxb, w_out, gates, o)


def _dy_conv(dxb, w_out, gates, conv_out, ln_g, ln_b, tm=512):
    S, D = dxb.shape
    C = conv_out.shape[1]

    def body(dx_ref, w_ref, gate_ref, cv_ref, g_ref, beta_ref, dgate_ref, dconv_ref, gg_ref, gb_ref, gcb_ref):
        @pl.when(pl.program_id(0) == 0)
        def _():
            gg_ref[...] = jnp.zeros_like(gg_ref)
            gb_ref[...] = jnp.zeros_like(gb_ref)
            gcb_ref[...] = jnp.zeros_like(gcb_ref)

        dyc = _nt(dx_ref[...], w_ref[...])
        cv = cv_ref[...]
        mu = jnp.mean(cv, axis=-1, keepdims=True)
        xc = cv - mu
        rstd = lax.rsqrt(jnp.mean(xc * xc, axis=-1, keepdims=True) + LN_EPS)
        nrm = xc * rstd
        gv = g_ref[...]
        ln = nrm * gv + beta_ref[...]
        u, du = _silu_and_grad(ln)
        gt = gate_ref[...]
        g2, dg2 = _silu_and_grad(gt)
        dgate_ref[...] = (dyc * u * dg2).astype(BF16)
        d_ln = dyc * g2 * du
        gb_ref[...] += jnp.sum(d_ln, axis=0, keepdims=True)
        gg_ref[...] += jnp.sum(d_ln * nrm, axis=0, keepdims=True)
        dn = d_ln * gv
        d_conv = rstd * (dn - jnp.mean(dn, axis=-1, keepdims=True)
                         - nrm * jnp.mean(dn * nrm, axis=-1, keepdims=True))
        dconv_ref[...] = d_conv
        gcb_ref[...] += jnp.sum(d_conv, axis=0, keepdims=True)

    row = lambda w: pl.BlockSpec((tm, w), lambda i: (i, 0))
    vec = pl.BlockSpec((1, C), lambda i: (0, 0))
    return pl.pallas_call(
        body, name="dy_conv", grid=(S // tm,),
        in_specs=[row(D), pl.BlockSpec((C, D), lambda i: (1, 0)),
                  pl.BlockSpec((tm, C), lambda i: (i, 2)), row(C), vec, vec],
        out_specs=(row(C), row(C), vec, vec, vec),
        out_shape=(jax.ShapeDtypeStruct((S, C), BF16), jax.ShapeDtypeStruct((S, C), F32),
                   jax.ShapeDtypeStruct((1, C), F32), jax.ShapeDtypeStruct((1, C), F32),
                   jax.ShapeDtypeStruct((1, C), F32)),
        compiler_params=_params(("arbitrary",)),
    )(dxb, w_out, gates, conv_out, ln_g, ln_b)


def _conv_bwd(d_conv, gates, d_c_gate, conv_w, hosted=None, tt=128):
    S, C = d_conv.shape
    hb = tt // CONV_HALO
    nt = S // tt
    hn = hosted.n if hosted is not None else 0

    def body(*refs):
        dc_ref, dnext_ref, val_ref, glu_ref, hval_ref, hglu_ref, dg_ref, w_ref = refs[:8]
        h_ins = refs[8:8 + hn]
        out_ref, gw_ref = refs[8 + hn:10 + hn]
        h_outs = refs[10 + hn:10 + 2 * hn]
        hbuf, dbuf, dhbuf, hph, dph = refs[10 + 2 * hn:15 + 2 * hn]
        h_sems = refs[15 + 2 * hn:]
        i = pl.program_id(0)

        @pl.when(i == 0)
        def _():
            gw_ref[...] = jnp.zeros_like(gw_ref)
            if hosted is not None:
                hosted.start(h_ins, h_outs, h_sems)

        val = val_ref[...]
        sg = _sigmoid(glu_ref[...])
        halo = hval_ref[...] * _sigmoid(hglu_ref[...])
        hbuf[0:CONV_HALO, :] = jnp.where(i > 0, halo, 0.0)
        hbuf[CONV_HALO:, :] = val * sg
        dbuf[0:tt, :] = dc_ref[...]
        dbuf[tt:, :] = jnp.where(i < nt - 1, dnext_ref[...], 0.0)
        _shifted_copies(hbuf, hph)
        _shifted_copies(dbuf, dph)
        for cb in range(C // LANES):
            cols = slice(cb * LANES, (cb + 1) * LANES)
            wj = [jnp.broadcast_to(w_ref[j:j + 1, cols], (8, LANES)) for j in range(CONV_K)]
            for rc in range(tt // 8):
                acc = jnp.zeros((8, LANES), F32)
                for j in range(CONV_K):
                    acc = acc + _window(dbuf, dph, rc * 8 + (CONV_K - 1) - j, cols) * wj[j]
                dhbuf[rc * 8:(rc + 1) * 8, cols] = acc
            gacc = [jnp.zeros((8, LANES), F32) for _ in range(CONV_K)]
            for rc in range(tt // 8):
                dcur = dbuf[rc * 8:(rc + 1) * 8, cols]
                for j in range(CONV_K):
                    hs = rc * 8 + CONV_HALO - (CONV_K - 1) + j
                    gacc[j] = gacc[j] + dcur * _window(hbuf, hph, hs, cols)
            for j in range(CONV_K):
                gw_ref[j:j + 1, cols] += jnp.sum(gacc[j], axis=0, keepdims=True)
        d_h = dhbuf[...]
        out_ref[:, 0:C] = (d_h * sg).astype(BF16)
        out_ref[:, C:2 * C] = (d_h * val * sg * (1.0 - sg)).astype(BF16)
        out_ref[:, 2 * C:3 * C] = dg_ref[...]

        if hosted is not None:
            @pl.when(i == nt - 1)
            def _():
                hosted.finish(h_ins, h_outs, h_sems)

    tile = lambda col: pl.BlockSpec((tt, C), lambda i: (i, col))
    in_specs = [tile(0),
                pl.BlockSpec((CONV_HALO, C), lambda i: (jnp.minimum((i + 1) * hb, S // CONV_HALO - 1), 0)),
                tile(0), tile(1),
                pl.BlockSpec((CONV_HALO, C), lambda i: (jnp.maximum(i * hb - 1, 0), 0)),
                pl.BlockSpec((CONV_HALO, C), lambda i: (jnp.maximum(i * hb - 1, 0), 1)),
                tile(0),
                pl.BlockSpec((CONV_HALO, C), lambda i: (0, 0))]
    args = [d_conv, d_conv, gates, gates, gates, gates, d_c_gate, conv_w]
    out_specs = [pl.BlockSpec((tt, 3 * C), lambda i: (i, 0)), pl.BlockSpec((CONV_HALO, C), lambda i: (0, 0))]
    out_shape = [jax.ShapeDtypeStruct((S, 3 * C), BF16), jax.ShapeDtypeStruct((CONV_HALO, C), F32)]
    scratch = [pltpu.VMEM((tt + CONV_HALO, C), F32), pltpu.VMEM((tt + CONV_HALO, C), F32),
               pltpu.VMEM((tt, C), F32),
               pltpu.VMEM((7, tt + CONV_HALO - 8, C), F32), pltpu.VMEM((7, tt + CONV_HALO - 8, C), F32)]
    if hosted is not None:
        in_specs += [ANY_SPEC] * hn
        args += hosted.arrays
        out_specs += [ANY_SPEC] * hn
        out_shape += hosted.out_shapes()
        scratch += hosted.sem_shapes()
    res = pl.pallas_call(
        body, name="conv_bwd", grid=(nt,),
        in_specs=in_specs, out_specs=tuple(out_specs), out_shape=tuple(out_shape), scratch_shapes=scratch,
        compiler_params=_params(("arbitrary",)),
    )(*args)
    return res[0], res[1], list(res[2:])


def _attn_bwd(q, kv, d_o, lse, delta, dil, prev, final, name, hosted=None):
    S = q.shape[0]
    rows = _Rows(dil, S)
    nb = rows.nb
    out_dt = BF16 if final else F32
    have_prev = prev is not None
    hn = hosted.n if hosted is not None else 0

    def body(*refs):
        refs = list(refs)
        q_ref, do_ref, lse_ref, dl_ref, kvc_ref, kvp_ref = refs[:6]
        del refs[:6]
        if have_prev:
            pdq_ref, pdkv_ref = refs[:2]
            del refs[:2]
        h_ins = refs[:hn]
        dq_ref, dkv_ref = refs[hn:hn + 2]
        h_outs = refs[hn + 2:2 * hn + 2]
        carry = refs[2 * hn + 2]
        h_sems = refs[2 * hn + 3:]
        n = pl.program_id(1)
        if hosted is not None:
            @pl.when((pl.program_id(0) == 0) & (n == 0))
            def _():
                hosted.start(h_ins, h_outs, h_sems)

        @pl.when(n == 0)
        def _():
            carry[...] = jnp.zeros_like(carry)

        @pl.when(n < nb)
        def _():
            valid, negd = _window_mask(n, rows)
            valid2 = jnp.concatenate([valid, valid], axis=0)
            negd2 = jnp.concatenate([negd, negd], axis=0)
            kv2 = jnp.concatenate([_ld(kvp_ref), _ld(kvc_ref)], axis=0)
            lse_t, dl_t = _ld(lse_ref), _ld(dl_ref)
            lo_mask = lax.broadcasted_iota(jnp.int32, (2 * BLK, LANES), 1) < HEAD_DIM
            top = lax.broadcasted_iota(jnp.int32, (2 * BLK, 1), 0) < BLK
            halves = [jnp.zeros((2 * BLK, LANES), F32) for _ in range(4)]
            for hk in range(N_KV_HEADS):
                k_lo, k_hi, v_lo, v_hi = _head_operands(kv2, hk, lo_mask)
                cols = [slice(b * LANES, (b + 1) * LANES) for b in (2 * hk, 2 * hk + 1)]
                q2 = jnp.concatenate([_ld(q_ref, cols[0]), _ld(q_ref, cols[1])], axis=0).astype(BF16)
                do2 = jnp.concatenate([_ld(do_ref, cols[0]), _ld(do_ref, cols[1])], axis=0).astype(BF16)
                dq2 = jnp.zeros((2 * BLK, LANES), F32)
                dks, dvs = [], []
                for which, (kk, vv) in enumerate(((k_lo, v_lo), (k_hi, v_hi))):
                    h0, h1 = 4 * hk + which, 4 * hk + 2 + which
                    s = _nt(q2, kk) + jnp.where(top, SLOPES[h0], SLOPES[h1]) * negd2
                    s = jnp.where(valid2, s, NEG)
                    lse2 = jnp.concatenate([lse_t[:, h0:h0 + 1], lse_t[:, h1:h1 + 1]], axis=0)
                    dl2 = jnp.concatenate([dl_t[:, h0:h0 + 1], dl_t[:, h1:h1 + 1]], axis=0)
                    p = jnp.exp(s - lse2)
                    ds = (p * (_nt(do2, vv) - dl2)).astype(BF16)
                    dq2 = dq2 + jnp.dot(ds, kk, preferred_element_type=F32)
                    dks.append(_tn(ds, q2))
                    dvs.append(_tn(p.astype(BF16), do2))
                dk_sum = jnp.where(lo_mask, dks[0], dks[1])
                dv_sum = jnp.where(lo_mask, dvs[0], dvs[1])
                for jp in range(2):
                    dq_blk = dq2[jp * BLK:(jp + 1) * BLK]
                    if have_prev:
                        dq_blk = dq_blk + _ld(pdq_ref, cols[jp])
                    if final:
                        dq_blk = dq_blk * (HEAD_DIM ** -0.5)
                    _st(dq_ref, dq_blk.astype(out_dt), cols[jp])
                half, pos = hk // 2, hk % 2
                here = lo_mask if pos == 0 else jnp.logical_not(lo_mask)
                dk_tot = dk_sum + pltpu.roll(dk_sum, HEAD_DIM, axis=1)
                dv_tot = dv_sum + pltpu.roll(dv_sum, HEAD_DIM, axis=1)
                halves[half] = halves[half] + jnp.where(here, dk_tot, 0.0)
                halves[2 + half] = halves[2 + half] + jnp.where(here, dv_tot, 0.0)
            for b in range(4):
                cols = slice(b * LANES, (b + 1) * LANES)
                done = carry[:, cols] + halves[b][0:BLK, :]
                if have_prev:
                    done = done + _ld(pdkv_ref, cols)
                _st(dkv_ref, done.astype(out_dt), cols)
                carry[:, cols] = halves[b][BLK:, :]

        @pl.when(n == nb)
        def _():
            done = carry[...]
            if have_prev:
                done = done + _ld(pdkv_ref)
            _st(dkv_ref, done.astype(out_dt))

        if hosted is not None:
            @pl.when((pl.program_id(0) == dil - 1) & (n == nb))
            def _():
                hosted.finish(h_ins, h_outs, h_sems)

    cur = lambda n: jnp.minimum(n, nb - 1)
    behind = lambda n: jnp.maximum(n - 1, 0)
    in_specs = [rows.spec(ATT_W, cur), rows.spec(ATT_W, cur), rows.spec(LANES, cur), rows.spec(LANES, cur),
                rows.spec(2 * KV_W, cur), rows.spec(2 * KV_W, behind)]
    args = [rows.of(q), rows.of(d_o), rows.of(lse), rows.of(delta), rows.of(kv), rows.of(kv)]
    if have_prev:
        in_specs += [rows.spec(ATT_W, cur), rows.spec(2 * KV_W, behind)]
        args += [rows.of(prev[0]), rows.of(prev[1])]
    out_specs = [rows.spec(ATT_W, cur), rows.spec(2 * KV_W, behind)]
    out_shape = [jax.ShapeDtypeStruct(rows.view + (ATT_W,), out_dt),
                 jax.ShapeDtypeStruct(rows.view + (2 * KV_W,), out_dt)]
    scratch = [pltpu.VMEM((BLK, 2 * KV_W), F32)]
    if hosted is not None:
        in_specs += [ANY_SPEC] * hn
        args += hosted.arrays
        out_specs += [ANY_SPEC] * hn
        out_shape += hosted.out_shapes()
        scratch += hosted.sem_shapes()
    res = pl.pallas_call(
        body, name=name, grid=(dil, nb + 1),
        in_specs=in_specs, out_specs=tuple(out_specs), out_shape=tuple(out_shape), scratch_shapes=scratch,
        compiler_params=_params(("arbitrary", "arbitrary")),
    )(*args)
    return (res[0].reshape(S, ATT_W), res[1].reshape(S, 2 * KV_W)), list(res[2:])


def _dh(segments, w_in, x, dx2, g, hosted=None, tm=1024, tk=512):
    S, D = x.shape
    ns = len(segments)
    counts = [a.shape[1] // tk for a, _ in segments]
    starts = [sum(counts[:s]) for s in range(ns)]
    nk = sum(counts)
    hn = hosted.n if hosted is not None else 0

    def body(*refs):
        seg_refs = refs[:ns]
        w_ref, x_ref, dx2_ref, g_ref = refs[ns:ns + 4]
        h_ins = refs[ns + 4:ns + 4 + hn]
        gx_ref, gng_ref = refs[ns + 4 + hn:ns + 6 + hn]
        h_outs = refs[ns + 6 + hn:ns + 6 + 2 * hn]
        acc = refs[ns + 6 + 2 * hn]
        h_sems = refs[ns + 7 + 2 * hn:]
        i, k = pl.program_id(0), pl.program_id(1)

        @pl.when((i == 0) & (k == 0))
        def _():
            gng_ref[...] = jnp.zeros_like(gng_ref)
            if hosted is not None:
                hosted.start(h_ins, h_outs, h_sems)

        @pl.when(k == 0)
        def _():
            acc[...] = jnp.zeros_like(acc)

        for s in range(ns):
            @pl.when((k >= starts[s]) & (k < starts[s] + counts[s]))
            def _(s=s):
                t = seg_refs[s][...]
                if segments[s][1]:
                    t = _perm_rows(t, True)
                acc[...] += _nt(t, w_ref[...])

        @pl.when(k == nk - 1)
        def _():
            dh = acc[...]
            xf = x_ref[...]
            r = lax.rsqrt(jnp.mean(xf * xf, axis=-1, keepdims=True) + NORM_EPS)
            nrm = xf * r
            gng_ref[...] += jnp.sum(dh * nrm, axis=0, keepdims=True)
            dn = dh * g_ref[...]
            gx_ref[...] = dx2_ref[...] + r * (dn - nrm * jnp.mean(dn * nrm, axis=-1, keepdims=True))

        if hosted is not None:
            @pl.when((i == S // tm - 1) & (k == nk - 1))
            def _():
                hosted.finish(h_ins, h_outs, h_sems)

    row = pl.BlockSpec((tm, D), lambda i, k: (i, 0))
    vec = pl.BlockSpec((1, D), lambda i, k: (0, 0))
    in_specs = [pl.BlockSpec((tm, tk), lambda i, k, s=s: (i, jnp.clip(k - starts[s], 0, counts[s] - 1)))
                for s in range(ns)]
    in_specs += [pl.BlockSpec((D, tk), lambda i, k: (0, k)), row, row, vec]
    args = [a for a, _ in segments] + [w_in, x, dx2, g]
    out_specs = [row, vec]
    out_shape = [jax.ShapeDtypeStruct((S, D), F32), jax.ShapeDtypeStruct((1, D), F32)]
    scratch = [pltpu.VMEM((tm, D), F32)]
    if hosted is not None:
        in_specs += [ANY_SPEC] * hn
        args += hosted.arrays
        out_specs += [ANY_SPEC] * hn
        out_shape += hosted.out_shapes()
        scratch += hosted.sem_shapes()
    res = pl.pallas_call(
        body, name="dh", grid=(S // tm, nk),
        in_specs=in_specs, out_specs=tuple(out_specs), out_shape=tuple(out_shape), scratch_shapes=scratch,
        compiler_params=_params(("arbitrary", "arbitrary")),
    )(*args)
    return res[0], res[1], list(res[2:])


def _tn_matmul(a, b, name, tm=512):
    M, K = a.shape
    N = b.shape[1]
    tn = min(N, 1024)

    def body(a_ref, b_ref, o_ref):
        @pl.when(pl.program_id(1) == 0)
        def _():
            o_ref[...] = jnp.zeros_like(o_ref)

        o_ref[...] += _tn(a_ref[...], b_ref[...])

    return pl.pallas_call(
        body, name=name, grid=(N // tn, M // tm),
        in_specs=[pl.BlockSpec((tm, K), lambda j, m: (m, 0)), pl.BlockSpec((tm, tn), lambda j, m: (m, j))],
        out_specs=pl.BlockSpec((K, tn), lambda j, m: (0, j)),
        out_shape=jax.ShapeDtypeStruct((K, N), F32),
        compiler_params=_params(("parallel", "arbitrary")),
    )(a, b)


def _adamw(parts, w, m, v, name, tr=None, split=None):
    R, C = w.shape
    tr = R if tr is None else tr
    parts = [parts] if split is None else list(parts)
    npar = len(parts)

    def total(p_ref):
        g = p_ref[0].astype(F32)
        for dev in range(1, N_DEV):
            g = g + p_ref[dev].astype(F32)
        return g

    def body(*refs):
        w_ref, m_ref, v_ref, g_out, d_out, m_out, v_out = refs[npar:]
        if split is None:
            g = total(refs[0])
        else:
            g = jnp.where(_mesh_pos()[3] < split, total(refs[0]), total(refs[1]))
        mn = ADAM_B1 * m_ref[...] + (1.0 - ADAM_B1) * g
        vn = ADAM_B2 * v_ref[...] + (1.0 - ADAM_B2) * (g * g)
        m_hat = mn / (1.0 - ADAM_B1 ** ADAM_STEP)
        v_hat = vn / (1.0 - ADAM_B2 ** ADAM_STEP)
        g_out[...] = g
        d_out[...] = -ADAM_LR * (m_hat / (jnp.sqrt(v_hat) + ADAM_EPS) + ADAM_WD * w_ref[...])
        m_out[...] = mn
        v_out[...] = vn

    blk = pl.BlockSpec((tr, C), lambda i: (i, 0))
    shp = jax.ShapeDtypeStruct((R, C), F32)
    return pl.pallas_call(
        body, name=name, grid=(R // tr,),
        in_specs=[pl.BlockSpec((N_DEV, tr, C), lambda i: (0, i, 0))] * npar + [blk, blk, blk],
        out_specs=(blk, blk, blk, blk), out_shape=(shp, shp, shp, shp),
        compiler_params=_params(("parallel",)),
    )(*parts, w, m, v)


def _local_step(x, target, norm_g, w_in, conv_w, conv_b, ln_g, ln_b, w_out, gf, exchanges=None):
    ex_out, ex_att, ex_conv = exchanges if exchanges is not None else (None, None, None)
    att_cols = ATT_W + 2 * KV_W
    q, h_rm = _inproj(x, norm_g, w_in, 0, ATT_W, F32, HEAD_DIM ** -0.5, True, True, "inproj_q")
    kv = _inproj(x, norm_g, w_in, ATT_W // 512, 2 * KV_W, F32, 1.0, False, True, "inproj_kv")
    a_gate = _inproj(x, norm_g, w_in, att_cols // 512, ATT_W, F32, 1.0, False, True, "inproj_a_gate")
    gates, h = _inproj(x, norm_g, w_in, (att_cols + ATT_W) // 512, w_in.shape[1] - att_cols - ATT_W, F32, 1.0,
                       True, False, "inproj_conv")

    pats = [_attn_fwd(q, kv, dil, "attn_fwd_d%d" % dil) for _, dil in PATTERNS]
    o, lse, y_att = _attn_combine([p[0] for p in pats], [p[1] for p in pats], a_gate)
    conv_out, y_conv = _conv_fwd(gates, conv_w, conv_b, ln_g, ln_b)
    dx2, dxb, loss_cols, g_gf = _outproj_loss(x, y_att, y_conv, w_out, gf, target)

    d_o, d_a_gate, delta, dxb_rm = _dy_att(dxb, w_out, a_gate, o)
    g_w_out = jnp.concatenate([_tn_matmul(y_att, dxb_rm, "gw_out_att"), _tn_matmul(y_conv, dxb, "gw_out_conv")],
                              axis=0)
    acc, out_parts = None, []
    for idx, (_, dil) in enumerate(reversed(PATTERNS)):
        hosted = ex_out(g_w_out) if (idx == 0 and ex_out is not None) else None
        acc, outs = _attn_bwd(q, kv, d_o, lse, delta, dil, acc, idx == len(PATTERNS) - 1, "attn_bwd_d%d" % dil,
                              hosted)
        out_parts += outs
    dq, dkv = acc
    g_q, g_kv, g_a = (_tn_matmul(h_rm, dq, "gw_in_q"), _tn_matmul(h_rm, dkv, "gw_in_kv"),
                      _tn_matmul(h_rm, d_a_gate, "gw_in_a_gate"))

    d_c_gate, d_conv, g_ln_g, g_ln_b, g_conv_b = _dy_conv(dxb, w_out, gates, conv_out, ln_g, ln_b)
    dgates, g_conv_w, att_parts = _conv_bwd(d_conv, gates, d_c_gate, conv_w,
                                            ex_att(g_q, g_kv, g_a) if ex_att is not None else None)
    g_c = _tn_matmul(h, dgates, "gw_in_conv")
    grad_x, g_norm_g, conv_parts = _dh(
        [(dq, True), (dkv, True), (d_a_gate, True), (dgates, False)], w_in, x, dx2, norm_g,
        ex_conv(g_a, g_c, g_conv_w) if ex_conv is not None else None)
    small = (g_norm_g, g_conv_b, g_ln_g, g_ln_b, g_gf, loss_cols)
    return grad_x, (g_q, g_kv, g_a, g_c), g_w_out, g_conv_w, small, (out_parts, att_parts, conv_parts)


def kernel(x, norm_g, w_in, conv_w, conv_b, conv_ln_g, conv_ln_b, w_out, final_norm_g, loss_target, m_norm_g, m_w_in, m_conv_w, m_conv_b, m_conv_ln_g, m_conv_ln_b, m_w_out, m_final_norm_g, v_norm_g, v_w_in, v_conv_w, v_conv_b, v_conv_ln_g, v_conv_ln_b, v_w_out, v_final_norm_g):
    S, D = x.shape[1], x.shape[2]
    win_sh, wout_sh, cw_sh = w_in[0], w_out[0], conv_w[0]
    cols_sh, rows_sh, ch_sh = win_sh.shape[1], wout_sh.shape[0], cw_sh.shape[1]

    win_all, wout_all, cw_all = _gather_two_level(
        [win_sh.astype(BF16), wout_sh.astype(BF16), cw_sh], "gather_weights")
    w_in_full = win_all.transpose(1, 0, 2).reshape(D, N_DEV * cols_sh)
    w_out_full = wout_all.reshape(N_DEV * rows_sh, D)
    conv_w_full = cw_all.transpose(1, 0, 2).reshape(CONV_K, N_DEV * ch_sh)
    conv_w_full = jnp.pad(conv_w_full, ((0, CONV_HALO - CONV_K), (0, 0)))
    gf = final_norm_g.reshape(1, D)

    first = -(-(ATT_W + 2 * KV_W) // cols_sh)
    a_off = first * cols_sh - (ATT_W + 2 * KV_W)
    assert 0 <= a_off <= ATT_W

    def pieces(g, n):
        return g.reshape(D, n, cols_sh).transpose(1, 0, 2).astype(BF16)

    def ex_out(g_w_out):
        return _Exchange([g_w_out.reshape(N_DEV, rows_sh, D).astype(BF16)], [(0, N_DEV)])

    def ex_att(g_q, g_kv, g_a):
        return _Exchange([pieces(jnp.concatenate([g_q, g_kv, g_a[:, :a_off]], axis=1), first)], [(0, first)])

    def ex_conv(g_a, g_c, g_conv_w):
        return _Exchange(
            [pieces(jnp.concatenate([g_a[:, a_off:], g_c], axis=1), N_DEV - first),
             g_conv_w[:CONV_K].reshape(CONV_K, N_DEV, ch_sh).transpose(1, 0, 2)],
            [(first, N_DEV), (0, N_DEV)])

    grad_x, _, _, _, small, parts = _local_step(
        x[0], loss_target[0], norm_g, w_in_full, conv_w_full, conv_b, conv_ln_g, conv_ln_b, w_out_full, gf,
        (ex_out, ex_att, ex_conv))
    (wout_parts,), (win_parts_lo,), (win_parts_hi, cw_parts) = parts

    small_pack = jnp.concatenate(list(small) + [jnp.zeros((2, D), F32)], axis=0)
    small_parts, = _exchange([small_pack], [None], "gather_small")

    upd_win = _adamw((win_parts_lo, win_parts_hi), win_sh, m_w_in[0], v_w_in[0], "adamw_w_in", tr=256, split=first)
    upd_wout = _adamw(wout_parts, wout_sh, m_w_out[0], v_w_out[0], "adamw_w_out", tr=128)
    upd_cw = _adamw(cw_parts, cw_sh, m_conv_w[0], v_conv_w[0], "adamw_conv_w")
    zeros3 = jnp.zeros((3, D), F32)
    stack = lambda a, b, c, d_, e: jnp.concatenate([a, b, c, d_, e.reshape(1, D), zeros3], axis=0)
    upd_small = _adamw(
        small_parts,
        stack(norm_g, conv_b, conv_ln_g, conv_ln_b, final_norm_g),
        stack(m_norm_g, m_conv_b, m_conv_ln_g, m_conv_ln_b, m_final_norm_g),
        stack(v_norm_g, v_conv_b, v_conv_ln_g, v_conv_ln_b, v_final_norm_g) + jnp.concatenate(
            [jnp.zeros((5, D), F32), jnp.ones((3, D), F32)], axis=0),
        "adamw_small")

    loss = 0.5 / D * jnp.sum(upd_small[0][5])

    def outputs(kind):
        sm = upd_small[kind]
        return [sm[0:1], upd_win[kind][None], upd_cw[kind][None], sm[1:2], sm[2:3], sm[3:4],
                upd_wout[kind][None], sm[4]]

    return (loss, grad_x[None], *outputs(0), *outputs(1), *outputs(2), *outputs(3))
```

```python
import jax
import jax.numpy as jnp
from jax import lax
from jax.experimental import pallas as pl
from jax.experimental.pallas import tpu as pltpu

F32 = jnp.float32
BF16 = jnp.bfloat16

HEAD_DIM = 64
N_KV_HEADS = 4
N_Q_HEADS = 16
ATT_W = 1024
KV_W = 256
CONV_K = 31
CONV_HALO = 32
PATTERNS = ((128, 1), (512, 4), (2048, 16))
BLK = 128
LANES = 128
NORM_EPS = 1e-6
LN_EPS = 1e-5
NEG = -1e30
N_DEV = 8
ADAM_LR, ADAM_B1, ADAM_B2, ADAM_EPS, ADAM_WD, ADAM_STEP = 0.001, 0.9, 0.999, 1e-08, 0.01, 10
VMEM_LIMIT = 48 * 1024 * 1024
SLOPES = tuple(2.0 ** (-8.0 * (h + 1) / N_Q_HEADS) for h in range(N_Q_HEADS))
MESH = pl.DeviceIdType.MESH


def _params(sem):
    return pltpu.CompilerParams(dimension_semantics=sem, vmem_limit_bytes=VMEM_LIMIT)


def _sigmoid(v):
    return 1.0 / (1.0 + jnp.exp(-v))


def _silu_and_grad(v):
    s = _sigmoid(v)
    return v * s, s * (1.0 + v * (1.0 - s))


ANY_SPEC = pl.BlockSpec(memory_space=pl.ANY)


def _mesh_pos():
    x, y, c = lax.axis_index("x"), lax.axis_index("y"), lax.axis_index("c")
    return x, y, c, 4 * x + 2 * y + c


def _flipped(k, x, y, c):
    px = 1 - x if k & 4 else x
    py = 1 - y if k & 2 else y
    pc = 1 - c if k & 1 else c
    return (px, py, pc), 4 * px + 2 * py + pc


class _Exchange:
    def __init__(self, arrays, dests):
        self.arrays, self.dests, self.n = list(arrays), list(dests), len(arrays)

    def out_shapes(self):
        return [jax.ShapeDtypeStruct((N_DEV,) + a.shape[-2:], a.dtype) for a in self.arrays]

    def sem_shapes(self):
        return [pltpu.SemaphoreType.DMA((self.n, N_DEV - 1)), pltpu.SemaphoreType.DMA((self.n, N_DEV - 1)),
                pltpu.SemaphoreType.DMA((self.n,))]

    def _when(self, a, dev, fn):
        if self.dests[a] is None:
            fn()
        else:
            lo, hi = self.dests[a]
            pl.when((dev >= lo) & (dev < hi))(fn)

    def _mine(self, ins, a, dev):
        return ins[a] if self.dests[a] is None else ins[a].at[dev - self.dests[a][0]]

    def _copy(self, ins, outs, sems, a, k, src_dev, slot, target):
        return pltpu.make_async_remote_copy(
            src_ref=self._mine(ins, a, src_dev), dst_ref=outs[a].at[slot],
            send_sem=sems[0].at[a, k - 1], recv_sem=sems[1].at[a, k - 1],
            device_id=target, device_id_type=MESH)

    def start(self, ins, outs, sems):
        x, y, c, me = _mesh_pos()
        for a in range(self.n):
            self._when(a, me, lambda a=a: pltpu.make_async_copy(
                self._mine(ins, a, me), outs[a].at[me], sems[2].at[a]).start())
            for k in range(1, N_DEV):
                target, peer = _flipped(k, x, y, c)
                self._when(a, peer, lambda a=a, k=k, target=target, peer=peer: self._copy(
                    ins, outs, sems, a, k, peer, me, target).start())

    def finish(self, ins, outs, sems):
        x, y, c, me = _mesh_pos()
        lo0 = [0 if d is None else d[0] for d in self.dests]
        for a in range(self.n):
            for k in range(1, N_DEV):
                target, peer = _flipped(k, x, y, c)
                self._when(a, me, lambda a=a, k=k, peer=peer: self._copy(
                    ins, outs, sems, a, k, lo0[a], peer, (x, y, c)).wait_recv())
            for k in range(1, N_DEV):
                target, peer = _flipped(k, x, y, c)
                self._when(a, peer, lambda a=a, k=k, target=target, peer=peer: self._copy(
                    ins, outs, sems, a, k, peer, me, target).wait_send())
            self._when(a, me, lambda a=a: pltpu.make_async_copy(
                self._mine(ins, a, me), outs[a].at[me], sems[2].at[a]).wait())


def _exchange(arrays, dests, name):
    ex = _Exchange(arrays, dests)
    na = ex.n

    def body(*refs):
        ins, outs, sems = refs[:na], refs[na:2 * na], refs[2 * na:]
        ex.start(ins, outs, sems)
        ex.finish(ins, outs, sems)

    return pl.pallas_call(
        body, name=name, out_shape=tuple(ex.out_shapes()),
        in_specs=[ANY_SPEC] * na, out_specs=tuple([ANY_SPEC] * na), scratch_shapes=ex.sem_shapes(),
    )(*arrays)


def _gather_two_level(arrays, name):
    na = len(arrays)

    def body(*refs):
        ins, outs = refs[:na], refs[na:2 * na]
        send_sems, recv_sems, loc_sems = refs[2 * na:]
        x, y, c, me = _mesh_pos()
        sibling = (x, y, 1 - c)
        chips = [(1 - x, y), (x, 1 - y), (1 - x, 1 - y)]

        def slot(px, py, pc):
            return 4 * px + 2 * py + pc

        def copy(a, k, src, block, to):
            return pltpu.make_async_remote_copy(
                src_ref=src, dst_ref=outs[a].at[slot(*block)], send_sem=send_sems.at[a, k], recv_sem=recv_sems.at[a, k],
                device_id=to, device_id_type=MESH)

        local = [pltpu.make_async_copy(ins[a], outs[a].at[me], loc_sems.at[a]) for a in range(na)]
        for cp in local:
            cp.start()
        started = []
        for a in range(na):
            started.append(copy(a, 0, ins[a], (x, y, c), sibling))
            started += [copy(a, 1 + j, ins[a], (x, y, c), (*chip, c)) for j, chip in enumerate(chips)]
        for cp in started:
            cp.start()
        for j, chip in enumerate(chips):
            for a in range(na):
                copy(a, 1 + j, ins[a], (*chip, c), (x, y, c)).wait_recv()
                fwd = copy(a, 4 + j, outs[a].at[slot(*chip, c)], (*chip, c), sibling)
                fwd.start()
                started.append(fwd)
        for a in range(na):
            copy(a, 0, ins[a], sibling, (x, y, c)).wait_recv()
            for j, chip in enumerate(chips):
                copy(a, 4 + j, ins[a], (*chip, 1 - c), (x, y, c)).wait_recv()
        for cp in started:
            cp.wait_send()
        for cp in local:
            cp.wait()

    return pl.pallas_call(
        body, name=name,
        out_shape=tuple(jax.ShapeDtypeStruct((N_DEV,) + a.shape, a.dtype) for a in arrays),
        in_specs=[ANY_SPEC] * na, out_specs=tuple([ANY_SPEC] * na),
        scratch_shapes=[pltpu.SemaphoreType.DMA((na, N_DEV - 1)), pltpu.SemaphoreType.DMA((na, N_DEV - 1)),
                        pltpu.SemaphoreType.DMA((na,))],
    )(*arrays)


CHUNK = 128
RESIDUES = 16
PER_RES = CHUNK // RESIDUES


def _perm_rows(tile, inverse):
    a = lax.broadcasted_iota(jnp.int32, (CHUNK, CHUNK), 0)
    b = lax.broadcasted_iota(jnp.int32, (CHUNK, CHUNK), 1)
    if inverse:
        a, b = b, a
    p = jnp.where(a == PER_RES * (b % RESIDUES) + b // RESIDUES, 1.0, 0.0).astype(BF16)
    parts = [jnp.dot(p, tile[c * CHUNK:(c + 1) * CHUNK], preferred_element_type=F32)
             for c in range(tile.shape[0] // CHUNK)]
    return jnp.concatenate(parts, axis=0).astype(BF16)


class _Rows:
    def __init__(self, dil, S):
        nc = S // CHUNK
        self.dil = dil
        if dil == 1:
            self.view, self.block, self.nb = (nc, CHUNK), (None, CHUNK), nc
            self.index = lambda r, b: (b, 0, 0)
        elif dil == 4:
            self.view, self.block, self.nb = (nc, 4, 4, PER_RES), (4, 4, None, PER_RES), nc // 4
            self.index = lambda r, b: (b, 0, r, 0, 0)
        elif dil == RESIDUES:
            self.view, self.block, self.nb = (nc, RESIDUES, PER_RES), (RESIDUES, None, PER_RES), nc // RESIDUES
            self.index = lambda r, b: (b, r, 0, 0)
        else:
            raise NotImplementedError(dil)

    def of(self, a):
        return a.reshape(self.view + (a.shape[-1],))

    def spec(self, width, which_block):
        return pl.BlockSpec(self.block + (width,), lambda r, n: self.index(r, which_block(n)))

    def pos(self, row):
        if self.dil == 1:
            return (row % PER_RES) * RESIDUES + row // PER_RES
        if self.dil == 4:
            return (row // 32) * 32 + (row % PER_RES) * 4 + (row % 32) // PER_RES
        return row


def _ld(ref, cols=slice(None)):
    v = ref[(slice(None),) * (len(ref.shape) - 1) + (cols,)]
    return v.reshape(BLK, v.shape[-1])


def _st(ref, val, cols=slice(None)):
    ref[(slice(None),) * (len(ref.shape) - 1) + (cols,)] = val.reshape(ref.shape[:-1] + (val.shape[-1],))


def _inproj(x, g, w, col_block, ncols, out_dtype, scale, emit_h, perm, name, tm=1024, tn=512):
    S, D = x.shape

    def body(x_ref, g_ref, w_ref, *rest):
        if emit_h:
            o_ref, h_out, h_scr = rest
        else:
            o_ref, h_scr = rest

        @pl.when(pl.program_id(1) == 0)
        def _():
            xf = x_ref[...]
            r = lax.rsqrt(jnp.mean(xf * xf, axis=-1, keepdims=True) + NORM_EPS)
            h = (xf * r * g_ref[...]).astype(BF16)
            if perm:
                h = _perm_rows(h, False)
            h_scr[...] = h
            if emit_h:
                h_out[...] = h

        acc = jnp.dot(h_scr[...], w_ref[...], preferred_element_type=F32)
        if scale != 1.0:
            acc = acc * scale
        o_ref[...] = acc.astype(out_dtype)

    out_shape = [jax.ShapeDtypeStruct((S, ncols), out_dtype)]
    out_specs = [pl.BlockSpec((tm, tn), lambda i, j: (i, j))]
    if emit_h:
        out_shape.append(jax.ShapeDtypeStruct((S, D), BF16))
        out_specs.append(pl.BlockSpec((tm, D), lambda i, j: (i, 0)))
    res = pl.pallas_call(
        body, name=name, grid=(S // tm, ncols // tn),
        in_specs=[pl.BlockSpec((tm, D), lambda i, j: (i, 0)),
                  pl.BlockSpec((1, D), lambda i, j: (0, 0)),
                  pl.BlockSpec((D, tn), lambda i, j: (0, col_block + j))],
        out_specs=tuple(out_specs), out_shape=tuple(out_shape),
        scratch_shapes=[pltpu.VMEM((tm, D), BF16)],
        compiler_params=_params(("parallel", "arbitrary")),
    )(x, g, w)
    return res if emit_h else res[0]


def _fill_bias_table(tbl, rows):
    qi = lax.broadcasted_iota(jnp.int32, (BLK, 2 * BLK), 0)
    kj = lax.broadcasted_iota(jnp.int32, (BLK, 2 * BLK), 1)
    dist = rows.pos(qi) - rows.pos(kj % BLK) + jnp.where(kj < BLK, BLK, 0)
    inside = (dist >= 0) & (dist <= BLK)
    negd = (dist * (-rows.dil)).astype(F32)
    for f, valid in enumerate((inside & (kj >= BLK), inside)):
        for h in range(N_Q_HEADS):
            tbl[f * N_Q_HEADS + h] = jnp.where(valid, SLOPES[h] * negd, NEG)


def _bias2(tbl, n, h0, h1):
    base = jnp.where(n == 0, 0, N_Q_HEADS)
    return jnp.concatenate([tbl[base + h0], tbl[base + h1]], axis=0)


def _head_operands(kv2, hk, lo_mask):
    half, pos = hk // 2, hk % 2
    out = []
    for base in (0, KV_W):
        t = kv2[:, base + half * LANES: base + (half + 1) * LANES]
        sw = pltpu.roll(t, HEAD_DIM, axis=1)
        at_lo, at_hi = (t, sw) if pos == 0 else (sw, t)
        out.append(jnp.where(lo_mask, at_lo, 0.0).astype(BF16))
        out.append(jnp.where(lo_mask, 0.0, at_hi).astype(BF16))
    return out


def _nt(a, b):
    return lax.dot_general(a, b, (((1,), (1,)), ((), ())), preferred_element_type=F32)


def _tn(a, b):
    return lax.dot_general(a, b, (((0,), (0,)), ((), ())), preferred_element_type=F32)


def _attn_fwd(q, kv, dil, name, prev=None, gate=None):
    S = q.shape[0]
    rows = _Rows(dil, S)
    nb = rows.nb
    have_prev, last = prev is not None, gate is not None

    def body(*refs):
        refs = list(refs)
        q_ref, kvc_ref, kvp_ref = refs[:3]
        del refs[:3]
        if have_prev:
            po_ref, pl_ref = refs[:2]
            del refs[:2]
        if last:
            gate_ref = refs.pop(0)
        o_ref, lse_ref = refs[:2]
        y_ref = refs[2] if last else None
        tbl = refs[-1]
        n = pl.program_id(1)

        @pl.when((pl.program_id(0) == 0) & (n == 0))
        def _():
            _fill_bias_table(tbl, rows)

        kv2 = jnp.concatenate([_ld(kvp_ref), _ld(kvc_ref)], axis=0)
        lo_mask = lax.broadcasted_iota(jnp.int32, (2 * BLK, LANES), 1) < HEAD_DIM
        lane = lax.broadcasted_iota(jnp.int32, (BLK, LANES), 1)
        stats = jnp.zeros((BLK, LANES), F32)
        if have_prev:
            lse_before = _ld(pl_ref)
        for hk in range(N_KV_HEADS):
            k_lo, k_hi, v_lo, v_hi = _head_operands(kv2, hk, lo_mask)
            cols = [slice(b * LANES, (b + 1) * LANES) for b in (2 * hk, 2 * hk + 1)]
            q2 = jnp.concatenate([_ld(q_ref, cols[0]), _ld(q_ref, cols[1])], axis=0).astype(BF16)
            o2 = jnp.zeros((2 * BLK, LANES), F32)
            keep = []
            for which, (kk, vv) in enumerate(((k_lo, v_lo), (k_hi, v_hi))):
                h0, h1 = 4 * hk + which, 4 * hk + 2 + which
                s = _nt(q2, kk) + _bias2(tbl, n, h0, h1)
                m = jnp.max(s, axis=1, keepdims=True)
                p = jnp.exp(s - m)
                l = jnp.sum(p, axis=1, keepdims=True)
                lse = m + jnp.log(l)
                scale = 1.0 / l
                if have_prev:
                    before = jnp.concatenate([lse_before[:, h0:h0 + 1], lse_before[:, h1:h1 + 1]], axis=0)
                    top = jnp.maximum(before, lse)
                    e_old, e_new = jnp.exp(before - top), jnp.exp(lse - top)
                    total = e_old + e_new
                    lse = top + jnp.log(total)
                    scale = scale * (e_new / total)
                    keep.append(e_old / total)
                o2 = o2 + jnp.dot(p.astype(BF16), vv, preferred_element_type=F32) * scale
                stats = jnp.where(lane == h0, lse[0:BLK], stats)
                stats = jnp.where(lane == h1, lse[BLK:], stats)
            if have_prev:
                o_before = jnp.concatenate([_ld(po_ref, cols[0]), _ld(po_ref, cols[1])], axis=0)
                o2 = o2 + o_before * jnp.where(lo_mask, keep[0], keep[1])
            for jp in range(2):
                o_blk = o2[jp * BLK:(jp + 1) * BLK]
                _st(o_ref, o_blk, cols[jp])
                if last:
                    a = _ld(gate_ref, cols[jp])
                    _st(y_ref, (o_blk * (a * _sigmoid(a))).astype(BF16), cols[jp])
        _st(lse_ref, stats)

    here = lambda n: n
    before_n = lambda n: jnp.maximum(n - 1, 0)
    in_specs = [rows.spec(ATT_W, here), rows.spec(2 * KV_W, here), rows.spec(2 * KV_W, before_n)]
    args = [rows.of(q), rows.of(kv), rows.of(kv)]
    if have_prev:
        in_specs += [rows.spec(ATT_W, here), rows.spec(LANES, here)]
        args += [rows.of(prev[0]), rows.of(prev[1])]
    out_specs = [rows.spec(ATT_W, here), rows.spec(LANES, here)]
    out_shape = [jax.ShapeDtypeStruct(rows.view + (ATT_W,), F32), jax.ShapeDtypeStruct(rows.view + (LANES,), F32)]
    if last:
        in_specs.append(rows.spec(ATT_W, here))
        args.append(rows.of(gate))
        out_specs.append(rows.spec(ATT_W, here))
        out_shape.append(jax.ShapeDtypeStruct(rows.view + (ATT_W,), BF16))
    res = pl.pallas_call(
        body, name=name, grid=(dil, nb),
        in_specs=in_specs, out_specs=tuple(out_specs), out_shape=tuple(out_shape),
        scratch_shapes=[pltpu.VMEM((2 * N_Q_HEADS, BLK, 2 * BLK), F32)],
        compiler_params=_params(("arbitrary", "arbitrary")),
    )(*args)
    return tuple(r.reshape(S, r.shape[-1]) for r in res)


def _shifted_copies(buf, phases):
    n = phases.shape[1]
    for b in range(1, 8):
        phases[b - 1] = buf[b:b + n, :]


def _window(buf, phases, start, cols):
    b = start % 8
    if b == 0:
        return buf[start:start + 8, cols]
    return phases[b - 1, start - b:start - b + 8, cols]


def _conv_fwd(gates, conv_w, conv_b, ln_g, ln_b, tt=256):
    S = gates.shape[0]
    C = conv_w.shape[1]
    hb = tt // CONV_HALO

    def body(val_ref, glu_ref, hval_ref, hglu_ref, gate_ref, w_ref, b_ref, g_ref, beta_ref,
             conv_ref, y_ref, hbuf, hph):
        i = pl.program_id(0)
        halo = hval_ref[...] * _sigmoid(hglu_ref[...])
        hbuf[0:CONV_HALO, :] = jnp.where(i > 0, halo, 0.0)
        hbuf[CONV_HALO:, :] = val_ref[...] * _sigmoid(glu_ref[...])
        _shifted_copies(hbuf, hph)
        for cb in range(C // LANES):
            cols = slice(cb * LANES, (cb + 1) * LANES)
            wj = [jnp.broadcast_to(w_ref[j:j + 1, cols], (8, LANES)) for j in range(CONV_K)]
            for rc in range(tt // 8):
                acc = jnp.zeros((8, LANES), F32)
                for j in range(CONV_K):
                    start = rc * 8 + CONV_HALO - (CONV_K - 1) + j
                    acc = acc + _window(hbuf, hph, start, cols) * wj[j]
                conv_ref[rc * 8:(rc + 1) * 8, cols] = acc
        cv = conv_ref[...] + b_ref[...]
        conv_ref[...] = cv
        mu = jnp.mean(cv, axis=-1, keepdims=True)
        xc = cv - mu
        var = jnp.mean(xc * xc, axis=-1, keepdims=True)
        ln = xc * lax.rsqrt(var + LN_EPS) * g_ref[...] + beta_ref[...]
        gt = gate_ref[...]
        y_ref[...] = (ln * _sigmoid(ln) * (gt * _sigmoid(gt))).astype(BF16)

    vec = pl.BlockSpec((1, C), lambda i: (0, 0))
    return pl.pallas_call(
        body, name="conv_fwd", grid=(S // tt,),
        in_specs=[pl.BlockSpec((tt, C), lambda i: (i, 0)),
                  pl.BlockSpec((tt, C), lambda i: (i, 1)),
                  pl.BlockSpec((CONV_HALO, C), lambda i: (jnp.maximum(i * hb - 1, 0), 0)),
                  pl.BlockSpec((CONV_HALO, C), lambda i: (jnp.maximum(i * hb - 1, 0), 1)),
                  pl.BlockSpec((tt, C), lambda i: (i, 2)),
                  pl.BlockSpec((CONV_HALO, C), lambda i: (0, 0)), vec, vec, vec],
        out_specs=(pl.BlockSpec((tt, C), lambda i: (i, 0)), pl.BlockSpec((tt, C), lambda i: (i, 0))),
        out_shape=(jax.ShapeDtypeStruct((S, C), F32), jax.ShapeDtypeStruct((S, C), BF16)),
        scratch_shapes=[pltpu.VMEM((tt + CONV_HALO, C), F32), pltpu.VMEM((7, tt + CONV_HALO - 8, C), F32)],
        compiler_params=_params(("parallel",)),
    )(gates, gates, gates, gates, gates, conv_w, conv_b, ln_g, ln_b)


def _outproj_loss(x, y_att, y_conv, w_out, gf, target, tm=512):
    S, D = x.shape
    E = y_att.shape[1]

    def body(x_ref, ya_ref, yc_ref, w_ref, gf_ref, t_ref, dx_ref, dxb_ref, loss_ref, ggf_ref):
        @pl.when(pl.program_id(0) == 0)
        def _():
            loss_ref[...] = jnp.zeros_like(loss_ref)
            ggf_ref[...] = jnp.zeros_like(ggf_ref)

        x2 = (x_ref[...] + jnp.dot(_perm_rows(ya_ref[...], True), w_ref[0:E, :], preferred_element_type=F32)
              + jnp.dot(yc_ref[...], w_ref[E:, :], preferred_element_type=F32))
        r = lax.rsqrt(jnp.mean(x2 * x2, axis=-1, keepdims=True) + NORM_EPS)
        nrm = x2 * r
        gfv = gf_ref[...]
        err = nrm * gfv - t_ref[...]
        loss_ref[...] += jnp.sum(err * err, axis=0, keepdims=True)
        dout = err * (1.0 / D)
        ggf_ref[...] += jnp.sum(dout * nrm, axis=0, keepdims=True)
        dn = dout * gfv
        dx2 = r * (dn - nrm * jnp.mean(dn * nrm, axis=-1, keepdims=True))
        dx_ref[...] = dx2
        dxb_ref[...] = dx2.astype(BF16)

    row = lambda w: pl.BlockSpec((tm, w), lambda i: (i, 0))
    vec = pl.BlockSpec((1, D), lambda i: (0, 0))
    return pl.pallas_call(
        body, name="outproj_loss", grid=(S // tm,),
        in_specs=[row(D), row(E), row(E), pl.BlockSpec((2 * E, D), lambda i: (0, 0)), vec, row(D)],
        out_specs=(row(D), row(D), vec, vec),
        out_shape=(jax.ShapeDtypeStruct((S, D), F32), jax.ShapeDtypeStruct((S, D), BF16),
                   jax.ShapeDtypeStruct((1, D), F32), jax.ShapeDtypeStruct((1, D), F32)),
        compiler_params=_params(("arbitrary",)),
    )(x, y_att, y_conv, w_out, gf, target)


def _split3(v):
    hi = v.astype(BF16)
    r1 = v - hi.astype(F32)
    mid = r1.astype(BF16)
    lo = (r1 - mid.astype(F32)).astype(BF16)
    return hi, mid, lo


def _dy_att(dxb, w_out, gates, o, tm=512):
    S, D = dxb.shape
    E = ATT_W

    def body(dx_ref, w_ref, a_ref, o_ref, do_ref, da_ref, dl_ref, dxr_ref):
        dxr = _perm_rows(dx_ref[...], False)
        dxr_ref[...] = dxr
        dya = _nt(dxr, w_ref[...])
        a = a_ref[...]
        ov = o_ref[...]
        sl, dsl = _silu_and_grad(a)
        d_o = dya * sl
        do_ref[...] = d_o
        da_ref[...] = (dya * ov * dsl).astype(BF16)
        ci = lax.broadcasted_iota(jnp.int32, (E, LANES), 0) // HEAD_DIM
        hi = lax.broadcasted_iota(jnp.int32, (E, LANES), 1)
        sel = jnp.where(ci == hi, 1.0, 0.0).astype(BF16)
        acc = jnp.zeros((tm, LANES), F32)
        for part in _split3(d_o * ov):
            acc = acc + jnp.dot(part, sel, preferred_element_type=F32)
        dl_ref[...] = acc

    row = lambda w: pl.BlockSpec((tm, w), lambda i: (i, 0))
    return pl.pallas_call(
        body, name="dy_att", grid=(S // tm,),
        in_specs=[row(D), pl.BlockSpec((E, D), lambda i: (0, 0)), row(E), row(E)],
        out_specs=(row(E), row(E), row(LANES), row(D)),
        out_shape=(jax.ShapeDtypeStruct((S, E), F32), jax.ShapeDtypeStruct((S, E), BF16),
                   jax.ShapeDtypeStruct((S, LANES), F32), jax.ShapeDtypeStruct((S, D), BF16)),
        compiler_params=_params(("parallel",)),
    )(dxb, w_out, gates, o)


def _dy_conv(dxb, w_out, gates, conv_out, ln_g, ln_b, tm=512):
    S, D = dxb.shape
    C = conv_out.shape[1]

    def body(dx_ref, w_ref, gate_ref, cv_ref, g_ref, beta_ref, dgate_ref, dconv_ref, gg_ref, gb_ref, gcb_ref):
        @pl.when(pl.program_id(0) == 0)
        def _():
            gg_ref[...] = jnp.zeros_like(gg_ref)
            gb_ref[...] = jnp.zeros_like(gb_ref)
            gcb_ref[...] = jnp.zeros_like(gcb_ref)

        dyc = _nt(dx_ref[...], w_ref[...])
        cv = cv_ref[...]
        mu = jnp.mean(cv, axis=-1, keepdims=True)
        xc = cv - mu
        rstd = lax.rsqrt(jnp.mean(xc * xc, axis=-1, keepdims=True) + LN_EPS)
        nrm = xc * rstd
        gv = g_ref[...]
        ln = nrm * gv + beta_ref[...]
        u, du = _silu_and_grad(ln)
        gt = gate_ref[...]
        g2, dg2 = _silu_and_grad(gt)
        dgate_ref[...] = (dyc * u * dg2).astype(BF16)
        d_ln = dyc * g2 * du
        gb_ref[...] += jnp.sum(d_ln, axis=0, keepdims=True)
        gg_ref[...] += jnp.sum(d_ln * nrm, axis=0, keepdims=True)
        dn = d_ln * gv
        d_conv = rstd * (dn - jnp.mean(dn, axis=-1, keepdims=True)
                         - nrm * jnp.mean(dn * nrm, axis=-1, keepdims=True))
        dconv_ref[...] = d_conv
        gcb_ref[...] += jnp.sum(d_conv, axis=0, keepdims=True)

    row = lambda w: pl.BlockSpec((tm, w), lambda i: (i, 0))
    vec = pl.BlockSpec((1, C), lambda i: (0, 0))
    return pl.pallas_call(
        body, name="dy_conv", grid=(S // tm,),
        in_specs=[row(D), pl.BlockSpec((C, D), lambda i: (1, 0)),
                  pl.BlockSpec((tm, C), lambda i: (i, 2)), row(C), vec, vec],
        out_specs=(row(C), row(C), vec, vec, vec),
        out_shape=(jax.ShapeDtypeStruct((S, C), BF16), jax.ShapeDtypeStruct((S, C), F32),
                   jax.ShapeDtypeStruct((1, C), F32), jax.ShapeDtypeStruct((1, C), F32),
                   jax.ShapeDtypeStruct((1, C), F32)),
        compiler_params=_params(("arbitrary",)),
    )(dxb, w_out, gates, conv_out, ln_g, ln_b)


def _conv_bwd(d_conv, gates, d_c_gate, conv_w, hosted=None, tt=256):
    S, C = d_conv.shape
    hb = tt // CONV_HALO
    nt = S // tt
    hn = hosted.n if hosted is not None else 0

    def body(*refs):
        dc_ref, dnext_ref, val_ref, glu_ref, hval_ref, hglu_ref, dg_ref, w_ref = refs[:8]
        h_ins = refs[8:8 + hn]
        out_ref, gw_ref = refs[8 + hn:10 + hn]
        h_outs = refs[10 + hn:10 + 2 * hn]
        hbuf, dbuf, dhbuf, hph, dph = refs[10 + 2 * hn:15 + 2 * hn]
        h_sems = refs[15 + 2 * hn:]
        i = pl.program_id(0)

        @pl.when(i == 0)
        def _():
            gw_ref[...] = jnp.zeros_like(gw_ref)
            if hosted is not None:
                hosted.start(h_ins, h_outs, h_sems)

        val = val_ref[...]
        sg = _sigmoid(glu_ref[...])
        halo = hval_ref[...] * _sigmoid(hglu_ref[...])
        hbuf[0:CONV_HALO, :] = jnp.where(i > 0, halo, 0.0)
        hbuf[CONV_HALO:, :] = val * sg
        dbuf[0:tt, :] = dc_ref[...]
        dbuf[tt:, :] = jnp.where(i < nt - 1, dnext_ref[...], 0.0)
        _shifted_copies(hbuf, hph)
        _shifted_copies(dbuf, dph)
        for cb in range(C // LANES):
            cols = slice(cb * LANES, (cb + 1) * LANES)
            wj = [jnp.broadcast_to(w_ref[j:j + 1, cols], (8, LANES)) for j in range(CONV_K)]
            for rc in range(tt // 8):
                acc = jnp.zeros((8, LANES), F32)
                for j in range(CONV_K):
                    acc = acc + _window(dbuf, dph, rc * 8 + (CONV_K - 1) - j, cols) * wj[j]
                dhbuf[rc * 8:(rc + 1) * 8, cols] = acc
            gacc = [jnp.zeros((8, LANES), F32) for _ in range(CONV_K)]
            for rc in range(tt // 8):
                dcur = dbuf[rc * 8:(rc + 1) * 8, cols]
                for j in range(CONV_K):
                    hs = rc * 8 + CONV_HALO - (CONV_K - 1) + j
                    gacc[j] = gacc[j] + dcur * _window(hbuf, hph, hs, cols)
            for j in range(CONV_K):
                gw_ref[j:j + 1, cols] += jnp.sum(gacc[j], axis=0, keepdims=True)
        d_h = dhbuf[...]
        out_ref[:, 0:C] = (d_h * sg).astype(BF16)
        out_ref[:, C:2 * C] = (d_h * val * sg * (1.0 - sg)).astype(BF16)
        out_ref[:, 2 * C:3 * C] = dg_ref[...]

        if hosted is not None:
            @pl.when(i == nt - 1)
            def _():
                hosted.finish(h_ins, h_outs, h_sems)

    tile = lambda col: pl.BlockSpec((tt, C), lambda i: (i, col))
    in_specs = [tile(0),
                pl.BlockSpec((CONV_HALO, C), lambda i: (jnp.minimum((i + 1) * hb, S // CONV_HALO - 1), 0)),
                tile(0), tile(1),
                pl.BlockSpec((CONV_HALO, C), lambda i: (jnp.maximum(i * hb - 1, 0), 0)),
                pl.BlockSpec((CONV_HALO, C), lambda i: (jnp.maximum(i * hb - 1, 0), 1)),
                tile(0),
                pl.BlockSpec((CONV_HALO, C), lambda i: (0, 0))]
    args = [d_conv, d_conv, gates, gates, gates, gates, d_c_gate, conv_w]
    out_specs = [pl.BlockSpec((tt, 3 * C), lambda i: (i, 0)), pl.BlockSpec((CONV_HALO, C), lambda i: (0, 0))]
    out_shape = [jax.ShapeDtypeStruct((S, 3 * C), BF16), jax.ShapeDtypeStruct((CONV_HALO, C), F32)]
    scratch = [pltpu.VMEM((tt + CONV_HALO, C), F32), pltpu.VMEM((tt + CONV_HALO, C), F32),
               pltpu.VMEM((tt, C), F32),
               pltpu.VMEM((7, tt + CONV_HALO - 8, C), F32), pltpu.VMEM((7, tt + CONV_HALO - 8, C), F32)]
    if hosted is not None:
        in_specs += [ANY_SPEC] * hn
        args += hosted.arrays
        out_specs += [ANY_SPEC] * hn
        out_shape += hosted.out_shapes()
        scratch += hosted.sem_shapes()
    res = pl.pallas_call(
        body, name="conv_bwd", grid=(nt,),
        in_specs=in_specs, out_specs=tuple(out_specs), out_shape=tuple(out_shape), scratch_shapes=scratch,
        compiler_params=_params(("arbitrary",)),
    )(*args)
    return res[0], res[1], list(res[2:])


def _attn_bwd(q, kv, d_o, lse, delta, dil, prev, final, name, hosted=None):
    S = q.shape[0]
    rows = _Rows(dil, S)
    nb = rows.nb
    out_dt = BF16 if final else F32
    have_prev = prev is not None
    hn = hosted.n if hosted is not None else 0

    def body(*refs):
        refs = list(refs)
        q_ref, do_ref, lse_ref, dl_ref, kvc_ref, kvp_ref = refs[:6]
        del refs[:6]
        if have_prev:
            pdq_ref, pdkv_ref = refs[:2]
            del refs[:2]
        h_ins = refs[:hn]
        dq_ref, dkv_ref = refs[hn:hn + 2]
        h_outs = refs[hn + 2:2 * hn + 2]
        carry, tbl = refs[2 * hn + 2:2 * hn + 4]
        h_sems = refs[2 * hn + 4:]
        n = pl.program_id(1)

        @pl.when((pl.program_id(0) == 0) & (n == 0))
        def _():
            if hosted is not None:
                hosted.start(h_ins, h_outs, h_sems)
            _fill_bias_table(tbl, rows)

        @pl.when(n == 0)
        def _():
            carry[...] = jnp.zeros_like(carry)

        @pl.when(n < nb)
        def _():
            kv2 = jnp.concatenate([_ld(kvp_ref), _ld(kvc_ref)], axis=0)
            lse_t, dl_t = _ld(lse_ref), _ld(dl_ref)
            lo_mask = lax.broadcasted_iota(jnp.int32, (2 * BLK, LANES), 1) < HEAD_DIM
            halves = [jnp.zeros((2 * BLK, LANES), F32) for _ in range(4)]
            for hk in range(N_KV_HEADS):
                k_lo, k_hi, v_lo, v_hi = _head_operands(kv2, hk, lo_mask)
                cols = [slice(b * LANES, (b + 1) * LANES) for b in (2 * hk, 2 * hk + 1)]
                q2 = jnp.concatenate([_ld(q_ref, cols[0]), _ld(q_ref, cols[1])], axis=0).astype(BF16)
                do2 = jnp.concatenate([_ld(do_ref, cols[0]), _ld(do_ref, cols[1])], axis=0).astype(BF16)
                dq2 = jnp.zeros((2 * BLK, LANES), F32)
                dks, dvs = [], []
                for which, (kk, vv) in enumerate(((k_lo, v_lo), (k_hi, v_hi))):
                    h0, h1 = 4 * hk + which, 4 * hk + 2 + which
                    s = _nt(q2, kk) + _bias2(tbl, n, h0, h1)
                    lse2 = jnp.concatenate([lse_t[:, h0:h0 + 1], lse_t[:, h1:h1 + 1]], axis=0)
                    dl2 = jnp.concatenate([dl_t[:, h0:h0 + 1], dl_t[:, h1:h1 + 1]], axis=0)
                    p = jnp.exp(s - lse2)
                    ds = (p * (_nt(do2, vv) - dl2)).astype(BF16)
                    dq2 = dq2 + jnp.dot(ds, kk, preferred_element_type=F32)
                    dks.append(_tn(ds, q2))
                    dvs.append(_tn(p.astype(BF16), do2))
                dk_sum = jnp.where(lo_mask, dks[0], dks[1])
                dv_sum = jnp.where(lo_mask, dvs[0], dvs[1])
                for jp in range(2):
                    dq_blk = dq2[jp * BLK:(jp + 1) * BLK]
                    if have_prev:
                        dq_blk = dq_blk + _ld(pdq_ref, cols[jp])
                    if final:
                        dq_blk = dq_blk * (HEAD_DIM ** -0.5)
                    _st(dq_ref, dq_blk.astype(out_dt), cols[jp])
                half, pos = hk // 2, hk % 2
                here = lo_mask if pos == 0 else jnp.logical_not(lo_mask)
                dk_tot = dk_sum + pltpu.roll(dk_sum, HEAD_DIM, axis=1)
                dv_tot = dv_sum + pltpu.roll(dv_sum, HEAD_DIM, axis=1)
                halves[half] = halves[half] + jnp.where(here, dk_tot, 0.0)
                halves[2 + half] = halves[2 + half] + jnp.where(here, dv_tot, 0.0)
            for b in range(4):
                cols = slice(b * LANES, (b + 1) * LANES)
                done = carry[:, cols] + halves[b][0:BLK, :]
                if have_prev:
                    done = done + _ld(pdkv_ref, cols)
                _st(dkv_ref, done.astype(out_dt), cols)
                carry[:, cols] = halves[b][BLK:, :]

        @pl.when(n == nb)
        def _():
            done = carry[...]
            if have_prev:
                done = done + _ld(pdkv_ref)
            _st(dkv_ref, done.astype(out_dt))

        if hosted is not None:
            @pl.when((pl.program_id(0) == dil - 1) & (n == nb))
            def _():
                hosted.finish(h_ins, h_outs, h_sems)

    cur = lambda n: jnp.minimum(n, nb - 1)
    behind = lambda n: jnp.maximum(n - 1, 0)
    in_specs = [rows.spec(ATT_W, cur), rows.spec(ATT_W, cur), rows.spec(LANES, cur), rows.spec(LANES, cur),
                rows.spec(2 * KV_W, cur), rows.spec(2 * KV_W, behind)]
    args = [rows.of(q), rows.of(d_o), rows.of(lse), rows.of(delta), rows.of(kv), rows.of(kv)]
    if have_prev:
        in_specs += [rows.spec(ATT_W, cur), rows.spec(2 * KV_W, behind)]
        args += [rows.of(prev[0]), rows.of(prev[1])]
    out_specs = [rows.spec(ATT_W, cur), rows.spec(2 * KV_W, behind)]
    out_shape = [jax.ShapeDtypeStruct(rows.view + (ATT_W,), out_dt),
                 jax.ShapeDtypeStruct(rows.view + (2 * KV_W,), out_dt)]
    scratch = [pltpu.VMEM((BLK, 2 * KV_W), F32), pltpu.VMEM((2 * N_Q_HEADS, BLK, 2 * BLK), F32)]
    if hosted is not None:
        in_specs += [ANY_SPEC] * hn
        args += hosted.arrays
        out_specs += [ANY_SPEC] * hn
        out_shape += hosted.out_shapes()
        scratch += hosted.sem_shapes()
    res = pl.pallas_call(
        body, name=name, grid=(dil, nb + 1),
        in_specs=in_specs, out_specs=tuple(out_specs), out_shape=tuple(out_shape), scratch_shapes=scratch,
        compiler_params=_params(("arbitrary", "arbitrary")),
    )(*args)
    return (res[0].reshape(S, ATT_W), res[1].reshape(S, 2 * KV_W)), list(res[2:])


def _dh(segments, w_in, x, dx2, g, hosted=None, tm=1024, tk=512):
    S, D = x.shape
    ns = len(segments)
    counts = [a.shape[1] // tk for a, _ in segments]
    starts = [sum(counts[:s]) for s in range(ns)]
    nk = sum(counts)
    hn = hosted.n if hosted is not None else 0

    def body(*refs):
        seg_refs = refs[:ns]
        w_ref, x_ref, dx2_ref, g_ref = refs[ns:ns + 4]
        h_ins = refs[ns + 4:ns + 4 + hn]
        gx_ref, gng_ref = refs[ns + 4 + hn:ns + 6 + hn]
        h_outs = refs[ns + 6 + hn:ns + 6 + 2 * hn]
        acc = refs[ns + 6 + 2 * hn]
        h_sems = refs[ns + 7 + 2 * hn:]
        i, k = pl.program_id(0), pl.program_id(1)

        @pl.when((i == 0) & (k == 0))
        def _():
            gng_ref[...] = jnp.zeros_like(gng_ref)
            if hosted is not None:
                hosted.start(h_ins, h_outs, h_sems)

        @pl.when(k == 0)
        def _():
            acc[...] = jnp.zeros_like(acc)

        for s in range(ns):
            @pl.when((k >= starts[s]) & (k < starts[s] + counts[s]))
            def _(s=s):
                t = seg_refs[s][...]
                if segments[s][1]:
                    t = _perm_rows(t, True)
                acc[...] += _nt(t, w_ref[...])

        @pl.when(k == nk - 1)
        def _():
            dh = acc[...]
            xf = x_ref[...]
            r = lax.rsqrt(jnp.mean(xf * xf, axis=-1, keepdims=True) + NORM_EPS)
            nrm = xf * r
            gng_ref[...] += jnp.sum(dh * nrm, axis=0, keepdims=True)
            dn = dh * g_ref[...]
            gx_ref[...] = dx2_ref[...] + r * (dn - nrm * jnp.mean(dn * nrm, axis=-1, keepdims=True))

        if hosted is not None:
            @pl.when((i == S // tm - 1) & (k == nk - 1))
            def _():
                hosted.finish(h_ins, h_outs, h_sems)

    row = pl.BlockSpec((tm, D), lambda i, k: (i, 0))
    vec = pl.BlockSpec((1, D), lambda i, k: (0, 0))
    in_specs = [pl.BlockSpec((tm, tk), lambda i, k, s=s: (i, jnp.clip(k - starts[s], 0, counts[s] - 1)))
                for s in range(ns)]
    in_specs += [pl.BlockSpec((D, tk), lambda i, k: (0, k)), row, row, vec]
    args = [a for a, _ in segments] + [w_in, x, dx2, g]
    out_specs = [row, vec]
    out_shape = [jax.ShapeDtypeStruct((S, D), F32), jax.ShapeDtypeStruct((1, D), F32)]
    scratch = [pltpu.VMEM((tm, D), F32)]
    if hosted is not None:
        in_specs += [ANY_SPEC] * hn
        args += hosted.arrays
        out_specs += [ANY_SPEC] * hn
        out_shape += hosted.out_shapes()
        scratch += hosted.sem_shapes()
    res = pl.pallas_call(
        body, name="dh", grid=(S // tm, nk),
        in_specs=in_specs, out_specs=tuple(out_specs), out_shape=tuple(out_shape), scratch_shapes=scratch,
        compiler_params=_params(("arbitrary", "arbitrary")),
    )(*args)
    return res[0], res[1], list(res[2:])


def _tn_matmul(a, b, name, tm=512):
    M, K = a.shape
    N = b.shape[1]
    tn = min(N, 1024)

    def body(a_ref, b_ref, o_ref):
        @pl.when(pl.program_id(1) == 0)
        def _():
            o_ref[...] = jnp.zeros_like(o_ref)

        o_ref[...] += _tn(a_ref[...], b_ref[...])

    return pl.pallas_call(
        body, name=name, grid=(N // tn, M // tm),
        in_specs=[pl.BlockSpec((tm, K), lambda j, m: (m, 0)), pl.BlockSpec((tm, tn), lambda j, m: (m, j))],
        out_specs=pl.BlockSpec((K, tn), lambda j, m: (0, j)),
        out_shape=jax.ShapeDtypeStruct((K, N), F32),
        compiler_params=_params(("parallel", "arbitrary")),
    )(a, b)


def _adamw(parts, w, m, v, name, tr=None, split=None):
    R, C = w.shape
    tr = R if tr is None else tr
    parts = [parts] if split is None else list(parts)
    npar = len(parts)

    def total(p_ref):
        g = p_ref[0].astype(F32)
        for dev in range(1, N_DEV):
            g = g + p_ref[dev].astype(F32)
        return g

    def body(*refs):
        w_ref, m_ref, v_ref, g_out, d_out, m_out, v_out = refs[npar:]
        if split is None:
            g = total(refs[0])
        else:
            g = jnp.where(_mesh_pos()[3] < split, total(refs[0]), total(refs[1]))
        mn = ADAM_B1 * m_ref[...] + (1.0 - ADAM_B1) * g
        vn = ADAM_B2 * v_ref[...] + (1.0 - ADAM_B2) * (g * g)
        m_hat = mn / (1.0 - ADAM_B1 ** ADAM_STEP)
        v_hat = vn / (1.0 - ADAM_B2 ** ADAM_STEP)
        g_out[...] = g
        d_out[...] = -ADAM_LR * (m_hat / (jnp.sqrt(v_hat) + ADAM_EPS) + ADAM_WD * w_ref[...])
        m_out[...] = mn
        v_out[...] = vn

    blk = pl.BlockSpec((tr, C), lambda i: (i, 0))
    shp = jax.ShapeDtypeStruct((R, C), F32)
    return pl.pallas_call(
        body, name=name, grid=(R // tr,),
        in_specs=[pl.BlockSpec((N_DEV, tr, C), lambda i: (0, i, 0))] * npar + [blk, blk, blk],
        out_specs=(blk, blk, blk, blk), out_shape=(shp, shp, shp, shp),
        compiler_params=_params(("parallel",)),
    )(*parts, w, m, v)


def _local_step(x, target, norm_g, w_in, conv_w, conv_b, ln_g, ln_b, w_out, gf, exchanges=None):
    ex_out, ex_att, ex_conv = exchanges if exchanges is not None else (None, None, None)
    att_cols = ATT_W + 2 * KV_W
    q, h_rm = _inproj(x, norm_g, w_in, 0, ATT_W, F32, HEAD_DIM ** -0.5, True, True, "inproj_q")
    kv = _inproj(x, norm_g, w_in, ATT_W // 512, 2 * KV_W, F32, 1.0, False, True, "inproj_kv")
    a_gate = _inproj(x, norm_g, w_in, att_cols // 512, ATT_W, F32, 1.0, False, True, "inproj_a_gate")
    gates, h = _inproj(x, norm_g, w_in, (att_cols + ATT_W) // 512, w_in.shape[1] - att_cols - ATT_W, F32, 1.0,
                       True, False, "inproj_conv")

    merged = None
    for idx, (_, dil) in enumerate(reversed(PATTERNS)):
        merged = _attn_fwd(q, kv, dil, "attn_fwd_d%d" % dil, merged,
                           a_gate if idx == len(PATTERNS) - 1 else None)
    o, lse, y_att = merged
    conv_out, y_conv = _conv_fwd(gates, conv_w, conv_b, ln_g, ln_b)
    dx2, dxb, loss_cols, g_gf = _outproj_loss(x, y_att, y_conv, w_out, gf, target)

    d_o, d_a_gate, delta, dxb_rm = _dy_att(dxb, w_out, a_gate, o)
    g_w_out = jnp.concatenate([_tn_matmul(y_att, dxb_rm, "gw_out_att"), _tn_matmul(y_conv, dxb, "gw_out_conv")],
                              axis=0)
    acc, out_parts = None, []
    for idx, (_, dil) in enumerate(reversed(PATTERNS)):
        hosted = ex_out(g_w_out) if (idx == 0 and ex_out is not None) else None
        acc, outs = _attn_bwd(q, kv, d_o, lse, delta, dil, acc, idx == len(PATTERNS) - 1, "attn_bwd_d%d" % dil,
                              hosted)
        out_parts += outs
    dq, dkv = acc
    g_q, g_kv, g_a = (_tn_matmul(h_rm, dq, "gw_in_q"), _tn_matmul(h_rm, dkv, "gw_in_kv"),
                      _tn_matmul(h_rm, d_a_gate, "gw_in_a_gate"))

    d_c_gate, d_conv, g_ln_g, g_ln_b, g_conv_b = _dy_conv(dxb, w_out, gates, conv_out, ln_g, ln_b)
    dgates, g_conv_w, att_parts = _conv_bwd(d_conv, gates, d_c_gate, conv_w,
                                            ex_att(g_q, g_kv, g_a) if ex_att is not None else None)
    g_c = _tn_matmul(h, dgates, "gw_in_conv")
    grad_x, g_norm_g, conv_parts = _dh(
        [(dq, True), (dkv, True), (d_a_gate, True), (dgates, False)], w_in, x, dx2, norm_g,
        ex_conv(g_a, g_c, g_conv_w) if ex_conv is not None else None)
    small = (g_norm_g, g_conv_b, g_ln_g, g_ln_b, g_gf, loss_cols)
    return grad_x, (g_q, g_kv, g_a, g_c), g_w_out, g_conv_w, small, (out_parts, att_parts, conv_parts)


def kernel(x, norm_g, w_in, conv_w, conv_b, conv_ln_g, conv_ln_b, w_out, final_norm_g, loss_target, m_norm_g, m_w_in, m_conv_w, m_conv_b, m_conv_ln_g, m_conv_ln_b, m_w_out, m_final_norm_g, v_norm_g, v_w_in, v_conv_w, v_conv_b, v_conv_ln_g, v_conv_ln_b, v_w_out, v_final_norm_g):
    S, D = x.shape[1], x.shape[2]
    win_sh, wout_sh, cw_sh = w_in[0], w_out[0], conv_w[0]
    cols_sh, rows_sh, ch_sh = win_sh.shape[1], wout_sh.shape[0], cw_sh.shape[1]

    win_all, wout_all, cw_all = _gather_two_level(
        [win_sh.astype(BF16), wout_sh.astype(BF16), cw_sh], "gather_weights")
    w_in_full = win_all.transpose(1, 0, 2).reshape(D, N_DEV * cols_sh)
    w_out_full = wout_all.reshape(N_DEV * rows_sh, D)
    conv_w_full = cw_all.transpose(1, 0, 2).reshape(CONV_K, N_DEV * ch_sh)
    conv_w_full = jnp.pad(conv_w_full, ((0, CONV_HALO - CONV_K), (0, 0)))
    gf = final_norm_g.reshape(1, D)

    first = -(-(ATT_W + 2 * KV_W) // cols_sh)
    a_off = first * cols_sh - (ATT_W + 2 * KV_W)
    assert 0 <= a_off <= ATT_W

    def pieces(g, n):
        return g.reshape(D, n, cols_sh).transpose(1, 0, 2).astype(BF16)

    def ex_out(g_w_out):
        return _Exchange([g_w_out.reshape(N_DEV, rows_sh, D).astype(BF16)], [(0, N_DEV)])

    def ex_att(g_q, g_kv, g_a):
        return _Exchange([pieces(jnp.concatenate([g_q, g_kv, g_a[:, :a_off]], axis=1), first)], [(0, first)])

    def ex_conv(g_a, g_c, g_conv_w):
        return _Exchange(
            [pieces(jnp.concatenate([g_a[:, a_off:], g_c], axis=1), N_DEV - first),
             g_conv_w[:CONV_K].reshape(CONV_K, N_DEV, ch_sh).transpose(1, 0, 2)],
            [(first, N_DEV), (0, N_DEV)])

    grad_x, _, _, _, small, parts = _local_step(
        x[0], loss_target[0], norm_g, w_in_full, conv_w_full, conv_b, conv_ln_g, conv_ln_b, w_out_full, gf,
        (ex_out, ex_att, ex_conv))
    (wout_parts,), (win_parts_lo,), (win_parts_hi, cw_parts) = parts

    small_pack = jnp.concatenate(list(small) + [jnp.zeros((2, D), F32)], axis=0)
    small_parts, = _exchange([small_pack], [None], "gather_small")

    upd_win = _adamw((win_parts_lo, win_parts_hi), win_sh, m_w_in[0], v_w_in[0], "adamw_w_in", tr=256, split=first)
    upd_wout = _adamw(wout_parts, wout_sh, m_w_out[0], v_w_out[0], "adamw_w_out", tr=128)
    upd_cw = _adamw(cw_parts, cw_sh, m_conv_w[0], v_conv_w[0], "adamw_conv_w")
    zeros3 = jnp.zeros((3, D), F32)
    stack = lambda a, b, c, d_, e: jnp.concatenate([a, b, c, d_, e.reshape(1, D), zeros3], axis=0)
    upd_small = _adamw(
        small_parts,
        stack(norm_g, conv_b, conv_ln_g, conv_ln_b, final_norm_g),
        stack(m_norm_g, m_conv_b, m_conv_ln_g, m_conv_ln_b, m_final_norm_g),
        stack(v_norm_g, v_conv_b, v_conv_ln_g, v_conv_ln_b, v_final_norm_g) + jnp.concatenate(
            [jnp.zeros((5, D), F32), jnp.ones((3, D), F32)], axis=0),
        "adamw_small")

    loss = 0.5 / D * jnp.sum(upd_small[0][5])

    def outputs(kind):
        sm = upd_small[kind]
        return [sm[0:1], upd_win[kind][None], upd_cw[kind][None], sm[1:2], sm[2:3], sm[3:4],
                upd_wout[kind][None], sm[4]]

    return (loss, grad_x[None], *outputs(0), *outputs(1), *outputs(2), *outputs(3))
```

```python
import jax
import jax.numpy as jnp
from jax import lax
from jax.experimental import pallas as pl
from jax.experimental.pallas import tpu as pltpu

F32 = jnp.float32
BF16 = jnp.bfloat16

HEAD_DIM = 64
N_KV_HEADS = 4
N_Q_HEADS = 16
ATT_W = 1024
KV_W = 256
CONV_K = 31
CONV_HALO = 32
PATTERNS = ((128, 1), (512, 4), (2048, 16))
BLK = 128
LANES = 128
NORM_EPS = 1e-6
LN_EPS = 1e-5
NEG = -1e30
N_DEV = 8
ADAM_LR, ADAM_B1, ADAM_B2, ADAM_EPS, ADAM_WD, ADAM_STEP = 0.001, 0.9, 0.999, 1e-08, 0.01, 10
VMEM_LIMIT = 48 * 1024 * 1024
SLOPES = tuple(2.0 ** (-8.0 * (h + 1) / N_Q_HEADS) for h in range(N_Q_HEADS))
MESH = pl.DeviceIdType.MESH


def _params(sem):
    return pltpu.CompilerParams(dimension_semantics=sem, vmem_limit_bytes=VMEM_LIMIT)


def _sigmoid(v):
    return 1.0 / (1.0 + jnp.exp(-v))


def _silu_and_grad(v):
    s = _sigmoid(v)
    return v * s, s * (1.0 + v * (1.0 - s))


ANY_SPEC = pl.BlockSpec(memory_space=pl.ANY)


def _mesh_pos():
    x, y, c = lax.axis_index("x"), lax.axis_index("y"), lax.axis_index("c")
    return x, y, c, 4 * x + 2 * y + c


def _flipped(k, x, y, c):
    px = 1 - x if k & 4 else x
    py = 1 - y if k & 2 else y
    pc = 1 - c if k & 1 else c
    return (px, py, pc), 4 * px + 2 * py + pc


class _Exchange:
    def __init__(self, arrays, dests):
        self.arrays, self.dests, self.n = list(arrays), list(dests), len(arrays)

    def out_shapes(self):
        return [jax.ShapeDtypeStruct((N_DEV,) + a.shape[-2:], a.dtype) for a in self.arrays]

    def sem_shapes(self):
        return [pltpu.SemaphoreType.DMA((self.n, N_DEV - 1)), pltpu.SemaphoreType.DMA((self.n, N_DEV - 1)),
                pltpu.SemaphoreType.DMA((self.n,))]

    def _when(self, a, dev, fn):
        if self.dests[a] is None:
            fn()
        else:
            lo, hi = self.dests[a]
            pl.when((dev >= lo) & (dev < hi))(fn)

    def _mine(self, ins, a, dev):
        return ins[a] if self.dests[a] is None else ins[a].at[dev - self.dests[a][0]]

    def _copy(self, ins, outs, sems, a, k, src_dev, slot, target):
        return pltpu.make_async_remote_copy(
            src_ref=self._mine(ins, a, src_dev), dst_ref=outs[a].at[slot],
            send_sem=sems[0].at[a, k - 1], recv_sem=sems[1].at[a, k - 1],
            device_id=target, device_id_type=MESH)

    def start(self, ins, outs, sems):
        x, y, c, me = _mesh_pos()
        for a in range(self.n):
            self._when(a, me, lambda a=a: pltpu.make_async_copy(
                self._mine(ins, a, me), outs[a].at[me], sems[2].at[a]).start())
            for k in range(1, N_DEV):
                target, peer = _flipped(k, x, y, c)
                self._when(a, peer, lambda a=a, k=k, target=target, peer=peer: self._copy(
                    ins, outs, sems, a, k, peer, me, target).start())

    def finish(self, ins, outs, sems):
        x, y, c, me = _mesh_pos()
        lo0 = [0 if d is None else d[0] for d in self.dests]
        for a in range(self.n):
            for k in range(1, N_DEV):
                target, peer = _flipped(k, x, y, c)
                self._when(a, me, lambda a=a, k=k, peer=peer: self._copy(
                    ins, outs, sems, a, k, lo0[a], peer, (x, y, c)).wait_recv())
            for k in range(1, N_DEV):
                target, peer = _flipped(k, x, y, c)
                self._when(a, peer, lambda a=a, k=k, target=target, peer=peer: self._copy(
                    ins, outs, sems, a, k, peer, me, target).wait_send())
            self._when(a, me, lambda a=a: pltpu.make_async_copy(
                self._mine(ins, a, me), outs[a].at[me], sems[2].at[a]).wait())


def _exchange(arrays, dests, name):
    ex = _Exchange(arrays, dests)
    na = ex.n

    def body(*refs):
        ins, outs, sems = refs[:na], refs[na:2 * na], refs[2 * na:]
        ex.start(ins, outs, sems)
        ex.finish(ins, outs, sems)

    return pl.pallas_call(
        body, name=name, out_shape=tuple(ex.out_shapes()),
        in_specs=[ANY_SPEC] * na, out_specs=tuple([ANY_SPEC] * na), scratch_shapes=ex.sem_shapes(),
    )(*arrays)


def _gather_two_level(arrays, name):
    na = len(arrays)

    def body(*refs):
        ins, outs = refs[:na], refs[na:2 * na]
        send_sems, recv_sems, loc_sems = refs[2 * na:]
        x, y, c, me = _mesh_pos()
        sibling = (x, y, 1 - c)
        chips = [(1 - x, y), (x, 1 - y), (1 - x, 1 - y)]

        def slot(px, py, pc):
            return 4 * px + 2 * py + pc

        def copy(a, k, src, block, to):
            return pltpu.make_async_remote_copy(
                src_ref=src, dst_ref=outs[a].at[slot(*block)], send_sem=send_sems.at[a, k], recv_sem=recv_sems.at[a, k],
                device_id=to, device_id_type=MESH)

        local = [pltpu.make_async_copy(ins[a], outs[a].at[me], loc_sems.at[a]) for a in range(na)]
        for cp in local:
            cp.start()
        started = []
        for a in range(na):
            started.append(copy(a, 0, ins[a], (x, y, c), sibling))
            started += [copy(a, 1 + j, ins[a], (x, y, c), (*chip, c)) for j, chip in enumerate(chips)]
        for cp in started:
            cp.start()
        for j, chip in enumerate(chips):
            for a in range(na):
                copy(a, 1 + j, ins[a], (*chip, c), (x, y, c)).wait_recv()
                fwd = copy(a, 4 + j, outs[a].at[slot(*chip, c)], (*chip, c), sibling)
                fwd.start()
                started.append(fwd)
        for a in range(na):
            copy(a, 0, ins[a], sibling, (x, y, c)).wait_recv()
            for j, chip in enumerate(chips):
                copy(a, 4 + j, ins[a], (*chip, 1 - c), (x, y, c)).wait_recv()
        for cp in started:
            cp.wait_send()
        for cp in local:
            cp.wait()

    return pl.pallas_call(
        body, name=name,
        out_shape=tuple(jax.ShapeDtypeStruct((N_DEV,) + a.shape, a.dtype) for a in arrays),
        in_specs=[ANY_SPEC] * na, out_specs=tuple([ANY_SPEC] * na),
        scratch_shapes=[pltpu.SemaphoreType.DMA((na, N_DEV - 1)), pltpu.SemaphoreType.DMA((na, N_DEV - 1)),
                        pltpu.SemaphoreType.DMA((na,))],
    )(*arrays)


CHUNK = 128
RESIDUES = 16
PER_RES = CHUNK // RESIDUES


def _perm_rows(tile, inverse):
    a = lax.broadcasted_iota(jnp.int32, (CHUNK, CHUNK), 0)
    b = lax.broadcasted_iota(jnp.int32, (CHUNK, CHUNK), 1)
    if inverse:
        a, b = b, a
    p = jnp.where(a == PER_RES * (b % RESIDUES) + b // RESIDUES, 1.0, 0.0).astype(BF16)
    parts = [jnp.dot(p, tile[c * CHUNK:(c + 1) * CHUNK], preferred_element_type=F32)
             for c in range(tile.shape[0] // CHUNK)]
    return jnp.concatenate(parts, axis=0).astype(BF16)


class _Rows:
    def __init__(self, dil, S):
        nc = S // CHUNK
        self.dil = dil
        if dil == 1:
            self.view, self.block, self.nb = (nc, CHUNK), (None, CHUNK), nc
            self.index = lambda r, b: (b, 0, 0)
        elif dil == 4:
            self.view, self.block, self.nb = (nc, 4, 4, PER_RES), (4, 4, None, PER_RES), nc // 4
            self.index = lambda r, b: (b, 0, r, 0, 0)
        elif dil == RESIDUES:
            self.view, self.block, self.nb = (nc, RESIDUES, PER_RES), (RESIDUES, None, PER_RES), nc // RESIDUES
            self.index = lambda r, b: (b, r, 0, 0)
        else:
            raise NotImplementedError(dil)

    def of(self, a):
        return a.reshape(self.view + (a.shape[-1],))

    def spec(self, width, which_block):
        return pl.BlockSpec(self.block + (width,), lambda r, n: self.index(r, which_block(n)))

    def pos(self, row):
        if self.dil == 1:
            return (row % PER_RES) * RESIDUES + row // PER_RES
        if self.dil == 4:
            return (row // 32) * 32 + (row % PER_RES) * 4 + (row % 32) // PER_RES
        return row


def _ld(ref, cols=slice(None)):
    v = ref[(slice(None),) * (len(ref.shape) - 1) + (cols,)]
    return v.reshape(BLK, v.shape[-1])


def _st(ref, val, cols=slice(None)):
    ref[(slice(None),) * (len(ref.shape) - 1) + (cols,)] = val.reshape(ref.shape[:-1] + (val.shape[-1],))


def _inproj(x, g, w, col_block, ncols, out_dtype, scale, emit_h, perm, name, tm=1024, tn=512):
    S, D = x.shape

    def body(x_ref, g_ref, w_ref, *rest):
        if emit_h:
            o_ref, h_out, h_scr = rest
        else:
            o_ref, h_scr = rest

        @pl.when(pl.program_id(1) == 0)
        def _():
            xf = x_ref[...]
            r = lax.rsqrt(jnp.mean(xf * xf, axis=-1, keepdims=True) + NORM_EPS)
            h = (xf * r * g_ref[...]).astype(BF16)
            if perm:
                h = _perm_rows(h, False)
            h_scr[...] = h
            if emit_h:
                h_out[...] = h

        acc = jnp.dot(h_scr[...], w_ref[...], preferred_element_type=F32)
        if scale != 1.0:
            acc = acc * scale
        o_ref[...] = acc.astype(out_dtype)

    out_shape = [jax.ShapeDtypeStruct((S, ncols), out_dtype)]
    out_specs = [pl.BlockSpec((tm, tn), lambda i, j: (i, j))]
    if emit_h:
        out_shape.append(jax.ShapeDtypeStruct((S, D), BF16))
        out_specs.append(pl.BlockSpec((tm, D), lambda i, j: (i, 0)))
    res = pl.pallas_call(
        body, name=name, grid=(S // tm, ncols // tn),
        in_specs=[pl.BlockSpec((tm, D), lambda i, j: (i, 0)),
                  pl.BlockSpec((1, D), lambda i, j: (0, 0)),
                  pl.BlockSpec((D, tn), lambda i, j: (0, col_block + j))],
        out_specs=tuple(out_specs), out_shape=tuple(out_shape),
        scratch_shapes=[pltpu.VMEM((tm, D), BF16)],
        compiler_params=_params(("parallel", "arbitrary")),
    )(x, g, w)
    return res if emit_h else res[0]


def _fill_bias_table(tbl, rows):
    qi = lax.broadcasted_iota(jnp.int32, (BLK, 2 * BLK), 0)
    kj = lax.broadcasted_iota(jnp.int32, (BLK, 2 * BLK), 1)
    dist = rows.pos(qi) - rows.pos(kj % BLK) + jnp.where(kj < BLK, BLK, 0)
    inside = (dist >= 0) & (dist <= BLK)
    negd = (dist * (-rows.dil)).astype(F32)
    for f, valid in enumerate((inside & (kj >= BLK), inside)):
        for h in range(N_Q_HEADS):
            tbl[f * N_Q_HEADS + h] = jnp.where(valid, SLOPES[h] * negd, NEG)


def _bias2(tbl, n, h0, h1):
    base = jnp.where(n == 0, 0, N_Q_HEADS)
    return jnp.concatenate([tbl[base + h0], tbl[base + h1]], axis=0)


def _head_operands(kv2, hk, lo_mask):
    half, pos = hk // 2, hk % 2
    out = []
    for base in (0, KV_W):
        t = kv2[:, base + half * LANES: base + (half + 1) * LANES]
        sw = pltpu.roll(t, HEAD_DIM, axis=1)
        at_lo, at_hi = (t, sw) if pos == 0 else (sw, t)
        out.append(jnp.where(lo_mask, at_lo, 0.0).astype(BF16))
        out.append(jnp.where(lo_mask, 0.0, at_hi).astype(BF16))
    return out


def _nt(a, b):
    return lax.dot_general(a, b, (((1,), (1,)), ((), ())), preferred_element_type=F32)


def _tn(a, b):
    return lax.dot_general(a, b, (((0,), (0,)), ((), ())), preferred_element_type=F32)


def _attn_fwd(q, kv, dil, name, prev=None, gate=None):
    S = q.shape[0]
    rows = _Rows(dil, S)
    nb = rows.nb
    have_prev, last = prev is not None, gate is not None

    def body(*refs):
        refs = list(refs)
        q_ref, kvc_ref, kvp_ref = refs[:3]
        del refs[:3]
        if have_prev:
            po_ref, pl_ref = refs[:2]
            del refs[:2]
        if last:
            gate_ref = refs.pop(0)
        o_ref, lse_ref = refs[:2]
        y_ref = refs[2] if last else None
        tbl = refs[-1]
        n = pl.program_id(1)

        @pl.when((pl.program_id(0) == 0) & (n == 0))
        def _():
            _fill_bias_table(tbl, rows)

        kv2 = jnp.concatenate([_ld(kvp_ref), _ld(kvc_ref)], axis=0)
        lo_mask = lax.broadcasted_iota(jnp.int32, (2 * BLK, LANES), 1) < HEAD_DIM
        lane = lax.broadcasted_iota(jnp.int32, (BLK, LANES), 1)
        stats = jnp.zeros((BLK, LANES), F32)
        for hk in range(N_KV_HEADS):
            k_lo, k_hi, v_lo, v_hi = _head_operands(kv2, hk, lo_mask)
            cols = [slice(b * LANES, (b + 1) * LANES) for b in (2 * hk, 2 * hk + 1)]
            q2 = jnp.concatenate([_ld(q_ref, cols[0]), _ld(q_ref, cols[1])], axis=0).astype(BF16)
            o2 = jnp.zeros((2 * BLK, LANES), F32)
            for which, (kk, vv) in enumerate(((k_lo, v_lo), (k_hi, v_hi))):
                h0, h1 = 4 * hk + which, 4 * hk + 2 + which
                s = _nt(q2, kk) + _bias2(tbl, n, h0, h1)
                m = jnp.max(s, axis=1, keepdims=True)
                p = jnp.exp(s - m)
                l = jnp.sum(p, axis=1, keepdims=True)
                o2 = o2 + jnp.dot(p.astype(BF16), vv, preferred_element_type=F32) * (1.0 / l)
                lse = m + jnp.log(l)
                stats = jnp.where(lane == h0, lse[0:BLK], stats)
                stats = jnp.where(lane == h1, lse[BLK:], stats)
            _st(o_ref, o2[0:BLK], cols[0])
            _st(o_ref, o2[BLK:], cols[1])
        if have_prev:
            before = _ld(pl_ref)
            top = jnp.maximum(before, stats)
            e_old, e_new = jnp.exp(before - top), jnp.exp(stats - top)
            total = e_old + e_new
            stats = top + jnp.log(total)
            inv = 1.0 / total
            w_old, w_new = e_old * inv, e_new * inv
        if have_prev or last:
            lo = lane < HEAD_DIM
            for blk in range(ATT_W // LANES):
                cols = slice(blk * LANES, (blk + 1) * LANES)
                o_blk = _ld(o_ref, cols)
                if have_prev:
                    pick = lambda w: jnp.where(lo, w[:, 2 * blk:2 * blk + 1], w[:, 2 * blk + 1:2 * blk + 2])
                    o_blk = o_blk * pick(w_new) + _ld(po_ref, cols) * pick(w_old)
                    _st(o_ref, o_blk, cols)
                if last:
                    a = _ld(gate_ref, cols)
                    _st(y_ref, (o_blk * (a * _sigmoid(a))).astype(BF16), cols)
        _st(lse_ref, stats)

    here = lambda n: n
    before_n = lambda n: jnp.maximum(n - 1, 0)
    in_specs = [rows.spec(ATT_W, here), rows.spec(2 * KV_W, here), rows.spec(2 * KV_W, before_n)]
    args = [rows.of(q), rows.of(kv), rows.of(kv)]
    if have_prev:
        in_specs += [rows.spec(ATT_W, here), rows.spec(LANES, here)]
        args += [rows.of(prev[0]), rows.of(prev[1])]
    out_specs = [rows.spec(ATT_W, here), rows.spec(LANES, here)]
    out_shape = [jax.ShapeDtypeStruct(rows.view + (ATT_W,), F32), jax.ShapeDtypeStruct(rows.view + (LANES,), F32)]
    if last:
        in_specs.append(rows.spec(ATT_W, here))
        args.append(rows.of(gate))
        out_specs.append(rows.spec(ATT_W, here))
        out_shape.append(jax.ShapeDtypeStruct(rows.view + (ATT_W,), BF16))
    res = pl.pallas_call(
        body, name=name, grid=(dil, nb),
        in_specs=in_specs, out_specs=tuple(out_specs), out_shape=tuple(out_shape),
        scratch_shapes=[pltpu.VMEM((2 * N_Q_HEADS, BLK, 2 * BLK), F32)],
        compiler_params=_params(("arbitrary", "arbitrary")),
    )(*args)
    return tuple(r.reshape(S, r.shape[-1]) for r in res)


def _shifted_copies(buf, phases):
    n = phases.shape[1]
    for b in range(1, 8):
        phases[b - 1] = buf[b:b + n, :]


def _window(buf, phases, start, cols):
    b = start % 8
    if b == 0:
        return buf[start:start + 8, cols]
    return phases[b - 1, start - b:start - b + 8, cols]


def _conv_fwd(gates, conv_w, conv_b, ln_g, ln_b, tt=256):
    S = gates.shape[0]
    C = conv_w.shape[1]
    hb = tt // CONV_HALO

    def body(val_ref, glu_ref, hval_ref, hglu_ref, gate_ref, w_ref, b_ref, g_ref, beta_ref,
             conv_ref, y_ref, hbuf, hph):
        i = pl.program_id(0)
        halo = hval_ref[...] * _sigmoid(hglu_ref[...])
        hbuf[0:CONV_HALO, :] = jnp.where(i > 0, halo, 0.0)
        hbuf[CONV_HALO:, :] = val_ref[...] * _sigmoid(glu_ref[...])
        _shifted_copies(hbuf, hph)
        for cb in range(C // LANES):
            cols = slice(cb * LANES, (cb + 1) * LANES)
            wj = [jnp.broadcast_to(w_ref[j:j + 1, cols], (8, LANES)) for j in range(CONV_K)]
            for rc in range(tt // 8):
                acc = jnp.zeros((8, LANES), F32)
                for j in range(CONV_K):
                    start = rc * 8 + CONV_HALO - (CONV_K - 1) + j
                    acc = acc + _window(hbuf, hph, start, cols) * wj[j]
                conv_ref[rc * 8:(rc + 1) * 8, cols] = acc
        cv = conv_ref[...] + b_ref[...]
        conv_ref[...] = cv
        mu = jnp.mean(cv, axis=-1, keepdims=True)
        xc = cv - mu
        var = jnp.mean(xc * xc, axis=-1, keepdims=True)
        ln = xc * lax.rsqrt(var + LN_EPS) * g_ref[...] + beta_ref[...]
        gt = gate_ref[...]
        y_ref[...] = (ln * _sigmoid(ln) * (gt * _sigmoid(gt))).astype(BF16)

    vec = pl.BlockSpec((1, C), lambda i: (0, 0))
    return pl.pallas_call(
        body, name="conv_fwd", grid=(S // tt,),
        in_specs=[pl.BlockSpec((tt, C), lambda i: (i, 0)),
                  pl.BlockSpec((tt, C), lambda i: (i, 1)),
                  pl.BlockSpec((CONV_HALO, C), lambda i: (jnp.maximum(i * hb - 1, 0), 0)),
                  pl.BlockSpec((CONV_HALO, C), lambda i: (jnp.maximum(i * hb - 1, 0), 1)),
                  pl.BlockSpec((tt, C), lambda i: (i, 2)),
                  pl.BlockSpec((CONV_HALO, C), lambda i: (0, 0)), vec, vec, vec],
        out_specs=(pl.BlockSpec((tt, C), lambda i: (i, 0)), pl.BlockSpec((tt, C), lambda i: (i, 0))),
        out_shape=(jax.ShapeDtypeStruct((S, C), F32), jax.ShapeDtypeStruct((S, C), BF16)),
        scratch_shapes=[pltpu.VMEM((tt + CONV_HALO, C), F32), pltpu.VMEM((7, tt + CONV_HALO - 8, C), F32)],
        compiler_params=_params(("parallel",)),
    )(gates, gates, gates, gates, gates, conv_w, conv_b, ln_g, ln_b)


def _outproj_loss(x, y_att, y_conv, w_out, gf, target, tm=512):
    S, D = x.shape
    E = y_att.shape[1]

    def body(x_ref, ya_ref, yc_ref, w_ref, gf_ref, t_ref, dx_ref, dxb_ref, loss_ref, ggf_ref):
        @pl.when(pl.program_id(0) == 0)
        def _():
            loss_ref[...] = jnp.zeros_like(loss_ref)
            ggf_ref[...] = jnp.zeros_like(ggf_ref)

        x2 = (x_ref[...] + jnp.dot(_perm_rows(ya_ref[...], True), w_ref[0:E, :], preferred_element_type=F32)
              + jnp.dot(yc_ref[...], w_ref[E:, :], preferred_element_type=F32))
        r = lax.rsqrt(jnp.mean(x2 * x2, axis=-1, keepdims=True) + NORM_EPS)
        nrm = x2 * r
        gfv = gf_ref[...]
        err = nrm * gfv - t_ref[...]
        loss_ref[...] += jnp.sum(err * err, axis=0, keepdims=True)
        dout = err * (1.0 / D)
        ggf_ref[...] += jnp.sum(dout * nrm, axis=0, keepdims=True)
        dn = dout * gfv
        dx2 = r * (dn - nrm * jnp.mean(dn * nrm, axis=-1, keepdims=True))
        dx_ref[...] = dx2
        dxb_ref[...] = dx2.astype(BF16)

    row = lambda w: pl.BlockSpec((tm, w), lambda i: (i, 0))
    vec = pl.BlockSpec((1, D), lambda i: (0, 0))
    return pl.pallas_call(
        body, name="outproj_loss", grid=(S // tm,),
        in_specs=[row(D), row(E), row(E), pl.BlockSpec((2 * E, D), lambda i: (0, 0)), vec, row(D)],
        out_specs=(row(D), row(D), vec, vec),
        out_shape=(jax.ShapeDtypeStruct((S, D), F32), jax.ShapeDtypeStruct((S, D), BF16),
                   jax.ShapeDtypeStruct((1, D), F32), jax.ShapeDtypeStruct((1, D), F32)),
        compiler_params=_params(("arbitrary",)),
    )(x, y_att, y_conv, w_out, gf, target)


def _split3(v):
    hi = v.astype(BF16)
    r1 = v - hi.astype(F32)
    mid = r1.astype(BF16)
    lo = (r1 - mid.astype(F32)).astype(BF16)
    return hi, mid, lo


def _dy_att(dxb, w_out, gates, o, tm=512):
    S, D = dxb.shape
    E = ATT_W

    def body(dx_ref, w_ref, a_ref, o_ref, do_ref, da_ref, dl_ref, dxr_ref):
        dxr = _perm_rows(dx_ref[...], False)
        dxr_ref[...] = dxr
        dya = _nt(dxr, w_ref[...])
        a = a_ref[...]
        ov = o_ref[...]
        sl, dsl = _silu_and_grad(a)
        d_o = dya * sl
        do_ref[...] = d_o
        da_ref[...] = (dya * ov * dsl).astype(BF16)
        ci = lax.broadcasted_iota(jnp.int32, (E, LANES), 0) // HEAD_DIM
        hi = lax.broadcasted_iota(jnp.int32, (E, LANES), 1)
        sel = jnp.where(ci == hi, 1.0, 0.0).astype(BF16)
        acc = jnp.zeros((tm, LANES), F32)
        for part in _split3(d_o * ov):
            acc = acc + jnp.dot(part, sel, preferred_element_type=F32)
        dl_ref[...] = acc

    row = lambda w: pl.BlockSpec((tm, w), lambda i: (i, 0))
    return pl.pallas_call(
        body, name="dy_att", grid=(S // tm,),
        in_specs=[row(D), pl.BlockSpec((E, D), lambda i: (0, 0)), row(E), row(E)],
        out_specs=(row(E), row(E), row(LANES), row(D)),
        out_shape=(jax.ShapeDtypeStruct((S, E), F32), jax.ShapeDtypeStruct((S, E), BF16),
                   jax.ShapeDtypeStruct((S, LANES), F32), jax.ShapeDtypeStruct((S, D), BF16)),
        compiler_params=_params(("parallel",)),
    )(dxb, w_out, gates, o)


def _dy_conv(dxb, w_out, gates, conv_out, ln_g, ln_b, tm=512):
    S, D = dxb.shape
    C = conv_out.shape[1]

    def body(dx_ref, w_ref, gate_ref, cv_ref, g_ref, beta_ref, dgate_ref, dconv_ref, gg_ref, gb_ref, gcb_ref):
        @pl.when(pl.program_id(0) == 0)
        def _():
            gg_ref[...] = jnp.zeros_like(gg_ref)
            gb_ref[...] = jnp.zeros_like(gb_ref)
            gcb_ref[...] = jnp.zeros_like(gcb_ref)

        dyc = _nt(dx_ref[...], w_ref[...])
        cv = cv_ref[...]
        mu = jnp.mean(cv, axis=-1, keepdims=True)
        xc = cv - mu
        rstd = lax.rsqrt(jnp.mean(xc * xc, axis=-1, keepdims=True) + LN_EPS)
        nrm = xc * rstd
        gv = g_ref[...]
        ln = nrm * gv + beta_ref[...]
        u, du = _silu_and_grad(ln)
        gt = gate_ref[...]
        g2, dg2 = _silu_and_grad(gt)
        dgate_ref[...] = (dyc * u * dg2).astype(BF16)
        d_ln = dyc * g2 * du
        gb_ref[...] += jnp.sum(d_ln, axis=0, keepdims=True)
        gg_ref[...] += jnp.sum(d_ln * nrm, axis=0, keepdims=True)
        dn = d_ln * gv
        d_conv = rstd * (dn - jnp.mean(dn, axis=-1, keepdims=True)
                         - nrm * jnp.mean(dn * nrm, axis=-1, keepdims=True))
        dconv_ref[...] = d_conv
        gcb_ref[...] += jnp.sum(d_conv, axis=0, keepdims=True)

    row = lambda w: pl.BlockSpec((tm, w), lambda i: (i, 0))
    vec = pl.BlockSpec((1, C), lambda i: (0, 0))
    return pl.pallas_call(
        body, name="dy_conv", grid=(S // tm,),
        in_specs=[row(D), pl.BlockSpec((C, D), lambda i: (1, 0)),
                  pl.BlockSpec((tm, C), lambda i: (i, 2)), row(C), vec, vec],
        out_specs=(row(C), row(C), vec, vec, vec),
        out_shape=(jax.ShapeDtypeStruct((S, C), BF16), jax.ShapeDtypeStruct((S, C), F32),
                   jax.ShapeDtypeStruct((1, C), F32), jax.ShapeDtypeStruct((1, C), F32),
                   jax.ShapeDtypeStruct((1, C), F32)),
        compiler_params=_params(("arbitrary",)),
    )(dxb, w_out, gates, conv_out, ln_g, ln_b)


def _conv_bwd(d_conv, gates, d_c_gate, conv_w, hosted=None, tt=256):
    S, C = d_conv.shape
    hb = tt // CONV_HALO
    nt = S // tt
    hn = hosted.n if hosted is not None else 0

    def body(*refs):
        dc_ref, dnext_ref, val_ref, glu_ref, hval_ref, hglu_ref, dg_ref, w_ref = refs[:8]
        h_ins = refs[8:8 + hn]
        out_ref, gw_ref = refs[8 + hn:10 + hn]
        h_outs = refs[10 + hn:10 + 2 * hn]
        hbuf, dbuf, dhbuf, hph, dph = refs[10 + 2 * hn:15 + 2 * hn]
        h_sems = refs[15 + 2 * hn:]
        i = pl.program_id(0)

        @pl.when(i == 0)
        def _():
            gw_ref[...] = jnp.zeros_like(gw_ref)
            if hosted is not None:
                hosted.start(h_ins, h_outs, h_sems)

        val = val_ref[...]
        sg = _sigmoid(glu_ref[...])
        halo = hval_ref[...] * _sigmoid(hglu_ref[...])
        hbuf[0:CONV_HALO, :] = jnp.where(i > 0, halo, 0.0)
        hbuf[CONV_HALO:, :] = val * sg
        dbuf[0:tt, :] = dc_ref[...]
        dbuf[tt:, :] = jnp.where(i < nt - 1, dnext_ref[...], 0.0)
        _shifted_copies(hbuf, hph)
        _shifted_copies(dbuf, dph)
        for cb in range(C // LANES):
            cols = slice(cb * LANES, (cb + 1) * LANES)
            wj = [jnp.broadcast_to(w_ref[j:j + 1, cols], (8, LANES)) for j in range(CONV_K)]
            for rc in range(tt // 8):
                acc = jnp.zeros((8, LANES), F32)
                for j in range(CONV_K):
                    acc = acc + _window(dbuf, dph, rc * 8 + (CONV_K - 1) - j, cols) * wj[j]
                dhbuf[rc * 8:(rc + 1) * 8, cols] = acc
            gacc = [jnp.zeros((8, LANES), F32) for _ in range(CONV_K)]
            for rc in range(tt // 8):
                dcur = dbuf[rc * 8:(rc + 1) * 8, cols]
                for j in range(CONV_K):
                    hs = rc * 8 + CONV_HALO - (CONV_K - 1) + j
                    gacc[j] = gacc[j] + dcur * _window(hbuf, hph, hs, cols)
            for j in range(CONV_K):
                gw_ref[j:j + 1, cols] += jnp.sum(gacc[j], axis=0, keepdims=True)
        d_h = dhbuf[...]
        out_ref[:, 0:C] = (d_h * sg).astype(BF16)
        out_ref[:, C:2 * C] = (d_h * val * sg * (1.0 - sg)).astype(BF16)
        out_ref[:, 2 * C:3 * C] = dg_ref[...]

        if hosted is not None:
            @pl.when(i == nt - 1)
            def _():
                hosted.finish(h_ins, h_outs, h_sems)

    tile = lambda col: pl.BlockSpec((tt, C), lambda i: (i, col))
    in_specs = [tile(0),
                pl.BlockSpec((CONV_HALO, C), lambda i: (jnp.minimum((i + 1) * hb, S // CONV_HALO - 1), 0)),
                tile(0), tile(1),
                pl.BlockSpec((CONV_HALO, C), lambda i: (jnp.maximum(i * hb - 1, 0), 0)),
                pl.BlockSpec((CONV_HALO, C), lambda i: (jnp.maximum(i * hb - 1, 0), 1)),
                tile(0),
                pl.BlockSpec((CONV_HALO, C), lambda i: (0, 0))]
    args = [d_conv, d_conv, gates, gates, gates, gates, d_c_gate, conv_w]
    out_specs = [pl.BlockSpec((tt, 3 * C), lambda i: (i, 0)), pl.BlockSpec((CONV_HALO, C), lambda i: (0, 0))]
    out_shape = [jax.ShapeDtypeStruct((S, 3 * C), BF16), jax.ShapeDtypeStruct((CONV_HALO, C), F32)]
    scratch = [pltpu.VMEM((tt + CONV_HALO, C), F32), pltpu.VMEM((tt + CONV_HALO, C), F32),
               pltpu.VMEM((tt, C), F32),
               pltpu.VMEM((7, tt + CONV_HALO - 8, C), F32), pltpu.VMEM((7, tt + CONV_HALO - 8, C), F32)]
    if hosted is not None:
        in_specs += [ANY_SPEC] * hn
        args += hosted.arrays
        out_specs += [ANY_SPEC] * hn
        out_shape += hosted.out_shapes()
        scratch += hosted.sem_shapes()
    res = pl.pallas_call(
        body, name="conv_bwd", grid=(nt,),
        in_specs=in_specs, out_specs=tuple(out_specs), out_shape=tuple(out_shape), scratch_shapes=scratch,
        compiler_params=_params(("arbitrary",)),
    )(*args)
    return res[0], res[1], list(res[2:])


def _attn_bwd(q, kv, d_o, lse, delta, dil, prev, final, name, hosted=None):
    S = q.shape[0]
    rows = _Rows(dil, S)
    nb = rows.nb
    out_dt = BF16 if final else F32
    have_prev = prev is not None
    hn = hosted.n if hosted is not None else 0

    def body(*refs):
        refs = list(refs)
        q_ref, do_ref, lse_ref, dl_ref, kvc_ref, kvp_ref = refs[:6]
        del refs[:6]
        if have_prev:
            pdq_ref, pdkv_ref = refs[:2]
            del refs[:2]
        h_ins = refs[:hn]
        dq_ref, dkv_ref = refs[hn:hn + 2]
        h_outs = refs[hn + 2:2 * hn + 2]
        carry, tbl = refs[2 * hn + 2:2 * hn + 4]
        h_sems = refs[2 * hn + 4:]
        n = pl.program_id(1)

        @pl.when((pl.program_id(0) == 0) & (n == 0))
        def _():
            if hosted is not None:
                hosted.start(h_ins, h_outs, h_sems)
            _fill_bias_table(tbl, rows)

        @pl.when(n == 0)
        def _():
            carry[...] = jnp.zeros_like(carry)

        @pl.when(n < nb)
        def _():
            kv2 = jnp.concatenate([_ld(kvp_ref), _ld(kvc_ref)], axis=0)
            lse_t, dl_t = _ld(lse_ref), _ld(dl_ref)
            lo_mask = lax.broadcasted_iota(jnp.int32, (2 * BLK, LANES), 1) < HEAD_DIM
            halves = [jnp.zeros((2 * BLK, LANES), F32) for _ in range(4)]
            for hk in range(N_KV_HEADS):
                k_lo, k_hi, v_lo, v_hi = _head_operands(kv2, hk, lo_mask)
                cols = [slice(b * LANES, (b + 1) * LANES) for b in (2 * hk, 2 * hk + 1)]
                q2 = jnp.concatenate([_ld(q_ref, cols[0]), _ld(q_ref, cols[1])], axis=0).astype(BF16)
                do2 = jnp.concatenate([_ld(do_ref, cols[0]), _ld(do_ref, cols[1])], axis=0).astype(BF16)
                dq2 = jnp.zeros((2 * BLK, LANES), F32)
                dks, dvs = [], []
                for which, (kk, vv) in enumerate(((k_lo, v_lo), (k_hi, v_hi))):
                    h0, h1 = 4 * hk + which, 4 * hk + 2 + which
                    s = _nt(q2, kk) + _bias2(tbl, n, h0, h1)
                    lse2 = jnp.concatenate([lse_t[:, h0:h0 + 1], lse_t[:, h1:h1 + 1]], axis=0)
                    dl2 = jnp.concatenate([dl_t[:, h0:h0 + 1], dl_t[:, h1:h1 + 1]], axis=0)
                    p = jnp.exp(s - lse2)
                    ds = (p * (_nt(do2, vv) - dl2)).astype(BF16)
                    dq2 = dq2 + jnp.dot(ds, kk, preferred_element_type=F32)
                    dks.append(_tn(ds, q2))
                    dvs.append(_tn(p.astype(BF16), do2))
                dk_sum = jnp.where(lo_mask, dks[0], dks[1])
                dv_sum = jnp.where(lo_mask, dvs[0], dvs[1])
                for jp in range(2):
                    dq_blk = dq2[jp * BLK:(jp + 1) * BLK]
                    if have_prev:
                        dq_blk = dq_blk + _ld(pdq_ref, cols[jp])
                    if final:
                        dq_blk = dq_blk * (HEAD_DIM ** -0.5)
                    _st(dq_ref, dq_blk.astype(out_dt), cols[jp])
                half, pos = hk // 2, hk % 2
                here = lo_mask if pos == 0 else jnp.logical_not(lo_mask)
                dk_tot = dk_sum + pltpu.roll(dk_sum, HEAD_DIM, axis=1)
                dv_tot = dv_sum + pltpu.roll(dv_sum, HEAD_DIM, axis=1)
                halves[half] = halves[half] + jnp.where(here, dk_tot, 0.0)
                halves[2 + half] = halves[2 + half] + jnp.where(here, dv_tot, 0.0)
            for b in range(4):
                cols = slice(b * LANES, (b + 1) * LANES)
                done = carry[:, cols] + halves[b][0:BLK, :]
                if have_prev:
                    done = done + _ld(pdkv_ref, cols)
                _st(dkv_ref, done.astype(out_dt), cols)
                carry[:, cols] = halves[b][BLK:, :]

        @pl.when(n == nb)
        def _():
            done = carry[...]
            if have_prev:
                done = done + _ld(pdkv_ref)
            _st(dkv_ref, done.astype(out_dt))

        if hosted is not None:
            @pl.when((pl.program_id(0) == dil - 1) & (n == nb))
            def _():
                hosted.finish(h_ins, h_outs, h_sems)

    cur = lambda n: jnp.minimum(n, nb - 1)
    behind = lambda n: jnp.maximum(n - 1, 0)
    in_specs = [rows.spec(ATT_W, cur), rows.spec(ATT_W, cur), rows.spec(LANES, cur), rows.spec(LANES, cur),
                rows.spec(2 * KV_W, cur), rows.spec(2 * KV_W, behind)]
    args = [rows.of(q), rows.of(d_o), rows.of(lse), rows.of(delta), rows.of(kv), rows.of(kv)]
    if have_prev:
        in_specs += [rows.spec(ATT_W, cur), rows.spec(2 * KV_W, behind)]
        args += [rows.of(prev[0]), rows.of(prev[1])]
    out_specs = [rows.spec(ATT_W, cur), rows.spec(2 * KV_W, behind)]
    out_shape = [jax.ShapeDtypeStruct(rows.view + (ATT_W,), out_dt),
                 jax.ShapeDtypeStruct(rows.view + (2 * KV_W,), out_dt)]
    scratch = [pltpu.VMEM((BLK, 2 * KV_W), F32), pltpu.VMEM((2 * N_Q_HEADS, BLK, 2 * BLK), F32)]
    if hosted is not None:
        in_specs += [ANY_SPEC] * hn
        args += hosted.arrays
        out_specs += [ANY_SPEC] * hn
        out_shape += hosted.out_shapes()
        scratch += hosted.sem_shapes()
    res = pl.pallas_call(
        body, name=name, grid=(dil, nb + 1),
        in_specs=in_specs, out_specs=tuple(out_specs), out_shape=tuple(out_shape), scratch_shapes=scratch,
        compiler_params=_params(("arbitrary", "arbitrary")),
    )(*args)
    return (res[0].reshape(S, ATT_W), res[1].reshape(S, 2 * KV_W)), list(res[2:])


def _dh(segments, w_in, x, dx2, g, hosted=None, tm=1024, tk=512):
    S, D = x.shape
    ns = len(segments)
    counts = [a.shape[1] // tk for a, _ in segments]
    starts = [sum(counts[:s]) for s in range(ns)]
    nk = sum(counts)
    hn = hosted.n if hosted is not None else 0

    def body(*refs):
        seg_refs = refs[:ns]
        w_ref, x_ref, dx2_ref, g_ref = refs[ns:ns + 4]
        h_ins = refs[ns + 4:ns + 4 + hn]
        gx_ref, gng_ref = refs[ns + 4 + hn:ns + 6 + hn]
        h_outs = refs[ns + 6 + hn:ns + 6 + 2 * hn]
        acc = refs[ns + 6 + 2 * hn]
        h_sems = refs[ns + 7 + 2 * hn:]
        i, k = pl.program_id(0), pl.program_id(1)

        @pl.when((i == 0) & (k == 0))
        def _():
            gng_ref[...] = jnp.zeros_like(gng_ref)
            if hosted is not None:
                hosted.start(h_ins, h_outs, h_sems)

        @pl.when(k == 0)
        def _():
            acc[...] = jnp.zeros_like(acc)

        for s in range(ns):
            @pl.when((k >= starts[s]) & (k < starts[s] + counts[s]))
            def _(s=s):
                t = seg_refs[s][...]
                if segments[s][1]:
                    t = _perm_rows(t, True)
                acc[...] += _nt(t, w_ref[...])

        @pl.when(k == nk - 1)
        def _():
            dh = acc[...]
            xf = x_ref[...]
            r = lax.rsqrt(jnp.mean(xf * xf, axis=-1, keepdims=True) + NORM_EPS)
            nrm = xf * r
            gng_ref[...] += jnp.sum(dh * nrm, axis=0, keepdims=True)
            dn = dh * g_ref[...]
            gx_ref[...] = dx2_ref[...] + r * (dn - nrm * jnp.mean(dn * nrm, axis=-1, keepdims=True))

        if hosted is not None:
            @pl.when((i == S // tm - 1) & (k == nk - 1))
            def _():
                hosted.finish(h_ins, h_outs, h_sems)

    row = pl.BlockSpec((tm, D), lambda i, k: (i, 0))
    vec = pl.BlockSpec((1, D), lambda i, k: (0, 0))
    in_specs = [pl.BlockSpec((tm, tk), lambda i, k, s=s: (i, jnp.clip(k - starts[s], 0, counts[s] - 1)))
                for s in range(ns)]
    in_specs += [pl.BlockSpec((D, tk), lambda i, k: (0, k)), row, row, vec]
    args = [a for a, _ in segments] + [w_in, x, dx2, g]
    out_specs = [row, vec]
    out_shape = [jax.ShapeDtypeStruct((S, D), F32), jax.ShapeDtypeStruct((1, D), F32)]
    scratch = [pltpu.VMEM((tm, D), F32)]
    if hosted is not None:
        in_specs += [ANY_SPEC] * hn
        args += hosted.arrays
        out_specs += [ANY_SPEC] * hn
        out_shape += hosted.out_shapes()
        scratch += hosted.sem_shapes()
    res = pl.pallas_call(
        body, name="dh", grid=(S // tm, nk),
        in_specs=in_specs, out_specs=tuple(out_specs), out_shape=tuple(out_shape), scratch_shapes=scratch,
        compiler_params=_params(("arbitrary", "arbitrary")),
    )(*args)
    return res[0], res[1], list(res[2:])


def _tn_matmul(a, b, name, tm=512):
    M, K = a.shape
    N = b.shape[1]
    tn = min(N, 1024)

    def body(a_ref, b_ref, o_ref):
        @pl.when(pl.program_id(1) == 0)
        def _():
            o_ref[...] = jnp.zeros_like(o_ref)

        o_ref[...] += _tn(a_ref[...], b_ref[...])

    return pl.pallas_call(
        body, name=name, grid=(N // tn, M // tm),
        in_specs=[pl.BlockSpec((tm, K), lambda j, m: (m, 0)), pl.BlockSpec((tm, tn), lambda j, m: (m, j))],
        out_specs=pl.BlockSpec((K, tn), lambda j, m: (0, j)),
        out_shape=jax.ShapeDtypeStruct((K, N), F32),
        compiler_params=_params(("parallel", "arbitrary")),
    )(a, b)


def _adamw(parts, w, m, v, name, tr=None, split=None):
    R, C = w.shape
    tr = R if tr is None else tr
    parts = [parts] if split is None else list(parts)
    npar = len(parts)

    def total(p_ref):
        g = p_ref[0].astype(F32)
        for dev in range(1, N_DEV):
            g = g + p_ref[dev].astype(F32)
        return g

    def body(*refs):
        w_ref, m_ref, v_ref, g_out, d_out, m_out, v_out = refs[npar:]
        if split is None:
            g = total(refs[0])
        else:
            g = jnp.where(_mesh_pos()[3] < split, total(refs[0]), total(refs[1]))
        mn = ADAM_B1 * m_ref[...] + (1.0 - ADAM_B1) * g
        vn = ADAM_B2 * v_ref[...] + (1.0 - ADAM_B2) * (g * g)
        m_hat = mn / (1.0 - ADAM_B1 ** ADAM_STEP)
        v_hat = vn / (1.0 - ADAM_B2 ** ADAM_STEP)
        g_out[...] = g
        d_out[...] = -ADAM_LR * (m_hat / (jnp.sqrt(v_hat) + ADAM_EPS) + ADAM_WD * w_ref[...])
        m_out[...] = mn
        v_out[...] = vn

    blk = pl.BlockSpec((tr, C), lambda i: (i, 0))
    shp = jax.ShapeDtypeStruct((R, C), F32)
    return pl.pallas_call(
        body, name=name, grid=(R // tr,),
        in_specs=[pl.BlockSpec((N_DEV, tr, C), lambda i: (0, i, 0))] * npar + [blk, blk, blk],
        out_specs=(blk, blk, blk, blk), out_shape=(shp, shp, shp, shp),
        compiler_params=_params(("parallel",)),
    )(*parts, w, m, v)


def _local_step(x, target, norm_g, w_in, conv_w, conv_b, ln_g, ln_b, w_out, gf, exchanges=None):
    ex_out, ex_att, ex_conv = exchanges if exchanges is not None else (None, None, None)
    att_cols = ATT_W + 2 * KV_W
    q, h_rm = _inproj(x, norm_g, w_in, 0, ATT_W, F32, HEAD_DIM ** -0.5, True, True, "inproj_q")
    kv = _inproj(x, norm_g, w_in, ATT_W // 512, 2 * KV_W, F32, 1.0, False, True, "inproj_kv")
    a_gate = _inproj(x, norm_g, w_in, att_cols // 512, ATT_W, F32, 1.0, False, True, "inproj_a_gate")
    gates, h = _inproj(x, norm_g, w_in, (att_cols + ATT_W) // 512, w_in.shape[1] - att_cols - ATT_W, F32, 1.0,
                       True, False, "inproj_conv")

    merged = None
    for idx, (_, dil) in enumerate(reversed(PATTERNS)):
        merged = _attn_fwd(q, kv, dil, "attn_fwd_d%d" % dil, merged,
                           a_gate if idx == len(PATTERNS) - 1 else None)
    o, lse, y_att = merged
    conv_out, y_conv = _conv_fwd(gates, conv_w, conv_b, ln_g, ln_b)
    dx2, dxb, loss_cols, g_gf = _outproj_loss(x, y_att, y_conv, w_out, gf, target)

    d_o, d_a_gate, delta, dxb_rm = _dy_att(dxb, w_out, a_gate, o)
    g_w_out = jnp.concatenate([_tn_matmul(y_att, dxb_rm, "gw_out_att"), _tn_matmul(y_conv, dxb, "gw_out_conv")],
                              axis=0)
    acc, out_parts = None, []
    for idx, (_, dil) in enumerate(reversed(PATTERNS)):
        hosted = ex_out(g_w_out) if (idx == 0 and ex_out is not None) else None
        acc, outs = _attn_bwd(q, kv, d_o, lse, delta, dil, acc, idx == len(PATTERNS) - 1, "attn_bwd_d%d" % dil,
                              hosted)
        out_parts += outs
    dq, dkv = acc
    g_q, g_kv, g_a = (_tn_matmul(h_rm, dq, "gw_in_q"), _tn_matmul(h_rm, dkv, "gw_in_kv"),
                      _tn_matmul(h_rm, d_a_gate, "gw_in_a_gate"))

    d_c_gate, d_conv, g_ln_g, g_ln_b, g_conv_b = _dy_conv(dxb, w_out, gates, conv_out, ln_g, ln_b)
    dgates, g_conv_w, att_parts = _conv_bwd(d_conv, gates, d_c_gate, conv_w,
                                            ex_att(g_q, g_kv, g_a) if ex_att is not None else None)
    g_c = _tn_matmul(h, dgates, "gw_in_conv")
    grad_x, g_norm_g, conv_parts = _dh(
        [(dq, True), (dkv, True), (d_a_gate, True), (dgates, False)], w_in, x, dx2, norm_g,
        ex_conv(g_a, g_c, g_conv_w) if ex_conv is not None else None)
    small = (g_norm_g, g_conv_b, g_ln_g, g_ln_b, g_gf, loss_cols)
    return grad_x, (g_q, g_kv, g_a, g_c), g_w_out, g_conv_w, small, (out_parts, att_parts, conv_parts)


def kernel(x, norm_g, w_in, conv_w, conv_b, conv_ln_g, conv_ln_b, w_out, final_norm_g, loss_target, m_norm_g, m_w_in, m_conv_w, m_conv_b, m_conv_ln_g, m_conv_ln_b, m_w_out, m_final_norm_g, v_norm_g, v_w_in, v_conv_w, v_conv_b, v_conv_ln_g, v_conv_ln_b, v_w_out, v_final_norm_g):
    S, D = x.shape[1], x.shape[2]
    win_sh, wout_sh, cw_sh = w_in[0], w_out[0], conv_w[0]
    cols_sh, rows_sh, ch_sh = win_sh.shape[1], wout_sh.shape[0], cw_sh.shape[1]

    win_all, wout_all, cw_all = _gather_two_level(
        [win_sh.astype(BF16), wout_sh.astype(BF16), cw_sh], "gather_weights")
    w_in_full = win_all.transpose(1, 0, 2).reshape(D, N_DEV * cols_sh)
    w_out_full = wout_all.reshape(N_DEV * rows_sh, D)
    conv_w_full = cw_all.transpose(1, 0, 2).reshape(CONV_K, N_DEV * ch_sh)
    conv_w_full = jnp.pad(conv_w_full, ((0, CONV_HALO - CONV_K), (0, 0)))
    gf = final_norm_g.reshape(1, D)

    first = -(-(ATT_W + 2 * KV_W) // cols_sh)
    a_off = first * cols_sh - (ATT_W + 2 * KV_W)
    assert 0 <= a_off <= ATT_W

    def pieces(g, n):
        return g.reshape(D, n, cols_sh).transpose(1, 0, 2).astype(BF16)

    def ex_out(g_w_out):
        return _Exchange([g_w_out.reshape(N_DEV, rows_sh, D).astype(BF16)], [(0, N_DEV)])

    def ex_att(g_q, g_kv, g_a):
        return _Exchange([pieces(jnp.concatenate([g_q, g_kv, g_a[:, :a_off]], axis=1), first)], [(0, first)])

    def ex_conv(g_a, g_c, g_conv_w):
        return _Exchange(
            [pieces(jnp.concatenate([g_a[:, a_off:], g_c], axis=1), N_DEV - first),
             g_conv_w[:CONV_K].reshape(CONV_K, N_DEV, ch_sh).transpose(1, 0, 2)],
            [(first, N_DEV), (0, N_DEV)])

    grad_x, _, _, _, small, parts = _local_step(
        x[0], loss_target[0], norm_g, w_in_full, conv_w_full, conv_b, conv_ln_g, conv_ln_b, w_out_full, gf,
        (ex_out, ex_att, ex_conv))
    (wout_parts,), (win_parts_lo,), (win_parts_hi, cw_parts) = parts

    small_pack = jnp.concatenate(list(small) + [jnp.zeros((2, D), F32)], axis=0)
    small_parts, = _exchange([small_pack], [None], "gather_small")

    upd_win = _adamw((win_parts_lo, win_parts_hi), win_sh, m_w_in[0], v_w_in[0], "adamw_w_in", tr=256, split=first)
    upd_wout = _adamw(wout_parts, wout_sh, m_w_out[0], v_w_out[0], "adamw_w_out", tr=128)
    upd_cw = _adamw(cw_parts, cw_sh, m_conv_w[0], v_conv_w[0], "adamw_conv_w")
    zeros3 = jnp.zeros((3, D), F32)
    stack = lambda a, b, c, d_, e: jnp.concatenate([a, b, c, d_, e.reshape(1, D), zeros3], axis=0)
    upd_small = _adamw(
        small_parts,
        stack(norm_g, conv_b, conv_ln_g, conv_ln_b, final_norm_g),
        stack(m_norm_g, m_conv_b, m_conv_ln_g, m_conv_ln_b, m_final_norm_g),
        stack(v_norm_g, v_conv_b, v_conv_ln_g, v_conv_ln_b, v_final_norm_g) + jnp.concatenate(
            [jnp.zeros((5, D), F32), jnp.ones((3, D), F32)], axis=0),
        "adamw_small")

    loss = 0.5 / D * jnp.sum(upd_small[0][5])

    def outputs(kind):
        sm = upd_small[kind]
        return [sm[0:1], upd_win[kind][None], upd_cw[kind][None], sm[1:2], sm[2:3], sm[3:4],
                upd_wout[kind][None], sm[4]]

    return (loss, grad_x[None], *outputs(0), *outputs(1), *outputs(2), *outputs(3))
```

```python
import jax
import jax.numpy as jnp
from jax import lax
from jax.experimental import pallas as pl
from jax.experimental.pallas import tpu as pltpu

F32 = jnp.float32
BF16 = jnp.bfloat16

HEAD_DIM = 64
N_KV_HEADS = 4
N_Q_HEADS = 16
ATT_W = 1024
KV_W = 256
CONV_K = 31
CONV_HALO = 32
PATTERNS = ((128, 1), (512, 4), (2048, 16))
BLK = 128
LANES = 128
NORM_EPS = 1e-6
LN_EPS = 1e-5
NEG = -1e30
N_DEV = 8
ADAM_LR, ADAM_B1, ADAM_B2, ADAM_EPS, ADAM_WD, ADAM_STEP = 0.001, 0.9, 0.999, 1e-08, 0.01, 10
VMEM_LIMIT = 48 * 1024 * 1024
SLOPES = tuple(2.0 ** (-8.0 * (h + 1) / N_Q_HEADS) for h in range(N_Q_HEADS))
MESH = pl.DeviceIdType.MESH


def _params(sem):
    return pltpu.CompilerParams(dimension_semantics=sem, vmem_limit_bytes=VMEM_LIMIT)


def _sigmoid(v):
    return 1.0 / (1.0 + jnp.exp(-v))


def _silu_and_grad(v):
    s = _sigmoid(v)
    return v * s, s * (1.0 + v * (1.0 - s))


ANY_SPEC = pl.BlockSpec(memory_space=pl.ANY)


def _mesh_pos():
    x, y, c = lax.axis_index("x"), lax.axis_index("y"), lax.axis_index("c")
    return x, y, c, 4 * x + 2 * y + c


def _flipped(k, x, y, c):
    px = 1 - x if k & 4 else x
    py = 1 - y if k & 2 else y
    pc = 1 - c if k & 1 else c
    return (px, py, pc), 4 * px + 2 * py + pc


class _Exchange:
    def __init__(self, arrays, dests):
        self.arrays, self.dests, self.n = list(arrays), list(dests), len(arrays)

    def out_shapes(self):
        return [jax.ShapeDtypeStruct((N_DEV,) + a.shape[-2:], a.dtype) for a in self.arrays]

    def sem_shapes(self):
        return [pltpu.SemaphoreType.DMA((self.n, N_DEV - 1)), pltpu.SemaphoreType.DMA((self.n, N_DEV - 1)),
                pltpu.SemaphoreType.DMA((self.n,))]

    def _when(self, a, dev, fn):
        if self.dests[a] is None:
            fn()
        else:
            lo, hi = self.dests[a]
            pl.when((dev >= lo) & (dev < hi))(fn)

    def _mine(self, ins, a, dev):
        return ins[a] if self.dests[a] is None else ins[a].at[dev - self.dests[a][0]]

    def _copy(self, ins, outs, sems, a, k, src_dev, slot, target):
        return pltpu.make_async_remote_copy(
            src_ref=self._mine(ins, a, src_dev), dst_ref=outs[a].at[slot],
            send_sem=sems[0].at[a, k - 1], recv_sem=sems[1].at[a, k - 1],
            device_id=target, device_id_type=MESH)

    def start(self, ins, outs, sems):
        x, y, c, me = _mesh_pos()
        for a in range(self.n):
            self._when(a, me, lambda a=a: pltpu.make_async_copy(
                self._mine(ins, a, me), outs[a].at[me], sems[2].at[a]).start())
            for k in range(1, N_DEV):
                target, peer = _flipped(k, x, y, c)
                self._when(a, peer, lambda a=a, k=k, target=target, peer=peer: self._copy(
                    ins, outs, sems, a, k, peer, me, target).start())

    def finish(self, ins, outs, sems):
        x, y, c, me = _mesh_pos()
        lo0 = [0 if d is None else d[0] for d in self.dests]
        for a in range(self.n):
            for k in range(1, N_DEV):
                target, peer = _flipped(k, x, y, c)
                self._when(a, me, lambda a=a, k=k, peer=peer: self._copy(
                    ins, outs, sems, a, k, lo0[a], peer, (x, y, c)).wait_recv())
            for k in range(1, N_DEV):
                target, peer = _flipped(k, x, y, c)
                self._when(a, peer, lambda a=a, k=k, target=target, peer=peer: self._copy(
                    ins, outs, sems, a, k, peer, me, target).wait_send())
            self._when(a, me, lambda a=a: pltpu.make_async_copy(
                self._mine(ins, a, me), outs[a].at[me], sems[2].at[a]).wait())


def _exchange(arrays, dests, name):
    ex = _Exchange(arrays, dests)
    na = ex.n

    def body(*refs):
        ins, outs, sems = refs[:na], refs[na:2 * na], refs[2 * na:]
        ex.start(ins, outs, sems)
        ex.finish(ins, outs, sems)

    return pl.pallas_call(
        body, name=name, out_shape=tuple(ex.out_shapes()),
        in_specs=[ANY_SPEC] * na, out_specs=tuple([ANY_SPEC] * na), scratch_shapes=ex.sem_shapes(),
    )(*arrays)


def _gather_two_level(arrays, name):
    na = len(arrays)

    def body(*refs):
        ins, outs = refs[:na], refs[na:2 * na]
        send_sems, recv_sems, loc_sems = refs[2 * na:]
        x, y, c, me = _mesh_pos()
        sibling = (x, y, 1 - c)
        chips = [(1 - x, y), (x, 1 - y), (1 - x, 1 - y)]

        def slot(px, py, pc):
            return 4 * px + 2 * py + pc

        def copy(a, k, src, block, to):
            return pltpu.make_async_remote_copy(
                src_ref=src, dst_ref=outs[a].at[slot(*block)], send_sem=send_sems.at[a, k], recv_sem=recv_sems.at[a, k],
                device_id=to, device_id_type=MESH)

        local = [pltpu.make_async_copy(ins[a], outs[a].at[me], loc_sems.at[a]) for a in range(na)]
        for cp in local:
            cp.start()
        started = []
        for a in range(na):
            started.append(copy(a, 0, ins[a], (x, y, c), sibling))
            started += [copy(a, 1 + j, ins[a], (x, y, c), (*chip, c)) for j, chip in enumerate(chips)]
        for cp in started:
            cp.start()
        for j, chip in enumerate(chips):
            for a in range(na):
                copy(a, 1 + j, ins[a], (*chip, c), (x, y, c)).wait_recv()
                fwd = copy(a, 4 + j, outs[a].at[slot(*chip, c)], (*chip, c), sibling)
                fwd.start()
                started.append(fwd)
        for a in range(na):
            copy(a, 0, ins[a], sibling, (x, y, c)).wait_recv()
            for j, chip in enumerate(chips):
                copy(a, 4 + j, ins[a], (*chip, 1 - c), (x, y, c)).wait_recv()
        for cp in started:
            cp.wait_send()
        for cp in local:
            cp.wait()

    return pl.pallas_call(
        body, name=name,
        out_shape=tuple(jax.ShapeDtypeStruct((N_DEV,) + a.shape, a.dtype) for a in arrays),
        in_specs=[ANY_SPEC] * na, out_specs=tuple([ANY_SPEC] * na),
        scratch_shapes=[pltpu.SemaphoreType.DMA((na, N_DEV - 1)), pltpu.SemaphoreType.DMA((na, N_DEV - 1)),
                        pltpu.SemaphoreType.DMA((na,))],
    )(*arrays)


CHUNK = 128
RESIDUES = 16
PER_RES = CHUNK // RESIDUES


def _perm_rows(tile, inverse):
    a = lax.broadcasted_iota(jnp.int32, (CHUNK, CHUNK), 0)
    b = lax.broadcasted_iota(jnp.int32, (CHUNK, CHUNK), 1)
    if inverse:
        a, b = b, a
    p = jnp.where(a == PER_RES * (b % RESIDUES) + b // RESIDUES, 1.0, 0.0).astype(BF16)
    parts = [jnp.dot(p, tile[c * CHUNK:(c + 1) * CHUNK], preferred_element_type=F32)
             for c in range(tile.shape[0] // CHUNK)]
    return jnp.concatenate(parts, axis=0).astype(BF16)


class _Rows:
    def __init__(self, dil, S):
        nc = S // CHUNK
        self.dil = dil
        if dil == 1:
            self.view, self.block, self.nb = (nc, CHUNK), (None, CHUNK), nc
            self.index = lambda r, b: (b, 0, 0)
        elif dil == 4:
            self.view, self.block, self.nb = (nc, 4, 4, PER_RES), (4, 4, None, PER_RES), nc // 4
            self.index = lambda r, b: (b, 0, r, 0, 0)
        elif dil == RESIDUES:
            self.view, self.block, self.nb = (nc, RESIDUES, PER_RES), (RESIDUES, None, PER_RES), nc // RESIDUES
            self.index = lambda r, b: (b, r, 0, 0)
        else:
            raise NotImplementedError(dil)

    def of(self, a):
        return a.reshape(self.view + (a.shape[-1],))

    def spec(self, width, which_block):
        return pl.BlockSpec(self.block + (width,), lambda r, n: self.index(r, which_block(n)))

    def pos(self, row):
        if self.dil == 1:
            return (row % PER_RES) * RESIDUES + row // PER_RES
        if self.dil == 4:
            return (row // 32) * 32 + (row % PER_RES) * 4 + (row % 32) // PER_RES
        return row


def _ld(ref, cols=slice(None)):
    v = ref[(slice(None),) * (len(ref.shape) - 1) + (cols,)]
    return v.reshape(BLK, v.shape[-1])


def _st(ref, val, cols=slice(None)):
    ref[(slice(None),) * (len(ref.shape) - 1) + (cols,)] = val.reshape(ref.shape[:-1] + (val.shape[-1],))


def _inproj(x, g, w, segments, tm=1024, tn=512):
    S, D = x.shape
    ns = len(segments)
    counts = [nc // tn for nc, _, _ in segments]
    starts = [sum(counts[:s]) for s in range(ns)]

    def body(x_ref, g_ref, w_ref, *rest):
        outs = rest[:ns]
        hrm_out, h_out, hrm_scr, h_scr = rest[ns:]
        j = pl.program_id(1)

        @pl.when(j == 0)
        def _():
            xf = x_ref[...]
            r = lax.rsqrt(jnp.mean(xf * xf, axis=-1, keepdims=True) + NORM_EPS)
            h = (xf * r * g_ref[...]).astype(BF16)
            hrm = _perm_rows(h, False)
            h_scr[...] = h
            hrm_scr[...] = hrm
            h_out[...] = h
            hrm_out[...] = hrm

        for s, (_, scale, rm) in enumerate(segments):
            @pl.when((j >= starts[s]) & (j < starts[s] + counts[s]))
            def _(s=s, scale=scale, rm=rm):
                acc = jnp.dot((hrm_scr if rm else h_scr)[...], w_ref[...], preferred_element_type=F32)
                outs[s][...] = acc * scale if scale != 1.0 else acc

    row = pl.BlockSpec((tm, D), lambda i, j: (i, 0))
    out_specs = [pl.BlockSpec((tm, tn), lambda i, j, s=s: (i, jnp.clip(j - starts[s], 0, counts[s] - 1)))
                 for s in range(ns)]
    out_shape = [jax.ShapeDtypeStruct((S, nc), F32) for nc, _, _ in segments]
    return pl.pallas_call(
        body, name="inproj", grid=(S // tm, sum(counts)),
        in_specs=[row, pl.BlockSpec((1, D), lambda i, j: (0, 0)), pl.BlockSpec((D, tn), lambda i, j: (0, j))],
        out_specs=tuple(out_specs + [row, row]),
        out_shape=tuple(out_shape + [jax.ShapeDtypeStruct((S, D), BF16)] * 2),
        scratch_shapes=[pltpu.VMEM((tm, D), BF16), pltpu.VMEM((tm, D), BF16)],
        compiler_params=_params(("arbitrary", "arbitrary")),
    )(x, g, w)


def _fill_bias_table(tbl, rows, keys_first=False):
    shape = (2 * BLK, BLK) if keys_first else (BLK, 2 * BLK)
    qi = lax.broadcasted_iota(jnp.int32, shape, 1 if keys_first else 0)
    kj = lax.broadcasted_iota(jnp.int32, shape, 0 if keys_first else 1)
    dist = rows.pos(qi) - rows.pos(kj % BLK) + jnp.where(kj < BLK, BLK, 0)
    inside = (dist >= 0) & (dist <= BLK)
    negd = (dist * (-rows.dil)).astype(F32)
    for f, valid in enumerate((inside & (kj >= BLK), inside)):
        for h in range(N_Q_HEADS):
            tbl[f * N_Q_HEADS + h] = jnp.where(valid, SLOPES[h] * negd, NEG)


def _bias2(tbl, n, h0, h1, axis=0):
    base = jnp.where(n == 0, 0, N_Q_HEADS)
    return jnp.concatenate([tbl[base + h0], tbl[base + h1]], axis=axis)


def _head_operands(kv2, hk, lo_mask):
    half, pos = hk // 2, hk % 2
    out = []
    for base in (0, KV_W):
        t = kv2[:, base + half * LANES: base + (half + 1) * LANES]
        sw = pltpu.roll(t, HEAD_DIM, axis=1)
        at_lo, at_hi = (t, sw) if pos == 0 else (sw, t)
        out.append(jnp.where(lo_mask, at_lo, 0.0).astype(BF16))
        out.append(jnp.where(lo_mask, 0.0, at_hi).astype(BF16))
    return out


def _nt(a, b):
    return lax.dot_general(a, b, (((1,), (1,)), ((), ())), preferred_element_type=F32)


def _tn(a, b):
    return lax.dot_general(a, b, (((0,), (0,)), ((), ())), preferred_element_type=F32)


def _attn_fwd(q, kv, dil, name, prev=None, gate=None):
    S = q.shape[0]
    rows = _Rows(dil, S)
    nb = rows.nb
    have_prev, last = prev is not None, gate is not None

    def body(*refs):
        refs = list(refs)
        q_ref, kvc_ref, kvp_ref = refs[:3]
        del refs[:3]
        if have_prev:
            po_ref, pl_ref = refs[:2]
            del refs[:2]
        if last:
            gate_ref = refs.pop(0)
        o_ref, lse_ref = refs[:2]
        y_ref = refs[2] if last else None
        tbl = refs[-1]
        n = pl.program_id(1)

        @pl.when((pl.program_id(0) == 0) & (n == 0))
        def _():
            _fill_bias_table(tbl, rows)

        kv2 = jnp.concatenate([_ld(kvp_ref), _ld(kvc_ref)], axis=0)
        lo_mask = lax.broadcasted_iota(jnp.int32, (2 * BLK, LANES), 1) < HEAD_DIM
        lane = lax.broadcasted_iota(jnp.int32, (BLK, LANES), 1)
        stats = jnp.zeros((BLK, LANES), F32)
        for hk in range(N_KV_HEADS):
            k_lo, k_hi, v_lo, v_hi = _head_operands(kv2, hk, lo_mask)
            cols = [slice(b * LANES, (b + 1) * LANES) for b in (2 * hk, 2 * hk + 1)]
            q2 = jnp.concatenate([_ld(q_ref, cols[0]), _ld(q_ref, cols[1])], axis=0).astype(BF16)
            o2 = jnp.zeros((2 * BLK, LANES), F32)
            for which, (kk, vv) in enumerate(((k_lo, v_lo), (k_hi, v_hi))):
                h0, h1 = 4 * hk + which, 4 * hk + 2 + which
                s = _nt(q2, kk) + _bias2(tbl, n, h0, h1)
                m = jnp.max(s, axis=1, keepdims=True)
                p = jnp.exp(s - m)
                l = jnp.sum(p, axis=1, keepdims=True)
                o2 = o2 + jnp.dot(p.astype(BF16), vv, preferred_element_type=F32) * (1.0 / l)
                lse = m + jnp.log(l)
                stats = jnp.where(lane == h0, lse[0:BLK], stats)
                stats = jnp.where(lane == h1, lse[BLK:], stats)
            _st(o_ref, o2[0:BLK], cols[0])
            _st(o_ref, o2[BLK:], cols[1])
        if have_prev:
            before = _ld(pl_ref)
            top = jnp.maximum(before, stats)
            e_old, e_new = jnp.exp(before - top), jnp.exp(stats - top)
            total = e_old + e_new
            stats = top + jnp.log(total)
            inv = 1.0 / total
            w_old, w_new = e_old * inv, e_new * inv
        if have_prev or last:
            lo = lane < HEAD_DIM
            for blk in range(ATT_W // LANES):
                cols = slice(blk * LANES, (blk + 1) * LANES)
                o_blk = _ld(o_ref, cols)
                if have_prev:
                    pick = lambda w: jnp.where(lo, w[:, 2 * blk:2 * blk + 1], w[:, 2 * blk + 1:2 * blk + 2])
                    o_blk = o_blk * pick(w_new) + _ld(po_ref, cols) * pick(w_old)
                    _st(o_ref, o_blk, cols)
                if last:
                    a = _ld(gate_ref, cols)
                    _st(y_ref, (o_blk * (a * _sigmoid(a))).astype(BF16), cols)
        _st(lse_ref, stats)

    here = lambda n: n
    before_n = lambda n: jnp.maximum(n - 1, 0)
    in_specs = [rows.spec(ATT_W, here), rows.spec(2 * KV_W, here), rows.spec(2 * KV_W, before_n)]
    args = [rows.of(q), rows.of(kv), rows.of(kv)]
    if have_prev:
        in_specs += [rows.spec(ATT_W, here), rows.spec(LANES, here)]
        args += [rows.of(prev[0]), rows.of(prev[1])]
    out_specs = [rows.spec(ATT_W, here), rows.spec(LANES, here)]
    out_shape = [jax.ShapeDtypeStruct(rows.view + (ATT_W,), F32), jax.ShapeDtypeStruct(rows.view + (LANES,), F32)]
    if last:
        in_specs.append(rows.spec(ATT_W, here))
        args.append(rows.of(gate))
        out_specs.append(rows.spec(ATT_W, here))
        out_shape.append(jax.ShapeDtypeStruct(rows.view + (ATT_W,), BF16))
    res = pl.pallas_call(
        body, name=name, grid=(dil, nb),
        in_specs=in_specs, out_specs=tuple(out_specs), out_shape=tuple(out_shape),
        scratch_shapes=[pltpu.VMEM((2 * N_Q_HEADS, BLK, 2 * BLK), F32)],
        compiler_params=_params(("arbitrary", "arbitrary")),
    )(*args)
    return tuple(r.reshape(S, r.shape[-1]) for r in res)


def _shifted_copies(buf, phases):
    n = phases.shape[1]
    for b in range(1, 8):
        phases[b - 1] = buf[b:b + n, :]


def _window(buf, phases, start, cols):
    b = start % 8
    if b == 0:
        return buf[start:start + 8, cols]
    return phases[b - 1, start - b:start - b + 8, cols]


def _conv_fwd(gates, conv_w, conv_b, ln_g, ln_b, tt=256):
    S = gates.shape[0]
    C = conv_w.shape[1]
    hb = tt // CONV_HALO

    def body(val_ref, glu_ref, hval_ref, hglu_ref, gate_ref, w_ref, b_ref, g_ref, beta_ref,
             conv_ref, y_ref, hbuf, hph):
        i = pl.program_id(0)
        halo = hval_ref[...] * _sigmoid(hglu_ref[...])
        hbuf[0:CONV_HALO, :] = jnp.where(i > 0, halo, 0.0)
        hbuf[CONV_HALO:, :] = val_ref[...] * _sigmoid(glu_ref[...])
        _shifted_copies(hbuf, hph)
        for cb in range(C // LANES):
            cols = slice(cb * LANES, (cb + 1) * LANES)
            wj = [jnp.broadcast_to(w_ref[j:j + 1, cols], (8, LANES)) for j in range(CONV_K)]
            for rc in range(tt // 8):
                acc = jnp.zeros((8, LANES), F32)
                for j in range(CONV_K):
                    start = rc * 8 + CONV_HALO - (CONV_K - 1) + j
                    acc = acc + _window(hbuf, hph, start, cols) * wj[j]
                conv_ref[rc * 8:(rc + 1) * 8, cols] = acc
        cv = conv_ref[...] + b_ref[...]
        conv_ref[...] = cv
        mu = jnp.mean(cv, axis=-1, keepdims=True)
        xc = cv - mu
        var = jnp.mean(xc * xc, axis=-1, keepdims=True)
        ln = xc * lax.rsqrt(var + LN_EPS) * g_ref[...] + beta_ref[...]
        gt = gate_ref[...]
        y_ref[...] = (ln * _sigmoid(ln) * (gt * _sigmoid(gt))).astype(BF16)

    vec = pl.BlockSpec((1, C), lambda i: (0, 0))
    return pl.pallas_call(
        body, name="conv_fwd", grid=(S // tt,),
        in_specs=[pl.BlockSpec((tt, C), lambda i: (i, 0)),
                  pl.BlockSpec((tt, C), lambda i: (i, 1)),
                  pl.BlockSpec((CONV_HALO, C), lambda i: (jnp.maximum(i * hb - 1, 0), 0)),
                  pl.BlockSpec((CONV_HALO, C), lambda i: (jnp.maximum(i * hb - 1, 0), 1)),
                  pl.BlockSpec((tt, C), lambda i: (i, 2)),
                  pl.BlockSpec((CONV_HALO, C), lambda i: (0, 0)), vec, vec, vec],
        out_specs=(pl.BlockSpec((tt, C), lambda i: (i, 0)), pl.BlockSpec((tt, C), lambda i: (i, 0))),
        out_shape=(jax.ShapeDtypeStruct((S, C), F32), jax.ShapeDtypeStruct((S, C), BF16)),
        scratch_shapes=[pltpu.VMEM((tt + CONV_HALO, C), F32), pltpu.VMEM((7, tt + CONV_HALO - 8, C), F32)],
        compiler_params=_params(("parallel",)),
    )(gates, gates, gates, gates, gates, conv_w, conv_b, ln_g, ln_b)


def _outproj_loss(x, y_att, y_conv, w_out, gf, target, tm=512):
    S, D = x.shape
    E = y_att.shape[1]

    def body(x_ref, ya_ref, yc_ref, w_ref, gf_ref, t_ref, dx_ref, dxb_ref, loss_ref, ggf_ref):
        @pl.when(pl.program_id(0) == 0)
        def _():
            loss_ref[...] = jnp.zeros_like(loss_ref)
            ggf_ref[...] = jnp.zeros_like(ggf_ref)

        x2 = (x_ref[...] + jnp.dot(_perm_rows(ya_ref[...], True), w_ref[0:E, :], preferred_element_type=F32)
              + jnp.dot(yc_ref[...], w_ref[E:, :], preferred_element_type=F32))
        r = lax.rsqrt(jnp.mean(x2 * x2, axis=-1, keepdims=True) + NORM_EPS)
        nrm = x2 * r
        gfv = gf_ref[...]
        err = nrm * gfv - t_ref[...]
        loss_ref[...] += jnp.sum(err * err, axis=0, keepdims=True)
        dout = err * (1.0 / D)
        ggf_ref[...] += jnp.sum(dout * nrm, axis=0, keepdims=True)
        dn = dout * gfv
        dx2 = r * (dn - nrm * jnp.mean(dn * nrm, axis=-1, keepdims=True))
        dx_ref[...] = dx2
        dxb_ref[...] = dx2.astype(BF16)

    row = lambda w: pl.BlockSpec((tm, w), lambda i: (i, 0))
    vec = pl.BlockSpec((1, D), lambda i: (0, 0))
    return pl.pallas_call(
        body, name="outproj_loss", grid=(S // tm,),
        in_specs=[row(D), row(E), row(E), pl.BlockSpec((2 * E, D), lambda i: (0, 0)), vec, row(D)],
        out_specs=(row(D), row(D), vec, vec),
        out_shape=(jax.ShapeDtypeStruct((S, D), F32), jax.ShapeDtypeStruct((S, D), BF16),
                   jax.ShapeDtypeStruct((1, D), F32), jax.ShapeDtypeStruct((1, D), F32)),
        compiler_params=_params(("arbitrary",)),
    )(x, y_att, y_conv, w_out, gf, target)


def _split3(v):
    hi = v.astype(BF16)
    r1 = v - hi.astype(F32)
    mid = r1.astype(BF16)
    lo = (r1 - mid.astype(F32)).astype(BF16)
    return hi, mid, lo


def _dy_att(dxb, w_out, gates, o, tm=512):
    S, D = dxb.shape
    E = ATT_W

    def body(dx_ref, w_ref, a_ref, o_ref, do_ref, da_ref, dl_ref, dxr_ref):
        dxr = _perm_rows(dx_ref[...], False)
        dxr_ref[...] = dxr
        dya = _nt(dxr, w_ref[...])
        a = a_ref[...]
        ov = o_ref[...]
        sl, dsl = _silu_and_grad(a)
        d_o = dya * sl
        do_ref[...] = d_o
        da_ref[...] = (dya * ov * dsl).astype(BF16)
        ci = lax.broadcasted_iota(jnp.int32, (E, LANES), 0) // HEAD_DIM
        hi = lax.broadcasted_iota(jnp.int32, (E, LANES), 1)
        sel = jnp.where(ci == hi, 1.0, 0.0).astype(BF16)
        acc = jnp.zeros((tm, LANES), F32)
        for part in _split3(d_o * ov):
            acc = acc + jnp.dot(part, sel, preferred_element_type=F32)
        dl_ref[...] = acc

    row = lambda w: pl.BlockSpec((tm, w), lambda i: (i, 0))
    return pl.pallas_call(
        body, name="dy_att", grid=(S // tm,),
        in_specs=[row(D), pl.BlockSpec((E, D), lambda i: (0, 0)), row(E), row(E)],
        out_specs=(row(E), row(E), row(LANES), row(D)),
        out_shape=(jax.ShapeDtypeStruct((S, E), F32), jax.ShapeDtypeStruct((S, E), BF16),
                   jax.ShapeDtypeStruct((S, LANES), F32), jax.ShapeDtypeStruct((S, D), BF16)),
        compiler_params=_params(("parallel",)),
    )(dxb, w_out, gates, o)


def _dy_conv(dxb, w_out, gates, conv_out, ln_g, ln_b, tm=512):
    S, D = dxb.shape
    C = conv_out.shape[1]

    def body(dx_ref, w_ref, gate_ref, cv_ref, g_ref, beta_ref, dgate_ref, dconv_ref, gg_ref, gb_ref, gcb_ref):
        @pl.when(pl.program_id(0) == 0)
        def _():
            gg_ref[...] = jnp.zeros_like(gg_ref)
            gb_ref[...] = jnp.zeros_like(gb_ref)
            gcb_ref[...] = jnp.zeros_like(gcb_ref)

        dyc = _nt(dx_ref[...], w_ref[...])
        cv = cv_ref[...]
        mu = jnp.mean(cv, axis=-1, keepdims=True)
        xc = cv - mu
        rstd = lax.rsqrt(jnp.mean(xc * xc, axis=-1, keepdims=True) + LN_EPS)
        nrm = xc * rstd
        gv = g_ref[...]
        ln = nrm * gv + beta_ref[...]
        u, du = _silu_and_grad(ln)
        gt = gate_ref[...]
        g2, dg2 = _silu_and_grad(gt)
        dgate_ref[...] = (dyc * u * dg2).astype(BF16)
        d_ln = dyc * g2 * du
        gb_ref[...] += jnp.sum(d_ln, axis=0, keepdims=True)
        gg_ref[...] += jnp.sum(d_ln * nrm, axis=0, keepdims=True)
        dn = d_ln * gv
        d_conv = rstd * (dn - jnp.mean(dn, axis=-1, keepdims=True)
                         - nrm * jnp.mean(dn * nrm, axis=-1, keepdims=True))
        dconv_ref[...] = d_conv
        gcb_ref[...] += jnp.sum(d_conv, axis=0, keepdims=True)

    row = lambda w: pl.BlockSpec((tm, w), lambda i: (i, 0))
    vec = pl.BlockSpec((1, C), lambda i: (0, 0))
    return pl.pallas_call(
        body, name="dy_conv", grid=(S // tm,),
        in_specs=[row(D), pl.BlockSpec((C, D), lambda i: (1, 0)),
                  pl.BlockSpec((tm, C), lambda i: (i, 2)), row(C), vec, vec],
        out_specs=(row(C), row(C), vec, vec, vec),
        out_shape=(jax.ShapeDtypeStruct((S, C), BF16), jax.ShapeDtypeStruct((S, C), F32),
                   jax.ShapeDtypeStruct((1, C), F32), jax.ShapeDtypeStruct((1, C), F32),
                   jax.ShapeDtypeStruct((1, C), F32)),
        compiler_params=_params(("arbitrary",)),
    )(dxb, w_out, gates, conv_out, ln_g, ln_b)


def _conv_bwd(d_conv, gates, d_c_gate, conv_w, hosted=None, tt=256):
    S, C = d_conv.shape
    hb = tt // CONV_HALO
    nt = S // tt
    hn = hosted.n if hosted is not None else 0

    def body(*refs):
        dc_ref, dnext_ref, val_ref, glu_ref, hval_ref, hglu_ref, dg_ref, w_ref = refs[:8]
        h_ins = refs[8:8 + hn]
        out_ref, gw_ref = refs[8 + hn:10 + hn]
        h_outs = refs[10 + hn:10 + 2 * hn]
        hbuf, dbuf, dhbuf, hph, dph = refs[10 + 2 * hn:15 + 2 * hn]
        h_sems = refs[15 + 2 * hn:]
        i = pl.program_id(0)

        @pl.when(i == 0)
        def _():
            gw_ref[...] = jnp.zeros_like(gw_ref)
            if hosted is not None:
                hosted.start(h_ins, h_outs, h_sems)

        val = val_ref[...]
        sg = _sigmoid(glu_ref[...])
        halo = hval_ref[...] * _sigmoid(hglu_ref[...])
        hbuf[0:CONV_HALO, :] = jnp.where(i > 0, halo, 0.0)
        hbuf[CONV_HALO:, :] = val * sg
        dbuf[0:tt, :] = dc_ref[...]
        dbuf[tt:, :] = jnp.where(i < nt - 1, dnext_ref[...], 0.0)
        _shifted_copies(hbuf, hph)
        _shifted_copies(dbuf, dph)
        for cb in range(C // LANES):
            cols = slice(cb * LANES, (cb + 1) * LANES)
            wj = [jnp.broadcast_to(w_ref[j:j + 1, cols], (8, LANES)) for j in range(CONV_K)]
            for rc in range(tt // 8):
                acc = jnp.zeros((8, LANES), F32)
                for j in range(CONV_K):
                    acc = acc + _window(dbuf, dph, rc * 8 + (CONV_K - 1) - j, cols) * wj[j]
                dhbuf[rc * 8:(rc + 1) * 8, cols] = acc
            gacc = [jnp.zeros((8, LANES), F32) for _ in range(CONV_K)]
            for rc in range(tt // 8):
                dcur = dbuf[rc * 8:(rc + 1) * 8, cols]
                for j in range(CONV_K):
                    hs = rc * 8 + CONV_HALO - (CONV_K - 1) + j
                    gacc[j] = gacc[j] + dcur * _window(hbuf, hph, hs, cols)
            for j in range(CONV_K):
                gw_ref[j:j + 1, cols] += jnp.sum(gacc[j], axis=0, keepdims=True)
        d_h = dhbuf[...]
        out_ref[:, 0:C] = (d_h * sg).astype(BF16)
        out_ref[:, C:2 * C] = (d_h * val * sg * (1.0 - sg)).astype(BF16)
        out_ref[:, 2 * C:3 * C] = dg_ref[...]

        if hosted is not None:
            @pl.when(i == nt - 1)
            def _():
                hosted.finish(h_ins, h_outs, h_sems)

    tile = lambda col: pl.BlockSpec((tt, C), lambda i: (i, col))
    in_specs = [tile(0),
                pl.BlockSpec((CONV_HALO, C), lambda i: (jnp.minimum((i + 1) * hb, S // CONV_HALO - 1), 0)),
                tile(0), tile(1),
                pl.BlockSpec((CONV_HALO, C), lambda i: (jnp.maximum(i * hb - 1, 0), 0)),
                pl.BlockSpec((CONV_HALO, C), lambda i: (jnp.maximum(i * hb - 1, 0), 1)),
                tile(0),
                pl.BlockSpec((CONV_HALO, C), lambda i: (0, 0))]
    args = [d_conv, d_conv, gates, gates, gates, gates, d_c_gate, conv_w]
    out_specs = [pl.BlockSpec((tt, 3 * C), lambda i: (i, 0)), pl.BlockSpec((CONV_HALO, C), lambda i: (0, 0))]
    out_shape = [jax.ShapeDtypeStruct((S, 3 * C), BF16), jax.ShapeDtypeStruct((CONV_HALO, C), F32)]
    scratch = [pltpu.VMEM((tt + CONV_HALO, C), F32), pltpu.VMEM((tt + CONV_HALO, C), F32),
               pltpu.VMEM((tt, C), F32),
               pltpu.VMEM((7, tt + CONV_HALO - 8, C), F32), pltpu.VMEM((7, tt + CONV_HALO - 8, C), F32)]
    if hosted is not None:
        in_specs += [ANY_SPEC] * hn
        args += hosted.arrays
        out_specs += [ANY_SPEC] * hn
        out_shape += hosted.out_shapes()
        scratch += hosted.sem_shapes()
    res = pl.pallas_call(
        body, name="conv_bwd", grid=(nt,),
        in_specs=in_specs, out_specs=tuple(out_specs), out_shape=tuple(out_shape), scratch_shapes=scratch,
        compiler_params=_params(("arbitrary",)),
    )(*args)
    return res[0], res[1], list(res[2:])


def _attn_bwd(q, kv, d_o, lse, delta, dil, prev, final, name, hosted=None):
    S = q.shape[0]
    rows = _Rows(dil, S)
    nb = rows.nb
    out_dt = BF16 if final else F32
    have_prev = prev is not None
    hn = hosted.n if hosted is not None else 0

    def body(*refs):
        refs = list(refs)
        q_ref, do_ref, lse_ref, dl_ref, kvc_ref, kvp_ref = refs[:6]
        del refs[:6]
        if have_prev:
            pdq_ref, pdkv_ref = refs[:2]
            del refs[:2]
        h_ins = refs[:hn]
        dq_ref, dkv_ref = refs[hn:hn + 2]
        h_outs = refs[hn + 2:2 * hn + 2]
        carry, tbl = refs[2 * hn + 2:2 * hn + 4]
        h_sems = refs[2 * hn + 4:]
        n = pl.program_id(1)

        @pl.when((pl.program_id(0) == 0) & (n == 0))
        def _():
            if hosted is not None:
                hosted.start(h_ins, h_outs, h_sems)
            _fill_bias_table(tbl, rows, keys_first=True)

        @pl.when(n == 0)
        def _():
            carry[...] = jnp.zeros_like(carry)

        @pl.when(n < nb)
        def _():
            kv2 = jnp.concatenate([_ld(kvp_ref), _ld(kvc_ref)], axis=0)
            lse_t, dl_t = _ld(lse_ref).T, _ld(dl_ref).T
            lo_mask = lax.broadcasted_iota(jnp.int32, (2 * BLK, LANES), 1) < HEAD_DIM
            halves = [jnp.zeros((2 * BLK, LANES), F32) for _ in range(4)]
            for hk in range(N_KV_HEADS):
                k_lo, k_hi, v_lo, v_hi = _head_operands(kv2, hk, lo_mask)
                cols = [slice(b * LANES, (b + 1) * LANES) for b in (2 * hk, 2 * hk + 1)]
                q2 = jnp.concatenate([_ld(q_ref, cols[0]), _ld(q_ref, cols[1])], axis=0).astype(BF16)
                do2 = jnp.concatenate([_ld(do_ref, cols[0]), _ld(do_ref, cols[1])], axis=0).astype(BF16)
                dq2 = jnp.zeros((2 * BLK, LANES), F32)
                dks, dvs = [], []
                for which, (kk, vv) in enumerate(((k_lo, v_lo), (k_hi, v_hi))):
                    h0, h1 = 4 * hk + which, 4 * hk + 2 + which
                    s = _nt(kk, q2) + _bias2(tbl, n, h0, h1, axis=1)
                    lse2 = jnp.concatenate([lse_t[h0:h0 + 1, :], lse_t[h1:h1 + 1, :]], axis=1)
                    dl2 = jnp.concatenate([dl_t[h0:h0 + 1, :], dl_t[h1:h1 + 1, :]], axis=1)
                    p = jnp.exp(s - lse2)
                    ds = (p * (_nt(vv, do2) - dl2)).astype(BF16)
                    dq2 = dq2 + _tn(ds, kk)
                    dks.append(jnp.dot(ds, q2, preferred_element_type=F32))
                    dvs.append(jnp.dot(p.astype(BF16), do2, preferred_element_type=F32))
                dk_sum = jnp.where(lo_mask, dks[0], dks[1])
                dv_sum = jnp.where(lo_mask, dvs[0], dvs[1])
                for jp in range(2):
                    dq_blk = dq2[jp * BLK:(jp + 1) * BLK]
                    if have_prev:
                        dq_blk = dq_blk + _ld(pdq_ref, cols[jp])
                    if final:
                        dq_blk = dq_blk * (HEAD_DIM ** -0.5)
                    _st(dq_ref, dq_blk.astype(out_dt), cols[jp])
                half, pos = hk // 2, hk % 2
                here = lo_mask if pos == 0 else jnp.logical_not(lo_mask)
                dk_tot = dk_sum + pltpu.roll(dk_sum, HEAD_DIM, axis=1)
                dv_tot = dv_sum + pltpu.roll(dv_sum, HEAD_DIM, axis=1)
                halves[half] = halves[half] + jnp.where(here, dk_tot, 0.0)
                halves[2 + half] = halves[2 + half] + jnp.where(here, dv_tot, 0.0)
            for b in range(4):
                cols = slice(b * LANES, (b + 1) * LANES)
                done = carry[:, cols] + halves[b][0:BLK, :]
                if have_prev:
                    done = done + _ld(pdkv_ref, cols)
                _st(dkv_ref, done.astype(out_dt), cols)
                carry[:, cols] = halves[b][BLK:, :]

        @pl.when(n == nb)
        def _():
            done = carry[...]
            if have_prev:
                done = done + _ld(pdkv_ref)
            _st(dkv_ref, done.astype(out_dt))

        if hosted is not None:
            @pl.when((pl.program_id(0) == dil - 1) & (n == nb))
            def _():
                hosted.finish(h_ins, h_outs, h_sems)

    cur = lambda n: jnp.minimum(n, nb - 1)
    behind = lambda n: jnp.maximum(n - 1, 0)
    in_specs = [rows.spec(ATT_W, cur), rows.spec(ATT_W, cur), rows.spec(LANES, cur), rows.spec(LANES, cur),
                rows.spec(2 * KV_W, cur), rows.spec(2 * KV_W, behind)]
    args = [rows.of(q), rows.of(d_o), rows.of(lse), rows.of(delta), rows.of(kv), rows.of(kv)]
    if have_prev:
        in_specs += [rows.spec(ATT_W, cur), rows.spec(2 * KV_W, behind)]
        args += [rows.of(prev[0]), rows.of(prev[1])]
    out_specs = [rows.spec(ATT_W, cur), rows.spec(2 * KV_W, behind)]
    out_shape = [jax.ShapeDtypeStruct(rows.view + (ATT_W,), out_dt),
                 jax.ShapeDtypeStruct(rows.view + (2 * KV_W,), out_dt)]
    scratch = [pltpu.VMEM((BLK, 2 * KV_W), F32), pltpu.VMEM((2 * N_Q_HEADS, 2 * BLK, BLK), F32)]
    if hosted is not None:
        in_specs += [ANY_SPEC] * hn
        args += hosted.arrays
        out_specs += [ANY_SPEC] * hn
        out_shape += hosted.out_shapes()
        scratch += hosted.sem_shapes()
    res = pl.pallas_call(
        body, name=name, grid=(dil, nb + 1),
        in_specs=in_specs, out_specs=tuple(out_specs), out_shape=tuple(out_shape), scratch_shapes=scratch,
        compiler_params=_params(("arbitrary", "arbitrary")),
    )(*args)
    return (res[0].reshape(S, ATT_W), res[1].reshape(S, 2 * KV_W)), list(res[2:])


def _dh(segments, w_in, x, dx2, g, hosted=None, tm=1024, tk=512):
    S, D = x.shape
    ns = len(segments)
    counts = [a.shape[1] // tk for a, _ in segments]
    starts = [sum(counts[:s]) for s in range(ns)]
    nk = sum(counts)
    hn = hosted.n if hosted is not None else 0

    def body(*refs):
        seg_refs = refs[:ns]
        w_ref, x_ref, dx2_ref, g_ref = refs[ns:ns + 4]
        h_ins = refs[ns + 4:ns + 4 + hn]
        gx_ref, gng_ref = refs[ns + 4 + hn:ns + 6 + hn]
        h_outs = refs[ns + 6 + hn:ns + 6 + 2 * hn]
        acc = refs[ns + 6 + 2 * hn]
        h_sems = refs[ns + 7 + 2 * hn:]
        i, k = pl.program_id(0), pl.program_id(1)

        @pl.when((i == 0) & (k == 0))
        def _():
            gng_ref[...] = jnp.zeros_like(gng_ref)
            if hosted is not None:
                hosted.start(h_ins, h_outs, h_sems)

        @pl.when(k == 0)
        def _():
            acc[...] = jnp.zeros_like(acc)

        for s in range(ns):
            @pl.when((k >= starts[s]) & (k < starts[s] + counts[s]))
            def _(s=s):
                t = seg_refs[s][...]
                if segments[s][1]:
                    t = _perm_rows(t, True)
                acc[...] += _nt(t, w_ref[...])

        @pl.when(k == nk - 1)
        def _():
            dh = acc[...]
            xf = x_ref[...]
            r = lax.rsqrt(jnp.mean(xf * xf, axis=-1, keepdims=True) + NORM_EPS)
            nrm = xf * r
            gng_ref[...] += jnp.sum(dh * nrm, axis=0, keepdims=True)
            dn = dh * g_ref[...]
            gx_ref[...] = dx2_ref[...] + r * (dn - nrm * jnp.mean(dn * nrm, axis=-1, keepdims=True))

        if hosted is not None:
            @pl.when((i == S // tm - 1) & (k == nk - 1))
            def _():
                hosted.finish(h_ins, h_outs, h_sems)

    row = pl.BlockSpec((tm, D), lambda i, k: (i, 0))
    vec = pl.BlockSpec((1, D), lambda i, k: (0, 0))
    in_specs = [pl.BlockSpec((tm, tk), lambda i, k, s=s: (i, jnp.clip(k - starts[s], 0, counts[s] - 1)))
                for s in range(ns)]
    in_specs += [pl.BlockSpec((D, tk), lambda i, k: (0, k)), row, row, vec]
    args = [a for a, _ in segments] + [w_in, x, dx2, g]
    out_specs = [row, vec]
    out_shape = [jax.ShapeDtypeStruct((S, D), F32), jax.ShapeDtypeStruct((1, D), F32)]
    scratch = [pltpu.VMEM((tm, D), F32)]
    if hosted is not None:
        in_specs += [ANY_SPEC] * hn
        args += hosted.arrays
        out_specs += [ANY_SPEC] * hn
        out_shape += hosted.out_shapes()
        scratch += hosted.sem_shapes()
    res = pl.pallas_call(
        body, name="dh", grid=(S // tm, nk),
        in_specs=in_specs, out_specs=tuple(out_specs), out_shape=tuple(out_shape), scratch_shapes=scratch,
        compiler_params=_params(("arbitrary", "arbitrary")),
    )(*args)
    return res[0], res[1], list(res[2:])


def _tn_matmul(a, bs, name, tm=512):
    M, K = a.shape
    nb = len(bs)

    def body(a_ref, *refs):
        @pl.when(pl.program_id(0) == 0)
        def _():
            for o_ref in refs[nb:]:
                o_ref[...] = jnp.zeros_like(o_ref)

        at = a_ref[...]
        for b_ref, o_ref in zip(refs[:nb], refs[nb:]):
            for c in range(0, b_ref.shape[1], 512):
                o_ref[:, c:c + 512] += _tn(at, b_ref[:, c:c + 512])

    return pl.pallas_call(
        body, name=name, grid=(M // tm,),
        in_specs=[pl.BlockSpec((tm, K), lambda m: (m, 0))] + [pl.BlockSpec((tm, b.shape[1]), lambda m: (m, 0))
                                                              for b in bs],
        out_specs=tuple(pl.BlockSpec((K, b.shape[1]), lambda m: (0, 0)) for b in bs),
        out_shape=tuple(jax.ShapeDtypeStruct((K, b.shape[1]), F32) for b in bs),
        compiler_params=_params(("arbitrary",)),
    )(a, *bs)


def _adamw(parts, w, m, v, name, tr=None, split=None):
    R, C = w.shape
    tr = R if tr is None else tr
    parts = [parts] if split is None else list(parts)
    npar = len(parts)

    def total(p_ref):
        g = p_ref[0].astype(F32)
        for dev in range(1, N_DEV):
            g = g + p_ref[dev].astype(F32)
        return g

    def body(*refs):
        w_ref, m_ref, v_ref, g_out, d_out, m_out, v_out = refs[npar:]
        if split is None:
            g = total(refs[0])
        else:
            g = jnp.where(_mesh_pos()[3] < split, total(refs[0]), total(refs[1]))
        mn = ADAM_B1 * m_ref[...] + (1.0 - ADAM_B1) * g
        vn = ADAM_B2 * v_ref[...] + (1.0 - ADAM_B2) * (g * g)
        m_hat = mn / (1.0 - ADAM_B1 ** ADAM_STEP)
        v_hat = vn / (1.0 - ADAM_B2 ** ADAM_STEP)
        g_out[...] = g
        d_out[...] = -ADAM_LR * (m_hat / (jnp.sqrt(v_hat) + ADAM_EPS) + ADAM_WD * w_ref[...])
        m_out[...] = mn
        v_out[...] = vn

    blk = pl.BlockSpec((tr, C), lambda i: (i, 0))
    shp = jax.ShapeDtypeStruct((R, C), F32)
    return pl.pallas_call(
        body, name=name, grid=(R // tr,),
        in_specs=[pl.BlockSpec((N_DEV, tr, C), lambda i: (0, i, 0))] * npar + [blk, blk, blk],
        out_specs=(blk, blk, blk, blk), out_shape=(shp, shp, shp, shp),
        compiler_params=_params(("parallel",)),
    )(*parts, w, m, v)


def _local_step(x, target, norm_g, w_in, conv_w, conv_b, ln_g, ln_b, w_out, gf, exchanges=None):
    ex_out, ex_att, ex_conv = exchanges if exchanges is not None else (None, None, None)
    conv_cols = w_in.shape[1] - 2 * ATT_W - 2 * KV_W
    q, kv, a_gate, gates, h_rm, h = _inproj(
        x, norm_g, w_in, [(ATT_W, HEAD_DIM ** -0.5, True), (2 * KV_W, 1.0, True), (ATT_W, 1.0, True),
                          (conv_cols, 1.0, False)])

    merged = None
    for idx, (_, dil) in enumerate(reversed(PATTERNS)):
        merged = _attn_fwd(q, kv, dil, "attn_fwd_d%d" % dil, merged,
                           a_gate if idx == len(PATTERNS) - 1 else None)
    o, lse, y_att = merged
    conv_out, y_conv = _conv_fwd(gates, conv_w, conv_b, ln_g, ln_b)
    dx2, dxb, loss_cols, g_gf = _outproj_loss(x, y_att, y_conv, w_out, gf, target)

    d_o, d_a_gate, delta, dxb_rm = _dy_att(dxb, w_out, a_gate, o)
    g_w_out = jnp.concatenate([_tn_matmul(y_att, [dxb_rm], "gw_out_att")[0],
                               _tn_matmul(y_conv, [dxb], "gw_out_conv")[0]], axis=0)
    acc, out_parts = None, []
    for idx, (_, dil) in enumerate(reversed(PATTERNS)):
        hosted = ex_out(g_w_out) if (idx == 0 and ex_out is not None) else None
        acc, outs = _attn_bwd(q, kv, d_o, lse, delta, dil, acc, idx == len(PATTERNS) - 1, "attn_bwd_d%d" % dil,
                              hosted)
        out_parts += outs
    dq, dkv = acc
    g_q, g_kv, g_a = _tn_matmul(h_rm, [dq, dkv, d_a_gate], "gw_in_att")

    d_c_gate, d_conv, g_ln_g, g_ln_b, g_conv_b = _dy_conv(dxb, w_out, gates, conv_out, ln_g, ln_b)
    dgates, g_conv_w, att_parts = _conv_bwd(d_conv, gates, d_c_gate, conv_w,
                                            ex_att(g_q, g_kv, g_a) if ex_att is not None else None)
    g_c, = _tn_matmul(h, [dgates], "gw_in_conv")
    grad_x, g_norm_g, conv_parts = _dh(
        [(dq, True), (dkv, True), (d_a_gate, True), (dgates, False)], w_in, x, dx2, norm_g,
        ex_conv(g_a, g_c, g_conv_w) if ex_conv is not None else None)
    small = (g_norm_g, g_conv_b, g_ln_g, g_ln_b, g_gf, loss_cols)
    return grad_x, (g_q, g_kv, g_a, g_c), g_w_out, g_conv_w, small, (out_parts, att_parts, conv_parts)


def kernel(x, norm_g, w_in, conv_w, conv_b, conv_ln_g, conv_ln_b, w_out, final_norm_g, loss_target, m_norm_g, m_w_in, m_conv_w, m_conv_b, m_conv_ln_g, m_conv_ln_b, m_w_out, m_final_norm_g, v_norm_g, v_w_in, v_conv_w, v_conv_b, v_conv_ln_g, v_conv_ln_b, v_w_out, v_final_norm_g):
    S, D = x.shape[1], x.shape[2]
    win_sh, wout_sh, cw_sh = w_in[0], w_out[0], conv_w[0]
    cols_sh, rows_sh, ch_sh = win_sh.shape[1], wout_sh.shape[0], cw_sh.shape[1]

    win_all, wout_all, cw_all = _gather_two_level(
        [win_sh.astype(BF16), wout_sh.astype(BF16), cw_sh], "gather_weights")
    w_in_full = win_all.transpose(1, 0, 2).reshape(D, N_DEV * cols_sh)
    w_out_full = wout_all.reshape(N_DEV * rows_sh, D)
    conv_w_full = cw_all.transpose(1, 0, 2).reshape(CONV_K, N_DEV * ch_sh)
    conv_w_full = jnp.pad(conv_w_full, ((0, CONV_HALO - CONV_K), (0, 0)))
    gf = final_norm_g.reshape(1, D)

    first = -(-(ATT_W + 2 * KV_W) // cols_sh)
    a_off = first * cols_sh - (ATT_W + 2 * KV_W)
    assert 0 <= a_off <= ATT_W

    def pieces(g, n):
        return g.reshape(D, n, cols_sh).transpose(1, 0, 2).astype(BF16)

    def ex_out(g_w_out):
        return _Exchange([g_w_out.reshape(N_DEV, rows_sh, D).astype(BF16)], [(0, N_DEV)])

    def ex_att(g_q, g_kv, g_a):
        return _Exchange([pieces(jnp.concatenate([g_q, g_kv, g_a[:, :a_off]], axis=1), first)], [(0, first)])

    def ex_conv(g_a, g_c, g_conv_w):
        return _Exchange(
            [pieces(jnp.concatenate([g_a[:, a_off:], g_c], axis=1), N_DEV - first),
             g_conv_w[:CONV_K].reshape(CONV_K, N_DEV, ch_sh).transpose(1, 0, 2)],
            [(first, N_DEV), (0, N_DEV)])

    grad_x, _, _, _, small, parts = _local_step(
        x[0], loss_target[0], norm_g, w_in_full, conv_w_full, conv_b, conv_ln_g, conv_ln_b, w_out_full, gf,
        (ex_out, ex_att, ex_conv))
    (wout_parts,), (win_parts_lo,), (win_parts_hi, cw_parts) = parts

    small_pack = jnp.concatenate(list(small) + [jnp.zeros((2, D), F32)], axis=0)
    small_parts, = _exchange([small_pack], [None], "gather_small")

    upd_win = _adamw((win_parts_lo, win_parts_hi), win_sh, m_w_in[0], v_w_in[0], "adamw_w_in", tr=256, split=first)
    upd_wout = _adamw(wout_parts, wout_sh, m_w_out[0], v_w_out[0], "adamw_w_out", tr=128)
    upd_cw = _adamw(cw_parts, cw_sh, m_conv_w[0], v_conv_w[0], "adamw_conv_w")
    zeros3 = jnp.zeros((3, D), F32)
    stack = lambda a, b, c, d_, e: jnp.concatenate([a, b, c, d_, e.reshape(1, D), zeros3], axis=0)
    upd_small = _adamw(
        small_parts,
        stack(norm_g, conv_b, conv_ln_g, conv_ln_b, final_norm_g),
        stack(m_norm_g, m_conv_b, m_conv_ln_g, m_conv_ln_b, m_final_norm_g),
        stack(v_norm_g, v_conv_b, v_conv_ln_g, v_conv_ln_b, v_final_norm_g) + jnp.concatenate(
            [jnp.zeros((5, D), F32), jnp.ones((3, D), F32)], axis=0),
        "adamw_small")

    loss = 0.5 / D * jnp.sum(upd_small[0][5])

    def outputs(kind):
        sm = upd_small[kind]
        return [sm[0:1], upd_win[kind][None], upd_cw[kind][None], sm[1:2], sm[2:3], sm[3:4],
                upd_wout[kind][None], sm[4]]

    return (loss, grad_x[None], *outputs(0), *outputs(1), *outputs(2), *outputs(3))
```

```python
import jax
import jax.numpy as jnp
from jax import lax
from jax.experimental import pallas as pl
from jax.experimental.pallas import tpu as pltpu

F32 = jnp.float32
BF16 = jnp.bfloat16

HEAD_DIM = 64
N_KV_HEADS = 4
N_Q_HEADS = 16
ATT_W = 1024
KV_W = 256
CONV_K = 31
CONV_HALO = 32
PATTERNS = ((128, 1), (512, 4), (2048, 16))
BLK = 128
LANES = 128
NORM_EPS = 1e-6
LN_EPS = 1e-5
NEG = -1e30
N_DEV = 8
ADAM_LR, ADAM_B1, ADAM_B2, ADAM_EPS, ADAM_WD, ADAM_STEP = 0.001, 0.9, 0.999, 1e-08, 0.01, 10
VMEM_LIMIT = 48 * 1024 * 1024
SLOPES = tuple(2.0 ** (-8.0 * (h + 1) / N_Q_HEADS) for h in range(N_Q_HEADS))
MESH = pl.DeviceIdType.MESH


def _params(sem):
    return pltpu.CompilerParams(dimension_semantics=sem, vmem_limit_bytes=VMEM_LIMIT)


def _sigmoid(v):
    return 1.0 / (1.0 + jnp.exp(-v))


def _silu_and_grad(v):
    s = _sigmoid(v)
    return v * s, s * (1.0 + v * (1.0 - s))


ANY_SPEC = pl.BlockSpec(memory_space=pl.ANY)


def _mesh_pos():
    x, y, c = lax.axis_index("x"), lax.axis_index("y"), lax.axis_index("c")
    return x, y, c, 4 * x + 2 * y + c


def _flipped(k, x, y, c):
    px = 1 - x if k & 4 else x
    py = 1 - y if k & 2 else y
    pc = 1 - c if k & 1 else c
    return (px, py, pc), 4 * px + 2 * py + pc


class _Exchange:
    def __init__(self, arrays, dests, flips=None):
        self.arrays, self.dests, self.n = list(arrays), list(dests), len(arrays)
        self.flips = [tuple(range(1, N_DEV)) if f is None else tuple(f)
                      for f in (flips if flips is not None else [None] * self.n)]

    def out_shapes(self):
        return [jax.ShapeDtypeStruct((N_DEV,) + a.shape[-2:], a.dtype) for a in self.arrays]

    def sem_shapes(self):
        return [pltpu.SemaphoreType.DMA((self.n, N_DEV - 1)), pltpu.SemaphoreType.DMA((self.n, N_DEV - 1)),
                pltpu.SemaphoreType.DMA((self.n,))]

    def _when(self, a, dev, fn):
        if self.dests[a] is None:
            fn()
        else:
            lo, hi = self.dests[a]
            pl.when((dev >= lo) & (dev < hi))(fn)

    def _mine(self, ins, a, dev):
        return ins[a] if self.dests[a] is None else ins[a].at[dev - self.dests[a][0]]

    def _copy(self, ins, outs, sems, a, k, src_dev, slot, target):
        return pltpu.make_async_remote_copy(
            src_ref=self._mine(ins, a, src_dev), dst_ref=outs[a].at[slot],
            send_sem=sems[0].at[a, k - 1], recv_sem=sems[1].at[a, k - 1],
            device_id=target, device_id_type=MESH)

    def start(self, ins, outs, sems):
        x, y, c, me = _mesh_pos()
        for a in range(self.n):
            self._when(a, me, lambda a=a: pltpu.make_async_copy(
                self._mine(ins, a, me), outs[a].at[me], sems[2].at[a]).start())
            for k in self.flips[a]:
                target, peer = _flipped(k, x, y, c)
                self._when(a, peer, lambda a=a, k=k, target=target, peer=peer: self._copy(
                    ins, outs, sems, a, k, peer, me, target).start())

    def finish(self, ins, outs, sems):
        x, y, c, me = _mesh_pos()
        lo0 = [0 if d is None else d[0] for d in self.dests]
        for a in range(self.n):
            for k in self.flips[a]:
                target, peer = _flipped(k, x, y, c)
                self._when(a, me, lambda a=a, k=k, peer=peer: self._copy(
                    ins, outs, sems, a, k, lo0[a], peer, (x, y, c)).wait_recv())
            for k in self.flips[a]:
                target, peer = _flipped(k, x, y, c)
                self._when(a, peer, lambda a=a, k=k, target=target, peer=peer: self._copy(
                    ins, outs, sems, a, k, peer, me, target).wait_send())
            self._when(a, me, lambda a=a: pltpu.make_async_copy(
                self._mine(ins, a, me), outs[a].at[me], sems[2].at[a]).wait())


def _exchange(arrays, dests, name, flips=None):
    ex = _Exchange(arrays, dests, flips)
    na = ex.n

    def body(*refs):
        ins, outs, sems = refs[:na], refs[na:2 * na], refs[2 * na:]
        ex.start(ins, outs, sems)
        ex.finish(ins, outs, sems)

    return pl.pallas_call(
        body, name=name, out_shape=tuple(ex.out_shapes()),
        in_specs=[ANY_SPEC] * na, out_specs=tuple([ANY_SPEC] * na), scratch_shapes=ex.sem_shapes(),
    )(*arrays)


def _chip_sum(pieces, lo, name):
    n, R, C = pieces.shape

    def swap(p_ref, t_ref, send_sems, recv_sems):
        x, y, c, me = _mesh_pos()
        for i in range(n):
            mine = (lo + i) % 2
            cp = pltpu.make_async_remote_copy(
                src_ref=p_ref.at[i], dst_ref=t_ref.at[i], send_sem=send_sems.at[i], recv_sem=recv_sems.at[i],
                device_id=(x, y, 1 - c), device_id_type=MESH)
            pl.when(c != mine)(cp.start)
        for i in range(n):
            mine = (lo + i) % 2
            cp = pltpu.make_async_remote_copy(
                src_ref=p_ref.at[i], dst_ref=t_ref.at[i], send_sem=send_sems.at[i], recv_sem=recv_sems.at[i],
                device_id=(x, y, 1 - c), device_id_type=MESH)
            pl.when(c == mine)(cp.wait_recv)
            pl.when(c != mine)(cp.wait_send)

    other = pl.pallas_call(
        swap, name=name + "_swap", out_shape=jax.ShapeDtypeStruct(pieces.shape, pieces.dtype),
        in_specs=[ANY_SPEC], out_specs=ANY_SPEC,
        scratch_shapes=[pltpu.SemaphoreType.DMA((n,)), pltpu.SemaphoreType.DMA((n,))],
    )(pieces)

    def add(p_ref, t_ref, o_ref):
        o_ref[...] = (p_ref[...].astype(F32) + t_ref[...].astype(F32)).astype(o_ref.dtype)

    tr = 256
    blk = pl.BlockSpec((None, tr, C), lambda i, r: (i, r, 0))
    return pl.pallas_call(
        add, name=name + "_add", grid=(n, R // tr), in_specs=[blk, blk], out_specs=blk,
        out_shape=jax.ShapeDtypeStruct(pieces.shape, pieces.dtype),
        compiler_params=_params(("parallel", "parallel")),
    )(pieces, other)


def _gather_two_level(arrays, name):
    na = len(arrays)

    def body(*refs):
        ins, outs = refs[:na], refs[na:2 * na]
        send_sems, recv_sems, loc_sems = refs[2 * na:]
        x, y, c, me = _mesh_pos()
        sibling = (x, y, 1 - c)
        chips = [(1 - x, y), (x, 1 - y), (1 - x, 1 - y)]

        def slot(px, py, pc):
            return 4 * px + 2 * py + pc

        def copy(a, k, src, block, to):
            return pltpu.make_async_remote_copy(
                src_ref=src, dst_ref=outs[a].at[slot(*block)], send_sem=send_sems.at[a, k], recv_sem=recv_sems.at[a, k],
                device_id=to, device_id_type=MESH)

        local = [pltpu.make_async_copy(ins[a], outs[a].at[me], loc_sems.at[a]) for a in range(na)]
        for cp in local:
            cp.start()
        started = []
        for a in range(na):
            started.append(copy(a, 0, ins[a], (x, y, c), sibling))
            started += [copy(a, 1 + j, ins[a], (x, y, c), (*chip, c)) for j, chip in enumerate(chips)]
        for cp in started:
            cp.start()
        for j, chip in enumerate(chips):
            for a in range(na):
                copy(a, 1 + j, ins[a], (*chip, c), (x, y, c)).wait_recv()
                fwd = copy(a, 4 + j, outs[a].at[slot(*chip, c)], (*chip, c), sibling)
                fwd.start()
                started.append(fwd)
        for a in range(na):
            copy(a, 0, ins[a], sibling, (x, y, c)).wait_recv()
            for j, chip in enumerate(chips):
                copy(a, 4 + j, ins[a], (*chip, 1 - c), (x, y, c)).wait_recv()
        for cp in started:
            cp.wait_send()
        for cp in local:
            cp.wait()

    return pl.pallas_call(
        body, name=name,
        out_shape=tuple(jax.ShapeDtypeStruct((N_DEV,) + a.shape, a.dtype) for a in arrays),
        in_specs=[ANY_SPEC] * na, out_specs=tuple([ANY_SPEC] * na),
        scratch_shapes=[pltpu.SemaphoreType.DMA((na, N_DEV - 1)), pltpu.SemaphoreType.DMA((na, N_DEV - 1)),
                        pltpu.SemaphoreType.DMA((na,))],
    )(*arrays)


CHUNK = 128
RESIDUES = 16
PER_RES = CHUNK // RESIDUES


def _perm_rows(tile, inverse):
    a = lax.broadcasted_iota(jnp.int32, (CHUNK, CHUNK), 0)
    b = lax.broadcasted_iota(jnp.int32, (CHUNK, CHUNK), 1)
    if inverse:
        a, b = b, a
    p = jnp.where(a == PER_RES * (b % RESIDUES) + b // RESIDUES, 1.0, 0.0).astype(BF16)
    parts = [jnp.dot(p, tile[c * CHUNK:(c + 1) * CHUNK], preferred_element_type=F32)
             for c in range(tile.shape[0] // CHUNK)]
    return jnp.concatenate(parts, axis=0).astype(BF16)


class _Rows:
    def __init__(self, dil, S):
        nc = S // CHUNK
        self.dil = dil
        if dil == 1:
            self.view, self.block, self.nb = (nc, CHUNK), (None, CHUNK), nc
            self.index = lambda r, b: (b, 0, 0)
        elif dil == 4:
            self.view, self.block, self.nb = (nc, 4, 4, PER_RES), (4, 4, None, PER_RES), nc // 4
            self.index = lambda r, b: (b, 0, r, 0, 0)
        elif dil == RESIDUES:
            self.view, self.block, self.nb = (nc, RESIDUES, PER_RES), (RESIDUES, None, PER_RES), nc // RESIDUES
            self.index = lambda r, b: (b, r, 0, 0)
        else:
            raise NotImplementedError(dil)

    def of(self, a):
        return a.reshape(self.view + (a.shape[-1],))

    def spec(self, width, which_block):
        return pl.BlockSpec(self.block + (width,), lambda r, n: self.index(r, which_block(n)))

    def pos(self, row):
        if self.dil == 1:
            return (row % PER_RES) * RESIDUES + row // PER_RES
        if self.dil == 4:
            return (row // 32) * 32 + (row % PER_RES) * 4 + (row % 32) // PER_RES
        return row


def _ld(ref, cols=slice(None)):
    v = ref[(slice(None),) * (len(ref.shape) - 1) + (cols,)]
    return v.reshape(BLK, v.shape[-1])


def _st(ref, val, cols=slice(None)):
    ref[(slice(None),) * (len(ref.shape) - 1) + (cols,)] = val.reshape(ref.shape[:-1] + (val.shape[-1],))


def _inproj(x, g, w, segments, tm=1024, tn=512):
    S, D = x.shape
    ns = len(segments)
    counts = [nc // tn for nc, _, _ in segments]
    starts = [sum(counts[:s]) for s in range(ns)]

    def body(x_ref, g_ref, w_ref, *rest):
        outs = rest[:ns]
        hrm_out, h_out, hrm_scr, h_scr = rest[ns:]
        j = pl.program_id(1)

        @pl.when(j == 0)
        def _():
            xf = x_ref[...]
            r = lax.rsqrt(jnp.mean(xf * xf, axis=-1, keepdims=True) + NORM_EPS)
            h = (xf * r * g_ref[...]).astype(BF16)
            hrm = _perm_rows(h, False)
            h_scr[...] = h
            hrm_scr[...] = hrm
            h_out[...] = h
            hrm_out[...] = hrm

        for s, (_, scale, rm) in enumerate(segments):
            @pl.when((j >= starts[s]) & (j < starts[s] + counts[s]))
            def _(s=s, scale=scale, rm=rm):
                acc = jnp.dot((hrm_scr if rm else h_scr)[...], w_ref[...], preferred_element_type=F32)
                outs[s][...] = acc * scale if scale != 1.0 else acc

    row = pl.BlockSpec((tm, D), lambda i, j: (i, 0))
    out_specs = [pl.BlockSpec((tm, tn), lambda i, j, s=s: (i, jnp.clip(j - starts[s], 0, counts[s] - 1)))
                 for s in range(ns)]
    out_shape = [jax.ShapeDtypeStruct((S, nc), F32) for nc, _, _ in segments]
    return pl.pallas_call(
        body, name="inproj", grid=(S // tm, sum(counts)),
        in_specs=[row, pl.BlockSpec((1, D), lambda i, j: (0, 0)), pl.BlockSpec((D, tn), lambda i, j: (0, j))],
        out_specs=tuple(out_specs + [row, row]),
        out_shape=tuple(out_shape + [jax.ShapeDtypeStruct((S, D), BF16)] * 2),
        scratch_shapes=[pltpu.VMEM((tm, D), BF16), pltpu.VMEM((tm, D), BF16)],
        compiler_params=_params(("arbitrary", "arbitrary")),
    )(x, g, w)


def _fill_bias_table(tbl, rows, keys_first=False):
    shape = (2 * BLK, BLK) if keys_first else (BLK, 2 * BLK)
    qi = lax.broadcasted_iota(jnp.int32, shape, 1 if keys_first else 0)
    kj = lax.broadcasted_iota(jnp.int32, shape, 0 if keys_first else 1)
    dist = rows.pos(qi) - rows.pos(kj % BLK) + jnp.where(kj < BLK, BLK, 0)
    inside = (dist >= 0) & (dist <= BLK)
    negd = (dist * (-rows.dil)).astype(F32)
    for f, valid in enumerate((inside & (kj >= BLK), inside)):
        for h in range(N_Q_HEADS):
            tbl[f * N_Q_HEADS + h] = jnp.where(valid, SLOPES[h] * negd, NEG)


def _bias2(tbl, n, h0, h1, axis=0):
    base = jnp.where(n == 0, 0, N_Q_HEADS)
    return jnp.concatenate([tbl[base + h0], tbl[base + h1]], axis=axis)


def _head_operands(kv2, hk, lo_mask):
    half, pos = hk // 2, hk % 2
    out = []
    for base in (0, KV_W):
        t = kv2[:, base + half * LANES: base + (half + 1) * LANES]
        sw = pltpu.roll(t, HEAD_DIM, axis=1)
        at_lo, at_hi = (t, sw) if pos == 0 else (sw, t)
        out.append(jnp.where(lo_mask, at_lo, 0.0).astype(BF16))
        out.append(jnp.where(lo_mask, 0.0, at_hi).astype(BF16))
    return out


def _nt(a, b):
    return lax.dot_general(a, b, (((1,), (1,)), ((), ())), preferred_element_type=F32)


def _tn(a, b):
    return lax.dot_general(a, b, (((0,), (0,)), ((), ())), preferred_element_type=F32)


def _attn_fwd(q, kv, dil, name, prev=None, gate=None):
    S = q.shape[0]
    rows = _Rows(dil, S)
    nb = rows.nb
    have_prev, last = prev is not None, gate is not None

    def body(*refs):
        refs = list(refs)
        q_ref, kvc_ref, kvp_ref = refs[:3]
        del refs[:3]
        if have_prev:
            po_ref, pl_ref = refs[:2]
            del refs[:2]
        if last:
            gate_ref = refs.pop(0)
        o_ref, lse_ref = refs[:2]
        y_ref = refs[2] if last else None
        tbl = refs[-1]
        n = pl.program_id(1)

        @pl.when((pl.program_id(0) == 0) & (n == 0))
        def _():
            _fill_bias_table(tbl, rows)

        kv2 = jnp.concatenate([_ld(kvp_ref), _ld(kvc_ref)], axis=0)
        lo_mask = lax.broadcasted_iota(jnp.int32, (2 * BLK, LANES), 1) < HEAD_DIM
        lane = lax.broadcasted_iota(jnp.int32, (BLK, LANES), 1)
        stats = jnp.zeros((BLK, LANES), F32)
        for hk in range(N_KV_HEADS):
            k_lo, k_hi, v_lo, v_hi = _head_operands(kv2, hk, lo_mask)
            cols = [slice(b * LANES, (b + 1) * LANES) for b in (2 * hk, 2 * hk + 1)]
            q2 = jnp.concatenate([_ld(q_ref, cols[0]), _ld(q_ref, cols[1])], axis=0).astype(BF16)
            o2 = jnp.zeros((2 * BLK, LANES), F32)
            for which, (kk, vv) in enumerate(((k_lo, v_lo), (k_hi, v_hi))):
                h0, h1 = 4 * hk + which, 4 * hk + 2 + which
                s = _nt(q2, kk) + _bias2(tbl, n, h0, h1)
                m = jnp.max(s, axis=1, keepdims=True)
                p = jnp.exp(s - m)
                l = jnp.sum(p, axis=1, keepdims=True)
                o2 = o2 + jnp.dot(p.astype(BF16), vv, preferred_element_type=F32) * (1.0 / l)
                lse = m + jnp.log(l)
                stats = jnp.where(lane == h0, lse[0:BLK], stats)
                stats = jnp.where(lane == h1, lse[BLK:], stats)
            _st(o_ref, o2[0:BLK], cols[0])
            _st(o_ref, o2[BLK:], cols[1])
        if have_prev:
            before = _ld(pl_ref)
            top = jnp.maximum(before, stats)
            e_old, e_new = jnp.exp(before - top), jnp.exp(stats - top)
            total = e_old + e_new
            stats = top + jnp.log(total)
            inv = 1.0 / total
            w_old, w_new = e_old * inv, e_new * inv
        if have_prev or last:
            lo = lane < HEAD_DIM
            for blk in range(ATT_W // LANES):
                cols = slice(blk * LANES, (blk + 1) * LANES)
                o_blk = _ld(o_ref, cols)
                if have_prev:
                    pick = lambda w: jnp.where(lo, w[:, 2 * blk:2 * blk + 1], w[:, 2 * blk + 1:2 * blk + 2])
                    o_blk = o_blk * pick(w_new) + _ld(po_ref, cols) * pick(w_old)
                    _st(o_ref, o_blk, cols)
                if last:
                    a = _ld(gate_ref, cols)
                    _st(y_ref, (o_blk * (a * _sigmoid(a))).astype(BF16), cols)
        _st(lse_ref, stats)

    here = lambda n: n
    before_n = lambda n: jnp.maximum(n - 1, 0)
    in_specs = [rows.spec(ATT_W, here), rows.spec(2 * KV_W, here), rows.spec(2 * KV_W, before_n)]
    args = [rows.of(q), rows.of(kv), rows.of(kv)]
    if have_prev:
        in_specs += [rows.spec(ATT_W, here), rows.spec(LANES, here)]
        args += [rows.of(prev[0]), rows.of(prev[1])]
    out_specs = [rows.spec(ATT_W, here), rows.spec(LANES, here)]
    out_shape = [jax.ShapeDtypeStruct(rows.view + (ATT_W,), F32), jax.ShapeDtypeStruct(rows.view + (LANES,), F32)]
    if last:
        in_specs.append(rows.spec(ATT_W, here))
        args.append(rows.of(gate))
        out_specs.append(rows.spec(ATT_W, here))
        out_shape.append(jax.ShapeDtypeStruct(rows.view + (ATT_W,), BF16))
    res = pl.pallas_call(
        body, name=name, grid=(dil, nb),
        in_specs=in_specs, out_specs=tuple(out_specs), out_shape=tuple(out_shape),
        scratch_shapes=[pltpu.VMEM((2 * N_Q_HEADS, BLK, 2 * BLK), F32)],
        compiler_params=_params(("arbitrary", "arbitrary")),
    )(*args)
    return tuple(r.reshape(S, r.shape[-1]) for r in res)


def _shifted_copies(buf, phases):
    n = phases.shape[1]
    for b in range(1, 8):
        phases[b - 1] = buf[b:b + n, :]


def _window(buf, phases, start, cols):
    b = start % 8
    if b == 0:
        return buf[start:start + 8, cols]
    return phases[b - 1, start - b:start - b + 8, cols]


def _broadcast_taps(w_ref, wb):
    for j in range(CONV_K):
        wb[j] = jnp.broadcast_to(w_ref[j:j + 1, :], wb.shape[1:])


def _conv_fwd(gates, conv_w, conv_b, ln_g, ln_b, tt=256):
    S = gates.shape[0]
    C = conv_w.shape[1]
    hb = tt // CONV_HALO

    def body(val_ref, glu_ref, hval_ref, hglu_ref, gate_ref, w_ref, b_ref, g_ref, beta_ref,
             conv_ref, y_ref, hbuf, hph):
        i = pl.program_id(0)
        halo = hval_ref[...] * _sigmoid(hglu_ref[...])
        hbuf[0:CONV_HALO, :] = jnp.where(i > 0, halo, 0.0)
        hbuf[CONV_HALO:, :] = val_ref[...] * _sigmoid(glu_ref[...])
        _shifted_copies(hbuf, hph)
        for cb in range(C // LANES):
            cols = slice(cb * LANES, (cb + 1) * LANES)
            wj = [jnp.broadcast_to(w_ref[j:j + 1, cols], (8, LANES)) for j in range(CONV_K)]
            for rc in range(tt // 8):
                acc = jnp.zeros((8, LANES), F32)
                for j in range(CONV_K):
                    start = rc * 8 + CONV_HALO - (CONV_K - 1) + j
                    acc = acc + _window(hbuf, hph, start, cols) * wj[j]
                conv_ref[rc * 8:(rc + 1) * 8, cols] = acc
        cv = conv_ref[...] + b_ref[...]
        conv_ref[...] = cv
        mu = jnp.mean(cv, axis=-1, keepdims=True)
        xc = cv - mu
        var = jnp.mean(xc * xc, axis=-1, keepdims=True)
        ln = xc * lax.rsqrt(var + LN_EPS) * g_ref[...] + beta_ref[...]
        gt = gate_ref[...]
        y_ref[...] = (ln * _sigmoid(ln) * (gt * _sigmoid(gt))).astype(BF16)

    vec = pl.BlockSpec((1, C), lambda i: (0, 0))
    return pl.pallas_call(
        body, name="conv_fwd", grid=(S // tt,),
        in_specs=[pl.BlockSpec((tt, C), lambda i: (i, 0)),
                  pl.BlockSpec((tt, C), lambda i: (i, 1)),
                  pl.BlockSpec((CONV_HALO, C), lambda i: (jnp.maximum(i * hb - 1, 0), 0)),
                  pl.BlockSpec((CONV_HALO, C), lambda i: (jnp.maximum(i * hb - 1, 0), 1)),
                  pl.BlockSpec((tt, C), lambda i: (i, 2)),
                  pl.BlockSpec((CONV_HALO, C), lambda i: (0, 0)), vec, vec, vec],
        out_specs=(pl.BlockSpec((tt, C), lambda i: (i, 0)), pl.BlockSpec((tt, C), lambda i: (i, 0))),
        out_shape=(jax.ShapeDtypeStruct((S, C), F32), jax.ShapeDtypeStruct((S, C), BF16)),
        scratch_shapes=[pltpu.VMEM((tt + CONV_HALO, C), F32), pltpu.VMEM((7, tt + CONV_HALO - 8, C), F32)],
        compiler_params=_params(("parallel",)),
    )(gates, gates, gates, gates, gates, conv_w, conv_b, ln_g, ln_b)


def _outproj_loss(x, y_att, y_conv, w_out, gf, target, tm=512):
    S, D = x.shape
    E = y_att.shape[1]

    def body(x_ref, ya_ref, yc_ref, w_ref, gf_ref, t_ref, dx_ref, dxb_ref, loss_ref, ggf_ref):
        @pl.when(pl.program_id(0) == 0)
        def _():
            loss_ref[...] = jnp.zeros_like(loss_ref)
            ggf_ref[...] = jnp.zeros_like(ggf_ref)

        x2 = (x_ref[...] + jnp.dot(_perm_rows(ya_ref[...], True), w_ref[0:E, :], preferred_element_type=F32)
              + jnp.dot(yc_ref[...], w_ref[E:, :], preferred_element_type=F32))
        r = lax.rsqrt(jnp.mean(x2 * x2, axis=-1, keepdims=True) + NORM_EPS)
        nrm = x2 * r
        gfv = gf_ref[...]
        err = nrm * gfv - t_ref[...]
        loss_ref[...] += jnp.sum(err * err, axis=0, keepdims=True)
        dout = err * (1.0 / D)
        ggf_ref[...] += jnp.sum(dout * nrm, axis=0, keepdims=True)
        dn = dout * gfv
        dx2 = r * (dn - nrm * jnp.mean(dn * nrm, axis=-1, keepdims=True))
        dx_ref[...] = dx2
        dxb_ref[...] = dx2.astype(BF16)

    row = lambda w: pl.BlockSpec((tm, w), lambda i: (i, 0))
    vec = pl.BlockSpec((1, D), lambda i: (0, 0))
    return pl.pallas_call(
        body, name="outproj_loss", grid=(S // tm,),
        in_specs=[row(D), row(E), row(E), pl.BlockSpec((2 * E, D), lambda i: (0, 0)), vec, row(D)],
        out_specs=(row(D), row(D), vec, vec),
        out_shape=(jax.ShapeDtypeStruct((S, D), F32), jax.ShapeDtypeStruct((S, D), BF16),
                   jax.ShapeDtypeStruct((1, D), F32), jax.ShapeDtypeStruct((1, D), F32)),
        compiler_params=_params(("arbitrary",)),
    )(x, y_att, y_conv, w_out, gf, target)


def _split3(v):
    hi = v.astype(BF16)
    r1 = v - hi.astype(F32)
    mid = r1.astype(BF16)
    lo = (r1 - mid.astype(F32)).astype(BF16)
    return hi, mid, lo


def _dy_att(dxb, w_out, gates, o, tm=512):
    S, D = dxb.shape
    E = ATT_W

    def body(dx_ref, w_ref, a_ref, o_ref, do_ref, da_ref, dl_ref, dxr_ref):
        dxr = _perm_rows(dx_ref[...], False)
        dxr_ref[...] = dxr
        dya = _nt(dxr, w_ref[...])
        a = a_ref[...]
        ov = o_ref[...]
        sl, dsl = _silu_and_grad(a)
        d_o = dya * sl
        do_ref[...] = d_o
        da_ref[...] = (dya * ov * dsl).astype(BF16)
        ci = lax.broadcasted_iota(jnp.int32, (E, LANES), 0) // HEAD_DIM
        hi = lax.broadcasted_iota(jnp.int32, (E, LANES), 1)
        sel = jnp.where(ci == hi, 1.0, 0.0).astype(BF16)
        acc = jnp.zeros((tm, LANES), F32)
        for part in _split3(d_o * ov):
            acc = acc + jnp.dot(part, sel, preferred_element_type=F32)
        dl_ref[...] = acc

    row = lambda w: pl.BlockSpec((tm, w), lambda i: (i, 0))
    return pl.pallas_call(
        body, name="dy_att", grid=(S // tm,),
        in_specs=[row(D), pl.BlockSpec((E, D), lambda i: (0, 0)), row(E), row(E)],
        out_specs=(row(E), row(E), row(LANES), row(D)),
        out_shape=(jax.ShapeDtypeStruct((S, E), F32), jax.ShapeDtypeStruct((S, E), BF16),
                   jax.ShapeDtypeStruct((S, LANES), F32), jax.ShapeDtypeStruct((S, D), BF16)),
        compiler_params=_params(("parallel",)),
    )(dxb, w_out, gates, o)


def _dy_conv(dxb, w_out, gates, conv_out, ln_g, ln_b, tm=512):
    S, D = dxb.shape
    C = conv_out.shape[1]

    def body(dx_ref, w_ref, gate_ref, cv_ref, g_ref, beta_ref, dgate_ref, dconv_ref, gg_ref, gb_ref, gcb_ref):
        @pl.when(pl.program_id(0) == 0)
        def _():
            gg_ref[...] = jnp.zeros_like(gg_ref)
            gb_ref[...] = jnp.zeros_like(gb_ref)
            gcb_ref[...] = jnp.zeros_like(gcb_ref)

        dyc = _nt(dx_ref[...], w_ref[...])
        cv = cv_ref[...]
        mu = jnp.mean(cv, axis=-1, keepdims=True)
        xc = cv - mu
        rstd = lax.rsqrt(jnp.mean(xc * xc, axis=-1, keepdims=True) + LN_EPS)
        nrm = xc * rstd
        gv = g_ref[...]
        ln = nrm * gv + beta_ref[...]
        u, du = _silu_and_grad(ln)
        gt = gate_ref[...]
        g2, dg2 = _silu_and_grad(gt)
        dgate_ref[...] = (dyc * u * dg2).astype(BF16)
        d_ln = dyc * g2 * du
        gb_ref[...] += jnp.sum(d_ln, axis=0, keepdims=True)
        gg_ref[...] += jnp.sum(d_ln * nrm, axis=0, keepdims=True)
        dn = d_ln * gv
        d_conv = rstd * (dn - jnp.mean(dn, axis=-1, keepdims=True)
                         - nrm * jnp.mean(dn * nrm, axis=-1, keepdims=True))
        dconv_ref[...] = d_conv
        gcb_ref[...] += jnp.sum(d_conv, axis=0, keepdims=True)

    row = lambda w: pl.BlockSpec((tm, w), lambda i: (i, 0))
    vec = pl.BlockSpec((1, C), lambda i: (0, 0))
    return pl.pallas_call(
        body, name="dy_conv", grid=(S // tm,),
        in_specs=[row(D), pl.BlockSpec((C, D), lambda i: (1, 0)),
                  pl.BlockSpec((tm, C), lambda i: (i, 2)), row(C), vec, vec],
        out_specs=(row(C), row(C), vec, vec, vec),
        out_shape=(jax.ShapeDtypeStruct((S, C), BF16), jax.ShapeDtypeStruct((S, C), F32),
                   jax.ShapeDtypeStruct((1, C), F32), jax.ShapeDtypeStruct((1, C), F32),
                   jax.ShapeDtypeStruct((1, C), F32)),
        compiler_params=_params(("arbitrary",)),
    )(dxb, w_out, gates, conv_out, ln_g, ln_b)


def _conv_bwd(d_conv, gates, d_c_gate, conv_w, hosted=None, tt=256):
    S, C = d_conv.shape
    hb = tt // CONV_HALO
    nt = S // tt
    hn = hosted.n if hosted is not None else 0

    def body(*refs):
        dc_ref, dnext_ref, val_ref, glu_ref, dg_ref, w_ref = refs[:6]
        h_ins = refs[6:6 + hn]
        out_ref, gw_ref = refs[6 + hn:8 + hn]
        h_outs = refs[8 + hn:8 + 2 * hn]
        hbuf, dbuf, dhbuf, dph, wb = refs[8 + 2 * hn:13 + 2 * hn]
        h_sems = refs[13 + 2 * hn:]
        i = pl.program_id(0)

        @pl.when(i == 0)
        def _():
            gw_ref[...] = jnp.zeros_like(gw_ref)
            _broadcast_taps(w_ref, wb)
            if hosted is not None:
                hosted.start(h_ins, h_outs, h_sems)

        val = val_ref[...]
        sg = _sigmoid(glu_ref[...])
        hbuf[...] = val * sg
        dbuf[0:tt, :] = dc_ref[...]
        dbuf[tt:, :] = jnp.where(i < nt - 1, dnext_ref[...], 0.0)
        _shifted_copies(dbuf, dph)
        for cb in range(C // LANES):
            cols = slice(cb * LANES, (cb + 1) * LANES)
            gacc = [jnp.zeros((8, LANES), F32) for _ in range(CONV_K)]
            group = 2
            for rc0 in range(0, tt // 8, group):
                hcur = [hbuf[(rc0 + r) * 8:(rc0 + r + 1) * 8, cols] for r in range(group)]
                accs = [jnp.zeros((8, LANES), F32) for _ in range(group)]
                for j in range(CONV_K):
                    wj = wb[j, :, cols]
                    for r in range(group):
                        dwin = _window(dbuf, dph, (rc0 + r) * 8 + (CONV_K - 1) - j, cols)
                        accs[r] = accs[r] + dwin * wj
                        gacc[j] = gacc[j] + dwin * hcur[r]
                for r in range(group):
                    dhbuf[(rc0 + r) * 8:(rc0 + r + 1) * 8, cols] = accs[r]
            for j in range(CONV_K):
                gw_ref[j:j + 1, cols] += jnp.sum(gacc[j], axis=0, keepdims=True)
        d_h = dhbuf[...]
        out_ref[:, 0:C] = (d_h * sg).astype(BF16)
        out_ref[:, C:2 * C] = (d_h * val * sg * (1.0 - sg)).astype(BF16)
        out_ref[:, 2 * C:3 * C] = dg_ref[...]

        if hosted is not None:
            @pl.when(i == nt - 1)
            def _():
                hosted.finish(h_ins, h_outs, h_sems)

    tile = lambda col: pl.BlockSpec((tt, C), lambda i: (i, col))
    in_specs = [tile(0),
                pl.BlockSpec((CONV_HALO, C), lambda i: (jnp.minimum((i + 1) * hb, S // CONV_HALO - 1), 0)),
                tile(0), tile(1), tile(0),
                pl.BlockSpec((CONV_HALO, C), lambda i: (0, 0))]
    args = [d_conv, d_conv, gates, gates, d_c_gate, conv_w]
    out_specs = [pl.BlockSpec((tt, 3 * C), lambda i: (i, 0)), pl.BlockSpec((CONV_HALO, C), lambda i: (0, 0))]
    out_shape = [jax.ShapeDtypeStruct((S, 3 * C), BF16), jax.ShapeDtypeStruct((CONV_HALO, C), F32)]
    scratch = [pltpu.VMEM((tt, C), F32), pltpu.VMEM((tt + CONV_HALO, C), F32), pltpu.VMEM((tt, C), F32),
               pltpu.VMEM((7, tt + CONV_HALO - 8, C), F32), pltpu.VMEM((CONV_K, 8, C), F32)]
    if hosted is not None:
        in_specs += [ANY_SPEC] * hn
        args += hosted.arrays
        out_specs += [ANY_SPEC] * hn
        out_shape += hosted.out_shapes()
        scratch += hosted.sem_shapes()
    res = pl.pallas_call(
        body, name="conv_bwd", grid=(nt,),
        in_specs=in_specs, out_specs=tuple(out_specs), out_shape=tuple(out_shape), scratch_shapes=scratch,
        compiler_params=_params(("arbitrary",)),
    )(*args)
    return res[0], res[1], list(res[2:])


def _attn_bwd(q, kv, d_o, lse, delta, dil, prev, final, name, hosted=None):
    S = q.shape[0]
    rows = _Rows(dil, S)
    nb = rows.nb
    out_dt = BF16 if final else F32
    have_prev = prev is not None
    hn = hosted.n if hosted is not None else 0

    def body(*refs):
        refs = list(refs)
        q_ref, do_ref, lse_ref, dl_ref, kvc_ref, kvp_ref = refs[:6]
        del refs[:6]
        if have_prev:
            pdq_ref, pdkv_ref = refs[:2]
            del refs[:2]
        h_ins = refs[:hn]
        dq_ref, dkv_ref = refs[hn:hn + 2]
        h_outs = refs[hn + 2:2 * hn + 2]
        carry, tbl = refs[2 * hn + 2:2 * hn + 4]
        h_sems = refs[2 * hn + 4:]
        n = pl.program_id(1)

        @pl.when((pl.program_id(0) == 0) & (n == 0))
        def _():
            if hosted is not None:
                hosted.start(h_ins, h_outs, h_sems)
            _fill_bias_table(tbl, rows, keys_first=True)

        @pl.when(n == 0)
        def _():
            carry[...] = jnp.zeros_like(carry)

        @pl.when(n < nb)
        def _():
            kv2 = jnp.concatenate([_ld(kvp_ref), _ld(kvc_ref)], axis=0)
            lse_t, dl_t = _ld(lse_ref).T, _ld(dl_ref).T
            lo_mask = lax.broadcasted_iota(jnp.int32, (2 * BLK, LANES), 1) < HEAD_DIM
            halves = [jnp.zeros((2 * BLK, LANES), F32) for _ in range(4)]
            for hk in range(N_KV_HEADS):
                k_lo, k_hi, v_lo, v_hi = _head_operands(kv2, hk, lo_mask)
                cols = [slice(b * LANES, (b + 1) * LANES) for b in (2 * hk, 2 * hk + 1)]
                q2 = jnp.concatenate([_ld(q_ref, cols[0]), _ld(q_ref, cols[1])], axis=0).astype(BF16)
                do2 = jnp.concatenate([_ld(do_ref, cols[0]), _ld(do_ref, cols[1])], axis=0).astype(BF16)
                dq2 = jnp.zeros((2 * BLK, LANES), F32)
                dks, dvs = [], []
                for which, (kk, vv) in enumerate(((k_lo, v_lo), (k_hi, v_hi))):
                    h0, h1 = 4 * hk + which, 4 * hk + 2 + which
                    s = _nt(kk, q2) + _bias2(tbl, n, h0, h1, axis=1)
                    lse2 = jnp.concatenate([lse_t[h0:h0 + 1, :], lse_t[h1:h1 + 1, :]], axis=1)
                    dl2 = jnp.concatenate([dl_t[h0:h0 + 1, :], dl_t[h1:h1 + 1, :]], axis=1)
                    p = jnp.exp(s - lse2)
                    ds = (p * (_nt(vv, do2) - dl2)).astype(BF16)
                    dq2 = dq2 + _tn(ds, kk)
                    dks.append(jnp.dot(ds, q2, preferred_element_type=F32))
                    dvs.append(jnp.dot(p.astype(BF16), do2, preferred_element_type=F32))
                dk_sum = jnp.where(lo_mask, dks[0], dks[1])
                dv_sum = jnp.where(lo_mask, dvs[0], dvs[1])
                for jp in range(2):
                    dq_blk = dq2[jp * BLK:(jp + 1) * BLK]
                    if have_prev:
                        dq_blk = dq_blk + _ld(pdq_ref, cols[jp])
                    if final:
                        dq_blk = dq_blk * (HEAD_DIM ** -0.5)
                    _st(dq_ref, dq_blk.astype(out_dt), cols[jp])
                half, pos = hk // 2, hk % 2
                here = lo_mask if pos == 0 else jnp.logical_not(lo_mask)
                dk_tot = dk_sum + pltpu.roll(dk_sum, HEAD_DIM, axis=1)
                dv_tot = dv_sum + pltpu.roll(dv_sum, HEAD_DIM, axis=1)
                halves[half] = halves[half] + jnp.where(here, dk_tot, 0.0)
                halves[2 + half] = halves[2 + half] + jnp.where(here, dv_tot, 0.0)
            for b in range(4):
                cols = slice(b * LANES, (b + 1) * LANES)
                done = carry[:, cols] + halves[b][0:BLK, :]
                if have_prev:
                    done = done + _ld(pdkv_ref, cols)
                _st(dkv_ref, done.astype(out_dt), cols)
                carry[:, cols] = halves[b][BLK:, :]

        @pl.when(n == nb)
        def _():
            done = carry[...]
            if have_prev:
                done = done + _ld(pdkv_ref)
            _st(dkv_ref, done.astype(out_dt))

        if hosted is not None:
            @pl.when((pl.program_id(0) == dil - 1) & (n == nb))
            def _():
                hosted.finish(h_ins, h_outs, h_sems)

    cur = lambda n: jnp.minimum(n, nb - 1)
    behind = lambda n: jnp.maximum(n - 1, 0)
    in_specs = [rows.spec(ATT_W, cur), rows.spec(ATT_W, cur), rows.spec(LANES, cur), rows.spec(LANES, cur),
                rows.spec(2 * KV_W, cur), rows.spec(2 * KV_W, behind)]
    args = [rows.of(q), rows.of(d_o), rows.of(lse), rows.of(delta), rows.of(kv), rows.of(kv)]
    if have_prev:
        in_specs += [rows.spec(ATT_W, cur), rows.spec(2 * KV_W, behind)]
        args += [rows.of(prev[0]), rows.of(prev[1])]
    out_specs = [rows.spec(ATT_W, cur), rows.spec(2 * KV_W, behind)]
    out_shape = [jax.ShapeDtypeStruct(rows.view + (ATT_W,), out_dt),
                 jax.ShapeDtypeStruct(rows.view + (2 * KV_W,), out_dt)]
    scratch = [pltpu.VMEM((BLK, 2 * KV_W), F32), pltpu.VMEM((2 * N_Q_HEADS, 2 * BLK, BLK), F32)]
    if hosted is not None:
        in_specs += [ANY_SPEC] * hn
        args += hosted.arrays
        out_specs += [ANY_SPEC] * hn
        out_shape += hosted.out_shapes()
        scratch += hosted.sem_shapes()
    res = pl.pallas_call(
        body, name=name, grid=(dil, nb + 1),
        in_specs=in_specs, out_specs=tuple(out_specs), out_shape=tuple(out_shape), scratch_shapes=scratch,
        compiler_params=_params(("arbitrary", "arbitrary")),
    )(*args)
    return (res[0].reshape(S, ATT_W), res[1].reshape(S, 2 * KV_W)), list(res[2:])


def _dh(segments, w_in, x, dx2, g, hosted=None, tm=1024, tk=512):
    S, D = x.shape
    ns = len(segments)
    counts = [a.shape[1] // tk for a, _ in segments]
    starts = [sum(counts[:s]) for s in range(ns)]
    nk = sum(counts)
    hn = hosted.n if hosted is not None else 0

    def body(*refs):
        seg_refs = refs[:ns]
        w_ref, x_ref, dx2_ref, g_ref = refs[ns:ns + 4]
        h_ins = refs[ns + 4:ns + 4 + hn]
        gx_ref, gng_ref = refs[ns + 4 + hn:ns + 6 + hn]
        h_outs = refs[ns + 6 + hn:ns + 6 + 2 * hn]
        acc = refs[ns + 6 + 2 * hn]
        h_sems = refs[ns + 7 + 2 * hn:]
        i, k = pl.program_id(0), pl.program_id(1)

        @pl.when((i == 0) & (k == 0))
        def _():
            gng_ref[...] = jnp.zeros_like(gng_ref)
            if hosted is not None:
                hosted.start(h_ins, h_outs, h_sems)

        @pl.when(k == 0)
        def _():
            acc[...] = jnp.zeros_like(acc)

        for s in range(ns):
            @pl.when((k >= starts[s]) & (k < starts[s] + counts[s]))
            def _(s=s):
                t = seg_refs[s][...]
                if segments[s][1]:
                    t = _perm_rows(t, True)
                acc[...] += _nt(t, w_ref[...])

        @pl.when(k == nk - 1)
        def _():
            dh = acc[...]
            xf = x_ref[...]
            r = lax.rsqrt(jnp.mean(xf * xf, axis=-1, keepdims=True) + NORM_EPS)
            nrm = xf * r
            gng_ref[...] += jnp.sum(dh * nrm, axis=0, keepdims=True)
            dn = dh * g_ref[...]
            gx_ref[...] = dx2_ref[...] + r * (dn - nrm * jnp.mean(dn * nrm, axis=-1, keepdims=True))

        if hosted is not None:
            @pl.when((i == S // tm - 1) & (k == nk - 1))
            def _():
                hosted.finish(h_ins, h_outs, h_sems)

    row = pl.BlockSpec((tm, D), lambda i, k: (i, 0))
    vec = pl.BlockSpec((1, D), lambda i, k: (0, 0))
    in_specs = [pl.BlockSpec((tm, tk), lambda i, k, s=s: (i, jnp.clip(k - starts[s], 0, counts[s] - 1)))
                for s in range(ns)]
    in_specs += [pl.BlockSpec((D, tk), lambda i, k: (0, k)), row, row, vec]
    args = [a for a, _ in segments] + [w_in, x, dx2, g]
    out_specs = [row, vec]
    out_shape = [jax.ShapeDtypeStruct((S, D), F32), jax.ShapeDtypeStruct((1, D), F32)]
    scratch = [pltpu.VMEM((tm, D), F32)]
    if hosted is not None:
        in_specs += [ANY_SPEC] * hn
        args += hosted.arrays
        out_specs += [ANY_SPEC] * hn
        out_shape += hosted.out_shapes()
        scratch += hosted.sem_shapes()
    res = pl.pallas_call(
        body, name="dh", grid=(S // tm, nk),
        in_specs=in_specs, out_specs=tuple(out_specs), out_shape=tuple(out_shape), scratch_shapes=scratch,
        compiler_params=_params(("arbitrary", "arbitrary")),
    )(*args)
    return res[0], res[1], list(res[2:])


def _tn_matmul(a, bs, name, tm=512):
    M, K = a.shape
    nb = len(bs)

    def body(a_ref, *refs):
        @pl.when(pl.program_id(0) == 0)
        def _():
            for o_ref in refs[nb:]:
                o_ref[...] = jnp.zeros_like(o_ref)

        at = a_ref[...]
        for b_ref, o_ref in zip(refs[:nb], refs[nb:]):
            for c in range(0, b_ref.shape[1], 512):
                o_ref[:, c:c + 512] += _tn(at, b_ref[:, c:c + 512])

    return pl.pallas_call(
        body, name=name, grid=(M // tm,),
        in_specs=[pl.BlockSpec((tm, K), lambda m: (m, 0))] + [pl.BlockSpec((tm, b.shape[1]), lambda m: (m, 0))
                                                              for b in bs],
        out_specs=tuple(pl.BlockSpec((K, b.shape[1]), lambda m: (0, 0)) for b in bs),
        out_shape=tuple(jax.ShapeDtypeStruct((K, b.shape[1]), F32) for b in bs),
        compiler_params=_params(("arbitrary",)),
    )(a, *bs)


def _adamw(parts, w, m, v, name, tr=None, split=None, by_chip=False):
    R, C = w.shape
    tr = R if tr is None else tr
    parts = [parts] if split is None else list(parts)
    npar = len(parts)

    def total(p_ref):
        if by_chip:
            c = lax.axis_index("c")
            g = p_ref[c].astype(F32)
            for chip in range(1, N_DEV // 2):
                g = g + p_ref[2 * chip + c].astype(F32)
            return g
        g = p_ref[0].astype(F32)
        for dev in range(1, N_DEV):
            g = g + p_ref[dev].astype(F32)
        return g

    def body(*refs):
        w_ref, m_ref, v_ref, g_out, d_out, m_out, v_out = refs[npar:]
        if split is None:
            g = total(refs[0])
        else:
            g = jnp.where(_mesh_pos()[3] < split, total(refs[0]), total(refs[1]))
        mn = ADAM_B1 * m_ref[...] + (1.0 - ADAM_B1) * g
        vn = ADAM_B2 * v_ref[...] + (1.0 - ADAM_B2) * (g * g)
        m_hat = mn / (1.0 - ADAM_B1 ** ADAM_STEP)
        v_hat = vn / (1.0 - ADAM_B2 ** ADAM_STEP)
        g_out[...] = g
        d_out[...] = -ADAM_LR * (m_hat / (jnp.sqrt(v_hat) + ADAM_EPS) + ADAM_WD * w_ref[...])
        m_out[...] = mn
        v_out[...] = vn

    blk = pl.BlockSpec((tr, C), lambda i: (i, 0))
    shp = jax.ShapeDtypeStruct((R, C), F32)
    return pl.pallas_call(
        body, name=name, grid=(R // tr,),
        in_specs=[pl.BlockSpec((N_DEV, tr, C), lambda i: (0, i, 0))] * npar + [blk, blk, blk],
        out_specs=(blk, blk, blk, blk), out_shape=(shp, shp, shp, shp),
        compiler_params=_params(("parallel",)),
    )(*parts, w, m, v)


def _local_step(x, target, norm_g, w_in, conv_w, conv_b, ln_g, ln_b, w_out, gf, exchanges=None):
    ex_out, ex_att, ex_conv = exchanges if exchanges is not None else (None, None, None)
    conv_cols = w_in.shape[1] - 2 * ATT_W - 2 * KV_W
    q, kv, a_gate, gates, h_rm, h = _inproj(
        x, norm_g, w_in, [(ATT_W, HEAD_DIM ** -0.5, True), (2 * KV_W, 1.0, True), (ATT_W, 1.0, True),
                          (conv_cols, 1.0, False)])

    merged = None
    for idx, (_, dil) in enumerate(reversed(PATTERNS)):
        merged = _attn_fwd(q, kv, dil, "attn_fwd_d%d" % dil, merged,
                           a_gate if idx == len(PATTERNS) - 1 else None)
    o, lse, y_att = merged
    conv_out, y_conv = _conv_fwd(gates, conv_w, conv_b, ln_g, ln_b)
    dx2, dxb, loss_cols, g_gf = _outproj_loss(x, y_att, y_conv, w_out, gf, target)

    d_o, d_a_gate, delta, dxb_rm = _dy_att(dxb, w_out, a_gate, o)
    g_w_out = jnp.concatenate([_tn_matmul(y_att, [dxb_rm], "gw_out_att")[0],
                               _tn_matmul(y_conv, [dxb], "gw_out_conv")[0]], axis=0)
    acc, out_parts = None, []
    for idx, (_, dil) in enumerate(reversed(PATTERNS)):
        hosted = ex_out(g_w_out) if (idx == 0 and ex_out is not None) else None
        acc, outs = _attn_bwd(q, kv, d_o, lse, delta, dil, acc, idx == len(PATTERNS) - 1, "attn_bwd_d%d" % dil,
                              hosted)
        out_parts += outs
    dq, dkv = acc
    g_q, g_kv, g_a = _tn_matmul(h_rm, [dq, dkv, d_a_gate], "gw_in_att")

    d_c_gate, d_conv, g_ln_g, g_ln_b, g_conv_b = _dy_conv(dxb, w_out, gates, conv_out, ln_g, ln_b)
    dgates, g_conv_w, att_parts = _conv_bwd(d_conv, gates, d_c_gate, conv_w,
                                            ex_att(g_q, g_kv, g_a) if ex_att is not None else None)
    g_c, = _tn_matmul(h, [dgates], "gw_in_conv")
    grad_x, g_norm_g, conv_parts = _dh(
        [(dq, True), (dkv, True), (d_a_gate, True), (dgates, False)], w_in, x, dx2, norm_g,
        ex_conv(g_a, g_c, g_conv_w) if ex_conv is not None else None)
    small = (g_norm_g, g_conv_b, g_ln_g, g_ln_b, g_gf, loss_cols)
    return grad_x, (g_q, g_kv, g_a, g_c), g_w_out, g_conv_w, small, (out_parts, att_parts, conv_parts)


def kernel(x, norm_g, w_in, conv_w, conv_b, conv_ln_g, conv_ln_b, w_out, final_norm_g, loss_target, m_norm_g, m_w_in, m_conv_w, m_conv_b, m_conv_ln_g, m_conv_ln_b, m_w_out, m_final_norm_g, v_norm_g, v_w_in, v_conv_w, v_conv_b, v_conv_ln_g, v_conv_ln_b, v_w_out, v_final_norm_g):
    S, D = x.shape[1], x.shape[2]
    win_sh, wout_sh, cw_sh = w_in[0], w_out[0], conv_w[0]
    cols_sh, rows_sh, ch_sh = win_sh.shape[1], wout_sh.shape[0], cw_sh.shape[1]

    win_all, wout_all, cw_all = _gather_two_level(
        [win_sh.astype(BF16), wout_sh.astype(BF16), cw_sh], "gather_weights")
    w_in_full = win_all.transpose(1, 0, 2).reshape(D, N_DEV * cols_sh)
    w_out_full = wout_all.reshape(N_DEV * rows_sh, D)
    conv_w_full = cw_all.transpose(1, 0, 2).reshape(CONV_K, N_DEV * ch_sh)
    conv_w_full = jnp.pad(conv_w_full, ((0, CONV_HALO - CONV_K), (0, 0)))
    gf = final_norm_g.reshape(1, D)

    first = -(-(ATT_W + 2 * KV_W) // cols_sh)
    a_off = first * cols_sh - (ATT_W + 2 * KV_W)
    assert 0 <= a_off <= ATT_W

    def pieces(g, n):
        return g.reshape(D, n, cols_sh).transpose(1, 0, 2).astype(BF16)

    def ex_out(g_w_out):
        return _Exchange([g_w_out.reshape(N_DEV, rows_sh, D).astype(BF16)], [(0, N_DEV)])

    same_core = (2, 4, 6)

    def ex_att(g_q, g_kv, g_a):
        mine = _chip_sum(pieces(jnp.concatenate([g_q, g_kv, g_a[:, :a_off]], axis=1), first), 0, "rs_att")
        return _Exchange([mine], [(0, first)], [same_core])

    def ex_conv(g_a, g_c, g_conv_w):
        mine = _chip_sum(pieces(jnp.concatenate([g_a[:, a_off:], g_c], axis=1), N_DEV - first), first, "rs_conv")
        return _Exchange(
            [mine, g_conv_w[:CONV_K].reshape(CONV_K, N_DEV, ch_sh).transpose(1, 0, 2)],
            [(first, N_DEV), (0, N_DEV)], [same_core, None])

    grad_x, _, _, _, small, parts = _local_step(
        x[0], loss_target[0], norm_g, w_in_full, conv_w_full, conv_b, conv_ln_g, conv_ln_b, w_out_full, gf,
        (ex_out, ex_att, ex_conv))
    (wout_parts,), (win_parts_lo,), (win_parts_hi, cw_parts) = parts

    small_pack = jnp.concatenate(list(small) + [jnp.zeros((2, D), F32)], axis=0)
    small_parts, = _exchange([small_pack], [None], "gather_small")

    upd_win = _adamw((win_parts_lo, win_parts_hi), win_sh, m_w_in[0], v_w_in[0], "adamw_w_in", tr=256, split=first,
                     by_chip=True)
    upd_wout = _adamw(wout_parts, wout_sh, m_w_out[0], v_w_out[0], "adamw_w_out", tr=128)
    upd_cw = _adamw(cw_parts, cw_sh, m_conv_w[0], v_conv_w[0], "adamw_conv_w")
    zeros3 = jnp.zeros((3, D), F32)
    stack = lambda a, b, c, d_, e: jnp.concatenate([a, b, c, d_, e.reshape(1, D), zeros3], axis=0)
    upd_small = _adamw(
        small_parts,
        stack(norm_g, conv_b, conv_ln_g, conv_ln_b, final_norm_g),
        stack(m_norm_g, m_conv_b, m_conv_ln_g, m_conv_ln_b, m_final_norm_g),
        stack(v_norm_g, v_conv_b, v_conv_ln_g, v_conv_ln_b, v_final_norm_g) + jnp.concatenate(
            [jnp.zeros((5, D), F32), jnp.ones((3, D), F32)], axis=0),
        "adamw_small")

    loss = 0.5 / D * jnp.sum(upd_small[0][5])

    def outputs(kind):
        sm = upd_small[kind]
        return [sm[0:1], upd_win[kind][None], upd_cw[kind][None], sm[1:2], sm[2:3], sm[3:4],
                upd_wout[kind][None], sm[4]]

    return (loss, grad_x[None], *outputs(0), *outputs(1), *outputs(2), *outputs(3))
```

```python
import jax
import jax.numpy as jnp
from jax import lax
from jax.experimental import pallas as pl
from jax.experimental.pallas import tpu as pltpu

F32 = jnp.float32
BF16 = jnp.bfloat16

HEAD_DIM = 64
N_KV_HEADS = 4
N_Q_HEADS = 16
ATT_W = 1024
KV_W = 256
CONV_K = 31
CONV_HALO = 32
PATTERNS = ((128, 1), (512, 4), (2048, 16))
BLK = 128
LANES = 128
NORM_EPS = 1e-6
LN_EPS = 1e-5
NEG = -1e30
N_DEV = 8
ADAM_LR, ADAM_B1, ADAM_B2, ADAM_EPS, ADAM_WD, ADAM_STEP = 0.001, 0.9, 0.999, 1e-08, 0.01, 10
VMEM_LIMIT = 48 * 1024 * 1024
SLOPES = tuple(2.0 ** (-8.0 * (h + 1) / N_Q_HEADS) for h in range(N_Q_HEADS))
MESH = pl.DeviceIdType.MESH


def _params(sem):
    return pltpu.CompilerParams(dimension_semantics=sem, vmem_limit_bytes=VMEM_LIMIT)


def _sigmoid(v):
    return 1.0 / (1.0 + jnp.exp(-v))


def _silu_and_grad(v):
    s = _sigmoid(v)
    return v * s, s * (1.0 + v * (1.0 - s))


ANY_SPEC = pl.BlockSpec(memory_space=pl.ANY)


def _mesh_pos():
    x, y, c = lax.axis_index("x"), lax.axis_index("y"), lax.axis_index("c")
    return x, y, c, 4 * x + 2 * y + c


def _flipped(k, x, y, c):
    px = 1 - x if k & 4 else x
    py = 1 - y if k & 2 else y
    pc = 1 - c if k & 1 else c
    return (px, py, pc), 4 * px + 2 * py + pc


class _Exchange:
    def __init__(self, arrays, dests, flips=None):
        self.arrays, self.dests, self.n = list(arrays), list(dests), len(arrays)
        self.flips = [tuple(range(1, N_DEV)) if f is None else tuple(f)
                      for f in (flips if flips is not None else [None] * self.n)]

    def out_shapes(self):
        return [jax.ShapeDtypeStruct((N_DEV,) + a.shape[-2:], a.dtype) for a in self.arrays]

    def sem_shapes(self):
        return [pltpu.SemaphoreType.DMA((self.n, N_DEV - 1)), pltpu.SemaphoreType.DMA((self.n, N_DEV - 1)),
                pltpu.SemaphoreType.DMA((self.n,))]

    def _when(self, a, dev, fn):
        if self.dests[a] is None:
            fn()
        else:
            lo, hi = self.dests[a]
            pl.when((dev >= lo) & (dev < hi))(fn)

    def _mine(self, ins, a, dev):
        return ins[a] if self.dests[a] is None else ins[a].at[dev - self.dests[a][0]]

    def _copy(self, ins, outs, sems, a, k, src_dev, slot, target):
        return pltpu.make_async_remote_copy(
            src_ref=self._mine(ins, a, src_dev), dst_ref=outs[a].at[slot],
            send_sem=sems[0].at[a, k - 1], recv_sem=sems[1].at[a, k - 1],
            device_id=target, device_id_type=MESH)

    def start(self, ins, outs, sems):
        x, y, c, me = _mesh_pos()
        for a in range(self.n):
            self._when(a, me, lambda a=a: pltpu.make_async_copy(
                self._mine(ins, a, me), outs[a].at[me], sems[2].at[a]).start())
            for k in self.flips[a]:
                target, peer = _flipped(k, x, y, c)
                self._when(a, peer, lambda a=a, k=k, target=target, peer=peer: self._copy(
                    ins, outs, sems, a, k, peer, me, target).start())

    def finish(self, ins, outs, sems):
        x, y, c, me = _mesh_pos()
        lo0 = [0 if d is None else d[0] for d in self.dests]
        for a in range(self.n):
            for k in self.flips[a]:
                target, peer = _flipped(k, x, y, c)
                self._when(a, me, lambda a=a, k=k, peer=peer: self._copy(
                    ins, outs, sems, a, k, lo0[a], peer, (x, y, c)).wait_recv())
            for k in self.flips[a]:
                target, peer = _flipped(k, x, y, c)
                self._when(a, peer, lambda a=a, k=k, target=target, peer=peer: self._copy(
                    ins, outs, sems, a, k, peer, me, target).wait_send())
            self._when(a, me, lambda a=a: pltpu.make_async_copy(
                self._mine(ins, a, me), outs[a].at[me], sems[2].at[a]).wait())


def _exchange(arrays, dests, name, flips=None):
    ex = _Exchange(arrays, dests, flips)
    na = ex.n

    def body(*refs):
        ins, outs, sems = refs[:na], refs[na:2 * na], refs[2 * na:]
        ex.start(ins, outs, sems)
        ex.finish(ins, outs, sems)

    return pl.pallas_call(
        body, name=name, out_shape=tuple(ex.out_shapes()),
        in_specs=[ANY_SPEC] * na, out_specs=tuple([ANY_SPEC] * na), scratch_shapes=ex.sem_shapes(),
    )(*arrays)


def _chip_sum(pieces, lo, name):
    n, R, C = pieces.shape

    def swap(p_ref, t_ref, send_sems, recv_sems):
        x, y, c, me = _mesh_pos()
        for i in range(n):
            mine = (lo + i) % 2
            cp = pltpu.make_async_remote_copy(
                src_ref=p_ref.at[i], dst_ref=t_ref.at[i], send_sem=send_sems.at[i], recv_sem=recv_sems.at[i],
                device_id=(x, y, 1 - c), device_id_type=MESH)
            pl.when(c != mine)(cp.start)
        for i in range(n):
            mine = (lo + i) % 2
            cp = pltpu.make_async_remote_copy(
                src_ref=p_ref.at[i], dst_ref=t_ref.at[i], send_sem=send_sems.at[i], recv_sem=recv_sems.at[i],
                device_id=(x, y, 1 - c), device_id_type=MESH)
            pl.when(c == mine)(cp.wait_recv)
            pl.when(c != mine)(cp.wait_send)

    other = pl.pallas_call(
        swap, name=name + "_swap", out_shape=jax.ShapeDtypeStruct(pieces.shape, pieces.dtype),
        in_specs=[ANY_SPEC], out_specs=ANY_SPEC,
        scratch_shapes=[pltpu.SemaphoreType.DMA((n,)), pltpu.SemaphoreType.DMA((n,))],
    )(pieces)

    def add(p_ref, t_ref, o_ref):
        o_ref[...] = (p_ref[...].astype(F32) + t_ref[...].astype(F32)).astype(o_ref.dtype)

    tr = R // 2
    blk = pl.BlockSpec((None, tr, C), lambda i, r: (i, r, 0))
    return pl.pallas_call(
        add, name=name + "_add", grid=(n, R // tr), in_specs=[blk, blk], out_specs=blk,
        out_shape=jax.ShapeDtypeStruct(pieces.shape, pieces.dtype),
        compiler_params=_params(("parallel", "parallel")),
    )(pieces, other)


def _gather_two_level(arrays, name):
    na = len(arrays)

    def body(*refs):
        ins, outs = refs[:na], refs[na:2 * na]
        send_sems, recv_sems, loc_sems = refs[2 * na:]
        x, y, c, me = _mesh_pos()
        sibling = (x, y, 1 - c)
        chips = [(1 - x, y), (x, 1 - y), (1 - x, 1 - y)]

        def slot(px, py, pc):
            return 4 * px + 2 * py + pc

        def copy(a, k, src, block, to):
            return pltpu.make_async_remote_copy(
                src_ref=src, dst_ref=outs[a].at[slot(*block)], send_sem=send_sems.at[a, k], recv_sem=recv_sems.at[a, k],
                device_id=to, device_id_type=MESH)

        local = [pltpu.make_async_copy(ins[a], outs[a].at[me], loc_sems.at[a]) for a in range(na)]
        for cp in local:
            cp.start()
        started = []
        for a in range(na):
            started.append(copy(a, 0, ins[a], (x, y, c), sibling))
            started += [copy(a, 1 + j, ins[a], (x, y, c), (*chip, c)) for j, chip in enumerate(chips)]
        for cp in started:
            cp.start()
        for j, chip in enumerate(chips):
            for a in range(na):
                copy(a, 1 + j, ins[a], (*chip, c), (x, y, c)).wait_recv()
                fwd = copy(a, 4 + j, outs[a].at[slot(*chip, c)], (*chip, c), sibling)
                fwd.start()
                started.append(fwd)
        for a in range(na):
            copy(a, 0, ins[a], sibling, (x, y, c)).wait_recv()
            for j, chip in enumerate(chips):
                copy(a, 4 + j, ins[a], (*chip, 1 - c), (x, y, c)).wait_recv()
        for cp in started:
            cp.wait_send()
        for cp in local:
            cp.wait()

    return pl.pallas_call(
        body, name=name,
        out_shape=tuple(jax.ShapeDtypeStruct((N_DEV,) + a.shape, a.dtype) for a in arrays),
        in_specs=[ANY_SPEC] * na, out_specs=tuple([ANY_SPEC] * na),
        scratch_shapes=[pltpu.SemaphoreType.DMA((na, N_DEV - 1)), pltpu.SemaphoreType.DMA((na, N_DEV - 1)),
                        pltpu.SemaphoreType.DMA((na,))],
    )(*arrays)


CHUNK = 128
RESIDUES = 16
PER_RES = CHUNK // RESIDUES


def _perm_rows(tile, inverse):
    a = lax.broadcasted_iota(jnp.int32, (CHUNK, CHUNK), 0)
    b = lax.broadcasted_iota(jnp.int32, (CHUNK, CHUNK), 1)
    if inverse:
        a, b = b, a
    p = jnp.where(a == PER_RES * (b % RESIDUES) + b // RESIDUES, 1.0, 0.0).astype(BF16)
    parts = [jnp.dot(p, tile[c * CHUNK:(c + 1) * CHUNK], preferred_element_type=F32)
             for c in range(tile.shape[0] // CHUNK)]
    return jnp.concatenate(parts, axis=0).astype(BF16)


class _Rows:
    def __init__(self, dil, S):
        nc = S // CHUNK
        self.dil = dil
        if dil == 1:
            self.view, self.block, self.nb = (nc, CHUNK), (None, CHUNK), nc
            self.index = lambda r, b: (b, 0, 0)
        elif dil == 4:
            self.view, self.block, self.nb = (nc, 4, 4, PER_RES), (4, 4, None, PER_RES), nc // 4
            self.index = lambda r, b: (b, 0, r, 0, 0)
        elif dil == RESIDUES:
            self.view, self.block, self.nb = (nc, RESIDUES, PER_RES), (RESIDUES, None, PER_RES), nc // RESIDUES
            self.index = lambda r, b: (b, r, 0, 0)
        else:
            raise NotImplementedError(dil)

    def of(self, a):
        return a.reshape(self.view + (a.shape[-1],))

    def spec(self, width, which_block):
        return pl.BlockSpec(self.block + (width,), lambda r, n: self.index(r, which_block(n)))

    def pos(self, row):
        if self.dil == 1:
            return (row % PER_RES) * RESIDUES + row // PER_RES
        if self.dil == 4:
            return (row // 32) * 32 + (row % PER_RES) * 4 + (row % 32) // PER_RES
        return row


def _ld(ref, cols=slice(None)):
    v = ref[(slice(None),) * (len(ref.shape) - 1) + (cols,)]
    return v.reshape(BLK, v.shape[-1])


def _st(ref, val, cols=slice(None)):
    ref[(slice(None),) * (len(ref.shape) - 1) + (cols,)] = val.reshape(ref.shape[:-1] + (val.shape[-1],))


def _inproj(x, g, w_t, segments, tm=1024, tn=512):
    S, D = x.shape
    ns = len(segments)
    counts = [nc // tn for nc, _, _ in segments]
    starts = [sum(counts[:s]) for s in range(ns)]

    def body(x_ref, g_ref, w_ref, *rest):
        outs = rest[:ns]
        hrm_out, h_out, hrm_scr, h_scr = rest[ns:]
        j = pl.program_id(1)

        @pl.when(j == 0)
        def _():
            xf = x_ref[...]
            r = lax.rsqrt(jnp.mean(xf * xf, axis=-1, keepdims=True) + NORM_EPS)
            h = (xf * r * g_ref[...]).astype(BF16)
            hrm = _perm_rows(h, False)
            h_scr[...] = h
            hrm_scr[...] = hrm
            h_out[...] = h
            hrm_out[...] = hrm

        for s, (_, scale, rm) in enumerate(segments):
            @pl.when((j >= starts[s]) & (j < starts[s] + counts[s]))
            def _(s=s, scale=scale, rm=rm):
                acc = _nt((hrm_scr if rm else h_scr)[...], w_ref[...])
                outs[s][...] = acc * scale if scale != 1.0 else acc

    row = pl.BlockSpec((tm, D), lambda i, j: (i, 0))
    out_specs = [pl.BlockSpec((tm, tn), lambda i, j, s=s: (i, jnp.clip(j - starts[s], 0, counts[s] - 1)))
                 for s in range(ns)]
    out_shape = [jax.ShapeDtypeStruct((S, nc), F32) for nc, _, _ in segments]
    return pl.pallas_call(
        body, name="inproj", grid=(S // tm, sum(counts)),
        in_specs=[row, pl.BlockSpec((1, D), lambda i, j: (0, 0)), pl.BlockSpec((tn, D), lambda i, j: (j, 0))],
        out_specs=tuple(out_specs + [row, row]),
        out_shape=tuple(out_shape + [jax.ShapeDtypeStruct((S, D), BF16)] * 2),
        scratch_shapes=[pltpu.VMEM((tm, D), BF16), pltpu.VMEM((tm, D), BF16)],
        compiler_params=_params(("arbitrary", "arbitrary")),
    )(x, g, w_t)


def _fill_bias_table(tbl, rows, keys_first=False):
    shape = (2 * BLK, BLK) if keys_first else (BLK, 2 * BLK)
    qi = lax.broadcasted_iota(jnp.int32, shape, 1 if keys_first else 0)
    kj = lax.broadcasted_iota(jnp.int32, shape, 0 if keys_first else 1)
    dist = rows.pos(qi) - rows.pos(kj % BLK) + jnp.where(kj < BLK, BLK, 0)
    inside = (dist >= 0) & (dist <= BLK)
    negd = (dist * (-rows.dil)).astype(F32)
    for f, valid in enumerate((inside & (kj >= BLK), inside)):
        for h in range(N_Q_HEADS):
            tbl[f * N_Q_HEADS + h] = jnp.where(valid, SLOPES[h] * negd, NEG)


def _bias2(tbl, n, h0, h1, axis=0):
    base = jnp.where(n == 0, 0, N_Q_HEADS)
    return jnp.concatenate([tbl[base + h0], tbl[base + h1]], axis=axis)


def _head_operands(kv2, hk, lo_mask):
    half, pos = hk // 2, hk % 2
    out = []
    for base in (0, KV_W):
        t = kv2[:, base + half * LANES: base + (half + 1) * LANES]
        sw = pltpu.roll(t, HEAD_DIM, axis=1)
        at_lo, at_hi = (t, sw) if pos == 0 else (sw, t)
        out.append(jnp.where(lo_mask, at_lo, 0.0).astype(BF16))
        out.append(jnp.where(lo_mask, 0.0, at_hi).astype(BF16))
    return out


def _nt(a, b):
    return lax.dot_general(a, b, (((1,), (1,)), ((), ())), preferred_element_type=F32)


def _tn(a, b):
    return lax.dot_general(a, b, (((0,), (0,)), ((), ())), preferred_element_type=F32)


def _attn_fwd(q, kv, dil, name, prev=None, gate=None):
    S = q.shape[0]
    rows = _Rows(dil, S)
    nb = rows.nb
    have_prev, last = prev is not None, gate is not None

    def body(*refs):
        refs = list(refs)
        q_ref, kvc_ref, kvp_ref = refs[:3]
        del refs[:3]
        if have_prev:
            po_ref, pl_ref = refs[:2]
            del refs[:2]
        if last:
            gate_ref = refs.pop(0)
        o_ref, lse_ref = refs[:2]
        y_ref = refs[2] if last else None
        tbl = refs[-1]
        n = pl.program_id(1)

        @pl.when((pl.program_id(0) == 0) & (n == 0))
        def _():
            _fill_bias_table(tbl, rows)

        kv2 = jnp.concatenate([_ld(kvp_ref), _ld(kvc_ref)], axis=0)
        lo_mask = lax.broadcasted_iota(jnp.int32, (2 * BLK, LANES), 1) < HEAD_DIM
        lane = lax.broadcasted_iota(jnp.int32, (BLK, LANES), 1)
        stats = jnp.zeros((BLK, LANES), F32)
        for hk in range(N_KV_HEADS):
            k_lo, k_hi, v_lo, v_hi = _head_operands(kv2, hk, lo_mask)
            cols = [slice(b * LANES, (b + 1) * LANES) for b in (2 * hk, 2 * hk + 1)]
            q2 = jnp.concatenate([_ld(q_ref, cols[0]), _ld(q_ref, cols[1])], axis=0).astype(BF16)
            o2 = jnp.zeros((2 * BLK, LANES), F32)
            for which, (kk, vv) in enumerate(((k_lo, v_lo), (k_hi, v_hi))):
                h0, h1 = 4 * hk + which, 4 * hk + 2 + which
                s = _nt(q2, kk) + _bias2(tbl, n, h0, h1)
                m = jnp.max(s, axis=1, keepdims=True)
                p = jnp.exp(s - m)
                l = jnp.sum(p, axis=1, keepdims=True)
                o2 = o2 + jnp.dot(p.astype(BF16), vv, preferred_element_type=F32) * (1.0 / l)
                lse = m + jnp.log(l)
                stats = jnp.where(lane == h0, lse[0:BLK], stats)
                stats = jnp.where(lane == h1, lse[BLK:], stats)
            _st(o_ref, o2[0:BLK], cols[0])
            _st(o_ref, o2[BLK:], cols[1])
        if have_prev:
            before = _ld(pl_ref)
            top = jnp.maximum(before, stats)
            e_old, e_new = jnp.exp(before - top), jnp.exp(stats - top)
            total = e_old + e_new
            stats = top + jnp.log(total)
            inv = 1.0 / total
            w_old, w_new = e_old * inv, e_new * inv
        if have_prev or last:
            lo = lane < HEAD_DIM
            for blk in range(ATT_W // LANES):
                cols = slice(blk * LANES, (blk + 1) * LANES)
                o_blk = _ld(o_ref, cols)
                if have_prev:
                    pick = lambda w: jnp.where(lo, w[:, 2 * blk:2 * blk + 1], w[:, 2 * blk + 1:2 * blk + 2])
                    o_blk = o_blk * pick(w_new) + _ld(po_ref, cols) * pick(w_old)
                    _st(o_ref, o_blk, cols)
                if last:
                    a = _ld(gate_ref, cols)
                    _st(y_ref, (o_blk * (a * _sigmoid(a))).astype(BF16), cols)
        _st(lse_ref, stats)

    here = lambda n: n
    before_n = lambda n: jnp.maximum(n - 1, 0)
    in_specs = [rows.spec(ATT_W, here), rows.spec(2 * KV_W, here), rows.spec(2 * KV_W, before_n)]
    args = [rows.of(q), rows.of(kv), rows.of(kv)]
    if have_prev:
        in_specs += [rows.spec(ATT_W, here), rows.spec(LANES, here)]
        args += [rows.of(prev[0]), rows.of(prev[1])]
    out_specs = [rows.spec(ATT_W, here), rows.spec(LANES, here)]
    out_shape = [jax.ShapeDtypeStruct(rows.view + (ATT_W,), F32), jax.ShapeDtypeStruct(rows.view + (LANES,), F32)]
    if last:
        in_specs.append(rows.spec(ATT_W, here))
        args.append(rows.of(gate))
        out_specs.append(rows.spec(ATT_W, here))
        out_shape.append(jax.ShapeDtypeStruct(rows.view + (ATT_W,), BF16))
    res = pl.pallas_call(
        body, name=name, grid=(dil, nb),
        in_specs=in_specs, out_specs=tuple(out_specs), out_shape=tuple(out_shape),
        scratch_shapes=[pltpu.VMEM((2 * N_Q_HEADS, BLK, 2 * BLK), F32)],
        compiler_params=_params(("arbitrary", "arbitrary")),
    )(*args)
    return tuple(r.reshape(S, r.shape[-1]) for r in res)


def _shifted_copies(buf, phases):
    n = phases.shape[1]
    for b in range(1, 8):
        phases[b - 1] = buf[b:b + n, :]


def _window(buf, phases, start, cols):
    b = start % 8
    if b == 0:
        return buf[start:start + 8, cols]
    return phases[b - 1, start - b:start - b + 8, cols]


def _broadcast_taps(w_ref, wb):
    for j in range(CONV_K):
        wb[j] = jnp.broadcast_to(w_ref[j:j + 1, :], wb.shape[1:])


def _conv_fwd(gates, conv_w, conv_b, ln_g, ln_b, tt=256):
    S = gates.shape[0]
    C = conv_w.shape[1]
    hb = tt // CONV_HALO

    def body(val_ref, glu_ref, hval_ref, hglu_ref, gate_ref, w_ref, b_ref, g_ref, beta_ref,
             conv_ref, y_ref, hbuf, hph):
        i = pl.program_id(0)
        halo = hval_ref[...] * _sigmoid(hglu_ref[...])
        hbuf[0:CONV_HALO, :] = jnp.where(i > 0, halo, 0.0)
        hbuf[CONV_HALO:, :] = val_ref[...] * _sigmoid(glu_ref[...])
        _shifted_copies(hbuf, hph)
        for cb in range(C // LANES):
            cols = slice(cb * LANES, (cb + 1) * LANES)
            wj = [jnp.broadcast_to(w_ref[j:j + 1, cols], (8, LANES)) for j in range(CONV_K)]
            for rc in range(tt // 8):
                acc = jnp.zeros((8, LANES), F32)
                for j in range(CONV_K):
                    start = rc * 8 + CONV_HALO - (CONV_K - 1) + j
                    acc = acc + _window(hbuf, hph, start, cols) * wj[j]
                conv_ref[rc * 8:(rc + 1) * 8, cols] = acc
        cv = conv_ref[...] + b_ref[...]
        conv_ref[...] = cv
        mu = jnp.mean(cv, axis=-1, keepdims=True)
        xc = cv - mu
        var = jnp.mean(xc * xc, axis=-1, keepdims=True)
        ln = xc * lax.rsqrt(var + LN_EPS) * g_ref[...] + beta_ref[...]
        gt = gate_ref[...]
        y_ref[...] = (ln * _sigmoid(ln) * (gt * _sigmoid(gt))).astype(BF16)

    vec = pl.BlockSpec((1, C), lambda i: (0, 0))
    return pl.pallas_call(
        body, name="conv_fwd", grid=(S // tt,),
        in_specs=[pl.BlockSpec((tt, C), lambda i: (i, 0)),
                  pl.BlockSpec((tt, C), lambda i: (i, 1)),
                  pl.BlockSpec((CONV_HALO, C), lambda i: (jnp.maximum(i * hb - 1, 0), 0)),
                  pl.BlockSpec((CONV_HALO, C), lambda i: (jnp.maximum(i * hb - 1, 0), 1)),
                  pl.BlockSpec((tt, C), lambda i: (i, 2)),
                  pl.BlockSpec((CONV_HALO, C), lambda i: (0, 0)), vec, vec, vec],
        out_specs=(pl.BlockSpec((tt, C), lambda i: (i, 0)), pl.BlockSpec((tt, C), lambda i: (i, 0))),
        out_shape=(jax.ShapeDtypeStruct((S, C), F32), jax.ShapeDtypeStruct((S, C), BF16)),
        scratch_shapes=[pltpu.VMEM((tt + CONV_HALO, C), F32), pltpu.VMEM((7, tt + CONV_HALO - 8, C), F32)],
        compiler_params=_params(("parallel",)),
    )(gates, gates, gates, gates, gates, conv_w, conv_b, ln_g, ln_b)


def _outproj_loss(x, y_att, y_conv, w_out, gf, target, tm=512):
    S, D = x.shape
    E = y_att.shape[1]

    def body(x_ref, ya_ref, yc_ref, w_ref, gf_ref, t_ref, dx_ref, dxb_ref, loss_ref, ggf_ref):
        @pl.when(pl.program_id(0) == 0)
        def _():
            loss_ref[...] = jnp.zeros_like(loss_ref)
            ggf_ref[...] = jnp.zeros_like(ggf_ref)

        x2 = (x_ref[...] + jnp.dot(_perm_rows(ya_ref[...], True), w_ref[0:E, :], preferred_element_type=F32)
              + jnp.dot(yc_ref[...], w_ref[E:, :], preferred_element_type=F32))
        r = lax.rsqrt(jnp.mean(x2 * x2, axis=-1, keepdims=True) + NORM_EPS)
        nrm = x2 * r
        gfv = gf_ref[...]
        err = nrm * gfv - t_ref[...]
        loss_ref[...] += jnp.sum(err * err, axis=0, keepdims=True)
        dout = err * (1.0 / D)
        ggf_ref[...] += jnp.sum(dout * nrm, axis=0, keepdims=True)
        dn = dout * gfv
        dx2 = r * (dn - nrm * jnp.mean(dn * nrm, axis=-1, keepdims=True))
        dx_ref[...] = dx2
        dxb_ref[...] = dx2.astype(BF16)

    row = lambda w: pl.BlockSpec((tm, w), lambda i: (i, 0))
    vec = pl.BlockSpec((1, D), lambda i: (0, 0))
    return pl.pallas_call(
        body, name="outproj_loss", grid=(S // tm,),
        in_specs=[row(D), row(E), row(E), pl.BlockSpec((2 * E, D), lambda i: (0, 0)), vec, row(D)],
        out_specs=(row(D), row(D), vec, vec),
        out_shape=(jax.ShapeDtypeStruct((S, D), F32), jax.ShapeDtypeStruct((S, D), BF16),
                   jax.ShapeDtypeStruct((1, D), F32), jax.ShapeDtypeStruct((1, D), F32)),
        compiler_params=_params(("arbitrary",)),
    )(x, y_att, y_conv, w_out, gf, target)


def _split3(v):
    hi = v.astype(BF16)
    r1 = v - hi.astype(F32)
    mid = r1.astype(BF16)
    lo = (r1 - mid.astype(F32)).astype(BF16)
    return hi, mid, lo


def _dy_att(dxb, w_out, gates, o, tm=512):
    S, D = dxb.shape
    E = ATT_W

    def body(dx_ref, w_ref, a_ref, o_ref, do_ref, da_ref, dl_ref, dxr_ref):
        dxr = _perm_rows(dx_ref[...], False)
        dxr_ref[...] = dxr
        dya = _nt(dxr, w_ref[...])
        a = a_ref[...]
        ov = o_ref[...]
        sl, dsl = _silu_and_grad(a)
        d_o = dya * sl
        do_ref[...] = d_o
        da_ref[...] = (dya * ov * dsl).astype(BF16)
        ci = lax.broadcasted_iota(jnp.int32, (E, LANES), 0) // HEAD_DIM
        hi = lax.broadcasted_iota(jnp.int32, (E, LANES), 1)
        sel = jnp.where(ci == hi, 1.0, 0.0).astype(BF16)
        acc = jnp.zeros((tm, LANES), F32)
        for part in _split3(d_o * ov):
            acc = acc + jnp.dot(part, sel, preferred_element_type=F32)
        dl_ref[...] = acc

    row = lambda w: pl.BlockSpec((tm, w), lambda i: (i, 0))
    return pl.pallas_call(
        body, name="dy_att", grid=(S // tm,),
        in_specs=[row(D), pl.BlockSpec((E, D), lambda i: (0, 0)), row(E), row(E)],
        out_specs=(row(E), row(E), row(LANES), row(D)),
        out_shape=(jax.ShapeDtypeStruct((S, E), F32), jax.ShapeDtypeStruct((S, E), BF16),
                   jax.ShapeDtypeStruct((S, LANES), F32), jax.ShapeDtypeStruct((S, D), BF16)),
        compiler_params=_params(("parallel",)),
    )(dxb, w_out, gates, o)


def _dy_conv(dxb, w_out, gates, conv_out, ln_g, ln_b, tm=512):
    S, D = dxb.shape
    C = conv_out.shape[1]

    def body(dx_ref, w_ref, gate_ref, cv_ref, g_ref, beta_ref, dgate_ref, dconv_ref, gg_ref, gb_ref, gcb_ref):
        @pl.when(pl.program_id(0) == 0)
        def _():
            gg_ref[...] = jnp.zeros_like(gg_ref)
            gb_ref[...] = jnp.zeros_like(gb_ref)
            gcb_ref[...] = jnp.zeros_like(gcb_ref)

        dyc = _nt(dx_ref[...], w_ref[...])
        cv = cv_ref[...]
        mu = jnp.mean(cv, axis=-1, keepdims=True)
        xc = cv - mu
        rstd = lax.rsqrt(jnp.mean(xc * xc, axis=-1, keepdims=True) + LN_EPS)
        nrm = xc * rstd
        gv = g_ref[...]
        ln = nrm * gv + beta_ref[...]
        u, du = _silu_and_grad(ln)
        gt = gate_ref[...]
        g2, dg2 = _silu_and_grad(gt)
        dgate_ref[...] = (dyc * u * dg2).astype(BF16)
        d_ln = dyc * g2 * du
        gb_ref[...] += jnp.sum(d_ln, axis=0, keepdims=True)
        gg_ref[...] += jnp.sum(d_ln * nrm, axis=0, keepdims=True)
        dn = d_ln * gv
        d_conv = rstd * (dn - jnp.mean(dn, axis=-1, keepdims=True)
                         - nrm * jnp.mean(dn * nrm, axis=-1, keepdims=True))
        dconv_ref[...] = d_conv
        gcb_ref[...] += jnp.sum(d_conv, axis=0, keepdims=True)

    row = lambda w: pl.BlockSpec((tm, w), lambda i: (i, 0))
    vec = pl.BlockSpec((1, C), lambda i: (0, 0))
    return pl.pallas_call(
        body, name="dy_conv", grid=(S // tm,),
        in_specs=[row(D), pl.BlockSpec((C, D), lambda i: (1, 0)),
                  pl.BlockSpec((tm, C), lambda i: (i, 2)), row(C), vec, vec],
        out_specs=(row(C), row(C), vec, vec, vec),
        out_shape=(jax.ShapeDtypeStruct((S, C), BF16), jax.ShapeDtypeStruct((S, C), F32),
                   jax.ShapeDtypeStruct((1, C), F32), jax.ShapeDtypeStruct((1, C), F32),
                   jax.ShapeDtypeStruct((1, C), F32)),
        compiler_params=_params(("arbitrary",)),
    )(dxb, w_out, gates, conv_out, ln_g, ln_b)


def _conv_bwd(d_conv, gates, d_c_gate, conv_w, hosted=None, tt=256):
    S, C = d_conv.shape
    hb = tt // CONV_HALO
    nt = S // tt
    hn = hosted.n if hosted is not None else 0

    def body(*refs):
        dc_ref, dnext_ref, val_ref, glu_ref, dg_ref, w_ref = refs[:6]
        h_ins = refs[6:6 + hn]
        out_ref, gw_ref = refs[6 + hn:8 + hn]
        h_outs = refs[8 + hn:8 + 2 * hn]
        hbuf, dbuf, dhbuf, dph, wb = refs[8 + 2 * hn:13 + 2 * hn]
        h_sems = refs[13 + 2 * hn:]
        i = pl.program_id(0)

        @pl.when(i == 0)
        def _():
            gw_ref[...] = jnp.zeros_like(gw_ref)
            _broadcast_taps(w_ref, wb)
            if hosted is not None:
                hosted.start(h_ins, h_outs, h_sems)

        val = val_ref[...]
        sg = _sigmoid(glu_ref[...])
        hbuf[...] = val * sg
        dbuf[0:tt, :] = dc_ref[...]
        dbuf[tt:, :] = jnp.where(i < nt - 1, dnext_ref[...], 0.0)
        _shifted_copies(dbuf, dph)
        for cb in range(C // LANES):
            cols = slice(cb * LANES, (cb + 1) * LANES)
            gacc = [jnp.zeros((8, LANES), F32) for _ in range(CONV_K)]
            group = 2
            for rc0 in range(0, tt // 8, group):
                hcur = [hbuf[(rc0 + r) * 8:(rc0 + r + 1) * 8, cols] for r in range(group)]
                accs = [jnp.zeros((8, LANES), F32) for _ in range(group)]
                for j in range(CONV_K):
                    wj = wb[j, :, cols]
                    for r in range(group):
                        dwin = _window(dbuf, dph, (rc0 + r) * 8 + (CONV_K - 1) - j, cols)
                        accs[r] = accs[r] + dwin * wj
                        gacc[j] = gacc[j] + dwin * hcur[r]
                for r in range(group):
                    dhbuf[(rc0 + r) * 8:(rc0 + r + 1) * 8, cols] = accs[r]
            for j in range(CONV_K):
                gw_ref[j:j + 1, cols] += jnp.sum(gacc[j], axis=0, keepdims=True)
        d_h = dhbuf[...]
        out_ref[:, 0:C] = (d_h * sg).astype(BF16)
        out_ref[:, C:2 * C] = (d_h * val * sg * (1.0 - sg)).astype(BF16)
        out_ref[:, 2 * C:3 * C] = dg_ref[...]

        if hosted is not None:
            @pl.when(i == nt - 1)
            def _():
                hosted.finish(h_ins, h_outs, h_sems)

    tile = lambda col: pl.BlockSpec((tt, C), lambda i: (i, col))
    in_specs = [tile(0),
                pl.BlockSpec((CONV_HALO, C), lambda i: (jnp.minimum((i + 1) * hb, S // CONV_HALO - 1), 0)),
                tile(0), tile(1), tile(0),
                pl.BlockSpec((CONV_HALO, C), lambda i: (0, 0))]
    args = [d_conv, d_conv, gates, gates, d_c_gate, conv_w]
    out_specs = [pl.BlockSpec((tt, 3 * C), lambda i: (i, 0)), pl.BlockSpec((CONV_HALO, C), lambda i: (0, 0))]
    out_shape = [jax.ShapeDtypeStruct((S, 3 * C), BF16), jax.ShapeDtypeStruct((CONV_HALO, C), F32)]
    scratch = [pltpu.VMEM((tt, C), F32), pltpu.VMEM((tt + CONV_HALO, C), F32), pltpu.VMEM((tt, C), F32),
               pltpu.VMEM((7, tt + CONV_HALO - 8, C), F32), pltpu.VMEM((CONV_K, 8, C), F32)]
    if hosted is not None:
        in_specs += [ANY_SPEC] * hn
        args += hosted.arrays
        out_specs += [ANY_SPEC] * hn
        out_shape += hosted.out_shapes()
        scratch += hosted.sem_shapes()
    res = pl.pallas_call(
        body, name="conv_bwd", grid=(nt,),
        in_specs=in_specs, out_specs=tuple(out_specs), out_shape=tuple(out_shape), scratch_shapes=scratch,
        compiler_params=_params(("arbitrary",)),
    )(*args)
    return res[0], res[1], list(res[2:])


def _attn_bwd(q, kv, d_o, lse, delta, dil, prev, final, name, hosted=None):
    S = q.shape[0]
    rows = _Rows(dil, S)
    nb = rows.nb
    out_dt = BF16 if final else F32
    have_prev = prev is not None
    hn = hosted.n if hosted is not None else 0

    def body(*refs):
        refs = list(refs)
        q_ref, do_ref, lse_ref, dl_ref, kvc_ref, kvp_ref = refs[:6]
        del refs[:6]
        if have_prev:
            pdq_ref, pdkv_ref = refs[:2]
            del refs[:2]
        h_ins = refs[:hn]
        dq_ref, dkv_ref = refs[hn:hn + 2]
        h_outs = refs[hn + 2:2 * hn + 2]
        carry, tbl = refs[2 * hn + 2:2 * hn + 4]
        h_sems = refs[2 * hn + 4:]
        n = pl.program_id(1)

        @pl.when((pl.program_id(0) == 0) & (n == 0))
        def _():
            if hosted is not None:
                hosted.start(h_ins, h_outs, h_sems)
            _fill_bias_table(tbl, rows, keys_first=True)

        @pl.when(n == 0)
        def _():
            carry[...] = jnp.zeros_like(carry)

        @pl.when(n < nb)
        def _():
            kv2 = jnp.concatenate([_ld(kvp_ref), _ld(kvc_ref)], axis=0)
            lse_t, dl_t = _ld(lse_ref).T, _ld(dl_ref).T
            lo_mask = lax.broadcasted_iota(jnp.int32, (2 * BLK, LANES), 1) < HEAD_DIM
            halves = [jnp.zeros((2 * BLK, LANES), F32) for _ in range(4)]
            for hk in range(N_KV_HEADS):
                k_lo, k_hi, v_lo, v_hi = _head_operands(kv2, hk, lo_mask)
                cols = [slice(b * LANES, (b + 1) * LANES) for b in (2 * hk, 2 * hk + 1)]
                q2 = jnp.concatenate([_ld(q_ref, cols[0]), _ld(q_ref, cols[1])], axis=0).astype(BF16)
                do2 = jnp.concatenate([_ld(do_ref, cols[0]), _ld(do_ref, cols[1])], axis=0).astype(BF16)
                dq2 = jnp.zeros((2 * BLK, LANES), F32)
                dks, dvs = [], []
                for which, (kk, vv) in enumerate(((k_lo, v_lo), (k_hi, v_hi))):
                    h0, h1 = 4 * hk + which, 4 * hk + 2 + which
                    s = _nt(kk, q2) + _bias2(tbl, n, h0, h1, axis=1)
                    lse2 = jnp.concatenate([lse_t[h0:h0 + 1, :], lse_t[h1:h1 + 1, :]], axis=1)
                    dl2 = jnp.concatenate([dl_t[h0:h0 + 1, :], dl_t[h1:h1 + 1, :]], axis=1)
                    p = jnp.exp(s - lse2)
                    ds = (p * (_nt(vv, do2) - dl2)).astype(BF16)
                    dq2 = dq2 + _tn(ds, kk)
                    dks.append(jnp.dot(ds, q2, preferred_element_type=F32))
                    dvs.append(jnp.dot(p.astype(BF16), do2, preferred_element_type=F32))
                dk_sum = jnp.where(lo_mask, dks[0], dks[1])
                dv_sum = jnp.where(lo_mask, dvs[0], dvs[1])
                for jp in range(2):
                    dq_blk = dq2[jp * BLK:(jp + 1) * BLK]
                    if have_prev:
                        dq_blk = dq_blk + _ld(pdq_ref, cols[jp])
                    if final:
                        dq_blk = dq_blk * (HEAD_DIM ** -0.5)
                    _st(dq_ref, dq_blk.astype(out_dt), cols[jp])
                half, pos = hk // 2, hk % 2
                here = lo_mask if pos == 0 else jnp.logical_not(lo_mask)
                dk_tot = dk_sum + pltpu.roll(dk_sum, HEAD_DIM, axis=1)
                dv_tot = dv_sum + pltpu.roll(dv_sum, HEAD_DIM, axis=1)
                halves[half] = halves[half] + jnp.where(here, dk_tot, 0.0)
                halves[2 + half] = halves[2 + half] + jnp.where(here, dv_tot, 0.0)
            for b in range(4):
                cols = slice(b * LANES, (b + 1) * LANES)
                done = carry[:, cols] + halves[b][0:BLK, :]
                if have_prev:
                    done = done + _ld(pdkv_ref, cols)
                _st(dkv_ref, done.astype(out_dt), cols)
                carry[:, cols] = halves[b][BLK:, :]

        @pl.when(n == nb)
        def _():
            done = carry[...]
            if have_prev:
                done = done + _ld(pdkv_ref)
            _st(dkv_ref, done.astype(out_dt))

        if hosted is not None:
            @pl.when((pl.program_id(0) == dil - 1) & (n == nb))
            def _():
                hosted.finish(h_ins, h_outs, h_sems)

    cur = lambda n: jnp.minimum(n, nb - 1)
    behind = lambda n: jnp.maximum(n - 1, 0)
    in_specs = [rows.spec(ATT_W, cur), rows.spec(ATT_W, cur), rows.spec(LANES, cur), rows.spec(LANES, cur),
                rows.spec(2 * KV_W, cur), rows.spec(2 * KV_W, behind)]
    args = [rows.of(q), rows.of(d_o), rows.of(lse), rows.of(delta), rows.of(kv), rows.of(kv)]
    if have_prev:
        in_specs += [rows.spec(ATT_W, cur), rows.spec(2 * KV_W, behind)]
        args += [rows.of(prev[0]), rows.of(prev[1])]
    out_specs = [rows.spec(ATT_W, cur), rows.spec(2 * KV_W, behind)]
    out_shape = [jax.ShapeDtypeStruct(rows.view + (ATT_W,), out_dt),
                 jax.ShapeDtypeStruct(rows.view + (2 * KV_W,), out_dt)]
    scratch = [pltpu.VMEM((BLK, 2 * KV_W), F32), pltpu.VMEM((2 * N_Q_HEADS, 2 * BLK, BLK), F32)]
    if hosted is not None:
        in_specs += [ANY_SPEC] * hn
        args += hosted.arrays
        out_specs += [ANY_SPEC] * hn
        out_shape += hosted.out_shapes()
        scratch += hosted.sem_shapes()
    res = pl.pallas_call(
        body, name=name, grid=(dil, nb + 1),
        in_specs=in_specs, out_specs=tuple(out_specs), out_shape=tuple(out_shape), scratch_shapes=scratch,
        compiler_params=_params(("arbitrary", "arbitrary")),
    )(*args)
    return (res[0].reshape(S, ATT_W), res[1].reshape(S, 2 * KV_W)), list(res[2:])


def _dh(segments, w_in, x, dx2, g, hosted=None, tm=1024, tk=512):
    S, D = x.shape
    ns = len(segments)
    counts = [a.shape[1] // tk for a, _ in segments]
    starts = [sum(counts[:s]) for s in range(ns)]
    nk = sum(counts)
    hn = hosted.n if hosted is not None else 0

    def body(*refs):
        seg_refs = refs[:ns]
        w_ref, x_ref, dx2_ref, g_ref = refs[ns:ns + 4]
        h_ins = refs[ns + 4:ns + 4 + hn]
        gx_ref, gng_ref = refs[ns + 4 + hn:ns + 6 + hn]
        h_outs = refs[ns + 6 + hn:ns + 6 + 2 * hn]
        acc = refs[ns + 6 + 2 * hn]
        h_sems = refs[ns + 7 + 2 * hn:]
        i, k = pl.program_id(0), pl.program_id(1)

        @pl.when((i == 0) & (k == 0))
        def _():
            gng_ref[...] = jnp.zeros_like(gng_ref)
            if hosted is not None:
                hosted.start(h_ins, h_outs, h_sems)

        @pl.when(k == 0)
        def _():
            acc[...] = jnp.zeros_like(acc)

        for s in range(ns):
            @pl.when((k >= starts[s]) & (k < starts[s] + counts[s]))
            def _(s=s):
                t = seg_refs[s][...]
                if segments[s][1]:
                    t = _perm_rows(t, True)
                acc[...] += jnp.dot(t, w_ref[...], preferred_element_type=F32)

        @pl.when(k == nk - 1)
        def _():
            dh = acc[...]
            xf = x_ref[...]
            r = lax.rsqrt(jnp.mean(xf * xf, axis=-1, keepdims=True) + NORM_EPS)
            nrm = xf * r
            gng_ref[...] += jnp.sum(dh * nrm, axis=0, keepdims=True)
            dn = dh * g_ref[...]
            gx_ref[...] = dx2_ref[...] + r * (dn - nrm * jnp.mean(dn * nrm, axis=-1, keepdims=True))

        if hosted is not None:
            @pl.when((i == S // tm - 1) & (k == nk - 1))
            def _():
                hosted.finish(h_ins, h_outs, h_sems)

    row = pl.BlockSpec((tm, D), lambda i, k: (i, 0))
    vec = pl.BlockSpec((1, D), lambda i, k: (0, 0))
    in_specs = [pl.BlockSpec((tm, tk), lambda i, k, s=s: (i, jnp.clip(k - starts[s], 0, counts[s] - 1)))
                for s in range(ns)]
    in_specs += [pl.BlockSpec((tk, D), lambda i, k: (k, 0)), row, row, vec]
    args = [a for a, _ in segments] + [w_in, x, dx2, g]
    out_specs = [row, vec]
    out_shape = [jax.ShapeDtypeStruct((S, D), F32), jax.ShapeDtypeStruct((1, D), F32)]
    scratch = [pltpu.VMEM((tm, D), F32)]
    if hosted is not None:
        in_specs += [ANY_SPEC] * hn
        args += hosted.arrays
        out_specs += [ANY_SPEC] * hn
        out_shape += hosted.out_shapes()
        scratch += hosted.sem_shapes()
    res = pl.pallas_call(
        body, name="dh", grid=(S // tm, nk),
        in_specs=in_specs, out_specs=tuple(out_specs), out_shape=tuple(out_shape), scratch_shapes=scratch,
        compiler_params=_params(("arbitrary", "arbitrary")),
    )(*args)
    return res[0], res[1], list(res[2:])


def _tn_matmul(a, bs, name, b_first=False, tm=512):
    M, K = a.shape
    nb = len(bs)
    shapes = [(b.shape[1], K) if b_first else (K, b.shape[1]) for b in bs]

    def body(a_ref, *refs):
        @pl.when(pl.program_id(0) == 0)
        def _():
            for o_ref in refs[nb:]:
                o_ref[...] = jnp.zeros_like(o_ref)

        at = a_ref[...]
        for b_ref, o_ref in zip(refs[:nb], refs[nb:]):
            for c in range(0, b_ref.shape[1], 512):
                if b_first:
                    o_ref[c:c + 512, :] += _tn(b_ref[:, c:c + 512], at)
                else:
                    o_ref[:, c:c + 512] += _tn(at, b_ref[:, c:c + 512])

    return pl.pallas_call(
        body, name=name, grid=(M // tm,),
        in_specs=[pl.BlockSpec((tm, K), lambda m: (m, 0))] + [pl.BlockSpec((tm, b.shape[1]), lambda m: (m, 0))
                                                              for b in bs],
        out_specs=tuple(pl.BlockSpec(s, lambda m: (0, 0)) for s in shapes),
        out_shape=tuple(jax.ShapeDtypeStruct(s, F32) for s in shapes),
        compiler_params=_params(("arbitrary",)),
    )(a, *bs)


def _adamw(parts, w, m, v, name, tr=None, split=None, by_chip=False):
    R, C = w.shape
    tr = R if tr is None else tr
    parts = [parts] if split is None else list(parts)
    npar = len(parts)

    def total(p_ref):
        if by_chip:
            c = lax.axis_index("c")
            g = p_ref[c].astype(F32)
            for chip in range(1, N_DEV // 2):
                g = g + p_ref[2 * chip + c].astype(F32)
            return g
        g = p_ref[0].astype(F32)
        for dev in range(1, N_DEV):
            g = g + p_ref[dev].astype(F32)
        return g

    def body(*refs):
        w_ref, m_ref, v_ref, g_out, d_out, m_out, v_out = refs[npar:]
        if split is None:
            g = total(refs[0])
        else:
            g = jnp.where(_mesh_pos()[3] < split, total(refs[0]), total(refs[1]))
        mn = ADAM_B1 * m_ref[...] + (1.0 - ADAM_B1) * g
        vn = ADAM_B2 * v_ref[...] + (1.0 - ADAM_B2) * (g * g)
        m_hat = mn / (1.0 - ADAM_B1 ** ADAM_STEP)
        v_hat = vn / (1.0 - ADAM_B2 ** ADAM_STEP)
        g_out[...] = g
        d_out[...] = -ADAM_LR * (m_hat / (jnp.sqrt(v_hat) + ADAM_EPS) + ADAM_WD * w_ref[...])
        m_out[...] = mn
        v_out[...] = vn

    blk = pl.BlockSpec((tr, C), lambda i: (i, 0))
    shp = jax.ShapeDtypeStruct((R, C), F32)
    return pl.pallas_call(
        body, name=name, grid=(R // tr,),
        in_specs=[pl.BlockSpec((N_DEV, tr, C), lambda i: (0, i, 0))] * npar + [blk, blk, blk],
        out_specs=(blk, blk, blk, blk), out_shape=(shp, shp, shp, shp),
        compiler_params=_params(("parallel",)),
    )(*parts, w, m, v)


def _local_step(x, target, norm_g, w_in, conv_w, conv_b, ln_g, ln_b, w_out, gf, exchanges=None):
    ex_out, ex_att, ex_conv = exchanges if exchanges is not None else (None, None, None)
    conv_cols = w_in.shape[0] - 2 * ATT_W - 2 * KV_W
    q, kv, a_gate, gates, h_rm, h = _inproj(
        x, norm_g, w_in, [(ATT_W, HEAD_DIM ** -0.5, True), (2 * KV_W, 1.0, True), (ATT_W, 1.0, True),
                          (conv_cols, 1.0, False)])

    merged = None
    for idx, (_, dil) in enumerate(reversed(PATTERNS)):
        merged = _attn_fwd(q, kv, dil, "attn_fwd_d%d" % dil, merged,
                           a_gate if idx == len(PATTERNS) - 1 else None)
    o, lse, y_att = merged
    conv_out, y_conv = _conv_fwd(gates, conv_w, conv_b, ln_g, ln_b)
    dx2, dxb, loss_cols, g_gf = _outproj_loss(x, y_att, y_conv, w_out, gf, target)

    d_o, d_a_gate, delta, dxb_rm = _dy_att(dxb, w_out, a_gate, o)
    g_w_out = jnp.concatenate([_tn_matmul(y_att, [dxb_rm], "gw_out_att")[0],
                               _tn_matmul(y_conv, [dxb], "gw_out_conv")[0]], axis=0)
    acc, out_parts = None, []
    for idx, (_, dil) in enumerate(reversed(PATTERNS)):
        hosted = ex_out(g_w_out) if (idx == 0 and ex_out is not None) else None
        acc, outs = _attn_bwd(q, kv, d_o, lse, delta, dil, acc, idx == len(PATTERNS) - 1, "attn_bwd_d%d" % dil,
                              hosted)
        out_parts += outs
    dq, dkv = acc
    g_q, g_kv, g_a = _tn_matmul(h_rm, [dq, dkv, d_a_gate], "gw_in_att", b_first=True)

    d_c_gate, d_conv, g_ln_g, g_ln_b, g_conv_b = _dy_conv(dxb, w_out, gates, conv_out, ln_g, ln_b)
    dgates, g_conv_w, att_parts = _conv_bwd(d_conv, gates, d_c_gate, conv_w,
                                            ex_att(g_q, g_kv, g_a) if ex_att is not None else None)
    g_c, = _tn_matmul(h, [dgates], "gw_in_conv", b_first=True)
    grad_x, g_norm_g, conv_parts = _dh(
        [(dq, True), (dkv, True), (d_a_gate, True), (dgates, False)], w_in, x, dx2, norm_g,
        ex_conv(g_a, g_c, g_conv_w) if ex_conv is not None else None)
    small = (g_norm_g, g_conv_b, g_ln_g, g_ln_b, g_gf, loss_cols)
    return grad_x, (g_q, g_kv, g_a, g_c), g_w_out, g_conv_w, small, (out_parts, att_parts, conv_parts)


def kernel(x, norm_g, w_in, conv_w, conv_b, conv_ln_g, conv_ln_b, w_out, final_norm_g, loss_target, m_norm_g, m_w_in, m_conv_w, m_conv_b, m_conv_ln_g, m_conv_ln_b, m_w_out, m_final_norm_g, v_norm_g, v_w_in, v_conv_w, v_conv_b, v_conv_ln_g, v_conv_ln_b, v_w_out, v_final_norm_g):
    S, D = x.shape[1], x.shape[2]
    win_sh, wout_sh, cw_sh = w_in[0].T, w_out[0], conv_w[0]
    cols_sh, rows_sh, ch_sh = win_sh.shape[0], wout_sh.shape[0], cw_sh.shape[1]

    win_all, wout_all, cw_all = _gather_two_level(
        [win_sh.astype(BF16), wout_sh.astype(BF16), cw_sh], "gather_weights")
    w_in_full = win_all.reshape(N_DEV * cols_sh, D)
    w_out_full = wout_all.reshape(N_DEV * rows_sh, D)
    conv_w_full = cw_all.transpose(1, 0, 2).reshape(CONV_K, N_DEV * ch_sh)
    conv_w_full = jnp.pad(conv_w_full, ((0, CONV_HALO - CONV_K), (0, 0)))
    gf = final_norm_g.reshape(1, D)

    first = -(-(ATT_W + 2 * KV_W) // cols_sh)
    a_off = first * cols_sh - (ATT_W + 2 * KV_W)
    assert 0 <= a_off <= ATT_W

    def pieces(parts, n):
        return jnp.concatenate([p.astype(BF16) for p in parts], axis=0).reshape(n, cols_sh, D)

    def ex_out(g_w_out):
        return _Exchange([g_w_out.reshape(N_DEV, rows_sh, D).astype(BF16)], [(0, N_DEV)])

    same_core = (2, 4, 6)

    def ex_att(g_q, g_kv, g_a):
        mine = _chip_sum(pieces([g_q, g_kv, g_a[:a_off]], first), 0, "rs_att")
        return _Exchange([mine], [(0, first)], [same_core])

    def ex_conv(g_a, g_c, g_conv_w):
        mine = _chip_sum(pieces([g_a[a_off:], g_c], N_DEV - first), first, "rs_conv")
        return _Exchange(
            [mine, g_conv_w[:CONV_K].reshape(CONV_K, N_DEV, ch_sh).transpose(1, 0, 2)],
            [(first, N_DEV), (0, N_DEV)], [same_core, None])

    grad_x, _, _, _, small, parts = _local_step(
        x[0], loss_target[0], norm_g, w_in_full, conv_w_full, conv_b, conv_ln_g, conv_ln_b, w_out_full, gf,
        (ex_out, ex_att, ex_conv))
    (wout_parts,), (win_parts_lo,), (win_parts_hi, cw_parts) = parts

    small_pack = jnp.concatenate(list(small) + [jnp.zeros((2, D), F32)], axis=0)
    small_parts, = _exchange([small_pack], [None], "gather_small")

    upd_win = _adamw((win_parts_lo, win_parts_hi), win_sh, m_w_in[0].T, v_w_in[0].T, "adamw_w_in",
                     tr=cols_sh // 2, split=first, by_chip=True)
    upd_wout = _adamw(wout_parts, wout_sh, m_w_out[0], v_w_out[0], "adamw_w_out", tr=128)
    upd_cw = _adamw(cw_parts, cw_sh, m_conv_w[0], v_conv_w[0], "adamw_conv_w")
    zeros3 = jnp.zeros((3, D), F32)
    stack = lambda a, b, c, d_, e: jnp.concatenate([a, b, c, d_, e.reshape(1, D), zeros3], axis=0)
    upd_small = _adamw(
        small_parts,
        stack(norm_g, conv_b, conv_ln_g, conv_ln_b, final_norm_g),
        stack(m_norm_g, m_conv_b, m_conv_ln_g, m_conv_ln_b, m_final_norm_g),
        stack(v_norm_g, v_conv_b, v_conv_ln_g, v_conv_ln_b, v_final_norm_g) + jnp.concatenate(
            [jnp.zeros((5, D), F32), jnp.ones((3, D), F32)], axis=0),
        "adamw_small")

    loss = 0.5 / D * jnp.sum(upd_small[0][5])

    def outputs(kind):
        sm = upd_small[kind]
        return [sm[0:1], upd_win[kind].T[None], upd_cw[kind][None], sm[1:2], sm[2:3], sm[3:4],
                upd_wout[kind][None], sm[4]]

    return (loss, grad_x[None], *outputs(0), *outputs(1), *outputs(2), *outputs(3))
```

```python
import jax
import jax.numpy as jnp
from jax import lax
from jax.experimental import pallas as pl
from jax.experimental.pallas import tpu as pltpu

F32 = jnp.float32
BF16 = jnp.bfloat16

HEAD_DIM = 64
N_KV_HEADS = 4
N_Q_HEADS = 16
ATT_W = 1024
KV_W = 256
CONV_K = 31
CONV_HALO = 32
PATTERNS = ((128, 1), (512, 4), (2048, 16))
BLK = 128
LANES = 128
NORM_EPS = 1e-6
LN_EPS = 1e-5
NEG = -1e30
N_DEV = 8
ADAM_LR, ADAM_B1, ADAM_B2, ADAM_EPS, ADAM_WD, ADAM_STEP = 0.001, 0.9, 0.999, 1e-08, 0.01, 10
VMEM_LIMIT = 48 * 1024 * 1024
BIG_VMEM_LIMIT = 58 * 1024 * 1024
SLOPES = tuple(2.0 ** (-8.0 * (h + 1) / N_Q_HEADS) for h in range(N_Q_HEADS))
MESH = pl.DeviceIdType.MESH


def _params(sem, vmem_limit=VMEM_LIMIT):
    return pltpu.CompilerParams(dimension_semantics=sem, vmem_limit_bytes=vmem_limit)


def _sigmoid(v):
    return 1.0 / (1.0 + jnp.exp(-v))


def _silu_and_grad(v):
    s = _sigmoid(v)
    return v * s, s * (1.0 + v * (1.0 - s))


ANY_SPEC = pl.BlockSpec(memory_space=pl.ANY)


def _mesh_pos():
    x, y, c = lax.axis_index("x"), lax.axis_index("y"), lax.axis_index("c")
    return x, y, c, 4 * x + 2 * y + c


def _flipped(k, x, y, c):
    px = 1 - x if k & 4 else x
    py = 1 - y if k & 2 else y
    pc = 1 - c if k & 1 else c
    return (px, py, pc), 4 * px + 2 * py + pc


class _Exchange:
    def __init__(self, arrays, dests, flips=None):
        self.arrays, self.dests, self.n = list(arrays), list(dests), len(arrays)
        self.flips = [tuple(range(1, N_DEV)) if f is None else tuple(f)
                      for f in (flips if flips is not None else [None] * self.n)]

    def out_shapes(self):
        return [jax.ShapeDtypeStruct((N_DEV,) + a.shape[-2:], a.dtype) for a in self.arrays]

    def sem_shapes(self):
        return [pltpu.SemaphoreType.DMA((self.n, N_DEV - 1)), pltpu.SemaphoreType.DMA((self.n, N_DEV - 1)),
                pltpu.SemaphoreType.DMA((self.n,))]

    def _when(self, a, dev, fn):
        if self.dests[a] is None:
            fn()
        else:
            lo, hi = self.dests[a]
            pl.when((dev >= lo) & (dev < hi))(fn)

    def _mine(self, ins, a, dev):
        return ins[a] if self.dests[a] is None else ins[a].at[dev - self.dests[a][0]]

    def _copy(self, ins, outs, sems, a, k, src_dev, slot, target):
        return pltpu.make_async_remote_copy(
            src_ref=self._mine(ins, a, src_dev), dst_ref=outs[a].at[slot],
            send_sem=sems[0].at[a, k - 1], recv_sem=sems[1].at[a, k - 1],
            device_id=target, device_id_type=MESH)

    def start(self, ins, outs, sems):
        x, y, c, me = _mesh_pos()
        for a in range(self.n):
            self._when(a, me, lambda a=a: pltpu.make_async_copy(
                self._mine(ins, a, me), outs[a].at[me], sems[2].at[a]).start())
            for k in self.flips[a]:
                target, peer = _flipped(k, x, y, c)
                self._when(a, peer, lambda a=a, k=k, target=target, peer=peer: self._copy(
                    ins, outs, sems, a, k, peer, me, target).start())

    def finish(self, ins, outs, sems):
        x, y, c, me = _mesh_pos()
        lo0 = [0 if d is None else d[0] for d in self.dests]
        for a in range(self.n):
            for k in self.flips[a]:
                target, peer = _flipped(k, x, y, c)
                self._when(a, me, lambda a=a, k=k, peer=peer: self._copy(
                    ins, outs, sems, a, k, lo0[a], peer, (x, y, c)).wait_recv())
            for k in self.flips[a]:
                target, peer = _flipped(k, x, y, c)
                self._when(a, peer, lambda a=a, k=k, target=target, peer=peer: self._copy(
                    ins, outs, sems, a, k, peer, me, target).wait_send())
            self._when(a, me, lambda a=a: pltpu.make_async_copy(
                self._mine(ins, a, me), outs[a].at[me], sems[2].at[a]).wait())


def _exchange(arrays, dests, name, flips=None):
    ex = _Exchange(arrays, dests, flips)
    na = ex.n

    def body(*refs):
        ins, outs, sems = refs[:na], refs[na:2 * na], refs[2 * na:]
        ex.start(ins, outs, sems)
        ex.finish(ins, outs, sems)

    return pl.pallas_call(
        body, name=name, out_shape=tuple(ex.out_shapes()),
        in_specs=[ANY_SPEC] * na, out_specs=tuple([ANY_SPEC] * na), scratch_shapes=ex.sem_shapes(),
    )(*arrays)


def _chip_sum(pieces, lo, name):
    n, R, C = pieces.shape

    def swap(p_ref, t_ref, send_sems, recv_sems):
        x, y, c, me = _mesh_pos()
        for i in range(n):
            mine = (lo + i) % 2
            cp = pltpu.make_async_remote_copy(
                src_ref=p_ref.at[i], dst_ref=t_ref.at[i], send_sem=send_sems.at[i], recv_sem=recv_sems.at[i],
                device_id=(x, y, 1 - c), device_id_type=MESH)
            pl.when(c != mine)(cp.start)
        for i in range(n):
            mine = (lo + i) % 2
            cp = pltpu.make_async_remote_copy(
                src_ref=p_ref.at[i], dst_ref=t_ref.at[i], send_sem=send_sems.at[i], recv_sem=recv_sems.at[i],
                device_id=(x, y, 1 - c), device_id_type=MESH)
            pl.when(c == mine)(cp.wait_recv)
            pl.when(c != mine)(cp.wait_send)

    other = pl.pallas_call(
        swap, name=name + "_swap", out_shape=jax.ShapeDtypeStruct(pieces.shape, pieces.dtype),
        in_specs=[ANY_SPEC], out_specs=ANY_SPEC,
        scratch_shapes=[pltpu.SemaphoreType.DMA((n,)), pltpu.SemaphoreType.DMA((n,))],
    )(pieces)

    def add(p_ref, t_ref, o_ref):
        o_ref[...] = (p_ref[...].astype(F32) + t_ref[...].astype(F32)).astype(o_ref.dtype)

    tr = R // 2
    blk = pl.BlockSpec((None, tr, C), lambda i, r: (i, r, 0))
    return pl.pallas_call(
        add, name=name + "_add", grid=(n, R // tr), in_specs=[blk, blk], out_specs=blk,
        out_shape=jax.ShapeDtypeStruct(pieces.shape, pieces.dtype),
        compiler_params=_params(("parallel", "parallel")),
    )(pieces, other)


def _gather_two_level(arrays, name):
    na = len(arrays)

    def body(*refs):
        ins, outs = refs[:na], refs[na:2 * na]
        send_sems, recv_sems, loc_sems = refs[2 * na:]
        x, y, c, me = _mesh_pos()
        sibling = (x, y, 1 - c)
        chips = [(1 - x, y), (x, 1 - y), (1 - x, 1 - y)]

        def slot(px, py, pc):
            return 4 * px + 2 * py + pc

        def copy(a, k, src, block, to):
            return pltpu.make_async_remote_copy(
                src_ref=src, dst_ref=outs[a].at[slot(*block)], send_sem=send_sems.at[a, k], recv_sem=recv_sems.at[a, k],
                device_id=to, device_id_type=MESH)

        local = [pltpu.make_async_copy(ins[a], outs[a].at[me], loc_sems.at[a]) for a in range(na)]
        for cp in local:
            cp.start()
        started = []
        for a in range(na):
            started.append(copy(a, 0, ins[a], (x, y, c), sibling))
            started += [copy(a, 1 + j, ins[a], (x, y, c), (*chip, c)) for j, chip in enumerate(chips)]
        for cp in started:
            cp.start()
        for j, chip in enumerate(chips):
            for a in range(na):
                copy(a, 1 + j, ins[a], (*chip, c), (x, y, c)).wait_recv()
                fwd = copy(a, 4 + j, outs[a].at[slot(*chip, c)], (*chip, c), sibling)
                fwd.start()
                started.append(fwd)
        for a in range(na):
            copy(a, 0, ins[a], sibling, (x, y, c)).wait_recv()
            for j, chip in enumerate(chips):
                copy(a, 4 + j, ins[a], (*chip, 1 - c), (x, y, c)).wait_recv()
        for cp in started:
            cp.wait_send()
        for cp in local:
            cp.wait()

    return pl.pallas_call(
        body, name=name,
        out_shape=tuple(jax.ShapeDtypeStruct((N_DEV,) + a.shape, a.dtype) for a in arrays),
        in_specs=[ANY_SPEC] * na, out_specs=tuple([ANY_SPEC] * na),
        scratch_shapes=[pltpu.SemaphoreType.DMA((na, N_DEV - 1)), pltpu.SemaphoreType.DMA((na, N_DEV - 1)),
                        pltpu.SemaphoreType.DMA((na,))],
    )(*arrays)


CHUNK = 128
RESIDUES = 16
PER_RES = CHUNK // RESIDUES


def _perm_rows(tile, inverse):
    a = lax.broadcasted_iota(jnp.int32, (CHUNK, CHUNK), 0)
    b = lax.broadcasted_iota(jnp.int32, (CHUNK, CHUNK), 1)
    if inverse:
        a, b = b, a
    p = jnp.where(a == PER_RES * (b % RESIDUES) + b // RESIDUES, 1.0, 0.0).astype(BF16)
    parts = [jnp.dot(p, tile[c * CHUNK:(c + 1) * CHUNK], preferred_element_type=F32)
             for c in range(tile.shape[0] // CHUNK)]
    return jnp.concatenate(parts, axis=0).astype(BF16)


class _Rows:
    def __init__(self, dil, S):
        nc = S // CHUNK
        self.dil = dil
        if dil == 1:
            self.view, self.block, self.nb = (nc, CHUNK), (None, CHUNK), nc
            self.index = lambda r, b: (b, 0, 0)
        elif dil == 4:
            self.view, self.block, self.nb = (nc, 4, 4, PER_RES), (4, 4, None, PER_RES), nc // 4
            self.index = lambda r, b: (b, 0, r, 0, 0)
        elif dil == RESIDUES:
            self.view, self.block, self.nb = (nc, RESIDUES, PER_RES), (RESIDUES, None, PER_RES), nc // RESIDUES
            self.index = lambda r, b: (b, r, 0, 0)
        else:
            raise NotImplementedError(dil)

    def of(self, a):
        return a.reshape(self.view + (a.shape[-1],))

    def spec(self, width, which_block):
        return pl.BlockSpec(self.block + (width,), lambda r, n: self.index(r, which_block(n)))

    def pos(self, row):
        if self.dil == 1:
            return (row % PER_RES) * RESIDUES + row // PER_RES
        if self.dil == 4:
            return (row // 32) * 32 + (row % PER_RES) * 4 + (row % 32) // PER_RES
        return row


def _ld(ref, cols=slice(None)):
    v = ref[(slice(None),) * (len(ref.shape) - 1) + (cols,)]
    return v.reshape(BLK, v.shape[-1])


def _st(ref, val, cols=slice(None)):
    ref[(slice(None),) * (len(ref.shape) - 1) + (cols,)] = val.reshape(ref.shape[:-1] + (val.shape[-1],))


def _inproj(x, g, w_t, segments, tm=1024, tn=512):
    S, D = x.shape
    ns = len(segments)
    ni = S // tm
    counts = [nc // tn for nc, _, _ in segments]
    starts = [sum(counts[:s]) for s in range(ns)]

    def body(x_ref, g_ref, w_ref, *rest):
        outs = rest[:ns]
        hrm_out, h_out, hrm_scr, h_scr = rest[ns:]
        p, i = pl.program_id(0), pl.program_id(1)

        @pl.when(p == 0)
        def _():
            xf = x_ref[...]
            r = lax.rsqrt(jnp.mean(xf * xf, axis=-1, keepdims=True) + NORM_EPS)
            h = (xf * r * g_ref[...]).astype(BF16)
            hrm = _perm_rows(h, False)
            h_scr[i] = h
            hrm_scr[i] = hrm
            h_out[...] = h
            hrm_out[...] = hrm

        for s, (_, scale, rm) in enumerate(segments):
            @pl.when((p > starts[s]) & (p <= starts[s] + counts[s]))
            def _(s=s, scale=scale, rm=rm):
                acc = _nt((hrm_scr if rm else h_scr)[i], w_ref[...])
                outs[s][...] = acc * scale if scale != 1.0 else acc

    def out_index(s):
        def index(p, i):
            j = p - 1 - starts[s]
            row = jnp.where(j < 0, 0, jnp.where(j >= counts[s], ni - 1, i))
            return row, jnp.clip(j, 0, counts[s] - 1)
        return index

    first_pass = pl.BlockSpec((tm, D), lambda p, i: (jnp.where(p == 0, i, ni - 1), 0))
    out_specs = [pl.BlockSpec((tm, tn), out_index(s)) for s in range(ns)]
    out_shape = [jax.ShapeDtypeStruct((S, nc), F32) for nc, _, _ in segments]
    return pl.pallas_call(
        body, name="inproj", grid=(1 + sum(counts), ni),
        in_specs=[first_pass, pl.BlockSpec((1, D), lambda p, i: (0, 0)),
                  pl.BlockSpec((tn, D), lambda p, i: (jnp.maximum(p - 1, 0), 0))],
        out_specs=tuple(out_specs + [first_pass, first_pass]),
        out_shape=tuple(out_shape + [jax.ShapeDtypeStruct((S, D), BF16)] * 2),
        scratch_shapes=[pltpu.VMEM((ni, tm, D), BF16), pltpu.VMEM((ni, tm, D), BF16)],
        compiler_params=_params(("arbitrary", "arbitrary"), BIG_VMEM_LIMIT),
    )(x, g, w_t)


def _fill_bias_table(tbl, rows, keys_first=False):
    shape = (2 * BLK, BLK) if keys_first else (BLK, 2 * BLK)
    qi = lax.broadcasted_iota(jnp.int32, shape, 1 if keys_first else 0)
    kj = lax.broadcasted_iota(jnp.int32, shape, 0 if keys_first else 1)
    dist = rows.pos(qi) - rows.pos(kj % BLK) + jnp.where(kj < BLK, BLK, 0)
    inside = (dist >= 0) & (dist <= BLK)
    negd = (dist * (-rows.dil)).astype(F32)
    for f, valid in enumerate((inside & (kj >= BLK), inside)):
        for h in range(N_Q_HEADS):
            tbl[f * N_Q_HEADS + h] = jnp.where(valid, SLOPES[h] * negd, NEG)


def _bias2(tbl, n, h0, h1, axis=0):
    base = jnp.where(n == 0, 0, N_Q_HEADS)
    return jnp.concatenate([tbl[base + h0], tbl[base + h1]], axis=axis)


def _head_operands(kv2, hk, lo_mask):
    half, pos = hk // 2, hk % 2
    out = []
    for base in (0, KV_W):
        t = kv2[:, base + half * LANES: base + (half + 1) * LANES]
        sw = pltpu.roll(t, HEAD_DIM, axis=1)
        at_lo, at_hi = (t, sw) if pos == 0 else (sw, t)
        out.append(jnp.where(lo_mask, at_lo, 0.0).astype(BF16))
        out.append(jnp.where(lo_mask, 0.0, at_hi).astype(BF16))
    return out


def _nt(a, b):
    return lax.dot_general(a, b, (((1,), (1,)), ((), ())), preferred_element_type=F32)


def _tn(a, b):
    return lax.dot_general(a, b, (((0,), (0,)), ((), ())), preferred_element_type=F32)


def _attn_fwd(q, kv, dil, name, prev=None, gate=None):
    S = q.shape[0]
    rows = _Rows(dil, S)
    nb = rows.nb
    have_prev, last = prev is not None, gate is not None

    def body(*refs):
        refs = list(refs)
        q_ref, kvc_ref, kvp_ref = refs[:3]
        del refs[:3]
        if have_prev:
            po_ref, pl_ref = refs[:2]
            del refs[:2]
        if last:
            gate_ref = refs.pop(0)
        o_ref, lse_ref = refs[:2]
        y_ref = refs[2] if last else None
        tbl = refs[-1]
        n = pl.program_id(1)

        @pl.when((pl.program_id(0) == 0) & (n == 0))
        def _():
            _fill_bias_table(tbl, rows)

        kv2 = jnp.concatenate([_ld(kvp_ref), _ld(kvc_ref)], axis=0)
        lo_mask = lax.broadcasted_iota(jnp.int32, (2 * BLK, LANES), 1) < HEAD_DIM
        lane = lax.broadcasted_iota(jnp.int32, (BLK, LANES), 1)
        stats = jnp.zeros((BLK, LANES), F32)
        for hk in range(N_KV_HEADS):
            k_lo, k_hi, v_lo, v_hi = _head_operands(kv2, hk, lo_mask)
            cols = [slice(b * LANES, (b + 1) * LANES) for b in (2 * hk, 2 * hk + 1)]
            q2 = jnp.concatenate([_ld(q_ref, cols[0]), _ld(q_ref, cols[1])], axis=0).astype(BF16)
            o2 = jnp.zeros((2 * BLK, LANES), F32)
            for which, (kk, vv) in enumerate(((k_lo, v_lo), (k_hi, v_hi))):
                h0, h1 = 4 * hk + which, 4 * hk + 2 + which
                s = _nt(q2, kk) + _bias2(tbl, n, h0, h1)
                m = jnp.max(s, axis=1, keepdims=True)
                p = jnp.exp(s - m)
                l = jnp.sum(p, axis=1, keepdims=True)
                o2 = o2 + jnp.dot(p.astype(BF16), vv, preferred_element_type=F32) * (1.0 / l)
                lse = m + jnp.log(l)
                stats = jnp.where(lane == h0, lse[0:BLK], stats)
                stats = jnp.where(lane == h1, lse[BLK:], stats)
            _st(o_ref, o2[0:BLK], cols[0])
            _st(o_ref, o2[BLK:], cols[1])
        if have_prev:
            before = _ld(pl_ref)
            top = jnp.maximum(before, stats)
            e_old, e_new = jnp.exp(before - top), jnp.exp(stats - top)
            total = e_old + e_new
            stats = top + jnp.log(total)
            inv = 1.0 / total
            w_old, w_new = e_old * inv, e_new * inv
        if have_prev or last:
            lo = lane < HEAD_DIM
            for blk in range(ATT_W // LANES):
                cols = slice(blk * LANES, (blk + 1) * LANES)
                o_blk = _ld(o_ref, cols)
                if have_prev:
                    pick = lambda w: jnp.where(lo, w[:, 2 * blk:2 * blk + 1], w[:, 2 * blk + 1:2 * blk + 2])
                    o_blk = o_blk * pick(w_new) + _ld(po_ref, cols) * pick(w_old)
                    _st(o_ref, o_blk, cols)
                if last:
                    a = _ld(gate_ref, cols)
                    _st(y_ref, (o_blk * (a * _sigmoid(a))).astype(BF16), cols)
        _st(lse_ref, stats)

    here = lambda n: n
    before_n = lambda n: jnp.maximum(n - 1, 0)
    in_specs = [rows.spec(ATT_W, here), rows.spec(2 * KV_W, here), rows.spec(2 * KV_W, before_n)]
    args = [rows.of(q), rows.of(kv), rows.of(kv)]
    if have_prev:
        in_specs += [rows.spec(ATT_W, here), rows.spec(LANES, here)]
        args += [rows.of(prev[0]), rows.of(prev[1])]
    out_specs = [rows.spec(ATT_W, here), rows.spec(LANES, here)]
    out_shape = [jax.ShapeDtypeStruct(rows.view + (ATT_W,), F32), jax.ShapeDtypeStruct(rows.view + (LANES,), F32)]
    if last:
        in_specs.append(rows.spec(ATT_W, here))
        args.append(rows.of(gate))
        out_specs.append(rows.spec(ATT_W, here))
        out_shape.append(jax.ShapeDtypeStruct(rows.view + (ATT_W,), BF16))
    res = pl.pallas_call(
        body, name=name, grid=(dil, nb),
        in_specs=in_specs, out_specs=tuple(out_specs), out_shape=tuple(out_shape),
        scratch_shapes=[pltpu.VMEM((2 * N_Q_HEADS, BLK, 2 * BLK), F32)],
        compiler_params=_params(("arbitrary", "arbitrary")),
    )(*args)
    return tuple(r.reshape(S, r.shape[-1]) for r in res)


def _shifted_copies(buf, phases):
    n = phases.shape[1]
    for b in range(1, 8):
        phases[b - 1] = buf[b:b + n, :]


def _window(buf, phases, start, cols):
    b = start % 8
    if b == 0:
        return buf[start:start + 8, cols]
    return phases[b - 1, start - b:start - b + 8, cols]


def _broadcast_taps(w_ref, wb):
    for j in range(CONV_K):
        wb[j] = jnp.broadcast_to(w_ref[j:j + 1, :], wb.shape[1:])


def _conv_fwd(gates, conv_w, conv_b, ln_g, ln_b, tt=256):
    S = gates.shape[0]
    C = conv_w.shape[1]
    hb = tt // CONV_HALO

    def body(val_ref, glu_ref, hval_ref, hglu_ref, gate_ref, w_ref, b_ref, g_ref, beta_ref,
             conv_ref, y_ref, hbuf, hph):
        i = pl.program_id(0)
        halo = hval_ref[...] * _sigmoid(hglu_ref[...])
        hbuf[0:CONV_HALO, :] = jnp.where(i > 0, halo, 0.0)
        hbuf[CONV_HALO:, :] = val_ref[...] * _sigmoid(glu_ref[...])
        _shifted_copies(hbuf, hph)
        for cb in range(C // LANES):
            cols = slice(cb * LANES, (cb + 1) * LANES)
            wj = [jnp.broadcast_to(w_ref[j:j + 1, cols], (8, LANES)) for j in range(CONV_K)]
            for rc in range(tt // 8):
                acc = jnp.zeros((8, LANES), F32)
                for j in range(CONV_K):
                    start = rc * 8 + CONV_HALO - (CONV_K - 1) + j
                    acc = acc + _window(hbuf, hph, start, cols) * wj[j]
                conv_ref[rc * 8:(rc + 1) * 8, cols] = acc
        cv = conv_ref[...] + b_ref[...]
        conv_ref[...] = cv
        mu = jnp.mean(cv, axis=-1, keepdims=True)
        xc = cv - mu
        var = jnp.mean(xc * xc, axis=-1, keepdims=True)
        ln = xc * lax.rsqrt(var + LN_EPS) * g_ref[...] + beta_ref[...]
        gt = gate_ref[...]
        y_ref[...] = (ln * _sigmoid(ln) * (gt * _sigmoid(gt))).astype(BF16)

    vec = pl.BlockSpec((1, C), lambda i: (0, 0))
    return pl.pallas_call(
        body, name="conv_fwd", grid=(S // tt,),
        in_specs=[pl.BlockSpec((tt, C), lambda i: (i, 0)),
                  pl.BlockSpec((tt, C), lambda i: (i, 1)),
                  pl.BlockSpec((CONV_HALO, C), lambda i: (jnp.maximum(i * hb - 1, 0), 0)),
                  pl.BlockSpec((CONV_HALO, C), lambda i: (jnp.maximum(i * hb - 1, 0), 1)),
                  pl.BlockSpec((tt, C), lambda i: (i, 2)),
                  pl.BlockSpec((CONV_HALO, C), lambda i: (0, 0)), vec, vec, vec],
        out_specs=(pl.BlockSpec((tt, C), lambda i: (i, 0)), pl.BlockSpec((tt, C), lambda i: (i, 0))),
        out_shape=(jax.ShapeDtypeStruct((S, C), F32), jax.ShapeDtypeStruct((S, C), BF16)),
        scratch_shapes=[pltpu.VMEM((tt + CONV_HALO, C), F32), pltpu.VMEM((7, tt + CONV_HALO - 8, C), F32)],
        compiler_params=_params(("parallel",)),
    )(gates, gates, gates, gates, gates, conv_w, conv_b, ln_g, ln_b)


def _outproj_loss(x, y_att, y_conv, w_out, gf, target, tm=512):
    S, D = x.shape
    E = y_att.shape[1]

    def body(x_ref, ya_ref, yc_ref, w_ref, gf_ref, t_ref, dx_ref, dxb_ref, loss_ref, ggf_ref):
        @pl.when(pl.program_id(0) == 0)
        def _():
            loss_ref[...] = jnp.zeros_like(loss_ref)
            ggf_ref[...] = jnp.zeros_like(ggf_ref)

        x2 = (x_ref[...] + jnp.dot(_perm_rows(ya_ref[...], True), w_ref[0:E, :], preferred_element_type=F32)
              + jnp.dot(yc_ref[...], w_ref[E:, :], preferred_element_type=F32))
        r = lax.rsqrt(jnp.mean(x2 * x2, axis=-1, keepdims=True) + NORM_EPS)
        nrm = x2 * r
        gfv = gf_ref[...]
        err = nrm * gfv - t_ref[...]
        loss_ref[...] += jnp.sum(err * err, axis=0, keepdims=True)
        dout = err * (1.0 / D)
        ggf_ref[...] += jnp.sum(dout * nrm, axis=0, keepdims=True)
        dn = dout * gfv
        dx2 = r * (dn - nrm * jnp.mean(dn * nrm, axis=-1, keepdims=True))
        dx_ref[...] = dx2
        dxb_ref[...] = dx2.astype(BF16)

    row = lambda w: pl.BlockSpec((tm, w), lambda i: (i, 0))
    vec = pl.BlockSpec((1, D), lambda i: (0, 0))
    return pl.pallas_call(
        body, name="outproj_loss", grid=(S // tm,),
        in_specs=[row(D), row(E), row(E), pl.BlockSpec((2 * E, D), lambda i: (0, 0)), vec, row(D)],
        out_specs=(row(D), row(D), vec, vec),
        out_shape=(jax.ShapeDtypeStruct((S, D), F32), jax.ShapeDtypeStruct((S, D), BF16),
                   jax.ShapeDtypeStruct((1, D), F32), jax.ShapeDtypeStruct((1, D), F32)),
        compiler_params=_params(("arbitrary",)),
    )(x, y_att, y_conv, w_out, gf, target)


def _split3(v):
    hi = v.astype(BF16)
    r1 = v - hi.astype(F32)
    mid = r1.astype(BF16)
    lo = (r1 - mid.astype(F32)).astype(BF16)
    return hi, mid, lo


def _dy_att(dxb, w_out, gates, o, tm=512):
    S, D = dxb.shape
    E = ATT_W

    def body(dx_ref, w_ref, a_ref, o_ref, do_ref, da_ref, dl_ref, dxr_ref):
        dxr = _perm_rows(dx_ref[...], False)
        dxr_ref[...] = dxr
        dya = _nt(dxr, w_ref[...])
        a = a_ref[...]
        ov = o_ref[...]
        sl, dsl = _silu_and_grad(a)
        d_o = dya * sl
        do_ref[...] = d_o
        da_ref[...] = (dya * ov * dsl).astype(BF16)
        ci = lax.broadcasted_iota(jnp.int32, (E, LANES), 0) // HEAD_DIM
        hi = lax.broadcasted_iota(jnp.int32, (E, LANES), 1)
        sel = jnp.where(ci == hi, 1.0, 0.0).astype(BF16)
        acc = jnp.zeros((tm, LANES), F32)
        for part in _split3(d_o * ov):
            acc = acc + jnp.dot(part, sel, preferred_element_type=F32)
        dl_ref[...] = acc

    row = lambda w: pl.BlockSpec((tm, w), lambda i: (i, 0))
    return pl.pallas_call(
        body, name="dy_att", grid=(S // tm,),
        in_specs=[row(D), pl.BlockSpec((E, D), lambda i: (0, 0)), row(E), row(E)],
        out_specs=(row(E), row(E), row(LANES), row(D)),
        out_shape=(jax.ShapeDtypeStruct((S, E), F32), jax.ShapeDtypeStruct((S, E), BF16),
                   jax.ShapeDtypeStruct((S, LANES), F32), jax.ShapeDtypeStruct((S, D), BF16)),
        compiler_params=_params(("parallel",)),
    )(dxb, w_out, gates, o)


def _dy_conv(dxb, w_out, gates, conv_out, ln_g, ln_b, tm=512):
    S, D = dxb.shape
    C = conv_out.shape[1]

    def body(dx_ref, w_ref, gate_ref, cv_ref, g_ref, beta_ref, dgate_ref, dconv_ref, gg_ref, gb_ref, gcb_ref):
        @pl.when(pl.program_id(0) == 0)
        def _():
            gg_ref[...] = jnp.zeros_like(gg_ref)
            gb_ref[...] = jnp.zeros_like(gb_ref)
            gcb_ref[...] = jnp.zeros_like(gcb_ref)

        dyc = _nt(dx_ref[...], w_ref[...])
        cv = cv_ref[...]
        mu = jnp.mean(cv, axis=-1, keepdims=True)
        xc = cv - mu
        rstd = lax.rsqrt(jnp.mean(xc * xc, axis=-1, keepdims=True) + LN_EPS)
        nrm = xc * rstd
        gv = g_ref[...]
        ln = nrm * gv + beta_ref[...]
        u, du = _silu_and_grad(ln)
        gt = gate_ref[...]
        g2, dg2 = _silu_and_grad(gt)
        dgate_ref[...] = (dyc * u * dg2).astype(BF16)
        d_ln = dyc * g2 * du
        gb_ref[...] += jnp.sum(d_ln, axis=0, keepdims=True)
        gg_ref[...] += jnp.sum(d_ln * nrm, axis=0, keepdims=True)
        dn = d_ln * gv
        d_conv = rstd * (dn - jnp.mean(dn, axis=-1, keepdims=True)
                         - nrm * jnp.mean(dn * nrm, axis=-1, keepdims=True))
        dconv_ref[...] = d_conv
        gcb_ref[...] += jnp.sum(d_conv, axis=0, keepdims=True)

    row = lambda w: pl.BlockSpec((tm, w), lambda i: (i, 0))
    vec = pl.BlockSpec((1, C), lambda i: (0, 0))
    return pl.pallas_call(
        body, name="dy_conv", grid=(S // tm,),
        in_specs=[row(D), pl.BlockSpec((C, D), lambda i: (1, 0)),
                  pl.BlockSpec((tm, C), lambda i: (i, 2)), row(C), vec, vec],
        out_specs=(row(C), row(C), vec, vec, vec),
        out_shape=(jax.ShapeDtypeStruct((S, C), BF16), jax.ShapeDtypeStruct((S, C), F32),
                   jax.ShapeDtypeStruct((1, C), F32), jax.ShapeDtypeStruct((1, C), F32),
                   jax.ShapeDtypeStruct((1, C), F32)),
        compiler_params=_params(("arbitrary",)),
    )(dxb, w_out, gates, conv_out, ln_g, ln_b)


def _conv_bwd(d_conv, gates, d_c_gate, conv_w, hosted=None, tt=256):
    S, C = d_conv.shape
    hb = tt // CONV_HALO
    nt = S // tt
    hn = hosted.n if hosted is not None else 0

    def body(*refs):
        dc_ref, dnext_ref, val_ref, glu_ref, dg_ref, w_ref = refs[:6]
        h_ins = refs[6:6 + hn]
        out_ref, gw_ref = refs[6 + hn:8 + hn]
        h_outs = refs[8 + hn:8 + 2 * hn]
        hbuf, dbuf, dhbuf, dph, wb = refs[8 + 2 * hn:13 + 2 * hn]
        h_sems = refs[13 + 2 * hn:]
        i = pl.program_id(0)

        @pl.when(i == 0)
        def _():
            gw_ref[...] = jnp.zeros_like(gw_ref)
            _broadcast_taps(w_ref, wb)
            if hosted is not None:
                hosted.start(h_ins, h_outs, h_sems)

        val = val_ref[...]
        sg = _sigmoid(glu_ref[...])
        hbuf[...] = val * sg
        dbuf[0:tt, :] = dc_ref[...]
        dbuf[tt:, :] = jnp.where(i < nt - 1, dnext_ref[...], 0.0)
        _shifted_copies(dbuf, dph)
        for cb in range(C // LANES):
            cols = slice(cb * LANES, (cb + 1) * LANES)
            gacc = [jnp.zeros((8, LANES), F32) for _ in range(CONV_K)]
            group = 2
            for rc0 in range(0, tt // 8, group):
                hcur = [hbuf[(rc0 + r) * 8:(rc0 + r + 1) * 8, cols] for r in range(group)]
                accs = [jnp.zeros((8, LANES), F32) for _ in range(group)]
                for j in range(CONV_K):
                    wj = wb[j, :, cols]
                    for r in range(group):
                        dwin = _window(dbuf, dph, (rc0 + r) * 8 + (CONV_K - 1) - j, cols)
                        accs[r] = accs[r] + dwin * wj
                        gacc[j] = gacc[j] + dwin * hcur[r]
                for r in range(group):
                    dhbuf[(rc0 + r) * 8:(rc0 + r + 1) * 8, cols] = accs[r]
            for j in range(CONV_K):
                gw_ref[j:j + 1, cols] += jnp.sum(gacc[j], axis=0, keepdims=True)
        d_h = dhbuf[...]
        out_ref[:, 0:C] = (d_h * sg).astype(BF16)
        out_ref[:, C:2 * C] = (d_h * val * sg * (1.0 - sg)).astype(BF16)
        out_ref[:, 2 * C:3 * C] = dg_ref[...]

        if hosted is not None:
            @pl.when(i == nt - 1)
            def _():
                hosted.finish(h_ins, h_outs, h_sems)

    tile = lambda col: pl.BlockSpec((tt, C), lambda i: (i, col))
    in_specs = [tile(0),
                pl.BlockSpec((CONV_HALO, C), lambda i: (jnp.minimum((i + 1) * hb, S // CONV_HALO - 1), 0)),
                tile(0), tile(1), tile(0),
                pl.BlockSpec((CONV_HALO, C), lambda i: (0, 0))]
    args = [d_conv, d_conv, gates, gates, d_c_gate, conv_w]
    out_specs = [pl.BlockSpec((tt, 3 * C), lambda i: (i, 0)), pl.BlockSpec((CONV_HALO, C), lambda i: (0, 0))]
    out_shape = [jax.ShapeDtypeStruct((S, 3 * C), BF16), jax.ShapeDtypeStruct((CONV_HALO, C), F32)]
    scratch = [pltpu.VMEM((tt, C), F32), pltpu.VMEM((tt + CONV_HALO, C), F32), pltpu.VMEM((tt, C), F32),
               pltpu.VMEM((7, tt + CONV_HALO - 8, C), F32), pltpu.VMEM((CONV_K, 8, C), F32)]
    if hosted is not None:
        in_specs += [ANY_SPEC] * hn
        args += hosted.arrays
        out_specs += [ANY_SPEC] * hn
        out_shape += hosted.out_shapes()
        scratch += hosted.sem_shapes()
    res = pl.pallas_call(
        body, name="conv_bwd", grid=(nt,),
        in_specs=in_specs, out_specs=tuple(out_specs), out_shape=tuple(out_shape), scratch_shapes=scratch,
        compiler_params=_params(("arbitrary",)),
    )(*args)
    return res[0], res[1], list(res[2:])


def _attn_bwd(q, kv, d_o, lse, delta, dil, prev, final, name, hosted=None):
    S = q.shape[0]
    rows = _Rows(dil, S)
    nb = rows.nb
    out_dt = BF16 if final else F32
    have_prev = prev is not None
    hn = hosted.n if hosted is not None else 0

    def body(*refs):
        refs = list(refs)
        q_ref, do_ref, lse_ref, dl_ref, kvc_ref, kvp_ref = refs[:6]
        del refs[:6]
        if have_prev:
            pdq_ref, pdkv_ref = refs[:2]
            del refs[:2]
        h_ins = refs[:hn]
        dq_ref, dkv_ref = refs[hn:hn + 2]
        h_outs = refs[hn + 2:2 * hn + 2]
        carry, tbl = refs[2 * hn + 2:2 * hn + 4]
        h_sems = refs[2 * hn + 4:]
        n = pl.program_id(1)

        @pl.when((pl.program_id(0) == 0) & (n == 0))
        def _():
            if hosted is not None:
                hosted.start(h_ins, h_outs, h_sems)
            _fill_bias_table(tbl, rows, keys_first=True)

        @pl.when(n == 0)
        def _():
            carry[...] = jnp.zeros_like(carry)

        @pl.when(n < nb)
        def _():
            kv2 = jnp.concatenate([_ld(kvp_ref), _ld(kvc_ref)], axis=0)
            lse_t, dl_t = _ld(lse_ref).T, _ld(dl_ref).T
            lo_mask = lax.broadcasted_iota(jnp.int32, (2 * BLK, LANES), 1) < HEAD_DIM
            halves = [jnp.zeros((2 * BLK, LANES), F32) for _ in range(4)]
            for hk in range(N_KV_HEADS):
                k_lo, k_hi, v_lo, v_hi = _head_operands(kv2, hk, lo_mask)
                cols = [slice(b * LANES, (b + 1) * LANES) for b in (2 * hk, 2 * hk + 1)]
                q2 = jnp.concatenate([_ld(q_ref, cols[0]), _ld(q_ref, cols[1])], axis=0).astype(BF16)
                do2 = jnp.concatenate([_ld(do_ref, cols[0]), _ld(do_ref, cols[1])], axis=0).astype(BF16)
                dq2 = jnp.zeros((2 * BLK, LANES), F32)
                dks, dvs = [], []
                for which, (kk, vv) in enumerate(((k_lo, v_lo), (k_hi, v_hi))):
                    h0, h1 = 4 * hk + which, 4 * hk + 2 + which
                    s = _nt(kk, q2) + _bias2(tbl, n, h0, h1, axis=1)
                    lse2 = jnp.concatenate([lse_t[h0:h0 + 1, :], lse_t[h1:h1 + 1, :]], axis=1)
                    dl2 = jnp.concatenate([dl_t[h0:h0 + 1, :], dl_t[h1:h1 + 1, :]], axis=1)
                    p = jnp.exp(s - lse2)
                    ds = (p * (_nt(vv, do2) - dl2)).astype(BF16)
                    dq2 = dq2 + _tn(ds, kk)
                    dks.append(jnp.dot(ds, q2, preferred_element_type=F32))
                    dvs.append(jnp.dot(p.astype(BF16), do2, preferred_element_type=F32))
                dk_sum = jnp.where(lo_mask, dks[0], dks[1])
                dv_sum = jnp.where(lo_mask, dvs[0], dvs[1])
                for jp in range(2):
                    dq_blk = dq2[jp * BLK:(jp + 1) * BLK]
                    if have_prev:
                        dq_blk = dq_blk + _ld(pdq_ref, cols[jp])
                    if final:
                        dq_blk = dq_blk * (HEAD_DIM ** -0.5)
                    _st(dq_ref, dq_blk.astype(out_dt), cols[jp])
                half, pos = hk // 2, hk % 2
                here = lo_mask if pos == 0 else jnp.logical_not(lo_mask)
                dk_tot = dk_sum + pltpu.roll(dk_sum, HEAD_DIM, axis=1)
                dv_tot = dv_sum + pltpu.roll(dv_sum, HEAD_DIM, axis=1)
                halves[half] = halves[half] + jnp.where(here, dk_tot, 0.0)
                halves[2 + half] = halves[2 + half] + jnp.where(here, dv_tot, 0.0)
            for b in range(4):
                cols = slice(b * LANES, (b + 1) * LANES)
                done = carry[:, cols] + halves[b][0:BLK, :]
                if have_prev:
                    done = done + _ld(pdkv_ref, cols)
                _st(dkv_ref, done.astype(out_dt), cols)
                carry[:, cols] = halves[b][BLK:, :]

        @pl.when(n == nb)
        def _():
            done = carry[...]
            if have_prev:
                done = done + _ld(pdkv_ref)
            _st(dkv_ref, done.astype(out_dt))

        if hosted is not None:
            @pl.when((pl.program_id(0) == dil - 1) & (n == nb))
            def _():
                hosted.finish(h_ins, h_outs, h_sems)

    cur = lambda n: jnp.minimum(n, nb - 1)
    behind = lambda n: jnp.maximum(n - 1, 0)
    in_specs = [rows.spec(ATT_W, cur), rows.spec(ATT_W, cur), rows.spec(LANES, cur), rows.spec(LANES, cur),
                rows.spec(2 * KV_W, cur), rows.spec(2 * KV_W, behind)]
    args = [rows.of(q), rows.of(d_o), rows.of(lse), rows.of(delta), rows.of(kv), rows.of(kv)]
    if have_prev:
        in_specs += [rows.spec(ATT_W, cur), rows.spec(2 * KV_W, behind)]
        args += [rows.of(prev[0]), rows.of(prev[1])]
    out_specs = [rows.spec(ATT_W, cur), rows.spec(2 * KV_W, behind)]
    out_shape = [jax.ShapeDtypeStruct(rows.view + (ATT_W,), out_dt),
                 jax.ShapeDtypeStruct(rows.view + (2 * KV_W,), out_dt)]
    scratch = [pltpu.VMEM((BLK, 2 * KV_W), F32), pltpu.VMEM((2 * N_Q_HEADS, 2 * BLK, BLK), F32)]
    if hosted is not None:
        in_specs += [ANY_SPEC] * hn
        args += hosted.arrays
        out_specs += [ANY_SPEC] * hn
        out_shape += hosted.out_shapes()
        scratch += hosted.sem_shapes()
    res = pl.pallas_call(
        body, name=name, grid=(dil, nb + 1),
        in_specs=in_specs, out_specs=tuple(out_specs), out_shape=tuple(out_shape), scratch_shapes=scratch,
        compiler_params=_params(("arbitrary", "arbitrary")),
    )(*args)
    return (res[0].reshape(S, ATT_W), res[1].reshape(S, 2 * KV_W)), list(res[2:])


def _dh(segments, w_in, x, dx2, g, hosted=None, tm=1024, tk=512):
    S, D = x.shape
    ns = len(segments)
    counts = [a.shape[1] // tk for a, _ in segments]
    starts = [sum(counts[:s]) for s in range(ns)]
    nk = sum(counts)
    hn = hosted.n if hosted is not None else 0

    def body(*refs):
        seg_refs = refs[:ns]
        w_ref, x_ref, dx2_ref, g_ref = refs[ns:ns + 4]
        h_ins = refs[ns + 4:ns + 4 + hn]
        gx_ref, gng_ref = refs[ns + 4 + hn:ns + 6 + hn]
        h_outs = refs[ns + 6 + hn:ns + 6 + 2 * hn]
        acc = refs[ns + 6 + 2 * hn]
        h_sems = refs[ns + 7 + 2 * hn:]
        k, i = pl.program_id(0), pl.program_id(1)

        @pl.when((i == 0) & (k == 0))
        def _():
            gng_ref[...] = jnp.zeros_like(gng_ref)
            if hosted is not None:
                hosted.start(h_ins, h_outs, h_sems)

        @pl.when(k == 0)
        def _():
            acc[i] = jnp.zeros(acc.shape[1:], F32)

        for s in range(ns):
            @pl.when((k >= starts[s]) & (k < starts[s] + counts[s]))
            def _(s=s):
                t = seg_refs[s][...]
                if segments[s][1]:
                    t = _perm_rows(t, True)
                acc[i] += jnp.dot(t, w_ref[...], preferred_element_type=F32)

        @pl.when(k == nk - 1)
        def _():
            dh = acc[i]
            xf = x_ref[...]
            r = lax.rsqrt(jnp.mean(xf * xf, axis=-1, keepdims=True) + NORM_EPS)
            nrm = xf * r
            gng_ref[...] += jnp.sum(dh * nrm, axis=0, keepdims=True)
            dn = dh * g_ref[...]
            gx_ref[...] = dx2_ref[...] + r * (dn - nrm * jnp.mean(dn * nrm, axis=-1, keepdims=True))

        if hosted is not None:
            @pl.when((i == S // tm - 1) & (k == nk - 1))
            def _():
                hosted.finish(h_ins, h_outs, h_sems)

    ni = S // tm
    row = pl.BlockSpec((tm, D), lambda k, i: (jnp.where(k == nk - 1, i, 0), 0))
    vec = pl.BlockSpec((1, D), lambda k, i: (0, 0))

    def seg_index(s):
        def index(k, i):
            j = k - starts[s]
            return jnp.where(j < 0, 0, jnp.where(j >= counts[s], ni - 1, i)), jnp.clip(j, 0, counts[s] - 1)
        return index

    in_specs = [pl.BlockSpec((tm, tk), seg_index(s)) for s in range(ns)]
    in_specs += [pl.BlockSpec((tk, D), lambda k, i: (k, 0)), row, row, vec]
    args = [a for a, _ in segments] + [w_in, x, dx2, g]
    out_specs = [row, vec]
    out_shape = [jax.ShapeDtypeStruct((S, D), F32), jax.ShapeDtypeStruct((1, D), F32)]
    scratch = [pltpu.VMEM((ni, tm, D), F32)]
    if hosted is not None:
        in_specs += [ANY_SPEC] * hn
        args += hosted.arrays
        out_specs += [ANY_SPEC] * hn
        out_shape += hosted.out_shapes()
        scratch += hosted.sem_shapes()
    res = pl.pallas_call(
        body, name="dh", grid=(nk, S // tm),
        in_specs=in_specs, out_specs=tuple(out_specs), out_shape=tuple(out_shape), scratch_shapes=scratch,
        compiler_params=_params(("arbitrary", "arbitrary"), BIG_VMEM_LIMIT),
    )(*args)
    return res[0], res[1], list(res[2:])


def _tn_matmul(a, bs, name, b_first=False, tm=512):
    M, K = a.shape
    nb = len(bs)
    shapes = [(b.shape[1], K) if b_first else (K, b.shape[1]) for b in bs]

    def body(a_ref, *refs):
        @pl.when(pl.program_id(0) == 0)
        def _():
            for o_ref in refs[nb:]:
                o_ref[...] = jnp.zeros_like(o_ref)

        at = a_ref[...]
        for b_ref, o_ref in zip(refs[:nb], refs[nb:]):
            for c in range(0, b_ref.shape[1], 512):
                if b_first:
                    o_ref[c:c + 512, :] += _tn(b_ref[:, c:c + 512], at)
                else:
                    o_ref[:, c:c + 512] += _tn(at, b_ref[:, c:c + 512])

    return pl.pallas_call(
        body, name=name, grid=(M // tm,),
        in_specs=[pl.BlockSpec((tm, K), lambda m: (m, 0))] + [pl.BlockSpec((tm, b.shape[1]), lambda m: (m, 0))
                                                              for b in bs],
        out_specs=tuple(pl.BlockSpec(s, lambda m: (0, 0)) for s in shapes),
        out_shape=tuple(jax.ShapeDtypeStruct(s, F32) for s in shapes),
        compiler_params=_params(("arbitrary",)),
    )(a, *bs)


def _adamw(parts, w, m, v, name, tr=None, split=None, by_chip=False):
    R, C = w.shape
    tr = R if tr is None else tr
    parts = [parts] if split is None else list(parts)
    npar = len(parts)

    def total(p_ref):
        if by_chip:
            c = lax.axis_index("c")
            g = p_ref[c].astype(F32)
            for chip in range(1, N_DEV // 2):
                g = g + p_ref[2 * chip + c].astype(F32)
            return g
        g = p_ref[0].astype(F32)
        for dev in range(1, N_DEV):
            g = g + p_ref[dev].astype(F32)
        return g

    def body(*refs):
        w_ref, m_ref, v_ref, g_out, d_out, m_out, v_out = refs[npar:]
        if split is None:
            g = total(refs[0])
        else:
            g = jnp.where(_mesh_pos()[3] < split, total(refs[0]), total(refs[1]))
        mn = ADAM_B1 * m_ref[...] + (1.0 - ADAM_B1) * g
        vn = ADAM_B2 * v_ref[...] + (1.0 - ADAM_B2) * (g * g)
        m_hat = mn / (1.0 - ADAM_B1 ** ADAM_STEP)
        v_hat = vn / (1.0 - ADAM_B2 ** ADAM_STEP)
        g_out[...] = g
        d_out[...] = -ADAM_LR * (m_hat / (jnp.sqrt(v_hat) + ADAM_EPS) + ADAM_WD * w_ref[...])
        m_out[...] = mn
        v_out[...] = vn

    blk = pl.BlockSpec((tr, C), lambda i: (i, 0))
    shp = jax.ShapeDtypeStruct((R, C), F32)
    return pl.pallas_call(
        body, name=name, grid=(R // tr,),
        in_specs=[pl.BlockSpec((N_DEV, tr, C), lambda i: (0, i, 0))] * npar + [blk, blk, blk],
        out_specs=(blk, blk, blk, blk), out_shape=(shp, shp, shp, shp),
        compiler_params=_params(("parallel",)),
    )(*parts, w, m, v)


def _local_step(x, target, norm_g, w_in, conv_w, conv_b, ln_g, ln_b, w_out, gf, exchanges=None):
    ex_out, ex_att, ex_conv = exchanges if exchanges is not None else (None, None, None)
    conv_cols = w_in.shape[0] - 2 * ATT_W - 2 * KV_W
    q, kv, a_gate, gates, h_rm, h = _inproj(
        x, norm_g, w_in, [(ATT_W, HEAD_DIM ** -0.5, True), (2 * KV_W, 1.0, True), (ATT_W, 1.0, True),
                          (conv_cols, 1.0, False)])

    merged = None
    for idx, (_, dil) in enumerate(reversed(PATTERNS)):
        merged = _attn_fwd(q, kv, dil, "attn_fwd_d%d" % dil, merged,
                           a_gate if idx == len(PATTERNS) - 1 else None)
    o, lse, y_att = merged
    conv_out, y_conv = _conv_fwd(gates, conv_w, conv_b, ln_g, ln_b)
    dx2, dxb, loss_cols, g_gf = _outproj_loss(x, y_att, y_conv, w_out, gf, target)

    d_o, d_a_gate, delta, dxb_rm = _dy_att(dxb, w_out, a_gate, o)
    g_w_out = jnp.concatenate([_tn_matmul(y_att, [dxb_rm], "gw_out_att")[0],
                               _tn_matmul(y_conv, [dxb], "gw_out_conv")[0]], axis=0)
    acc, out_parts = None, []
    for idx, (_, dil) in enumerate(reversed(PATTERNS)):
        hosted = ex_out(g_w_out) if (idx == 0 and ex_out is not None) else None
        acc, outs = _attn_bwd(q, kv, d_o, lse, delta, dil, acc, idx == len(PATTERNS) - 1, "attn_bwd_d%d" % dil,
                              hosted)
        out_parts += outs
    dq, dkv = acc
    g_q, g_kv, g_a = _tn_matmul(h_rm, [dq, dkv, d_a_gate], "gw_in_att", b_first=True)

    d_c_gate, d_conv, g_ln_g, g_ln_b, g_conv_b = _dy_conv(dxb, w_out, gates, conv_out, ln_g, ln_b)
    dgates, g_conv_w, att_parts = _conv_bwd(d_conv, gates, d_c_gate, conv_w,
                                            ex_att(g_q, g_kv, g_a) if ex_att is not None else None)
    g_c, = _tn_matmul(h, [dgates], "gw_in_conv", b_first=True)
    grad_x, g_norm_g, conv_parts = _dh(
        [(dq, True), (dkv, True), (d_a_gate, True), (dgates, False)], w_in, x, dx2, norm_g,
        ex_conv(g_a, g_c, g_conv_w) if ex_conv is not None else None)
    small = (g_norm_g, g_conv_b, g_ln_g, g_ln_b, g_gf, loss_cols)
    return grad_x, (g_q, g_kv, g_a, g_c), g_w_out, g_conv_w, small, (out_parts, att_parts, conv_parts)


def kernel(x, norm_g, w_in, conv_w, conv_b, conv_ln_g, conv_ln_b, w_out, final_norm_g, loss_target, m_norm_g, m_w_in, m_conv_w, m_conv_b, m_conv_ln_g, m_conv_ln_b, m_w_out, m_final_norm_g, v_norm_g, v_w_in, v_conv_w, v_conv_b, v_conv_ln_g, v_conv_ln_b, v_w_out, v_final_norm_g):
    S, D = x.shape[1], x.shape[2]
    win_sh, wout_sh, cw_sh = w_in[0].T, w_out[0], conv_w[0]
    cols_sh, rows_sh, ch_sh = win_sh.shape[0], wout_sh.shape[0], cw_sh.shape[1]

    win_all, wout_all, cw_all = _gather_two_level(
        [win_sh.astype(BF16), wout_sh.astype(BF16), cw_sh], "gather_weights")
    w_in_full = win_all.reshape(N_DEV * cols_sh, D)
    w_out_full = wout_all.reshape(N_DEV * rows_sh, D)
    conv_w_full = cw_all.transpose(1, 0, 2).reshape(CONV_K, N_DEV * ch_sh)
    conv_w_full = jnp.pad(conv_w_full, ((0, CONV_HALO - CONV_K), (0, 0)))
    gf = final_norm_g.reshape(1, D)

    first = -(-(ATT_W + 2 * KV_W) // cols_sh)
    a_off = first * cols_sh - (ATT_W + 2 * KV_W)
    assert 0 <= a_off <= ATT_W

    def pieces(parts, n):
        return jnp.concatenate([p.astype(BF16) for p in parts], axis=0).reshape(n, cols_sh, D)

    def ex_out(g_w_out):
        return _Exchange([g_w_out.reshape(N_DEV, rows_sh, D).astype(BF16)], [(0, N_DEV)])

    same_core = (2, 4, 6)

    def ex_att(g_q, g_kv, g_a):
        mine = _chip_sum(pieces([g_q, g_kv, g_a[:a_off]], first), 0, "rs_att")
        return _Exchange([mine], [(0, first)], [same_core])

    def ex_conv(g_a, g_c, g_conv_w):
        mine = _chip_sum(pieces([g_a[a_off:], g_c], N_DEV - first), first, "rs_conv")
        return _Exchange(
            [mine, g_conv_w[:CONV_K].reshape(CONV_K, N_DEV, ch_sh).transpose(1, 0, 2)],
            [(first, N_DEV), (0, N_DEV)], [same_core, None])

    grad_x, _, _, _, small, parts = _local_step(
        x[0], loss_target[0], norm_g, w_in_full, conv_w_full, conv_b, conv_ln_g, conv_ln_b, w_out_full, gf,
        (ex_out, ex_att, ex_conv))
    (wout_parts,), (win_parts_lo,), (win_parts_hi, cw_parts) = parts

    small_pack = jnp.concatenate(list(small) + [jnp.zeros((2, D), F32)], axis=0)
    small_parts, = _exchange([small_pack], [None], "gather_small")

    upd_win = _adamw((win_parts_lo, win_parts_hi), win_sh, m_w_in[0].T, v_w_in[0].T, "adamw_w_in",
                     tr=cols_sh // 2, split=first, by_chip=True)
    upd_wout = _adamw(wout_parts, wout_sh, m_w_out[0], v_w_out[0], "adamw_w_out", tr=128)
    upd_cw = _adamw(cw_parts, cw_sh, m_conv_w[0], v_conv_w[0], "adamw_conv_w")
    zeros3 = jnp.zeros((3, D), F32)
    stack = lambda a, b, c, d_, e: jnp.concatenate([a, b, c, d_, e.reshape(1, D), zeros3], axis=0)
    upd_small = _adamw(
        small_parts,
        stack(norm_g, conv_b, conv_ln_g, conv_ln_b, final_norm_g),
        stack(m_norm_g, m_conv_b, m_conv_ln_g, m_conv_ln_b, m_final_norm_g),
        stack(v_norm_g, v_conv_b, v_conv_ln_g, v_conv_ln_b, v_final_norm_g) + jnp.concatenate(
            [jnp.zeros((5, D), F32), jnp.ones((3, D), F32)], axis=0),
        "adamw_small")

    loss = 0.5 / D * jnp.sum(upd_small[0][5])

    def outputs(kind):
        sm = upd_small[kind]
        return [sm[0:1], upd_win[kind].T[None], upd_cw[kind][None], sm[1:2], sm[2:3], sm[3:4],
                upd_wout[kind][None], sm[4]]

    return (loss, grad_x[None], *outputs(0), *outputs(1), *outputs(2), *outputs(3))
```

```python
import jax
import jax.numpy as jnp
from jax import lax
from jax.experimental import pallas as pl
from jax.experimental.pallas import tpu as pltpu

F32 = jnp.float32
BF16 = jnp.bfloat16

HEAD_DIM = 64
N_KV_HEADS = 4
N_Q_HEADS = 16
ATT_W = 1024
KV_W = 256
CONV_K = 31
CONV_HALO = 32
PATTERNS = ((128, 1), (512, 4), (2048, 16))
BLK = 128
LANES = 128
NORM_EPS = 1e-6
LN_EPS = 1e-5
NEG = -1e30
N_DEV = 8
ADAM_LR, ADAM_B1, ADAM_B2, ADAM_EPS, ADAM_WD, ADAM_STEP = 0.001, 0.9, 0.999, 1e-08, 0.01, 10
VMEM_LIMIT = 48 * 1024 * 1024
BIG_VMEM_LIMIT = 58 * 1024 * 1024
SLOPES = tuple(2.0 ** (-8.0 * (h + 1) / N_Q_HEADS) for h in range(N_Q_HEADS))
MESH = pl.DeviceIdType.MESH


def _params(sem, vmem_limit=VMEM_LIMIT):
    return pltpu.CompilerParams(dimension_semantics=sem, vmem_limit_bytes=vmem_limit)


def _sigmoid(v):
    return 1.0 / (1.0 + jnp.exp(-v))


def _silu_and_grad(v):
    s = _sigmoid(v)
    return v * s, s * (1.0 + v * (1.0 - s))


ANY_SPEC = pl.BlockSpec(memory_space=pl.ANY)


def _mesh_pos():
    x, y, c = lax.axis_index("x"), lax.axis_index("y"), lax.axis_index("c")
    return x, y, c, 4 * x + 2 * y + c


def _flipped(k, x, y, c):
    px = 1 - x if k & 4 else x
    py = 1 - y if k & 2 else y
    pc = 1 - c if k & 1 else c
    return (px, py, pc), 4 * px + 2 * py + pc


class _Exchange:
    def __init__(self, arrays, dests, flips=None):
        self.arrays, self.dests, self.n = list(arrays), list(dests), len(arrays)
        self.flips = [tuple(range(1, N_DEV)) if f is None else tuple(f)
                      for f in (flips if flips is not None else [None] * self.n)]

    def out_shapes(self):
        return [jax.ShapeDtypeStruct((N_DEV,) + a.shape[-2:], a.dtype) for a in self.arrays]

    def sem_shapes(self):
        return [pltpu.SemaphoreType.DMA((self.n, N_DEV - 1)), pltpu.SemaphoreType.DMA((self.n, N_DEV - 1)),
                pltpu.SemaphoreType.DMA((self.n,))]

    def _when(self, a, dev, fn):
        if self.dests[a] is None:
            fn()
        else:
            lo, hi = self.dests[a]
            pl.when((dev >= lo) & (dev < hi))(fn)

    def _mine(self, ins, a, dev):
        return ins[a] if self.dests[a] is None else ins[a].at[dev - self.dests[a][0]]

    def _copy(self, ins, outs, sems, a, k, src_dev, slot, target):
        return pltpu.make_async_remote_copy(
            src_ref=self._mine(ins, a, src_dev), dst_ref=outs[a].at[slot],
            send_sem=sems[0].at[a, k - 1], recv_sem=sems[1].at[a, k - 1],
            device_id=target, device_id_type=MESH)

    def start(self, ins, outs, sems):
        x, y, c, me = _mesh_pos()
        for a in range(self.n):
            self._when(a, me, lambda a=a: pltpu.make_async_copy(
                self._mine(ins, a, me), outs[a].at[me], sems[2].at[a]).start())
            for k in self.flips[a]:
                target, peer = _flipped(k, x, y, c)
                self._when(a, peer, lambda a=a, k=k, target=target, peer=peer: self._copy(
                    ins, outs, sems, a, k, peer, me, target).start())

    def finish(self, ins, outs, sems):
        x, y, c, me = _mesh_pos()
        lo0 = [0 if d is None else d[0] for d in self.dests]
        for a in range(self.n):
            for k in self.flips[a]:
                target, peer = _flipped(k, x, y, c)
                self._when(a, me, lambda a=a, k=k, peer=peer: self._copy(
                    ins, outs, sems, a, k, lo0[a], peer, (x, y, c)).wait_recv())
            for k in self.flips[a]:
                target, peer = _flipped(k, x, y, c)
                self._when(a, peer, lambda a=a, k=k, target=target, peer=peer: self._copy(
                    ins, outs, sems, a, k, peer, me, target).wait_send())
            self._when(a, me, lambda a=a: pltpu.make_async_copy(
                self._mine(ins, a, me), outs[a].at[me], sems[2].at[a]).wait())


def _exchange(arrays, dests, name, flips=None):
    ex = _Exchange(arrays, dests, flips)
    na = ex.n

    def body(*refs):
        ins, outs, sems = refs[:na], refs[na:2 * na], refs[2 * na:]
        ex.start(ins, outs, sems)
        ex.finish(ins, outs, sems)

    return pl.pallas_call(
        body, name=name, out_shape=tuple(ex.out_shapes()),
        in_specs=[ANY_SPEC] * na, out_specs=tuple([ANY_SPEC] * na), scratch_shapes=ex.sem_shapes(),
    )(*arrays)


def _chip_sum(pieces, lo, name):
    n, R, C = pieces.shape

    def swap(p_ref, t_ref, send_sems, recv_sems):
        x, y, c, me = _mesh_pos()
        for i in range(n):
            mine = (lo + i) % 2
            cp = pltpu.make_async_remote_copy(
                src_ref=p_ref.at[i], dst_ref=t_ref.at[i], send_sem=send_sems.at[i], recv_sem=recv_sems.at[i],
                device_id=(x, y, 1 - c), device_id_type=MESH)
            pl.when(c != mine)(cp.start)
        for i in range(n):
            mine = (lo + i) % 2
            cp = pltpu.make_async_remote_copy(
                src_ref=p_ref.at[i], dst_ref=t_ref.at[i], send_sem=send_sems.at[i], recv_sem=recv_sems.at[i],
                device_id=(x, y, 1 - c), device_id_type=MESH)
            pl.when(c == mine)(cp.wait_recv)
            pl.when(c != mine)(cp.wait_send)

    other = pl.pallas_call(
        swap, name=name + "_swap", out_shape=jax.ShapeDtypeStruct(pieces.shape, pieces.dtype),
        in_specs=[ANY_SPEC], out_specs=ANY_SPEC,
        scratch_shapes=[pltpu.SemaphoreType.DMA((n,)), pltpu.SemaphoreType.DMA((n,))],
    )(pieces)

    def add(p_ref, t_ref, o_ref):
        o_ref[...] = (p_ref[...].astype(F32) + t_ref[...].astype(F32)).astype(o_ref.dtype)

    tr = R // 2
    blk = pl.BlockSpec((None, tr, C), lambda i, r: (i, r, 0))
    return pl.pallas_call(
        add, name=name + "_add", grid=(n, R // tr), in_specs=[blk, blk], out_specs=blk,
        out_shape=jax.ShapeDtypeStruct(pieces.shape, pieces.dtype),
        compiler_params=_params(("parallel", "parallel")),
    )(pieces, other)


class _Gather:
    def __init__(self, arrays):
        self.arrays, self.n = list(arrays), len(arrays)

    def out_shapes(self):
        return [jax.ShapeDtypeStruct((N_DEV,) + a.shape, a.dtype) for a in self.arrays]

    def sem_shapes(self):
        return [pltpu.SemaphoreType.DMA((self.n, N_DEV - 1)), pltpu.SemaphoreType.DMA((self.n, N_DEV - 1)),
                pltpu.SemaphoreType.DMA((self.n,))]

    def _plan(self, ins, outs, sems):
        x, y, c, me = _mesh_pos()
        chips = [(1 - x, y), (x, 1 - y), (1 - x, 1 - y)]

        def copy(a, k, src, block, to):
            px, py, pc = block
            return pltpu.make_async_remote_copy(
                src_ref=src, dst_ref=outs[a].at[4 * px + 2 * py + pc], send_sem=sems[0].at[a, k],
                recv_sem=sems[1].at[a, k], device_id=to, device_id_type=MESH)

        local = [pltpu.make_async_copy(ins[a], outs[a].at[me], sems[2].at[a]) for a in range(self.n)]
        first = []
        for a in range(self.n):
            first.append(copy(a, 0, ins[a], (x, y, c), (x, y, 1 - c)))
            first += [copy(a, 1 + j, ins[a], (x, y, c), (*chip, c)) for j, chip in enumerate(chips)]
        return (x, y, c), chips, copy, local, first

    def start(self, ins, outs, sems):
        _, _, _, local, first = self._plan(ins, outs, sems)
        for cp in local + first:
            cp.start()

    def finish(self, ins, outs, sems):
        (x, y, c), chips, copy, local, first = self._plan(ins, outs, sems)
        passed = []
        for j, chip in enumerate(chips):
            for a in range(self.n):
                copy(a, 1 + j, ins[a], (*chip, c), (x, y, c)).wait_recv()
                px, py = chip
                fwd = copy(a, 4 + j, outs[a].at[4 * px + 2 * py + c], (*chip, c), (x, y, 1 - c))
                fwd.start()
                passed.append(fwd)
        for a in range(self.n):
            copy(a, 0, ins[a], (x, y, 1 - c), (x, y, c)).wait_recv()
            for j, chip in enumerate(chips):
                copy(a, 4 + j, ins[a], (*chip, 1 - c), (x, y, c)).wait_recv()
        for cp in first + passed:
            cp.wait_send()
        for cp in local:
            cp.wait()


def _gather_two_level(arrays, name):
    ga = _Gather(arrays)
    na = ga.n

    def body(*refs):
        ins, outs, sems = refs[:na], refs[na:2 * na], refs[2 * na:]
        ga.start(ins, outs, sems)
        ga.finish(ins, outs, sems)

    return pl.pallas_call(
        body, name=name, out_shape=tuple(ga.out_shapes()),
        in_specs=[ANY_SPEC] * na, out_specs=tuple([ANY_SPEC] * na), scratch_shapes=ga.sem_shapes(),
    )(*arrays)


CHUNK = 128
RESIDUES = 16
PER_RES = CHUNK // RESIDUES


def _perm_rows(tile, inverse):
    a = lax.broadcasted_iota(jnp.int32, (CHUNK, CHUNK), 0)
    b = lax.broadcasted_iota(jnp.int32, (CHUNK, CHUNK), 1)
    if inverse:
        a, b = b, a
    p = jnp.where(a == PER_RES * (b % RESIDUES) + b // RESIDUES, 1.0, 0.0).astype(BF16)
    parts = [jnp.dot(p, tile[c * CHUNK:(c + 1) * CHUNK], preferred_element_type=F32)
             for c in range(tile.shape[0] // CHUNK)]
    return jnp.concatenate(parts, axis=0).astype(BF16)


class _Rows:
    def __init__(self, dil, S):
        nc = S // CHUNK
        self.dil = dil
        if dil == 1:
            self.view, self.block, self.nb = (nc, CHUNK), (None, CHUNK), nc
            self.index = lambda r, b: (b, 0, 0)
        elif dil == 4:
            self.view, self.block, self.nb = (nc, 4, 4, PER_RES), (4, 4, None, PER_RES), nc // 4
            self.index = lambda r, b: (b, 0, r, 0, 0)
        elif dil == RESIDUES:
            self.view, self.block, self.nb = (nc, RESIDUES, PER_RES), (RESIDUES, None, PER_RES), nc // RESIDUES
            self.index = lambda r, b: (b, r, 0, 0)
        else:
            raise NotImplementedError(dil)

    def of(self, a):
        return a.reshape(self.view + (a.shape[-1],))

    def spec(self, width, which_block):
        return pl.BlockSpec(self.block + (width,), lambda r, n: self.index(r, which_block(n)))

    def pos(self, row):
        if self.dil == 1:
            return (row % PER_RES) * RESIDUES + row // PER_RES
        if self.dil == 4:
            return (row // 32) * 32 + (row % PER_RES) * 4 + (row % 32) // PER_RES
        return row


def _ld(ref, cols=slice(None)):
    v = ref[(slice(None),) * (len(ref.shape) - 1) + (cols,)]
    return v.reshape(BLK, v.shape[-1])


def _st(ref, val, cols=slice(None)):
    ref[(slice(None),) * (len(ref.shape) - 1) + (cols,)] = val.reshape(ref.shape[:-1] + (val.shape[-1],))


def _inproj(x, g, w_t, segments, hosted=None, tm=1024, tn=512):
    S, D = x.shape
    ns = len(segments)
    ni = S // tm
    counts = [nc // tn for nc, _, _ in segments]
    starts = [sum(counts[:s]) for s in range(ns)]

    hn = hosted.n if hosted is not None else 0
    last_p = sum(counts)

    def body(x_ref, g_ref, w_ref, *rest):
        h_ins, rest = rest[:hn], rest[hn:]
        outs = rest[:ns]
        hrm_out, h_out = rest[ns:ns + 2]
        h_outs = rest[ns + 2:ns + 2 + hn]
        hrm_scr, h_scr = rest[ns + 2 + hn:ns + 4 + hn]
        h_sems = rest[ns + 4 + hn:]
        p, i = pl.program_id(0), pl.program_id(1)

        if hosted is not None:
            @pl.when((p == 0) & (i == 0))
            def _():
                hosted.start(h_ins, h_outs, h_sems)

            @pl.when((p == last_p) & (i == ni - 1))
            def _():
                hosted.finish(h_ins, h_outs, h_sems)

        @pl.when(p == 0)
        def _():
            xf = x_ref[...]
            r = lax.rsqrt(jnp.mean(xf * xf, axis=-1, keepdims=True) + NORM_EPS)
            h = (xf * r * g_ref[...]).astype(BF16)
            hrm = _perm_rows(h, False)
            h_scr[i] = h
            hrm_scr[i] = hrm
            h_out[...] = h
            hrm_out[...] = hrm

        for s, (_, scale, rm) in enumerate(segments):
            @pl.when((p > starts[s]) & (p <= starts[s] + counts[s]))
            def _(s=s, scale=scale, rm=rm):
                acc = _nt((hrm_scr if rm else h_scr)[i], w_ref[...])
                outs[s][...] = acc * scale if scale != 1.0 else acc

    def out_index(s):
        def index(p, i):
            j = p - 1 - starts[s]
            row = jnp.where(j < 0, 0, jnp.where(j >= counts[s], ni - 1, i))
            return row, jnp.clip(j, 0, counts[s] - 1)
        return index

    first_pass = pl.BlockSpec((tm, D), lambda p, i: (jnp.where(p == 0, i, ni - 1), 0))
    out_specs = [pl.BlockSpec((tm, tn), out_index(s)) for s in range(ns)]
    out_shape = [jax.ShapeDtypeStruct((S, nc), F32) for nc, _, _ in segments]
    in_specs = [first_pass, pl.BlockSpec((1, D), lambda p, i: (0, 0)),
                pl.BlockSpec((tn, D), lambda p, i: (jnp.maximum(p - 1, 0), 0))]
    args = [x, g, w_t]
    out_specs = out_specs + [first_pass, first_pass]
    out_shape = out_shape + [jax.ShapeDtypeStruct((S, D), BF16)] * 2
    scratch = [pltpu.VMEM((ni, tm, D), BF16), pltpu.VMEM((ni, tm, D), BF16)]
    if hosted is not None:
        in_specs += [ANY_SPEC] * hn
        args += hosted.arrays
        out_specs += [ANY_SPEC] * hn
        out_shape += hosted.out_shapes()
        scratch += hosted.sem_shapes()
    return pl.pallas_call(
        body, name="inproj", grid=(1 + last_p, ni),
        in_specs=in_specs, out_specs=tuple(out_specs), out_shape=tuple(out_shape), scratch_shapes=scratch,
        compiler_params=_params(("arbitrary", "arbitrary"), BIG_VMEM_LIMIT),
    )(*args)


def _fill_bias_table(tbl, rows, keys_first=False):
    shape = (2 * BLK, BLK) if keys_first else (BLK, 2 * BLK)
    qi = lax.broadcasted_iota(jnp.int32, shape, 1 if keys_first else 0)
    kj = lax.broadcasted_iota(jnp.int32, shape, 0 if keys_first else 1)
    dist = rows.pos(qi) - rows.pos(kj % BLK) + jnp.where(kj < BLK, BLK, 0)
    inside = (dist >= 0) & (dist <= BLK)
    negd = (dist * (-rows.dil)).astype(F32)
    for f, valid in enumerate((inside & (kj >= BLK), inside)):
        for h in range(N_Q_HEADS):
            tbl[f * N_Q_HEADS + h] = jnp.where(valid, SLOPES[h] * negd, NEG)


def _bias2(tbl, n, h0, h1, axis=0):
    base = jnp.where(n == 0, 0, N_Q_HEADS)
    return jnp.concatenate([tbl[base + h0], tbl[base + h1]], axis=axis)


def _head_operands(kv2, hk, lo_mask):
    half, pos = hk // 2, hk % 2
    out = []
    for base in (0, KV_W):
        t = kv2[:, base + half * LANES: base + (half + 1) * LANES]
        sw = pltpu.roll(t, HEAD_DIM, axis=1)
        at_lo, at_hi = (t, sw) if pos == 0 else (sw, t)
        out.append(jnp.where(lo_mask, at_lo, 0.0).astype(BF16))
        out.append(jnp.where(lo_mask, 0.0, at_hi).astype(BF16))
    return out


def _nt(a, b):
    return lax.dot_general(a, b, (((1,), (1,)), ((), ())), preferred_element_type=F32)


def _tn(a, b):
    return lax.dot_general(a, b, (((0,), (0,)), ((), ())), preferred_element_type=F32)


def _attn_fwd(q, kv, dil, name, prev=None, gate=None):
    S = q.shape[0]
    rows = _Rows(dil, S)
    nb = rows.nb
    have_prev, last = prev is not None, gate is not None

    def body(*refs):
        refs = list(refs)
        q_ref, kvc_ref, kvp_ref = refs[:3]
        del refs[:3]
        if have_prev:
            po_ref, pl_ref = refs[:2]
            del refs[:2]
        if last:
            gate_ref = refs.pop(0)
        o_ref, lse_ref = refs[:2]
        y_ref = refs[2] if last else None
        tbl = refs[-1]
        n = pl.program_id(1)

        @pl.when((pl.program_id(0) == 0) & (n == 0))
        def _():
            _fill_bias_table(tbl, rows)

        kv2 = jnp.concatenate([_ld(kvp_ref), _ld(kvc_ref)], axis=0)
        lo_mask = lax.broadcasted_iota(jnp.int32, (2 * BLK, LANES), 1) < HEAD_DIM
        lane = lax.broadcasted_iota(jnp.int32, (BLK, LANES), 1)
        stats = jnp.zeros((BLK, LANES), F32)
        for hk in range(N_KV_HEADS):
            k_lo, k_hi, v_lo, v_hi = _head_operands(kv2, hk, lo_mask)
            cols = [slice(b * LANES, (b + 1) * LANES) for b in (2 * hk, 2 * hk + 1)]
            q2 = jnp.concatenate([_ld(q_ref, cols[0]), _ld(q_ref, cols[1])], axis=0).astype(BF16)
            o2 = jnp.zeros((2 * BLK, LANES), F32)
            for which, (kk, vv) in enumerate(((k_lo, v_lo), (k_hi, v_hi))):
                h0, h1 = 4 * hk + which, 4 * hk + 2 + which
                s = _nt(q2, kk) + _bias2(tbl, n, h0, h1)
                m = jnp.max(s, axis=1, keepdims=True)
                p = jnp.exp(s - m)
                l = jnp.sum(p, axis=1, keepdims=True)
                o2 = o2 + jnp.dot(p.astype(BF16), vv, preferred_element_type=F32) * (1.0 / l)
                lse = m + jnp.log(l)
                stats = jnp.where(lane == h0, lse[0:BLK], stats)
                stats = jnp.where(lane == h1, lse[BLK:], stats)
            _st(o_ref, o2[0:BLK], cols[0])
            _st(o_ref, o2[BLK:], cols[1])
        if have_prev:
            before = _ld(pl_ref)
            top = jnp.maximum(before, stats)
            e_old, e_new = jnp.exp(before - top), jnp.exp(stats - top)
            total = e_old + e_new
            stats = top + jnp.log(total)
            inv = 1.0 / total
            w_old, w_new = e_old * inv, e_new * inv
        if have_prev or last:
            lo = lane < HEAD_DIM
            for blk in range(ATT_W // LANES):
                cols = slice(blk * LANES, (blk + 1) * LANES)
                o_blk = _ld(o_ref, cols)
                if have_prev:
                    pick = lambda w: jnp.where(lo, w[:, 2 * blk:2 * blk + 1], w[:, 2 * blk + 1:2 * blk + 2])
                    o_blk = o_blk * pick(w_new) + _ld(po_ref, cols) * pick(w_old)
                    _st(o_ref, o_blk, cols)
                if last:
                    a = _ld(gate_ref, cols)
                    _st(y_ref, (o_blk * (a * _sigmoid(a))).astype(BF16), cols)
        _st(lse_ref, stats)

    here = lambda n: n
    before_n = lambda n: jnp.maximum(n - 1, 0)
    in_specs = [rows.spec(ATT_W, here), rows.spec(2 * KV_W, here), rows.spec(2 * KV_W, before_n)]
    args = [rows.of(q), rows.of(kv), rows.of(kv)]
    if have_prev:
        in_specs += [rows.spec(ATT_W, here), rows.spec(LANES, here)]
        args += [rows.of(prev[0]), rows.of(prev[1])]
    out_specs = [rows.spec(ATT_W, here), rows.spec(LANES, here)]
    out_shape = [jax.ShapeDtypeStruct(rows.view + (ATT_W,), F32), jax.ShapeDtypeStruct(rows.view + (LANES,), F32)]
    if last:
        in_specs.append(rows.spec(ATT_W, here))
        args.append(rows.of(gate))
        out_specs.append(rows.spec(ATT_W, here))
        out_shape.append(jax.ShapeDtypeStruct(rows.view + (ATT_W,), BF16))
    res = pl.pallas_call(
        body, name=name, grid=(dil, nb),
        in_specs=in_specs, out_specs=tuple(out_specs), out_shape=tuple(out_shape),
        scratch_shapes=[pltpu.VMEM((2 * N_Q_HEADS, BLK, 2 * BLK), F32)],
        compiler_params=_params(("arbitrary", "arbitrary")),
    )(*args)
    return tuple(r.reshape(S, r.shape[-1]) for r in res)


def _shifted_copies(buf, phases):
    n = phases.shape[1]
    for b in range(1, 8):
        phases[b - 1] = buf[b:b + n, :]


def _window(buf, phases, start, cols):
    b = start % 8
    if b == 0:
        return buf[start:start + 8, cols]
    return phases[b - 1, start - b:start - b + 8, cols]


def _broadcast_taps(w_ref, wb):
    for j in range(CONV_K):
        wb[j] = jnp.broadcast_to(w_ref[j:j + 1, :], wb.shape[1:])


def _conv_fwd(gates, conv_w, conv_b, ln_g, ln_b, tt=256):
    S = gates.shape[0]
    C = conv_w.shape[1]
    hb = tt // CONV_HALO

    def body(val_ref, glu_ref, hval_ref, hglu_ref, gate_ref, w_ref, b_ref, g_ref, beta_ref,
             conv_ref, y_ref, hbuf, hph):
        i = pl.program_id(0)
        halo = hval_ref[...] * _sigmoid(hglu_ref[...])
        hbuf[0:CONV_HALO, :] = jnp.where(i > 0, halo, 0.0)
        hbuf[CONV_HALO:, :] = val_ref[...] * _sigmoid(glu_ref[...])
        _shifted_copies(hbuf, hph)
        for cb in range(C // LANES):
            cols = slice(cb * LANES, (cb + 1) * LANES)
            wj = [jnp.broadcast_to(w_ref[j:j + 1, cols], (8, LANES)) for j in range(CONV_K)]
            for rc in range(tt // 8):
                acc = jnp.zeros((8, LANES), F32)
                for j in range(CONV_K):
                    start = rc * 8 + CONV_HALO - (CONV_K - 1) + j
                    acc = acc + _window(hbuf, hph, start, cols) * wj[j]
                conv_ref[rc * 8:(rc + 1) * 8, cols] = acc
        cv = conv_ref[...] + b_ref[...]
        conv_ref[...] = cv
        mu = jnp.mean(cv, axis=-1, keepdims=True)
        xc = cv - mu
        var = jnp.mean(xc * xc, axis=-1, keepdims=True)
        ln = xc * lax.rsqrt(var + LN_EPS) * g_ref[...] + beta_ref[...]
        gt = gate_ref[...]
        y_ref[...] = (ln * _sigmoid(ln) * (gt * _sigmoid(gt))).astype(BF16)

    vec = pl.BlockSpec((1, C), lambda i: (0, 0))
    return pl.pallas_call(
        body, name="conv_fwd", grid=(S // tt,),
        in_specs=[pl.BlockSpec((tt, C), lambda i: (i, 0)),
                  pl.BlockSpec((tt, C), lambda i: (i, 1)),
                  pl.BlockSpec((CONV_HALO, C), lambda i: (jnp.maximum(i * hb - 1, 0), 0)),
                  pl.BlockSpec((CONV_HALO, C), lambda i: (jnp.maximum(i * hb - 1, 0), 1)),
                  pl.BlockSpec((tt, C), lambda i: (i, 2)),
                  pl.BlockSpec((CONV_HALO, C), lambda i: (0, 0)), vec, vec, vec],
        out_specs=(pl.BlockSpec((tt, C), lambda i: (i, 0)), pl.BlockSpec((tt, C), lambda i: (i, 0))),
        out_shape=(jax.ShapeDtypeStruct((S, C), F32), jax.ShapeDtypeStruct((S, C), BF16)),
        scratch_shapes=[pltpu.VMEM((tt + CONV_HALO, C), F32), pltpu.VMEM((7, tt + CONV_HALO - 8, C), F32)],
        compiler_params=_params(("parallel",)),
    )(gates, gates, gates, gates, gates, conv_w, conv_b, ln_g, ln_b)


def _outproj_loss(x, y_att, y_conv, w_out, gf, target, tm=512):
    S, D = x.shape
    E = y_att.shape[1]

    def body(x_ref, ya_ref, yc_ref, w_ref, gf_ref, t_ref, dx_ref, dxb_ref, loss_ref, ggf_ref):
        @pl.when(pl.program_id(0) == 0)
        def _():
            loss_ref[...] = jnp.zeros_like(loss_ref)
            ggf_ref[...] = jnp.zeros_like(ggf_ref)

        x2 = (x_ref[...] + jnp.dot(_perm_rows(ya_ref[...], True), w_ref[0:E, :], preferred_element_type=F32)
              + jnp.dot(yc_ref[...], w_ref[E:, :], preferred_element_type=F32))
        r = lax.rsqrt(jnp.mean(x2 * x2, axis=-1, keepdims=True) + NORM_EPS)
        nrm = x2 * r
        gfv = gf_ref[...]
        err = nrm * gfv - t_ref[...]
        loss_ref[...] += jnp.sum(err * err, axis=0, keepdims=True)
        dout = err * (1.0 / D)
        ggf_ref[...] += jnp.sum(dout * nrm, axis=0, keepdims=True)
        dn = dout * gfv
        dx2 = r * (dn - nrm * jnp.mean(dn * nrm, axis=-1, keepdims=True))
        dx_ref[...] = dx2
        dxb_ref[...] = dx2.astype(BF16)

    row = lambda w: pl.BlockSpec((tm, w), lambda i: (i, 0))
    vec = pl.BlockSpec((1, D), lambda i: (0, 0))
    return pl.pallas_call(
        body, name="outproj_loss", grid=(S // tm,),
        in_specs=[row(D), row(E), row(E), pl.BlockSpec((2 * E, D), lambda i: (0, 0)), vec, row(D)],
        out_specs=(row(D), row(D), vec, vec),
        out_shape=(jax.ShapeDtypeStruct((S, D), F32), jax.ShapeDtypeStruct((S, D), BF16),
                   jax.ShapeDtypeStruct((1, D), F32), jax.ShapeDtypeStruct((1, D), F32)),
        compiler_params=_params(("arbitrary",)),
    )(x, y_att, y_conv, w_out, gf, target)


def _split3(v):
    hi = v.astype(BF16)
    r1 = v - hi.astype(F32)
    mid = r1.astype(BF16)
    lo = (r1 - mid.astype(F32)).astype(BF16)
    return hi, mid, lo


def _dy_att(dxb, w_out, gates, o, tm=512):
    S, D = dxb.shape
    E = ATT_W

    def body(dx_ref, w_ref, a_ref, o_ref, do_ref, da_ref, dl_ref, dxr_ref):
        dxr = _perm_rows(dx_ref[...], False)
        dxr_ref[...] = dxr
        dya = _nt(dxr, w_ref[...])
        a = a_ref[...]
        ov = o_ref[...]
        sl, dsl = _silu_and_grad(a)
        d_o = dya * sl
        do_ref[...] = d_o
        da_ref[...] = (dya * ov * dsl).astype(BF16)
        ci = lax.broadcasted_iota(jnp.int32, (E, LANES), 0) // HEAD_DIM
        hi = lax.broadcasted_iota(jnp.int32, (E, LANES), 1)
        sel = jnp.where(ci == hi, 1.0, 0.0).astype(BF16)
        acc = jnp.zeros((tm, LANES), F32)
        for part in _split3(d_o * ov):
            acc = acc + jnp.dot(part, sel, preferred_element_type=F32)
        dl_ref[...] = acc

    row = lambda w: pl.BlockSpec((tm, w), lambda i: (i, 0))
    return pl.pallas_call(
        body, name="dy_att", grid=(S // tm,),
        in_specs=[row(D), pl.BlockSpec((E, D), lambda i: (0, 0)), row(E), row(E)],
        out_specs=(row(E), row(E), row(LANES), row(D)),
        out_shape=(jax.ShapeDtypeStruct((S, E), F32), jax.ShapeDtypeStruct((S, E), BF16),
                   jax.ShapeDtypeStruct((S, LANES), F32), jax.ShapeDtypeStruct((S, D), BF16)),
        compiler_params=_params(("parallel",)),
    )(dxb, w_out, gates, o)


def _dy_conv(dxb, w_out, gates, conv_out, ln_g, ln_b, tm=512):
    S, D = dxb.shape
    C = conv_out.shape[1]

    def body(dx_ref, w_ref, gate_ref, cv_ref, g_ref, beta_ref, dgate_ref, dconv_ref, gg_ref, gb_ref, gcb_ref):
        @pl.when(pl.program_id(0) == 0)
        def _():
            gg_ref[...] = jnp.zeros_like(gg_ref)
            gb_ref[...] = jnp.zeros_like(gb_ref)
            gcb_ref[...] = jnp.zeros_like(gcb_ref)

        dyc = _nt(dx_ref[...], w_ref[...])
        cv = cv_ref[...]
        mu = jnp.mean(cv, axis=-1, keepdims=True)
        xc = cv - mu
        rstd = lax.rsqrt(jnp.mean(xc * xc, axis=-1, keepdims=True) + LN_EPS)
        nrm = xc * rstd
        gv = g_ref[...]
        ln = nrm * gv + beta_ref[...]
        u, du = _silu_and_grad(ln)
        gt = gate_ref[...]
        g2, dg2 = _silu_and_grad(gt)
        dgate_ref[...] = (dyc * u * dg2).astype(BF16)
        d_ln = dyc * g2 * du
        gb_ref[...] += jnp.sum(d_ln, axis=0, keepdims=True)
        gg_ref[...] += jnp.sum(d_ln * nrm, axis=0, keepdims=True)
        dn = d_ln * gv
        d_conv = rstd * (dn - jnp.mean(dn, axis=-1, keepdims=True)
                         - nrm * jnp.mean(dn * nrm, axis=-1, keepdims=True))
        dconv_ref[...] = d_conv
        gcb_ref[...] += jnp.sum(d_conv, axis=0, keepdims=True)

    row = lambda w: pl.BlockSpec((tm, w), lambda i: (i, 0))
    vec = pl.BlockSpec((1, C), lambda i: (0, 0))
    return pl.pallas_call(
        body, name="dy_conv", grid=(S // tm,),
        in_specs=[row(D), pl.BlockSpec((C, D), lambda i: (1, 0)),
                  pl.BlockSpec((tm, C), lambda i: (i, 2)), row(C), vec, vec],
        out_specs=(row(C), row(C), vec, vec, vec),
        out_shape=(jax.ShapeDtypeStruct((S, C), BF16), jax.ShapeDtypeStruct((S, C), F32),
                   jax.ShapeDtypeStruct((1, C), F32), jax.ShapeDtypeStruct((1, C), F32),
                   jax.ShapeDtypeStruct((1, C), F32)),
        compiler_params=_params(("arbitrary",)),
    )(dxb, w_out, gates, conv_out, ln_g, ln_b)


def _conv_bwd(d_conv, gates, d_c_gate, conv_w, hosted=None, tt=256):
    S, C = d_conv.shape
    hb = tt // CONV_HALO
    nt = S // tt
    hn = hosted.n if hosted is not None else 0

    def body(*refs):
        dc_ref, dnext_ref, val_ref, glu_ref, dg_ref, w_ref = refs[:6]
        h_ins = refs[6:6 + hn]
        out_ref, gw_ref = refs[6 + hn:8 + hn]
        h_outs = refs[8 + hn:8 + 2 * hn]
        hbuf, dbuf, dhbuf, dph, wb = refs[8 + 2 * hn:13 + 2 * hn]
        h_sems = refs[13 + 2 * hn:]
        i = pl.program_id(0)

        @pl.when(i == 0)
        def _():
            gw_ref[...] = jnp.zeros_like(gw_ref)
            _broadcast_taps(w_ref, wb)
            if hosted is not None:
                hosted.start(h_ins, h_outs, h_sems)

        val = val_ref[...]
        sg = _sigmoid(glu_ref[...])
        hbuf[...] = val * sg
        dbuf[0:tt, :] = dc_ref[...]
        dbuf[tt:, :] = jnp.where(i < nt - 1, dnext_ref[...], 0.0)
        _shifted_copies(dbuf, dph)
        for cb in range(C // LANES):
            cols = slice(cb * LANES, (cb + 1) * LANES)
            gacc = [jnp.zeros((8, LANES), F32) for _ in range(CONV_K)]
            group = 2
            for rc0 in range(0, tt // 8, group):
                hcur = [hbuf[(rc0 + r) * 8:(rc0 + r + 1) * 8, cols] for r in range(group)]
                accs = [jnp.zeros((8, LANES), F32) for _ in range(group)]
                for j in range(CONV_K):
                    wj = wb[j, :, cols]
                    for r in range(group):
                        dwin = _window(dbuf, dph, (rc0 + r) * 8 + (CONV_K - 1) - j, cols)
                        accs[r] = accs[r] + dwin * wj
                        gacc[j] = gacc[j] + dwin * hcur[r]
                for r in range(group):
                    dhbuf[(rc0 + r) * 8:(rc0 + r + 1) * 8, cols] = accs[r]
            for j in range(CONV_K):
                gw_ref[j:j + 1, cols] += jnp.sum(gacc[j], axis=0, keepdims=True)
        d_h = dhbuf[...]
        out_ref[:, 0:C] = (d_h * sg).astype(BF16)
        out_ref[:, C:2 * C] = (d_h * val * sg * (1.0 - sg)).astype(BF16)
        out_ref[:, 2 * C:3 * C] = dg_ref[...]

        if hosted is not None:
            @pl.when(i == nt - 1)
            def _():
                hosted.finish(h_ins, h_outs, h_sems)

    tile = lambda col: pl.BlockSpec((tt, C), lambda i: (i, col))
    in_specs = [tile(0),
                pl.BlockSpec((CONV_HALO, C), lambda i: (jnp.minimum((i + 1) * hb, S // CONV_HALO - 1), 0)),
                tile(0), tile(1), tile(0),
                pl.BlockSpec((CONV_HALO, C), lambda i: (0, 0))]
    args = [d_conv, d_conv, gates, gates, d_c_gate, conv_w]
    out_specs = [pl.BlockSpec((tt, 3 * C), lambda i: (i, 0)), pl.BlockSpec((CONV_HALO, C), lambda i: (0, 0))]
    out_shape = [jax.ShapeDtypeStruct((S, 3 * C), BF16), jax.ShapeDtypeStruct((CONV_HALO, C), F32)]
    scratch = [pltpu.VMEM((tt, C), F32), pltpu.VMEM((tt + CONV_HALO, C), F32), pltpu.VMEM((tt, C), F32),
               pltpu.VMEM((7, tt + CONV_HALO - 8, C), F32), pltpu.VMEM((CONV_K, 8, C), F32)]
    if hosted is not None:
        in_specs += [ANY_SPEC] * hn
        args += hosted.arrays
        out_specs += [ANY_SPEC] * hn
        out_shape += hosted.out_shapes()
        scratch += hosted.sem_shapes()
    res = pl.pallas_call(
        body, name="conv_bwd", grid=(nt,),
        in_specs=in_specs, out_specs=tuple(out_specs), out_shape=tuple(out_shape), scratch_shapes=scratch,
        compiler_params=_params(("arbitrary",)),
    )(*args)
    return res[0], res[1], list(res[2:])


def _attn_bwd(q, kv, d_o, lse, delta, dil, prev, final, name, hosted=None):
    S = q.shape[0]
    rows = _Rows(dil, S)
    nb = rows.nb
    out_dt = BF16 if final else F32
    have_prev = prev is not None
    hn = hosted.n if hosted is not None else 0

    def body(*refs):
        refs = list(refs)
        q_ref, do_ref, lse_ref, dl_ref, kvc_ref, kvp_ref = refs[:6]
        del refs[:6]
        if have_prev:
            pdq_ref, pdkv_ref = refs[:2]
            del refs[:2]
        h_ins = refs[:hn]
        dq_ref, dkv_ref = refs[hn:hn + 2]
        h_outs = refs[hn + 2:2 * hn + 2]
        carry, tbl = refs[2 * hn + 2:2 * hn + 4]
        h_sems = refs[2 * hn + 4:]
        n = pl.program_id(1)

        @pl.when((pl.program_id(0) == 0) & (n == 0))
        def _():
            if hosted is not None:
                hosted.start(h_ins, h_outs, h_sems)
            _fill_bias_table(tbl, rows, keys_first=True)

        @pl.when(n == 0)
        def _():
            carry[...] = jnp.zeros_like(carry)

        @pl.when(n < nb)
        def _():
            kv2 = jnp.concatenate([_ld(kvp_ref), _ld(kvc_ref)], axis=0)
            lse_t, dl_t = _ld(lse_ref).T, _ld(dl_ref).T
            lo_mask = lax.broadcasted_iota(jnp.int32, (2 * BLK, LANES), 1) < HEAD_DIM
            halves = [jnp.zeros((2 * BLK, LANES), F32) for _ in range(4)]
            for hk in range(N_KV_HEADS):
                k_lo, k_hi, v_lo, v_hi = _head_operands(kv2, hk, lo_mask)
                cols = [slice(b * LANES, (b + 1) * LANES) for b in (2 * hk, 2 * hk + 1)]
                q2 = jnp.concatenate([_ld(q_ref, cols[0]), _ld(q_ref, cols[1])], axis=0).astype(BF16)
                do2 = jnp.concatenate([_ld(do_ref, cols[0]), _ld(do_ref, cols[1])], axis=0).astype(BF16)
                dq2 = jnp.zeros((2 * BLK, LANES), F32)
                dks, dvs = [], []
                for which, (kk, vv) in enumerate(((k_lo, v_lo), (k_hi, v_hi))):
                    h0, h1 = 4 * hk + which, 4 * hk + 2 + which
                    s = _nt(kk, q2) + _bias2(tbl, n, h0, h1, axis=1)
                    lse2 = jnp.concatenate([lse_t[h0:h0 + 1, :], lse_t[h1:h1 + 1, :]], axis=1)
                    dl2 = jnp.concatenate([dl_t[h0:h0 + 1, :], dl_t[h1:h1 + 1, :]], axis=1)
                    p = jnp.exp(s - lse2)
                    ds = (p * (_nt(vv, do2) - dl2)).astype(BF16)
                    dq2 = dq2 + _tn(ds, kk)
                    dks.append(jnp.dot(ds, q2, preferred_element_type=F32))
                    dvs.append(jnp.dot(p.astype(BF16), do2, preferred_element_type=F32))
                dk_sum = jnp.where(lo_mask, dks[0], dks[1])
                dv_sum = jnp.where(lo_mask, dvs[0], dvs[1])
                for jp in range(2):
                    dq_blk = dq2[jp * BLK:(jp + 1) * BLK]
                    if have_prev:
                        dq_blk = dq_blk + _ld(pdq_ref, cols[jp])
                    if final:
                        dq_blk = dq_blk * (HEAD_DIM ** -0.5)
                    _st(dq_ref, dq_blk.astype(out_dt), cols[jp])
                half, pos = hk // 2, hk % 2
                here = lo_mask if pos == 0 else jnp.logical_not(lo_mask)
                dk_tot = dk_sum + pltpu.roll(dk_sum, HEAD_DIM, axis=1)
                dv_tot = dv_sum + pltpu.roll(dv_sum, HEAD_DIM, axis=1)
                halves[half] = halves[half] + jnp.where(here, dk_tot, 0.0)
                halves[2 + half] = halves[2 + half] + jnp.where(here, dv_tot, 0.0)
            for b in range(4):
                cols = slice(b * LANES, (b + 1) * LANES)
                done = carry[:, cols] + halves[b][0:BLK, :]
                if have_prev:
                    done = done + _ld(pdkv_ref, cols)
                _st(dkv_ref, done.astype(out_dt), cols)
                carry[:, cols] = halves[b][BLK:, :]

        @pl.when(n == nb)
        def _():
            done = carry[...]
            if have_prev:
                done = done + _ld(pdkv_ref)
            _st(dkv_ref, done.astype(out_dt))

        if hosted is not None:
            @pl.when((pl.program_id(0) == dil - 1) & (n == nb))
            def _():
                hosted.finish(h_ins, h_outs, h_sems)

    cur = lambda n: jnp.minimum(n, nb - 1)
    behind = lambda n: jnp.maximum(n - 1, 0)
    in_specs = [rows.spec(ATT_W, cur), rows.spec(ATT_W, cur), rows.spec(LANES, cur), rows.spec(LANES, cur),
                rows.spec(2 * KV_W, cur), rows.spec(2 * KV_W, behind)]
    args = [rows.of(q), rows.of(d_o), rows.of(lse), rows.of(delta), rows.of(kv), rows.of(kv)]
    if have_prev:
        in_specs += [rows.spec(ATT_W, cur), rows.spec(2 * KV_W, behind)]
        args += [rows.of(prev[0]), rows.of(prev[1])]
    out_specs = [rows.spec(ATT_W, cur), rows.spec(2 * KV_W, behind)]
    out_shape = [jax.ShapeDtypeStruct(rows.view + (ATT_W,), out_dt),
                 jax.ShapeDtypeStruct(rows.view + (2 * KV_W,), out_dt)]
    scratch = [pltpu.VMEM((BLK, 2 * KV_W), F32), pltpu.VMEM((2 * N_Q_HEADS, 2 * BLK, BLK), F32)]
    if hosted is not None:
        in_specs += [ANY_SPEC] * hn
        args += hosted.arrays
        out_specs += [ANY_SPEC] * hn
        out_shape += hosted.out_shapes()
        scratch += hosted.sem_shapes()
    res = pl.pallas_call(
        body, name=name, grid=(dil, nb + 1),
        in_specs=in_specs, out_specs=tuple(out_specs), out_shape=tuple(out_shape), scratch_shapes=scratch,
        compiler_params=_params(("arbitrary", "arbitrary")),
    )(*args)
    return (res[0].reshape(S, ATT_W), res[1].reshape(S, 2 * KV_W)), list(res[2:])


def _dh(segments, w_in, x, dx2, g, hosted=None, tm=1024, tk=512):
    S, D = x.shape
    ns = len(segments)
    counts = [a.shape[1] // tk for a, _ in segments]
    starts = [sum(counts[:s]) for s in range(ns)]
    nk = sum(counts)
    hn = hosted.n if hosted is not None else 0

    def body(*refs):
        seg_refs = refs[:ns]
        w_ref, x_ref, dx2_ref, g_ref = refs[ns:ns + 4]
        h_ins = refs[ns + 4:ns + 4 + hn]
        gx_ref, gng_ref = refs[ns + 4 + hn:ns + 6 + hn]
        h_outs = refs[ns + 6 + hn:ns + 6 + 2 * hn]
        acc = refs[ns + 6 + 2 * hn]
        h_sems = refs[ns + 7 + 2 * hn:]
        k, i = pl.program_id(0), pl.program_id(1)

        @pl.when((i == 0) & (k == 0))
        def _():
            gng_ref[...] = jnp.zeros_like(gng_ref)
            if hosted is not None:
                hosted.start(h_ins, h_outs, h_sems)

        @pl.when(k == 0)
        def _():
            acc[i] = jnp.zeros(acc.shape[1:], F32)

        for s in range(ns):
            @pl.when((k >= starts[s]) & (k < starts[s] + counts[s]))
            def _(s=s):
                t = seg_refs[s][...]
                if segments[s][1]:
                    t = _perm_rows(t, True)
                acc[i] += jnp.dot(t, w_ref[...], preferred_element_type=F32)

        @pl.when(k == nk - 1)
        def _():
            dh = acc[i]
            xf = x_ref[...]
            r = lax.rsqrt(jnp.mean(xf * xf, axis=-1, keepdims=True) + NORM_EPS)
            nrm = xf * r
            gng_ref[...] += jnp.sum(dh * nrm, axis=0, keepdims=True)
            dn = dh * g_ref[...]
            gx_ref[...] = dx2_ref[...] + r * (dn - nrm * jnp.mean(dn * nrm, axis=-1, keepdims=True))

        if hosted is not None:
            @pl.when((i == S // tm - 1) & (k == nk - 1))
            def _():
                hosted.finish(h_ins, h_outs, h_sems)

    ni = S // tm
    row = pl.BlockSpec((tm, D), lambda k, i: (jnp.where(k == nk - 1, i, 0), 0))
    vec = pl.BlockSpec((1, D), lambda k, i: (0, 0))

    def seg_index(s):
        def index(k, i):
            j = k - starts[s]
            return jnp.where(j < 0, 0, jnp.where(j >= counts[s], ni - 1, i)), jnp.clip(j, 0, counts[s] - 1)
        return index

    in_specs = [pl.BlockSpec((tm, tk), seg_index(s)) for s in range(ns)]
    in_specs += [pl.BlockSpec((tk, D), lambda k, i: (k, 0)), row, row, vec]
    args = [a for a, _ in segments] + [w_in, x, dx2, g]
    out_specs = [row, vec]
    out_shape = [jax.ShapeDtypeStruct((S, D), F32), jax.ShapeDtypeStruct((1, D), F32)]
    scratch = [pltpu.VMEM((ni, tm, D), F32)]
    if hosted is not None:
        in_specs += [ANY_SPEC] * hn
        args += hosted.arrays
        out_specs += [ANY_SPEC] * hn
        out_shape += hosted.out_shapes()
        scratch += hosted.sem_shapes()
    res = pl.pallas_call(
        body, name="dh", grid=(nk, S // tm),
        in_specs=in_specs, out_specs=tuple(out_specs), out_shape=tuple(out_shape), scratch_shapes=scratch,
        compiler_params=_params(("arbitrary", "arbitrary"), BIG_VMEM_LIMIT),
    )(*args)
    return res[0], res[1], list(res[2:])


def _tn_matmul(a, bs, name, b_first=False, tm=512):
    M, K = a.shape
    nb = len(bs)
    shapes = [(b.shape[1], K) if b_first else (K, b.shape[1]) for b in bs]

    def body(a_ref, *refs):
        @pl.when(pl.program_id(0) == 0)
        def _():
            for o_ref in refs[nb:]:
                o_ref[...] = jnp.zeros_like(o_ref)

        at = a_ref[...]
        for b_ref, o_ref in zip(refs[:nb], refs[nb:]):
            for c in range(0, b_ref.shape[1], 512):
                if b_first:
                    o_ref[c:c + 512, :] += _tn(b_ref[:, c:c + 512], at)
                else:
                    o_ref[:, c:c + 512] += _tn(at, b_ref[:, c:c + 512])

    return pl.pallas_call(
        body, name=name, grid=(M // tm,),
        in_specs=[pl.BlockSpec((tm, K), lambda m: (m, 0))] + [pl.BlockSpec((tm, b.shape[1]), lambda m: (m, 0))
                                                              for b in bs],
        out_specs=tuple(pl.BlockSpec(s, lambda m: (0, 0)) for s in shapes),
        out_shape=tuple(jax.ShapeDtypeStruct(s, F32) for s in shapes),
        compiler_params=_params(("arbitrary",)),
    )(a, *bs)


def _adamw(parts, w, m, v, name, tr=None, split=None, by_chip=False):
    R, C = w.shape
    tr = R if tr is None else tr
    parts = [parts] if split is None else list(parts)
    npar = len(parts)

    def total(p_ref):
        if by_chip:
            c = lax.axis_index("c")
            g = p_ref[c].astype(F32)
            for chip in range(1, N_DEV // 2):
                g = g + p_ref[2 * chip + c].astype(F32)
            return g
        g = p_ref[0].astype(F32)
        for dev in range(1, N_DEV):
            g = g + p_ref[dev].astype(F32)
        return g

    def body(*refs):
        w_ref, m_ref, v_ref, g_out, d_out, m_out, v_out = refs[npar:]
        if split is None:
            g = total(refs[0])
        else:
            g = jnp.where(_mesh_pos()[3] < split, total(refs[0]), total(refs[1]))
        mn = ADAM_B1 * m_ref[...] + (1.0 - ADAM_B1) * g
        vn = ADAM_B2 * v_ref[...] + (1.0 - ADAM_B2) * (g * g)
        m_hat = mn / (1.0 - ADAM_B1 ** ADAM_STEP)
        v_hat = vn / (1.0 - ADAM_B2 ** ADAM_STEP)
        g_out[...] = g
        d_out[...] = -ADAM_LR * (m_hat / (jnp.sqrt(v_hat) + ADAM_EPS) + ADAM_WD * w_ref[...])
        m_out[...] = mn
        v_out[...] = vn

    blk = pl.BlockSpec((tr, C), lambda i: (i, 0))
    shp = jax.ShapeDtypeStruct((R, C), F32)
    return pl.pallas_call(
        body, name=name, grid=(R // tr,),
        in_specs=[pl.BlockSpec((N_DEV, tr, C), lambda i: (0, i, 0))] * npar + [blk, blk, blk],
        out_specs=(blk, blk, blk, blk), out_shape=(shp, shp, shp, shp),
        compiler_params=_params(("parallel",)),
    )(*parts, w, m, v)


def _local_step(x, target, norm_g, w_in, conv_w, conv_b, ln_g, ln_b, w_out, gf, exchanges=None, late_weights=None):
    ex_out, ex_att, ex_conv = exchanges if exchanges is not None else (None, None, None)
    conv_cols = w_in.shape[0] - 2 * ATT_W - 2 * KV_W
    q, kv, a_gate, gates, h_rm, h, *gathered = _inproj(
        x, norm_g, w_in, [(ATT_W, HEAD_DIM ** -0.5, True), (2 * KV_W, 1.0, True), (ATT_W, 1.0, True),
                          (conv_cols, 1.0, False)], late_weights[0] if late_weights is not None else None)
    if late_weights is not None:
        conv_w, w_out = late_weights[1](gathered)

    merged = None
    for idx, (_, dil) in enumerate(reversed(PATTERNS)):
        merged = _attn_fwd(q, kv, dil, "attn_fwd_d%d" % dil, merged,
                           a_gate if idx == len(PATTERNS) - 1 else None)
    o, lse, y_att = merged
    conv_out, y_conv = _conv_fwd(gates, conv_w, conv_b, ln_g, ln_b)
    dx2, dxb, loss_cols, g_gf = _outproj_loss(x, y_att, y_conv, w_out, gf, target)

    d_o, d_a_gate, delta, dxb_rm = _dy_att(dxb, w_out, a_gate, o)
    g_w_out = jnp.concatenate([_tn_matmul(y_att, [dxb_rm], "gw_out_att")[0],
                               _tn_matmul(y_conv, [dxb], "gw_out_conv")[0]], axis=0)
    acc, out_parts = None, []
    for idx, (_, dil) in enumerate(reversed(PATTERNS)):
        hosted = ex_out(g_w_out) if (idx == 0 and ex_out is not None) else None
        acc, outs = _attn_bwd(q, kv, d_o, lse, delta, dil, acc, idx == len(PATTERNS) - 1, "attn_bwd_d%d" % dil,
                              hosted)
        out_parts += outs
    dq, dkv = acc
    g_q, g_kv, g_a = _tn_matmul(h_rm, [dq, dkv, d_a_gate], "gw_in_att", b_first=True)

    d_c_gate, d_conv, g_ln_g, g_ln_b, g_conv_b = _dy_conv(dxb, w_out, gates, conv_out, ln_g, ln_b)
    dgates, g_conv_w, att_parts = _conv_bwd(d_conv, gates, d_c_gate, conv_w,
                                            ex_att(g_q, g_kv, g_a) if ex_att is not None else None)
    g_c, = _tn_matmul(h, [dgates], "gw_in_conv", b_first=True)
    grad_x, g_norm_g, conv_parts = _dh(
        [(dq, True), (dkv, True), (d_a_gate, True), (dgates, False)], w_in, x, dx2, norm_g,
        ex_conv(g_a, g_c, g_conv_w) if ex_conv is not None else None)
    small = (g_norm_g, g_conv_b, g_ln_g, g_ln_b, g_gf, loss_cols)
    return grad_x, (g_q, g_kv, g_a, g_c), g_w_out, g_conv_w, small, (out_parts, att_parts, conv_parts)


def kernel(x, norm_g, w_in, conv_w, conv_b, conv_ln_g, conv_ln_b, w_out, final_norm_g, loss_target, m_norm_g, m_w_in, m_conv_w, m_conv_b, m_conv_ln_g, m_conv_ln_b, m_w_out, m_final_norm_g, v_norm_g, v_w_in, v_conv_w, v_conv_b, v_conv_ln_g, v_conv_ln_b, v_w_out, v_final_norm_g):
    S, D = x.shape[1], x.shape[2]
    win_sh, wout_sh, cw_sh = w_in[0].T, w_out[0], conv_w[0]
    cols_sh, rows_sh, ch_sh = win_sh.shape[0], wout_sh.shape[0], cw_sh.shape[1]

    win_all, = _gather_two_level([win_sh.astype(BF16)], "gather_w_in")
    w_in_full = win_all.reshape(N_DEV * cols_sh, D)

    def late_weights(gathered):
        wout_all, cw_all = gathered
        conv_w_full = cw_all.transpose(1, 0, 2).reshape(CONV_K, N_DEV * ch_sh)
        return jnp.pad(conv_w_full, ((0, CONV_HALO - CONV_K), (0, 0))), wout_all.reshape(N_DEV * rows_sh, D)

    gf = final_norm_g.reshape(1, D)

    first = -(-(ATT_W + 2 * KV_W) // cols_sh)
    a_off = first * cols_sh - (ATT_W + 2 * KV_W)
    assert 0 <= a_off <= ATT_W

    def pieces(parts, n):
        return jnp.concatenate([p.astype(BF16) for p in parts], axis=0).reshape(n, cols_sh, D)

    def ex_out(g_w_out):
        return _Exchange([g_w_out.reshape(N_DEV, rows_sh, D).astype(BF16)], [(0, N_DEV)])

    same_core = (2, 4, 6)

    def ex_att(g_q, g_kv, g_a):
        mine = _chip_sum(pieces([g_q, g_kv, g_a[:a_off]], first), 0, "rs_att")
        return _Exchange([mine], [(0, first)], [same_core])

    def ex_conv(g_a, g_c, g_conv_w):
        mine = _chip_sum(pieces([g_a[a_off:], g_c], N_DEV - first), first, "rs_conv")
        return _Exchange(
            [mine, g_conv_w[:CONV_K].reshape(CONV_K, N_DEV, ch_sh).transpose(1, 0, 2)],
            [(first, N_DEV), (0, N_DEV)], [same_core, None])

    grad_x, _, _, _, small, parts = _local_step(
        x[0], loss_target[0], norm_g, w_in_full, None, conv_b, conv_ln_g, conv_ln_b, None, gf,
        (ex_out, ex_att, ex_conv), (_Gather([wout_sh.astype(BF16), cw_sh]), late_weights))
    (wout_parts,), (win_parts_lo,), (win_parts_hi, cw_parts) = parts

    small_pack = jnp.concatenate(list(small) + [jnp.zeros((2, D), F32)], axis=0)
    small_parts, = _exchange([small_pack], [None], "gather_small")

    upd_win = _adamw((win_parts_lo, win_parts_hi), win_sh, m_w_in[0].T, v_w_in[0].T, "adamw_w_in",
                     tr=cols_sh // 2, split=first, by_chip=True)
    upd_wout = _adamw(wout_parts, wout_sh, m_w_out[0], v_w_out[0], "adamw_w_out", tr=128)
    upd_cw = _adamw(cw_parts, cw_sh, m_conv_w[0], v_conv_w[0], "adamw_conv_w")
    zeros3 = jnp.zeros((3, D), F32)
    stack = lambda a, b, c, d_, e: jnp.concatenate([a, b, c, d_, e.reshape(1, D), zeros3], axis=0)
    upd_small = _adamw(
        small_parts,
        stack(norm_g, conv_b, conv_ln_g, conv_ln_b, final_norm_g),
        stack(m_norm_g, m_conv_b, m_conv_ln_g, m_conv_ln_b, m_final_norm_g),
        stack(v_norm_g, v_conv_b, v_conv_ln_g, v_conv_ln_b, v_final_norm_g) + jnp.concatenate(
            [jnp.zeros((5, D), F32), jnp.ones((3, D), F32)], axis=0),
        "adamw_small")

    loss = 0.5 / D * jnp.sum(upd_small[0][5])

    def outputs(kind):
        sm = upd_small[kind]
        return [sm[0:1], upd_win[kind].T[None], upd_cw[kind][None], sm[1:2], sm[2:3], sm[3:4],
                upd_wout[kind][None], sm[4]]

    return (loss, grad_x[None], *outputs(0), *outputs(1), *outputs(2), *outputs(3))
```

```python
import jax
import jax.numpy as jnp
from jax import lax
from jax.experimental import pallas as pl
from jax.experimental.pallas import tpu as pltpu

F32 = jnp.float32
BF16 = jnp.bfloat16

HEAD_DIM = 64
N_KV_HEADS = 4
N_Q_HEADS = 16
ATT_W = 1024
KV_W = 256
CONV_K = 31
CONV_HALO = 32
PATTERNS = ((128, 1), (512, 4), (2048, 16))
BLK = 128
LANES = 128
NORM_EPS = 1e-6
LN_EPS = 1e-5
NEG = -1e30
N_DEV = 8
ADAM_LR, ADAM_B1, ADAM_B2, ADAM_EPS, ADAM_WD, ADAM_STEP = 0.001, 0.9, 0.999, 1e-08, 0.01, 10
VMEM_LIMIT = 48 * 1024 * 1024
BIG_VMEM_LIMIT = 58 * 1024 * 1024
SLOPES = tuple(2.0 ** (-8.0 * (h + 1) / N_Q_HEADS) for h in range(N_Q_HEADS))
MESH = pl.DeviceIdType.MESH


def _params(sem, vmem_limit=VMEM_LIMIT):
    return pltpu.CompilerParams(dimension_semantics=sem, vmem_limit_bytes=vmem_limit)


def _sigmoid(v):
    return 1.0 / (1.0 + jnp.exp(-v))


def _silu_and_grad(v):
    s = _sigmoid(v)
    return v * s, s * (1.0 + v * (1.0 - s))


ANY_SPEC = pl.BlockSpec(memory_space=pl.ANY)


def _mesh_pos():
    x, y, c = lax.axis_index("x"), lax.axis_index("y"), lax.axis_index("c")
    return x, y, c, 4 * x + 2 * y + c


def _flipped(k, x, y, c):
    px = 1 - x if k & 4 else x
    py = 1 - y if k & 2 else y
    pc = 1 - c if k & 1 else c
    return (px, py, pc), 4 * px + 2 * py + pc


class _Exchange:
    def __init__(self, arrays, dests, flips=None):
        self.arrays, self.dests, self.n = list(arrays), list(dests), len(arrays)
        self.flips = [tuple(range(1, N_DEV)) if f is None else tuple(f)
                      for f in (flips if flips is not None else [None] * self.n)]

    def out_shapes(self):
        return [jax.ShapeDtypeStruct((N_DEV,) + a.shape[-2:], a.dtype) for a in self.arrays]

    def sem_shapes(self):
        return [pltpu.SemaphoreType.DMA((self.n, N_DEV - 1)), pltpu.SemaphoreType.DMA((self.n, N_DEV - 1)),
                pltpu.SemaphoreType.DMA((self.n,))]

    def _when(self, a, dev, fn):
        if self.dests[a] is None:
            fn()
        else:
            lo, hi = self.dests[a]
            pl.when((dev >= lo) & (dev < hi))(fn)

    def _mine(self, ins, a, dev):
        return ins[a] if self.dests[a] is None else ins[a].at[dev - self.dests[a][0]]

    def _copy(self, ins, outs, sems, a, k, src_dev, slot, target):
        return pltpu.make_async_remote_copy(
            src_ref=self._mine(ins, a, src_dev), dst_ref=outs[a].at[slot],
            send_sem=sems[0].at[a, k - 1], recv_sem=sems[1].at[a, k - 1],
            device_id=target, device_id_type=MESH)

    def start(self, ins, outs, sems):
        x, y, c, me = _mesh_pos()
        for a in range(self.n):
            self._when(a, me, lambda a=a: pltpu.make_async_copy(
                self._mine(ins, a, me), outs[a].at[me], sems[2].at[a]).start())
            for k in self.flips[a]:
                target, peer = _flipped(k, x, y, c)
                self._when(a, peer, lambda a=a, k=k, target=target, peer=peer: self._copy(
                    ins, outs, sems, a, k, peer, me, target).start())

    def finish(self, ins, outs, sems):
        x, y, c, me = _mesh_pos()
        lo0 = [0 if d is None else d[0] for d in self.dests]
        for a in range(self.n):
            for k in self.flips[a]:
                target, peer = _flipped(k, x, y, c)
                self._when(a, me, lambda a=a, k=k, peer=peer: self._copy(
                    ins, outs, sems, a, k, lo0[a], peer, (x, y, c)).wait_recv())
            for k in self.flips[a]:
                target, peer = _flipped(k, x, y, c)
                self._when(a, peer, lambda a=a, k=k, target=target, peer=peer: self._copy(
                    ins, outs, sems, a, k, peer, me, target).wait_send())
            self._when(a, me, lambda a=a: pltpu.make_async_copy(
                self._mine(ins, a, me), outs[a].at[me], sems[2].at[a]).wait())


def _exchange(arrays, dests, name, flips=None):
    ex = _Exchange(arrays, dests, flips)
    na = ex.n

    def body(*refs):
        ins, outs, sems = refs[:na], refs[na:2 * na], refs[2 * na:]
        ex.start(ins, outs, sems)
        ex.finish(ins, outs, sems)

    return pl.pallas_call(
        body, name=name, out_shape=tuple(ex.out_shapes()),
        in_specs=[ANY_SPEC] * na, out_specs=tuple([ANY_SPEC] * na), scratch_shapes=ex.sem_shapes(),
    )(*arrays)


def _chip_sum(pieces, lo, name):
    n, R, C = pieces.shape

    def swap(p_ref, t_ref, send_sems, recv_sems):
        x, y, c, me = _mesh_pos()
        for i in range(n):
            mine = (lo + i) % 2
            cp = pltpu.make_async_remote_copy(
                src_ref=p_ref.at[i], dst_ref=t_ref.at[i], send_sem=send_sems.at[i], recv_sem=recv_sems.at[i],
                device_id=(x, y, 1 - c), device_id_type=MESH)
            pl.when(c != mine)(cp.start)
        for i in range(n):
            mine = (lo + i) % 2
            cp = pltpu.make_async_remote_copy(
                src_ref=p_ref.at[i], dst_ref=t_ref.at[i], send_sem=send_sems.at[i], recv_sem=recv_sems.at[i],
                device_id=(x, y, 1 - c), device_id_type=MESH)
            pl.when(c == mine)(cp.wait_recv)
            pl.when(c != mine)(cp.wait_send)

    other = pl.pallas_call(
        swap, name=name + "_swap", out_shape=jax.ShapeDtypeStruct(pieces.shape, pieces.dtype),
        in_specs=[ANY_SPEC], out_specs=ANY_SPEC,
        scratch_shapes=[pltpu.SemaphoreType.DMA((n,)), pltpu.SemaphoreType.DMA((n,))],
    )(pieces)

    def add(p_ref, t_ref, o_ref):
        o_ref[...] = (p_ref[...].astype(F32) + t_ref[...].astype(F32)).astype(o_ref.dtype)

    tr = R // 2
    blk = pl.BlockSpec((None, tr, C), lambda i, r: (i, r, 0))
    return pl.pallas_call(
        add, name=name + "_add", grid=(n, R // tr), in_specs=[blk, blk], out_specs=blk,
        out_shape=jax.ShapeDtypeStruct(pieces.shape, pieces.dtype),
        compiler_params=_params(("parallel", "parallel")),
    )(pieces, other)


class _Gather:
    def __init__(self, arrays):
        self.arrays, self.n = list(arrays), len(arrays)

    def out_shapes(self):
        return [jax.ShapeDtypeStruct((N_DEV,) + a.shape, a.dtype) for a in self.arrays]

    def sem_shapes(self):
        return [pltpu.SemaphoreType.DMA((self.n, N_DEV - 1)), pltpu.SemaphoreType.DMA((self.n, N_DEV - 1)),
                pltpu.SemaphoreType.DMA((self.n,))]

    def _plan(self, ins, outs, sems):
        x, y, c, me = _mesh_pos()
        chips = [(1 - x, y), (x, 1 - y), (1 - x, 1 - y)]

        def copy(a, k, src, block, to):
            px, py, pc = block
            return pltpu.make_async_remote_copy(
                src_ref=src, dst_ref=outs[a].at[4 * px + 2 * py + pc], send_sem=sems[0].at[a, k],
                recv_sem=sems[1].at[a, k], device_id=to, device_id_type=MESH)

        local = [pltpu.make_async_copy(ins[a], outs[a].at[me], sems[2].at[a]) for a in range(self.n)]
        first = []
        for a in range(self.n):
            first.append(copy(a, 0, ins[a], (x, y, c), (x, y, 1 - c)))
            first += [copy(a, 1 + j, ins[a], (x, y, c), (*chip, c)) for j, chip in enumerate(chips)]
        return (x, y, c), chips, copy, local, first

    def start(self, ins, outs, sems):
        _, _, _, local, first = self._plan(ins, outs, sems)
        for cp in local + first:
            cp.start()

    def finish(self, ins, outs, sems):
        (x, y, c), chips, copy, local, first = self._plan(ins, outs, sems)
        passed = []
        for j, chip in enumerate(chips):
            for a in range(self.n):
                copy(a, 1 + j, ins[a], (*chip, c), (x, y, c)).wait_recv()
                px, py = chip
                fwd = copy(a, 4 + j, outs[a].at[4 * px + 2 * py + c], (*chip, c), (x, y, 1 - c))
                fwd.start()
                passed.append(fwd)
        for a in range(self.n):
            copy(a, 0, ins[a], (x, y, 1 - c), (x, y, c)).wait_recv()
            for j, chip in enumerate(chips):
                copy(a, 4 + j, ins[a], (*chip, 1 - c), (x, y, c)).wait_recv()
        for cp in first + passed:
            cp.wait_send()
        for cp in local:
            cp.wait()


def _gather_two_level(arrays, name):
    ga = _Gather(arrays)
    na = ga.n

    def body(*refs):
        ins, outs, sems = refs[:na], refs[na:2 * na], refs[2 * na:]
        ga.start(ins, outs, sems)
        ga.finish(ins, outs, sems)

    return pl.pallas_call(
        body, name=name, out_shape=tuple(ga.out_shapes()),
        in_specs=[ANY_SPEC] * na, out_specs=tuple([ANY_SPEC] * na), scratch_shapes=ga.sem_shapes(),
    )(*arrays)


CHUNK = 128
RESIDUES = 16
PER_RES = CHUNK // RESIDUES


def _perm_rows(tile, inverse):
    a = lax.broadcasted_iota(jnp.int32, (CHUNK, CHUNK), 0)
    b = lax.broadcasted_iota(jnp.int32, (CHUNK, CHUNK), 1)
    if inverse:
        a, b = b, a
    p = jnp.where(a == PER_RES * (b % RESIDUES) + b // RESIDUES, 1.0, 0.0).astype(BF16)
    parts = [jnp.dot(p, tile[c * CHUNK:(c + 1) * CHUNK], preferred_element_type=F32)
             for c in range(tile.shape[0] // CHUNK)]
    return jnp.concatenate(parts, axis=0).astype(BF16)


class _Rows:
    def __init__(self, dil, S):
        nc = S // CHUNK
        self.dil = dil
        if dil == 1:
            self.view, self.block, self.nb = (nc, CHUNK), (None, CHUNK), nc
            self.index = lambda r, b: (b, 0, 0)
        elif dil == 4:
            self.view, self.block, self.nb = (nc, 4, 4, PER_RES), (4, 4, None, PER_RES), nc // 4
            self.index = lambda r, b: (b, 0, r, 0, 0)
        elif dil == RESIDUES:
            self.view, self.block, self.nb = (nc, RESIDUES, PER_RES), (RESIDUES, None, PER_RES), nc // RESIDUES
            self.index = lambda r, b: (b, r, 0, 0)
        else:
            raise NotImplementedError(dil)

    def of(self, a):
        return a.reshape(self.view + (a.shape[-1],))

    def spec(self, width, which_block):
        return pl.BlockSpec(self.block + (width,), lambda r, n: self.index(r, which_block(n)))

    def pos(self, row):
        if self.dil == 1:
            return (row % PER_RES) * RESIDUES + row // PER_RES
        if self.dil == 4:
            return (row // 32) * 32 + (row % PER_RES) * 4 + (row % 32) // PER_RES
        return row


def _ld(ref, cols=slice(None)):
    v = ref[(slice(None),) * (len(ref.shape) - 1) + (cols,)]
    return v.reshape(BLK, v.shape[-1])


def _st(ref, val, cols=slice(None)):
    ref[(slice(None),) * (len(ref.shape) - 1) + (cols,)] = val.reshape(ref.shape[:-1] + (val.shape[-1],))


def _inproj(x, g, w_t, segments, hosted=None, tm=1024, tn=512):
    S, D = x.shape
    ns = len(segments)
    ni = S // tm
    counts = [nc // tn for nc, _, _ in segments]
    starts = [sum(counts[:s]) for s in range(ns)]

    hn = hosted.n if hosted is not None else 0
    last_p = sum(counts)

    def body(x_ref, g_ref, w_ref, *rest):
        h_ins, rest = rest[:hn], rest[hn:]
        outs = rest[:ns]
        hrm_out, h_out = rest[ns:ns + 2]
        h_outs = rest[ns + 2:ns + 2 + hn]
        hrm_scr, h_scr = rest[ns + 2 + hn:ns + 4 + hn]
        h_sems = rest[ns + 4 + hn:]
        p, i = pl.program_id(0), pl.program_id(1)

        if hosted is not None:
            @pl.when((p == 0) & (i == 0))
            def _():
                hosted.start(h_ins, h_outs, h_sems)

            @pl.when((p == last_p) & (i == ni - 1))
            def _():
                hosted.finish(h_ins, h_outs, h_sems)

        @pl.when(p == 0)
        def _():
            xf = x_ref[...]
            r = lax.rsqrt(jnp.mean(xf * xf, axis=-1, keepdims=True) + NORM_EPS)
            h = (xf * r * g_ref[...]).astype(BF16)
            hrm = _perm_rows(h, False)
            h_scr[i] = h
            hrm_scr[i] = hrm
            h_out[...] = h
            hrm_out[...] = hrm

        for s, (_, scale, rm) in enumerate(segments):
            @pl.when((p > starts[s]) & (p <= starts[s] + counts[s]))
            def _(s=s, scale=scale, rm=rm):
                acc = _nt((hrm_scr if rm else h_scr)[i], w_ref[...])
                outs[s][...] = acc * scale if scale != 1.0 else acc

    def out_index(s):
        def index(p, i):
            j = p - 1 - starts[s]
            row = jnp.where(j < 0, 0, jnp.where(j >= counts[s], ni - 1, i))
            return row, jnp.clip(j, 0, counts[s] - 1)
        return index

    first_pass = pl.BlockSpec((tm, D), lambda p, i: (jnp.where(p == 0, i, ni - 1), 0))
    out_specs = [pl.BlockSpec((tm, tn), out_index(s)) for s in range(ns)]
    out_shape = [jax.ShapeDtypeStruct((S, nc), F32) for nc, _, _ in segments]
    in_specs = [first_pass, pl.BlockSpec((1, D), lambda p, i: (0, 0)),
                pl.BlockSpec((tn, D), lambda p, i: (jnp.maximum(p - 1, 0), 0))]
    args = [x, g, w_t]
    out_specs = out_specs + [first_pass, first_pass]
    out_shape = out_shape + [jax.ShapeDtypeStruct((S, D), BF16)] * 2
    scratch = [pltpu.VMEM((ni, tm, D), BF16), pltpu.VMEM((ni, tm, D), BF16)]
    if hosted is not None:
        in_specs += [ANY_SPEC] * hn
        args += hosted.arrays
        out_specs += [ANY_SPEC] * hn
        out_shape += hosted.out_shapes()
        scratch += hosted.sem_shapes()
    return pl.pallas_call(
        body, name="inproj", grid=(1 + last_p, ni),
        in_specs=in_specs, out_specs=tuple(out_specs), out_shape=tuple(out_shape), scratch_shapes=scratch,
        compiler_params=_params(("arbitrary", "arbitrary"), BIG_VMEM_LIMIT),
    )(*args)


def _fill_bias_table(tbl, rows, keys_first=False):
    shape = (2 * BLK, BLK) if keys_first else (BLK, 2 * BLK)
    qi = lax.broadcasted_iota(jnp.int32, shape, 1 if keys_first else 0)
    kj = lax.broadcasted_iota(jnp.int32, shape, 0 if keys_first else 1)
    dist = rows.pos(qi) - rows.pos(kj % BLK) + jnp.where(kj < BLK, BLK, 0)
    inside = (dist >= 0) & (dist <= BLK)
    negd = (dist * (-rows.dil)).astype(F32)
    for f, valid in enumerate((inside & (kj >= BLK), inside)):
        for h in range(N_Q_HEADS):
            tbl[f * N_Q_HEADS + h] = jnp.where(valid, SLOPES[h] * negd, NEG)


def _bias2(tbl, n, h0, h1, axis=0):
    base = jnp.where(n == 0, 0, N_Q_HEADS)
    return jnp.concatenate([tbl[base + h0], tbl[base + h1]], axis=axis)


def _head_operands(kv2, hk, lo_mask):
    half, pos = hk // 2, hk % 2
    out = []
    for base in (0, KV_W):
        t = kv2[:, base + half * LANES: base + (half + 1) * LANES]
        sw = pltpu.roll(t, HEAD_DIM, axis=1)
        at_lo, at_hi = (t, sw) if pos == 0 else (sw, t)
        out.append(jnp.where(lo_mask, at_lo, 0.0).astype(BF16))
        out.append(jnp.where(lo_mask, 0.0, at_hi).astype(BF16))
    return out


def _nt(a, b):
    return lax.dot_general(a, b, (((1,), (1,)), ((), ())), preferred_element_type=F32)


def _tn(a, b):
    return lax.dot_general(a, b, (((0,), (0,)), ((), ())), preferred_element_type=F32)


def _attn_fwd(q, kv, dil, name, prev=None, gate=None):
    S = q.shape[0]
    rows = _Rows(dil, S)
    nb = rows.nb
    have_prev, last = prev is not None, gate is not None

    def body(*refs):
        refs = list(refs)
        q_ref, kvc_ref, kvp_ref = refs[:3]
        del refs[:3]
        if have_prev:
            po_ref, pl_ref = refs[:2]
            del refs[:2]
        if last:
            gate_ref = refs.pop(0)
        o_ref, lse_ref = refs[:2]
        y_ref = refs[2] if last else None
        tbl = refs[-1]
        n = pl.program_id(1)

        @pl.when((pl.program_id(0) == 0) & (n == 0))
        def _():
            _fill_bias_table(tbl, rows)

        kv2 = jnp.concatenate([_ld(kvp_ref), _ld(kvc_ref)], axis=0)
        lo_mask = lax.broadcasted_iota(jnp.int32, (2 * BLK, LANES), 1) < HEAD_DIM
        lane = lax.broadcasted_iota(jnp.int32, (BLK, LANES), 1)
        stats = jnp.zeros((BLK, LANES), F32)
        for hk in range(N_KV_HEADS):
            k_lo, k_hi, v_lo, v_hi = _head_operands(kv2, hk, lo_mask)
            cols = [slice(b * LANES, (b + 1) * LANES) for b in (2 * hk, 2 * hk + 1)]
            q2 = jnp.concatenate([_ld(q_ref, cols[0]), _ld(q_ref, cols[1])], axis=0).astype(BF16)
            o2 = jnp.zeros((2 * BLK, LANES), F32)
            for which, (kk, vv) in enumerate(((k_lo, v_lo), (k_hi, v_hi))):
                h0, h1 = 4 * hk + which, 4 * hk + 2 + which
                s = _nt(q2, kk) + _bias2(tbl, n, h0, h1)
                m = jnp.max(s, axis=1, keepdims=True)
                p = jnp.exp(s - m)
                l = jnp.sum(p, axis=1, keepdims=True)
                o2 = o2 + jnp.dot(p.astype(BF16), vv, preferred_element_type=F32) * (1.0 / l)
                lse = m + jnp.log(l)
                stats = jnp.where(lane == h0, lse[0:BLK], stats)
                stats = jnp.where(lane == h1, lse[BLK:], stats)
            _st(o_ref, o2[0:BLK], cols[0])
            _st(o_ref, o2[BLK:], cols[1])
        if have_prev:
            before = _ld(pl_ref)
            top = jnp.maximum(before, stats)
            e_old, e_new = jnp.exp(before - top), jnp.exp(stats - top)
            total = e_old + e_new
            stats = top + jnp.log(total)
            inv = 1.0 / total
            w_old, w_new = e_old * inv, e_new * inv
        if have_prev or last:
            lo = lane < HEAD_DIM
            for blk in range(ATT_W // LANES):
                cols = slice(blk * LANES, (blk + 1) * LANES)
                o_blk = _ld(o_ref, cols)
                if have_prev:
                    pick = lambda w: jnp.where(lo, w[:, 2 * blk:2 * blk + 1], w[:, 2 * blk + 1:2 * blk + 2])
                    o_blk = o_blk * pick(w_new) + _ld(po_ref, cols) * pick(w_old)
                    _st(o_ref, o_blk, cols)
                if last:
                    a = _ld(gate_ref, cols)
                    _st(y_ref, (o_blk * (a * _sigmoid(a))).astype(BF16), cols)
        _st(lse_ref, stats)

    here = lambda n: n
    before_n = lambda n: jnp.maximum(n - 1, 0)
    in_specs = [rows.spec(ATT_W, here), rows.spec(2 * KV_W, here), rows.spec(2 * KV_W, before_n)]
    args = [rows.of(q), rows.of(kv), rows.of(kv)]
    if have_prev:
        in_specs += [rows.spec(ATT_W, here), rows.spec(LANES, here)]
        args += [rows.of(prev[0]), rows.of(prev[1])]
    out_specs = [rows.spec(ATT_W, here), rows.spec(LANES, here)]
    out_shape = [jax.ShapeDtypeStruct(rows.view + (ATT_W,), F32), jax.ShapeDtypeStruct(rows.view + (LANES,), F32)]
    if last:
        in_specs.append(rows.spec(ATT_W, here))
        args.append(rows.of(gate))
        out_specs.append(rows.spec(ATT_W, here))
        out_shape.append(jax.ShapeDtypeStruct(rows.view + (ATT_W,), BF16))
    res = pl.pallas_call(
        body, name=name, grid=(dil, nb),
        in_specs=in_specs, out_specs=tuple(out_specs), out_shape=tuple(out_shape),
        scratch_shapes=[pltpu.VMEM((2 * N_Q_HEADS, BLK, 2 * BLK), F32)],
        compiler_params=_params(("arbitrary", "arbitrary")),
    )(*args)
    return tuple(r.reshape(S, r.shape[-1]) for r in res)


def _shifted_copies(buf, phases):
    n = phases.shape[1]
    for b in range(1, 8):
        phases[b - 1] = buf[b:b + n, :]


def _window(buf, phases, start, cols):
    b = start % 8
    if b == 0:
        return buf[start:start + 8, cols]
    return phases[b - 1, start - b:start - b + 8, cols]


def _broadcast_taps(w_ref, wb):
    for j in range(CONV_K):
        wb[j] = jnp.broadcast_to(w_ref[j:j + 1, :], wb.shape[1:])


def _conv_fwd(gates, conv_w, conv_b, ln_g, ln_b, tt=256):
    S = gates.shape[0]
    C = conv_w.shape[1]
    hb = tt // CONV_HALO

    def body(val_ref, glu_ref, hval_ref, hglu_ref, gate_ref, w_ref, b_ref, g_ref, beta_ref,
             conv_ref, y_ref, hbuf, hph):
        i = pl.program_id(0)
        halo = hval_ref[...] * _sigmoid(hglu_ref[...])
        hbuf[0:CONV_HALO, :] = jnp.where(i > 0, halo, 0.0)
        hbuf[CONV_HALO:, :] = val_ref[...] * _sigmoid(glu_ref[...])
        _shifted_copies(hbuf, hph)
        for cb in range(C // LANES):
            cols = slice(cb * LANES, (cb + 1) * LANES)
            wj = [jnp.broadcast_to(w_ref[j:j + 1, cols], (8, LANES)) for j in range(CONV_K)]
            for rc in range(tt // 8):
                acc = jnp.zeros((8, LANES), F32)
                for j in range(CONV_K):
                    start = rc * 8 + CONV_HALO - (CONV_K - 1) + j
                    acc = acc + _window(hbuf, hph, start, cols) * wj[j]
                conv_ref[rc * 8:(rc + 1) * 8, cols] = acc
        cv = conv_ref[...] + b_ref[...]
        conv_ref[...] = cv
        mu = jnp.mean(cv, axis=-1, keepdims=True)
        xc = cv - mu
        var = jnp.mean(xc * xc, axis=-1, keepdims=True)
        ln = xc * lax.rsqrt(var + LN_EPS) * g_ref[...] + beta_ref[...]
        gt = gate_ref[...]
        y_ref[...] = (ln * _sigmoid(ln) * (gt * _sigmoid(gt))).astype(BF16)

    vec = pl.BlockSpec((1, C), lambda i: (0, 0))
    return pl.pallas_call(
        body, name="conv_fwd", grid=(S // tt,),
        in_specs=[pl.BlockSpec((tt, C), lambda i: (i, 0)),
                  pl.BlockSpec((tt, C), lambda i: (i, 1)),
                  pl.BlockSpec((CONV_HALO, C), lambda i: (jnp.maximum(i * hb - 1, 0), 0)),
                  pl.BlockSpec((CONV_HALO, C), lambda i: (jnp.maximum(i * hb - 1, 0), 1)),
                  pl.BlockSpec((tt, C), lambda i: (i, 2)),
                  pl.BlockSpec((CONV_HALO, C), lambda i: (0, 0)), vec, vec, vec],
        out_specs=(pl.BlockSpec((tt, C), lambda i: (i, 0)), pl.BlockSpec((tt, C), lambda i: (i, 0))),
        out_shape=(jax.ShapeDtypeStruct((S, C), F32), jax.ShapeDtypeStruct((S, C), BF16)),
        scratch_shapes=[pltpu.VMEM((tt + CONV_HALO, C), F32), pltpu.VMEM((7, tt + CONV_HALO - 8, C), F32)],
        compiler_params=_params(("parallel",)),
    )(gates, gates, gates, gates, gates, conv_w, conv_b, ln_g, ln_b)


def _outproj_loss(x, y_att, y_conv, w_out, gf, target, tm=512):
    S, D = x.shape
    E = y_att.shape[1]

    def body(x_ref, ya_ref, yc_ref, w_ref, gf_ref, t_ref, dx_ref, dxb_ref, loss_ref, ggf_ref):
        @pl.when(pl.program_id(0) == 0)
        def _():
            loss_ref[...] = jnp.zeros_like(loss_ref)
            ggf_ref[...] = jnp.zeros_like(ggf_ref)

        x2 = (x_ref[...] + jnp.dot(_perm_rows(ya_ref[...], True), w_ref[0:E, :], preferred_element_type=F32)
              + jnp.dot(yc_ref[...], w_ref[E:, :], preferred_element_type=F32))
        r = lax.rsqrt(jnp.mean(x2 * x2, axis=-1, keepdims=True) + NORM_EPS)
        nrm = x2 * r
        gfv = gf_ref[...]
        err = nrm * gfv - t_ref[...]
        loss_ref[...] += jnp.sum(err * err, axis=0, keepdims=True)
        dout = err * (1.0 / D)
        ggf_ref[...] += jnp.sum(dout * nrm, axis=0, keepdims=True)
        dn = dout * gfv
        dx2 = r * (dn - nrm * jnp.mean(dn * nrm, axis=-1, keepdims=True))
        dx_ref[...] = dx2
        dxb_ref[...] = dx2.astype(BF16)

    row = lambda w: pl.BlockSpec((tm, w), lambda i: (i, 0))
    vec = pl.BlockSpec((1, D), lambda i: (0, 0))
    return pl.pallas_call(
        body, name="outproj_loss", grid=(S // tm,),
        in_specs=[row(D), row(E), row(E), pl.BlockSpec((2 * E, D), lambda i: (0, 0)), vec, row(D)],
        out_specs=(row(D), row(D), vec, vec),
        out_shape=(jax.ShapeDtypeStruct((S, D), F32), jax.ShapeDtypeStruct((S, D), BF16),
                   jax.ShapeDtypeStruct((1, D), F32), jax.ShapeDtypeStruct((1, D), F32)),
        compiler_params=_params(("arbitrary",)),
    )(x, y_att, y_conv, w_out, gf, target)


def _split3(v):
    hi = v.astype(BF16)
    r1 = v - hi.astype(F32)
    mid = r1.astype(BF16)
    lo = (r1 - mid.astype(F32)).astype(BF16)
    return hi, mid, lo


def _dy_att(dxb, w_out, gates, o, tm=512):
    S, D = dxb.shape
    E = ATT_W

    def body(dx_ref, w_ref, a_ref, o_ref, do_ref, da_ref, dl_ref, dxr_ref):
        dxr = _perm_rows(dx_ref[...], False)
        dxr_ref[...] = dxr
        dya = _nt(dxr, w_ref[...])
        a = a_ref[...]
        ov = o_ref[...]
        sl, dsl = _silu_and_grad(a)
        d_o = dya * sl
        do_ref[...] = d_o
        da_ref[...] = (dya * ov * dsl).astype(BF16)
        ci = lax.broadcasted_iota(jnp.int32, (E, LANES), 0) // HEAD_DIM
        hi = lax.broadcasted_iota(jnp.int32, (E, LANES), 1)
        sel = jnp.where(ci == hi, 1.0, 0.0).astype(BF16)
        acc = jnp.zeros((tm, LANES), F32)
        for part in _split3(d_o * ov):
            acc = acc + jnp.dot(part, sel, preferred_element_type=F32)
        dl_ref[...] = acc

    row = lambda w: pl.BlockSpec((tm, w), lambda i: (i, 0))
    return pl.pallas_call(
        body, name="dy_att", grid=(S // tm,),
        in_specs=[row(D), pl.BlockSpec((E, D), lambda i: (0, 0)), row(E), row(E)],
        out_specs=(row(E), row(E), row(LANES), row(D)),
        out_shape=(jax.ShapeDtypeStruct((S, E), F32), jax.ShapeDtypeStruct((S, E), BF16),
                   jax.ShapeDtypeStruct((S, LANES), F32), jax.ShapeDtypeStruct((S, D), BF16)),
        compiler_params=_params(("parallel",)),
    )(dxb, w_out, gates, o)


def _dy_conv(dxb, w_out, gates, conv_out, ln_g, ln_b, tm=512):
    S, D = dxb.shape
    C = conv_out.shape[1]

    def body(dx_ref, w_ref, gate_ref, cv_ref, g_ref, beta_ref, dgate_ref, dconv_ref, gg_ref, gb_ref, gcb_ref):
        @pl.when(pl.program_id(0) == 0)
        def _():
            gg_ref[...] = jnp.zeros_like(gg_ref)
            gb_ref[...] = jnp.zeros_like(gb_ref)
            gcb_ref[...] = jnp.zeros_like(gcb_ref)

        dyc = _nt(dx_ref[...], w_ref[...])
        cv = cv_ref[...]
        mu = jnp.mean(cv, axis=-1, keepdims=True)
        xc = cv - mu
        rstd = lax.rsqrt(jnp.mean(xc * xc, axis=-1, keepdims=True) + LN_EPS)
        nrm = xc * rstd
        gv = g_ref[...]
        ln = nrm * gv + beta_ref[...]
        u, du = _silu_and_grad(ln)
        gt = gate_ref[...]
        g2, dg2 = _silu_and_grad(gt)
        dgate_ref[...] = (dyc * u * dg2).astype(BF16)
        d_ln = dyc * g2 * du
        gb_ref[...] += jnp.sum(d_ln, axis=0, keepdims=True)
        gg_ref[...] += jnp.sum(d_ln * nrm, axis=0, keepdims=True)
        dn = d_ln * gv
        d_conv = rstd * (dn - jnp.mean(dn, axis=-1, keepdims=True)
                         - nrm * jnp.mean(dn * nrm, axis=-1, keepdims=True))
        dconv_ref[...] = d_conv
        gcb_ref[...] += jnp.sum(d_conv, axis=0, keepdims=True)

    row = lambda w: pl.BlockSpec((tm, w), lambda i: (i, 0))
    vec = pl.BlockSpec((1, C), lambda i: (0, 0))
    return pl.pallas_call(
        body, name="dy_conv", grid=(S // tm,),
        in_specs=[row(D), pl.BlockSpec((C, D), lambda i: (1, 0)),
                  pl.BlockSpec((tm, C), lambda i: (i, 2)), row(C), vec, vec],
        out_specs=(row(C), row(C), vec, vec, vec),
        out_shape=(jax.ShapeDtypeStruct((S, C), BF16), jax.ShapeDtypeStruct((S, C), F32),
                   jax.ShapeDtypeStruct((1, C), F32), jax.ShapeDtypeStruct((1, C), F32),
                   jax.ShapeDtypeStruct((1, C), F32)),
        compiler_params=_params(("arbitrary",)),
    )(dxb, w_out, gates, conv_out, ln_g, ln_b)


def _conv_bwd(d_conv, gates, d_c_gate, conv_w, hosted=None, tt=256):
    S, C = d_conv.shape
    hb = tt // CONV_HALO
    nt = S // tt
    hn = hosted.n if hosted is not None else 0

    def body(*refs):
        dc_ref, dnext_ref, val_ref, glu_ref, dg_ref, w_ref = refs[:6]
        h_ins = refs[6:6 + hn]
        out_ref, gw_ref = refs[6 + hn:8 + hn]
        h_outs = refs[8 + hn:8 + 2 * hn]
        hbuf, dbuf, dhbuf, dph, wb = refs[8 + 2 * hn:13 + 2 * hn]
        h_sems = refs[13 + 2 * hn:]
        i = pl.program_id(0)

        @pl.when(i == 0)
        def _():
            gw_ref[...] = jnp.zeros_like(gw_ref)
            _broadcast_taps(w_ref, wb)
            if hosted is not None:
                hosted.start(h_ins, h_outs, h_sems)

        val = val_ref[...]
        sg = _sigmoid(glu_ref[...])
        hbuf[...] = val * sg
        dbuf[0:tt, :] = dc_ref[...]
        dbuf[tt:, :] = jnp.where(i < nt - 1, dnext_ref[...], 0.0)
        _shifted_copies(dbuf, dph)
        for cb in range(C // LANES):
            cols = slice(cb * LANES, (cb + 1) * LANES)
            gacc = [jnp.zeros((8, LANES), F32) for _ in range(CONV_K)]
            group = 2
            for rc0 in range(0, tt // 8, group):
                hcur = [hbuf[(rc0 + r) * 8:(rc0 + r + 1) * 8, cols] for r in range(group)]
                accs = [jnp.zeros((8, LANES), F32) for _ in range(group)]
                for j in range(CONV_K):
                    wj = wb[j, :, cols]
                    for r in range(group):
                        dwin = _window(dbuf, dph, (rc0 + r) * 8 + (CONV_K - 1) - j, cols)
                        accs[r] = accs[r] + dwin * wj
                        gacc[j] = gacc[j] + dwin * hcur[r]
                for r in range(group):
                    dhbuf[(rc0 + r) * 8:(rc0 + r + 1) * 8, cols] = accs[r]
            for j in range(CONV_K):
                gw_ref[j:j + 1, cols] += jnp.sum(gacc[j], axis=0, keepdims=True)
        d_h = dhbuf[...]
        out_ref[:, 0:C] = (d_h * sg).astype(BF16)
        out_ref[:, C:2 * C] = (d_h * val * sg * (1.0 - sg)).astype(BF16)
        out_ref[:, 2 * C:3 * C] = dg_ref[...]

        if hosted is not None:
            @pl.when(i == nt - 1)
            def _():
                hosted.finish(h_ins, h_outs, h_sems)

    tile = lambda col: pl.BlockSpec((tt, C), lambda i: (i, col))
    in_specs = [tile(0),
                pl.BlockSpec((CONV_HALO, C), lambda i: (jnp.minimum((i + 1) * hb, S // CONV_HALO - 1), 0)),
                tile(0), tile(1), tile(0),
                pl.BlockSpec((CONV_HALO, C), lambda i: (0, 0))]
    args = [d_conv, d_conv, gates, gates, d_c_gate, conv_w]
    out_specs = [pl.BlockSpec((tt, 3 * C), lambda i: (i, 0)), pl.BlockSpec((CONV_HALO, C), lambda i: (0, 0))]
    out_shape = [jax.ShapeDtypeStruct((S, 3 * C), BF16), jax.ShapeDtypeStruct((CONV_HALO, C), F32)]
    scratch = [pltpu.VMEM((tt, C), F32), pltpu.VMEM((tt + CONV_HALO, C), F32), pltpu.VMEM((tt, C), F32),
               pltpu.VMEM((7, tt + CONV_HALO - 8, C), F32), pltpu.VMEM((CONV_K, 8, C), F32)]
    if hosted is not None:
        in_specs += [ANY_SPEC] * hn
        args += hosted.arrays
        out_specs += [ANY_SPEC] * hn
        out_shape += hosted.out_shapes()
        scratch += hosted.sem_shapes()
    res = pl.pallas_call(
        body, name="conv_bwd", grid=(nt,),
        in_specs=in_specs, out_specs=tuple(out_specs), out_shape=tuple(out_shape), scratch_shapes=scratch,
        compiler_params=_params(("arbitrary",)),
    )(*args)
    return res[0], res[1], list(res[2:])


def _attn_bwd(q, kv, d_o, lse, delta, dil, prev, final, name, hosted=None):
    S = q.shape[0]
    rows = _Rows(dil, S)
    nb = rows.nb
    steps = dil * nb
    out_dt = BF16 if final else F32
    have_prev = prev is not None
    hn = hosted.n if hosted is not None else 0

    def body(*refs):
        refs = list(refs)
        q_ref, do_ref, lse_ref, dl_ref, kvc_ref, kvp_ref = refs[:6]
        del refs[:6]
        if have_prev:
            pdq_ref, pdkv_ref = refs[:2]
            del refs[:2]
        h_ins = refs[:hn]
        dq_ref, dkv_ref = refs[hn:hn + 2]
        h_outs = refs[hn + 2:2 * hn + 2]
        carry, tbl = refs[2 * hn + 2:2 * hn + 4]
        h_sems = refs[2 * hn + 4:]
        t = pl.program_id(0)
        n = t % nb

        @pl.when(t == 0)
        def _():
            if hosted is not None:
                hosted.start(h_ins, h_outs, h_sems)
            _fill_bias_table(tbl, rows, keys_first=True)
            carry[...] = jnp.zeros_like(carry)

        @pl.when(t < steps)
        def _():
            kv2 = jnp.concatenate([_ld(kvp_ref), _ld(kvc_ref)], axis=0)
            lse_t, dl_t = _ld(lse_ref).T, _ld(dl_ref).T
            lo_mask = lax.broadcasted_iota(jnp.int32, (2 * BLK, LANES), 1) < HEAD_DIM
            halves = [jnp.zeros((2 * BLK, LANES), F32) for _ in range(4)]
            for hk in range(N_KV_HEADS):
                k_lo, k_hi, v_lo, v_hi = _head_operands(kv2, hk, lo_mask)
                cols = [slice(b * LANES, (b + 1) * LANES) for b in (2 * hk, 2 * hk + 1)]
                q2 = jnp.concatenate([_ld(q_ref, cols[0]), _ld(q_ref, cols[1])], axis=0).astype(BF16)
                do2 = jnp.concatenate([_ld(do_ref, cols[0]), _ld(do_ref, cols[1])], axis=0).astype(BF16)
                dq2 = jnp.zeros((2 * BLK, LANES), F32)
                dks, dvs = [], []
                for which, (kk, vv) in enumerate(((k_lo, v_lo), (k_hi, v_hi))):
                    h0, h1 = 4 * hk + which, 4 * hk + 2 + which
                    s = _nt(kk, q2) + _bias2(tbl, n, h0, h1, axis=1)
                    lse2 = jnp.concatenate([lse_t[h0:h0 + 1, :], lse_t[h1:h1 + 1, :]], axis=1)
                    dl2 = jnp.concatenate([dl_t[h0:h0 + 1, :], dl_t[h1:h1 + 1, :]], axis=1)
                    p = jnp.exp(s - lse2)
                    ds = (p * (_nt(vv, do2) - dl2)).astype(BF16)
                    dq2 = dq2 + _tn(ds, kk)
                    dks.append(jnp.dot(ds, q2, preferred_element_type=F32))
                    dvs.append(jnp.dot(p.astype(BF16), do2, preferred_element_type=F32))
                dk_sum = jnp.where(lo_mask, dks[0], dks[1])
                dv_sum = jnp.where(lo_mask, dvs[0], dvs[1])
                for jp in range(2):
                    dq_blk = dq2[jp * BLK:(jp + 1) * BLK]
                    if have_prev:
                        dq_blk = dq_blk + _ld(pdq_ref, cols[jp])
                    if final:
                        dq_blk = dq_blk * (HEAD_DIM ** -0.5)
                    _st(dq_ref, dq_blk.astype(out_dt), cols[jp])
                half, pos = hk // 2, hk % 2
                here = lo_mask if pos == 0 else jnp.logical_not(lo_mask)
                dk_tot = dk_sum + pltpu.roll(dk_sum, HEAD_DIM, axis=1)
                dv_tot = dv_sum + pltpu.roll(dv_sum, HEAD_DIM, axis=1)
                halves[half] = halves[half] + jnp.where(here, dk_tot, 0.0)
                halves[2 + half] = halves[2 + half] + jnp.where(here, dv_tot, 0.0)
            for b in range(4):
                cols = slice(b * LANES, (b + 1) * LANES)
                done = carry[:, cols] + halves[b][0:BLK, :]
                if have_prev:
                    done = done + _ld(pdkv_ref, cols)
                _st(dkv_ref, done.astype(out_dt), cols)
                carry[:, cols] = halves[b][BLK:, :]

        @pl.when(t == steps)
        def _():
            done = carry[...]
            if have_prev:
                done = done + _ld(pdkv_ref)
            _st(dkv_ref, done.astype(out_dt))
            if hosted is not None:
                hosted.finish(h_ins, h_outs, h_sems)

    def spec(width, lag):
        def index(t):
            u = jnp.clip(t - lag, 0, steps - 1)
            return rows.index(u // nb, u % nb)
        return pl.BlockSpec(rows.block + (width,), index)

    def key_prev(t):
        u = jnp.minimum(t, steps - 1)
        return rows.index(u // nb, jnp.maximum(u % nb - 1, 0))

    in_specs = [spec(ATT_W, 0), spec(ATT_W, 0), spec(LANES, 0), spec(LANES, 0), spec(2 * KV_W, 0),
                pl.BlockSpec(rows.block + (2 * KV_W,), key_prev)]
    args = [rows.of(q), rows.of(d_o), rows.of(lse), rows.of(delta), rows.of(kv), rows.of(kv)]
    if have_prev:
        in_specs += [spec(ATT_W, 0), spec(2 * KV_W, 1)]
        args += [rows.of(prev[0]), rows.of(prev[1])]
    out_specs = [spec(ATT_W, 0), spec(2 * KV_W, 1)]
    out_shape = [jax.ShapeDtypeStruct(rows.view + (ATT_W,), out_dt),
                 jax.ShapeDtypeStruct(rows.view + (2 * KV_W,), out_dt)]
    scratch = [pltpu.VMEM((BLK, 2 * KV_W), F32), pltpu.VMEM((2 * N_Q_HEADS, 2 * BLK, BLK), F32)]
    if hosted is not None:
        in_specs += [ANY_SPEC] * hn
        args += hosted.arrays
        out_specs += [ANY_SPEC] * hn
        out_shape += hosted.out_shapes()
        scratch += hosted.sem_shapes()
    res = pl.pallas_call(
        body, name=name, grid=(steps + 1,),
        in_specs=in_specs, out_specs=tuple(out_specs), out_shape=tuple(out_shape), scratch_shapes=scratch,
        compiler_params=_params(("arbitrary",)),
    )(*args)
    return (res[0].reshape(S, ATT_W), res[1].reshape(S, 2 * KV_W)), list(res[2:])


def _dh(segments, w_in, x, dx2, g, hosted=None, tm=1024, tk=512):
    S, D = x.shape
    ns = len(segments)
    counts = [a.shape[1] // tk for a, _ in segments]
    starts = [sum(counts[:s]) for s in range(ns)]
    nk = sum(counts)
    hn = hosted.n if hosted is not None else 0

    def body(*refs):
        seg_refs = refs[:ns]
        w_ref, x_ref, dx2_ref, g_ref = refs[ns:ns + 4]
        h_ins = refs[ns + 4:ns + 4 + hn]
        gx_ref, gng_ref = refs[ns + 4 + hn:ns + 6 + hn]
        h_outs = refs[ns + 6 + hn:ns + 6 + 2 * hn]
        acc = refs[ns + 6 + 2 * hn]
        h_sems = refs[ns + 7 + 2 * hn:]
        k, i = pl.program_id(0), pl.program_id(1)

        @pl.when((i == 0) & (k == 0))
        def _():
            gng_ref[...] = jnp.zeros_like(gng_ref)
            if hosted is not None:
                hosted.start(h_ins, h_outs, h_sems)

        @pl.when(k == 0)
        def _():
            acc[i] = jnp.zeros(acc.shape[1:], F32)

        for s in range(ns):
            @pl.when((k >= starts[s]) & (k < starts[s] + counts[s]))
            def _(s=s):
                t = seg_refs[s][...]
                if segments[s][1]:
                    t = _perm_rows(t, True)
                acc[i] += jnp.dot(t, w_ref[...], preferred_element_type=F32)

        @pl.when(k == nk - 1)
        def _():
            dh = acc[i]
            xf = x_ref[...]
            r = lax.rsqrt(jnp.mean(xf * xf, axis=-1, keepdims=True) + NORM_EPS)
            nrm = xf * r
            gng_ref[...] += jnp.sum(dh * nrm, axis=0, keepdims=True)
            dn = dh * g_ref[...]
            gx_ref[...] = dx2_ref[...] + r * (dn - nrm * jnp.mean(dn * nrm, axis=-1, keepdims=True))

        if hosted is not None:
            @pl.when((i == S // tm - 1) & (k == nk - 1))
            def _():
                hosted.finish(h_ins, h_outs, h_sems)

    ni = S // tm
    row = pl.BlockSpec((tm, D), lambda k, i: (jnp.where(k == nk - 1, i, 0), 0))
    vec = pl.BlockSpec((1, D), lambda k, i: (0, 0))

    def seg_index(s):
        def index(k, i):
            j = k - starts[s]
            return jnp.where(j < 0, 0, jnp.where(j >= counts[s], ni - 1, i)), jnp.clip(j, 0, counts[s] - 1)
        return index

    in_specs = [pl.BlockSpec((tm, tk), seg_index(s)) for s in range(ns)]
    in_specs += [pl.BlockSpec((tk, D), lambda k, i: (k, 0)), row, row, vec]
    args = [a for a, _ in segments] + [w_in, x, dx2, g]
    out_specs = [row, vec]
    out_shape = [jax.ShapeDtypeStruct((S, D), F32), jax.ShapeDtypeStruct((1, D), F32)]
    scratch = [pltpu.VMEM((ni, tm, D), F32)]
    if hosted is not None:
        in_specs += [ANY_SPEC] * hn
        args += hosted.arrays
        out_specs += [ANY_SPEC] * hn
        out_shape += hosted.out_shapes()
        scratch += hosted.sem_shapes()
    res = pl.pallas_call(
        body, name="dh", grid=(nk, S // tm),
        in_specs=in_specs, out_specs=tuple(out_specs), out_shape=tuple(out_shape), scratch_shapes=scratch,
        compiler_params=_params(("arbitrary", "arbitrary"), BIG_VMEM_LIMIT),
    )(*args)
    return res[0], res[1], list(res[2:])


def _tn_matmul(a, bs, name, b_first=False, tm=512):
    M, K = a.shape
    nb = len(bs)
    shapes = [(b.shape[1], K) if b_first else (K, b.shape[1]) for b in bs]

    def body(a_ref, *refs):
        @pl.when(pl.program_id(0) == 0)
        def _():
            for o_ref in refs[nb:]:
                o_ref[...] = jnp.zeros_like(o_ref)

        at = a_ref[...]
        for b_ref, o_ref in zip(refs[:nb], refs[nb:]):
            for c in range(0, b_ref.shape[1], 512):
                if b_first:
                    o_ref[c:c + 512, :] += _tn(b_ref[:, c:c + 512], at)
                else:
                    o_ref[:, c:c + 512] += _tn(at, b_ref[:, c:c + 512])

    return pl.pallas_call(
        body, name=name, grid=(M // tm,),
        in_specs=[pl.BlockSpec((tm, K), lambda m: (m, 0))] + [pl.BlockSpec((tm, b.shape[1]), lambda m: (m, 0))
                                                              for b in bs],
        out_specs=tuple(pl.BlockSpec(s, lambda m: (0, 0)) for s in shapes),
        out_shape=tuple(jax.ShapeDtypeStruct(s, F32) for s in shapes),
        compiler_params=_params(("arbitrary",)),
    )(a, *bs)


def _adamw(parts, w, m, v, name, tr=None, split=None, by_chip=False):
    R, C = w.shape
    tr = R if tr is None else tr
    parts = [parts] if split is None else list(parts)
    npar = len(parts)

    def total(p_ref):
        if by_chip:
            c = lax.axis_index("c")
            g = p_ref[c].astype(F32)
            for chip in range(1, N_DEV // 2):
                g = g + p_ref[2 * chip + c].astype(F32)
            return g
        g = p_ref[0].astype(F32)
        for dev in range(1, N_DEV):
            g = g + p_ref[dev].astype(F32)
        return g

    def body(*refs):
        w_ref, m_ref, v_ref, g_out, d_out, m_out, v_out = refs[npar:]
        if split is None:
            g = total(refs[0])
        else:
            g = jnp.where(_mesh_pos()[3] < split, total(refs[0]), total(refs[1]))
        mn = ADAM_B1 * m_ref[...] + (1.0 - ADAM_B1) * g
        vn = ADAM_B2 * v_ref[...] + (1.0 - ADAM_B2) * (g * g)
        m_hat = mn / (1.0 - ADAM_B1 ** ADAM_STEP)
        v_hat = vn / (1.0 - ADAM_B2 ** ADAM_STEP)
        g_out[...] = g
        d_out[...] = -ADAM_LR * (m_hat / (jnp.sqrt(v_hat) + ADAM_EPS) + ADAM_WD * w_ref[...])
        m_out[...] = mn
        v_out[...] = vn

    blk = pl.BlockSpec((tr, C), lambda i: (i, 0))
    shp = jax.ShapeDtypeStruct((R, C), F32)
    return pl.pallas_call(
        body, name=name, grid=(R // tr,),
        in_specs=[pl.BlockSpec((N_DEV, tr, C), lambda i: (0, i, 0))] * npar + [blk, blk, blk],
        out_specs=(blk, blk, blk, blk), out_shape=(shp, shp, shp, shp),
        compiler_params=_params(("parallel",)),
    )(*parts, w, m, v)


def _local_step(x, target, norm_g, w_in, conv_w, conv_b, ln_g, ln_b, w_out, gf, exchanges=None, late_weights=None):
    ex_out, ex_att, ex_conv = exchanges if exchanges is not None else (None, None, None)
    conv_cols = w_in.shape[0] - 2 * ATT_W - 2 * KV_W
    q, kv, a_gate, gates, h_rm, h, *gathered = _inproj(
        x, norm_g, w_in, [(ATT_W, HEAD_DIM ** -0.5, True), (2 * KV_W, 1.0, True), (ATT_W, 1.0, True),
                          (conv_cols, 1.0, False)], late_weights[0] if late_weights is not None else None)
    if late_weights is not None:
        conv_w, w_out = late_weights[1](gathered)

    merged = None
    for idx, (_, dil) in enumerate(reversed(PATTERNS)):
        merged = _attn_fwd(q, kv, dil, "attn_fwd_d%d" % dil, merged,
                           a_gate if idx == len(PATTERNS) - 1 else None)
    o, lse, y_att = merged
    conv_out, y_conv = _conv_fwd(gates, conv_w, conv_b, ln_g, ln_b)
    dx2, dxb, loss_cols, g_gf = _outproj_loss(x, y_att, y_conv, w_out, gf, target)

    d_o, d_a_gate, delta, dxb_rm = _dy_att(dxb, w_out, a_gate, o)
    g_w_out = jnp.concatenate([_tn_matmul(y_att, [dxb_rm], "gw_out_att")[0],
                               _tn_matmul(y_conv, [dxb], "gw_out_conv")[0]], axis=0)
    acc, out_parts = None, []
    for idx, (_, dil) in enumerate(reversed(PATTERNS)):
        hosted = ex_out(g_w_out) if (idx == 0 and ex_out is not None) else None
        acc, outs = _attn_bwd(q, kv, d_o, lse, delta, dil, acc, idx == len(PATTERNS) - 1, "attn_bwd_d%d" % dil,
                              hosted)
        out_parts += outs
    dq, dkv = acc
    g_q, g_kv, g_a = _tn_matmul(h_rm, [dq, dkv, d_a_gate], "gw_in_att", b_first=True)

    d_c_gate, d_conv, g_ln_g, g_ln_b, g_conv_b = _dy_conv(dxb, w_out, gates, conv_out, ln_g, ln_b)
    dgates, g_conv_w, att_parts = _conv_bwd(d_conv, gates, d_c_gate, conv_w,
                                            ex_att(g_q, g_kv, g_a) if ex_att is not None else None)
    g_c, = _tn_matmul(h, [dgates], "gw_in_conv", b_first=True)
    grad_x, g_norm_g, conv_parts = _dh(
        [(dq, True), (dkv, True), (d_a_gate, True), (dgates, False)], w_in, x, dx2, norm_g,
        ex_conv(g_a, g_c, g_conv_w) if ex_conv is not None else None)
    small = (g_norm_g, g_conv_b, g_ln_g, g_ln_b, g_gf, loss_cols)
    return grad_x, (g_q, g_kv, g_a, g_c), g_w_out, g_conv_w, small, (out_parts, att_parts, conv_parts)


def kernel(x, norm_g, w_in, conv_w, conv_b, conv_ln_g, conv_ln_b, w_out, final_norm_g, loss_target, m_norm_g, m_w_in, m_conv_w, m_conv_b, m_conv_ln_g, m_conv_ln_b, m_w_out, m_final_norm_g, v_norm_g, v_w_in, v_conv_w, v_conv_b, v_conv_ln_g, v_conv_ln_b, v_w_out, v_final_norm_g):
    S, D = x.shape[1], x.shape[2]
    win_sh, wout_sh, cw_sh = w_in[0].T, w_out[0], conv_w[0]
    cols_sh, rows_sh, ch_sh = win_sh.shape[0], wout_sh.shape[0], cw_sh.shape[1]

    win_all, = _gather_two_level([win_sh.astype(BF16)], "gather_w_in")
    w_in_full = win_all.reshape(N_DEV * cols_sh, D)

    def late_weights(gathered):
        wout_all, cw_all = gathered
        conv_w_full = cw_all.transpose(1, 0, 2).reshape(CONV_K, N_DEV * ch_sh)
        return jnp.pad(conv_w_full, ((0, CONV_HALO - CONV_K), (0, 0))), wout_all.reshape(N_DEV * rows_sh, D)

    gf = final_norm_g.reshape(1, D)

    first = -(-(ATT_W + 2 * KV_W) // cols_sh)
    a_off = first * cols_sh - (ATT_W + 2 * KV_W)
    assert 0 <= a_off <= ATT_W

    def pieces(parts, n):
        return jnp.concatenate([p.astype(BF16) for p in parts], axis=0).reshape(n, cols_sh, D)

    def ex_out(g_w_out):
        return _Exchange([g_w_out.reshape(N_DEV, rows_sh, D).astype(BF16)], [(0, N_DEV)])

    same_core = (2, 4, 6)

    def ex_att(g_q, g_kv, g_a):
        mine = _chip_sum(pieces([g_q, g_kv, g_a[:a_off]], first), 0, "rs_att")
        return _Exchange([mine], [(0, first)], [same_core])

    def ex_conv(g_a, g_c, g_conv_w):
        mine = _chip_sum(pieces([g_a[a_off:], g_c], N_DEV - first), first, "rs_conv")
        return _Exchange(
            [mine, g_conv_w[:CONV_K].reshape(CONV_K, N_DEV, ch_sh).transpose(1, 0, 2)],
            [(first, N_DEV), (0, N_DEV)], [same_core, None])

    grad_x, _, _, _, small, parts = _local_step(
        x[0], loss_target[0], norm_g, w_in_full, None, conv_b, conv_ln_g, conv_ln_b, None, gf,
        (ex_out, ex_att, ex_conv), (_Gather([wout_sh.astype(BF16), cw_sh]), late_weights))
    (wout_parts,), (win_parts_lo,), (win_parts_hi, cw_parts) = parts

    small_pack = jnp.concatenate(list(small) + [jnp.zeros((2, D), F32)], axis=0)
    small_parts, = _exchange([small_pack], [None], "gather_small")

    upd_win = _adamw((win_parts_lo, win_parts_hi), win_sh, m_w_in[0].T, v_w_in[0].T, "adamw_w_in",
                     tr=cols_sh // 2, split=first, by_chip=True)
    upd_wout = _adamw(wout_parts, wout_sh, m_w_out[0], v_w_out[0], "adamw_w_out", tr=128)
    upd_cw = _adamw(cw_parts, cw_sh, m_conv_w[0], v_conv_w[0], "adamw_conv_w")
    zeros3 = jnp.zeros((3, D), F32)
    stack = lambda a, b, c, d_, e: jnp.concatenate([a, b, c, d_, e.reshape(1, D), zeros3], axis=0)
    upd_small = _adamw(
        small_parts,
        stack(norm_g, conv_b, conv_ln_g, conv_ln_b, final_norm_g),
        stack(m_norm_g, m_conv_b, m_conv_ln_g, m_conv_ln_b, m_final_norm_g),
        stack(v_norm_g, v_conv_b, v_conv_ln_g, v_conv_ln_b, v_final_norm_g) + jnp.concatenate(
            [jnp.zeros((5, D), F32), jnp.ones((3, D), F32)], axis=0),
        "adamw_small")

    loss = 0.5 / D * jnp.sum(upd_small[0][5])

    def outputs(kind):
        sm = upd_small[kind]
        return [sm[0:1], upd_win[kind].T[None], upd_cw[kind][None], sm[1:2], sm[2:3], sm[3:4],
                upd_wout[kind][None], sm[4]]

    return (loss, grad_x[None], *outputs(0), *outputs(1), *outputs(2), *outputs(3))
```

```python
import jax
import jax.numpy as jnp
from jax import lax
from jax.experimental import pallas as pl
from jax.experimental.pallas import tpu as pltpu

F32 = jnp.float32
BF16 = jnp.bfloat16

HEAD_DIM = 64
N_KV_HEADS = 4
N_Q_HEADS = 16
ATT_W = 1024
KV_W = 256
CONV_K = 31
CONV_HALO = 32
PATTERNS = ((128, 1), (512, 4), (2048, 16))
BLK = 128
LANES = 128
NORM_EPS = 1e-6
LN_EPS = 1e-5
NEG = -1e30
N_DEV = 8
ADAM_LR, ADAM_B1, ADAM_B2, ADAM_EPS, ADAM_WD, ADAM_STEP = 0.001, 0.9, 0.999, 1e-08, 0.01, 10
VMEM_LIMIT = 48 * 1024 * 1024
BIG_VMEM_LIMIT = 58 * 1024 * 1024
SLOPES = tuple(2.0 ** (-8.0 * (h + 1) / N_Q_HEADS) for h in range(N_Q_HEADS))
MESH = pl.DeviceIdType.MESH


def _params(sem, vmem_limit=VMEM_LIMIT):
    return pltpu.CompilerParams(dimension_semantics=sem, vmem_limit_bytes=vmem_limit)


def _sigmoid(v):
    return 1.0 / (1.0 + jnp.exp(-v))


def _silu_and_grad(v):
    s = _sigmoid(v)
    return v * s, s * (1.0 + v * (1.0 - s))


ANY_SPEC = pl.BlockSpec(memory_space=pl.ANY)


def _mesh_pos():
    x, y, c = lax.axis_index("x"), lax.axis_index("y"), lax.axis_index("c")
    return x, y, c, 4 * x + 2 * y + c


def _flipped(k, x, y, c):
    px = 1 - x if k & 4 else x
    py = 1 - y if k & 2 else y
    pc = 1 - c if k & 1 else c
    return (px, py, pc), 4 * px + 2 * py + pc


class _Exchange:
    def __init__(self, arrays, dests, flips=None):
        self.arrays, self.dests, self.n = list(arrays), list(dests), len(arrays)
        self.flips = [tuple(range(1, N_DEV)) if f is None else tuple(f)
                      for f in (flips if flips is not None else [None] * self.n)]

    def out_shapes(self):
        return [jax.ShapeDtypeStruct((N_DEV,) + a.shape[-2:], a.dtype) for a in self.arrays]

    def sem_shapes(self):
        return [pltpu.SemaphoreType.DMA((self.n, N_DEV - 1)), pltpu.SemaphoreType.DMA((self.n, N_DEV - 1)),
                pltpu.SemaphoreType.DMA((self.n,))]

    def _when(self, a, dev, fn):
        if self.dests[a] is None:
            fn()
        else:
            lo, hi = self.dests[a]
            pl.when((dev >= lo) & (dev < hi))(fn)

    def _mine(self, ins, a, dev):
        return ins[a] if self.dests[a] is None else ins[a].at[dev - self.dests[a][0]]

    def _copy(self, ins, outs, sems, a, k, src_dev, slot, target):
        return pltpu.make_async_remote_copy(
            src_ref=self._mine(ins, a, src_dev), dst_ref=outs[a].at[slot],
            send_sem=sems[0].at[a, k - 1], recv_sem=sems[1].at[a, k - 1],
            device_id=target, device_id_type=MESH)

    def start(self, ins, outs, sems):
        x, y, c, me = _mesh_pos()
        for a in range(self.n):
            self._when(a, me, lambda a=a: pltpu.make_async_copy(
                self._mine(ins, a, me), outs[a].at[me], sems[2].at[a]).start())
            for k in self.flips[a]:
                target, peer = _flipped(k, x, y, c)
                self._when(a, peer, lambda a=a, k=k, target=target, peer=peer: self._copy(
                    ins, outs, sems, a, k, peer, me, target).start())

    def finish(self, ins, outs, sems):
        x, y, c, me = _mesh_pos()
        lo0 = [0 if d is None else d[0] for d in self.dests]
        for a in range(self.n):
            for k in self.flips[a]:
                target, peer = _flipped(k, x, y, c)
                self._when(a, me, lambda a=a, k=k, peer=peer: self._copy(
                    ins, outs, sems, a, k, lo0[a], peer, (x, y, c)).wait_recv())
            for k in self.flips[a]:
                target, peer = _flipped(k, x, y, c)
                self._when(a, peer, lambda a=a, k=k, target=target, peer=peer: self._copy(
                    ins, outs, sems, a, k, peer, me, target).wait_send())
            self._when(a, me, lambda a=a: pltpu.make_async_copy(
                self._mine(ins, a, me), outs[a].at[me], sems[2].at[a]).wait())


def _exchange(arrays, dests, name, flips=None):
    ex = _Exchange(arrays, dests, flips)
    na = ex.n

    def body(*refs):
        ins, outs, sems = refs[:na], refs[na:2 * na], refs[2 * na:]
        ex.start(ins, outs, sems)
        ex.finish(ins, outs, sems)

    return pl.pallas_call(
        body, name=name, out_shape=tuple(ex.out_shapes()),
        in_specs=[ANY_SPEC] * na, out_specs=tuple([ANY_SPEC] * na), scratch_shapes=ex.sem_shapes(),
    )(*arrays)


def _chip_sum(pieces, lo, name):
    n, R, C = pieces.shape

    def swap(p_ref, t_ref, send_sems, recv_sems):
        x, y, c, me = _mesh_pos()
        for i in range(n):
            mine = (lo + i) % 2
            cp = pltpu.make_async_remote_copy(
                src_ref=p_ref.at[i], dst_ref=t_ref.at[i], send_sem=send_sems.at[i], recv_sem=recv_sems.at[i],
                device_id=(x, y, 1 - c), device_id_type=MESH)
            pl.when(c != mine)(cp.start)
        for i in range(n):
            mine = (lo + i) % 2
            cp = pltpu.make_async_remote_copy(
                src_ref=p_ref.at[i], dst_ref=t_ref.at[i], send_sem=send_sems.at[i], recv_sem=recv_sems.at[i],
                device_id=(x, y, 1 - c), device_id_type=MESH)
            pl.when(c == mine)(cp.wait_recv)
            pl.when(c != mine)(cp.wait_send)

    other = pl.pallas_call(
        swap, name=name + "_swap", out_shape=jax.ShapeDtypeStruct(pieces.shape, pieces.dtype),
        in_specs=[ANY_SPEC], out_specs=ANY_SPEC,
        scratch_shapes=[pltpu.SemaphoreType.DMA((n,)), pltpu.SemaphoreType.DMA((n,))],
    )(pieces)

    def add(p_ref, t_ref, o_ref):
        o_ref[...] = (p_ref[...].astype(F32) + t_ref[...].astype(F32)).astype(o_ref.dtype)

    tr = R // 2
    blk = pl.BlockSpec((None, tr, C), lambda i, r: (i, r, 0))
    return pl.pallas_call(
        add, name=name + "_add", grid=(n, R // tr), in_specs=[blk, blk], out_specs=blk,
        out_shape=jax.ShapeDtypeStruct(pieces.shape, pieces.dtype),
        compiler_params=_params(("parallel", "parallel")),
    )(pieces, other)


class _Gather:
    def __init__(self, arrays):
        self.arrays, self.n = list(arrays), len(arrays)

    def out_shapes(self):
        return [jax.ShapeDtypeStruct((N_DEV,) + a.shape, a.dtype) for a in self.arrays]

    def sem_shapes(self):
        return [pltpu.SemaphoreType.DMA((self.n, N_DEV - 1)), pltpu.SemaphoreType.DMA((self.n, N_DEV - 1)),
                pltpu.SemaphoreType.DMA((self.n,))]

    def _plan(self, ins, outs, sems):
        x, y, c, me = _mesh_pos()
        chips = [(1 - x, y), (x, 1 - y), (1 - x, 1 - y)]

        def copy(a, k, src, block, to):
            px, py, pc = block
            return pltpu.make_async_remote_copy(
                src_ref=src, dst_ref=outs[a].at[4 * px + 2 * py + pc], send_sem=sems[0].at[a, k],
                recv_sem=sems[1].at[a, k], device_id=to, device_id_type=MESH)

        local = [pltpu.make_async_copy(ins[a], outs[a].at[me], sems[2].at[a]) for a in range(self.n)]
        first = []
        for a in range(self.n):
            first.append(copy(a, 0, ins[a], (x, y, c), (x, y, 1 - c)))
            first += [copy(a, 1 + j, ins[a], (x, y, c), (*chip, c)) for j, chip in enumerate(chips)]
        return (x, y, c), chips, copy, local, first

    def start(self, ins, outs, sems):
        _, _, _, local, first = self._plan(ins, outs, sems)
        for cp in local + first:
            cp.start()

    def finish(self, ins, outs, sems):
        (x, y, c), chips, copy, local, first = self._plan(ins, outs, sems)
        passed = []
        for j, chip in enumerate(chips):
            for a in range(self.n):
                copy(a, 1 + j, ins[a], (*chip, c), (x, y, c)).wait_recv()
                px, py = chip
                fwd = copy(a, 4 + j, outs[a].at[4 * px + 2 * py + c], (*chip, c), (x, y, 1 - c))
                fwd.start()
                passed.append(fwd)
        for a in range(self.n):
            copy(a, 0, ins[a], (x, y, 1 - c), (x, y, c)).wait_recv()
            for j, chip in enumerate(chips):
                copy(a, 4 + j, ins[a], (*chip, 1 - c), (x, y, c)).wait_recv()
        for cp in first + passed:
            cp.wait_send()
        for cp in local:
            cp.wait()


def _gather_two_level(arrays, name):
    ga = _Gather(arrays)
    na = ga.n

    def body(*refs):
        ins, outs, sems = refs[:na], refs[na:2 * na], refs[2 * na:]
        ga.start(ins, outs, sems)
        ga.finish(ins, outs, sems)

    return pl.pallas_call(
        body, name=name, out_shape=tuple(ga.out_shapes()),
        in_specs=[ANY_SPEC] * na, out_specs=tuple([ANY_SPEC] * na), scratch_shapes=ga.sem_shapes(),
    )(*arrays)


CHUNK = 128
RESIDUES = 16
PER_RES = CHUNK // RESIDUES


def _perm_rows(tile, inverse):
    a = lax.broadcasted_iota(jnp.int32, (CHUNK, CHUNK), 0)
    b = lax.broadcasted_iota(jnp.int32, (CHUNK, CHUNK), 1)
    if inverse:
        a, b = b, a
    p = jnp.where(a == PER_RES * (b % RESIDUES) + b // RESIDUES, 1.0, 0.0).astype(BF16)
    parts = [jnp.dot(p, tile[c * CHUNK:(c + 1) * CHUNK], preferred_element_type=F32)
             for c in range(tile.shape[0] // CHUNK)]
    return jnp.concatenate(parts, axis=0).astype(BF16)


class _Rows:
    def __init__(self, dil, S):
        nc = S // CHUNK
        self.dil = dil
        if dil == 1:
            self.view, self.block, self.nb = (nc, CHUNK), (None, CHUNK), nc
            self.index = lambda r, b: (b, 0, 0)
        elif dil == 4:
            self.view, self.block, self.nb = (nc, 4, 4, PER_RES), (4, 4, None, PER_RES), nc // 4
            self.index = lambda r, b: (b, 0, r, 0, 0)
        elif dil == RESIDUES:
            self.view, self.block, self.nb = (nc, RESIDUES, PER_RES), (RESIDUES, None, PER_RES), nc // RESIDUES
            self.index = lambda r, b: (b, r, 0, 0)
        else:
            raise NotImplementedError(dil)

    def of(self, a):
        return a.reshape(self.view + (a.shape[-1],))

    def spec(self, width, which_block):
        return pl.BlockSpec(self.block + (width,), lambda r, n: self.index(r, which_block(n)))

    def pos(self, row):
        if self.dil == 1:
            return (row % PER_RES) * RESIDUES + row // PER_RES
        if self.dil == 4:
            return (row // 32) * 32 + (row % PER_RES) * 4 + (row % 32) // PER_RES
        return row


def _ld(ref, cols=slice(None)):
    v = ref[(slice(None),) * (len(ref.shape) - 1) + (cols,)]
    return v.reshape(BLK, v.shape[-1])


def _st(ref, val, cols=slice(None)):
    ref[(slice(None),) * (len(ref.shape) - 1) + (cols,)] = val.reshape(ref.shape[:-1] + (val.shape[-1],))


def _inproj(x, g, w_t, segments, hosted=None, tm=1024, tn=512):
    S, D = x.shape
    ns = len(segments)
    ni = S // tm
    counts = [seg[0] // tn for seg in segments]
    starts = [sum(counts[:s]) for s in range(ns)]

    hn = hosted.n if hosted is not None else 0
    last_p = sum(counts)

    def body(x_ref, g_ref, w_ref, *rest):
        h_ins, rest = rest[:hn], rest[hn:]
        outs = rest[:ns]
        hrm_out, h_out = rest[ns:ns + 2]
        h_outs = rest[ns + 2:ns + 2 + hn]
        hrm_scr, h_scr = rest[ns + 2 + hn:ns + 4 + hn]
        h_sems = rest[ns + 4 + hn:]
        p, i = pl.program_id(0), pl.program_id(1)

        if hosted is not None:
            @pl.when((p == 0) & (i == 0))
            def _():
                hosted.start(h_ins, h_outs, h_sems)

            @pl.when((p == last_p) & (i == ni - 1))
            def _():
                hosted.finish(h_ins, h_outs, h_sems)

        @pl.when(p == 0)
        def _():
            xf = x_ref[...]
            r = lax.rsqrt(jnp.mean(xf * xf, axis=-1, keepdims=True) + NORM_EPS)
            h = (xf * r * g_ref[...]).astype(BF16)
            hrm = _perm_rows(h, False)
            h_scr[i] = h
            hrm_scr[i] = hrm
            h_out[...] = h
            hrm_out[...] = hrm

        for s, (_, scale, rm, dtype) in enumerate(segments):
            @pl.when((p > starts[s]) & (p <= starts[s] + counts[s]))
            def _(s=s, scale=scale, rm=rm, dtype=dtype):
                acc = _nt((hrm_scr if rm else h_scr)[i], w_ref[...])
                outs[s][...] = (acc * scale if scale != 1.0 else acc).astype(dtype)

    def out_index(s):
        def index(p, i):
            j = p - 1 - starts[s]
            row = jnp.where(j < 0, 0, jnp.where(j >= counts[s], ni - 1, i))
            return row, jnp.clip(j, 0, counts[s] - 1)
        return index

    first_pass = pl.BlockSpec((tm, D), lambda p, i: (jnp.where(p == 0, i, ni - 1), 0))
    out_specs = [pl.BlockSpec((tm, tn), out_index(s)) for s in range(ns)]
    out_shape = [jax.ShapeDtypeStruct((S, seg[0]), seg[3]) for seg in segments]
    in_specs = [first_pass, pl.BlockSpec((1, D), lambda p, i: (0, 0)),
                pl.BlockSpec((tn, D), lambda p, i: (jnp.maximum(p - 1, 0), 0))]
    args = [x, g, w_t]
    out_specs = out_specs + [first_pass, first_pass]
    out_shape = out_shape + [jax.ShapeDtypeStruct((S, D), BF16)] * 2
    scratch = [pltpu.VMEM((ni, tm, D), BF16), pltpu.VMEM((ni, tm, D), BF16)]
    if hosted is not None:
        in_specs += [ANY_SPEC] * hn
        args += hosted.arrays
        out_specs += [ANY_SPEC] * hn
        out_shape += hosted.out_shapes()
        scratch += hosted.sem_shapes()
    return pl.pallas_call(
        body, name="inproj", grid=(1 + last_p, ni),
        in_specs=in_specs, out_specs=tuple(out_specs), out_shape=tuple(out_shape), scratch_shapes=scratch,
        compiler_params=_params(("arbitrary", "arbitrary"), BIG_VMEM_LIMIT),
    )(*args)


def _fill_bias_table(tbl, rows, keys_first=False):
    shape = (2 * BLK, BLK) if keys_first else (BLK, 2 * BLK)
    qi = lax.broadcasted_iota(jnp.int32, shape, 1 if keys_first else 0)
    kj = lax.broadcasted_iota(jnp.int32, shape, 0 if keys_first else 1)
    dist = rows.pos(qi) - rows.pos(kj % BLK) + jnp.where(kj < BLK, BLK, 0)
    inside = (dist >= 0) & (dist <= BLK)
    negd = (dist * (-rows.dil)).astype(F32)
    for f, valid in enumerate((inside & (kj >= BLK), inside)):
        for h in range(N_Q_HEADS):
            tbl[f * N_Q_HEADS + h] = jnp.where(valid, SLOPES[h] * negd, NEG)


def _bias2(tbl, n, h0, h1, axis=0):
    base = jnp.where(n == 0, 0, N_Q_HEADS)
    return jnp.concatenate([tbl[base + h0], tbl[base + h1]], axis=axis)


def _head_operands(kv2, hk, lo_mask):
    half, pos = hk // 2, hk % 2
    out = []
    for base in (0, KV_W):
        t = kv2[:, base + half * LANES: base + (half + 1) * LANES]
        sw = pltpu.roll(t, HEAD_DIM, axis=1)
        at_lo, at_hi = (t, sw) if pos == 0 else (sw, t)
        out.append(jnp.where(lo_mask, at_lo, 0.0).astype(BF16))
        out.append(jnp.where(lo_mask, 0.0, at_hi).astype(BF16))
    return out


def _nt(a, b):
    return lax.dot_general(a, b, (((1,), (1,)), ((), ())), preferred_element_type=F32)


def _tn(a, b):
    return lax.dot_general(a, b, (((0,), (0,)), ((), ())), preferred_element_type=F32)


def _attn_fwd(q, kv, dil, name, prev=None, gate=None):
    S = q.shape[0]
    rows = _Rows(dil, S)
    nb = rows.nb
    have_prev, last = prev is not None, gate is not None

    def body(*refs):
        refs = list(refs)
        q_ref, kvc_ref, kvp_ref = refs[:3]
        del refs[:3]
        if have_prev:
            po_ref, pl_ref = refs[:2]
            del refs[:2]
        if last:
            gate_ref = refs.pop(0)
        o_ref, lse_ref = refs[:2]
        y_ref = refs[2] if last else None
        tbl = refs[-1]
        n = pl.program_id(1)

        @pl.when((pl.program_id(0) == 0) & (n == 0))
        def _():
            _fill_bias_table(tbl, rows)

        kv2 = jnp.concatenate([_ld(kvp_ref), _ld(kvc_ref)], axis=0)
        lo_mask = lax.broadcasted_iota(jnp.int32, (2 * BLK, LANES), 1) < HEAD_DIM
        lane = lax.broadcasted_iota(jnp.int32, (BLK, LANES), 1)
        stats = jnp.zeros((BLK, LANES), F32)
        for hk in range(N_KV_HEADS):
            k_lo, k_hi, v_lo, v_hi = _head_operands(kv2, hk, lo_mask)
            cols = [slice(b * LANES, (b + 1) * LANES) for b in (2 * hk, 2 * hk + 1)]
            q2 = jnp.concatenate([_ld(q_ref, cols[0]), _ld(q_ref, cols[1])], axis=0).astype(BF16)
            o2 = jnp.zeros((2 * BLK, LANES), F32)
            for which, (kk, vv) in enumerate(((k_lo, v_lo), (k_hi, v_hi))):
                h0, h1 = 4 * hk + which, 4 * hk + 2 + which
                s = _nt(q2, kk) + _bias2(tbl, n, h0, h1)
                m = jnp.max(s, axis=1, keepdims=True)
                p = jnp.exp(s - m)
                l = jnp.sum(p, axis=1, keepdims=True)
                o2 = o2 + jnp.dot(p.astype(BF16), vv, preferred_element_type=F32) * (1.0 / l)
                lse = m + jnp.log(l)
                stats = jnp.where(lane == h0, lse[0:BLK], stats)
                stats = jnp.where(lane == h1, lse[BLK:], stats)
            _st(o_ref, o2[0:BLK], cols[0])
            _st(o_ref, o2[BLK:], cols[1])
        if have_prev:
            before = _ld(pl_ref)
            top = jnp.maximum(before, stats)
            e_old, e_new = jnp.exp(before - top), jnp.exp(stats - top)
            total = e_old + e_new
            stats = top + jnp.log(total)
            inv = 1.0 / total
            w_old, w_new = e_old * inv, e_new * inv
        if have_prev or last:
            lo = lane < HEAD_DIM
            for blk in range(ATT_W // LANES):
                cols = slice(blk * LANES, (blk + 1) * LANES)
                o_blk = _ld(o_ref, cols)
                if have_prev:
                    pick = lambda w: jnp.where(lo, w[:, 2 * blk:2 * blk + 1], w[:, 2 * blk + 1:2 * blk + 2])
                    o_blk = o_blk * pick(w_new) + _ld(po_ref, cols) * pick(w_old)
                    _st(o_ref, o_blk, cols)
                if last:
                    a = _ld(gate_ref, cols).astype(F32)
                    _st(y_ref, (o_blk * (a * _sigmoid(a))).astype(BF16), cols)
        _st(lse_ref, stats)

    here = lambda n: n
    before_n = lambda n: jnp.maximum(n - 1, 0)
    in_specs = [rows.spec(ATT_W, here), rows.spec(2 * KV_W, here), rows.spec(2 * KV_W, before_n)]
    args = [rows.of(q), rows.of(kv), rows.of(kv)]
    if have_prev:
        in_specs += [rows.spec(ATT_W, here), rows.spec(LANES, here)]
        args += [rows.of(prev[0]), rows.of(prev[1])]
    out_specs = [rows.spec(ATT_W, here), rows.spec(LANES, here)]
    out_shape = [jax.ShapeDtypeStruct(rows.view + (ATT_W,), F32), jax.ShapeDtypeStruct(rows.view + (LANES,), F32)]
    if last:
        in_specs.append(rows.spec(ATT_W, here))
        args.append(rows.of(gate))
        out_specs.append(rows.spec(ATT_W, here))
        out_shape.append(jax.ShapeDtypeStruct(rows.view + (ATT_W,), BF16))
    res = pl.pallas_call(
        body, name=name, grid=(dil, nb),
        in_specs=in_specs, out_specs=tuple(out_specs), out_shape=tuple(out_shape),
        scratch_shapes=[pltpu.VMEM((2 * N_Q_HEADS, BLK, 2 * BLK), F32)],
        compiler_params=_params(("arbitrary", "arbitrary")),
    )(*args)
    return tuple(r.reshape(S, r.shape[-1]) for r in res)


def _shifted_copies(buf, phases):
    n = phases.shape[1]
    for b in range(1, 8):
        phases[b - 1] = buf[b:b + n, :]


def _window(buf, phases, start, cols):
    b = start % 8
    if b == 0:
        return buf[start:start + 8, cols]
    return phases[b - 1, start - b:start - b + 8, cols]


def _broadcast_taps(w_ref, wb):
    for j in range(CONV_K):
        wb[j] = jnp.broadcast_to(w_ref[j:j + 1, :], wb.shape[1:])


def _conv_fwd(gates, conv_w, conv_b, ln_g, ln_b, tt=256):
    S = gates.shape[0]
    C = conv_w.shape[1]
    hb = tt // CONV_HALO

    def body(val_ref, glu_ref, hval_ref, hglu_ref, gate_ref, w_ref, b_ref, g_ref, beta_ref,
             conv_ref, y_ref, hbuf, hph):
        i = pl.program_id(0)
        halo = hval_ref[...].astype(F32) * _sigmoid(hglu_ref[...].astype(F32))
        hbuf[0:CONV_HALO, :] = jnp.where(i > 0, halo, 0.0)
        hbuf[CONV_HALO:, :] = val_ref[...].astype(F32) * _sigmoid(glu_ref[...].astype(F32))
        _shifted_copies(hbuf, hph)
        for cb in range(C // LANES):
            cols = slice(cb * LANES, (cb + 1) * LANES)
            wj = [jnp.broadcast_to(w_ref[j:j + 1, cols], (8, LANES)) for j in range(CONV_K)]
            for rc in range(tt // 8):
                acc = jnp.zeros((8, LANES), F32)
                for j in range(CONV_K):
                    start = rc * 8 + CONV_HALO - (CONV_K - 1) + j
                    acc = acc + _window(hbuf, hph, start, cols) * wj[j]
                conv_ref[rc * 8:(rc + 1) * 8, cols] = acc
        cv = conv_ref[...] + b_ref[...]
        conv_ref[...] = cv
        mu = jnp.mean(cv, axis=-1, keepdims=True)
        xc = cv - mu
        var = jnp.mean(xc * xc, axis=-1, keepdims=True)
        ln = xc * lax.rsqrt(var + LN_EPS) * g_ref[...] + beta_ref[...]
        gt = gate_ref[...].astype(F32)
        y_ref[...] = (ln * _sigmoid(ln) * (gt * _sigmoid(gt))).astype(BF16)

    vec = pl.BlockSpec((1, C), lambda i: (0, 0))
    return pl.pallas_call(
        body, name="conv_fwd", grid=(S // tt,),
        in_specs=[pl.BlockSpec((tt, C), lambda i: (i, 0)),
                  pl.BlockSpec((tt, C), lambda i: (i, 1)),
                  pl.BlockSpec((CONV_HALO, C), lambda i: (jnp.maximum(i * hb - 1, 0), 0)),
                  pl.BlockSpec((CONV_HALO, C), lambda i: (jnp.maximum(i * hb - 1, 0), 1)),
                  pl.BlockSpec((tt, C), lambda i: (i, 2)),
                  pl.BlockSpec((CONV_HALO, C), lambda i: (0, 0)), vec, vec, vec],
        out_specs=(pl.BlockSpec((tt, C), lambda i: (i, 0)), pl.BlockSpec((tt, C), lambda i: (i, 0))),
        out_shape=(jax.ShapeDtypeStruct((S, C), F32), jax.ShapeDtypeStruct((S, C), BF16)),
        scratch_shapes=[pltpu.VMEM((tt + CONV_HALO, C), F32), pltpu.VMEM((7, tt + CONV_HALO - 8, C), F32)],
        compiler_params=_params(("parallel",)),
    )(gates, gates, gates, gates, gates, conv_w, conv_b, ln_g, ln_b)


def _outproj_loss(x, y_att, y_conv, w_out, gf, target, tm=512):
    S, D = x.shape
    E = y_att.shape[1]

    def body(x_ref, ya_ref, yc_ref, w_ref, gf_ref, t_ref, dx_ref, dxb_ref, loss_ref, ggf_ref):
        @pl.when(pl.program_id(0) == 0)
        def _():
            loss_ref[...] = jnp.zeros_like(loss_ref)
            ggf_ref[...] = jnp.zeros_like(ggf_ref)

        x2 = (x_ref[...] + jnp.dot(_perm_rows(ya_ref[...], True), w_ref[0:E, :], preferred_element_type=F32)
              + jnp.dot(yc_ref[...], w_ref[E:, :], preferred_element_type=F32))
        r = lax.rsqrt(jnp.mean(x2 * x2, axis=-1, keepdims=True) + NORM_EPS)
        nrm = x2 * r
        gfv = gf_ref[...]
        err = nrm * gfv - t_ref[...]
        loss_ref[...] += jnp.sum(err * err, axis=0, keepdims=True)
        dout = err * (1.0 / D)
        ggf_ref[...] += jnp.sum(dout * nrm, axis=0, keepdims=True)
        dn = dout * gfv
        dx2 = r * (dn - nrm * jnp.mean(dn * nrm, axis=-1, keepdims=True))
        dx_ref[...] = dx2
        dxb_ref[...] = dx2.astype(BF16)

    row = lambda w: pl.BlockSpec((tm, w), lambda i: (i, 0))
    vec = pl.BlockSpec((1, D), lambda i: (0, 0))
    return pl.pallas_call(
        body, name="outproj_loss", grid=(S // tm,),
        in_specs=[row(D), row(E), row(E), pl.BlockSpec((2 * E, D), lambda i: (0, 0)), vec, row(D)],
        out_specs=(row(D), row(D), vec, vec),
        out_shape=(jax.ShapeDtypeStruct((S, D), F32), jax.ShapeDtypeStruct((S, D), BF16),
                   jax.ShapeDtypeStruct((1, D), F32), jax.ShapeDtypeStruct((1, D), F32)),
        compiler_params=_params(("arbitrary",)),
    )(x, y_att, y_conv, w_out, gf, target)


def _split3(v):
    hi = v.astype(BF16)
    r1 = v - hi.astype(F32)
    mid = r1.astype(BF16)
    lo = (r1 - mid.astype(F32)).astype(BF16)
    return hi, mid, lo


def _dy_att(dxb, w_out, gates, o, tm=512):
    S, D = dxb.shape
    E = ATT_W

    def body(dx_ref, w_ref, a_ref, o_ref, do_ref, da_ref, dl_ref, dxr_ref):
        dxr = _perm_rows(dx_ref[...], False)
        dxr_ref[...] = dxr
        dya = _nt(dxr, w_ref[...])
        a = a_ref[...].astype(F32)
        ov = o_ref[...]
        sl, dsl = _silu_and_grad(a)
        d_o = dya * sl
        do_ref[...] = d_o
        da_ref[...] = (dya * ov * dsl).astype(BF16)
        ci = lax.broadcasted_iota(jnp.int32, (E, LANES), 0) // HEAD_DIM
        hi = lax.broadcasted_iota(jnp.int32, (E, LANES), 1)
        sel = jnp.where(ci == hi, 1.0, 0.0).astype(BF16)
        acc = jnp.zeros((tm, LANES), F32)
        for part in _split3(d_o * ov):
            acc = acc + jnp.dot(part, sel, preferred_element_type=F32)
        dl_ref[...] = acc

    row = lambda w: pl.BlockSpec((tm, w), lambda i: (i, 0))
    return pl.pallas_call(
        body, name="dy_att", grid=(S // tm,),
        in_specs=[row(D), pl.BlockSpec((E, D), lambda i: (0, 0)), row(E), row(E)],
        out_specs=(row(E), row(E), row(LANES), row(D)),
        out_shape=(jax.ShapeDtypeStruct((S, E), F32), jax.ShapeDtypeStruct((S, E), BF16),
                   jax.ShapeDtypeStruct((S, LANES), F32), jax.ShapeDtypeStruct((S, D), BF16)),
        compiler_params=_params(("parallel",)),
    )(dxb, w_out, gates, o)


def _dy_conv(dxb, w_out, gates, conv_out, ln_g, ln_b, tm=512):
    S, D = dxb.shape
    C = conv_out.shape[1]

    def body(dx_ref, w_ref, gate_ref, cv_ref, g_ref, beta_ref, dgate_ref, dconv_ref, gg_ref, gb_ref, gcb_ref):
        @pl.when(pl.program_id(0) == 0)
        def _():
            gg_ref[...] = jnp.zeros_like(gg_ref)
            gb_ref[...] = jnp.zeros_like(gb_ref)
            gcb_ref[...] = jnp.zeros_like(gcb_ref)

        dyc = _nt(dx_ref[...], w_ref[...])
        cv = cv_ref[...]
        mu = jnp.mean(cv, axis=-1, keepdims=True)
        xc = cv - mu
        rstd = lax.rsqrt(jnp.mean(xc * xc, axis=-1, keepdims=True) + LN_EPS)
        nrm = xc * rstd
        gv = g_ref[...]
        ln = nrm * gv + beta_ref[...]
        u, du = _silu_and_grad(ln)
        gt = gate_ref[...].astype(F32)
        g2, dg2 = _silu_and_grad(gt)
        dgate_ref[...] = (dyc * u * dg2).astype(BF16)
        d_ln = dyc * g2 * du
        gb_ref[...] += jnp.sum(d_ln, axis=0, keepdims=True)
        gg_ref[...] += jnp.sum(d_ln * nrm, axis=0, keepdims=True)
        dn = d_ln * gv
        d_conv = rstd * (dn - jnp.mean(dn, axis=-1, keepdims=True)
                         - nrm * jnp.mean(dn * nrm, axis=-1, keepdims=True))
        dconv_ref[...] = d_conv
        gcb_ref[...] += jnp.sum(d_conv, axis=0, keepdims=True)

    row = lambda w: pl.BlockSpec((tm, w), lambda i: (i, 0))
    vec = pl.BlockSpec((1, C), lambda i: (0, 0))
    return pl.pallas_call(
        body, name="dy_conv", grid=(S // tm,),
        in_specs=[row(D), pl.BlockSpec((C, D), lambda i: (1, 0)),
                  pl.BlockSpec((tm, C), lambda i: (i, 2)), row(C), vec, vec],
        out_specs=(row(C), row(C), vec, vec, vec),
        out_shape=(jax.ShapeDtypeStruct((S, C), BF16), jax.ShapeDtypeStruct((S, C), F32),
                   jax.ShapeDtypeStruct((1, C), F32), jax.ShapeDtypeStruct((1, C), F32),
                   jax.ShapeDtypeStruct((1, C), F32)),
        compiler_params=_params(("arbitrary",)),
    )(dxb, w_out, gates, conv_out, ln_g, ln_b)


def _conv_bwd(d_conv, gates, d_c_gate, conv_w, hosted=None, tt=256):
    S, C = d_conv.shape
    hb = tt // CONV_HALO
    nt = S // tt
    hn = hosted.n if hosted is not None else 0

    def body(*refs):
        dc_ref, dnext_ref, val_ref, glu_ref, dg_ref, w_ref = refs[:6]
        h_ins = refs[6:6 + hn]
        out_ref, gw_ref = refs[6 + hn:8 + hn]
        h_outs = refs[8 + hn:8 + 2 * hn]
        hbuf, dbuf, dhbuf, dph, wb = refs[8 + 2 * hn:13 + 2 * hn]
        h_sems = refs[13 + 2 * hn:]
        i = pl.program_id(0)

        @pl.when(i == 0)
        def _():
            gw_ref[...] = jnp.zeros_like(gw_ref)
            _broadcast_taps(w_ref, wb)
            if hosted is not None:
                hosted.start(h_ins, h_outs, h_sems)

        val = val_ref[...].astype(F32)
        sg = _sigmoid(glu_ref[...].astype(F32))
        hbuf[...] = val * sg
        dbuf[0:tt, :] = dc_ref[...]
        dbuf[tt:, :] = jnp.where(i < nt - 1, dnext_ref[...], 0.0)
        _shifted_copies(dbuf, dph)
        for cb in range(C // LANES):
            cols = slice(cb * LANES, (cb + 1) * LANES)
            gacc = [jnp.zeros((8, LANES), F32) for _ in range(CONV_K)]
            group = 2
            for rc0 in range(0, tt // 8, group):
                hcur = [hbuf[(rc0 + r) * 8:(rc0 + r + 1) * 8, cols] for r in range(group)]
                accs = [jnp.zeros((8, LANES), F32) for _ in range(group)]
                for j in range(CONV_K):
                    wj = wb[j, :, cols]
                    for r in range(group):
                        dwin = _window(dbuf, dph, (rc0 + r) * 8 + (CONV_K - 1) - j, cols)
                        accs[r] = accs[r] + dwin * wj
                        gacc[j] = gacc[j] + dwin * hcur[r]
                for r in range(group):
                    dhbuf[(rc0 + r) * 8:(rc0 + r + 1) * 8, cols] = accs[r]
            for j in range(CONV_K):
                gw_ref[j:j + 1, cols] += jnp.sum(gacc[j], axis=0, keepdims=True)
        d_h = dhbuf[...]
        out_ref[:, 0:C] = (d_h * sg).astype(BF16)
        out_ref[:, C:2 * C] = (d_h * val * sg * (1.0 - sg)).astype(BF16)
        out_ref[:, 2 * C:3 * C] = dg_ref[...]

        if hosted is not None:
            @pl.when(i == nt - 1)
            def _():
                hosted.finish(h_ins, h_outs, h_sems)

    tile = lambda col: pl.BlockSpec((tt, C), lambda i: (i, col))
    in_specs = [tile(0),
                pl.BlockSpec((CONV_HALO, C), lambda i: (jnp.minimum((i + 1) * hb, S // CONV_HALO - 1), 0)),
                tile(0), tile(1), tile(0),
                pl.BlockSpec((CONV_HALO, C), lambda i: (0, 0))]
    args = [d_conv, d_conv, gates, gates, d_c_gate, conv_w]
    out_specs = [pl.BlockSpec((tt, 3 * C), lambda i: (i, 0)), pl.BlockSpec((CONV_HALO, C), lambda i: (0, 0))]
    out_shape = [jax.ShapeDtypeStruct((S, 3 * C), BF16), jax.ShapeDtypeStruct((CONV_HALO, C), F32)]
    scratch = [pltpu.VMEM((tt, C), F32), pltpu.VMEM((tt + CONV_HALO, C), F32), pltpu.VMEM((tt, C), F32),
               pltpu.VMEM((7, tt + CONV_HALO - 8, C), F32), pltpu.VMEM((CONV_K, 8, C), F32)]
    if hosted is not None:
        in_specs += [ANY_SPEC] * hn
        args += hosted.arrays
        out_specs += [ANY_SPEC] * hn
        out_shape += hosted.out_shapes()
        scratch += hosted.sem_shapes()
    res = pl.pallas_call(
        body, name="conv_bwd", grid=(nt,),
        in_specs=in_specs, out_specs=tuple(out_specs), out_shape=tuple(out_shape), scratch_shapes=scratch,
        compiler_params=_params(("arbitrary",)),
    )(*args)
    return res[0], res[1], list(res[2:])


def _attn_bwd(q, kv, d_o, lse, delta, dil, prev, final, name, hosted=None):
    S = q.shape[0]
    rows = _Rows(dil, S)
    nb = rows.nb
    steps = dil * nb
    out_dt = BF16 if final else F32
    have_prev = prev is not None
    hn = hosted.n if hosted is not None else 0

    def body(*refs):
        refs = list(refs)
        q_ref, do_ref, lse_ref, dl_ref, kvc_ref, kvp_ref = refs[:6]
        del refs[:6]
        if have_prev:
            pdq_ref, pdkv_ref = refs[:2]
            del refs[:2]
        h_ins = refs[:hn]
        dq_ref, dkv_ref = refs[hn:hn + 2]
        h_outs = refs[hn + 2:2 * hn + 2]
        carry, tbl = refs[2 * hn + 2:2 * hn + 4]
        h_sems = refs[2 * hn + 4:]
        t = pl.program_id(0)
        n = t % nb

        @pl.when(t == 0)
        def _():
            if hosted is not None:
                hosted.start(h_ins, h_outs, h_sems)
            _fill_bias_table(tbl, rows, keys_first=True)
            carry[...] = jnp.zeros_like(carry)

        @pl.when(t < steps)
        def _():
            kv2 = jnp.concatenate([_ld(kvp_ref), _ld(kvc_ref)], axis=0)
            lse_t, dl_t = _ld(lse_ref).T, _ld(dl_ref).T
            lo_mask = lax.broadcasted_iota(jnp.int32, (2 * BLK, LANES), 1) < HEAD_DIM
            halves = [jnp.zeros((2 * BLK, LANES), F32) for _ in range(4)]
            for hk in range(N_KV_HEADS):
                k_lo, k_hi, v_lo, v_hi = _head_operands(kv2, hk, lo_mask)
                cols = [slice(b * LANES, (b + 1) * LANES) for b in (2 * hk, 2 * hk + 1)]
                q2 = jnp.concatenate([_ld(q_ref, cols[0]), _ld(q_ref, cols[1])], axis=0).astype(BF16)
                do2 = jnp.concatenate([_ld(do_ref, cols[0]), _ld(do_ref, cols[1])], axis=0).astype(BF16)
                dq2 = jnp.zeros((2 * BLK, LANES), F32)
                dks, dvs = [], []
                for which, (kk, vv) in enumerate(((k_lo, v_lo), (k_hi, v_hi))):
                    h0, h1 = 4 * hk + which, 4 * hk + 2 + which
                    s = _nt(kk, q2) + _bias2(tbl, n, h0, h1, axis=1)
                    lse2 = jnp.concatenate([lse_t[h0:h0 + 1, :], lse_t[h1:h1 + 1, :]], axis=1)
                    dl2 = jnp.concatenate([dl_t[h0:h0 + 1, :], dl_t[h1:h1 + 1, :]], axis=1)
                    p = jnp.exp(s - lse2)
                    ds = (p * (_nt(vv, do2) - dl2)).astype(BF16)
                    dq2 = dq2 + _tn(ds, kk)
                    dks.append(jnp.dot(ds, q2, preferred_element_type=F32))
                    dvs.append(jnp.dot(p.astype(BF16), do2, preferred_element_type=F32))
                dk_sum = jnp.where(lo_mask, dks[0], dks[1])
                dv_sum = jnp.where(lo_mask, dvs[0], dvs[1])
                for jp in range(2):
                    dq_blk = dq2[jp * BLK:(jp + 1) * BLK]
                    if have_prev:
                        dq_blk = dq_blk + _ld(pdq_ref, cols[jp])
                    if final:
                        dq_blk = dq_blk * (HEAD_DIM ** -0.5)
                    _st(dq_ref, dq_blk.astype(out_dt), cols[jp])
                half, pos = hk // 2, hk % 2
                here = lo_mask if pos == 0 else jnp.logical_not(lo_mask)
                dk_tot = dk_sum + pltpu.roll(dk_sum, HEAD_DIM, axis=1)
                dv_tot = dv_sum + pltpu.roll(dv_sum, HEAD_DIM, axis=1)
                halves[half] = halves[half] + jnp.where(here, dk_tot, 0.0)
                halves[2 + half] = halves[2 + half] + jnp.where(here, dv_tot, 0.0)
            for b in range(4):
                cols = slice(b * LANES, (b + 1) * LANES)
                done = carry[:, cols] + halves[b][0:BLK, :]
                if have_prev:
                    done = done + _ld(pdkv_ref, cols)
                _st(dkv_ref, done.astype(out_dt), cols)
                carry[:, cols] = halves[b][BLK:, :]

        @pl.when(t == steps)
        def _():
            done = carry[...]
            if have_prev:
                done = done + _ld(pdkv_ref)
            _st(dkv_ref, done.astype(out_dt))
            if hosted is not None:
                hosted.finish(h_ins, h_outs, h_sems)

    def spec(width, lag):
        def index(t):
            u = jnp.clip(t - lag, 0, steps - 1)
            return rows.index(u // nb, u % nb)
        return pl.BlockSpec(rows.block + (width,), index)

    def key_prev(t):
        u = jnp.minimum(t, steps - 1)
        return rows.index(u // nb, jnp.maximum(u % nb - 1, 0))

    in_specs = [spec(ATT_W, 0), spec(ATT_W, 0), spec(LANES, 0), spec(LANES, 0), spec(2 * KV_W, 0),
                pl.BlockSpec(rows.block + (2 * KV_W,), key_prev)]
    args = [rows.of(q), rows.of(d_o), rows.of(lse), rows.of(delta), rows.of(kv), rows.of(kv)]
    if have_prev:
        in_specs += [spec(ATT_W, 0), spec(2 * KV_W, 1)]
        args += [rows.of(prev[0]), rows.of(prev[1])]
    out_specs = [spec(ATT_W, 0), spec(2 * KV_W, 1)]
    out_shape = [jax.ShapeDtypeStruct(rows.view + (ATT_W,), out_dt),
                 jax.ShapeDtypeStruct(rows.view + (2 * KV_W,), out_dt)]
    scratch = [pltpu.VMEM((BLK, 2 * KV_W), F32), pltpu.VMEM((2 * N_Q_HEADS, 2 * BLK, BLK), F32)]
    if hosted is not None:
        in_specs += [ANY_SPEC] * hn
        args += hosted.arrays
        out_specs += [ANY_SPEC] * hn
        out_shape += hosted.out_shapes()
        scratch += hosted.sem_shapes()
    res = pl.pallas_call(
        body, name=name, grid=(steps + 1,),
        in_specs=in_specs, out_specs=tuple(out_specs), out_shape=tuple(out_shape), scratch_shapes=scratch,
        compiler_params=_params(("arbitrary",)),
    )(*args)
    return (res[0].reshape(S, ATT_W), res[1].reshape(S, 2 * KV_W)), list(res[2:])


def _dh(segments, w_in, x, dx2, g, hosted=None, tm=1024, tk=512):
    S, D = x.shape
    ns = len(segments)
    counts = [a.shape[1] // tk for a, _ in segments]
    starts = [sum(counts[:s]) for s in range(ns)]
    nk = sum(counts)
    hn = hosted.n if hosted is not None else 0

    def body(*refs):
        seg_refs = refs[:ns]
        w_ref, x_ref, dx2_ref, g_ref = refs[ns:ns + 4]
        h_ins = refs[ns + 4:ns + 4 + hn]
        gx_ref, gng_ref = refs[ns + 4 + hn:ns + 6 + hn]
        h_outs = refs[ns + 6 + hn:ns + 6 + 2 * hn]
        acc = refs[ns + 6 + 2 * hn]
        h_sems = refs[ns + 7 + 2 * hn:]
        k, i = pl.program_id(0), pl.program_id(1)

        @pl.when((i == 0) & (k == 0))
        def _():
            gng_ref[...] = jnp.zeros_like(gng_ref)
            if hosted is not None:
                hosted.start(h_ins, h_outs, h_sems)

        @pl.when(k == 0)
        def _():
            acc[i] = jnp.zeros(acc.shape[1:], F32)

        for s in range(ns):
            @pl.when((k >= starts[s]) & (k < starts[s] + counts[s]))
            def _(s=s):
                t = seg_refs[s][...]
                if segments[s][1]:
                    t = _perm_rows(t, True)
                acc[i] += jnp.dot(t, w_ref[...], preferred_element_type=F32)

        @pl.when(k == nk - 1)
        def _():
            dh = acc[i]
            xf = x_ref[...]
            r = lax.rsqrt(jnp.mean(xf * xf, axis=-1, keepdims=True) + NORM_EPS)
            nrm = xf * r
            gng_ref[...] += jnp.sum(dh * nrm, axis=0, keepdims=True)
            dn = dh * g_ref[...]
            gx_ref[...] = dx2_ref[...] + r * (dn - nrm * jnp.mean(dn * nrm, axis=-1, keepdims=True))

        if hosted is not None:
            @pl.when((i == S // tm - 1) & (k == nk - 1))
            def _():
                hosted.finish(h_ins, h_outs, h_sems)

    ni = S // tm
    row = pl.BlockSpec((tm, D), lambda k, i: (jnp.where(k == nk - 1, i, 0), 0))
    vec = pl.BlockSpec((1, D), lambda k, i: (0, 0))

    def seg_index(s):
        def index(k, i):
            j = k - starts[s]
            return jnp.where(j < 0, 0, jnp.where(j >= counts[s], ni - 1, i)), jnp.clip(j, 0, counts[s] - 1)
        return index

    in_specs = [pl.BlockSpec((tm, tk), seg_index(s)) for s in range(ns)]
    in_specs += [pl.BlockSpec((tk, D), lambda k, i: (k, 0)), row, row, vec]
    args = [a for a, _ in segments] + [w_in, x, dx2, g]
    out_specs = [row, vec]
    out_shape = [jax.ShapeDtypeStruct((S, D), F32), jax.ShapeDtypeStruct((1, D), F32)]
    scratch = [pltpu.VMEM((ni, tm, D), F32)]
    if hosted is not None:
        in_specs += [ANY_SPEC] * hn
        args += hosted.arrays
        out_specs += [ANY_SPEC] * hn
        out_shape += hosted.out_shapes()
        scratch += hosted.sem_shapes()
    res = pl.pallas_call(
        body, name="dh", grid=(nk, S // tm),
        in_specs=in_specs, out_specs=tuple(out_specs), out_shape=tuple(out_shape), scratch_shapes=scratch,
        compiler_params=_params(("arbitrary", "arbitrary"), BIG_VMEM_LIMIT),
    )(*args)
    return res[0], res[1], list(res[2:])


def _tn_matmul(a, bs, name, b_first=False, tm=512):
    M, K = a.shape
    nb = len(bs)
    shapes = [(b.shape[1], K) if b_first else (K, b.shape[1]) for b in bs]

    def body(a_ref, *refs):
        @pl.when(pl.program_id(0) == 0)
        def _():
            for o_ref in refs[nb:]:
                o_ref[...] = jnp.zeros_like(o_ref)

        at = a_ref[...]
        for b_ref, o_ref in zip(refs[:nb], refs[nb:]):
            for c in range(0, b_ref.shape[1], 512):
                if b_first:
                    o_ref[c:c + 512, :] += _tn(b_ref[:, c:c + 512], at)
                else:
                    o_ref[:, c:c + 512] += _tn(at, b_ref[:, c:c + 512])

    return pl.pallas_call(
        body, name=name, grid=(M // tm,),
        in_specs=[pl.BlockSpec((tm, K), lambda m: (m, 0))] + [pl.BlockSpec((tm, b.shape[1]), lambda m: (m, 0))
                                                              for b in bs],
        out_specs=tuple(pl.BlockSpec(s, lambda m: (0, 0)) for s in shapes),
        out_shape=tuple(jax.ShapeDtypeStruct(s, F32) for s in shapes),
        compiler_params=_params(("arbitrary",)),
    )(a, *bs)


def _adamw(parts, w, m, v, name, tr=None, split=None, by_chip=False):
    R, C = w.shape
    tr = R if tr is None else tr
    parts = [parts] if split is None else list(parts)
    npar = len(parts)

    def total(p_ref):
        if by_chip:
            c = lax.axis_index("c")
            g = p_ref[c].astype(F32)
            for chip in range(1, N_DEV // 2):
                g = g + p_ref[2 * chip + c].astype(F32)
            return g
        g = p_ref[0].astype(F32)
        for dev in range(1, N_DEV):
            g = g + p_ref[dev].astype(F32)
        return g

    def body(*refs):
        w_ref, m_ref, v_ref, g_out, d_out, m_out, v_out = refs[npar:]
        if split is None:
            g = total(refs[0])
        else:
            g = jnp.where(_mesh_pos()[3] < split, total(refs[0]), total(refs[1]))
        mn = ADAM_B1 * m_ref[...] + (1.0 - ADAM_B1) * g
        vn = ADAM_B2 * v_ref[...] + (1.0 - ADAM_B2) * (g * g)
        m_hat = mn / (1.0 - ADAM_B1 ** ADAM_STEP)
        v_hat = vn / (1.0 - ADAM_B2 ** ADAM_STEP)
        g_out[...] = g
        d_out[...] = -ADAM_LR * (m_hat / (jnp.sqrt(v_hat) + ADAM_EPS) + ADAM_WD * w_ref[...])
        m_out[...] = mn
        v_out[...] = vn

    blk = pl.BlockSpec((tr, C), lambda i: (i, 0))
    shp = jax.ShapeDtypeStruct((R, C), F32)
    return pl.pallas_call(
        body, name=name, grid=(R // tr,),
        in_specs=[pl.BlockSpec((N_DEV, tr, C), lambda i: (0, i, 0))] * npar + [blk, blk, blk],
        out_specs=(blk, blk, blk, blk), out_shape=(shp, shp, shp, shp),
        compiler_params=_params(("parallel",)),
    )(*parts, w, m, v)


def _local_step(x, target, norm_g, w_in, conv_w, conv_b, ln_g, ln_b, w_out, gf, exchanges=None, late_weights=None):
    ex_out, ex_att, ex_conv = exchanges if exchanges is not None else (None, None, None)
    conv_cols = w_in.shape[0] - 2 * ATT_W - 2 * KV_W
    q, kv, a_gate, gates, h_rm, h, *gathered = _inproj(
        x, norm_g, w_in,
        [(ATT_W, HEAD_DIM ** -0.5, True, F32), (2 * KV_W, 1.0, True, F32), (ATT_W, 1.0, True, BF16),
         (conv_cols, 1.0, False, BF16)], late_weights[0] if late_weights is not None else None)
    if late_weights is not None:
        conv_w, w_out = late_weights[1](gathered)

    merged = None
    for idx, (_, dil) in enumerate(reversed(PATTERNS)):
        merged = _attn_fwd(q, kv, dil, "attn_fwd_d%d" % dil, merged,
                           a_gate if idx == len(PATTERNS) - 1 else None)
    o, lse, y_att = merged
    conv_out, y_conv = _conv_fwd(gates, conv_w, conv_b, ln_g, ln_b)
    dx2, dxb, loss_cols, g_gf = _outproj_loss(x, y_att, y_conv, w_out, gf, target)

    d_o, d_a_gate, delta, dxb_rm = _dy_att(dxb, w_out, a_gate, o)
    g_w_out = jnp.concatenate([_tn_matmul(y_att, [dxb_rm], "gw_out_att")[0],
                               _tn_matmul(y_conv, [dxb], "gw_out_conv")[0]], axis=0)
    acc, out_parts = None, []
    for idx, (_, dil) in enumerate(reversed(PATTERNS)):
        hosted = ex_out(g_w_out) if (idx == 0 and ex_out is not None) else None
        acc, outs = _attn_bwd(q, kv, d_o, lse, delta, dil, acc, idx == len(PATTERNS) - 1, "attn_bwd_d%d" % dil,
                              hosted)
        out_parts += outs
    dq, dkv = acc
    g_q, g_kv, g_a = _tn_matmul(h_rm, [dq, dkv, d_a_gate], "gw_in_att", b_first=True)

    d_c_gate, d_conv, g_ln_g, g_ln_b, g_conv_b = _dy_conv(dxb, w_out, gates, conv_out, ln_g, ln_b)
    dgates, g_conv_w, att_parts = _conv_bwd(d_conv, gates, d_c_gate, conv_w,
                                            ex_att(g_q, g_kv, g_a) if ex_att is not None else None)
    g_c, = _tn_matmul(h, [dgates], "gw_in_conv", b_first=True)
    grad_x, g_norm_g, conv_parts = _dh(
        [(dq, True), (dkv, True), (d_a_gate, True), (dgates, False)], w_in, x, dx2, norm_g,
        ex_conv(g_a, g_c, g_conv_w) if ex_conv is not None else None)
    small = (g_norm_g, g_conv_b, g_ln_g, g_ln_b, g_gf, loss_cols)
    return grad_x, (g_q, g_kv, g_a, g_c), g_w_out, g_conv_w, small, (out_parts, att_parts, conv_parts)


def kernel(x, norm_g, w_in, conv_w, conv_b, conv_ln_g, conv_ln_b, w_out, final_norm_g, loss_target, m_norm_g, m_w_in, m_conv_w, m_conv_b, m_conv_ln_g, m_conv_ln_b, m_w_out, m_final_norm_g, v_norm_g, v_w_in, v_conv_w, v_conv_b, v_conv_ln_g, v_conv_ln_b, v_w_out, v_final_norm_g):
    S, D = x.shape[1], x.shape[2]
    win_sh, wout_sh, cw_sh = w_in[0].T, w_out[0], conv_w[0]
    cols_sh, rows_sh, ch_sh = win_sh.shape[0], wout_sh.shape[0], cw_sh.shape[1]

    win_all, = _gather_two_level([win_sh.astype(BF16)], "gather_w_in")
    w_in_full = win_all.reshape(N_DEV * cols_sh, D)

    def late_weights(gathered):
        wout_all, cw_all = gathered
        conv_w_full = cw_all.transpose(1, 0, 2).reshape(CONV_K, N_DEV * ch_sh)
        return jnp.pad(conv_w_full, ((0, CONV_HALO - CONV_K), (0, 0))), wout_all.reshape(N_DEV * rows_sh, D)

    gf = final_norm_g.reshape(1, D)

    first = -(-(ATT_W + 2 * KV_W) // cols_sh)
    a_off = first * cols_sh - (ATT_W + 2 * KV_W)
    assert 0 <= a_off <= ATT_W

    def pieces(parts, n):
        return jnp.concatenate([p.astype(BF16) for p in parts], axis=0).reshape(n, cols_sh, D)

    def ex_out(g_w_out):
        return _Exchange([g_w_out.reshape(N_DEV, rows_sh, D).astype(BF16)], [(0, N_DEV)])

    same_core = (2, 4, 6)

    def ex_att(g_q, g_kv, g_a):
        mine = _chip_sum(pieces([g_q, g_kv, g_a[:a_off]], first), 0, "rs_att")
        return _Exchange([mine], [(0, first)], [same_core])

    def ex_conv(g_a, g_c, g_conv_w):
        mine = _chip_sum(pieces([g_a[a_off:], g_c], N_DEV - first), first, "rs_conv")
        return _Exchange(
            [mine, g_conv_w[:CONV_K].reshape(CONV_K, N_DEV, ch_sh).transpose(1, 0, 2)],
            [(first, N_DEV), (0, N_DEV)], [same_core, None])

    grad_x, _, _, _, small, parts = _local_step(
        x[0], loss_target[0], norm_g, w_in_full, None, conv_b, conv_ln_g, conv_ln_b, None, gf,
        (ex_out, ex_att, ex_conv), (_Gather([wout_sh.astype(BF16), cw_sh]), late_weights))
    (wout_parts,), (win_parts_lo,), (win_parts_hi, cw_parts) = parts

    small_pack = jnp.concatenate(list(small) + [jnp.zeros((2, D), F32)], axis=0)
    small_parts, = _exchange([small_pack], [None], "gather_small")

    upd_win = _adamw((win_parts_lo, win_parts_hi), win_sh, m_w_in[0].T, v_w_in[0].T, "adamw_w_in",
                     tr=cols_sh // 2, split=first, by_chip=True)
    upd_wout = _adamw(wout_parts, wout_sh, m_w_out[0], v_w_out[0], "adamw_w_out", tr=128)
    upd_cw = _adamw(cw_parts, cw_sh, m_conv_w[0], v_conv_w[0], "adamw_conv_w")
    zeros3 = jnp.zeros((3, D), F32)
    stack = lambda a, b, c, d_, e: jnp.concatenate([a, b, c, d_, e.reshape(1, D), zeros3], axis=0)
    upd_small = _adamw(
        small_parts,
        stack(norm_g, conv_b, conv_ln_g, conv_ln_b, final_norm_g),
        stack(m_norm_g, m_conv_b, m_conv_ln_g, m_conv_ln_b, m_final_norm_g),
        stack(v_norm_g, v_conv_b, v_conv_ln_g, v_conv_ln_b, v_final_norm_g) + jnp.concatenate(
            [jnp.zeros((5, D), F32), jnp.ones((3, D), F32)], axis=0),
        "adamw_small")

    loss = 0.5 / D * jnp.sum(upd_small[0][5])

    def outputs(kind):
        sm = upd_small[kind]
        return [sm[0:1], upd_win[kind].T[None], upd_cw[kind][None], sm[1:2], sm[2:3], sm[3:4],
                upd_wout[kind][None], sm[4]]

    return (loss, grad_x[None], *outputs(0), *outputs(1), *outputs(2), *outputs(3))
```

```python
import jax
import jax.numpy as jnp
from jax import lax
from jax.experimental import pallas as pl
from jax.experimental.pallas import tpu as pltpu

F32 = jnp.float32
BF16 = jnp.bfloat16

HEAD_DIM = 64
N_KV_HEADS = 4
N_Q_HEADS = 16
ATT_W = 1024
KV_W = 256
CONV_K = 31
CONV_HALO = 32
PATTERNS = ((128, 1), (512, 4), (2048, 16))
BLK = 128
LANES = 128
NORM_EPS = 1e-6
LN_EPS = 1e-5
NEG = -1e30
N_DEV = 8
ADAM_LR, ADAM_B1, ADAM_B2, ADAM_EPS, ADAM_WD, ADAM_STEP = 0.001, 0.9, 0.999, 1e-08, 0.01, 10
VMEM_LIMIT = 48 * 1024 * 1024
BIG_VMEM_LIMIT = 58 * 1024 * 1024
SLOPES = tuple(2.0 ** (-8.0 * (h + 1) / N_Q_HEADS) for h in range(N_Q_HEADS))
MESH = pl.DeviceIdType.MESH


def _params(sem, vmem_limit=VMEM_LIMIT):
    return pltpu.CompilerParams(dimension_semantics=sem, vmem_limit_bytes=vmem_limit)


def _sigmoid(v):
    return 1.0 / (1.0 + jnp.exp(-v))


def _silu_and_grad(v):
    s = _sigmoid(v)
    return v * s, s * (1.0 + v * (1.0 - s))


ANY_SPEC = pl.BlockSpec(memory_space=pl.ANY)


def _mesh_pos():
    x, y, c = lax.axis_index("x"), lax.axis_index("y"), lax.axis_index("c")
    return x, y, c, 4 * x + 2 * y + c


def _flipped(k, x, y, c):
    px = 1 - x if k & 4 else x
    py = 1 - y if k & 2 else y
    pc = 1 - c if k & 1 else c
    return (px, py, pc), 4 * px + 2 * py + pc


class _Exchange:
    def __init__(self, arrays, dests, flips=None):
        self.arrays, self.dests, self.n = list(arrays), list(dests), len(arrays)
        self.flips = [tuple(range(1, N_DEV)) if f is None else tuple(f)
                      for f in (flips if flips is not None else [None] * self.n)]

    def out_shapes(self):
        return [jax.ShapeDtypeStruct((N_DEV,) + a.shape[-2:], a.dtype) for a in self.arrays]

    def sem_shapes(self):
        return [pltpu.SemaphoreType.DMA((self.n, N_DEV - 1)), pltpu.SemaphoreType.DMA((self.n, N_DEV - 1)),
                pltpu.SemaphoreType.DMA((self.n,))]

    def _when(self, a, dev, fn):
        if self.dests[a] is None:
            fn()
        else:
            lo, hi = self.dests[a]
            pl.when((dev >= lo) & (dev < hi))(fn)

    def _mine(self, ins, a, dev):
        return ins[a] if self.dests[a] is None else ins[a].at[dev - self.dests[a][0]]

    def _copy(self, ins, outs, sems, a, k, src_dev, slot, target):
        return pltpu.make_async_remote_copy(
            src_ref=self._mine(ins, a, src_dev), dst_ref=outs[a].at[slot],
            send_sem=sems[0].at[a, k - 1], recv_sem=sems[1].at[a, k - 1],
            device_id=target, device_id_type=MESH)

    def start(self, ins, outs, sems):
        x, y, c, me = _mesh_pos()
        for a in range(self.n):
            self._when(a, me, lambda a=a: pltpu.make_async_copy(
                self._mine(ins, a, me), outs[a].at[me], sems[2].at[a]).start())
            for k in self.flips[a]:
                target, peer = _flipped(k, x, y, c)
                self._when(a, peer, lambda a=a, k=k, target=target, peer=peer: self._copy(
                    ins, outs, sems, a, k, peer, me, target).start())

    def finish(self, ins, outs, sems):
        x, y, c, me = _mesh_pos()
        lo0 = [0 if d is None else d[0] for d in self.dests]
        for a in range(self.n):
            for k in self.flips[a]:
                target, peer = _flipped(k, x, y, c)
                self._when(a, me, lambda a=a, k=k, peer=peer: self._copy(
                    ins, outs, sems, a, k, lo0[a], peer, (x, y, c)).wait_recv())
            for k in self.flips[a]:
                target, peer = _flipped(k, x, y, c)
                self._when(a, peer, lambda a=a, k=k, target=target, peer=peer: self._copy(
                    ins, outs, sems, a, k, peer, me, target).wait_send())
            self._when(a, me, lambda a=a: pltpu.make_async_copy(
                self._mine(ins, a, me), outs[a].at[me], sems[2].at[a]).wait())


def _exchange(arrays, dests, name, flips=None):
    ex = _Exchange(arrays, dests, flips)
    na = ex.n

    def body(*refs):
        ins, outs, sems = refs[:na], refs[na:2 * na], refs[2 * na:]
        ex.start(ins, outs, sems)
        ex.finish(ins, outs, sems)

    return pl.pallas_call(
        body, name=name, out_shape=tuple(ex.out_shapes()),
        in_specs=[ANY_SPEC] * na, out_specs=tuple([ANY_SPEC] * na), scratch_shapes=ex.sem_shapes(),
    )(*arrays)


def _chip_sum(pieces, lo, name):
    n, R, C = pieces.shape

    def swap(p_ref, t_ref, send_sems, recv_sems):
        x, y, c, me = _mesh_pos()
        for i in range(n):
            mine = (lo + i) % 2
            cp = pltpu.make_async_remote_copy(
                src_ref=p_ref.at[i], dst_ref=t_ref.at[i], send_sem=send_sems.at[i], recv_sem=recv_sems.at[i],
                device_id=(x, y, 1 - c), device_id_type=MESH)
            pl.when(c != mine)(cp.start)
        for i in range(n):
            mine = (lo + i) % 2
            cp = pltpu.make_async_remote_copy(
                src_ref=p_ref.at[i], dst_ref=t_ref.at[i], send_sem=send_sems.at[i], recv_sem=recv_sems.at[i],
                device_id=(x, y, 1 - c), device_id_type=MESH)
            pl.when(c == mine)(cp.wait_recv)
            pl.when(c != mine)(cp.wait_send)

    other = pl.pallas_call(
        swap, name=name + "_swap", out_shape=jax.ShapeDtypeStruct(pieces.shape, pieces.dtype),
        in_specs=[ANY_SPEC], out_specs=ANY_SPEC,
        scratch_shapes=[pltpu.SemaphoreType.DMA((n,)), pltpu.SemaphoreType.DMA((n,))],
    )(pieces)

    def add(p_ref, t_ref, o_ref):
        o_ref[...] = (p_ref[...].astype(F32) + t_ref[...].astype(F32)).astype(o_ref.dtype)

    tr = R // 2
    blk = pl.BlockSpec((None, tr, C), lambda i, r: (i, r, 0))
    return pl.pallas_call(
        add, name=name + "_add", grid=(n, R // tr), in_specs=[blk, blk], out_specs=blk,
        out_shape=jax.ShapeDtypeStruct(pieces.shape, pieces.dtype),
        compiler_params=_params(("parallel", "parallel")),
    )(pieces, other)


class _Gather:
    def __init__(self, arrays):
        self.arrays, self.n = list(arrays), len(arrays)

    def out_shapes(self):
        return [jax.ShapeDtypeStruct((N_DEV,) + a.shape, a.dtype) for a in self.arrays]

    def sem_shapes(self):
        return [pltpu.SemaphoreType.DMA((self.n, N_DEV - 1)), pltpu.SemaphoreType.DMA((self.n, N_DEV - 1)),
                pltpu.SemaphoreType.DMA((self.n,))]

    def _plan(self, ins, outs, sems):
        x, y, c, me = _mesh_pos()
        chips = [(1 - x, y), (x, 1 - y), (1 - x, 1 - y)]

        def copy(a, k, src, block, to):
            px, py, pc = block
            return pltpu.make_async_remote_copy(
                src_ref=src, dst_ref=outs[a].at[4 * px + 2 * py + pc], send_sem=sems[0].at[a, k],
                recv_sem=sems[1].at[a, k], device_id=to, device_id_type=MESH)

        local = [pltpu.make_async_copy(ins[a], outs[a].at[me], sems[2].at[a]) for a in range(self.n)]
        first = []
        for a in range(self.n):
            first.append(copy(a, 0, ins[a], (x, y, c), (x, y, 1 - c)))
            first += [copy(a, 1 + j, ins[a], (x, y, c), (*chip, c)) for j, chip in enumerate(chips)]
        return (x, y, c), chips, copy, local, first

    def start(self, ins, outs, sems):
        _, _, _, local, first = self._plan(ins, outs, sems)
        for cp in local + first:
            cp.start()

    def finish(self, ins, outs, sems):
        (x, y, c), chips, copy, local, first = self._plan(ins, outs, sems)
        passed = []
        for j, chip in enumerate(chips):
            for a in range(self.n):
                copy(a, 1 + j, ins[a], (*chip, c), (x, y, c)).wait_recv()
                px, py = chip
                fwd = copy(a, 4 + j, outs[a].at[4 * px + 2 * py + c], (*chip, c), (x, y, 1 - c))
                fwd.start()
                passed.append(fwd)
        for a in range(self.n):
            copy(a, 0, ins[a], (x, y, 1 - c), (x, y, c)).wait_recv()
            for j, chip in enumerate(chips):
                copy(a, 4 + j, ins[a], (*chip, 1 - c), (x, y, c)).wait_recv()
        for cp in first + passed:
            cp.wait_send()
        for cp in local:
            cp.wait()


def _gather_two_level(arrays, name):
    ga = _Gather(arrays)
    na = ga.n

    def body(*refs):
        ins, outs, sems = refs[:na], refs[na:2 * na], refs[2 * na:]
        ga.start(ins, outs, sems)
        ga.finish(ins, outs, sems)

    return pl.pallas_call(
        body, name=name, out_shape=tuple(ga.out_shapes()),
        in_specs=[ANY_SPEC] * na, out_specs=tuple([ANY_SPEC] * na), scratch_shapes=ga.sem_shapes(),
    )(*arrays)


CHUNK = 128
RESIDUES = 16
PER_RES = CHUNK // RESIDUES


def _perm_rows(tile, inverse):
    a = lax.broadcasted_iota(jnp.int32, (CHUNK, CHUNK), 0)
    b = lax.broadcasted_iota(jnp.int32, (CHUNK, CHUNK), 1)
    if inverse:
        a, b = b, a
    p = jnp.where(a == PER_RES * (b % RESIDUES) + b // RESIDUES, 1.0, 0.0).astype(BF16)
    parts = [jnp.dot(p, tile[c * CHUNK:(c + 1) * CHUNK], preferred_element_type=F32)
             for c in range(tile.shape[0] // CHUNK)]
    return jnp.concatenate(parts, axis=0).astype(BF16)


class _Rows:
    def __init__(self, dil, S):
        nc = S // CHUNK
        self.dil = dil
        if dil == 1:
            self.view, self.block, self.nb = (nc, CHUNK), (None, CHUNK), nc
            self.index = lambda r, b: (b, 0, 0)
        elif dil == 4:
            self.view, self.block, self.nb = (nc, 4, 4, PER_RES), (4, 4, None, PER_RES), nc // 4
            self.index = lambda r, b: (b, 0, r, 0, 0)
        elif dil == RESIDUES:
            self.view, self.block, self.nb = (nc, RESIDUES, PER_RES), (RESIDUES, None, PER_RES), nc // RESIDUES
            self.index = lambda r, b: (b, r, 0, 0)
        else:
            raise NotImplementedError(dil)

    def of(self, a):
        return a.reshape(self.view + (a.shape[-1],))

    def spec(self, width, which_block):
        return pl.BlockSpec(self.block + (width,), lambda r, n: self.index(r, which_block(n)))

    def pos(self, row):
        if self.dil == 1:
            return (row % PER_RES) * RESIDUES + row // PER_RES
        if self.dil == 4:
            return (row // 32) * 32 + (row % PER_RES) * 4 + (row % 32) // PER_RES
        return row


def _ld(ref, cols=slice(None)):
    v = ref[(slice(None),) * (len(ref.shape) - 1) + (cols,)]
    return v.reshape(BLK, v.shape[-1])


def _st(ref, val, cols=slice(None)):
    ref[(slice(None),) * (len(ref.shape) - 1) + (cols,)] = val.reshape(ref.shape[:-1] + (val.shape[-1],))


def _norm_rows(x, g, hosted=None, tm=512):
    S, D = x.shape
    hn = hosted.n if hosted is not None else 0

    def body(x_ref, g_ref, *rest):
        h_ins = rest[:hn]
        hrm_out, h_out = rest[hn:hn + 2]
        h_outs = rest[hn + 2:2 * hn + 2]
        h_sems = rest[2 * hn + 2:]
        i = pl.program_id(0)
        if hosted is not None:
            pl.when(i == 0)(lambda: hosted.start(h_ins, h_outs, h_sems))
        xf = x_ref[...]
        r = lax.rsqrt(jnp.mean(xf * xf, axis=-1, keepdims=True) + NORM_EPS)
        h = (xf * r * g_ref[...]).astype(BF16)
        h_out[...] = h
        hrm_out[...] = _perm_rows(h, False)
        if hosted is not None:
            pl.when(i == S // tm - 1)(lambda: hosted.finish(h_ins, h_outs, h_sems))

    row = pl.BlockSpec((tm, D), lambda i: (i, 0))
    in_specs, args = [row, pl.BlockSpec((1, D), lambda i: (0, 0))], [x, g]
    out_specs, out_shape, scratch = [row, row], [jax.ShapeDtypeStruct((S, D), BF16)] * 2, []
    if hosted is not None:
        in_specs += [ANY_SPEC] * hn
        args += hosted.arrays
        out_specs += [ANY_SPEC] * hn
        out_shape += hosted.out_shapes()
        scratch += hosted.sem_shapes()
    return pl.pallas_call(
        body, name="norm_rows", grid=(S // tm,),
        in_specs=in_specs, out_specs=tuple(out_specs), out_shape=tuple(out_shape), scratch_shapes=scratch,
        compiler_params=_params(("arbitrary",)),
    )(*args)


def _inproj(h_rm, h, w_t, segments, hosted=None, tm=1024, tn=512):
    S, D = h.shape
    ns = len(segments)
    ni = S // tm
    counts = [seg[0] // tn for seg in segments]
    starts = [sum(counts[:s]) for s in range(ns)]

    hn = hosted.n if hosted is not None else 0
    last_p = sum(counts)

    def body(hrm_ref, h_ref, w_ref, *rest):
        h_ins, rest = rest[:hn], rest[hn:]
        outs = rest[:ns]
        h_outs = rest[ns:ns + hn]
        hrm_scr, h_scr = rest[ns + hn:ns + 2 + hn]
        h_sems = rest[ns + 2 + hn:]
        p, i = pl.program_id(0), pl.program_id(1)

        if hosted is not None:
            @pl.when((p == 0) & (i == 0))
            def _():
                hosted.start(h_ins, h_outs, h_sems)

            @pl.when((p == last_p) & (i == ni - 1))
            def _():
                hosted.finish(h_ins, h_outs, h_sems)

        @pl.when(p == 0)
        def _():
            h_scr[i] = h_ref[...]
            hrm_scr[i] = hrm_ref[...]

        for s, (_, scale, rm, dtype) in enumerate(segments):
            @pl.when((p > starts[s]) & (p <= starts[s] + counts[s]))
            def _(s=s, scale=scale, rm=rm, dtype=dtype):
                acc = _nt((hrm_scr if rm else h_scr)[i], w_ref[...])
                outs[s][...] = (acc * scale if scale != 1.0 else acc).astype(dtype)

    def out_index(s):
        def index(p, i):
            j = p - 1 - starts[s]
            row = jnp.where(j < 0, 0, jnp.where(j >= counts[s], ni - 1, i))
            return row, jnp.clip(j, 0, counts[s] - 1)
        return index

    first_pass = pl.BlockSpec((tm, D), lambda p, i: (jnp.where(p == 0, i, ni - 1), 0))
    out_specs = [pl.BlockSpec((tm, tn), out_index(s)) for s in range(ns)]
    out_shape = [jax.ShapeDtypeStruct((S, seg[0]), seg[3]) for seg in segments]
    in_specs = [first_pass, first_pass, pl.BlockSpec((tn, D), lambda p, i: (jnp.maximum(p - 1, 0), 0))]
    args = [h_rm, h, w_t]
    scratch = [pltpu.VMEM((ni, tm, D), BF16), pltpu.VMEM((ni, tm, D), BF16)]
    if hosted is not None:
        in_specs += [ANY_SPEC] * hn
        args += hosted.arrays
        out_specs += [ANY_SPEC] * hn
        out_shape += hosted.out_shapes()
        scratch += hosted.sem_shapes()
    return pl.pallas_call(
        body, name="inproj", grid=(1 + last_p, ni),
        in_specs=in_specs, out_specs=tuple(out_specs), out_shape=tuple(out_shape), scratch_shapes=scratch,
        compiler_params=_params(("arbitrary", "arbitrary"), BIG_VMEM_LIMIT),
    )(*args)


def _fill_bias_table(tbl, rows, keys_first=False):
    shape = (2 * BLK, BLK) if keys_first else (BLK, 2 * BLK)
    qi = lax.broadcasted_iota(jnp.int32, shape, 1 if keys_first else 0)
    kj = lax.broadcasted_iota(jnp.int32, shape, 0 if keys_first else 1)
    dist = rows.pos(qi) - rows.pos(kj % BLK) + jnp.where(kj < BLK, BLK, 0)
    inside = (dist >= 0) & (dist <= BLK)
    negd = (dist * (-rows.dil)).astype(F32)
    for f, valid in enumerate((inside & (kj >= BLK), inside)):
        for h in range(N_Q_HEADS):
            tbl[f * N_Q_HEADS + h] = jnp.where(valid, SLOPES[h] * negd, NEG)


def _bias2(tbl, n, h0, h1, axis=0):
    base = jnp.where(n == 0, 0, N_Q_HEADS)
    return jnp.concatenate([tbl[base + h0], tbl[base + h1]], axis=axis)


def _head_operands(kv2, hk, lo_mask):
    half, pos = hk // 2, hk % 2
    out = []
    for base in (0, KV_W):
        t = kv2[:, base + half * LANES: base + (half + 1) * LANES]
        sw = pltpu.roll(t, HEAD_DIM, axis=1)
        at_lo, at_hi = (t, sw) if pos == 0 else (sw, t)
        out.append(jnp.where(lo_mask, at_lo, 0.0).astype(BF16))
        out.append(jnp.where(lo_mask, 0.0, at_hi).astype(BF16))
    return out


def _nt(a, b):
    return lax.dot_general(a, b, (((1,), (1,)), ((), ())), preferred_element_type=F32)


def _tn(a, b):
    return lax.dot_general(a, b, (((0,), (0,)), ((), ())), preferred_element_type=F32)


def _attn_fwd(q, kv, dil, name, prev=None, gate=None):
    S = q.shape[0]
    rows = _Rows(dil, S)
    nb = rows.nb
    have_prev, last = prev is not None, gate is not None

    def body(*refs):
        refs = list(refs)
        q_ref, kvc_ref, kvp_ref = refs[:3]
        del refs[:3]
        if have_prev:
            po_ref, pl_ref = refs[:2]
            del refs[:2]
        if last:
            gate_ref = refs.pop(0)
        o_ref, lse_ref = refs[:2]
        y_ref = refs[2] if last else None
        tbl = refs[-1]
        n = pl.program_id(1)

        @pl.when((pl.program_id(0) == 0) & (n == 0))
        def _():
            _fill_bias_table(tbl, rows)

        kv2 = jnp.concatenate([_ld(kvp_ref), _ld(kvc_ref)], axis=0)
        lo_mask = lax.broadcasted_iota(jnp.int32, (2 * BLK, LANES), 1) < HEAD_DIM
        lane = lax.broadcasted_iota(jnp.int32, (BLK, LANES), 1)
        stats = jnp.zeros((BLK, LANES), F32)
        for hk in range(N_KV_HEADS):
            k_lo, k_hi, v_lo, v_hi = _head_operands(kv2, hk, lo_mask)
            cols = [slice(b * LANES, (b + 1) * LANES) for b in (2 * hk, 2 * hk + 1)]
            q2 = jnp.concatenate([_ld(q_ref, cols[0]), _ld(q_ref, cols[1])], axis=0).astype(BF16)
            o2 = jnp.zeros((2 * BLK, LANES), F32)
            for which, (kk, vv) in enumerate(((k_lo, v_lo), (k_hi, v_hi))):
                h0, h1 = 4 * hk + which, 4 * hk + 2 + which
                s = _nt(q2, kk) + _bias2(tbl, n, h0, h1)
                m = jnp.max(s, axis=1, keepdims=True)
                p = jnp.exp(s - m)
                l = jnp.sum(p, axis=1, keepdims=True)
                o2 = o2 + jnp.dot(p.astype(BF16), vv, preferred_element_type=F32) * (1.0 / l)
                lse = m + jnp.log(l)
                stats = jnp.where(lane == h0, lse[0:BLK], stats)
                stats = jnp.where(lane == h1, lse[BLK:], stats)
            _st(o_ref, o2[0:BLK], cols[0])
            _st(o_ref, o2[BLK:], cols[1])
        if have_prev:
            before = _ld(pl_ref)
            top = jnp.maximum(before, stats)
            e_old, e_new = jnp.exp(before - top), jnp.exp(stats - top)
            total = e_old + e_new
            stats = top + jnp.log(total)
            inv = 1.0 / total
            w_old, w_new = e_old * inv, e_new * inv
        if have_prev or last:
            lo = lane < HEAD_DIM
            for blk in range(ATT_W // LANES):
                cols = slice(blk * LANES, (blk + 1) * LANES)
                o_blk = _ld(o_ref, cols)
                if have_prev:
                    pick = lambda w: jnp.where(lo, w[:, 2 * blk:2 * blk + 1], w[:, 2 * blk + 1:2 * blk + 2])
                    o_blk = o_blk * pick(w_new) + _ld(po_ref, cols) * pick(w_old)
                    _st(o_ref, o_blk, cols)
                if last:
                    a = _ld(gate_ref, cols)
                    _st(y_ref, (o_blk * (a * _sigmoid(a))).astype(BF16), cols)
        _st(lse_ref, stats)

    here = lambda n: n
    before_n = lambda n: jnp.maximum(n - 1, 0)
    in_specs = [rows.spec(ATT_W, here), rows.spec(2 * KV_W, here), rows.spec(2 * KV_W, before_n)]
    args = [rows.of(q), rows.of(kv), rows.of(kv)]
    if have_prev:
        in_specs += [rows.spec(ATT_W, here), rows.spec(LANES, here)]
        args += [rows.of(prev[0]), rows.of(prev[1])]
    out_specs = [rows.spec(ATT_W, here), rows.spec(LANES, here)]
    out_shape = [jax.ShapeDtypeStruct(rows.view + (ATT_W,), F32), jax.ShapeDtypeStruct(rows.view + (LANES,), F32)]
    if last:
        in_specs.append(rows.spec(ATT_W, here))
        args.append(rows.of(gate))
        out_specs.append(rows.spec(ATT_W, here))
        out_shape.append(jax.ShapeDtypeStruct(rows.view + (ATT_W,), BF16))
    res = pl.pallas_call(
        body, name=name, grid=(dil, nb),
        in_specs=in_specs, out_specs=tuple(out_specs), out_shape=tuple(out_shape),
        scratch_shapes=[pltpu.VMEM((2 * N_Q_HEADS, BLK, 2 * BLK), F32)],
        compiler_params=_params(("arbitrary", "arbitrary")),
    )(*args)
    return tuple(r.reshape(S, r.shape[-1]) for r in res)


def _shifted_copies(buf, phases):
    n = phases.shape[1]
    for b in range(1, 8):
        phases[b - 1] = buf[b:b + n, :]


def _window(buf, phases, start, cols):
    b = start % 8
    if b == 0:
        return buf[start:start + 8, cols]
    return phases[b - 1, start - b:start - b + 8, cols]


def _broadcast_taps(w_ref, wb):
    for j in range(CONV_K):
        wb[j] = jnp.broadcast_to(w_ref[j:j + 1, :], wb.shape[1:])


def _conv_fwd(gates, conv_w, conv_b, ln_g, ln_b, tt=256):
    S = gates.shape[0]
    C = conv_w.shape[1]
    hb = tt // CONV_HALO

    def body(val_ref, glu_ref, hval_ref, hglu_ref, gate_ref, w_ref, b_ref, g_ref, beta_ref,
             conv_ref, y_ref, hbuf, hph):
        i = pl.program_id(0)
        halo = hval_ref[...] * _sigmoid(hglu_ref[...])
        hbuf[0:CONV_HALO, :] = jnp.where(i > 0, halo, 0.0)
        hbuf[CONV_HALO:, :] = val_ref[...] * _sigmoid(glu_ref[...])
        _shifted_copies(hbuf, hph)
        for cb in range(C // LANES):
            cols = slice(cb * LANES, (cb + 1) * LANES)
            wj = [jnp.broadcast_to(w_ref[j:j + 1, cols], (8, LANES)) for j in range(CONV_K)]
            for rc in range(tt // 8):
                acc = jnp.zeros((8, LANES), F32)
                for j in range(CONV_K):
                    start = rc * 8 + CONV_HALO - (CONV_K - 1) + j
                    acc = acc + _window(hbuf, hph, start, cols) * wj[j]
                conv_ref[rc * 8:(rc + 1) * 8, cols] = acc
        cv = conv_ref[...] + b_ref[...]
        conv_ref[...] = cv
        mu = jnp.mean(cv, axis=-1, keepdims=True)
        xc = cv - mu
        var = jnp.mean(xc * xc, axis=-1, keepdims=True)
        ln = xc * lax.rsqrt(var + LN_EPS) * g_ref[...] + beta_ref[...]
        gt = gate_ref[...]
        y_ref[...] = (ln * _sigmoid(ln) * (gt * _sigmoid(gt))).astype(BF16)

    vec = pl.BlockSpec((1, C), lambda i: (0, 0))
    return pl.pallas_call(
        body, name="conv_fwd", grid=(S // tt,),
        in_specs=[pl.BlockSpec((tt, C), lambda i: (i, 0)),
                  pl.BlockSpec((tt, C), lambda i: (i, 1)),
                  pl.BlockSpec((CONV_HALO, C), lambda i: (jnp.maximum(i * hb - 1, 0), 0)),
                  pl.BlockSpec((CONV_HALO, C), lambda i: (jnp.maximum(i * hb - 1, 0), 1)),
                  pl.BlockSpec((tt, C), lambda i: (i, 2)),
                  pl.BlockSpec((CONV_HALO, C), lambda i: (0, 0)), vec, vec, vec],
        out_specs=(pl.BlockSpec((tt, C), lambda i: (i, 0)), pl.BlockSpec((tt, C), lambda i: (i, 0))),
        out_shape=(jax.ShapeDtypeStruct((S, C), F32), jax.ShapeDtypeStruct((S, C), BF16)),
        scratch_shapes=[pltpu.VMEM((tt + CONV_HALO, C), F32), pltpu.VMEM((7, tt + CONV_HALO - 8, C), F32)],
        compiler_params=_params(("parallel",)),
    )(gates, gates, gates, gates, gates, conv_w, conv_b, ln_g, ln_b)


def _outproj_loss(x, y_att, y_conv, w_out, gf, target, tm=512):
    S, D = x.shape
    E = y_att.shape[1]

    def body(x_ref, ya_ref, yc_ref, w_ref, gf_ref, t_ref, dx_ref, dxb_ref, loss_ref, ggf_ref):
        @pl.when(pl.program_id(0) == 0)
        def _():
            loss_ref[...] = jnp.zeros_like(loss_ref)
            ggf_ref[...] = jnp.zeros_like(ggf_ref)

        x2 = (x_ref[...] + jnp.dot(_perm_rows(ya_ref[...], True), w_ref[0:E, :], preferred_element_type=F32)
              + jnp.dot(yc_ref[...], w_ref[E:, :], preferred_element_type=F32))
        r = lax.rsqrt(jnp.mean(x2 * x2, axis=-1, keepdims=True) + NORM_EPS)
        nrm = x2 * r
        gfv = gf_ref[...]
        err = nrm * gfv - t_ref[...]
        loss_ref[...] += jnp.sum(err * err, axis=0, keepdims=True)
        dout = err * (1.0 / D)
        ggf_ref[...] += jnp.sum(dout * nrm, axis=0, keepdims=True)
        dn = dout * gfv
        dx2 = r * (dn - nrm * jnp.mean(dn * nrm, axis=-1, keepdims=True))
        dx_ref[...] = dx2
        dxb_ref[...] = dx2.astype(BF16)

    row = lambda w: pl.BlockSpec((tm, w), lambda i: (i, 0))
    vec = pl.BlockSpec((1, D), lambda i: (0, 0))
    return pl.pallas_call(
        body, name="outproj_loss", grid=(S // tm,),
        in_specs=[row(D), row(E), row(E), pl.BlockSpec((2 * E, D), lambda i: (0, 0)), vec, row(D)],
        out_specs=(row(D), row(D), vec, vec),
        out_shape=(jax.ShapeDtypeStruct((S, D), F32), jax.ShapeDtypeStruct((S, D), BF16),
                   jax.ShapeDtypeStruct((1, D), F32), jax.ShapeDtypeStruct((1, D), F32)),
        compiler_params=_params(("arbitrary",)),
    )(x, y_att, y_conv, w_out, gf, target)


def _split3(v):
    hi = v.astype(BF16)
    r1 = v - hi.astype(F32)
    mid = r1.astype(BF16)
    lo = (r1 - mid.astype(F32)).astype(BF16)
    return hi, mid, lo


def _dy_att(dxb, w_out, gates, o, tm=512):
    S, D = dxb.shape
    E = ATT_W

    def body(dx_ref, w_ref, a_ref, o_ref, do_ref, da_ref, dl_ref, dxr_ref):
        dxr = _perm_rows(dx_ref[...], False)
        dxr_ref[...] = dxr
        dya = _nt(dxr, w_ref[...])
        a = a_ref[...]
        ov = o_ref[...]
        sl, dsl = _silu_and_grad(a)
        d_o = dya * sl
        do_ref[...] = d_o
        da_ref[...] = (dya * ov * dsl).astype(BF16)
        ci = lax.broadcasted_iota(jnp.int32, (E, LANES), 0) // HEAD_DIM
        hi = lax.broadcasted_iota(jnp.int32, (E, LANES), 1)
        sel = jnp.where(ci == hi, 1.0, 0.0).astype(BF16)
        acc = jnp.zeros((tm, LANES), F32)
        for part in _split3(d_o * ov):
            acc = acc + jnp.dot(part, sel, preferred_element_type=F32)
        dl_ref[...] = acc

    row = lambda w: pl.BlockSpec((tm, w), lambda i: (i, 0))
    return pl.pallas_call(
        body, name="dy_att", grid=(S // tm,),
        in_specs=[row(D), pl.BlockSpec((E, D), lambda i: (0, 0)), row(E), row(E)],
        out_specs=(row(E), row(E), row(LANES), row(D)),
        out_shape=(jax.ShapeDtypeStruct((S, E), F32), jax.ShapeDtypeStruct((S, E), BF16),
                   jax.ShapeDtypeStruct((S, LANES), F32), jax.ShapeDtypeStruct((S, D), BF16)),
        compiler_params=_params(("parallel",)),
    )(dxb, w_out, gates, o)


def _dy_conv(dxb, w_out, gates, conv_out, ln_g, ln_b, tm=512):
    S, D = dxb.shape
    C = conv_out.shape[1]

    def body(dx_ref, w_ref, gate_ref, cv_ref, g_ref, beta_ref, dgate_ref, dconv_ref, gg_ref, gb_ref, gcb_ref):
        @pl.when(pl.program_id(0) == 0)
        def _():
            gg_ref[...] = jnp.zeros_like(gg_ref)
            gb_ref[...] = jnp.zeros_like(gb_ref)
            gcb_ref[...] = jnp.zeros_like(gcb_ref)

        dyc = _nt(dx_ref[...], w_ref[...])
        cv = cv_ref[...]
        mu = jnp.mean(cv, axis=-1, keepdims=True)
        xc = cv - mu
        rstd = lax.rsqrt(jnp.mean(xc * xc, axis=-1, keepdims=True) + LN_EPS)
        nrm = xc * rstd
        gv = g_ref[...]
        ln = nrm * gv + beta_ref[...]
        u, du = _silu_and_grad(ln)
        gt = gate_ref[...]
        g2, dg2 = _silu_and_grad(gt)
        dgate_ref[...] = (dyc * u * dg2).astype(BF16)
        d_ln = dyc * g2 * du
        gb_ref[...] += jnp.sum(d_ln, axis=0, keepdims=True)
        gg_ref[...] += jnp.sum(d_ln * nrm, axis=0, keepdims=True)
        dn = d_ln * gv
        d_conv = rstd * (dn - jnp.mean(dn, axis=-1, keepdims=True)
                         - nrm * jnp.mean(dn * nrm, axis=-1, keepdims=True))
        dconv_ref[...] = d_conv
        gcb_ref[...] += jnp.sum(d_conv, axis=0, keepdims=True)

    row = lambda w: pl.BlockSpec((tm, w), lambda i: (i, 0))
    vec = pl.BlockSpec((1, C), lambda i: (0, 0))
    return pl.pallas_call(
        body, name="dy_conv", grid=(S // tm,),
        in_specs=[row(D), pl.BlockSpec((C, D), lambda i: (1, 0)),
                  pl.BlockSpec((tm, C), lambda i: (i, 2)), row(C), vec, vec],
        out_specs=(row(C), row(C), vec, vec, vec),
        out_shape=(jax.ShapeDtypeStruct((S, C), BF16), jax.ShapeDtypeStruct((S, C), F32),
                   jax.ShapeDtypeStruct((1, C), F32), jax.ShapeDtypeStruct((1, C), F32),
                   jax.ShapeDtypeStruct((1, C), F32)),
        compiler_params=_params(("arbitrary",)),
    )(dxb, w_out, gates, conv_out, ln_g, ln_b)


def _conv_bwd(d_conv, gates, d_c_gate, conv_w, hosted=None, tt=256):
    S, C = d_conv.shape
    hb = tt // CONV_HALO
    nt = S // tt
    hn = hosted.n if hosted is not None else 0

    def body(*refs):
        dc_ref, dnext_ref, val_ref, glu_ref, dg_ref, w_ref = refs[:6]
        h_ins = refs[6:6 + hn]
        out_ref, gw_ref = refs[6 + hn:8 + hn]
        h_outs = refs[8 + hn:8 + 2 * hn]
        hbuf, dbuf, dhbuf, dph, wb = refs[8 + 2 * hn:13 + 2 * hn]
        h_sems = refs[13 + 2 * hn:]
        i = pl.program_id(0)

        @pl.when(i == 0)
        def _():
            gw_ref[...] = jnp.zeros_like(gw_ref)
            _broadcast_taps(w_ref, wb)
            if hosted is not None:
                hosted.start(h_ins, h_outs, h_sems)

        val = val_ref[...]
        sg = _sigmoid(glu_ref[...])
        hbuf[...] = val * sg
        dbuf[0:tt, :] = dc_ref[...]
        dbuf[tt:, :] = jnp.where(i < nt - 1, dnext_ref[...], 0.0)
        _shifted_copies(dbuf, dph)
        for cb in range(C // LANES):
            cols = slice(cb * LANES, (cb + 1) * LANES)
            gacc = [jnp.zeros((8, LANES), F32) for _ in range(CONV_K)]
            group = 2
            for rc0 in range(0, tt // 8, group):
                hcur = [hbuf[(rc0 + r) * 8:(rc0 + r + 1) * 8, cols] for r in range(group)]
                accs = [jnp.zeros((8, LANES), F32) for _ in range(group)]
                for j in range(CONV_K):
                    wj = wb[j, :, cols]
                    for r in range(group):
                        dwin = _window(dbuf, dph, (rc0 + r) * 8 + (CONV_K - 1) - j, cols)
                        accs[r] = accs[r] + dwin * wj
                        gacc[j] = gacc[j] + dwin * hcur[r]
                for r in range(group):
                    dhbuf[(rc0 + r) * 8:(rc0 + r + 1) * 8, cols] = accs[r]
            for j in range(CONV_K):
                gw_ref[j:j + 1, cols] += jnp.sum(gacc[j], axis=0, keepdims=True)
        d_h = dhbuf[...]
        out_ref[:, 0:C] = (d_h * sg).astype(BF16)
        out_ref[:, C:2 * C] = (d_h * val * sg * (1.0 - sg)).astype(BF16)
        out_ref[:, 2 * C:3 * C] = dg_ref[...]

        if hosted is not None:
            @pl.when(i == nt - 1)
            def _():
                hosted.finish(h_ins, h_outs, h_sems)

    tile = lambda col: pl.BlockSpec((tt, C), lambda i: (i, col))
    in_specs = [tile(0),
                pl.BlockSpec((CONV_HALO, C), lambda i: (jnp.minimum((i + 1) * hb, S // CONV_HALO - 1), 0)),
                tile(0), tile(1), tile(0),
                pl.BlockSpec((CONV_HALO, C), lambda i: (0, 0))]
    args = [d_conv, d_conv, gates, gates, d_c_gate, conv_w]
    out_specs = [pl.BlockSpec((tt, 3 * C), lambda i: (i, 0)), pl.BlockSpec((CONV_HALO, C), lambda i: (0, 0))]
    out_shape = [jax.ShapeDtypeStruct((S, 3 * C), BF16), jax.ShapeDtypeStruct((CONV_HALO, C), F32)]
    scratch = [pltpu.VMEM((tt, C), F32), pltpu.VMEM((tt + CONV_HALO, C), F32), pltpu.VMEM((tt, C), F32),
               pltpu.VMEM((7, tt + CONV_HALO - 8, C), F32), pltpu.VMEM((CONV_K, 8, C), F32)]
    if hosted is not None:
        in_specs += [ANY_SPEC] * hn
        args += hosted.arrays
        out_specs += [ANY_SPEC] * hn
        out_shape += hosted.out_shapes()
        scratch += hosted.sem_shapes()
    res = pl.pallas_call(
        body, name="conv_bwd", grid=(nt,),
        in_specs=in_specs, out_specs=tuple(out_specs), out_shape=tuple(out_shape), scratch_shapes=scratch,
        compiler_params=_params(("arbitrary",)),
    )(*args)
    return res[0], res[1], list(res[2:])


def _attn_bwd(q, kv, d_o, lse, delta, dil, prev, final, name, hosted=None):
    S = q.shape[0]
    rows = _Rows(dil, S)
    nb = rows.nb
    steps = dil * nb
    out_dt = BF16 if final else F32
    have_prev = prev is not None
    hn = hosted.n if hosted is not None else 0

    def body(*refs):
        refs = list(refs)
        q_ref, do_ref, lse_ref, dl_ref, kvc_ref, kvp_ref = refs[:6]
        del refs[:6]
        if have_prev:
            pdq_ref, pdkv_ref = refs[:2]
            del refs[:2]
        h_ins = refs[:hn]
        dq_ref, dkv_ref = refs[hn:hn + 2]
        h_outs = refs[hn + 2:2 * hn + 2]
        carry, tbl = refs[2 * hn + 2:2 * hn + 4]
        h_sems = refs[2 * hn + 4:]
        t = pl.program_id(0)
        n = t % nb

        @pl.when(t == 0)
        def _():
            if hosted is not None:
                hosted.start(h_ins, h_outs, h_sems)
            _fill_bias_table(tbl, rows, keys_first=True)
            carry[...] = jnp.zeros_like(carry)

        @pl.when(t < steps)
        def _():
            kv2 = jnp.concatenate([_ld(kvp_ref), _ld(kvc_ref)], axis=0)
            lse_t, dl_t = _ld(lse_ref).T, _ld(dl_ref).T
            lo_mask = lax.broadcasted_iota(jnp.int32, (2 * BLK, LANES), 1) < HEAD_DIM
            halves = [jnp.zeros((2 * BLK, LANES), F32) for _ in range(4)]
            for hk in range(N_KV_HEADS):
                k_lo, k_hi, v_lo, v_hi = _head_operands(kv2, hk, lo_mask)
                cols = [slice(b * LANES, (b + 1) * LANES) for b in (2 * hk, 2 * hk + 1)]
                q2 = jnp.concatenate([_ld(q_ref, cols[0]), _ld(q_ref, cols[1])], axis=0).astype(BF16)
                do2 = jnp.concatenate([_ld(do_ref, cols[0]), _ld(do_ref, cols[1])], axis=0).astype(BF16)
                dq2 = jnp.zeros((2 * BLK, LANES), F32)
                dks, dvs = [], []
                for which, (kk, vv) in enumerate(((k_lo, v_lo), (k_hi, v_hi))):
                    h0, h1 = 4 * hk + which, 4 * hk + 2 + which
                    s = _nt(kk, q2) + _bias2(tbl, n, h0, h1, axis=1)
                    lse2 = jnp.concatenate([lse_t[h0:h0 + 1, :], lse_t[h1:h1 + 1, :]], axis=1)
                    dl2 = jnp.concatenate([dl_t[h0:h0 + 1, :], dl_t[h1:h1 + 1, :]], axis=1)
                    p = jnp.exp(s - lse2)
                    ds = (p * (_nt(vv, do2) - dl2)).astype(BF16)
                    dq2 = dq2 + _tn(ds, kk)
                    dks.append(jnp.dot(ds, q2, preferred_element_type=F32))
                    dvs.append(jnp.dot(p.astype(BF16), do2, preferred_element_type=F32))
                dk_sum = jnp.where(lo_mask, dks[0], dks[1])
                dv_sum = jnp.where(lo_mask, dvs[0], dvs[1])
                for jp in range(2):
                    dq_blk = dq2[jp * BLK:(jp + 1) * BLK]
                    if have_prev:
                        dq_blk = dq_blk + _ld(pdq_ref, cols[jp])
                    if final:
                        dq_blk = dq_blk * (HEAD_DIM ** -0.5)
                    _st(dq_ref, dq_blk.astype(out_dt), cols[jp])
                half, pos = hk // 2, hk % 2
                here = lo_mask if pos == 0 else jnp.logical_not(lo_mask)
                dk_tot = dk_sum + pltpu.roll(dk_sum, HEAD_DIM, axis=1)
                dv_tot = dv_sum + pltpu.roll(dv_sum, HEAD_DIM, axis=1)
                halves[half] = halves[half] + jnp.where(here, dk_tot, 0.0)
                halves[2 + half] = halves[2 + half] + jnp.where(here, dv_tot, 0.0)
            for b in range(4):
                cols = slice(b * LANES, (b + 1) * LANES)
                done = carry[:, cols] + halves[b][0:BLK, :]
                if have_prev:
                    done = done + _ld(pdkv_ref, cols)
                _st(dkv_ref, done.astype(out_dt), cols)
                carry[:, cols] = halves[b][BLK:, :]

        @pl.when(t == steps)
        def _():
            done = carry[...]
            if have_prev:
                done = done + _ld(pdkv_ref)
            _st(dkv_ref, done.astype(out_dt))
            if hosted is not None:
                hosted.finish(h_ins, h_outs, h_sems)

    def spec(width, lag):
        def index(t):
            u = jnp.clip(t - lag, 0, steps - 1)
            return rows.index(u // nb, u % nb)
        return pl.BlockSpec(rows.block + (width,), index)

    def key_prev(t):
        u = jnp.minimum(t, steps - 1)
        return rows.index(u // nb, jnp.maximum(u % nb - 1, 0))

    in_specs = [spec(ATT_W, 0), spec(ATT_W, 0), spec(LANES, 0), spec(LANES, 0), spec(2 * KV_W, 0),
                pl.BlockSpec(rows.block + (2 * KV_W,), key_prev)]
    args = [rows.of(q), rows.of(d_o), rows.of(lse), rows.of(delta), rows.of(kv), rows.of(kv)]
    if have_prev:
        in_specs += [spec(ATT_W, 0), spec(2 * KV_W, 1)]
        args += [rows.of(prev[0]), rows.of(prev[1])]
    out_specs = [spec(ATT_W, 0), spec(2 * KV_W, 1)]
    out_shape = [jax.ShapeDtypeStruct(rows.view + (ATT_W,), out_dt),
                 jax.ShapeDtypeStruct(rows.view + (2 * KV_W,), out_dt)]
    scratch = [pltpu.VMEM((BLK, 2 * KV_W), F32), pltpu.VMEM((2 * N_Q_HEADS, 2 * BLK, BLK), F32)]
    if hosted is not None:
        in_specs += [ANY_SPEC] * hn
        args += hosted.arrays
        out_specs += [ANY_SPEC] * hn
        out_shape += hosted.out_shapes()
        scratch += hosted.sem_shapes()
    res = pl.pallas_call(
        body, name=name, grid=(steps + 1,),
        in_specs=in_specs, out_specs=tuple(out_specs), out_shape=tuple(out_shape), scratch_shapes=scratch,
        compiler_params=_params(("arbitrary",)),
    )(*args)
    return (res[0].reshape(S, ATT_W), res[1].reshape(S, 2 * KV_W)), list(res[2:])


def _dh(segments, w_in, x, dx2, g, hosted=None, tm=1024, tk=512):
    S, D = x.shape
    ns = len(segments)
    counts = [a.shape[1] // tk for a, _ in segments]
    starts = [sum(counts[:s]) for s in range(ns)]
    nk = sum(counts)
    hn = hosted.n if hosted is not None else 0

    def body(*refs):
        seg_refs = refs[:ns]
        w_ref, x_ref, dx2_ref, g_ref = refs[ns:ns + 4]
        h_ins = refs[ns + 4:ns + 4 + hn]
        gx_ref, gng_ref = refs[ns + 4 + hn:ns + 6 + hn]
        h_outs = refs[ns + 6 + hn:ns + 6 + 2 * hn]
        acc = refs[ns + 6 + 2 * hn]
        h_sems = refs[ns + 7 + 2 * hn:]
        k, i = pl.program_id(0), pl.program_id(1)

        @pl.when((i == 0) & (k == 0))
        def _():
            gng_ref[...] = jnp.zeros_like(gng_ref)
            if hosted is not None:
                hosted.start(h_ins, h_outs, h_sems)

        @pl.when(k == 0)
        def _():
            acc[i] = jnp.zeros(acc.shape[1:], F32)

        for s in range(ns):
            @pl.when((k >= starts[s]) & (k < starts[s] + counts[s]))
            def _(s=s):
                t = seg_refs[s][...]
                if segments[s][1]:
                    t = _perm_rows(t, True)
                acc[i] += jnp.dot(t, w_ref[...], preferred_element_type=F32)

        @pl.when(k == nk - 1)
        def _():
            dh = acc[i]
            xf = x_ref[...]
            r = lax.rsqrt(jnp.mean(xf * xf, axis=-1, keepdims=True) + NORM_EPS)
            nrm = xf * r
            gng_ref[...] += jnp.sum(dh * nrm, axis=0, keepdims=True)
            dn = dh * g_ref[...]
            gx_ref[...] = dx2_ref[...] + r * (dn - nrm * jnp.mean(dn * nrm, axis=-1, keepdims=True))

        if hosted is not None:
            @pl.when((i == S // tm - 1) & (k == nk - 1))
            def _():
                hosted.finish(h_ins, h_outs, h_sems)

    ni = S // tm
    row = pl.BlockSpec((tm, D), lambda k, i: (jnp.where(k == nk - 1, i, 0), 0))
    vec = pl.BlockSpec((1, D), lambda k, i: (0, 0))

    def seg_index(s):
        def index(k, i):
            j = k - starts[s]
            return jnp.where(j < 0, 0, jnp.where(j >= counts[s], ni - 1, i)), jnp.clip(j, 0, counts[s] - 1)
        return index

    in_specs = [pl.BlockSpec((tm, tk), seg_index(s)) for s in range(ns)]
    in_specs += [pl.BlockSpec((tk, D), lambda k, i: (k, 0)), row, row, vec]
    args = [a for a, _ in segments] + [w_in, x, dx2, g]
    out_specs = [row, vec]
    out_shape = [jax.ShapeDtypeStruct((S, D), F32), jax.ShapeDtypeStruct((1, D), F32)]
    scratch = [pltpu.VMEM((ni, tm, D), F32)]
    if hosted is not None:
        in_specs += [ANY_SPEC] * hn
        args += hosted.arrays
        out_specs += [ANY_SPEC] * hn
        out_shape += hosted.out_shapes()
        scratch += hosted.sem_shapes()
    res = pl.pallas_call(
        body, name="dh", grid=(nk, S // tm),
        in_specs=in_specs, out_specs=tuple(out_specs), out_shape=tuple(out_shape), scratch_shapes=scratch,
        compiler_params=_params(("arbitrary", "arbitrary"), BIG_VMEM_LIMIT),
    )(*args)
    return res[0], res[1], list(res[2:])


def _tn_matmul(a, bs, name, b_first=False, tm=512):
    M, K = a.shape
    nb = len(bs)
    shapes = [(b.shape[1], K) if b_first else (K, b.shape[1]) for b in bs]

    def body(a_ref, *refs):
        @pl.when(pl.program_id(0) == 0)
        def _():
            for o_ref in refs[nb:]:
                o_ref[...] = jnp.zeros_like(o_ref)

        at = a_ref[...]
        for b_ref, o_ref in zip(refs[:nb], refs[nb:]):
            for c in range(0, b_ref.shape[1], 512):
                if b_first:
                    o_ref[c:c + 512, :] += _tn(b_ref[:, c:c + 512], at)
                else:
                    o_ref[:, c:c + 512] += _tn(at, b_ref[:, c:c + 512])

    return pl.pallas_call(
        body, name=name, grid=(M // tm,),
        in_specs=[pl.BlockSpec((tm, K), lambda m: (m, 0))] + [pl.BlockSpec((tm, b.shape[1]), lambda m: (m, 0))
                                                              for b in bs],
        out_specs=tuple(pl.BlockSpec(s, lambda m: (0, 0)) for s in shapes),
        out_shape=tuple(jax.ShapeDtypeStruct(s, F32) for s in shapes),
        compiler_params=_params(("arbitrary",)),
    )(a, *bs)


def _adamw(parts, w, m, v, name, tr=None, split=None, by_chip=False):
    R, C = w.shape
    tr = R if tr is None else tr
    parts = [parts] if split is None else list(parts)
    npar = len(parts)

    def total(p_ref):
        if by_chip:
            c = lax.axis_index("c")
            g = p_ref[c].astype(F32)
            for chip in range(1, N_DEV // 2):
                g = g + p_ref[2 * chip + c].astype(F32)
            return g
        g = p_ref[0].astype(F32)
        for dev in range(1, N_DEV):
            g = g + p_ref[dev].astype(F32)
        return g

    def body(*refs):
        w_ref, m_ref, v_ref, g_out, d_out, m_out, v_out = refs[npar:]
        if split is None:
            g = total(refs[0])
        else:
            g = jnp.where(_mesh_pos()[3] < split, total(refs[0]), total(refs[1]))
        mn = ADAM_B1 * m_ref[...] + (1.0 - ADAM_B1) * g
        vn = ADAM_B2 * v_ref[...] + (1.0 - ADAM_B2) * (g * g)
        m_hat = mn / (1.0 - ADAM_B1 ** ADAM_STEP)
        v_hat = vn / (1.0 - ADAM_B2 ** ADAM_STEP)
        g_out[...] = g
        d_out[...] = -ADAM_LR * (m_hat / (jnp.sqrt(v_hat) + ADAM_EPS) + ADAM_WD * w_ref[...])
        m_out[...] = mn
        v_out[...] = vn

    blk = pl.BlockSpec((tr, C), lambda i: (i, 0))
    shp = jax.ShapeDtypeStruct((R, C), F32)
    return pl.pallas_call(
        body, name=name, grid=(R // tr,),
        in_specs=[pl.BlockSpec((N_DEV, tr, C), lambda i: (0, i, 0))] * npar + [blk, blk, blk],
        out_specs=(blk, blk, blk, blk), out_shape=(shp, shp, shp, shp),
        compiler_params=_params(("parallel",)),
    )(*parts, w, m, v)


def _local_step(x, target, norm_g, w_in, conv_w, conv_b, ln_g, ln_b, w_out, gf, exchanges=None, first_weights=None,
                late_weights=None):
    ex_out, ex_att, ex_conv = exchanges if exchanges is not None else (None, None, None)
    h_rm, h, *first = _norm_rows(x, norm_g, first_weights[0] if first_weights is not None else None)
    if first_weights is not None:
        w_in = first_weights[1](first)
    conv_cols = w_in.shape[0] - 2 * ATT_W - 2 * KV_W
    q, kv, a_gate, gates, *gathered = _inproj(
        h_rm, h, w_in,
        [(ATT_W, HEAD_DIM ** -0.5, True, F32), (2 * KV_W, 1.0, True, F32), (ATT_W, 1.0, True, F32),
         (conv_cols, 1.0, False, F32)], late_weights[0] if late_weights is not None else None)
    if late_weights is not None:
        conv_w, w_out = late_weights[1](gathered)

    merged = None
    for idx, (_, dil) in enumerate(reversed(PATTERNS)):
        merged = _attn_fwd(q, kv, dil, "attn_fwd_d%d" % dil, merged,
                           a_gate if idx == len(PATTERNS) - 1 else None)
    o, lse, y_att = merged
    conv_out, y_conv = _conv_fwd(gates, conv_w, conv_b, ln_g, ln_b)
    dx2, dxb, loss_cols, g_gf = _outproj_loss(x, y_att, y_conv, w_out, gf, target)

    d_o, d_a_gate, delta, dxb_rm = _dy_att(dxb, w_out, a_gate, o)
    g_w_out = jnp.concatenate([_tn_matmul(y_att, [dxb_rm], "gw_out_att")[0],
                               _tn_matmul(y_conv, [dxb], "gw_out_conv")[0]], axis=0)
    acc, out_parts = None, []
    for idx, (_, dil) in enumerate(reversed(PATTERNS)):
        hosted = ex_out(g_w_out) if (idx == 0 and ex_out is not None) else None
        acc, outs = _attn_bwd(q, kv, d_o, lse, delta, dil, acc, idx == len(PATTERNS) - 1, "attn_bwd_d%d" % dil,
                              hosted)
        out_parts += outs
    dq, dkv = acc
    g_q, g_kv, g_a = _tn_matmul(h_rm, [dq, dkv, d_a_gate], "gw_in_att", b_first=True)

    d_c_gate, d_conv, g_ln_g, g_ln_b, g_conv_b = _dy_conv(dxb, w_out, gates, conv_out, ln_g, ln_b)
    dgates, g_conv_w, att_parts = _conv_bwd(d_conv, gates, d_c_gate, conv_w,
                                            ex_att(g_q, g_kv, g_a) if ex_att is not None else None)
    g_c, = _tn_matmul(h, [dgates], "gw_in_conv", b_first=True)
    grad_x, g_norm_g, conv_parts = _dh(
        [(dq, True), (dkv, True), (d_a_gate, True), (dgates, False)], w_in, x, dx2, norm_g,
        ex_conv(g_a, g_c, g_conv_w) if ex_conv is not None else None)
    small = (g_norm_g, g_conv_b, g_ln_g, g_ln_b, g_gf, loss_cols)
    return grad_x, (g_q, g_kv, g_a, g_c), g_w_out, g_conv_w, small, (out_parts, att_parts, conv_parts)


def kernel(x, norm_g, w_in, conv_w, conv_b, conv_ln_g, conv_ln_b, w_out, final_norm_g, loss_target, m_norm_g, m_w_in, m_conv_w, m_conv_b, m_conv_ln_g, m_conv_ln_b, m_w_out, m_final_norm_g, v_norm_g, v_w_in, v_conv_w, v_conv_b, v_conv_ln_g, v_conv_ln_b, v_w_out, v_final_norm_g):
    S, D = x.shape[1], x.shape[2]
    win_sh, wout_sh, cw_sh = w_in[0].T, w_out[0], conv_w[0]
    cols_sh, rows_sh, ch_sh = win_sh.shape[0], wout_sh.shape[0], cw_sh.shape[1]

    def first_weights(gathered):
        return gathered[0].reshape(N_DEV * cols_sh, D)

    def late_weights(gathered):
        wout_all, cw_all = gathered
        conv_w_full = cw_all.transpose(1, 0, 2).reshape(CONV_K, N_DEV * ch_sh)
        return jnp.pad(conv_w_full, ((0, CONV_HALO - CONV_K), (0, 0))), wout_all.reshape(N_DEV * rows_sh, D)

    gf = final_norm_g.reshape(1, D)

    first = -(-(ATT_W + 2 * KV_W) // cols_sh)
    a_off = first * cols_sh - (ATT_W + 2 * KV_W)
    assert 0 <= a_off <= ATT_W

    def pieces(parts, n):
        return jnp.concatenate([p.astype(BF16) for p in parts], axis=0).reshape(n, cols_sh, D)

    def ex_out(g_w_out):
        return _Exchange([g_w_out.reshape(N_DEV, rows_sh, D).astype(BF16)], [(0, N_DEV)])

    same_core = (2, 4, 6)

    def ex_att(g_q, g_kv, g_a):
        mine = _chip_sum(pieces([g_q, g_kv, g_a[:a_off]], first), 0, "rs_att")
        return _Exchange([mine], [(0, first)], [same_core])

    def ex_conv(g_a, g_c, g_conv_w):
        mine = _chip_sum(pieces([g_a[a_off:], g_c], N_DEV - first), first, "rs_conv")
        return _Exchange(
            [mine, g_conv_w[:CONV_K].reshape(CONV_K, N_DEV, ch_sh).transpose(1, 0, 2)],
            [(first, N_DEV), (0, N_DEV)], [same_core, None])

    grad_x, _, _, _, small, parts = _local_step(
        x[0], loss_target[0], norm_g, None, None, conv_b, conv_ln_g, conv_ln_b, None, gf,
        (ex_out, ex_att, ex_conv), (_Gather([win_sh.astype(BF16)]), first_weights),
        (_Gather([wout_sh.astype(BF16), cw_sh]), late_weights))
    (wout_parts,), (win_parts_lo,), (win_parts_hi, cw_parts) = parts

    small_pack = jnp.concatenate(list(small) + [jnp.zeros((2, D), F32)], axis=0)
    small_parts, = _exchange([small_pack], [None], "gather_small")

    upd_win = _adamw((win_parts_lo, win_parts_hi), win_sh, m_w_in[0].T, v_w_in[0].T, "adamw_w_in",
                     tr=cols_sh // 2, split=first, by_chip=True)
    upd_wout = _adamw(wout_parts, wout_sh, m_w_out[0], v_w_out[0], "adamw_w_out", tr=128)
    upd_cw = _adamw(cw_parts, cw_sh, m_conv_w[0], v_conv_w[0], "adamw_conv_w")
    zeros3 = jnp.zeros((3, D), F32)
    stack = lambda a, b, c, d_, e: jnp.concatenate([a, b, c, d_, e.reshape(1, D), zeros3], axis=0)
    upd_small = _adamw(
        small_parts,
        stack(norm_g, conv_b, conv_ln_g, conv_ln_b, final_norm_g),
        stack(m_norm_g, m_conv_b, m_conv_ln_g, m_conv_ln_b, m_final_norm_g),
        stack(v_norm_g, v_conv_b, v_conv_ln_g, v_conv_ln_b, v_final_norm_g) + jnp.concatenate(
            [jnp.zeros((5, D), F32), jnp.ones((3, D), F32)], axis=0),
        "adamw_small")

    loss = 0.5 / D * jnp.sum(upd_small[0][5])

    def outputs(kind):
        sm = upd_small[kind]
        return [sm[0:1], upd_win[kind].T[None], upd_cw[kind][None], sm[1:2], sm[2:3], sm[3:4],
                upd_wout[kind][None], sm[4]]

    return (loss, grad_x[None], *outputs(0), *outputs(1), *outputs(2), *outputs(3))
```

```python
import jax
import jax.numpy as jnp
from jax import lax
from jax.experimental import pallas as pl
from jax.experimental.pallas import tpu as pltpu

F32 = jnp.float32
BF16 = jnp.bfloat16

HEAD_DIM = 64
N_KV_HEADS = 4
N_Q_HEADS = 16
ATT_W = 1024
KV_W = 256
CONV_K = 31
CONV_HALO = 32
PATTERNS = ((128, 1), (512, 4), (2048, 16))
BLK = 128
LANES = 128
NORM_EPS = 1e-6
LN_EPS = 1e-5
NEG = -1e30
N_DEV = 8
ADAM_LR, ADAM_B1, ADAM_B2, ADAM_EPS, ADAM_WD, ADAM_STEP = 0.001, 0.9, 0.999, 1e-08, 0.01, 10
VMEM_LIMIT = 48 * 1024 * 1024
BIG_VMEM_LIMIT = 58 * 1024 * 1024
SLOPES = tuple(2.0 ** (-8.0 * (h + 1) / N_Q_HEADS) for h in range(N_Q_HEADS))
MESH = pl.DeviceIdType.MESH


def _params(sem, vmem_limit=VMEM_LIMIT):
    return pltpu.CompilerParams(dimension_semantics=sem, vmem_limit_bytes=vmem_limit)


def _sigmoid(v):
    return 1.0 / (1.0 + jnp.exp(-v))


def _silu_and_grad(v):
    s = _sigmoid(v)
    return v * s, s * (1.0 + v * (1.0 - s))


ANY_SPEC = pl.BlockSpec(memory_space=pl.ANY)


def _mesh_pos():
    x, y, c = lax.axis_index("x"), lax.axis_index("y"), lax.axis_index("c")
    return x, y, c, 4 * x + 2 * y + c


def _flipped(k, x, y, c):
    px = 1 - x if k & 4 else x
    py = 1 - y if k & 2 else y
    pc = 1 - c if k & 1 else c
    return (px, py, pc), 4 * px + 2 * py + pc


class _Exchange:
    def __init__(self, arrays, dests, flips=None):
        self.arrays, self.dests, self.n = list(arrays), list(dests), len(arrays)
        self.flips = [tuple(range(1, N_DEV)) if f is None else tuple(f)
                      for f in (flips if flips is not None else [None] * self.n)]

    def out_shapes(self):
        return [jax.ShapeDtypeStruct((N_DEV,) + a.shape[-2:], a.dtype) for a in self.arrays]

    def sem_shapes(self):
        return [pltpu.SemaphoreType.DMA((self.n, N_DEV - 1)), pltpu.SemaphoreType.DMA((self.n, N_DEV - 1)),
                pltpu.SemaphoreType.DMA((self.n,))]

    def _when(self, a, dev, fn):
        if self.dests[a] is None:
            fn()
        else:
            lo, hi = self.dests[a]
            pl.when((dev >= lo) & (dev < hi))(fn)

    def _mine(self, ins, a, dev):
        return ins[a] if self.dests[a] is None else ins[a].at[dev - self.dests[a][0]]

    def _copy(self, ins, outs, sems, a, k, src_dev, slot, target):
        return pltpu.make_async_remote_copy(
            src_ref=self._mine(ins, a, src_dev), dst_ref=outs[a].at[slot],
            send_sem=sems[0].at[a, k - 1], recv_sem=sems[1].at[a, k - 1],
            device_id=target, device_id_type=MESH)

    def start(self, ins, outs, sems):
        x, y, c, me = _mesh_pos()
        for a in range(self.n):
            self._when(a, me, lambda a=a: pltpu.make_async_copy(
                self._mine(ins, a, me), outs[a].at[me], sems[2].at[a]).start())
            for k in self.flips[a]:
                target, peer = _flipped(k, x, y, c)
                self._when(a, peer, lambda a=a, k=k, target=target, peer=peer: self._copy(
                    ins, outs, sems, a, k, peer, me, target).start())

    def finish(self, ins, outs, sems):
        x, y, c, me = _mesh_pos()
        lo0 = [0 if d is None else d[0] for d in self.dests]
        for a in range(self.n):
            for k in self.flips[a]:
                target, peer = _flipped(k, x, y, c)
                self._when(a, me, lambda a=a, k=k, peer=peer: self._copy(
                    ins, outs, sems, a, k, lo0[a], peer, (x, y, c)).wait_recv())
            for k in self.flips[a]:
                target, peer = _flipped(k, x, y, c)
                self._when(a, peer, lambda a=a, k=k, target=target, peer=peer: self._copy(
                    ins, outs, sems, a, k, peer, me, target).wait_send())
            self._when(a, me, lambda a=a: pltpu.make_async_copy(
                self._mine(ins, a, me), outs[a].at[me], sems[2].at[a]).wait())


def _exchange(ex, name):
    na = ex.n

    def body(*refs):
        ins, outs, sems = refs[:na], refs[na:2 * na], refs[2 * na:]
        ex.start(ins, outs, sems)
        ex.finish(ins, outs, sems)

    return pl.pallas_call(
        body, name=name, out_shape=tuple(ex.out_shapes()),
        in_specs=[ANY_SPEC] * na, out_specs=tuple([ANY_SPEC] * na), scratch_shapes=ex.sem_shapes(),
    )(*ex.arrays)


def _chip_sum(pieces, lo, name):
    n, R, C = pieces.shape

    def swap(p_ref, t_ref, send_sems, recv_sems):
        x, y, c, me = _mesh_pos()
        for i in range(n):
            mine = (lo + i) % 2
            cp = pltpu.make_async_remote_copy(
                src_ref=p_ref.at[i], dst_ref=t_ref.at[i], send_sem=send_sems.at[i], recv_sem=recv_sems.at[i],
                device_id=(x, y, 1 - c), device_id_type=MESH)
            pl.when(c != mine)(cp.start)
        for i in range(n):
            mine = (lo + i) % 2
            cp = pltpu.make_async_remote_copy(
                src_ref=p_ref.at[i], dst_ref=t_ref.at[i], send_sem=send_sems.at[i], recv_sem=recv_sems.at[i],
                device_id=(x, y, 1 - c), device_id_type=MESH)
            pl.when(c == mine)(cp.wait_recv)
            pl.when(c != mine)(cp.wait_send)

    other = pl.pallas_call(
        swap, name=name + "_swap", out_shape=jax.ShapeDtypeStruct(pieces.shape, pieces.dtype),
        in_specs=[ANY_SPEC], out_specs=ANY_SPEC,
        scratch_shapes=[pltpu.SemaphoreType.DMA((n,)), pltpu.SemaphoreType.DMA((n,))],
    )(pieces)

    def add(p_ref, t_ref, o_ref):
        o_ref[...] = (p_ref[...].astype(F32) + t_ref[...].astype(F32)).astype(o_ref.dtype)

    tr = R // 2
    blk = pl.BlockSpec((None, tr, C), lambda i, r: (i, r, 0))
    return pl.pallas_call(
        add, name=name + "_add", grid=(n, R // tr), in_specs=[blk, blk], out_specs=blk,
        out_shape=jax.ShapeDtypeStruct(pieces.shape, pieces.dtype),
        compiler_params=_params(("parallel", "parallel")),
    )(pieces, other)


class _Gather:
    def __init__(self, arrays):
        self.arrays, self.n = list(arrays), len(arrays)

    def out_shapes(self):
        return [jax.ShapeDtypeStruct((N_DEV,) + a.shape, a.dtype) for a in self.arrays]

    def sem_shapes(self):
        return [pltpu.SemaphoreType.DMA((self.n, N_DEV - 1)), pltpu.SemaphoreType.DMA((self.n, N_DEV - 1)),
                pltpu.SemaphoreType.DMA((self.n,))]

    def _plan(self, ins, outs, sems):
        x, y, c, me = _mesh_pos()
        chips = [(1 - x, y), (x, 1 - y), (1 - x, 1 - y)]

        def copy(a, k, src, block, to):
            px, py, pc = block
            return pltpu.make_async_remote_copy(
                src_ref=src, dst_ref=outs[a].at[4 * px + 2 * py + pc], send_sem=sems[0].at[a, k],
                recv_sem=sems[1].at[a, k], device_id=to, device_id_type=MESH)

        local = [pltpu.make_async_copy(ins[a], outs[a].at[me], sems[2].at[a]) for a in range(self.n)]
        first = []
        for a in range(self.n):
            first.append(copy(a, 0, ins[a], (x, y, c), (x, y, 1 - c)))
            first += [copy(a, 1 + j, ins[a], (x, y, c), (*chip, c)) for j, chip in enumerate(chips)]
        return (x, y, c), chips, copy, local, first

    def start(self, ins, outs, sems):
        _, _, _, local, first = self._plan(ins, outs, sems)
        for cp in local + first:
            cp.start()

    def finish(self, ins, outs, sems):
        (x, y, c), chips, copy, local, first = self._plan(ins, outs, sems)
        passed = []
        for j, chip in enumerate(chips):
            for a in range(self.n):
                copy(a, 1 + j, ins[a], (*chip, c), (x, y, c)).wait_recv()
                px, py = chip
                fwd = copy(a, 4 + j, outs[a].at[4 * px + 2 * py + c], (*chip, c), (x, y, 1 - c))
                fwd.start()
                passed.append(fwd)
        for a in range(self.n):
            copy(a, 0, ins[a], (x, y, 1 - c), (x, y, c)).wait_recv()
            for j, chip in enumerate(chips):
                copy(a, 4 + j, ins[a], (*chip, 1 - c), (x, y, c)).wait_recv()
        for cp in first + passed:
            cp.wait_send()
        for cp in local:
            cp.wait()


CHUNK = 128
RESIDUES = 16
PER_RES = CHUNK // RESIDUES


def _perm_rows(tile, inverse):
    a = lax.broadcasted_iota(jnp.int32, (CHUNK, CHUNK), 0)
    b = lax.broadcasted_iota(jnp.int32, (CHUNK, CHUNK), 1)
    if inverse:
        a, b = b, a
    p = jnp.where(a == PER_RES * (b % RESIDUES) + b // RESIDUES, 1.0, 0.0).astype(BF16)
    parts = [jnp.dot(p, tile[c * CHUNK:(c + 1) * CHUNK], preferred_element_type=F32)
             for c in range(tile.shape[0] // CHUNK)]
    return jnp.concatenate(parts, axis=0).astype(BF16)


class _Rows:
    def __init__(self, dil, S):
        nc = S // CHUNK
        self.dil = dil
        if dil == 1:
            self.view, self.block, self.nb = (nc, CHUNK), (None, CHUNK), nc
            self.index = lambda r, b: (b, 0, 0)
        elif dil == 4:
            self.view, self.block, self.nb = (nc, 4, 4, PER_RES), (4, 4, None, PER_RES), nc // 4
            self.index = lambda r, b: (b, 0, r, 0, 0)
        elif dil == RESIDUES:
            self.view, self.block, self.nb = (nc, RESIDUES, PER_RES), (RESIDUES, None, PER_RES), nc // RESIDUES
            self.index = lambda r, b: (b, r, 0, 0)
        else:
            raise NotImplementedError(dil)

    def of(self, a):
        return a.reshape(self.view + (a.shape[-1],))

    def spec(self, width, which_block):
        return pl.BlockSpec(self.block + (width,), lambda r, n: self.index(r, which_block(n)))

    def pos(self, row):
        if self.dil == 1:
            return (row % PER_RES) * RESIDUES + row // PER_RES
        if self.dil == 4:
            return (row // 32) * 32 + (row % PER_RES) * 4 + (row % 32) // PER_RES
        return row


def _ld(ref, cols=slice(None)):
    v = ref[(slice(None),) * (len(ref.shape) - 1) + (cols,)]
    return v.reshape(BLK, v.shape[-1])


def _st(ref, val, cols=slice(None)):
    ref[(slice(None),) * (len(ref.shape) - 1) + (cols,)] = val.reshape(ref.shape[:-1] + (val.shape[-1],))


def _norm_rows(x, g, hosted=None, tm=512):
    S, D = x.shape
    hn = hosted.n if hosted is not None else 0

    def body(x_ref, g_ref, *rest):
        h_ins = rest[:hn]
        hrm_out, h_out = rest[hn:hn + 2]
        h_outs = rest[hn + 2:2 * hn + 2]
        h_sems = rest[2 * hn + 2:]
        i = pl.program_id(0)
        if hosted is not None:
            pl.when(i == 0)(lambda: hosted.start(h_ins, h_outs, h_sems))
        xf = x_ref[...]
        r = lax.rsqrt(jnp.mean(xf * xf, axis=-1, keepdims=True) + NORM_EPS)
        h = (xf * r * g_ref[...]).astype(BF16)
        h_out[...] = h
        hrm_out[...] = _perm_rows(h, False)
        if hosted is not None:
            pl.when(i == S // tm - 1)(lambda: hosted.finish(h_ins, h_outs, h_sems))

    row = pl.BlockSpec((tm, D), lambda i: (i, 0))
    in_specs, args = [row, pl.BlockSpec((1, D), lambda i: (0, 0))], [x, g]
    out_specs, out_shape, scratch = [row, row], [jax.ShapeDtypeStruct((S, D), BF16)] * 2, []
    if hosted is not None:
        in_specs += [ANY_SPEC] * hn
        args += hosted.arrays
        out_specs += [ANY_SPEC] * hn
        out_shape += hosted.out_shapes()
        scratch += hosted.sem_shapes()
    return pl.pallas_call(
        body, name="norm_rows", grid=(S // tm,),
        in_specs=in_specs, out_specs=tuple(out_specs), out_shape=tuple(out_shape), scratch_shapes=scratch,
        compiler_params=_params(("arbitrary",)),
    )(*args)


def _inproj(h_rm, h, w_t, segments, hosted=None, tm=1024, tn=512):
    S, D = h.shape
    ns = len(segments)
    ni = S // tm
    counts = [seg[0] // tn for seg in segments]
    starts = [sum(counts[:s]) for s in range(ns)]

    hn = hosted.n if hosted is not None else 0
    last_p = sum(counts)

    def body(hrm_ref, h_ref, w_ref, *rest):
        h_ins, rest = rest[:hn], rest[hn:]
        outs = rest[:ns]
        h_outs = rest[ns:ns + hn]
        hrm_scr, h_scr = rest[ns + hn:ns + 2 + hn]
        h_sems = rest[ns + 2 + hn:]
        p, i = pl.program_id(0), pl.program_id(1)

        if hosted is not None:
            @pl.when((p == 0) & (i == 0))
            def _():
                hosted.start(h_ins, h_outs, h_sems)

            @pl.when((p == last_p) & (i == ni - 1))
            def _():
                hosted.finish(h_ins, h_outs, h_sems)

        @pl.when(p == 0)
        def _():
            h_scr[i] = h_ref[...]
            hrm_scr[i] = hrm_ref[...]

        for s, (_, scale, rm) in enumerate(segments):
            @pl.when((p > starts[s]) & (p <= starts[s] + counts[s]))
            def _(s=s, scale=scale, rm=rm):
                acc = _nt((hrm_scr if rm else h_scr)[i], w_ref[...])
                outs[s][...] = acc * scale if scale != 1.0 else acc

    def out_index(s):
        def index(p, i):
            j = p - 1 - starts[s]
            row = jnp.where(j < 0, 0, jnp.where(j >= counts[s], ni - 1, i))
            return row, jnp.clip(j, 0, counts[s] - 1)
        return index

    first_pass = pl.BlockSpec((tm, D), lambda p, i: (jnp.where(p == 0, i, ni - 1), 0))
    out_specs = [pl.BlockSpec((tm, tn), out_index(s)) for s in range(ns)]
    out_shape = [jax.ShapeDtypeStruct((S, seg[0]), F32) for seg in segments]
    in_specs = [first_pass, first_pass, pl.BlockSpec((tn, D), lambda p, i: (jnp.maximum(p - 1, 0), 0))]
    args = [h_rm, h, w_t]
    scratch = [pltpu.VMEM((ni, tm, D), BF16), pltpu.VMEM((ni, tm, D), BF16)]
    if hosted is not None:
        in_specs += [ANY_SPEC] * hn
        args += hosted.arrays
        out_specs += [ANY_SPEC] * hn
        out_shape += hosted.out_shapes()
        scratch += hosted.sem_shapes()
    return pl.pallas_call(
        body, name="inproj", grid=(1 + last_p, ni),
        in_specs=in_specs, out_specs=tuple(out_specs), out_shape=tuple(out_shape), scratch_shapes=scratch,
        compiler_params=_params(("arbitrary", "arbitrary"), BIG_VMEM_LIMIT),
    )(*args)


def _fill_bias_table(tbl, rows, keys_first=False):
    shape = (2 * BLK, BLK) if keys_first else (BLK, 2 * BLK)
    qi = lax.broadcasted_iota(jnp.int32, shape, 1 if keys_first else 0)
    kj = lax.broadcasted_iota(jnp.int32, shape, 0 if keys_first else 1)
    dist = rows.pos(qi) - rows.pos(kj % BLK) + jnp.where(kj < BLK, BLK, 0)
    inside = (dist >= 0) & (dist <= BLK)
    negd = (dist * (-rows.dil)).astype(F32)
    for f, valid in enumerate((inside & (kj >= BLK), inside)):
        for h in range(N_Q_HEADS):
            tbl[f * N_Q_HEADS + h] = jnp.where(valid, SLOPES[h] * negd, NEG)


def _bias2(tbl, n, h0, h1, axis=0):
    base = jnp.where(n == 0, 0, N_Q_HEADS)
    return jnp.concatenate([tbl[base + h0], tbl[base + h1]], axis=axis)


def _head_operands(kv2, hk, lo_mask):
    half, pos = hk // 2, hk % 2
    out = []
    for base in (0, KV_W):
        t = kv2[:, base + half * LANES: base + (half + 1) * LANES]
        sw = pltpu.roll(t, HEAD_DIM, axis=1)
        at_lo, at_hi = (t, sw) if pos == 0 else (sw, t)
        out.append(jnp.where(lo_mask, at_lo, 0.0).astype(BF16))
        out.append(jnp.where(lo_mask, 0.0, at_hi).astype(BF16))
    return out


def _nt(a, b):
    return lax.dot_general(a, b, (((1,), (1,)), ((), ())), preferred_element_type=F32)


def _tn(a, b):
    return lax.dot_general(a, b, (((0,), (0,)), ((), ())), preferred_element_type=F32)


def _attn_fwd(q, kv, dil, name, prev=(), gate=None):
    S = q.shape[0]
    rows = _Rows(dil, S)
    nb = rows.nb
    have_prev, last = len(prev) > 0, gate is not None

    def body(*refs):
        refs = list(refs)
        q_ref, kvc_ref, kvp_ref = refs[:3]
        del refs[:3]
        po_refs, pl_refs = refs[0:2 * len(prev):2], refs[1:2 * len(prev):2]
        del refs[:2 * len(prev)]
        if last:
            gate_ref = refs.pop(0)
        o_ref, lse_ref = refs[:2]
        y_ref = refs[2] if last else None
        tbl = refs[-1]
        n = pl.program_id(1)

        @pl.when((pl.program_id(0) == 0) & (n == 0))
        def _():
            _fill_bias_table(tbl, rows)

        kv2 = jnp.concatenate([_ld(kvp_ref), _ld(kvc_ref)], axis=0)
        lo_mask = lax.broadcasted_iota(jnp.int32, (2 * BLK, LANES), 1) < HEAD_DIM
        lane = lax.broadcasted_iota(jnp.int32, (BLK, LANES), 1)
        stats = jnp.zeros((BLK, LANES), F32)
        for hk in range(N_KV_HEADS):
            k_lo, k_hi, v_lo, v_hi = _head_operands(kv2, hk, lo_mask)
            cols = [slice(b * LANES, (b + 1) * LANES) for b in (2 * hk, 2 * hk + 1)]
            q2 = jnp.concatenate([_ld(q_ref, cols[0]), _ld(q_ref, cols[1])], axis=0).astype(BF16)
            o2 = jnp.zeros((2 * BLK, LANES), F32)
            for which, (kk, vv) in enumerate(((k_lo, v_lo), (k_hi, v_hi))):
                h0, h1 = 4 * hk + which, 4 * hk + 2 + which
                s = _nt(q2, kk) + _bias2(tbl, n, h0, h1)
                m = jnp.max(s, axis=1, keepdims=True)
                p = jnp.exp(s - m)
                l = jnp.sum(p, axis=1, keepdims=True)
                o2 = o2 + jnp.dot(p.astype(BF16), vv, preferred_element_type=F32) * (1.0 / l)
                lse = m + jnp.log(l)
                stats = jnp.where(lane == h0, lse[0:BLK], stats)
                stats = jnp.where(lane == h1, lse[BLK:], stats)
            _st(o_ref, o2[0:BLK], cols[0])
            _st(o_ref, o2[BLK:], cols[1])
        if have_prev:
            others = [_ld(r) for r in pl_refs]
            top = stats
            for b in others:
                top = jnp.maximum(top, b)
            e_new = jnp.exp(stats - top)
            e_old = [jnp.exp(b - top) for b in others]
            total = e_new
            for e in e_old:
                total = total + e
            stats = top + jnp.log(total)
            inv = 1.0 / total
            w_new, w_old = e_new * inv, [e * inv for e in e_old]
        if have_prev or last:
            lo = lane < HEAD_DIM
            for blk in range(ATT_W // LANES):
                cols = slice(blk * LANES, (blk + 1) * LANES)
                o_blk = _ld(o_ref, cols)
                if have_prev:
                    pick = lambda w: jnp.where(lo, w[:, 2 * blk:2 * blk + 1], w[:, 2 * blk + 1:2 * blk + 2])
                    o_blk = o_blk * pick(w_new)
                    for po_ref, w in zip(po_refs, w_old):
                        o_blk = o_blk + _ld(po_ref, cols) * pick(w)
                    _st(o_ref, o_blk, cols)
                if last:
                    a = _ld(gate_ref, cols)
                    _st(y_ref, (o_blk * (a * _sigmoid(a))).astype(BF16), cols)
        _st(lse_ref, stats)

    here = lambda n: n
    before_n = lambda n: jnp.maximum(n - 1, 0)
    in_specs = [rows.spec(ATT_W, here), rows.spec(2 * KV_W, here), rows.spec(2 * KV_W, before_n)]
    args = [rows.of(q), rows.of(kv), rows.of(kv)]
    for o_other, lse_other in prev:
        in_specs += [rows.spec(ATT_W, here), rows.spec(LANES, here)]
        args += [rows.of(o_other), rows.of(lse_other)]
    out_specs = [rows.spec(ATT_W, here), rows.spec(LANES, here)]
    out_shape = [jax.ShapeDtypeStruct(rows.view + (ATT_W,), F32), jax.ShapeDtypeStruct(rows.view + (LANES,), F32)]
    if last:
        in_specs.append(rows.spec(ATT_W, here))
        args.append(rows.of(gate))
        out_specs.append(rows.spec(ATT_W, here))
        out_shape.append(jax.ShapeDtypeStruct(rows.view + (ATT_W,), BF16))
    res = pl.pallas_call(
        body, name=name, grid=(dil, nb),
        in_specs=in_specs, out_specs=tuple(out_specs), out_shape=tuple(out_shape),
        scratch_shapes=[pltpu.VMEM((2 * N_Q_HEADS, BLK, 2 * BLK), F32)],
        compiler_params=_params(("arbitrary", "arbitrary")),
    )(*args)
    return tuple(r.reshape(S, r.shape[-1]) for r in res)


def _shifted_copies(buf, phases):
    n = phases.shape[1]
    for b in range(1, 8):
        phases[b - 1] = buf[b:b + n, :]


def _window(buf, phases, start, cols):
    b = start % 8
    if b == 0:
        return buf[start:start + 8, cols]
    return phases[b - 1, start - b:start - b + 8, cols]


def _broadcast_taps(w_ref, wb):
    for j in range(CONV_K):
        wb[j] = jnp.broadcast_to(w_ref[j:j + 1, :], wb.shape[1:])


def _conv_fwd(gates, conv_w, conv_b, ln_g, ln_b, tt=256):
    S = gates.shape[0]
    C = conv_w.shape[1]
    hb = tt // CONV_HALO

    def body(val_ref, glu_ref, hval_ref, hglu_ref, gate_ref, w_ref, b_ref, g_ref, beta_ref,
             conv_ref, y_ref, hbuf, hph):
        i = pl.program_id(0)
        halo = hval_ref[...] * _sigmoid(hglu_ref[...])
        hbuf[0:CONV_HALO, :] = jnp.where(i > 0, halo, 0.0)
        hbuf[CONV_HALO:, :] = val_ref[...] * _sigmoid(glu_ref[...])
        _shifted_copies(hbuf, hph)
        for cb in range(C // LANES):
            cols = slice(cb * LANES, (cb + 1) * LANES)
            wj = [jnp.broadcast_to(w_ref[j:j + 1, cols], (8, LANES)) for j in range(CONV_K)]
            for rc in range(tt // 8):
                acc = jnp.zeros((8, LANES), F32)
                for j in range(CONV_K):
                    start = rc * 8 + CONV_HALO - (CONV_K - 1) + j
                    acc = acc + _window(hbuf, hph, start, cols) * wj[j]
                conv_ref[rc * 8:(rc + 1) * 8, cols] = acc
        cv = conv_ref[...] + b_ref[...]
        conv_ref[...] = cv
        mu = jnp.mean(cv, axis=-1, keepdims=True)
        xc = cv - mu
        var = jnp.mean(xc * xc, axis=-1, keepdims=True)
        ln = xc * lax.rsqrt(var + LN_EPS) * g_ref[...] + beta_ref[...]
        gt = gate_ref[...]
        y_ref[...] = (ln * _sigmoid(ln) * (gt * _sigmoid(gt))).astype(BF16)

    vec = pl.BlockSpec((1, C), lambda i: (0, 0))
    return pl.pallas_call(
        body, name="conv_fwd", grid=(S // tt,),
        in_specs=[pl.BlockSpec((tt, C), lambda i: (i, 0)),
                  pl.BlockSpec((tt, C), lambda i: (i, 1)),
                  pl.BlockSpec((CONV_HALO, C), lambda i: (jnp.maximum(i * hb - 1, 0), 0)),
                  pl.BlockSpec((CONV_HALO, C), lambda i: (jnp.maximum(i * hb - 1, 0), 1)),
                  pl.BlockSpec((tt, C), lambda i: (i, 2)),
                  pl.BlockSpec((CONV_HALO, C), lambda i: (0, 0)), vec, vec, vec],
        out_specs=(pl.BlockSpec((tt, C), lambda i: (i, 0)), pl.BlockSpec((tt, C), lambda i: (i, 0))),
        out_shape=(jax.ShapeDtypeStruct((S, C), F32), jax.ShapeDtypeStruct((S, C), BF16)),
        scratch_shapes=[pltpu.VMEM((tt + CONV_HALO, C), F32), pltpu.VMEM((7, tt + CONV_HALO - 8, C), F32)],
        compiler_params=_params(("parallel",)),
    )(gates, gates, gates, gates, gates, conv_w, conv_b, ln_g, ln_b)


def _outproj_loss(x, y_att, y_conv, w_out, gf, target, tm=512):
    S, D = x.shape
    E = y_att.shape[1]

    def body(x_ref, ya_ref, yc_ref, w_ref, gf_ref, t_ref, dx_ref, dxb_ref, loss_ref, ggf_ref):
        @pl.when(pl.program_id(0) == 0)
        def _():
            loss_ref[...] = jnp.zeros_like(loss_ref)
            ggf_ref[...] = jnp.zeros_like(ggf_ref)

        x2 = (x_ref[...] + jnp.dot(_perm_rows(ya_ref[...], True), w_ref[0:E, :], preferred_element_type=F32)
              + jnp.dot(yc_ref[...], w_ref[E:, :], preferred_element_type=F32))
        r = lax.rsqrt(jnp.mean(x2 * x2, axis=-1, keepdims=True) + NORM_EPS)
        nrm = x2 * r
        gfv = gf_ref[...]
        err = nrm * gfv - t_ref[...]
        loss_ref[...] += jnp.sum(err * err, axis=0, keepdims=True)
        dout = err * (1.0 / D)
        ggf_ref[...] += jnp.sum(dout * nrm, axis=0, keepdims=True)
        dn = dout * gfv
        dx2 = r * (dn - nrm * jnp.mean(dn * nrm, axis=-1, keepdims=True))
        dx_ref[...] = dx2
        dxb_ref[...] = dx2.astype(BF16)

    row = lambda w: pl.BlockSpec((tm, w), lambda i: (i, 0))
    vec = pl.BlockSpec((1, D), lambda i: (0, 0))
    return pl.pallas_call(
        body, name="outproj_loss", grid=(S // tm,),
        in_specs=[row(D), row(E), row(E), pl.BlockSpec((2 * E, D), lambda i: (0, 0)), vec, row(D)],
        out_specs=(row(D), row(D), vec, vec),
        out_shape=(jax.ShapeDtypeStruct((S, D), F32), jax.ShapeDtypeStruct((S, D), BF16),
                   jax.ShapeDtypeStruct((1, D), F32), jax.ShapeDtypeStruct((1, D), F32)),
        compiler_params=_params(("arbitrary",)),
    )(x, y_att, y_conv, w_out, gf, target)


def _split3(v):
    hi = v.astype(BF16)
    r1 = v - hi.astype(F32)
    mid = r1.astype(BF16)
    lo = (r1 - mid.astype(F32)).astype(BF16)
    return hi, mid, lo


def _dy_att(dxb, w_out, gates, o, tm=512):
    S, D = dxb.shape
    E = ATT_W

    def body(dx_ref, w_ref, a_ref, o_ref, do_ref, da_ref, dl_ref, dxr_ref):
        dxr = _perm_rows(dx_ref[...], False)
        dxr_ref[...] = dxr
        dya = _nt(dxr, w_ref[...])
        a = a_ref[...]
        ov = o_ref[...]
        sl, dsl = _silu_and_grad(a)
        d_o = dya * sl
        do_ref[...] = d_o
        da_ref[...] = (dya * ov * dsl).astype(BF16)
        ci = lax.broadcasted_iota(jnp.int32, (E, LANES), 0) // HEAD_DIM
        hi = lax.broadcasted_iota(jnp.int32, (E, LANES), 1)
        sel = jnp.where(ci == hi, 1.0, 0.0).astype(BF16)
        acc = jnp.zeros((tm, LANES), F32)
        for part in _split3(d_o * ov):
            acc = acc + jnp.dot(part, sel, preferred_element_type=F32)
        dl_ref[...] = acc

    row = lambda w: pl.BlockSpec((tm, w), lambda i: (i, 0))
    return pl.pallas_call(
        body, name="dy_att", grid=(S // tm,),
        in_specs=[row(D), pl.BlockSpec((E, D), lambda i: (0, 0)), row(E), row(E)],
        out_specs=(row(E), row(E), row(LANES), row(D)),
        out_shape=(jax.ShapeDtypeStruct((S, E), F32), jax.ShapeDtypeStruct((S, E), BF16),
                   jax.ShapeDtypeStruct((S, LANES), F32), jax.ShapeDtypeStruct((S, D), BF16)),
        compiler_params=_params(("parallel",)),
    )(dxb, w_out, gates, o)


def _dy_conv(dxb, w_out, gates, conv_out, ln_g, ln_b, tm=512):
    S, D = dxb.shape
    C = conv_out.shape[1]

    def body(dx_ref, w_ref, gate_ref, cv_ref, g_ref, beta_ref, dgate_ref, dconv_ref, gg_ref, gb_ref, gcb_ref):
        @pl.when(pl.program_id(0) == 0)
        def _():
            gg_ref[...] = jnp.zeros_like(gg_ref)
            gb_ref[...] = jnp.zeros_like(gb_ref)
            gcb_ref[...] = jnp.zeros_like(gcb_ref)

        dyc = _nt(dx_ref[...], w_ref[...])
        cv = cv_ref[...]
        mu = jnp.mean(cv, axis=-1, keepdims=True)
        xc = cv - mu
        rstd = lax.rsqrt(jnp.mean(xc * xc, axis=-1, keepdims=True) + LN_EPS)
        nrm = xc * rstd
        gv = g_ref[...]
        ln = nrm * gv + beta_ref[...]
        u, du = _silu_and_grad(ln)
        gt = gate_ref[...]
        g2, dg2 = _silu_and_grad(gt)
        dgate_ref[...] = (dyc * u * dg2).astype(BF16)
        d_ln = dyc * g2 * du
        gb_ref[...] += jnp.sum(d_ln, axis=0, keepdims=True)
        gg_ref[...] += jnp.sum(d_ln * nrm, axis=0, keepdims=True)
        dn = d_ln * gv
        d_conv = rstd * (dn - jnp.mean(dn, axis=-1, keepdims=True)
                         - nrm * jnp.mean(dn * nrm, axis=-1, keepdims=True))
        dconv_ref[...] = d_conv
        gcb_ref[...] += jnp.sum(d_conv, axis=0, keepdims=True)

    row = lambda w: pl.BlockSpec((tm, w), lambda i: (i, 0))
    vec = pl.BlockSpec((1, C), lambda i: (0, 0))
    return pl.pallas_call(
        body, name="dy_conv", grid=(S // tm,),
        in_specs=[row(D), pl.BlockSpec((C, D), lambda i: (1, 0)),
                  pl.BlockSpec((tm, C), lambda i: (i, 2)), row(C), vec, vec],
        out_specs=(row(C), row(C), vec, vec, vec),
        out_shape=(jax.ShapeDtypeStruct((S, C), BF16), jax.ShapeDtypeStruct((S, C), F32),
                   jax.ShapeDtypeStruct((1, C), F32), jax.ShapeDtypeStruct((1, C), F32),
                   jax.ShapeDtypeStruct((1, C), F32)),
        compiler_params=_params(("arbitrary",)),
    )(dxb, w_out, gates, conv_out, ln_g, ln_b)


def _conv_bwd(d_conv, gates, d_c_gate, conv_w, hosted=None, tt=256):
    S, C = d_conv.shape
    hb = tt // CONV_HALO
    nt = S // tt
    hn = hosted.n if hosted is not None else 0

    def body(*refs):
        dc_ref, dnext_ref, val_ref, glu_ref, dg_ref, w_ref = refs[:6]
        h_ins = refs[6:6 + hn]
        out_ref, gw_ref = refs[6 + hn:8 + hn]
        h_outs = refs[8 + hn:8 + 2 * hn]
        hbuf, dbuf, dhbuf, dph, wb = refs[8 + 2 * hn:13 + 2 * hn]
        h_sems = refs[13 + 2 * hn:]
        i = pl.program_id(0)

        @pl.when(i == 0)
        def _():
            gw_ref[...] = jnp.zeros_like(gw_ref)
            _broadcast_taps(w_ref, wb)
            if hosted is not None:
                hosted.start(h_ins, h_outs, h_sems)

        val = val_ref[...]
        sg = _sigmoid(glu_ref[...])
        hbuf[...] = val * sg
        dbuf[0:tt, :] = dc_ref[...]
        dbuf[tt:, :] = jnp.where(i < nt - 1, dnext_ref[...], 0.0)
        _shifted_copies(dbuf, dph)
        for cb in range(C // LANES):
            cols = slice(cb * LANES, (cb + 1) * LANES)
            gacc = [jnp.zeros((8, LANES), F32) for _ in range(CONV_K)]
            group = 2
            for rc0 in range(0, tt // 8, group):
                hcur = [hbuf[(rc0 + r) * 8:(rc0 + r + 1) * 8, cols] for r in range(group)]
                accs = [jnp.zeros((8, LANES), F32) for _ in range(group)]
                for j in range(CONV_K):
                    wj = wb[j, :, cols]
                    for r in range(group):
                        dwin = _window(dbuf, dph, (rc0 + r) * 8 + (CONV_K - 1) - j, cols)
                        accs[r] = accs[r] + dwin * wj
                        gacc[j] = gacc[j] + dwin * hcur[r]
                for r in range(group):
                    dhbuf[(rc0 + r) * 8:(rc0 + r + 1) * 8, cols] = accs[r]
            for j in range(CONV_K):
                gw_ref[j:j + 1, cols] += jnp.sum(gacc[j], axis=0, keepdims=True)
        d_h = dhbuf[...]
        out_ref[:, 0:C] = (d_h * sg).astype(BF16)
        out_ref[:, C:2 * C] = (d_h * val * sg * (1.0 - sg)).astype(BF16)
        out_ref[:, 2 * C:3 * C] = dg_ref[...]

        if hosted is not None:
            @pl.when(i == nt - 1)
            def _():
                hosted.finish(h_ins, h_outs, h_sems)

    tile = lambda col: pl.BlockSpec((tt, C), lambda i: (i, col))
    in_specs = [tile(0),
                pl.BlockSpec((CONV_HALO, C), lambda i: (jnp.minimum((i + 1) * hb, S // CONV_HALO - 1), 0)),
                tile(0), tile(1), tile(0),
                pl.BlockSpec((CONV_HALO, C), lambda i: (0, 0))]
    args = [d_conv, d_conv, gates, gates, d_c_gate, conv_w]
    out_specs = [pl.BlockSpec((tt, 3 * C), lambda i: (i, 0)), pl.BlockSpec((CONV_HALO, C), lambda i: (0, 0))]
    out_shape = [jax.ShapeDtypeStruct((S, 3 * C), BF16), jax.ShapeDtypeStruct((CONV_HALO, C), F32)]
    scratch = [pltpu.VMEM((tt, C), F32), pltpu.VMEM((tt + CONV_HALO, C), F32), pltpu.VMEM((tt, C), F32),
               pltpu.VMEM((7, tt + CONV_HALO - 8, C), F32), pltpu.VMEM((CONV_K, 8, C), F32)]
    if hosted is not None:
        in_specs += [ANY_SPEC] * hn
        args += hosted.arrays
        out_specs += [ANY_SPEC] * hn
        out_shape += hosted.out_shapes()
        scratch += hosted.sem_shapes()
    res = pl.pallas_call(
        body, name="conv_bwd", grid=(nt,),
        in_specs=in_specs, out_specs=tuple(out_specs), out_shape=tuple(out_shape), scratch_shapes=scratch,
        compiler_params=_params(("arbitrary",)),
    )(*args)
    return res[0], res[1], list(res[2:])


def _attn_bwd(q, kv, d_o, lse, delta, dil, prev, final, name, hosted=None):
    S = q.shape[0]
    rows = _Rows(dil, S)
    nb = rows.nb
    steps = dil * nb
    out_dt = BF16 if final else F32
    have_prev = prev is not None
    hn = hosted.n if hosted is not None else 0

    def body(*refs):
        refs = list(refs)
        q_ref, do_ref, lse_ref, dl_ref, kvc_ref, kvp_ref = refs[:6]
        del refs[:6]
        if have_prev:
            pdq_ref, pdkv_ref = refs[:2]
            del refs[:2]
        h_ins = refs[:hn]
        dq_ref, dkv_ref = refs[hn:hn + 2]
        h_outs = refs[hn + 2:2 * hn + 2]
        carry, tbl = refs[2 * hn + 2:2 * hn + 4]
        h_sems = refs[2 * hn + 4:]
        t = pl.program_id(0)
        n = t % nb

        @pl.when(t == 0)
        def _():
            if hosted is not None:
                hosted.start(h_ins, h_outs, h_sems)
            _fill_bias_table(tbl, rows, keys_first=True)
            carry[...] = jnp.zeros_like(carry)

        @pl.when(t < steps)
        def _():
            kv2 = jnp.concatenate([_ld(kvp_ref), _ld(kvc_ref)], axis=0)
            lse_t, dl_t = _ld(lse_ref).T, _ld(dl_ref).T
            lo_mask = lax.broadcasted_iota(jnp.int32, (2 * BLK, LANES), 1) < HEAD_DIM
            halves = [jnp.zeros((2 * BLK, LANES), F32) for _ in range(4)]
            for hk in range(N_KV_HEADS):
                k_lo, k_hi, v_lo, v_hi = _head_operands(kv2, hk, lo_mask)
                cols = [slice(b * LANES, (b + 1) * LANES) for b in (2 * hk, 2 * hk + 1)]
                q2 = jnp.concatenate([_ld(q_ref, cols[0]), _ld(q_ref, cols[1])], axis=0).astype(BF16)
                do2 = jnp.concatenate([_ld(do_ref, cols[0]), _ld(do_ref, cols[1])], axis=0).astype(BF16)
                dq2 = jnp.zeros((2 * BLK, LANES), F32)
                dks, dvs = [], []
                for which, (kk, vv) in enumerate(((k_lo, v_lo), (k_hi, v_hi))):
                    h0, h1 = 4 * hk + which, 4 * hk + 2 + which
                    s = _nt(kk, q2) + _bias2(tbl, n, h0, h1, axis=1)
                    lse2 = jnp.concatenate([lse_t[h0:h0 + 1, :], lse_t[h1:h1 + 1, :]], axis=1)
                    dl2 = jnp.concatenate([dl_t[h0:h0 + 1, :], dl_t[h1:h1 + 1, :]], axis=1)
                    p = jnp.exp(s - lse2)
                    ds = (p * (_nt(vv, do2) - dl2)).astype(BF16)
                    dq2 = dq2 + _tn(ds, kk)
                    dks.append(jnp.dot(ds, q2, preferred_element_type=F32))
                    dvs.append(jnp.dot(p.astype(BF16), do2, preferred_element_type=F32))
                dk_sum = jnp.where(lo_mask, dks[0], dks[1])
                dv_sum = jnp.where(lo_mask, dvs[0], dvs[1])
                for jp in range(2):
                    dq_blk = dq2[jp * BLK:(jp + 1) * BLK]
                    if have_prev:
                        dq_blk = dq_blk + _ld(pdq_ref, cols[jp])
                    if final:
                        dq_blk = dq_blk * (HEAD_DIM ** -0.5)
                    _st(dq_ref, dq_blk.astype(out_dt), cols[jp])
                half, pos = hk // 2, hk % 2
                here = lo_mask if pos == 0 else jnp.logical_not(lo_mask)
                dk_tot = dk_sum + pltpu.roll(dk_sum, HEAD_DIM, axis=1)
                dv_tot = dv_sum + pltpu.roll(dv_sum, HEAD_DIM, axis=1)
                halves[half] = halves[half] + jnp.where(here, dk_tot, 0.0)
                halves[2 + half] = halves[2 + half] + jnp.where(here, dv_tot, 0.0)
            for b in range(4):
                cols = slice(b * LANES, (b + 1) * LANES)
                done = carry[:, cols] + halves[b][0:BLK, :]
                if have_prev:
                    done = done + _ld(pdkv_ref, cols)
                _st(dkv_ref, done.astype(out_dt), cols)
                carry[:, cols] = halves[b][BLK:, :]

        @pl.when(t == steps)
        def _():
            done = carry[...]
            if have_prev:
                done = done + _ld(pdkv_ref)
            _st(dkv_ref, done.astype(out_dt))
            if hosted is not None:
                hosted.finish(h_ins, h_outs, h_sems)

    def spec(width, lag):
        def index(t):
            u = jnp.clip(t - lag, 0, steps - 1)
            return rows.index(u // nb, u % nb)
        return pl.BlockSpec(rows.block + (width,), index)

    def key_prev(t):
        u = jnp.minimum(t, steps - 1)
        return rows.index(u // nb, jnp.maximum(u % nb - 1, 0))

    in_specs = [spec(ATT_W, 0), spec(ATT_W, 0), spec(LANES, 0), spec(LANES, 0), spec(2 * KV_W, 0),
                pl.BlockSpec(rows.block + (2 * KV_W,), key_prev)]
    args = [rows.of(q), rows.of(d_o), rows.of(lse), rows.of(delta), rows.of(kv), rows.of(kv)]
    if have_prev:
        in_specs += [spec(ATT_W, 0), spec(2 * KV_W, 1)]
        args += [rows.of(prev[0]), rows.of(prev[1])]
    out_specs = [spec(ATT_W, 0), spec(2 * KV_W, 1)]
    out_shape = [jax.ShapeDtypeStruct(rows.view + (ATT_W,), out_dt),
                 jax.ShapeDtypeStruct(rows.view + (2 * KV_W,), out_dt)]
    scratch = [pltpu.VMEM((BLK, 2 * KV_W), F32), pltpu.VMEM((2 * N_Q_HEADS, 2 * BLK, BLK), F32)]
    if hosted is not None:
        in_specs += [ANY_SPEC] * hn
        args += hosted.arrays
        out_specs += [ANY_SPEC] * hn
        out_shape += hosted.out_shapes()
        scratch += hosted.sem_shapes()
    res = pl.pallas_call(
        body, name=name, grid=(steps + 1,),
        in_specs=in_specs, out_specs=tuple(out_specs), out_shape=tuple(out_shape), scratch_shapes=scratch,
        compiler_params=_params(("arbitrary",)),
    )(*args)
    return (res[0].reshape(S, ATT_W), res[1].reshape(S, 2 * KV_W)), list(res[2:])


def _dh(segments, w_in, x, dx2, g, hosted=None, tm=1024, tk=512):
    S, D = x.shape
    ns = len(segments)
    counts = [a.shape[1] // tk for a, _ in segments]
    starts = [sum(counts[:s]) for s in range(ns)]
    nk = sum(counts)
    hn = hosted.n if hosted is not None else 0

    def body(*refs):
        seg_refs = refs[:ns]
        w_ref, x_ref, dx2_ref, g_ref = refs[ns:ns + 4]
        h_ins = refs[ns + 4:ns + 4 + hn]
        gx_ref, gng_ref = refs[ns + 4 + hn:ns + 6 + hn]
        h_outs = refs[ns + 6 + hn:ns + 6 + 2 * hn]
        acc = refs[ns + 6 + 2 * hn]
        h_sems = refs[ns + 7 + 2 * hn:]
        k, i = pl.program_id(0), pl.program_id(1)

        @pl.when((i == 0) & (k == 0))
        def _():
            gng_ref[...] = jnp.zeros_like(gng_ref)
            if hosted is not None:
                hosted.start(h_ins, h_outs, h_sems)

        @pl.when(k == 0)
        def _():
            acc[i] = jnp.zeros(acc.shape[1:], F32)

        for s in range(ns):
            @pl.when((k >= starts[s]) & (k < starts[s] + counts[s]))
            def _(s=s):
                t = seg_refs[s][...]
                if segments[s][1]:
                    t = _perm_rows(t, True)
                acc[i] += jnp.dot(t, w_ref[...], preferred_element_type=F32)

        @pl.when(k == nk - 1)
        def _():
            dh = acc[i]
            xf = x_ref[...]
            r = lax.rsqrt(jnp.mean(xf * xf, axis=-1, keepdims=True) + NORM_EPS)
            nrm = xf * r
            gng_ref[...] += jnp.sum(dh * nrm, axis=0, keepdims=True)
            dn = dh * g_ref[...]
            gx_ref[...] = dx2_ref[...] + r * (dn - nrm * jnp.mean(dn * nrm, axis=-1, keepdims=True))

        if hosted is not None:
            @pl.when((i == S // tm - 1) & (k == nk - 1))
            def _():
                hosted.finish(h_ins, h_outs, h_sems)

    ni = S // tm
    row = pl.BlockSpec((tm, D), lambda k, i: (jnp.where(k == nk - 1, i, 0), 0))
    vec = pl.BlockSpec((1, D), lambda k, i: (0, 0))

    def seg_index(s):
        def index(k, i):
            j = k - starts[s]
            return jnp.where(j < 0, 0, jnp.where(j >= counts[s], ni - 1, i)), jnp.clip(j, 0, counts[s] - 1)
        return index

    in_specs = [pl.BlockSpec((tm, tk), seg_index(s)) for s in range(ns)]
    in_specs += [pl.BlockSpec((tk, D), lambda k, i: (k, 0)), row, row, vec]
    args = [a for a, _ in segments] + [w_in, x, dx2, g]
    out_specs = [row, vec]
    out_shape = [jax.ShapeDtypeStruct((S, D), F32), jax.ShapeDtypeStruct((1, D), F32)]
    scratch = [pltpu.VMEM((ni, tm, D), F32)]
    if hosted is not None:
        in_specs += [ANY_SPEC] * hn
        args += hosted.arrays
        out_specs += [ANY_SPEC] * hn
        out_shape += hosted.out_shapes()
        scratch += hosted.sem_shapes()
    res = pl.pallas_call(
        body, name="dh", grid=(nk, S // tm),
        in_specs=in_specs, out_specs=tuple(out_specs), out_shape=tuple(out_shape), scratch_shapes=scratch,
        compiler_params=_params(("arbitrary", "arbitrary"), BIG_VMEM_LIMIT),
    )(*args)
    return res[0], res[1], list(res[2:])


def _tn_matmul(a, bs, name, b_first=False, tm=512):
    M, K = a.shape
    nb = len(bs)
    shapes = [(b.shape[1], K) if b_first else (K, b.shape[1]) for b in bs]

    def body(a_ref, *refs):
        @pl.when(pl.program_id(0) == 0)
        def _():
            for o_ref in refs[nb:]:
                o_ref[...] = jnp.zeros_like(o_ref)

        at = a_ref[...]
        for b_ref, o_ref in zip(refs[:nb], refs[nb:]):
            for c in range(0, b_ref.shape[1], 512):
                if b_first:
                    o_ref[c:c + 512, :] += _tn(b_ref[:, c:c + 512], at)
                else:
                    o_ref[:, c:c + 512] += _tn(at, b_ref[:, c:c + 512])

    return pl.pallas_call(
        body, name=name, grid=(M // tm,),
        in_specs=[pl.BlockSpec((tm, K), lambda m: (m, 0))] + [pl.BlockSpec((tm, b.shape[1]), lambda m: (m, 0))
                                                              for b in bs],
        out_specs=tuple(pl.BlockSpec(s, lambda m: (0, 0)) for s in shapes),
        out_shape=tuple(jax.ShapeDtypeStruct(s, F32) for s in shapes),
        compiler_params=_params(("arbitrary",)),
    )(a, *bs)


def _adamw(parts, w, m, v, name, tr=None, split=None, by_chip=False):
    R, C = w.shape
    tr = R if tr is None else tr
    parts = [parts] if split is None else list(parts)
    npar = len(parts)

    def total(p_ref):
        if by_chip:
            c = lax.axis_index("c")
            g = p_ref[c].astype(F32)
            for chip in range(1, N_DEV // 2):
                g = g + p_ref[2 * chip + c].astype(F32)
            return g
        g = p_ref[0].astype(F32)
        for dev in range(1, N_DEV):
            g = g + p_ref[dev].astype(F32)
        return g

    def body(*refs):
        w_ref, m_ref, v_ref, g_out, d_out, m_out, v_out = refs[npar:]
        if split is None:
            g = total(refs[0])
        else:
            g = jnp.where(_mesh_pos()[3] < split, total(refs[0]), total(refs[1]))
        mn = ADAM_B1 * m_ref[...] + (1.0 - ADAM_B1) * g
        vn = ADAM_B2 * v_ref[...] + (1.0 - ADAM_B2) * (g * g)
        m_hat = mn / (1.0 - ADAM_B1 ** ADAM_STEP)
        v_hat = vn / (1.0 - ADAM_B2 ** ADAM_STEP)
        g_out[...] = g
        d_out[...] = -ADAM_LR * (m_hat / (jnp.sqrt(v_hat) + ADAM_EPS) + ADAM_WD * w_ref[...])
        m_out[...] = mn
        v_out[...] = vn

    blk = pl.BlockSpec((tr, C), lambda i: (i, 0))
    shp = jax.ShapeDtypeStruct((R, C), F32)
    return pl.pallas_call(
        body, name=name, grid=(R // tr,),
        in_specs=[pl.BlockSpec((N_DEV, tr, C), lambda i: (0, i, 0))] * npar + [blk, blk, blk],
        out_specs=(blk, blk, blk, blk), out_shape=(shp, shp, shp, shp),
        compiler_params=_params(("parallel",)),
    )(*parts, w, m, v)


def _local_step(x, target, norm_g, w_in, conv_w, conv_b, ln_g, ln_b, w_out, gf, exchanges=None, first_weights=None,
                late_weights=None):
    ex_out, ex_att, ex_conv = exchanges if exchanges is not None else (None, None, None)
    h_rm, h, *first = _norm_rows(x, norm_g, first_weights[0] if first_weights is not None else None)
    if first_weights is not None:
        w_in = first_weights[1](first)
    conv_cols = w_in.shape[0] - 2 * ATT_W - 2 * KV_W
    q, kv, a_gate, gates, *gathered = _inproj(
        h_rm, h, w_in,
        [(ATT_W, HEAD_DIM ** -0.5, True), (2 * KV_W, 1.0, True), (ATT_W, 1.0, True), (conv_cols, 1.0, False)],
        late_weights[0] if late_weights is not None else None)
    if late_weights is not None:
        conv_w, w_out = late_weights[1](gathered)

    alone = [_attn_fwd(q, kv, dil, "attn_fwd_d%d" % dil) for _, dil in PATTERNS[1:]]
    o, lse, y_att = _attn_fwd(q, kv, PATTERNS[0][1], "attn_fwd_d%d" % PATTERNS[0][1], alone, a_gate)
    conv_out, y_conv = _conv_fwd(gates, conv_w, conv_b, ln_g, ln_b)
    dx2, dxb, loss_cols, g_gf = _outproj_loss(x, y_att, y_conv, w_out, gf, target)

    d_o, d_a_gate, delta, dxb_rm = _dy_att(dxb, w_out, a_gate, o)
    g_w_out = jnp.concatenate([_tn_matmul(y_att, [dxb_rm], "gw_out_att")[0],
                               _tn_matmul(y_conv, [dxb], "gw_out_conv")[0]], axis=0)
    acc, out_parts = None, []
    for idx, (_, dil) in enumerate(reversed(PATTERNS)):
        hosted = ex_out(g_w_out) if (idx == 0 and ex_out is not None) else None
        acc, outs = _attn_bwd(q, kv, d_o, lse, delta, dil, acc, idx == len(PATTERNS) - 1, "attn_bwd_d%d" % dil,
                              hosted)
        out_parts += outs
    dq, dkv = acc
    g_q, g_kv, g_a = _tn_matmul(h_rm, [dq, dkv, d_a_gate], "gw_in_att", b_first=True)

    d_c_gate, d_conv, g_ln_g, g_ln_b, g_conv_b = _dy_conv(dxb, w_out, gates, conv_out, ln_g, ln_b)
    dgates, g_conv_w, att_parts = _conv_bwd(d_conv, gates, d_c_gate, conv_w,
                                            ex_att(g_q, g_kv, g_a) if ex_att is not None else None)
    g_c, = _tn_matmul(h, [dgates], "gw_in_conv", b_first=True)
    grad_x, g_norm_g, conv_parts = _dh(
        [(dq, True), (dkv, True), (d_a_gate, True), (dgates, False)], w_in, x, dx2, norm_g,
        ex_conv(g_a, g_c, g_conv_w) if ex_conv is not None else None)
    small = (g_norm_g, g_conv_b, g_ln_g, g_ln_b, g_gf, loss_cols)
    return grad_x, (g_q, g_kv, g_a, g_c), g_w_out, g_conv_w, small, (out_parts, att_parts, conv_parts)


def kernel(x, norm_g, w_in, conv_w, conv_b, conv_ln_g, conv_ln_b, w_out, final_norm_g, loss_target, m_norm_g, m_w_in, m_conv_w, m_conv_b, m_conv_ln_g, m_conv_ln_b, m_w_out, m_final_norm_g, v_norm_g, v_w_in, v_conv_w, v_conv_b, v_conv_ln_g, v_conv_ln_b, v_w_out, v_final_norm_g):
    S, D = x.shape[1], x.shape[2]
    win_sh, wout_sh, cw_sh = w_in[0].T, w_out[0], conv_w[0]
    cols_sh, rows_sh, ch_sh = win_sh.shape[0], wout_sh.shape[0], cw_sh.shape[1]

    def first_weights(gathered):
        return gathered[0].reshape(N_DEV * cols_sh, D)

    def late_weights(gathered):
        wout_all, cw_all = gathered
        conv_w_full = cw_all.transpose(1, 0, 2).reshape(CONV_K, N_DEV * ch_sh)
        return jnp.pad(conv_w_full, ((0, CONV_HALO - CONV_K), (0, 0))), wout_all.reshape(N_DEV * rows_sh, D)

    gf = final_norm_g.reshape(1, D)

    first = -(-(ATT_W + 2 * KV_W) // cols_sh)
    a_off = first * cols_sh - (ATT_W + 2 * KV_W)
    assert 0 <= a_off <= ATT_W

    def pieces(parts, n):
        return jnp.concatenate([p.astype(BF16) for p in parts], axis=0).reshape(n, cols_sh, D)

    def ex_out(g_w_out):
        return _Exchange([g_w_out.reshape(N_DEV, rows_sh, D).astype(BF16)], [(0, N_DEV)])

    same_core = (2, 4, 6)

    def ex_att(g_q, g_kv, g_a):
        mine = _chip_sum(pieces([g_q, g_kv, g_a[:a_off]], first), 0, "rs_att")
        return _Exchange([mine], [(0, first)], [same_core])

    def ex_conv(g_a, g_c, g_conv_w):
        mine = _chip_sum(pieces([g_a[a_off:], g_c], N_DEV - first), first, "rs_conv")
        return _Exchange(
            [mine, g_conv_w[:CONV_K].reshape(CONV_K, N_DEV, ch_sh).transpose(1, 0, 2)],
            [(first, N_DEV), (0, N_DEV)], [same_core, None])

    grad_x, _, _, _, small, parts = _local_step(
        x[0], loss_target[0], norm_g, None, None, conv_b, conv_ln_g, conv_ln_b, None, gf,
        (ex_out, ex_att, ex_conv), (_Gather([win_sh.astype(BF16)]), first_weights),
        (_Gather([wout_sh.astype(BF16), cw_sh]), late_weights))
    (wout_parts,), (win_parts_lo,), (win_parts_hi, cw_parts) = parts

    small_pack = jnp.concatenate(list(small) + [jnp.zeros((2, D), F32)], axis=0)
    small_parts, = _exchange(_Exchange([small_pack], [None]), "gather_small")

    upd_win = _adamw((win_parts_lo, win_parts_hi), win_sh, m_w_in[0].T, v_w_in[0].T, "adamw_w_in",
                     tr=cols_sh // 2, split=first, by_chip=True)
    upd_wout = _adamw(wout_parts, wout_sh, m_w_out[0], v_w_out[0], "adamw_w_out", tr=128)
    upd_cw = _adamw(cw_parts, cw_sh, m_conv_w[0], v_conv_w[0], "adamw_conv_w")
    zeros3 = jnp.zeros((3, D), F32)
    stack = lambda a, b, c, d_, e: jnp.concatenate([a, b, c, d_, e.reshape(1, D), zeros3], axis=0)
    upd_small = _adamw(
        small_parts,
        stack(norm_g, conv_b, conv_ln_g, conv_ln_b, final_norm_g),
        stack(m_norm_g, m_conv_b, m_conv_ln_g, m_conv_ln_b, m_final_norm_g),
        stack(v_norm_g, v_conv_b, v_conv_ln_g, v_conv_ln_b, v_final_norm_g) + jnp.concatenate(
            [jnp.zeros((5, D), F32), jnp.ones((3, D), F32)], axis=0),
        "adamw_small")

    loss = 0.5 / D * jnp.sum(upd_small[0][5])

    def outputs(kind):
        sm = upd_small[kind]
        return [sm[0:1], upd_win[kind].T[None], upd_cw[kind][None], sm[1:2], sm[2:3], sm[3:4],
                upd_wout[kind][None], sm[4]]

    return (loss, grad_x[None], *outputs(0), *outputs(1), *outputs(2), *outputs(3))
```

```python
import jax
import jax.numpy as jnp
from jax import lax
from jax.experimental import pallas as pl
from jax.experimental.pallas import tpu as pltpu

F32 = jnp.float32
BF16 = jnp.bfloat16

HEAD_DIM = 64
N_KV_HEADS = 4
N_Q_HEADS = 16
ATT_W = 1024
KV_W = 256
CONV_K = 31
CONV_HALO = 32
PATTERNS = ((128, 1), (512, 4), (2048, 16))
BLK = 128
LANES = 128
NORM_EPS = 1e-6
LN_EPS = 1e-5
NEG = -1e30
N_DEV = 8
ADAM_LR, ADAM_B1, ADAM_B2, ADAM_EPS, ADAM_WD, ADAM_STEP = 0.001, 0.9, 0.999, 1e-08, 0.01, 10
VMEM_LIMIT = 48 * 1024 * 1024
BIG_VMEM_LIMIT = 58 * 1024 * 1024
SLOPES = tuple(2.0 ** (-8.0 * (h + 1) / N_Q_HEADS) for h in range(N_Q_HEADS))
MESH = pl.DeviceIdType.MESH


def _params(sem, vmem_limit=VMEM_LIMIT):
    return pltpu.CompilerParams(dimension_semantics=sem, vmem_limit_bytes=vmem_limit)


def _sigmoid(v):
    return 1.0 / (1.0 + jnp.exp(-v))


def _silu_and_grad(v):
    s = _sigmoid(v)
    return v * s, s * (1.0 + v * (1.0 - s))


ANY_SPEC = pl.BlockSpec(memory_space=pl.ANY)


def _mesh_pos():
    x, y, c = lax.axis_index("x"), lax.axis_index("y"), lax.axis_index("c")
    return x, y, c, 4 * x + 2 * y + c


def _flipped(k, x, y, c):
    px = 1 - x if k & 4 else x
    py = 1 - y if k & 2 else y
    pc = 1 - c if k & 1 else c
    return (px, py, pc), 4 * px + 2 * py + pc


class _Exchange:
    def __init__(self, arrays, dests, flips=None):
        self.arrays, self.dests, self.n = list(arrays), list(dests), len(arrays)
        self.flips = [tuple(range(1, N_DEV)) if f is None else tuple(f)
                      for f in (flips if flips is not None else [None] * self.n)]

    def out_shapes(self):
        return [jax.ShapeDtypeStruct((N_DEV,) + a.shape[-2:], a.dtype) for a in self.arrays]

    def sem_shapes(self):
        return [pltpu.SemaphoreType.DMA((self.n, N_DEV - 1)), pltpu.SemaphoreType.DMA((self.n, N_DEV - 1)),
                pltpu.SemaphoreType.DMA((self.n,))]

    def _when(self, a, dev, fn):
        if self.dests[a] is None:
            fn()
        else:
            lo, hi = self.dests[a]
            pl.when((dev >= lo) & (dev < hi))(fn)

    def _mine(self, ins, a, dev):
        return ins[a] if self.dests[a] is None else ins[a].at[dev - self.dests[a][0]]

    def _copy(self, ins, outs, sems, a, k, src_dev, slot, target):
        return pltpu.make_async_remote_copy(
            src_ref=self._mine(ins, a, src_dev), dst_ref=outs[a].at[slot],
            send_sem=sems[0].at[a, k - 1], recv_sem=sems[1].at[a, k - 1],
            device_id=target, device_id_type=MESH)

    def start(self, ins, outs, sems):
        x, y, c, me = _mesh_pos()
        for a in range(self.n):
            self._when(a, me, lambda a=a: pltpu.make_async_copy(
                self._mine(ins, a, me), outs[a].at[me], sems[2].at[a]).start())
            for k in self.flips[a]:
                target, peer = _flipped(k, x, y, c)
                self._when(a, peer, lambda a=a, k=k, target=target, peer=peer: self._copy(
                    ins, outs, sems, a, k, peer, me, target).start())

    def finish(self, ins, outs, sems):
        x, y, c, me = _mesh_pos()
        lo0 = [0 if d is None else d[0] for d in self.dests]
        for a in range(self.n):
            for k in self.flips[a]:
                target, peer = _flipped(k, x, y, c)
                self._when(a, me, lambda a=a, k=k, peer=peer: self._copy(
                    ins, outs, sems, a, k, lo0[a], peer, (x, y, c)).wait_recv())
            for k in self.flips[a]:
                target, peer = _flipped(k, x, y, c)
                self._when(a, peer, lambda a=a, k=k, target=target, peer=peer: self._copy(
                    ins, outs, sems, a, k, peer, me, target).wait_send())
            self._when(a, me, lambda a=a: pltpu.make_async_copy(
                self._mine(ins, a, me), outs[a].at[me], sems[2].at[a]).wait())


def _exchange(ex, name):
    na = ex.n

    def body(*refs):
        ins, outs, sems = refs[:na], refs[na:2 * na], refs[2 * na:]
        ex.start(ins, outs, sems)
        ex.finish(ins, outs, sems)

    return pl.pallas_call(
        body, name=name, out_shape=tuple(ex.out_shapes()),
        in_specs=[ANY_SPEC] * na, out_specs=tuple([ANY_SPEC] * na), scratch_shapes=ex.sem_shapes(),
    )(*ex.arrays)


def _chip_sum(pieces, lo, name):
    n, R, C = pieces.shape

    def swap(p_ref, t_ref, send_sems, recv_sems):
        x, y, c, me = _mesh_pos()
        for i in range(n):
            mine = (lo + i) % 2
            cp = pltpu.make_async_remote_copy(
                src_ref=p_ref.at[i], dst_ref=t_ref.at[i], send_sem=send_sems.at[i], recv_sem=recv_sems.at[i],
                device_id=(x, y, 1 - c), device_id_type=MESH)
            pl.when(c != mine)(cp.start)
        for i in range(n):
            mine = (lo + i) % 2
            cp = pltpu.make_async_remote_copy(
                src_ref=p_ref.at[i], dst_ref=t_ref.at[i], send_sem=send_sems.at[i], recv_sem=recv_sems.at[i],
                device_id=(x, y, 1 - c), device_id_type=MESH)
            pl.when(c == mine)(cp.wait_recv)
            pl.when(c != mine)(cp.wait_send)

    other = pl.pallas_call(
        swap, name=name + "_swap", out_shape=jax.ShapeDtypeStruct(pieces.shape, pieces.dtype),
        in_specs=[ANY_SPEC], out_specs=ANY_SPEC,
        scratch_shapes=[pltpu.SemaphoreType.DMA((n,)), pltpu.SemaphoreType.DMA((n,))],
    )(pieces)

    def add(p_ref, t_ref, o_ref):
        o_ref[...] = (p_ref[...].astype(F32) + t_ref[...].astype(F32)).astype(o_ref.dtype)

    tr = R // 2
    blk = pl.BlockSpec((None, tr, C), lambda i, r: (i, r, 0))
    return pl.pallas_call(
        add, name=name + "_add", grid=(n, R // tr), in_specs=[blk, blk], out_specs=blk,
        out_shape=jax.ShapeDtypeStruct(pieces.shape, pieces.dtype),
        compiler_params=_params(("parallel", "parallel")),
    )(pieces, other)


class _Gather:
    def __init__(self, arrays):
        self.arrays, self.n = list(arrays), len(arrays)

    def out_shapes(self):
        return [jax.ShapeDtypeStruct((N_DEV,) + a.shape, a.dtype) for a in self.arrays]

    def sem_shapes(self):
        return [pltpu.SemaphoreType.DMA((self.n, N_DEV - 1)), pltpu.SemaphoreType.DMA((self.n, N_DEV - 1)),
                pltpu.SemaphoreType.DMA((self.n,))]

    def _plan(self, ins, outs, sems):
        x, y, c, me = _mesh_pos()
        chips = [(1 - x, y), (x, 1 - y), (1 - x, 1 - y)]

        def copy(a, k, src, block, to):
            px, py, pc = block
            return pltpu.make_async_remote_copy(
                src_ref=src, dst_ref=outs[a].at[4 * px + 2 * py + pc], send_sem=sems[0].at[a, k],
                recv_sem=sems[1].at[a, k], device_id=to, device_id_type=MESH)

        def landed(a, block):
            px, py, pc = block
            return outs[a].at[4 * px + 2 * py + pc]

        local = [pltpu.make_async_copy(ins[a], outs[a].at[me], sems[2].at[a]) for a in range(self.n)]
        first = []
        for a in range(self.n):
            first.append(copy(a, 0, ins[a], (x, y, c), (x, y, 1 - c)))
            first += [copy(a, 1 + j, ins[a], (x, y, c), (*chip, c)) for j, chip in enumerate(chips[:2])]
        return (x, y, c), chips, copy, landed, local, first

    def start(self, ins, outs, sems):
        *_, local, first = self._plan(ins, outs, sems)
        for cp in local + first:
            cp.start()

    def finish(self, ins, outs, sems):
        (x, y, c), chips, copy, landed, local, first = self._plan(ins, outs, sems)
        south = c == 0
        came = (jnp.where(south, 1 - x, x), jnp.where(south, y, 1 - y), c)
        goes = (jnp.where(south, x, 1 - x), jnp.where(south, 1 - y, y), c)
        passed = []
        for a in range(self.n):
            for j, chip in enumerate(chips[:2]):
                copy(a, 1 + j, ins[a], (*chip, c), (x, y, c)).wait_recv()
            passed.append(copy(a, 3, landed(a, came), came, goes))
            passed += [copy(a, 4 + j, landed(a, (*chip, c)), (*chip, c), (x, y, 1 - c))
                       for j, chip in enumerate(chips[:2])]
        for cp in passed:
            cp.start()
        for a in range(self.n):
            diagonal = (*chips[2], c)
            copy(a, 3, ins[a], diagonal, (x, y, c)).wait_recv()
            cp = copy(a, 6, landed(a, diagonal), diagonal, (x, y, 1 - c))
            cp.start()
            passed.append(cp)
        for a in range(self.n):
            copy(a, 0, ins[a], (x, y, 1 - c), (x, y, c)).wait_recv()
            for j, chip in enumerate(chips):
                copy(a, 4 + j, ins[a], (*chip, 1 - c), (x, y, c)).wait_recv()
        for cp in first + passed:
            cp.wait_send()
        for cp in local:
            cp.wait()


CHUNK = 128
RESIDUES = 16
PER_RES = CHUNK // RESIDUES


def _perm_rows(tile, inverse):
    a = lax.broadcasted_iota(jnp.int32, (CHUNK, CHUNK), 0)
    b = lax.broadcasted_iota(jnp.int32, (CHUNK, CHUNK), 1)
    if inverse:
        a, b = b, a
    p = jnp.where(a == PER_RES * (b % RESIDUES) + b // RESIDUES, 1.0, 0.0).astype(BF16)
    parts = [jnp.dot(p, tile[c * CHUNK:(c + 1) * CHUNK], preferred_element_type=F32)
             for c in range(tile.shape[0] // CHUNK)]
    return jnp.concatenate(parts, axis=0).astype(BF16)


class _Rows:
    def __init__(self, dil, S):
        nc = S // CHUNK
        self.dil = dil
        if dil == 1:
            self.view, self.block, self.nb = (nc, CHUNK), (None, CHUNK), nc
            self.index = lambda r, b: (b, 0, 0)
        elif dil == 4:
            self.view, self.block, self.nb = (nc, 4, 4, PER_RES), (4, 4, None, PER_RES), nc // 4
            self.index = lambda r, b: (b, 0, r, 0, 0)
        elif dil == RESIDUES:
            self.view, self.block, self.nb = (nc, RESIDUES, PER_RES), (RESIDUES, None, PER_RES), nc // RESIDUES
            self.index = lambda r, b: (b, r, 0, 0)
        else:
            raise NotImplementedError(dil)

    def of(self, a):
        return a.reshape(self.view + (a.shape[-1],))

    def spec(self, width, which_block):
        return pl.BlockSpec(self.block + (width,), lambda r, n: self.index(r, which_block(n)))

    def pos(self, row):
        if self.dil == 1:
            return (row % PER_RES) * RESIDUES + row // PER_RES
        if self.dil == 4:
            return (row // 32) * 32 + (row % PER_RES) * 4 + (row % 32) // PER_RES
        return row


def _ld(ref, cols=slice(None)):
    v = ref[(slice(None),) * (len(ref.shape) - 1) + (cols,)]
    return v.reshape(BLK, v.shape[-1])


def _st(ref, val, cols=slice(None)):
    ref[(slice(None),) * (len(ref.shape) - 1) + (cols,)] = val.reshape(ref.shape[:-1] + (val.shape[-1],))


def _norm_rows(x, g, hosted=None, tm=512):
    S, D = x.shape
    hn = hosted.n if hosted is not None else 0

    def body(x_ref, g_ref, *rest):
        h_ins = rest[:hn]
        hrm_out, h_out = rest[hn:hn + 2]
        h_outs = rest[hn + 2:2 * hn + 2]
        h_sems = rest[2 * hn + 2:]
        i = pl.program_id(0)
        if hosted is not None:
            pl.when(i == 0)(lambda: hosted.start(h_ins, h_outs, h_sems))
        xf = x_ref[...]
        r = lax.rsqrt(jnp.mean(xf * xf, axis=-1, keepdims=True) + NORM_EPS)
        h = (xf * r * g_ref[...]).astype(BF16)
        h_out[...] = h
        hrm_out[...] = _perm_rows(h, False)
        if hosted is not None:
            pl.when(i == S // tm - 1)(lambda: hosted.finish(h_ins, h_outs, h_sems))

    row = pl.BlockSpec((tm, D), lambda i: (i, 0))
    in_specs, args = [row, pl.BlockSpec((1, D), lambda i: (0, 0))], [x, g]
    out_specs, out_shape, scratch = [row, row], [jax.ShapeDtypeStruct((S, D), BF16)] * 2, []
    if hosted is not None:
        in_specs += [ANY_SPEC] * hn
        args += hosted.arrays
        out_specs += [ANY_SPEC] * hn
        out_shape += hosted.out_shapes()
        scratch += hosted.sem_shapes()
    return pl.pallas_call(
        body, name="norm_rows", grid=(S // tm,),
        in_specs=in_specs, out_specs=tuple(out_specs), out_shape=tuple(out_shape), scratch_shapes=scratch,
        compiler_params=_params(("arbitrary",)),
    )(*args)


def _inproj(h_rm, h, w_t, segments, hosted=None, tm=1024, tn=512):
    S, D = h.shape
    ns = len(segments)
    ni = S // tm
    counts = [seg[0] // tn for seg in segments]
    starts = [sum(counts[:s]) for s in range(ns)]

    hn = hosted.n if hosted is not None else 0
    last_p = sum(counts)

    def body(hrm_ref, h_ref, w_ref, *rest):
        h_ins, rest = rest[:hn], rest[hn:]
        outs = rest[:ns]
        h_outs = rest[ns:ns + hn]
        hrm_scr, h_scr = rest[ns + hn:ns + 2 + hn]
        h_sems = rest[ns + 2 + hn:]
        p, i = pl.program_id(0), pl.program_id(1)

        if hosted is not None:
            @pl.when((p == 0) & (i == 0))
            def _():
                hosted.start(h_ins, h_outs, h_sems)

            @pl.when((p == last_p) & (i == ni - 1))
            def _():
                hosted.finish(h_ins, h_outs, h_sems)

        @pl.when(p == 0)
        def _():
            h_scr[i] = h_ref[...]
            hrm_scr[i] = hrm_ref[...]

        for s, (_, scale, rm) in enumerate(segments):
            @pl.when((p > starts[s]) & (p <= starts[s] + counts[s]))
            def _(s=s, scale=scale, rm=rm):
                acc = _nt((hrm_scr if rm else h_scr)[i], w_ref[...])
                outs[s][...] = acc * scale if scale != 1.0 else acc

    def out_index(s):
        def index(p, i):
            j = p - 1 - starts[s]
            row = jnp.where(j < 0, 0, jnp.where(j >= counts[s], ni - 1, i))
            return row, jnp.clip(j, 0, counts[s] - 1)
        return index

    first_pass = pl.BlockSpec((tm, D), lambda p, i: (jnp.where(p == 0, i, ni - 1), 0))
    out_specs = [pl.BlockSpec((tm, tn), out_index(s)) for s in range(ns)]
    out_shape = [jax.ShapeDtypeStruct((S, seg[0]), F32) for seg in segments]
    in_specs = [first_pass, first_pass, pl.BlockSpec((tn, D), lambda p, i: (jnp.maximum(p - 1, 0), 0))]
    args = [h_rm, h, w_t]
    scratch = [pltpu.VMEM((ni, tm, D), BF16), pltpu.VMEM((ni, tm, D), BF16)]
    if hosted is not None:
        in_specs += [ANY_SPEC] * hn
        args += hosted.arrays
        out_specs += [ANY_SPEC] * hn
        out_shape += hosted.out_shapes()
        scratch += hosted.sem_shapes()
    return pl.pallas_call(
        body, name="inproj", grid=(1 + last_p, ni),
        in_specs=in_specs, out_specs=tuple(out_specs), out_shape=tuple(out_shape), scratch_shapes=scratch,
        compiler_params=_params(("arbitrary", "arbitrary"), BIG_VMEM_LIMIT),
    )(*args)


def _fill_bias_table(tbl, rows, keys_first=False):
    shape = (2 * BLK, BLK) if keys_first else (BLK, 2 * BLK)
    qi = lax.broadcasted_iota(jnp.int32, shape, 1 if keys_first else 0)
    kj = lax.broadcasted_iota(jnp.int32, shape, 0 if keys_first else 1)
    dist = rows.pos(qi) - rows.pos(kj % BLK) + jnp.where(kj < BLK, BLK, 0)
    inside = (dist >= 0) & (dist <= BLK)
    negd = (dist * (-rows.dil)).astype(F32)
    for f, valid in enumerate((inside & (kj >= BLK), inside)):
        for h in range(N_Q_HEADS):
            tbl[f * N_Q_HEADS + h] = jnp.where(valid, SLOPES[h] * negd, NEG)


def _bias2(tbl, n, h0, h1, axis=0):
    base = jnp.where(n == 0, 0, N_Q_HEADS)
    return jnp.concatenate([tbl[base + h0], tbl[base + h1]], axis=axis)


def _head_operands(kv2, hk, lo_mask):
    half, pos = hk // 2, hk % 2
    out = []
    for base in (0, KV_W):
        t = kv2[:, base + half * LANES: base + (half + 1) * LANES]
        sw = pltpu.roll(t, HEAD_DIM, axis=1)
        at_lo, at_hi = (t, sw) if pos == 0 else (sw, t)
        out.append(jnp.where(lo_mask, at_lo, 0.0).astype(BF16))
        out.append(jnp.where(lo_mask, 0.0, at_hi).astype(BF16))
    return out


def _nt(a, b):
    return lax.dot_general(a, b, (((1,), (1,)), ((), ())), preferred_element_type=F32)


def _tn(a, b):
    return lax.dot_general(a, b, (((0,), (0,)), ((), ())), preferred_element_type=F32)


def _attn_fwd(q, kv, dil, name, prev=(), gate=None):
    S = q.shape[0]
    rows = _Rows(dil, S)
    nb = rows.nb
    have_prev, last = len(prev) > 0, gate is not None

    def body(*refs):
        refs = list(refs)
        q_ref, kvc_ref, kvp_ref = refs[:3]
        del refs[:3]
        po_refs, pl_refs = refs[0:2 * len(prev):2], refs[1:2 * len(prev):2]
        del refs[:2 * len(prev)]
        if last:
            gate_ref = refs.pop(0)
        o_ref, lse_ref = refs[:2]
        y_ref = refs[2] if last else None
        tbl = refs[-1]
        n = pl.program_id(1)

        @pl.when((pl.program_id(0) == 0) & (n == 0))
        def _():
            _fill_bias_table(tbl, rows)

        kv2 = jnp.concatenate([_ld(kvp_ref), _ld(kvc_ref)], axis=0)
        lo_mask = lax.broadcasted_iota(jnp.int32, (2 * BLK, LANES), 1) < HEAD_DIM
        lane = lax.broadcasted_iota(jnp.int32, (BLK, LANES), 1)
        stats = jnp.zeros((BLK, LANES), F32)
        for hk in range(N_KV_HEADS):
            k_lo, k_hi, v_lo, v_hi = _head_operands(kv2, hk, lo_mask)
            cols = [slice(b * LANES, (b + 1) * LANES) for b in (2 * hk, 2 * hk + 1)]
            q2 = jnp.concatenate([_ld(q_ref, cols[0]), _ld(q_ref, cols[1])], axis=0).astype(BF16)
            o2 = jnp.zeros((2 * BLK, LANES), F32)
            for which, (kk, vv) in enumerate(((k_lo, v_lo), (k_hi, v_hi))):
                h0, h1 = 4 * hk + which, 4 * hk + 2 + which
                s = _nt(q2, kk) + _bias2(tbl, n, h0, h1)
                m = jnp.max(s, axis=1, keepdims=True)
                p = jnp.exp(s - m)
                l = jnp.sum(p, axis=1, keepdims=True)
                o2 = o2 + jnp.dot(p.astype(BF16), vv, preferred_element_type=F32) * (1.0 / l)
                lse = m + jnp.log(l)
                stats = jnp.where(lane == h0, lse[0:BLK], stats)
                stats = jnp.where(lane == h1, lse[BLK:], stats)
            _st(o_ref, o2[0:BLK], cols[0])
            _st(o_ref, o2[BLK:], cols[1])
        if have_prev:
            others = [_ld(r) for r in pl_refs]
            top = stats
            for b in others:
                top = jnp.maximum(top, b)
            e_new = jnp.exp(stats - top)
            e_old = [jnp.exp(b - top) for b in others]
            total = e_new
            for e in e_old:
                total = total + e
            stats = top + jnp.log(total)
            inv = 1.0 / total
            w_new, w_old = e_new * inv, [e * inv for e in e_old]
        if have_prev or last:
            lo = lane < HEAD_DIM
            for blk in range(ATT_W // LANES):
                cols = slice(blk * LANES, (blk + 1) * LANES)
                o_blk = _ld(o_ref, cols)
                if have_prev:
                    pick = lambda w: jnp.where(lo, w[:, 2 * blk:2 * blk + 1], w[:, 2 * blk + 1:2 * blk + 2])
                    o_blk = o_blk * pick(w_new)
                    for po_ref, w in zip(po_refs, w_old):
                        o_blk = o_blk + _ld(po_ref, cols) * pick(w)
                    _st(o_ref, o_blk, cols)
                if last:
                    a = _ld(gate_ref, cols)
                    _st(y_ref, (o_blk * (a * _sigmoid(a))).astype(BF16), cols)
        _st(lse_ref, stats)

    here = lambda n: n
    before_n = lambda n: jnp.maximum(n - 1, 0)
    in_specs = [rows.spec(ATT_W, here), rows.spec(2 * KV_W, here), rows.spec(2 * KV_W, before_n)]
    args = [rows.of(q), rows.of(kv), rows.of(kv)]
    for o_other, lse_other in prev:
        in_specs += [rows.spec(ATT_W, here), rows.spec(LANES, here)]
        args += [rows.of(o_other), rows.of(lse_other)]
    out_specs = [rows.spec(ATT_W, here), rows.spec(LANES, here)]
    out_shape = [jax.ShapeDtypeStruct(rows.view + (ATT_W,), F32), jax.ShapeDtypeStruct(rows.view + (LANES,), F32)]
    if last:
        in_specs.append(rows.spec(ATT_W, here))
        args.append(rows.of(gate))
        out_specs.append(rows.spec(ATT_W, here))
        out_shape.append(jax.ShapeDtypeStruct(rows.view + (ATT_W,), BF16))
    res = pl.pallas_call(
        body, name=name, grid=(dil, nb),
        in_specs=in_specs, out_specs=tuple(out_specs), out_shape=tuple(out_shape),
        scratch_shapes=[pltpu.VMEM((2 * N_Q_HEADS, BLK, 2 * BLK), F32)],
        compiler_params=_params(("arbitrary", "arbitrary")),
    )(*args)
    return tuple(r.reshape(S, r.shape[-1]) for r in res)


def _shifted_copies(buf, phases):
    n = phases.shape[1]
    for b in range(1, 8):
        phases[b - 1] = buf[b:b + n, :]


def _window(buf, phases, start, cols):
    b = start % 8
    if b == 0:
        return buf[start:start + 8, cols]
    return phases[b - 1, start - b:start - b + 8, cols]


def _broadcast_taps(w_ref, wb):
    for j in range(CONV_K):
        wb[j] = jnp.broadcast_to(w_ref[j:j + 1, :], wb.shape[1:])


def _conv_fwd(gates, conv_w, conv_b, ln_g, ln_b, tt=256):
    S = gates.shape[0]
    C = conv_w.shape[1]
    hb = tt // CONV_HALO

    def body(val_ref, glu_ref, hval_ref, hglu_ref, gate_ref, w_ref, b_ref, g_ref, beta_ref,
             conv_ref, y_ref, hbuf, hph):
        i = pl.program_id(0)
        halo = hval_ref[...] * _sigmoid(hglu_ref[...])
        hbuf[0:CONV_HALO, :] = jnp.where(i > 0, halo, 0.0)
        hbuf[CONV_HALO:, :] = val_ref[...] * _sigmoid(glu_ref[...])
        _shifted_copies(hbuf, hph)
        for cb in range(C // LANES):
            cols = slice(cb * LANES, (cb + 1) * LANES)
            wj = [jnp.broadcast_to(w_ref[j:j + 1, cols], (8, LANES)) for j in range(CONV_K)]
            for rc in range(tt // 8):
                acc = jnp.zeros((8, LANES), F32)
                for j in range(CONV_K):
                    start = rc * 8 + CONV_HALO - (CONV_K - 1) + j
                    acc = acc + _window(hbuf, hph, start, cols) * wj[j]
                conv_ref[rc * 8:(rc + 1) * 8, cols] = acc
        cv = conv_ref[...] + b_ref[...]
        conv_ref[...] = cv
        mu = jnp.mean(cv, axis=-1, keepdims=True)
        xc = cv - mu
        var = jnp.mean(xc * xc, axis=-1, keepdims=True)
        ln = xc * lax.rsqrt(var + LN_EPS) * g_ref[...] + beta_ref[...]
        gt = gate_ref[...]
        y_ref[...] = (ln * _sigmoid(ln) * (gt * _sigmoid(gt))).astype(BF16)

    vec = pl.BlockSpec((1, C), lambda i: (0, 0))
    return pl.pallas_call(
        body, name="conv_fwd", grid=(S // tt,),
        in_specs=[pl.BlockSpec((tt, C), lambda i: (i, 0)),
                  pl.BlockSpec((tt, C), lambda i: (i, 1)),
                  pl.BlockSpec((CONV_HALO, C), lambda i: (jnp.maximum(i * hb - 1, 0), 0)),
                  pl.BlockSpec((CONV_HALO, C), lambda i: (jnp.maximum(i * hb - 1, 0), 1)),
                  pl.BlockSpec((tt, C), lambda i: (i, 2)),
                  pl.BlockSpec((CONV_HALO, C), lambda i: (0, 0)), vec, vec, vec],
        out_specs=(pl.BlockSpec((tt, C), lambda i: (i, 0)), pl.BlockSpec((tt, C), lambda i: (i, 0))),
        out_shape=(jax.ShapeDtypeStruct((S, C), F32), jax.ShapeDtypeStruct((S, C), BF16)),
        scratch_shapes=[pltpu.VMEM((tt + CONV_HALO, C), F32), pltpu.VMEM((7, tt + CONV_HALO - 8, C), F32)],
        compiler_params=_params(("parallel",)),
    )(gates, gates, gates, gates, gates, conv_w, conv_b, ln_g, ln_b)


def _outproj_loss(x, y_att, y_conv, w_out, gf, target, tm=512):
    S, D = x.shape
    E = y_att.shape[1]

    def body(x_ref, ya_ref, yc_ref, w_ref, gf_ref, t_ref, dx_ref, dxb_ref, loss_ref, ggf_ref):
        @pl.when(pl.program_id(0) == 0)
        def _():
            loss_ref[...] = jnp.zeros_like(loss_ref)
            ggf_ref[...] = jnp.zeros_like(ggf_ref)

        x2 = (x_ref[...] + jnp.dot(_perm_rows(ya_ref[...], True), w_ref[0:E, :], preferred_element_type=F32)
              + jnp.dot(yc_ref[...], w_ref[E:, :], preferred_element_type=F32))
        r = lax.rsqrt(jnp.mean(x2 * x2, axis=-1, keepdims=True) + NORM_EPS)
        nrm = x2 * r
        gfv = gf_ref[...]
        err = nrm * gfv - t_ref[...]
        loss_ref[...] += jnp.sum(err * err, axis=0, keepdims=True)
        dout = err * (1.0 / D)
        ggf_ref[...] += jnp.sum(dout * nrm, axis=0, keepdims=True)
        dn = dout * gfv
        dx2 = r * (dn - nrm * jnp.mean(dn * nrm, axis=-1, keepdims=True))
        dx_ref[...] = dx2
        dxb_ref[...] = dx2.astype(BF16)

    row = lambda w: pl.BlockSpec((tm, w), lambda i: (i, 0))
    vec = pl.BlockSpec((1, D), lambda i: (0, 0))
    return pl.pallas_call(
        body, name="outproj_loss", grid=(S // tm,),
        in_specs=[row(D), row(E), row(E), pl.BlockSpec((2 * E, D), lambda i: (0, 0)), vec, row(D)],
        out_specs=(row(D), row(D), vec, vec),
        out_shape=(jax.ShapeDtypeStruct((S, D), F32), jax.ShapeDtypeStruct((S, D), BF16),
                   jax.ShapeDtypeStruct((1, D), F32), jax.ShapeDtypeStruct((1, D), F32)),
        compiler_params=_params(("arbitrary",)),
    )(x, y_att, y_conv, w_out, gf, target)


def _split3(v):
    hi = v.astype(BF16)
    r1 = v - hi.astype(F32)
    mid = r1.astype(BF16)
    lo = (r1 - mid.astype(F32)).astype(BF16)
    return hi, mid, lo


def _dy_att(dxb, w_out, gates, o, tm=512):
    S, D = dxb.shape
    E = ATT_W

    def body(dx_ref, w_ref, a_ref, o_ref, do_ref, da_ref, dl_ref, dxr_ref):
        dxr = _perm_rows(dx_ref[...], False)
        dxr_ref[...] = dxr
        dya = _nt(dxr, w_ref[...])
        a = a_ref[...]
        ov = o_ref[...]
        sl, dsl = _silu_and_grad(a)
        d_o = dya * sl
        do_ref[...] = d_o
        da_ref[...] = (dya * ov * dsl).astype(BF16)
        ci = lax.broadcasted_iota(jnp.int32, (E, LANES), 0) // HEAD_DIM
        hi = lax.broadcasted_iota(jnp.int32, (E, LANES), 1)
        sel = jnp.where(ci == hi, 1.0, 0.0).astype(BF16)
        acc = jnp.zeros((tm, LANES), F32)
        for part in _split3(d_o * ov):
            acc = acc + jnp.dot(part, sel, preferred_element_type=F32)
        dl_ref[...] = acc

    row = lambda w: pl.BlockSpec((tm, w), lambda i: (i, 0))
    return pl.pallas_call(
        body, name="dy_att", grid=(S // tm,),
        in_specs=[row(D), pl.BlockSpec((E, D), lambda i: (0, 0)), row(E), row(E)],
        out_specs=(row(E), row(E), row(LANES), row(D)),
        out_shape=(jax.ShapeDtypeStruct((S, E), F32), jax.ShapeDtypeStruct((S, E), BF16),
                   jax.ShapeDtypeStruct((S, LANES), F32), jax.ShapeDtypeStruct((S, D), BF16)),
        compiler_params=_params(("parallel",)),
    )(dxb, w_out, gates, o)


def _dy_conv(dxb, w_out, gates, conv_out, ln_g, ln_b, tm=512):
    S, D = dxb.shape
    C = conv_out.shape[1]

    def body(dx_ref, w_ref, gate_ref, cv_ref, g_ref, beta_ref, dgate_ref, dconv_ref, gg_ref, gb_ref, gcb_ref):
        @pl.when(pl.program_id(0) == 0)
        def _():
            gg_ref[...] = jnp.zeros_like(gg_ref)
            gb_ref[...] = jnp.zeros_like(gb_ref)
            gcb_ref[...] = jnp.zeros_like(gcb_ref)

        dyc = _nt(dx_ref[...], w_ref[...])
        cv = cv_ref[...]
        mu = jnp.mean(cv, axis=-1, keepdims=True)
        xc = cv - mu
        rstd = lax.rsqrt(jnp.mean(xc * xc, axis=-1, keepdims=True) + LN_EPS)
        nrm = xc * rstd
        gv = g_ref[...]
        ln = nrm * gv + beta_ref[...]
        u, du = _silu_and_grad(ln)
        gt = gate_ref[...]
        g2, dg2 = _silu_and_grad(gt)
        dgate_ref[...] = (dyc * u * dg2).astype(BF16)
        d_ln = dyc * g2 * du
        gb_ref[...] += jnp.sum(d_ln, axis=0, keepdims=True)
        gg_ref[...] += jnp.sum(d_ln * nrm, axis=0, keepdims=True)
        dn = d_ln * gv
        d_conv = rstd * (dn - jnp.mean(dn, axis=-1, keepdims=True)
                         - nrm * jnp.mean(dn * nrm, axis=-1, keepdims=True))
        dconv_ref[...] = d_conv
        gcb_ref[...] += jnp.sum(d_conv, axis=0, keepdims=True)

    row = lambda w: pl.BlockSpec((tm, w), lambda i: (i, 0))
    vec = pl.BlockSpec((1, C), lambda i: (0, 0))
    return pl.pallas_call(
        body, name="dy_conv", grid=(S // tm,),
        in_specs=[row(D), pl.BlockSpec((C, D), lambda i: (1, 0)),
                  pl.BlockSpec((tm, C), lambda i: (i, 2)), row(C), vec, vec],
        out_specs=(row(C), row(C), vec, vec, vec),
        out_shape=(jax.ShapeDtypeStruct((S, C), BF16), jax.ShapeDtypeStruct((S, C), F32),
                   jax.ShapeDtypeStruct((1, C), F32), jax.ShapeDtypeStruct((1, C), F32),
                   jax.ShapeDtypeStruct((1, C), F32)),
        compiler_params=_params(("arbitrary",)),
    )(dxb, w_out, gates, conv_out, ln_g, ln_b)


def _conv_bwd(d_conv, gates, d_c_gate, conv_w, hosted=None, tt=256):
    S, C = d_conv.shape
    hb = tt // CONV_HALO
    nt = S // tt
    hn = hosted.n if hosted is not None else 0

    def body(*refs):
        dc_ref, dnext_ref, val_ref, glu_ref, dg_ref, w_ref = refs[:6]
        h_ins = refs[6:6 + hn]
        out_ref, gw_ref = refs[6 + hn:8 + hn]
        h_outs = refs[8 + hn:8 + 2 * hn]
        hbuf, dbuf, dhbuf, dph, wb = refs[8 + 2 * hn:13 + 2 * hn]
        h_sems = refs[13 + 2 * hn:]
        i = pl.program_id(0)

        @pl.when(i == 0)
        def _():
            gw_ref[...] = jnp.zeros_like(gw_ref)
            _broadcast_taps(w_ref, wb)
            if hosted is not None:
                hosted.start(h_ins, h_outs, h_sems)

        val = val_ref[...]
        sg = _sigmoid(glu_ref[...])
        hbuf[...] = val * sg
        dbuf[0:tt, :] = dc_ref[...]
        dbuf[tt:, :] = jnp.where(i < nt - 1, dnext_ref[...], 0.0)
        _shifted_copies(dbuf, dph)
        for cb in range(C // LANES):
            cols = slice(cb * LANES, (cb + 1) * LANES)
            gacc = [jnp.zeros((8, LANES), F32) for _ in range(CONV_K)]
            group = 2
            for rc0 in range(0, tt // 8, group):
                hcur = [hbuf[(rc0 + r) * 8:(rc0 + r + 1) * 8, cols] for r in range(group)]
                accs = [jnp.zeros((8, LANES), F32) for _ in range(group)]
                for j in range(CONV_K):
                    wj = wb[j, :, cols]
                    for r in range(group):
                        dwin = _window(dbuf, dph, (rc0 + r) * 8 + (CONV_K - 1) - j, cols)
                        accs[r] = accs[r] + dwin * wj
                        gacc[j] = gacc[j] + dwin * hcur[r]
                for r in range(group):
                    dhbuf[(rc0 + r) * 8:(rc0 + r + 1) * 8, cols] = accs[r]
            for j in range(CONV_K):
                gw_ref[j:j + 1, cols] += jnp.sum(gacc[j], axis=0, keepdims=True)
        d_h = dhbuf[...]
        out_ref[:, 0:C] = (d_h * sg).astype(BF16)
        out_ref[:, C:2 * C] = (d_h * val * sg * (1.0 - sg)).astype(BF16)
        out_ref[:, 2 * C:3 * C] = dg_ref[...]

        if hosted is not None:
            @pl.when(i == nt - 1)
            def _():
                hosted.finish(h_ins, h_outs, h_sems)

    tile = lambda col: pl.BlockSpec((tt, C), lambda i: (i, col))
    in_specs = [tile(0),
                pl.BlockSpec((CONV_HALO, C), lambda i: (jnp.minimum((i + 1) * hb, S // CONV_HALO - 1), 0)),
                tile(0), tile(1), tile(0),
                pl.BlockSpec((CONV_HALO, C), lambda i: (0, 0))]
    args = [d_conv, d_conv, gates, gates, d_c_gate, conv_w]
    out_specs = [pl.BlockSpec((tt, 3 * C), lambda i: (i, 0)), pl.BlockSpec((CONV_HALO, C), lambda i: (0, 0))]
    out_shape = [jax.ShapeDtypeStruct((S, 3 * C), BF16), jax.ShapeDtypeStruct((CONV_HALO, C), F32)]
    scratch = [pltpu.VMEM((tt, C), F32), pltpu.VMEM((tt + CONV_HALO, C), F32), pltpu.VMEM((tt, C), F32),
               pltpu.VMEM((7, tt + CONV_HALO - 8, C), F32), pltpu.VMEM((CONV_K, 8, C), F32)]
    if hosted is not None:
        in_specs += [ANY_SPEC] * hn
        args += hosted.arrays
        out_specs += [ANY_SPEC] * hn
        out_shape += hosted.out_shapes()
        scratch += hosted.sem_shapes()
    res = pl.pallas_call(
        body, name="conv_bwd", grid=(nt,),
        in_specs=in_specs, out_specs=tuple(out_specs), out_shape=tuple(out_shape), scratch_shapes=scratch,
        compiler_params=_params(("arbitrary",)),
    )(*args)
    return res[0], res[1], list(res[2:])


def _attn_bwd(q, kv, d_o, lse, delta, dil, prev, final, name, hosted=None):
    S = q.shape[0]
    rows = _Rows(dil, S)
    nb = rows.nb
    steps = dil * nb
    out_dt = BF16 if final else F32
    have_prev = prev is not None
    hn = hosted.n if hosted is not None else 0

    def body(*refs):
        refs = list(refs)
        q_ref, do_ref, lse_ref, dl_ref, kvc_ref, kvp_ref = refs[:6]
        del refs[:6]
        if have_prev:
            pdq_ref, pdkv_ref = refs[:2]
            del refs[:2]
        h_ins = refs[:hn]
        dq_ref, dkv_ref = refs[hn:hn + 2]
        h_outs = refs[hn + 2:2 * hn + 2]
        carry, tbl = refs[2 * hn + 2:2 * hn + 4]
        h_sems = refs[2 * hn + 4:]
        t = pl.program_id(0)
        n = t % nb

        @pl.when(t == 0)
        def _():
            if hosted is not None:
                hosted.start(h_ins, h_outs, h_sems)
            _fill_bias_table(tbl, rows, keys_first=True)
            carry[...] = jnp.zeros_like(carry)

        @pl.when(t < steps)
        def _():
            kv2 = jnp.concatenate([_ld(kvp_ref), _ld(kvc_ref)], axis=0)
            lse_t, dl_t = _ld(lse_ref).T, _ld(dl_ref).T
            lo_mask = lax.broadcasted_iota(jnp.int32, (2 * BLK, LANES), 1) < HEAD_DIM
            halves = [jnp.zeros((2 * BLK, LANES), F32) for _ in range(4)]
            for hk in range(N_KV_HEADS):
                k_lo, k_hi, v_lo, v_hi = _head_operands(kv2, hk, lo_mask)
                cols = [slice(b * LANES, (b + 1) * LANES) for b in (2 * hk, 2 * hk + 1)]
                q2 = jnp.concatenate([_ld(q_ref, cols[0]), _ld(q_ref, cols[1])], axis=0).astype(BF16)
                do2 = jnp.concatenate([_ld(do_ref, cols[0]), _ld(do_ref, cols[1])], axis=0).astype(BF16)
                dq2 = jnp.zeros((2 * BLK, LANES), F32)
                dks, dvs = [], []
                for which, (kk, vv) in enumerate(((k_lo, v_lo), (k_hi, v_hi))):
                    h0, h1 = 4 * hk + which, 4 * hk + 2 + which
                    s = _nt(kk, q2) + _bias2(tbl, n, h0, h1, axis=1)
                    lse2 = jnp.concatenate([lse_t[h0:h0 + 1, :], lse_t[h1:h1 + 1, :]], axis=1)
                    dl2 = jnp.concatenate([dl_t[h0:h0 + 1, :], dl_t[h1:h1 + 1, :]], axis=1)
                    p = jnp.exp(s - lse2)
                    ds = (p * (_nt(vv, do2) - dl2)).astype(BF16)
                    dq2 = dq2 + _tn(ds, kk)
                    dks.append(jnp.dot(ds, q2, preferred_element_type=F32))
                    dvs.append(jnp.dot(p.astype(BF16), do2, preferred_element_type=F32))
                dk_sum = jnp.where(lo_mask, dks[0], dks[1])
                dv_sum = jnp.where(lo_mask, dvs[0], dvs[1])
                for jp in range(2):
                    dq_blk = dq2[jp * BLK:(jp + 1) * BLK]
                    if have_prev:
                        dq_blk = dq_blk + _ld(pdq_ref, cols[jp])
                    if final:
                        dq_blk = dq_blk * (HEAD_DIM ** -0.5)
                    _st(dq_ref, dq_blk.astype(out_dt), cols[jp])
                half, pos = hk // 2, hk % 2
                here = lo_mask if pos == 0 else jnp.logical_not(lo_mask)
                dk_tot = dk_sum + pltpu.roll(dk_sum, HEAD_DIM, axis=1)
                dv_tot = dv_sum + pltpu.roll(dv_sum, HEAD_DIM, axis=1)
                halves[half] = halves[half] + jnp.where(here, dk_tot, 0.0)
                halves[2 + half] = halves[2 + half] + jnp.where(here, dv_tot, 0.0)
            for b in range(4):
                cols = slice(b * LANES, (b + 1) * LANES)
                done = carry[:, cols] + halves[b][0:BLK, :]
                if have_prev:
                    done = done + _ld(pdkv_ref, cols)
                _st(dkv_ref, done.astype(out_dt), cols)
                carry[:, cols] = halves[b][BLK:, :]

        @pl.when(t == steps)
        def _():
            done = carry[...]
            if have_prev:
                done = done + _ld(pdkv_ref)
            _st(dkv_ref, done.astype(out_dt))
            if hosted is not None:
                hosted.finish(h_ins, h_outs, h_sems)

    def spec(width, lag):
        def index(t):
            u = jnp.clip(t - lag, 0, steps - 1)
            return rows.index(u // nb, u % nb)
        return pl.BlockSpec(rows.block + (width,), index)

    def key_prev(t):
        u = jnp.minimum(t, steps - 1)
        return rows.index(u // nb, jnp.maximum(u % nb - 1, 0))

    in_specs = [spec(ATT_W, 0), spec(ATT_W, 0), spec(LANES, 0), spec(LANES, 0), spec(2 * KV_W, 0),
                pl.BlockSpec(rows.block + (2 * KV_W,), key_prev)]
    args = [rows.of(q), rows.of(d_o), rows.of(lse), rows.of(delta), rows.of(kv), rows.of(kv)]
    if have_prev:
        in_specs += [spec(ATT_W, 0), spec(2 * KV_W, 1)]
        args += [rows.of(prev[0]), rows.of(prev[1])]
    out_specs = [spec(ATT_W, 0), spec(2 * KV_W, 1)]
    out_shape = [jax.ShapeDtypeStruct(rows.view + (ATT_W,), out_dt),
                 jax.ShapeDtypeStruct(rows.view + (2 * KV_W,), out_dt)]
    scratch = [pltpu.VMEM((BLK, 2 * KV_W), F32), pltpu.VMEM((2 * N_Q_HEADS, 2 * BLK, BLK), F32)]
    if hosted is not None:
        in_specs += [ANY_SPEC] * hn
        args += hosted.arrays
        out_specs += [ANY_SPEC] * hn
        out_shape += hosted.out_shapes()
        scratch += hosted.sem_shapes()
    res = pl.pallas_call(
        body, name=name, grid=(steps + 1,),
        in_specs=in_specs, out_specs=tuple(out_specs), out_shape=tuple(out_shape), scratch_shapes=scratch,
        compiler_params=_params(("arbitrary",)),
    )(*args)
    return (res[0].reshape(S, ATT_W), res[1].reshape(S, 2 * KV_W)), list(res[2:])


def _dh(segments, w_in, x, dx2, g, hosted=None, tm=1024, tk=512):
    S, D = x.shape
    ns = len(segments)
    counts = [a.shape[1] // tk for a, _ in segments]
    starts = [sum(counts[:s]) for s in range(ns)]
    nk = sum(counts)
    hn = hosted.n if hosted is not None else 0

    def body(*refs):
        seg_refs = refs[:ns]
        w_ref, x_ref, dx2_ref, g_ref = refs[ns:ns + 4]
        h_ins = refs[ns + 4:ns + 4 + hn]
        gx_ref, gng_ref = refs[ns + 4 + hn:ns + 6 + hn]
        h_outs = refs[ns + 6 + hn:ns + 6 + 2 * hn]
        acc = refs[ns + 6 + 2 * hn]
        h_sems = refs[ns + 7 + 2 * hn:]
        k, i = pl.program_id(0), pl.program_id(1)

        @pl.when((i == 0) & (k == 0))
        def _():
            gng_ref[...] = jnp.zeros_like(gng_ref)
            if hosted is not None:
                hosted.start(h_ins, h_outs, h_sems)

        @pl.when(k == 0)
        def _():
            acc[i] = jnp.zeros(acc.shape[1:], F32)

        for s in range(ns):
            @pl.when((k >= starts[s]) & (k < starts[s] + counts[s]))
            def _(s=s):
                t = seg_refs[s][...]
                if segments[s][1]:
                    t = _perm_rows(t, True)
                acc[i] += jnp.dot(t, w_ref[...], preferred_element_type=F32)

        @pl.when(k == nk - 1)
        def _():
            dh = acc[i]
            xf = x_ref[...]
            r = lax.rsqrt(jnp.mean(xf * xf, axis=-1, keepdims=True) + NORM_EPS)
            nrm = xf * r
            gng_ref[...] += jnp.sum(dh * nrm, axis=0, keepdims=True)
            dn = dh * g_ref[...]
            gx_ref[...] = dx2_ref[...] + r * (dn - nrm * jnp.mean(dn * nrm, axis=-1, keepdims=True))

        if hosted is not None:
            @pl.when((i == S // tm - 1) & (k == nk - 1))
            def _():
                hosted.finish(h_ins, h_outs, h_sems)

    ni = S // tm
    row = pl.BlockSpec((tm, D), lambda k, i: (jnp.where(k == nk - 1, i, 0), 0))
    vec = pl.BlockSpec((1, D), lambda k, i: (0, 0))

    def seg_index(s):
        def index(k, i):
            j = k - starts[s]
            return jnp.where(j < 0, 0, jnp.where(j >= counts[s], ni - 1, i)), jnp.clip(j, 0, counts[s] - 1)
        return index

    in_specs = [pl.BlockSpec((tm, tk), seg_index(s)) for s in range(ns)]
    in_specs += [pl.BlockSpec((tk, D), lambda k, i: (k, 0)), row, row, vec]
    args = [a for a, _ in segments] + [w_in, x, dx2, g]
    out_specs = [row, vec]
    out_shape = [jax.ShapeDtypeStruct((S, D), F32), jax.ShapeDtypeStruct((1, D), F32)]
    scratch = [pltpu.VMEM((ni, tm, D), F32)]
    if hosted is not None:
        in_specs += [ANY_SPEC] * hn
        args += hosted.arrays
        out_specs += [ANY_SPEC] * hn
        out_shape += hosted.out_shapes()
        scratch += hosted.sem_shapes()
    res = pl.pallas_call(
        body, name="dh", grid=(nk, S // tm),
        in_specs=in_specs, out_specs=tuple(out_specs), out_shape=tuple(out_shape), scratch_shapes=scratch,
        compiler_params=_params(("arbitrary", "arbitrary"), BIG_VMEM_LIMIT),
    )(*args)
    return res[0], res[1], list(res[2:])


def _tn_matmul(a, bs, name, b_first=False, tm=512):
    M, K = a.shape
    nb = len(bs)
    shapes = [(b.shape[1], K) if b_first else (K, b.shape[1]) for b in bs]

    def body(a_ref, *refs):
        @pl.when(pl.program_id(0) == 0)
        def _():
            for o_ref in refs[nb:]:
                o_ref[...] = jnp.zeros_like(o_ref)

        at = a_ref[...]
        for b_ref, o_ref in zip(refs[:nb], refs[nb:]):
            for c in range(0, b_ref.shape[1], 512):
                if b_first:
                    o_ref[c:c + 512, :] += _tn(b_ref[:, c:c + 512], at)
                else:
                    o_ref[:, c:c + 512] += _tn(at, b_ref[:, c:c + 512])

    return pl.pallas_call(
        body, name=name, grid=(M // tm,),
        in_specs=[pl.BlockSpec((tm, K), lambda m: (m, 0))] + [pl.BlockSpec((tm, b.shape[1]), lambda m: (m, 0))
                                                              for b in bs],
        out_specs=tuple(pl.BlockSpec(s, lambda m: (0, 0)) for s in shapes),
        out_shape=tuple(jax.ShapeDtypeStruct(s, F32) for s in shapes),
        compiler_params=_params(("arbitrary",)),
    )(a, *bs)


def _adamw(parts, w, m, v, name, tr=None, split=None, by_chip=False):
    R, C = w.shape
    tr = R if tr is None else tr
    parts = [parts] if split is None else list(parts)
    npar = len(parts)

    def total(p_ref):
        if by_chip:
            c = lax.axis_index("c")
            g = p_ref[c].astype(F32)
            for chip in range(1, N_DEV // 2):
                g = g + p_ref[2 * chip + c].astype(F32)
            return g
        g = p_ref[0].astype(F32)
        for dev in range(1, N_DEV):
            g = g + p_ref[dev].astype(F32)
        return g

    def body(*refs):
        w_ref, m_ref, v_ref, g_out, d_out, m_out, v_out = refs[npar:]
        if split is None:
            g = total(refs[0])
        else:
            g = jnp.where(_mesh_pos()[3] < split, total(refs[0]), total(refs[1]))
        mn = ADAM_B1 * m_ref[...] + (1.0 - ADAM_B1) * g
        vn = ADAM_B2 * v_ref[...] + (1.0 - ADAM_B2) * (g * g)
        m_hat = mn / (1.0 - ADAM_B1 ** ADAM_STEP)
        v_hat = vn / (1.0 - ADAM_B2 ** ADAM_STEP)
        g_out[...] = g
        d_out[...] = -ADAM_LR * (m_hat / (jnp.sqrt(v_hat) + ADAM_EPS) + ADAM_WD * w_ref[...])
        m_out[...] = mn
        v_out[...] = vn

    blk = pl.BlockSpec((tr, C), lambda i: (i, 0))
    shp = jax.ShapeDtypeStruct((R, C), F32)
    return pl.pallas_call(
        body, name=name, grid=(R // tr,),
        in_specs=[pl.BlockSpec((N_DEV, tr, C), lambda i: (0, i, 0))] * npar + [blk, blk, blk],
        out_specs=(blk, blk, blk, blk), out_shape=(shp, shp, shp, shp),
        compiler_params=_params(("parallel",)),
    )(*parts, w, m, v)


def _local_step(x, target, norm_g, w_in, conv_w, conv_b, ln_g, ln_b, w_out, gf, exchanges=None, first_weights=None,
                late_weights=None):
    ex_out, ex_att, ex_conv = exchanges if exchanges is not None else (None, None, None)
    h_rm, h, *first = _norm_rows(x, norm_g, first_weights[0] if first_weights is not None else None)
    if first_weights is not None:
        w_in = first_weights[1](first)
    conv_cols = w_in.shape[0] - 2 * ATT_W - 2 * KV_W
    q, kv, a_gate, gates, *gathered = _inproj(
        h_rm, h, w_in,
        [(ATT_W, HEAD_DIM ** -0.5, True), (2 * KV_W, 1.0, True), (ATT_W, 1.0, True), (conv_cols, 1.0, False)],
        late_weights[0] if late_weights is not None else None)
    if late_weights is not None:
        conv_w, w_out = late_weights[1](gathered)

    alone = [_attn_fwd(q, kv, dil, "attn_fwd_d%d" % dil) for _, dil in PATTERNS[1:]]
    o, lse, y_att = _attn_fwd(q, kv, PATTERNS[0][1], "attn_fwd_d%d" % PATTERNS[0][1], alone, a_gate)
    conv_out, y_conv = _conv_fwd(gates, conv_w, conv_b, ln_g, ln_b)
    dx2, dxb, loss_cols, g_gf = _outproj_loss(x, y_att, y_conv, w_out, gf, target)

    d_o, d_a_gate, delta, dxb_rm = _dy_att(dxb, w_out, a_gate, o)
    g_w_out = jnp.concatenate([_tn_matmul(y_att, [dxb_rm], "gw_out_att")[0],
                               _tn_matmul(y_conv, [dxb], "gw_out_conv")[0]], axis=0)
    acc, out_parts = None, []
    for idx, (_, dil) in enumerate(reversed(PATTERNS)):
        hosted = ex_out(g_w_out) if (idx == 0 and ex_out is not None) else None
        acc, outs = _attn_bwd(q, kv, d_o, lse, delta, dil, acc, idx == len(PATTERNS) - 1, "attn_bwd_d%d" % dil,
                              hosted)
        out_parts += outs
    dq, dkv = acc
    g_q, g_kv, g_a = _tn_matmul(h_rm, [dq, dkv, d_a_gate], "gw_in_att", b_first=True)

    d_c_gate, d_conv, g_ln_g, g_ln_b, g_conv_b = _dy_conv(dxb, w_out, gates, conv_out, ln_g, ln_b)
    dgates, g_conv_w, att_parts = _conv_bwd(d_conv, gates, d_c_gate, conv_w,
                                            ex_att(g_q, g_kv, g_a) if ex_att is not None else None)
    g_c, = _tn_matmul(h, [dgates], "gw_in_conv", b_first=True)
    grad_x, g_norm_g, conv_parts = _dh(
        [(dq, True), (dkv, True), (d_a_gate, True), (dgates, False)], w_in, x, dx2, norm_g,
        ex_conv(g_a, g_c, g_conv_w) if ex_conv is not None else None)
    small = (g_norm_g, g_conv_b, g_ln_g, g_ln_b, g_gf, loss_cols)
    return grad_x, (g_q, g_kv, g_a, g_c), g_w_out, g_conv_w, small, (out_parts, att_parts, conv_parts)


def kernel(x, norm_g, w_in, conv_w, conv_b, conv_ln_g, conv_ln_b, w_out, final_norm_g, loss_target, m_norm_g, m_w_in, m_conv_w, m_conv_b, m_conv_ln_g, m_conv_ln_b, m_w_out, m_final_norm_g, v_norm_g, v_w_in, v_conv_w, v_conv_b, v_conv_ln_g, v_conv_ln_b, v_w_out, v_final_norm_g):
    S, D = x.shape[1], x.shape[2]
    win_sh, wout_sh, cw_sh = w_in[0].T, w_out[0], conv_w[0]
    cols_sh, rows_sh, ch_sh = win_sh.shape[0], wout_sh.shape[0], cw_sh.shape[1]

    def first_weights(gathered):
        return gathered[0].reshape(N_DEV * cols_sh, D)

    def late_weights(gathered):
        wout_all, cw_all = gathered
        conv_w_full = cw_all.transpose(1, 0, 2).reshape(CONV_K, N_DEV * ch_sh)
        return jnp.pad(conv_w_full, ((0, CONV_HALO - CONV_K), (0, 0))), wout_all.reshape(N_DEV * rows_sh, D)

    gf = final_norm_g.reshape(1, D)

    first = -(-(ATT_W + 2 * KV_W) // cols_sh)
    a_off = first * cols_sh - (ATT_W + 2 * KV_W)
    assert 0 <= a_off <= ATT_W

    def pieces(parts, n):
        return jnp.concatenate([p.astype(BF16) for p in parts], axis=0).reshape(n, cols_sh, D)

    def ex_out(g_w_out):
        return _Exchange([g_w_out.reshape(N_DEV, rows_sh, D).astype(BF16)], [(0, N_DEV)])

    same_core = (2, 4, 6)

    def ex_att(g_q, g_kv, g_a):
        mine = _chip_sum(pieces([g_q, g_kv, g_a[:a_off]], first), 0, "rs_att")
        return _Exchange([mine], [(0, first)], [same_core])

    def ex_conv(g_a, g_c, g_conv_w):
        mine = _chip_sum(pieces([g_a[a_off:], g_c], N_DEV - first), first, "rs_conv")
        return _Exchange(
            [mine, g_conv_w[:CONV_K].reshape(CONV_K, N_DEV, ch_sh).transpose(1, 0, 2)],
            [(first, N_DEV), (0, N_DEV)], [same_core, None])

    grad_x, _, _, _, small, parts = _local_step(
        x[0], loss_target[0], norm_g, None, None, conv_b, conv_ln_g, conv_ln_b, None, gf,
        (ex_out, ex_att, ex_conv), (_Gather([win_sh.astype(BF16)]), first_weights),
        (_Gather([wout_sh.astype(BF16), cw_sh]), late_weights))
    (wout_parts,), (win_parts_lo,), (win_parts_hi, cw_parts) = parts

    small_pack = jnp.concatenate(list(small) + [jnp.zeros((2, D), F32)], axis=0)
    small_parts, = _exchange(_Exchange([small_pack], [None]), "gather_small")

    upd_win = _adamw((win_parts_lo, win_parts_hi), win_sh, m_w_in[0].T, v_w_in[0].T, "adamw_w_in",
                     tr=cols_sh // 2, split=first, by_chip=True)
    upd_wout = _adamw(wout_parts, wout_sh, m_w_out[0], v_w_out[0], "adamw_w_out", tr=128)
    upd_cw = _adamw(cw_parts, cw_sh, m_conv_w[0], v_conv_w[0], "adamw_conv_w")
    zeros3 = jnp.zeros((3, D), F32)
    stack = lambda a, b, c, d_, e: jnp.concatenate([a, b, c, d_, e.reshape(1, D), zeros3], axis=0)
    upd_small = _adamw(
        small_parts,
        stack(norm_g, conv_b, conv_ln_g, conv_ln_b, final_norm_g),
        stack(m_norm_g, m_conv_b, m_conv_ln_g, m_conv_ln_b, m_final_norm_g),
        stack(v_norm_g, v_conv_b, v_conv_ln_g, v_conv_ln_b, v_final_norm_g) + jnp.concatenate(
            [jnp.zeros((5, D), F32), jnp.ones((3, D), F32)], axis=0),
        "adamw_small")

    loss = 0.5 / D * jnp.sum(upd_small[0][5])

    def outputs(kind):
        sm = upd_small[kind]
        return [sm[0:1], upd_win[kind].T[None], upd_cw[kind][None], sm[1:2], sm[2:3], sm[3:4],
                upd_wout[kind][None], sm[4]]

    return (loss, grad_x[None], *outputs(0), *outputs(1), *outputs(2), *outputs(3))
```

```python
import jax
import jax.numpy as jnp
from jax import lax
from jax.experimental import pallas as pl
from jax.experimental.pallas import tpu as pltpu

F32 = jnp.float32
BF16 = jnp.bfloat16

HEAD_DIM = 64
N_KV_HEADS = 4
N_Q_HEADS = 16
ATT_W = 1024
KV_W = 256
CONV_K = 31
CONV_HALO = 32
PATTERNS = ((128, 1), (512, 4), (2048, 16))
BLK = 128
LANES = 128
NORM_EPS = 1e-6
LN_EPS = 1e-5
NEG = -1e30
N_DEV = 8
ADAM_LR, ADAM_B1, ADAM_B2, ADAM_EPS, ADAM_WD, ADAM_STEP = 0.001, 0.9, 0.999, 1e-08, 0.01, 10
VMEM_LIMIT = 48 * 1024 * 1024
BIG_VMEM_LIMIT = 58 * 1024 * 1024
SLOPES = tuple(2.0 ** (-8.0 * (h + 1) / N_Q_HEADS) for h in range(N_Q_HEADS))
MESH = pl.DeviceIdType.MESH


def _params(sem, vmem_limit=VMEM_LIMIT):
    return pltpu.CompilerParams(dimension_semantics=sem, vmem_limit_bytes=vmem_limit)


def _sigmoid(v):
    return 1.0 / (1.0 + jnp.exp(-v))


def _silu_and_grad(v):
    s = _sigmoid(v)
    return v * s, s * (1.0 + v * (1.0 - s))


ANY_SPEC = pl.BlockSpec(memory_space=pl.ANY)


def _mesh_pos():
    x, y, c = lax.axis_index("x"), lax.axis_index("y"), lax.axis_index("c")
    return x, y, c, 4 * x + 2 * y + c


def _flipped(k, x, y, c):
    px = 1 - x if k & 4 else x
    py = 1 - y if k & 2 else y
    pc = 1 - c if k & 1 else c
    return (px, py, pc), 4 * px + 2 * py + pc


class _Exchange:
    def __init__(self, arrays, dests, flips=None):
        self.arrays, self.dests, self.n = list(arrays), list(dests), len(arrays)
        self.flips = [tuple(range(1, N_DEV)) if f is None else tuple(f)
                      for f in (flips if flips is not None else [None] * self.n)]

    def out_shapes(self):
        return [jax.ShapeDtypeStruct((N_DEV,) + a.shape[-2:], a.dtype) for a in self.arrays]

    def sem_shapes(self):
        return [pltpu.SemaphoreType.DMA((self.n, N_DEV - 1)), pltpu.SemaphoreType.DMA((self.n, N_DEV - 1)),
                pltpu.SemaphoreType.DMA((self.n,))]

    def _when(self, a, dev, fn):
        if self.dests[a] is None:
            fn()
        else:
            lo, hi = self.dests[a]
            pl.when((dev >= lo) & (dev < hi))(fn)

    def _mine(self, ins, a, dev):
        return ins[a] if self.dests[a] is None else ins[a].at[dev - self.dests[a][0]]

    def _copy(self, ins, outs, sems, a, k, src_dev, slot, target):
        return pltpu.make_async_remote_copy(
            src_ref=self._mine(ins, a, src_dev), dst_ref=outs[a].at[slot],
            send_sem=sems[0].at[a, k - 1], recv_sem=sems[1].at[a, k - 1],
            device_id=target, device_id_type=MESH)

    def start(self, ins, outs, sems):
        x, y, c, me = _mesh_pos()
        for a in range(self.n):
            self._when(a, me, lambda a=a: pltpu.make_async_copy(
                self._mine(ins, a, me), outs[a].at[me], sems[2].at[a]).start())
            for k in self.flips[a]:
                target, peer = _flipped(k, x, y, c)
                self._when(a, peer, lambda a=a, k=k, target=target, peer=peer: self._copy(
                    ins, outs, sems, a, k, peer, me, target).start())

    def finish(self, ins, outs, sems):
        x, y, c, me = _mesh_pos()
        lo0 = [0 if d is None else d[0] for d in self.dests]
        for a in range(self.n):
            for k in self.flips[a]:
                target, peer = _flipped(k, x, y, c)
                self._when(a, me, lambda a=a, k=k, peer=peer: self._copy(
                    ins, outs, sems, a, k, lo0[a], peer, (x, y, c)).wait_recv())
            for k in self.flips[a]:
                target, peer = _flipped(k, x, y, c)
                self._when(a, peer, lambda a=a, k=k, target=target, peer=peer: self._copy(
                    ins, outs, sems, a, k, peer, me, target).wait_send())
            self._when(a, me, lambda a=a: pltpu.make_async_copy(
                self._mine(ins, a, me), outs[a].at[me], sems[2].at[a]).wait())


def _exchange(ex, name):
    na = ex.n

    def body(*refs):
        ins, outs, sems = refs[:na], refs[na:2 * na], refs[2 * na:]
        ex.start(ins, outs, sems)
        ex.finish(ins, outs, sems)

    return pl.pallas_call(
        body, name=name, out_shape=tuple(ex.out_shapes()),
        in_specs=[ANY_SPEC] * na, out_specs=tuple([ANY_SPEC] * na), scratch_shapes=ex.sem_shapes(),
    )(*ex.arrays)


class _Swap:
    def __init__(self, pieces, lo):
        self.arrays, self.n, self.lo = [pieces], 1, lo

    def out_shapes(self):
        return [jax.ShapeDtypeStruct(self.arrays[0].shape, self.arrays[0].dtype)]

    def sem_shapes(self):
        slots = self.arrays[0].shape[0]
        return [pltpu.SemaphoreType.DMA((slots,)), pltpu.SemaphoreType.DMA((slots,))]

    def _copies(self, ins, outs, sems):
        x, y, c, me = _mesh_pos()
        for i in range(self.arrays[0].shape[0]):
            mine = c == (self.lo + i) % 2
            yield mine, pltpu.make_async_remote_copy(
                src_ref=ins[0].at[i], dst_ref=outs[0].at[i], send_sem=sems[0].at[i], recv_sem=sems[1].at[i],
                device_id=(x, y, 1 - c), device_id_type=MESH)

    def start(self, ins, outs, sems):
        for mine, cp in self._copies(ins, outs, sems):
            pl.when(jnp.logical_not(mine))(cp.start)

    def finish(self, ins, outs, sems):
        for mine, cp in self._copies(ins, outs, sems):
            pl.when(mine)(cp.wait_recv)
            pl.when(jnp.logical_not(mine))(cp.wait_send)


def _pair_add(pieces, other, name):
    n, R, C = pieces.shape

    def add(p_ref, t_ref, o_ref):
        o_ref[...] = (p_ref[...].astype(F32) + t_ref[...].astype(F32)).astype(o_ref.dtype)

    tr = R // 2
    blk = pl.BlockSpec((None, tr, C), lambda i, r: (i, r, 0))
    return pl.pallas_call(
        add, name=name, grid=(n, R // tr), in_specs=[blk, blk], out_specs=blk,
        out_shape=jax.ShapeDtypeStruct(pieces.shape, pieces.dtype),
        compiler_params=_params(("parallel", "parallel")),
    )(pieces, other)


class _Gather:
    def __init__(self, arrays):
        self.arrays, self.n = list(arrays), len(arrays)

    def out_shapes(self):
        return [jax.ShapeDtypeStruct((N_DEV,) + a.shape, a.dtype) for a in self.arrays]

    def sem_shapes(self):
        return [pltpu.SemaphoreType.DMA((self.n, N_DEV - 1)), pltpu.SemaphoreType.DMA((self.n, N_DEV - 1)),
                pltpu.SemaphoreType.DMA((self.n,))]

    def _plan(self, ins, outs, sems):
        x, y, c, me = _mesh_pos()
        chips = [(1 - x, y), (x, 1 - y), (1 - x, 1 - y)]

        def copy(a, k, src, block, to):
            px, py, pc = block
            return pltpu.make_async_remote_copy(
                src_ref=src, dst_ref=outs[a].at[4 * px + 2 * py + pc], send_sem=sems[0].at[a, k],
                recv_sem=sems[1].at[a, k], device_id=to, device_id_type=MESH)

        def landed(a, block):
            px, py, pc = block
            return outs[a].at[4 * px + 2 * py + pc]

        local = [pltpu.make_async_copy(ins[a], outs[a].at[me], sems[2].at[a]) for a in range(self.n)]
        first = []
        for a in range(self.n):
            first.append(copy(a, 0, ins[a], (x, y, c), (x, y, 1 - c)))
            first += [copy(a, 1 + j, ins[a], (x, y, c), (*chip, c)) for j, chip in enumerate(chips[:2])]
        return (x, y, c), chips, copy, landed, local, first

    def start(self, ins, outs, sems):
        *_, local, first = self._plan(ins, outs, sems)
        for cp in local + first:
            cp.start()

    def finish(self, ins, outs, sems):
        (x, y, c), chips, copy, landed, local, first = self._plan(ins, outs, sems)
        south = c == 0
        came = (jnp.where(south, 1 - x, x), jnp.where(south, y, 1 - y), c)
        goes = (jnp.where(south, x, 1 - x), jnp.where(south, 1 - y, y), c)
        passed = []
        for a in range(self.n):
            for j, chip in enumerate(chips[:2]):
                copy(a, 1 + j, ins[a], (*chip, c), (x, y, c)).wait_recv()
            passed.append(copy(a, 3, landed(a, came), came, goes))
            passed += [copy(a, 4 + j, landed(a, (*chip, c)), (*chip, c), (x, y, 1 - c))
                       for j, chip in enumerate(chips[:2])]
        for cp in passed:
            cp.start()
        for a in range(self.n):
            diagonal = (*chips[2], c)
            copy(a, 3, ins[a], diagonal, (x, y, c)).wait_recv()
            cp = copy(a, 6, landed(a, diagonal), diagonal, (x, y, 1 - c))
            cp.start()
            passed.append(cp)
        for a in range(self.n):
            copy(a, 0, ins[a], (x, y, 1 - c), (x, y, c)).wait_recv()
            for j, chip in enumerate(chips):
                copy(a, 4 + j, ins[a], (*chip, 1 - c), (x, y, c)).wait_recv()
        for cp in first + passed:
            cp.wait_send()
        for cp in local:
            cp.wait()


CHUNK = 128
RESIDUES = 16
PER_RES = CHUNK // RESIDUES


def _perm_rows(tile, inverse):
    a = lax.broadcasted_iota(jnp.int32, (CHUNK, CHUNK), 0)
    b = lax.broadcasted_iota(jnp.int32, (CHUNK, CHUNK), 1)
    if inverse:
        a, b = b, a
    p = jnp.where(a == PER_RES * (b % RESIDUES) + b // RESIDUES, 1.0, 0.0).astype(BF16)
    parts = [jnp.dot(p, tile[c * CHUNK:(c + 1) * CHUNK], preferred_element_type=F32)
             for c in range(tile.shape[0] // CHUNK)]
    return jnp.concatenate(parts, axis=0).astype(BF16)


class _Rows:
    def __init__(self, dil, S):
        nc = S // CHUNK
        self.dil = dil
        if dil == 1:
            self.view, self.block, self.nb = (nc, CHUNK), (None, CHUNK), nc
            self.index = lambda r, b: (b, 0, 0)
        elif dil == 4:
            self.view, self.block, self.nb = (nc, 4, 4, PER_RES), (4, 4, None, PER_RES), nc // 4
            self.index = lambda r, b: (b, 0, r, 0, 0)
        elif dil == RESIDUES:
            self.view, self.block, self.nb = (nc, RESIDUES, PER_RES), (RESIDUES, None, PER_RES), nc // RESIDUES
            self.index = lambda r, b: (b, r, 0, 0)
        else:
            raise NotImplementedError(dil)

    def of(self, a):
        return a.reshape(self.view + (a.shape[-1],))

    def spec(self, width, which_block):
        return pl.BlockSpec(self.block + (width,), lambda r, n: self.index(r, which_block(n)))

    def pos(self, row):
        if self.dil == 1:
            return (row % PER_RES) * RESIDUES + row // PER_RES
        if self.dil == 4:
            return (row // 32) * 32 + (row % PER_RES) * 4 + (row % 32) // PER_RES
        return row


def _ld(ref, cols=slice(None)):
    v = ref[(slice(None),) * (len(ref.shape) - 1) + (cols,)]
    return v.reshape(BLK, v.shape[-1])


def _st(ref, val, cols=slice(None)):
    ref[(slice(None),) * (len(ref.shape) - 1) + (cols,)] = val.reshape(ref.shape[:-1] + (val.shape[-1],))


def _norm_rows(x, g, hosted=None, tm=512):
    S, D = x.shape
    hn = hosted.n if hosted is not None else 0

    def body(x_ref, g_ref, *rest):
        h_ins = rest[:hn]
        hrm_out, h_out = rest[hn:hn + 2]
        h_outs = rest[hn + 2:2 * hn + 2]
        h_sems = rest[2 * hn + 2:]
        i = pl.program_id(0)
        if hosted is not None:
            pl.when(i == 0)(lambda: hosted.start(h_ins, h_outs, h_sems))
        xf = x_ref[...]
        r = lax.rsqrt(jnp.mean(xf * xf, axis=-1, keepdims=True) + NORM_EPS)
        h = (xf * r * g_ref[...]).astype(BF16)
        h_out[...] = h
        hrm_out[...] = _perm_rows(h, False)
        if hosted is not None:
            pl.when(i == S // tm - 1)(lambda: hosted.finish(h_ins, h_outs, h_sems))

    row = pl.BlockSpec((tm, D), lambda i: (i, 0))
    in_specs, args = [row, pl.BlockSpec((1, D), lambda i: (0, 0))], [x, g]
    out_specs, out_shape, scratch = [row, row], [jax.ShapeDtypeStruct((S, D), BF16)] * 2, []
    if hosted is not None:
        in_specs += [ANY_SPEC] * hn
        args += hosted.arrays
        out_specs += [ANY_SPEC] * hn
        out_shape += hosted.out_shapes()
        scratch += hosted.sem_shapes()
    return pl.pallas_call(
        body, name="norm_rows", grid=(S // tm,),
        in_specs=in_specs, out_specs=tuple(out_specs), out_shape=tuple(out_shape), scratch_shapes=scratch,
        compiler_params=_params(("arbitrary",)),
    )(*args)


def _inproj(h_rm, h, w_t, segments, hosted=None, tm=1024, tn=512):
    S, D = h.shape
    ns = len(segments)
    ni = S // tm
    counts = [seg[0] // tn for seg in segments]
    starts = [sum(counts[:s]) for s in range(ns)]

    hn = hosted.n if hosted is not None else 0
    last_p = sum(counts)

    def body(hrm_ref, h_ref, w_ref, *rest):
        h_ins, rest = rest[:hn], rest[hn:]
        outs = rest[:ns]
        h_outs = rest[ns:ns + hn]
        hrm_scr, h_scr = rest[ns + hn:ns + 2 + hn]
        h_sems = rest[ns + 2 + hn:]
        p, i = pl.program_id(0), pl.program_id(1)

        if hosted is not None:
            @pl.when((p == 0) & (i == 0))
            def _():
                hosted.start(h_ins, h_outs, h_sems)

            @pl.when((p == last_p) & (i == ni - 1))
            def _():
                hosted.finish(h_ins, h_outs, h_sems)

        @pl.when(p == 0)
        def _():
            h_scr[i] = h_ref[...]
            hrm_scr[i] = hrm_ref[...]

        for s, (_, scale, rm) in enumerate(segments):
            @pl.when((p > starts[s]) & (p <= starts[s] + counts[s]))
            def _(s=s, scale=scale, rm=rm):
                acc = _nt((hrm_scr if rm else h_scr)[i], w_ref[...])
                outs[s][...] = acc * scale if scale != 1.0 else acc

    def out_index(s):
        def index(p, i):
            j = p - 1 - starts[s]
            row = jnp.where(j < 0, 0, jnp.where(j >= counts[s], ni - 1, i))
            return row, jnp.clip(j, 0, counts[s] - 1)
        return index

    first_pass = pl.BlockSpec((tm, D), lambda p, i: (jnp.where(p == 0, i, ni - 1), 0))
    out_specs = [pl.BlockSpec((tm, tn), out_index(s)) for s in range(ns)]
    out_shape = [jax.ShapeDtypeStruct((S, seg[0]), F32) for seg in segments]
    in_specs = [first_pass, first_pass, pl.BlockSpec((tn, D), lambda p, i: (jnp.maximum(p - 1, 0), 0))]
    args = [h_rm, h, w_t]
    scratch = [pltpu.VMEM((ni, tm, D), BF16), pltpu.VMEM((ni, tm, D), BF16)]
    if hosted is not None:
        in_specs += [ANY_SPEC] * hn
        args += hosted.arrays
        out_specs += [ANY_SPEC] * hn
        out_shape += hosted.out_shapes()
        scratch += hosted.sem_shapes()
    return pl.pallas_call(
        body, name="inproj", grid=(1 + last_p, ni),
        in_specs=in_specs, out_specs=tuple(out_specs), out_shape=tuple(out_shape), scratch_shapes=scratch,
        compiler_params=_params(("arbitrary", "arbitrary"), BIG_VMEM_LIMIT),
    )(*args)


def _fill_bias_table(tbl, rows, keys_first=False):
    shape = (2 * BLK, BLK) if keys_first else (BLK, 2 * BLK)
    qi = lax.broadcasted_iota(jnp.int32, shape, 1 if keys_first else 0)
    kj = lax.broadcasted_iota(jnp.int32, shape, 0 if keys_first else 1)
    dist = rows.pos(qi) - rows.pos(kj % BLK) + jnp.where(kj < BLK, BLK, 0)
    inside = (dist >= 0) & (dist <= BLK)
    negd = (dist * (-rows.dil)).astype(F32)
    for f, valid in enumerate((inside & (kj >= BLK), inside)):
        for h in range(N_Q_HEADS):
            tbl[f * N_Q_HEADS + h] = jnp.where(valid, SLOPES[h] * negd, NEG)


def _bias2(tbl, n, h0, h1, axis=0):
    base = jnp.where(n == 0, 0, N_Q_HEADS)
    return jnp.concatenate([tbl[base + h0], tbl[base + h1]], axis=axis)


def _head_operands(kv2, hk, lo_mask):
    half, pos = hk // 2, hk % 2
    out = []
    for base in (0, KV_W):
        t = kv2[:, base + half * LANES: base + (half + 1) * LANES]
        sw = pltpu.roll(t, HEAD_DIM, axis=1)
        at_lo, at_hi = (t, sw) if pos == 0 else (sw, t)
        out.append(jnp.where(lo_mask, at_lo, 0.0).astype(BF16))
        out.append(jnp.where(lo_mask, 0.0, at_hi).astype(BF16))
    return out


def _nt(a, b):
    return lax.dot_general(a, b, (((1,), (1,)), ((), ())), preferred_element_type=F32)


def _tn(a, b):
    return lax.dot_general(a, b, (((0,), (0,)), ((), ())), preferred_element_type=F32)


def _attn_fwd(q, kv, dil, name, prev=(), gate=None):
    S = q.shape[0]
    rows = _Rows(dil, S)
    nb = rows.nb
    have_prev, last = len(prev) > 0, gate is not None

    def body(*refs):
        refs = list(refs)
        q_ref, kvc_ref, kvp_ref = refs[:3]
        del refs[:3]
        po_refs, pl_refs = refs[0:2 * len(prev):2], refs[1:2 * len(prev):2]
        del refs[:2 * len(prev)]
        if last:
            gate_ref = refs.pop(0)
        o_ref, lse_ref = refs[:2]
        y_ref = refs[2] if last else None
        tbl = refs[-1]
        n = pl.program_id(1)

        @pl.when((pl.program_id(0) == 0) & (n == 0))
        def _():
            _fill_bias_table(tbl, rows)

        kv2 = jnp.concatenate([_ld(kvp_ref), _ld(kvc_ref)], axis=0)
        lo_mask = lax.broadcasted_iota(jnp.int32, (2 * BLK, LANES), 1) < HEAD_DIM
        lane = lax.broadcasted_iota(jnp.int32, (BLK, LANES), 1)
        stats = jnp.zeros((BLK, LANES), F32)
        for hk in range(N_KV_HEADS):
            k_lo, k_hi, v_lo, v_hi = _head_operands(kv2, hk, lo_mask)
            cols = [slice(b * LANES, (b + 1) * LANES) for b in (2 * hk, 2 * hk + 1)]
            q2 = jnp.concatenate([_ld(q_ref, cols[0]), _ld(q_ref, cols[1])], axis=0).astype(BF16)
            o2 = jnp.zeros((2 * BLK, LANES), F32)
            for which, (kk, vv) in enumerate(((k_lo, v_lo), (k_hi, v_hi))):
                h0, h1 = 4 * hk + which, 4 * hk + 2 + which
                s = _nt(q2, kk) + _bias2(tbl, n, h0, h1)
                m = jnp.max(s, axis=1, keepdims=True)
                p = jnp.exp(s - m)
                l = jnp.sum(p, axis=1, keepdims=True)
                o2 = o2 + jnp.dot(p.astype(BF16), vv, preferred_element_type=F32) * (1.0 / l)
                lse = m + jnp.log(l)
                stats = jnp.where(lane == h0, lse[0:BLK], stats)
                stats = jnp.where(lane == h1, lse[BLK:], stats)
            _st(o_ref, o2[0:BLK], cols[0])
            _st(o_ref, o2[BLK:], cols[1])
        if have_prev:
            others = [_ld(r) for r in pl_refs]
            top = stats
            for b in others:
                top = jnp.maximum(top, b)
            e_new = jnp.exp(stats - top)
            e_old = [jnp.exp(b - top) for b in others]
            total = e_new
            for e in e_old:
                total = total + e
            stats = top + jnp.log(total)
            inv = 1.0 / total
            w_new, w_old = e_new * inv, [e * inv for e in e_old]
        if have_prev or last:
            lo = lane < HEAD_DIM
            for blk in range(ATT_W // LANES):
                cols = slice(blk * LANES, (blk + 1) * LANES)
                o_blk = _ld(o_ref, cols)
                if have_prev:
                    pick = lambda w: jnp.where(lo, w[:, 2 * blk:2 * blk + 1], w[:, 2 * blk + 1:2 * blk + 2])
                    o_blk = o_blk * pick(w_new)
                    for po_ref, w in zip(po_refs, w_old):
                        o_blk = o_blk + _ld(po_ref, cols) * pick(w)
                    _st(o_ref, o_blk, cols)
                if last:
                    a = _ld(gate_ref, cols)
                    _st(y_ref, (o_blk * (a * _sigmoid(a))).astype(BF16), cols)
        _st(lse_ref, stats)

    here = lambda n: n
    before_n = lambda n: jnp.maximum(n - 1, 0)
    in_specs = [rows.spec(ATT_W, here), rows.spec(2 * KV_W, here), rows.spec(2 * KV_W, before_n)]
    args = [rows.of(q), rows.of(kv), rows.of(kv)]
    for o_other, lse_other in prev:
        in_specs += [rows.spec(ATT_W, here), rows.spec(LANES, here)]
        args += [rows.of(o_other), rows.of(lse_other)]
    out_specs = [rows.spec(ATT_W, here), rows.spec(LANES, here)]
    out_shape = [jax.ShapeDtypeStruct(rows.view + (ATT_W,), F32), jax.ShapeDtypeStruct(rows.view + (LANES,), F32)]
    if last:
        in_specs.append(rows.spec(ATT_W, here))
        args.append(rows.of(gate))
        out_specs.append(rows.spec(ATT_W, here))
        out_shape.append(jax.ShapeDtypeStruct(rows.view + (ATT_W,), BF16))
    res = pl.pallas_call(
        body, name=name, grid=(dil, nb),
        in_specs=in_specs, out_specs=tuple(out_specs), out_shape=tuple(out_shape),
        scratch_shapes=[pltpu.VMEM((2 * N_Q_HEADS, BLK, 2 * BLK), F32)],
        compiler_params=_params(("arbitrary", "arbitrary")),
    )(*args)
    return tuple(r.reshape(S, r.shape[-1]) for r in res)


def _shifted_copies(buf, phases):
    n = phases.shape[1]
    for b in range(1, 8):
        phases[b - 1] = buf[b:b + n, :]


def _window(buf, phases, start, cols):
    b = start % 8
    if b == 0:
        return buf[start:start + 8, cols]
    return phases[b - 1, start - b:start - b + 8, cols]


def _broadcast_taps(w_ref, wb):
    for j in range(CONV_K):
        wb[j] = jnp.broadcast_to(w_ref[j:j + 1, :], wb.shape[1:])


def _conv_fwd(gates, conv_w, conv_b, ln_g, ln_b, tt=256):
    S = gates.shape[0]
    C = conv_w.shape[1]
    hb = tt // CONV_HALO

    def body(val_ref, glu_ref, hval_ref, hglu_ref, gate_ref, w_ref, b_ref, g_ref, beta_ref,
             conv_ref, y_ref, hbuf, hph):
        i = pl.program_id(0)
        halo = hval_ref[...] * _sigmoid(hglu_ref[...])
        hbuf[0:CONV_HALO, :] = jnp.where(i > 0, halo, 0.0)
        hbuf[CONV_HALO:, :] = val_ref[...] * _sigmoid(glu_ref[...])
        _shifted_copies(hbuf, hph)
        for cb in range(C // LANES):
            cols = slice(cb * LANES, (cb + 1) * LANES)
            wj = [jnp.broadcast_to(w_ref[j:j + 1, cols], (8, LANES)) for j in range(CONV_K)]
            for rc in range(tt // 8):
                acc = jnp.zeros((8, LANES), F32)
                for j in range(CONV_K):
                    start = rc * 8 + CONV_HALO - (CONV_K - 1) + j
                    acc = acc + _window(hbuf, hph, start, cols) * wj[j]
                conv_ref[rc * 8:(rc + 1) * 8, cols] = acc
        cv = conv_ref[...] + b_ref[...]
        conv_ref[...] = cv
        mu = jnp.mean(cv, axis=-1, keepdims=True)
        xc = cv - mu
        var = jnp.mean(xc * xc, axis=-1, keepdims=True)
        ln = xc * lax.rsqrt(var + LN_EPS) * g_ref[...] + beta_ref[...]
        gt = gate_ref[...]
        y_ref[...] = (ln * _sigmoid(ln) * (gt * _sigmoid(gt))).astype(BF16)

    vec = pl.BlockSpec((1, C), lambda i: (0, 0))
    return pl.pallas_call(
        body, name="conv_fwd", grid=(S // tt,),
        in_specs=[pl.BlockSpec((tt, C), lambda i: (i, 0)),
                  pl.BlockSpec((tt, C), lambda i: (i, 1)),
                  pl.BlockSpec((CONV_HALO, C), lambda i: (jnp.maximum(i * hb - 1, 0), 0)),
                  pl.BlockSpec((CONV_HALO, C), lambda i: (jnp.maximum(i * hb - 1, 0), 1)),
                  pl.BlockSpec((tt, C), lambda i: (i, 2)),
                  pl.BlockSpec((CONV_HALO, C), lambda i: (0, 0)), vec, vec, vec],
        out_specs=(pl.BlockSpec((tt, C), lambda i: (i, 0)), pl.BlockSpec((tt, C), lambda i: (i, 0))),
        out_shape=(jax.ShapeDtypeStruct((S, C), F32), jax.ShapeDtypeStruct((S, C), BF16)),
        scratch_shapes=[pltpu.VMEM((tt + CONV_HALO, C), F32), pltpu.VMEM((7, tt + CONV_HALO - 8, C), F32)],
        compiler_params=_params(("parallel",)),
    )(gates, gates, gates, gates, gates, conv_w, conv_b, ln_g, ln_b)


def _outproj_loss(x, y_att, y_conv, w_out, gf, target, tm=512):
    S, D = x.shape
    E = y_att.shape[1]

    def body(x_ref, ya_ref, yc_ref, w_ref, gf_ref, t_ref, dx_ref, dxb_ref, loss_ref, ggf_ref):
        @pl.when(pl.program_id(0) == 0)
        def _():
            loss_ref[...] = jnp.zeros_like(loss_ref)
            ggf_ref[...] = jnp.zeros_like(ggf_ref)

        x2 = (x_ref[...] + jnp.dot(_perm_rows(ya_ref[...], True), w_ref[0:E, :], preferred_element_type=F32)
              + jnp.dot(yc_ref[...], w_ref[E:, :], preferred_element_type=F32))
        r = lax.rsqrt(jnp.mean(x2 * x2, axis=-1, keepdims=True) + NORM_EPS)
        nrm = x2 * r
        gfv = gf_ref[...]
        err = nrm * gfv - t_ref[...]
        loss_ref[...] += jnp.sum(err * err, axis=0, keepdims=True)
        dout = err * (1.0 / D)
        ggf_ref[...] += jnp.sum(dout * nrm, axis=0, keepdims=True)
        dn = dout * gfv
        dx2 = r * (dn - nrm * jnp.mean(dn * nrm, axis=-1, keepdims=True))
        dx_ref[...] = dx2
        dxb_ref[...] = dx2.astype(BF16)

    row = lambda w: pl.BlockSpec((tm, w), lambda i: (i, 0))
    vec = pl.BlockSpec((1, D), lambda i: (0, 0))
    return pl.pallas_call(
        body, name="outproj_loss", grid=(S // tm,),
        in_specs=[row(D), row(E), row(E), pl.BlockSpec((2 * E, D), lambda i: (0, 0)), vec, row(D)],
        out_specs=(row(D), row(D), vec, vec),
        out_shape=(jax.ShapeDtypeStruct((S, D), F32), jax.ShapeDtypeStruct((S, D), BF16),
                   jax.ShapeDtypeStruct((1, D), F32), jax.ShapeDtypeStruct((1, D), F32)),
        compiler_params=_params(("arbitrary",)),
    )(x, y_att, y_conv, w_out, gf, target)


def _split3(v):
    hi = v.astype(BF16)
    r1 = v - hi.astype(F32)
    mid = r1.astype(BF16)
    lo = (r1 - mid.astype(F32)).astype(BF16)
    return hi, mid, lo


def _dy_att(dxb, w_out, gates, o, tm=512):
    S, D = dxb.shape
    E = ATT_W

    def body(dx_ref, w_ref, a_ref, o_ref, do_ref, da_ref, dl_ref, dxr_ref):
        dxr = _perm_rows(dx_ref[...], False)
        dxr_ref[...] = dxr
        dya = _nt(dxr, w_ref[...])
        a = a_ref[...]
        ov = o_ref[...]
        sl, dsl = _silu_and_grad(a)
        d_o = dya * sl
        do_ref[...] = d_o
        da_ref[...] = (dya * ov * dsl).astype(BF16)
        ci = lax.broadcasted_iota(jnp.int32, (E, LANES), 0) // HEAD_DIM
        hi = lax.broadcasted_iota(jnp.int32, (E, LANES), 1)
        sel = jnp.where(ci == hi, 1.0, 0.0).astype(BF16)
        acc = jnp.zeros((tm, LANES), F32)
        for part in _split3(d_o * ov):
            acc = acc + jnp.dot(part, sel, preferred_element_type=F32)
        dl_ref[...] = acc

    row = lambda w: pl.BlockSpec((tm, w), lambda i: (i, 0))
    return pl.pallas_call(
        body, name="dy_att", grid=(S // tm,),
        in_specs=[row(D), pl.BlockSpec((E, D), lambda i: (0, 0)), row(E), row(E)],
        out_specs=(row(E), row(E), row(LANES), row(D)),
        out_shape=(jax.ShapeDtypeStruct((S, E), F32), jax.ShapeDtypeStruct((S, E), BF16),
                   jax.ShapeDtypeStruct((S, LANES), F32), jax.ShapeDtypeStruct((S, D), BF16)),
        compiler_params=_params(("parallel",)),
    )(dxb, w_out, gates, o)


def _dy_conv(dxb, w_out, gates, conv_out, ln_g, ln_b, hosted=None, tm=512):
    S, D = dxb.shape
    C = conv_out.shape[1]
    hn = hosted.n if hosted is not None else 0

    def body(dx_ref, w_ref, gate_ref, cv_ref, g_ref, beta_ref, *rest):
        h_ins = rest[:hn]
        dgate_ref, dconv_ref, gg_ref, gb_ref, gcb_ref = rest[hn:hn + 5]
        h_outs = rest[hn + 5:2 * hn + 5]
        h_sems = rest[2 * hn + 5:]

        @pl.when(pl.program_id(0) == 0)
        def _():
            gg_ref[...] = jnp.zeros_like(gg_ref)
            gb_ref[...] = jnp.zeros_like(gb_ref)
            gcb_ref[...] = jnp.zeros_like(gcb_ref)
            if hosted is not None:
                hosted.start(h_ins, h_outs, h_sems)

        dyc = _nt(dx_ref[...], w_ref[...])
        cv = cv_ref[...]
        mu = jnp.mean(cv, axis=-1, keepdims=True)
        xc = cv - mu
        rstd = lax.rsqrt(jnp.mean(xc * xc, axis=-1, keepdims=True) + LN_EPS)
        nrm = xc * rstd
        gv = g_ref[...]
        ln = nrm * gv + beta_ref[...]
        u, du = _silu_and_grad(ln)
        gt = gate_ref[...]
        g2, dg2 = _silu_and_grad(gt)
        dgate_ref[...] = (dyc * u * dg2).astype(BF16)
        d_ln = dyc * g2 * du
        gb_ref[...] += jnp.sum(d_ln, axis=0, keepdims=True)
        gg_ref[...] += jnp.sum(d_ln * nrm, axis=0, keepdims=True)
        dn = d_ln * gv
        d_conv = rstd * (dn - jnp.mean(dn, axis=-1, keepdims=True)
                         - nrm * jnp.mean(dn * nrm, axis=-1, keepdims=True))
        dconv_ref[...] = d_conv
        gcb_ref[...] += jnp.sum(d_conv, axis=0, keepdims=True)

        if hosted is not None:
            pl.when(pl.program_id(0) == S // tm - 1)(lambda: hosted.finish(h_ins, h_outs, h_sems))

    row = lambda w: pl.BlockSpec((tm, w), lambda i: (i, 0))
    vec = pl.BlockSpec((1, C), lambda i: (0, 0))
    in_specs = [row(D), pl.BlockSpec((C, D), lambda i: (1, 0)), pl.BlockSpec((tm, C), lambda i: (i, 2)), row(C),
                vec, vec]
    args = [dxb, w_out, gates, conv_out, ln_g, ln_b]
    out_specs = [row(C), row(C), vec, vec, vec]
    out_shape = [jax.ShapeDtypeStruct((S, C), BF16), jax.ShapeDtypeStruct((S, C), F32)]
    out_shape += [jax.ShapeDtypeStruct((1, C), F32)] * 3
    scratch = []
    if hosted is not None:
        in_specs += [ANY_SPEC] * hn
        args += hosted.arrays
        out_specs += [ANY_SPEC] * hn
        out_shape += hosted.out_shapes()
        scratch += hosted.sem_shapes()
    res = pl.pallas_call(
        body, name="dy_conv", grid=(S // tm,),
        in_specs=in_specs, out_specs=tuple(out_specs), out_shape=tuple(out_shape), scratch_shapes=scratch,
        compiler_params=_params(("arbitrary",)),
    )(*args)
    return res[0], res[1], res[2], res[3], res[4], list(res[5:])


def _conv_bwd(d_conv, gates, d_c_gate, conv_w, hosted=None, tt=256):
    S, C = d_conv.shape
    hb = tt // CONV_HALO
    nt = S // tt
    hn = hosted.n if hosted is not None else 0

    def body(*refs):
        dc_ref, dnext_ref, val_ref, glu_ref, dg_ref, w_ref = refs[:6]
        h_ins = refs[6:6 + hn]
        out_ref, gw_ref = refs[6 + hn:8 + hn]
        h_outs = refs[8 + hn:8 + 2 * hn]
        hbuf, dbuf, dhbuf, dph, wb = refs[8 + 2 * hn:13 + 2 * hn]
        h_sems = refs[13 + 2 * hn:]
        i = pl.program_id(0)

        @pl.when(i == 0)
        def _():
            gw_ref[...] = jnp.zeros_like(gw_ref)
            _broadcast_taps(w_ref, wb)
            if hosted is not None:
                hosted.start(h_ins, h_outs, h_sems)

        val = val_ref[...]
        sg = _sigmoid(glu_ref[...])
        hbuf[...] = val * sg
        dbuf[0:tt, :] = dc_ref[...]
        dbuf[tt:, :] = jnp.where(i < nt - 1, dnext_ref[...], 0.0)
        _shifted_copies(dbuf, dph)
        for cb in range(C // LANES):
            cols = slice(cb * LANES, (cb + 1) * LANES)
            gacc = [jnp.zeros((8, LANES), F32) for _ in range(CONV_K)]
            group = 2
            for rc0 in range(0, tt // 8, group):
                hcur = [hbuf[(rc0 + r) * 8:(rc0 + r + 1) * 8, cols] for r in range(group)]
                accs = [jnp.zeros((8, LANES), F32) for _ in range(group)]
                for j in range(CONV_K):
                    wj = wb[j, :, cols]
                    for r in range(group):
                        dwin = _window(dbuf, dph, (rc0 + r) * 8 + (CONV_K - 1) - j, cols)
                        accs[r] = accs[r] + dwin * wj
                        gacc[j] = gacc[j] + dwin * hcur[r]
                for r in range(group):
                    dhbuf[(rc0 + r) * 8:(rc0 + r + 1) * 8, cols] = accs[r]
            for j in range(CONV_K):
                gw_ref[j:j + 1, cols] += jnp.sum(gacc[j], axis=0, keepdims=True)
        d_h = dhbuf[...]
        out_ref[:, 0:C] = (d_h * sg).astype(BF16)
        out_ref[:, C:2 * C] = (d_h * val * sg * (1.0 - sg)).astype(BF16)
        out_ref[:, 2 * C:3 * C] = dg_ref[...]

        if hosted is not None:
            @pl.when(i == nt - 1)
            def _():
                hosted.finish(h_ins, h_outs, h_sems)

    tile = lambda col: pl.BlockSpec((tt, C), lambda i: (i, col))
    in_specs = [tile(0),
                pl.BlockSpec((CONV_HALO, C), lambda i: (jnp.minimum((i + 1) * hb, S // CONV_HALO - 1), 0)),
                tile(0), tile(1), tile(0),
                pl.BlockSpec((CONV_HALO, C), lambda i: (0, 0))]
    args = [d_conv, d_conv, gates, gates, d_c_gate, conv_w]
    out_specs = [pl.BlockSpec((tt, 3 * C), lambda i: (i, 0)), pl.BlockSpec((CONV_HALO, C), lambda i: (0, 0))]
    out_shape = [jax.ShapeDtypeStruct((S, 3 * C), BF16), jax.ShapeDtypeStruct((CONV_HALO, C), F32)]
    scratch = [pltpu.VMEM((tt, C), F32), pltpu.VMEM((tt + CONV_HALO, C), F32), pltpu.VMEM((tt, C), F32),
               pltpu.VMEM((7, tt + CONV_HALO - 8, C), F32), pltpu.VMEM((CONV_K, 8, C), F32)]
    if hosted is not None:
        in_specs += [ANY_SPEC] * hn
        args += hosted.arrays
        out_specs += [ANY_SPEC] * hn
        out_shape += hosted.out_shapes()
        scratch += hosted.sem_shapes()
    res = pl.pallas_call(
        body, name="conv_bwd", grid=(nt,),
        in_specs=in_specs, out_specs=tuple(out_specs), out_shape=tuple(out_shape), scratch_shapes=scratch,
        compiler_params=_params(("arbitrary",)),
    )(*args)
    return res[0], res[1], list(res[2:])


def _attn_bwd(q, kv, d_o, lse, delta, dil, prev, final, name, hosted=None):
    S = q.shape[0]
    rows = _Rows(dil, S)
    nb = rows.nb
    steps = dil * nb
    out_dt = BF16 if final else F32
    have_prev = prev is not None
    hn = hosted.n if hosted is not None else 0

    def body(*refs):
        refs = list(refs)
        q_ref, do_ref, lse_ref, dl_ref, kvc_ref, kvp_ref = refs[:6]
        del refs[:6]
        if have_prev:
            pdq_ref, pdkv_ref = refs[:2]
            del refs[:2]
        h_ins = refs[:hn]
        dq_ref, dkv_ref = refs[hn:hn + 2]
        h_outs = refs[hn + 2:2 * hn + 2]
        carry, tbl = refs[2 * hn + 2:2 * hn + 4]
        h_sems = refs[2 * hn + 4:]
        t = pl.program_id(0)
        n = t % nb

        @pl.when(t == 0)
        def _():
            if hosted is not None:
                hosted.start(h_ins, h_outs, h_sems)
            _fill_bias_table(tbl, rows, keys_first=True)
            carry[...] = jnp.zeros_like(carry)

        @pl.when(t < steps)
        def _():
            kv2 = jnp.concatenate([_ld(kvp_ref), _ld(kvc_ref)], axis=0)
            lse_t, dl_t = _ld(lse_ref).T, _ld(dl_ref).T
            lo_mask = lax.broadcasted_iota(jnp.int32, (2 * BLK, LANES), 1) < HEAD_DIM
            halves = [jnp.zeros((2 * BLK, LANES), F32) for _ in range(4)]
            for hk in range(N_KV_HEADS):
                k_lo, k_hi, v_lo, v_hi = _head_operands(kv2, hk, lo_mask)
                cols = [slice(b * LANES, (b + 1) * LANES) for b in (2 * hk, 2 * hk + 1)]
                q2 = jnp.concatenate([_ld(q_ref, cols[0]), _ld(q_ref, cols[1])], axis=0).astype(BF16)
                do2 = jnp.concatenate([_ld(do_ref, cols[0]), _ld(do_ref, cols[1])], axis=0).astype(BF16)
                dq2 = jnp.zeros((2 * BLK, LANES), F32)
                dks, dvs = [], []
                for which, (kk, vv) in enumerate(((k_lo, v_lo), (k_hi, v_hi))):
                    h0, h1 = 4 * hk + which, 4 * hk + 2 + which
                    s = _nt(kk, q2) + _bias2(tbl, n, h0, h1, axis=1)
                    lse2 = jnp.concatenate([lse_t[h0:h0 + 1, :], lse_t[h1:h1 + 1, :]], axis=1)
                    dl2 = jnp.concatenate([dl_t[h0:h0 + 1, :], dl_t[h1:h1 + 1, :]], axis=1)
                    p = jnp.exp(s - lse2)
                    ds = (p * (_nt(vv, do2) - dl2)).astype(BF16)
                    dq2 = dq2 + _tn(ds, kk)
                    dks.append(jnp.dot(ds, q2, preferred_element_type=F32))
                    dvs.append(jnp.dot(p.astype(BF16), do2, preferred_element_type=F32))
                dk_sum = jnp.where(lo_mask, dks[0], dks[1])
                dv_sum = jnp.where(lo_mask, dvs[0], dvs[1])
                for jp in range(2):
                    dq_blk = dq2[jp * BLK:(jp + 1) * BLK]
                    if have_prev:
                        dq_blk = dq_blk + _ld(pdq_ref, cols[jp])
                    if final:
                        dq_blk = dq_blk * (HEAD_DIM ** -0.5)
                    _st(dq_ref, dq_blk.astype(out_dt), cols[jp])
                half, pos = hk // 2, hk % 2
                here = lo_mask if pos == 0 else jnp.logical_not(lo_mask)
                dk_tot = dk_sum + pltpu.roll(dk_sum, HEAD_DIM, axis=1)
                dv_tot = dv_sum + pltpu.roll(dv_sum, HEAD_DIM, axis=1)
                halves[half] = halves[half] + jnp.where(here, dk_tot, 0.0)
                halves[2 + half] = halves[2 + half] + jnp.where(here, dv_tot, 0.0)
            for b in range(4):
                cols = slice(b * LANES, (b + 1) * LANES)
                done = carry[:, cols] + halves[b][0:BLK, :]
                if have_prev:
                    done = done + _ld(pdkv_ref, cols)
                _st(dkv_ref, done.astype(out_dt), cols)
                carry[:, cols] = halves[b][BLK:, :]

        @pl.when(t == steps)
        def _():
            done = carry[...]
            if have_prev:
                done = done + _ld(pdkv_ref)
            _st(dkv_ref, done.astype(out_dt))
            if hosted is not None:
                hosted.finish(h_ins, h_outs, h_sems)

    def spec(width, lag):
        def index(t):
            u = jnp.clip(t - lag, 0, steps - 1)
            return rows.index(u // nb, u % nb)
        return pl.BlockSpec(rows.block + (width,), index)

    def key_prev(t):
        u = jnp.minimum(t, steps - 1)
        return rows.index(u // nb, jnp.maximum(u % nb - 1, 0))

    in_specs = [spec(ATT_W, 0), spec(ATT_W, 0), spec(LANES, 0), spec(LANES, 0), spec(2 * KV_W, 0),
                pl.BlockSpec(rows.block + (2 * KV_W,), key_prev)]
    args = [rows.of(q), rows.of(d_o), rows.of(lse), rows.of(delta), rows.of(kv), rows.of(kv)]
    if have_prev:
        in_specs += [spec(ATT_W, 0), spec(2 * KV_W, 1)]
        args += [rows.of(prev[0]), rows.of(prev[1])]
    out_specs = [spec(ATT_W, 0), spec(2 * KV_W, 1)]
    out_shape = [jax.ShapeDtypeStruct(rows.view + (ATT_W,), out_dt),
                 jax.ShapeDtypeStruct(rows.view + (2 * KV_W,), out_dt)]
    scratch = [pltpu.VMEM((BLK, 2 * KV_W), F32), pltpu.VMEM((2 * N_Q_HEADS, 2 * BLK, BLK), F32)]
    if hosted is not None:
        in_specs += [ANY_SPEC] * hn
        args += hosted.arrays
        out_specs += [ANY_SPEC] * hn
        out_shape += hosted.out_shapes()
        scratch += hosted.sem_shapes()
    res = pl.pallas_call(
        body, name=name, grid=(steps + 1,),
        in_specs=in_specs, out_specs=tuple(out_specs), out_shape=tuple(out_shape), scratch_shapes=scratch,
        compiler_params=_params(("arbitrary",)),
    )(*args)
    return (res[0].reshape(S, ATT_W), res[1].reshape(S, 2 * KV_W)), list(res[2:])


def _dh(segments, w_in, x, dx2, g, hosted=None, tm=1024, tk=512):
    S, D = x.shape
    ns = len(segments)
    counts = [a.shape[1] // tk for a, _ in segments]
    starts = [sum(counts[:s]) for s in range(ns)]
    nk = sum(counts)
    hn = hosted.n if hosted is not None else 0

    def body(*refs):
        seg_refs = refs[:ns]
        w_ref, x_ref, dx2_ref, g_ref = refs[ns:ns + 4]
        h_ins = refs[ns + 4:ns + 4 + hn]
        gx_ref, gng_ref = refs[ns + 4 + hn:ns + 6 + hn]
        h_outs = refs[ns + 6 + hn:ns + 6 + 2 * hn]
        acc = refs[ns + 6 + 2 * hn]
        h_sems = refs[ns + 7 + 2 * hn:]
        k, i = pl.program_id(0), pl.program_id(1)

        @pl.when((i == 0) & (k == 0))
        def _():
            gng_ref[...] = jnp.zeros_like(gng_ref)
            if hosted is not None:
                hosted.start(h_ins, h_outs, h_sems)

        @pl.when(k == 0)
        def _():
            acc[i] = jnp.zeros(acc.shape[1:], F32)

        for s in range(ns):
            @pl.when((k >= starts[s]) & (k < starts[s] + counts[s]))
            def _(s=s):
                t = seg_refs[s][...]
                if segments[s][1]:
                    t = _perm_rows(t, True)
                acc[i] += jnp.dot(t, w_ref[...], preferred_element_type=F32)

        @pl.when(k == nk - 1)
        def _():
            dh = acc[i]
            xf = x_ref[...]
            r = lax.rsqrt(jnp.mean(xf * xf, axis=-1, keepdims=True) + NORM_EPS)
            nrm = xf * r
            gng_ref[...] += jnp.sum(dh * nrm, axis=0, keepdims=True)
            dn = dh * g_ref[...]
            gx_ref[...] = dx2_ref[...] + r * (dn - nrm * jnp.mean(dn * nrm, axis=-1, keepdims=True))

        if hosted is not None:
            @pl.when((i == S // tm - 1) & (k == nk - 1))
            def _():
                hosted.finish(h_ins, h_outs, h_sems)

    ni = S // tm
    row = pl.BlockSpec((tm, D), lambda k, i: (jnp.where(k == nk - 1, i, 0), 0))
    vec = pl.BlockSpec((1, D), lambda k, i: (0, 0))

    def seg_index(s):
        def index(k, i):
            j = k - starts[s]
            return jnp.where(j < 0, 0, jnp.where(j >= counts[s], ni - 1, i)), jnp.clip(j, 0, counts[s] - 1)
        return index

    in_specs = [pl.BlockSpec((tm, tk), seg_index(s)) for s in range(ns)]
    in_specs += [pl.BlockSpec((tk, D), lambda k, i: (k, 0)), row, row, vec]
    args = [a for a, _ in segments] + [w_in, x, dx2, g]
    out_specs = [row, vec]
    out_shape = [jax.ShapeDtypeStruct((S, D), F32), jax.ShapeDtypeStruct((1, D), F32)]
    scratch = [pltpu.VMEM((ni, tm, D), F32)]
    if hosted is not None:
        in_specs += [ANY_SPEC] * hn
        args += hosted.arrays
        out_specs += [ANY_SPEC] * hn
        out_shape += hosted.out_shapes()
        scratch += hosted.sem_shapes()
    res = pl.pallas_call(
        body, name="dh", grid=(nk, S // tm),
        in_specs=in_specs, out_specs=tuple(out_specs), out_shape=tuple(out_shape), scratch_shapes=scratch,
        compiler_params=_params(("arbitrary", "arbitrary"), BIG_VMEM_LIMIT),
    )(*args)
    return res[0], res[1], list(res[2:])


def _tn_matmul(a, bs, name, b_first=False, tm=512):
    M, K = a.shape
    nb = len(bs)
    shapes = [(b.shape[1], K) if b_first else (K, b.shape[1]) for b in bs]

    def body(a_ref, *refs):
        @pl.when(pl.program_id(0) == 0)
        def _():
            for o_ref in refs[nb:]:
                o_ref[...] = jnp.zeros_like(o_ref)

        at = a_ref[...]
        for b_ref, o_ref in zip(refs[:nb], refs[nb:]):
            for c in range(0, b_ref.shape[1], 512):
                if b_first:
                    o_ref[c:c + 512, :] += _tn(b_ref[:, c:c + 512], at)
                else:
                    o_ref[:, c:c + 512] += _tn(at, b_ref[:, c:c + 512])

    return pl.pallas_call(
        body, name=name, grid=(M // tm,),
        in_specs=[pl.BlockSpec((tm, K), lambda m: (m, 0))] + [pl.BlockSpec((tm, b.shape[1]), lambda m: (m, 0))
                                                              for b in bs],
        out_specs=tuple(pl.BlockSpec(s, lambda m: (0, 0)) for s in shapes),
        out_shape=tuple(jax.ShapeDtypeStruct(s, F32) for s in shapes),
        compiler_params=_params(("arbitrary",)),
    )(a, *bs)


def _adamw(parts, w, m, v, name, tr=None, split=None, by_chip=False):
    R, C = w.shape
    tr = R if tr is None else tr
    parts = [parts] if split is None else list(parts)
    npar = len(parts)

    def total(p_ref):
        if by_chip:
            c = lax.axis_index("c")
            g = p_ref[c].astype(F32)
            for chip in range(1, N_DEV // 2):
                g = g + p_ref[2 * chip + c].astype(F32)
            return g
        g = p_ref[0].astype(F32)
        for dev in range(1, N_DEV):
            g = g + p_ref[dev].astype(F32)
        return g

    def body(*refs):
        w_ref, m_ref, v_ref, g_out, d_out, m_out, v_out = refs[npar:]
        if split is None:
            g = total(refs[0])
        else:
            g = jnp.where(_mesh_pos()[3] < split, total(refs[0]), total(refs[1]))
        mn = ADAM_B1 * m_ref[...] + (1.0 - ADAM_B1) * g
        vn = ADAM_B2 * v_ref[...] + (1.0 - ADAM_B2) * (g * g)
        m_hat = mn / (1.0 - ADAM_B1 ** ADAM_STEP)
        v_hat = vn / (1.0 - ADAM_B2 ** ADAM_STEP)
        g_out[...] = g
        d_out[...] = -ADAM_LR * (m_hat / (jnp.sqrt(v_hat) + ADAM_EPS) + ADAM_WD * w_ref[...])
        m_out[...] = mn
        v_out[...] = vn

    blk = pl.BlockSpec((tr, C), lambda i: (i, 0))
    shp = jax.ShapeDtypeStruct((R, C), F32)
    return pl.pallas_call(
        body, name=name, grid=(R // tr,),
        in_specs=[pl.BlockSpec((N_DEV, tr, C), lambda i: (0, i, 0))] * npar + [blk, blk, blk],
        out_specs=(blk, blk, blk, blk), out_shape=(shp, shp, shp, shp),
        compiler_params=_params(("parallel",)),
    )(*parts, w, m, v)


def _local_step(x, target, norm_g, w_in, conv_w, conv_b, ln_g, ln_b, w_out, gf, exchanges=None, first_weights=None,
                late_weights=None):
    ex_out, ex_att, ex_conv = exchanges if exchanges is not None else (None, None, None)
    h_rm, h, *first = _norm_rows(x, norm_g, first_weights[0] if first_weights is not None else None)
    if first_weights is not None:
        w_in = first_weights[1](first)
    conv_cols = w_in.shape[0] - 2 * ATT_W - 2 * KV_W
    q, kv, a_gate, gates, *gathered = _inproj(
        h_rm, h, w_in,
        [(ATT_W, HEAD_DIM ** -0.5, True), (2 * KV_W, 1.0, True), (ATT_W, 1.0, True), (conv_cols, 1.0, False)],
        late_weights[0] if late_weights is not None else None)
    if late_weights is not None:
        conv_w, w_out = late_weights[1](gathered)

    alone = [_attn_fwd(q, kv, dil, "attn_fwd_d%d" % dil) for _, dil in PATTERNS[1:]]
    o, lse, y_att = _attn_fwd(q, kv, PATTERNS[0][1], "attn_fwd_d%d" % PATTERNS[0][1], alone, a_gate)
    conv_out, y_conv = _conv_fwd(gates, conv_w, conv_b, ln_g, ln_b)
    dx2, dxb, loss_cols, g_gf = _outproj_loss(x, y_att, y_conv, w_out, gf, target)

    d_o, d_a_gate, delta, dxb_rm = _dy_att(dxb, w_out, a_gate, o)
    g_w_out = jnp.concatenate([_tn_matmul(y_att, [dxb_rm], "gw_out_att")[0],
                               _tn_matmul(y_conv, [dxb], "gw_out_conv")[0]], axis=0)
    acc, out_parts = None, []
    for idx, (_, dil) in enumerate(reversed(PATTERNS)):
        hosted = ex_out(g_w_out) if (idx == 0 and ex_out is not None) else None
        acc, outs = _attn_bwd(q, kv, d_o, lse, delta, dil, acc, idx == len(PATTERNS) - 1, "attn_bwd_d%d" % dil,
                              hosted)
        out_parts += outs
    dq, dkv = acc
    g_q, g_kv, g_a = _tn_matmul(h_rm, [dq, dkv, d_a_gate], "gw_in_att", b_first=True)

    swap = ex_att[0](g_q, g_kv, g_a) if ex_att is not None else None
    d_c_gate, d_conv, g_ln_g, g_ln_b, g_conv_b, swapped = _dy_conv(dxb, w_out, gates, conv_out, ln_g, ln_b, swap)
    dgates, g_conv_w, att_parts = _conv_bwd(d_conv, gates, d_c_gate, conv_w,
                                            ex_att[1](swap, swapped[0]) if ex_att is not None else None)
    g_c, = _tn_matmul(h, [dgates], "gw_in_conv", b_first=True)
    grad_x, g_norm_g, conv_parts = _dh(
        [(dq, True), (dkv, True), (d_a_gate, True), (dgates, False)], w_in, x, dx2, norm_g,
        ex_conv(g_a, g_c, g_conv_w) if ex_conv is not None else None)
    small = (g_norm_g, g_conv_b, g_ln_g, g_ln_b, g_gf, loss_cols)
    return grad_x, (g_q, g_kv, g_a, g_c), g_w_out, g_conv_w, small, (out_parts, att_parts, conv_parts)


def kernel(x, norm_g, w_in, conv_w, conv_b, conv_ln_g, conv_ln_b, w_out, final_norm_g, loss_target, m_norm_g, m_w_in, m_conv_w, m_conv_b, m_conv_ln_g, m_conv_ln_b, m_w_out, m_final_norm_g, v_norm_g, v_w_in, v_conv_w, v_conv_b, v_conv_ln_g, v_conv_ln_b, v_w_out, v_final_norm_g):
    S, D = x.shape[1], x.shape[2]
    win_sh, wout_sh, cw_sh = w_in[0].T, w_out[0], conv_w[0]
    cols_sh, rows_sh, ch_sh = win_sh.shape[0], wout_sh.shape[0], cw_sh.shape[1]

    def first_weights(gathered):
        return gathered[0].reshape(N_DEV * cols_sh, D)

    def late_weights(gathered):
        wout_all, cw_all = gathered
        conv_w_full = cw_all.transpose(1, 0, 2).reshape(CONV_K, N_DEV * ch_sh)
        return jnp.pad(conv_w_full, ((0, CONV_HALO - CONV_K), (0, 0))), wout_all.reshape(N_DEV * rows_sh, D)

    gf = final_norm_g.reshape(1, D)

    first = -(-(ATT_W + 2 * KV_W) // cols_sh)
    a_off = first * cols_sh - (ATT_W + 2 * KV_W)
    assert 0 <= a_off <= ATT_W

    def pieces(parts, n):
        return jnp.concatenate([p.astype(BF16) for p in parts], axis=0).reshape(n, cols_sh, D)

    def ex_out(g_w_out):
        return _Exchange([g_w_out.reshape(N_DEV, rows_sh, D).astype(BF16)], [(0, N_DEV)])

    same_core = (2, 4, 6)

    def att_swap(g_q, g_kv, g_a):
        return _Swap(pieces([g_q, g_kv, g_a[:a_off]], first), 0)

    def att_exchange(swap, other):
        return _Exchange([_pair_add(swap.arrays[0], other, "rs_att_add")], [(0, first)], [same_core])

    def ex_conv(g_a, g_c, g_conv_w):
        mine = pieces([g_a[a_off:], g_c], N_DEV - first)
        other, = _exchange(_Swap(mine, first), "rs_conv_swap")
        mine = _pair_add(mine, other, "rs_conv_add")
        return _Exchange(
            [mine, g_conv_w[:CONV_K].reshape(CONV_K, N_DEV, ch_sh).transpose(1, 0, 2)],
            [(first, N_DEV), (0, N_DEV)], [same_core, None])

    grad_x, _, _, _, small, parts = _local_step(
        x[0], loss_target[0], norm_g, None, None, conv_b, conv_ln_g, conv_ln_b, None, gf,
        (ex_out, (att_swap, att_exchange), ex_conv), (_Gather([win_sh.astype(BF16)]), first_weights),
        (_Gather([wout_sh.astype(BF16), cw_sh]), late_weights))
    (wout_parts,), (win_parts_lo,), (win_parts_hi, cw_parts) = parts

    small_pack = jnp.concatenate(list(small) + [jnp.zeros((2, D), F32)], axis=0)
    small_parts, = _exchange(_Exchange([small_pack], [None]), "gather_small")

    upd_win = _adamw((win_parts_lo, win_parts_hi), win_sh, m_w_in[0].T, v_w_in[0].T, "adamw_w_in",
                     tr=cols_sh // 2, split=first, by_chip=True)
    upd_wout = _adamw(wout_parts, wout_sh, m_w_out[0], v_w_out[0], "adamw_w_out", tr=128)
    upd_cw = _adamw(cw_parts, cw_sh, m_conv_w[0], v_conv_w[0], "adamw_conv_w")
    zeros3 = jnp.zeros((3, D), F32)
    stack = lambda a, b, c, d_, e: jnp.concatenate([a, b, c, d_, e.reshape(1, D), zeros3], axis=0)
    upd_small = _adamw(
        small_parts,
        stack(norm_g, conv_b, conv_ln_g, conv_ln_b, final_norm_g),
        stack(m_norm_g, m_conv_b, m_conv_ln_g, m_conv_ln_b, m_final_norm_g),
        stack(v_norm_g, v_conv_b, v_conv_ln_g, v_conv_ln_b, v_final_norm_g) + jnp.concatenate(
            [jnp.zeros((5, D), F32), jnp.ones((3, D), F32)], axis=0),
        "adamw_small")

    loss = 0.5 / D * jnp.sum(upd_small[0][5])

    def outputs(kind):
        sm = upd_small[kind]
        return [sm[0:1], upd_win[kind].T[None], upd_cw[kind][None], sm[1:2], sm[2:3], sm[3:4],
                upd_wout[kind][None], sm[4]]

    return (loss, grad_x[None], *outputs(0), *outputs(1), *outputs(2), *outputs(3))
```

```python
import jax
import jax.numpy as jnp
from jax import lax
from jax.experimental import pallas as pl
from jax.experimental.pallas import tpu as pltpu

F32 = jnp.float32
BF16 = jnp.bfloat16

HEAD_DIM = 64
N_KV_HEADS = 4
N_Q_HEADS = 16
ATT_W = 1024
KV_W = 256
CONV_K = 31
CONV_HALO = 32
PATTERNS = ((128, 1), (512, 4), (2048, 16))
BLK = 128
LANES = 128
NORM_EPS = 1e-6
LN_EPS = 1e-5
NEG = -1e30
N_DEV = 8
ADAM_LR, ADAM_B1, ADAM_B2, ADAM_EPS, ADAM_WD, ADAM_STEP = 0.001, 0.9, 0.999, 1e-08, 0.01, 10
VMEM_LIMIT = 48 * 1024 * 1024
BIG_VMEM_LIMIT = 58 * 1024 * 1024
SLOPES = tuple(2.0 ** (-8.0 * (h + 1) / N_Q_HEADS) for h in range(N_Q_HEADS))
MESH = pl.DeviceIdType.MESH


def _params(sem, vmem_limit=VMEM_LIMIT):
    return pltpu.CompilerParams(dimension_semantics=sem, vmem_limit_bytes=vmem_limit)


def _sigmoid(v):
    return 1.0 / (1.0 + jnp.exp(-v))


def _silu_and_grad(v):
    s = _sigmoid(v)
    return v * s, s * (1.0 + v * (1.0 - s))


ANY_SPEC = pl.BlockSpec(memory_space=pl.ANY)


def _mesh_pos():
    x, y, c = lax.axis_index("x"), lax.axis_index("y"), lax.axis_index("c")
    return x, y, c, 4 * x + 2 * y + c


def _flipped(k, x, y, c):
    px = 1 - x if k & 4 else x
    py = 1 - y if k & 2 else y
    pc = 1 - c if k & 1 else c
    return (px, py, pc), 4 * px + 2 * py + pc


class _Exchange:
    def __init__(self, arrays, dests, flips=None):
        self.arrays, self.dests, self.n = list(arrays), list(dests), len(arrays)
        self.flips = [tuple(range(1, N_DEV)) if f is None else tuple(f)
                      for f in (flips if flips is not None else [None] * self.n)]

    def out_shapes(self):
        return [jax.ShapeDtypeStruct((N_DEV,) + a.shape[-2:], a.dtype) for a in self.arrays]

    def sem_shapes(self):
        return [pltpu.SemaphoreType.DMA((self.n, N_DEV - 1)), pltpu.SemaphoreType.DMA((self.n, N_DEV - 1)),
                pltpu.SemaphoreType.DMA((self.n,))]

    def _when(self, a, dev, fn):
        if self.dests[a] is None:
            fn()
        else:
            lo, hi = self.dests[a]
            pl.when((dev >= lo) & (dev < hi))(fn)

    def _mine(self, ins, a, dev):
        return ins[a] if self.dests[a] is None else ins[a].at[dev - self.dests[a][0]]

    def _copy(self, ins, outs, sems, a, k, src_dev, slot, target):
        return pltpu.make_async_remote_copy(
            src_ref=self._mine(ins, a, src_dev), dst_ref=outs[a].at[slot],
            send_sem=sems[0].at[a, k - 1], recv_sem=sems[1].at[a, k - 1],
            device_id=target, device_id_type=MESH)

    def start(self, ins, outs, sems):
        x, y, c, me = _mesh_pos()
        for a in range(self.n):
            self._when(a, me, lambda a=a: pltpu.make_async_copy(
                self._mine(ins, a, me), outs[a].at[me], sems[2].at[a]).start())
            for k in self.flips[a]:
                target, peer = _flipped(k, x, y, c)
                self._when(a, peer, lambda a=a, k=k, target=target, peer=peer: self._copy(
                    ins, outs, sems, a, k, peer, me, target).start())

    def finish(self, ins, outs, sems):
        x, y, c, me = _mesh_pos()
        lo0 = [0 if d is None else d[0] for d in self.dests]
        for a in range(self.n):
            for k in self.flips[a]:
                target, peer = _flipped(k, x, y, c)
                self._when(a, me, lambda a=a, k=k, peer=peer: self._copy(
                    ins, outs, sems, a, k, lo0[a], peer, (x, y, c)).wait_recv())
            for k in self.flips[a]:
                target, peer = _flipped(k, x, y, c)
                self._when(a, peer, lambda a=a, k=k, target=target, peer=peer: self._copy(
                    ins, outs, sems, a, k, peer, me, target).wait_send())
            self._when(a, me, lambda a=a: pltpu.make_async_copy(
                self._mine(ins, a, me), outs[a].at[me], sems[2].at[a]).wait())


def _exchange(ex, name):
    na = ex.n

    def body(*refs):
        ins, outs, sems = refs[:na], refs[na:2 * na], refs[2 * na:]
        ex.start(ins, outs, sems)
        ex.finish(ins, outs, sems)

    return pl.pallas_call(
        body, name=name, out_shape=tuple(ex.out_shapes()),
        in_specs=[ANY_SPEC] * na, out_specs=tuple([ANY_SPEC] * na), scratch_shapes=ex.sem_shapes(),
    )(*ex.arrays)


def _chip_sum(pieces, lo, name):
    n, R, C = pieces.shape

    def swap(p_ref, t_ref, send_sems, recv_sems):
        x, y, c, me = _mesh_pos()
        for i in range(n):
            mine = (lo + i) % 2
            cp = pltpu.make_async_remote_copy(
                src_ref=p_ref.at[i], dst_ref=t_ref.at[i], send_sem=send_sems.at[i], recv_sem=recv_sems.at[i],
                device_id=(x, y, 1 - c), device_id_type=MESH)
            pl.when(c != mine)(cp.start)
        for i in range(n):
            mine = (lo + i) % 2
            cp = pltpu.make_async_remote_copy(
                src_ref=p_ref.at[i], dst_ref=t_ref.at[i], send_sem=send_sems.at[i], recv_sem=recv_sems.at[i],
                device_id=(x, y, 1 - c), device_id_type=MESH)
            pl.when(c == mine)(cp.wait_recv)
            pl.when(c != mine)(cp.wait_send)

    other = pl.pallas_call(
        swap, name=name + "_swap", out_shape=jax.ShapeDtypeStruct(pieces.shape, pieces.dtype),
        in_specs=[ANY_SPEC], out_specs=ANY_SPEC,
        scratch_shapes=[pltpu.SemaphoreType.DMA((n,)), pltpu.SemaphoreType.DMA((n,))],
    )(pieces)

    def add(p_ref, t_ref, o_ref):
        o_ref[...] = (p_ref[...].astype(F32) + t_ref[...].astype(F32)).astype(o_ref.dtype)

    tr = R // 2
    blk = pl.BlockSpec((None, tr, C), lambda i, r: (i, r, 0))
    return pl.pallas_call(
        add, name=name + "_add", grid=(n, R // tr), in_specs=[blk, blk], out_specs=blk,
        out_shape=jax.ShapeDtypeStruct(pieces.shape, pieces.dtype),
        compiler_params=_params(("parallel", "parallel")),
    )(pieces, other)


class _Gather:
    def __init__(self, arrays):
        self.arrays, self.n = list(arrays), len(arrays)

    def out_shapes(self):
        return [jax.ShapeDtypeStruct((N_DEV,) + a.shape, a.dtype) for a in self.arrays]

    def sem_shapes(self):
        return [pltpu.SemaphoreType.DMA((self.n, N_DEV - 1)), pltpu.SemaphoreType.DMA((self.n, N_DEV - 1)),
                pltpu.SemaphoreType.DMA((self.n,))]

    def _plan(self, ins, outs, sems):
        x, y, c, me = _mesh_pos()
        chips = [(1 - x, y), (x, 1 - y), (1 - x, 1 - y)]

        def copy(a, k, src, block, to):
            px, py, pc = block
            return pltpu.make_async_remote_copy(
                src_ref=src, dst_ref=outs[a].at[4 * px + 2 * py + pc], send_sem=sems[0].at[a, k],
                recv_sem=sems[1].at[a, k], device_id=to, device_id_type=MESH)

        def landed(a, block):
            px, py, pc = block
            return outs[a].at[4 * px + 2 * py + pc]

        local = [pltpu.make_async_copy(ins[a], outs[a].at[me], sems[2].at[a]) for a in range(self.n)]
        first = []
        for a in range(self.n):
            first.append(copy(a, 0, ins[a], (x, y, c), (x, y, 1 - c)))
            first += [copy(a, 1 + j, ins[a], (x, y, c), (*chip, c)) for j, chip in enumerate(chips[:2])]
        return (x, y, c), chips, copy, landed, local, first

    def start(self, ins, outs, sems):
        *_, local, first = self._plan(ins, outs, sems)
        for cp in local + first:
            cp.start()

    def finish(self, ins, outs, sems):
        (x, y, c), chips, copy, landed, local, first = self._plan(ins, outs, sems)
        south = c == 0
        came = (jnp.where(south, 1 - x, x), jnp.where(south, y, 1 - y), c)
        goes = (jnp.where(south, x, 1 - x), jnp.where(south, 1 - y, y), c)
        passed = []
        for a in range(self.n):
            for j, chip in enumerate(chips[:2]):
                copy(a, 1 + j, ins[a], (*chip, c), (x, y, c)).wait_recv()
            passed.append(copy(a, 3, landed(a, came), came, goes))
            passed += [copy(a, 4 + j, landed(a, (*chip, c)), (*chip, c), (x, y, 1 - c))
                       for j, chip in enumerate(chips[:2])]
        for cp in passed:
            cp.start()
        for a in range(self.n):
            diagonal = (*chips[2], c)
            copy(a, 3, ins[a], diagonal, (x, y, c)).wait_recv()
            cp = copy(a, 6, landed(a, diagonal), diagonal, (x, y, 1 - c))
            cp.start()
            passed.append(cp)
        for a in range(self.n):
            copy(a, 0, ins[a], (x, y, 1 - c), (x, y, c)).wait_recv()
            for j, chip in enumerate(chips):
                copy(a, 4 + j, ins[a], (*chip, 1 - c), (x, y, c)).wait_recv()
        for cp in first + passed:
            cp.wait_send()
        for cp in local:
            cp.wait()


CHUNK = 128
RESIDUES = 16
PER_RES = CHUNK // RESIDUES


def _perm_rows(tile, inverse):
    a = lax.broadcasted_iota(jnp.int32, (CHUNK, CHUNK), 0)
    b = lax.broadcasted_iota(jnp.int32, (CHUNK, CHUNK), 1)
    if inverse:
        a, b = b, a
    p = jnp.where(a == PER_RES * (b % RESIDUES) + b // RESIDUES, 1.0, 0.0).astype(BF16)
    parts = [jnp.dot(p, tile[c * CHUNK:(c + 1) * CHUNK], preferred_element_type=F32)
             for c in range(tile.shape[0] // CHUNK)]
    return jnp.concatenate(parts, axis=0).astype(BF16)


class _Rows:
    def __init__(self, dil, S):
        nc = S // CHUNK
        self.dil = dil
        if dil == 1:
            self.view, self.block, self.nb = (nc, CHUNK), (None, CHUNK), nc
            self.index = lambda r, b: (b, 0, 0)
        elif dil == 4:
            self.view, self.block, self.nb = (nc, 4, 4, PER_RES), (4, 4, None, PER_RES), nc // 4
            self.index = lambda r, b: (b, 0, r, 0, 0)
        elif dil == RESIDUES:
            self.view, self.block, self.nb = (nc, RESIDUES, PER_RES), (RESIDUES, None, PER_RES), nc // RESIDUES
            self.index = lambda r, b: (b, r, 0, 0)
        else:
            raise NotImplementedError(dil)

    def of(self, a):
        return a.reshape(self.view + (a.shape[-1],))

    def spec(self, width, which_block):
        return pl.BlockSpec(self.block + (width,), lambda r, n: self.index(r, which_block(n)))

    def pos(self, row):
        if self.dil == 1:
            return (row % PER_RES) * RESIDUES + row // PER_RES
        if self.dil == 4:
            return (row // 32) * 32 + (row % PER_RES) * 4 + (row % 32) // PER_RES
        return row


def _ld(ref, cols=slice(None)):
    v = ref[(slice(None),) * (len(ref.shape) - 1) + (cols,)]
    return v.reshape(BLK, v.shape[-1])


def _st(ref, val, cols=slice(None)):
    ref[(slice(None),) * (len(ref.shape) - 1) + (cols,)] = val.reshape(ref.shape[:-1] + (val.shape[-1],))


def _norm_rows(x, g, hosted=None, tm=512):
    S, D = x.shape
    hn = hosted.n if hosted is not None else 0

    def body(x_ref, g_ref, *rest):
        h_ins = rest[:hn]
        hrm_out, h_out = rest[hn:hn + 2]
        h_outs = rest[hn + 2:2 * hn + 2]
        h_sems = rest[2 * hn + 2:]
        i = pl.program_id(0)
        if hosted is not None:
            pl.when(i == 0)(lambda: hosted.start(h_ins, h_outs, h_sems))
        xf = x_ref[...]
        r = lax.rsqrt(jnp.mean(xf * xf, axis=-1, keepdims=True) + NORM_EPS)
        h = (xf * r * g_ref[...]).astype(BF16)
        h_out[...] = h
        hrm_out[...] = _perm_rows(h, False)
        if hosted is not None:
            pl.when(i == S // tm - 1)(lambda: hosted.finish(h_ins, h_outs, h_sems))

    row = pl.BlockSpec((tm, D), lambda i: (i, 0))
    in_specs, args = [row, pl.BlockSpec((1, D), lambda i: (0, 0))], [x, g]
    out_specs, out_shape, scratch = [row, row], [jax.ShapeDtypeStruct((S, D), BF16)] * 2, []
    if hosted is not None:
        in_specs += [ANY_SPEC] * hn
        args += hosted.arrays
        out_specs += [ANY_SPEC] * hn
        out_shape += hosted.out_shapes()
        scratch += hosted.sem_shapes()
    return pl.pallas_call(
        body, name="norm_rows", grid=(S // tm,),
        in_specs=in_specs, out_specs=tuple(out_specs), out_shape=tuple(out_shape), scratch_shapes=scratch,
        compiler_params=_params(("arbitrary",)),
    )(*args)


def _inproj(h_rm, h, w_t, segments, hosted=None, tm=1024, tn=512):
    S, D = h.shape
    ns = len(segments)
    ni = S // tm
    counts = [seg[0] // tn for seg in segments]
    starts = [sum(counts[:s]) for s in range(ns)]

    hn = hosted.n if hosted is not None else 0
    last_p = sum(counts)

    def body(hrm_ref, h_ref, w_ref, *rest):
        h_ins, rest = rest[:hn], rest[hn:]
        outs = rest[:ns]
        h_outs = rest[ns:ns + hn]
        hrm_scr, h_scr = rest[ns + hn:ns + 2 + hn]
        h_sems = rest[ns + 2 + hn:]
        p, i = pl.program_id(0), pl.program_id(1)

        if hosted is not None:
            @pl.when((p == 0) & (i == 0))
            def _():
                hosted.start(h_ins, h_outs, h_sems)

            @pl.when((p == last_p) & (i == ni - 1))
            def _():
                hosted.finish(h_ins, h_outs, h_sems)

        @pl.when(p == 0)
        def _():
            h_scr[i] = h_ref[...]
            hrm_scr[i] = hrm_ref[...]

        for s, (_, scale, rm) in enumerate(segments):
            @pl.when((p > starts[s]) & (p <= starts[s] + counts[s]))
            def _(s=s, scale=scale, rm=rm):
                acc = _nt((hrm_scr if rm else h_scr)[i], w_ref[...])
                outs[s][...] = acc * scale if scale != 1.0 else acc

    def out_index(s):
        def index(p, i):
            j = p - 1 - starts[s]
            row = jnp.where(j < 0, 0, jnp.where(j >= counts[s], ni - 1, i))
            return row, jnp.clip(j, 0, counts[s] - 1)
        return index

    first_pass = pl.BlockSpec((tm, D), lambda p, i: (jnp.where(p == 0, i, ni - 1), 0))
    out_specs = [pl.BlockSpec((tm, tn), out_index(s)) for s in range(ns)]
    out_shape = [jax.ShapeDtypeStruct((S, seg[0]), F32) for seg in segments]
    in_specs = [first_pass, first_pass, pl.BlockSpec((tn, D), lambda p, i: (jnp.maximum(p - 1, 0), 0))]
    args = [h_rm, h, w_t]
    scratch = [pltpu.VMEM((ni, tm, D), BF16), pltpu.VMEM((ni, tm, D), BF16)]
    if hosted is not None:
        in_specs += [ANY_SPEC] * hn
        args += hosted.arrays
        out_specs += [ANY_SPEC] * hn
        out_shape += hosted.out_shapes()
        scratch += hosted.sem_shapes()
    return pl.pallas_call(
        body, name="inproj", grid=(1 + last_p, ni),
        in_specs=in_specs, out_specs=tuple(out_specs), out_shape=tuple(out_shape), scratch_shapes=scratch,
        compiler_params=_params(("arbitrary", "arbitrary"), BIG_VMEM_LIMIT),
    )(*args)


def _fill_bias_table(tbl, rows, keys_first=False):
    shape = (2 * BLK, BLK) if keys_first else (BLK, 2 * BLK)
    qi = lax.broadcasted_iota(jnp.int32, shape, 1 if keys_first else 0)
    kj = lax.broadcasted_iota(jnp.int32, shape, 0 if keys_first else 1)
    dist = rows.pos(qi) - rows.pos(kj % BLK) + jnp.where(kj < BLK, BLK, 0)
    inside = (dist >= 0) & (dist <= BLK)
    negd = (dist * (-rows.dil)).astype(F32)
    for f, valid in enumerate((inside & (kj >= BLK), inside)):
        for h in range(N_Q_HEADS):
            tbl[f * N_Q_HEADS + h] = jnp.where(valid, SLOPES[h] * negd, NEG)


def _bias2(tbl, n, h0, h1, axis=0):
    base = jnp.where(n == 0, 0, N_Q_HEADS)
    return jnp.concatenate([tbl[base + h0], tbl[base + h1]], axis=axis)


def _head_operands(kv2, hk, lo_mask):
    half, pos = hk // 2, hk % 2
    out = []
    for base in (0, KV_W):
        t = kv2[:, base + half * LANES: base + (half + 1) * LANES]
        sw = pltpu.roll(t, HEAD_DIM, axis=1)
        at_lo, at_hi = (t, sw) if pos == 0 else (sw, t)
        out.append(jnp.where(lo_mask, at_lo, 0.0).astype(BF16))
        out.append(jnp.where(lo_mask, 0.0, at_hi).astype(BF16))
    return out


def _nt(a, b):
    return lax.dot_general(a, b, (((1,), (1,)), ((), ())), preferred_element_type=F32)


def _tn(a, b):
    return lax.dot_general(a, b, (((0,), (0,)), ((), ())), preferred_element_type=F32)


def _attn_fwd(q, kv, dil, name, prev=(), gate=None):
    S = q.shape[0]
    rows = _Rows(dil, S)
    nb = rows.nb
    have_prev, last = len(prev) > 0, gate is not None

    def body(*refs):
        refs = list(refs)
        q_ref, kvc_ref, kvp_ref = refs[:3]
        del refs[:3]
        po_refs, pl_refs = refs[0:2 * len(prev):2], refs[1:2 * len(prev):2]
        del refs[:2 * len(prev)]
        if last:
            gate_ref = refs.pop(0)
        o_ref, lse_ref = refs[:2]
        y_ref = refs[2] if last else None
        tbl = refs[-1]
        n = pl.program_id(1)

        @pl.when((pl.program_id(0) == 0) & (n == 0))
        def _():
            _fill_bias_table(tbl, rows)

        kv2 = jnp.concatenate([_ld(kvp_ref), _ld(kvc_ref)], axis=0)
        lo_mask = lax.broadcasted_iota(jnp.int32, (2 * BLK, LANES), 1) < HEAD_DIM
        lane = lax.broadcasted_iota(jnp.int32, (BLK, LANES), 1)
        stats = jnp.zeros((BLK, LANES), F32)
        for hk in range(N_KV_HEADS):
            k_lo, k_hi, v_lo, v_hi = _head_operands(kv2, hk, lo_mask)
            cols = [slice(b * LANES, (b + 1) * LANES) for b in (2 * hk, 2 * hk + 1)]
            q2 = jnp.concatenate([_ld(q_ref, cols[0]), _ld(q_ref, cols[1])], axis=0).astype(BF16)
            o2 = jnp.zeros((2 * BLK, LANES), F32)
            for which, (kk, vv) in enumerate(((k_lo, v_lo), (k_hi, v_hi))):
                h0, h1 = 4 * hk + which, 4 * hk + 2 + which
                s = _nt(q2, kk) + _bias2(tbl, n, h0, h1)
                m = jnp.max(s, axis=1, keepdims=True)
                p = jnp.exp(s - m)
                l = jnp.sum(p, axis=1, keepdims=True)
                o2 = o2 + jnp.dot(p.astype(BF16), vv, preferred_element_type=F32) * (1.0 / l)
                lse = m + jnp.log(l)
                stats = jnp.where(lane == h0, lse[0:BLK], stats)
                stats = jnp.where(lane == h1, lse[BLK:], stats)
            _st(o_ref, o2[0:BLK], cols[0])
            _st(o_ref, o2[BLK:], cols[1])
        if have_prev:
            others = [_ld(r) for r in pl_refs]
            top = stats
            for b in others:
                top = jnp.maximum(top, b)
            e_new = jnp.exp(stats - top)
            e_old = [jnp.exp(b - top) for b in others]
            total = e_new
            for e in e_old:
                total = total + e
            stats = top + jnp.log(total)
            inv = 1.0 / total
            w_new, w_old = e_new * inv, [e * inv for e in e_old]
        if have_prev or last:
            lo = lane < HEAD_DIM
            for blk in range(ATT_W // LANES):
                cols = slice(blk * LANES, (blk + 1) * LANES)
                o_blk = _ld(o_ref, cols)
                if have_prev:
                    pick = lambda w: jnp.where(lo, w[:, 2 * blk:2 * blk + 1], w[:, 2 * blk + 1:2 * blk + 2])
                    o_blk = o_blk * pick(w_new)
                    for po_ref, w in zip(po_refs, w_old):
                        o_blk = o_blk + _ld(po_ref, cols) * pick(w)
                    _st(o_ref, o_blk, cols)
                if last:
                    a = _ld(gate_ref, cols)
                    _st(y_ref, (o_blk * (a * _sigmoid(a))).astype(BF16), cols)
        _st(lse_ref, stats)

    here = lambda n: n
    before_n = lambda n: jnp.maximum(n - 1, 0)
    in_specs = [rows.spec(ATT_W, here), rows.spec(2 * KV_W, here), rows.spec(2 * KV_W, before_n)]
    args = [rows.of(q), rows.of(kv), rows.of(kv)]
    for o_other, lse_other in prev:
        in_specs += [rows.spec(ATT_W, here), rows.spec(LANES, here)]
        args += [rows.of(o_other), rows.of(lse_other)]
    out_specs = [rows.spec(ATT_W, here), rows.spec(LANES, here)]
    out_shape = [jax.ShapeDtypeStruct(rows.view + (ATT_W,), F32), jax.ShapeDtypeStruct(rows.view + (LANES,), F32)]
    if last:
        in_specs.append(rows.spec(ATT_W, here))
        args.append(rows.of(gate))
        out_specs.append(rows.spec(ATT_W, here))
        out_shape.append(jax.ShapeDtypeStruct(rows.view + (ATT_W,), BF16))
    res = pl.pallas_call(
        body, name=name, grid=(dil, nb),
        in_specs=in_specs, out_specs=tuple(out_specs), out_shape=tuple(out_shape),
        scratch_shapes=[pltpu.VMEM((2 * N_Q_HEADS, BLK, 2 * BLK), F32)],
        compiler_params=_params(("arbitrary", "arbitrary")),
    )(*args)
    return tuple(r.reshape(S, r.shape[-1]) for r in res)


def _shifted_copies(buf, phases):
    n = phases.shape[1]
    for b in range(1, 8):
        phases[b - 1] = buf[b:b + n, :]


def _window(buf, phases, start, cols):
    b = start % 8
    if b == 0:
        return buf[start:start + 8, cols]
    return phases[b - 1, start - b:start - b + 8, cols]


def _broadcast_taps(w_ref, wb):
    for j in range(CONV_K):
        wb[j] = jnp.broadcast_to(w_ref[j:j + 1, :], wb.shape[1:])


def _conv_fwd(gates, conv_w, conv_b, ln_g, ln_b, tt=256):
    S = gates.shape[0]
    C = conv_w.shape[1]
    hb = tt // CONV_HALO

    def body(val_ref, glu_ref, hval_ref, hglu_ref, gate_ref, w_ref, b_ref, g_ref, beta_ref,
             conv_ref, y_ref, hbuf, hph):
        i = pl.program_id(0)
        halo = hval_ref[...] * _sigmoid(hglu_ref[...])
        hbuf[0:CONV_HALO, :] = jnp.where(i > 0, halo, 0.0)
        hbuf[CONV_HALO:, :] = val_ref[...] * _sigmoid(glu_ref[...])
        _shifted_copies(hbuf, hph)
        for cb in range(C // LANES):
            cols = slice(cb * LANES, (cb + 1) * LANES)
            wj = [jnp.broadcast_to(w_ref[j:j + 1, cols], (8, LANES)) for j in range(CONV_K)]
            for rc in range(tt // 8):
                acc = jnp.zeros((8, LANES), F32)
                for j in range(CONV_K):
                    start = rc * 8 + CONV_HALO - (CONV_K - 1) + j
                    acc = acc + _window(hbuf, hph, start, cols) * wj[j]
                conv_ref[rc * 8:(rc + 1) * 8, cols] = acc
        cv = conv_ref[...] + b_ref[...]
        conv_ref[...] = cv
        mu = jnp.mean(cv, axis=-1, keepdims=True)
        xc = cv - mu
        var = jnp.mean(xc * xc, axis=-1, keepdims=True)
        ln = xc * lax.rsqrt(var + LN_EPS) * g_ref[...] + beta_ref[...]
        gt = gate_ref[...]
        y_ref[...] = (ln * _sigmoid(ln) * (gt * _sigmoid(gt))).astype(BF16)

    vec = pl.BlockSpec((1, C), lambda i: (0, 0))
    return pl.pallas_call(
        body, name="conv_fwd", grid=(S // tt,),
        in_specs=[pl.BlockSpec((tt, C), lambda i: (i, 0)),
                  pl.BlockSpec((tt, C), lambda i: (i, 1)),
                  pl.BlockSpec((CONV_HALO, C), lambda i: (jnp.maximum(i * hb - 1, 0), 0)),
                  pl.BlockSpec((CONV_HALO, C), lambda i: (jnp.maximum(i * hb - 1, 0), 1)),
                  pl.BlockSpec((tt, C), lambda i: (i, 2)),
                  pl.BlockSpec((CONV_HALO, C), lambda i: (0, 0)), vec, vec, vec],
        out_specs=(pl.BlockSpec((tt, C), lambda i: (i, 0)), pl.BlockSpec((tt, C), lambda i: (i, 0))),
        out_shape=(jax.ShapeDtypeStruct((S, C), F32), jax.ShapeDtypeStruct((S, C), BF16)),
        scratch_shapes=[pltpu.VMEM((tt + CONV_HALO, C), F32), pltpu.VMEM((7, tt + CONV_HALO - 8, C), F32)],
        compiler_params=_params(("parallel",)),
    )(gates, gates, gates, gates, gates, conv_w, conv_b, ln_g, ln_b)


def _outproj_loss(x, y_att, y_conv, w_out, gf, target, tm=512):
    S, D = x.shape
    E = y_att.shape[1]

    def body(x_ref, ya_ref, yc_ref, w_ref, gf_ref, t_ref, dx_ref, dxb_ref, loss_ref, ggf_ref):
        @pl.when(pl.program_id(0) == 0)
        def _():
            loss_ref[...] = jnp.zeros_like(loss_ref)
            ggf_ref[...] = jnp.zeros_like(ggf_ref)

        x2 = (x_ref[...] + jnp.dot(_perm_rows(ya_ref[...], True), w_ref[0:E, :], preferred_element_type=F32)
              + jnp.dot(yc_ref[...], w_ref[E:, :], preferred_element_type=F32))
        r = lax.rsqrt(jnp.mean(x2 * x2, axis=-1, keepdims=True) + NORM_EPS)
        nrm = x2 * r
        gfv = gf_ref[...]
        err = nrm * gfv - t_ref[...]
        loss_ref[...] += jnp.sum(err * err, axis=0, keepdims=True)
        dout = err * (1.0 / D)
        ggf_ref[...] += jnp.sum(dout * nrm, axis=0, keepdims=True)
        dn = dout * gfv
        dx2 = r * (dn - nrm * jnp.mean(dn * nrm, axis=-1, keepdims=True))
        dx_ref[...] = dx2
        dxb_ref[...] = dx2.astype(BF16)

    row = lambda w: pl.BlockSpec((tm, w), lambda i: (i, 0))
    vec = pl.BlockSpec((1, D), lambda i: (0, 0))
    return pl.pallas_call(
        body, name="outproj_loss", grid=(S // tm,),
        in_specs=[row(D), row(E), row(E), pl.BlockSpec((2 * E, D), lambda i: (0, 0)), vec, row(D)],
        out_specs=(row(D), row(D), vec, vec),
        out_shape=(jax.ShapeDtypeStruct((S, D), F32), jax.ShapeDtypeStruct((S, D), BF16),
                   jax.ShapeDtypeStruct((1, D), F32), jax.ShapeDtypeStruct((1, D), F32)),
        compiler_params=_params(("arbitrary",)),
    )(x, y_att, y_conv, w_out, gf, target)


def _split3(v):
    hi = v.astype(BF16)
    r1 = v - hi.astype(F32)
    mid = r1.astype(BF16)
    lo = (r1 - mid.astype(F32)).astype(BF16)
    return hi, mid, lo


def _dy_att(dxb, w_out, gates, o, tm=512):
    S, D = dxb.shape
    E = ATT_W

    def body(dx_ref, w_ref, a_ref, o_ref, do_ref, da_ref, dl_ref, dxr_ref):
        dxr = _perm_rows(dx_ref[...], False)
        dxr_ref[...] = dxr
        dya = _nt(dxr, w_ref[...])
        a = a_ref[...]
        ov = o_ref[...]
        sl, dsl = _silu_and_grad(a)
        d_o = dya * sl
        do_ref[...] = d_o
        da_ref[...] = (dya * ov * dsl).astype(BF16)
        ci = lax.broadcasted_iota(jnp.int32, (E, LANES), 0) // HEAD_DIM
        hi = lax.broadcasted_iota(jnp.int32, (E, LANES), 1)
        sel = jnp.where(ci == hi, 1.0, 0.0).astype(BF16)
        acc = jnp.zeros((tm, LANES), F32)
        for part in _split3(d_o * ov):
            acc = acc + jnp.dot(part, sel, preferred_element_type=F32)
        dl_ref[...] = acc

    row = lambda w: pl.BlockSpec((tm, w), lambda i: (i, 0))
    return pl.pallas_call(
        body, name="dy_att", grid=(S // tm,),
        in_specs=[row(D), pl.BlockSpec((E, D), lambda i: (0, 0)), row(E), row(E)],
        out_specs=(row(E), row(E), row(LANES), row(D)),
        out_shape=(jax.ShapeDtypeStruct((S, E), F32), jax.ShapeDtypeStruct((S, E), BF16),
                   jax.ShapeDtypeStruct((S, LANES), F32), jax.ShapeDtypeStruct((S, D), BF16)),
        compiler_params=_params(("parallel",)),
    )(dxb, w_out, gates, o)


def _dy_conv(dxb, w_out, gates, conv_out, ln_g, ln_b, tm=512):
    S, D = dxb.shape
    C = conv_out.shape[1]

    def body(dx_ref, w_ref, gate_ref, cv_ref, g_ref, beta_ref, dgate_ref, dconv_ref, gg_ref, gb_ref, gcb_ref):
        @pl.when(pl.program_id(0) == 0)
        def _():
            gg_ref[...] = jnp.zeros_like(gg_ref)
            gb_ref[...] = jnp.zeros_like(gb_ref)
            gcb_ref[...] = jnp.zeros_like(gcb_ref)

        dyc = _nt(dx_ref[...], w_ref[...])
        cv = cv_ref[...]
        mu = jnp.mean(cv, axis=-1, keepdims=True)
        xc = cv - mu
        rstd = lax.rsqrt(jnp.mean(xc * xc, axis=-1, keepdims=True) + LN_EPS)
        nrm = xc * rstd
        gv = g_ref[...]
        ln = nrm * gv + beta_ref[...]
        u, du = _silu_and_grad(ln)
        gt = gate_ref[...]
        g2, dg2 = _silu_and_grad(gt)
        dgate_ref[...] = (dyc * u * dg2).astype(BF16)
        d_ln = dyc * g2 * du
        gb_ref[...] += jnp.sum(d_ln, axis=0, keepdims=True)
        gg_ref[...] += jnp.sum(d_ln * nrm, axis=0, keepdims=True)
        dn = d_ln * gv
        d_conv = rstd * (dn - jnp.mean(dn, axis=-1, keepdims=True)
                         - nrm * jnp.mean(dn * nrm, axis=-1, keepdims=True))
        dconv_ref[...] = d_conv
        gcb_ref[...] += jnp.sum(d_conv, axis=0, keepdims=True)

    row = lambda w: pl.BlockSpec((tm, w), lambda i: (i, 0))
    vec = pl.BlockSpec((1, C), lambda i: (0, 0))
    return pl.pallas_call(
        body, name="dy_conv", grid=(S // tm,),
        in_specs=[row(D), pl.BlockSpec((C, D), lambda i: (1, 0)),
                  pl.BlockSpec((tm, C), lambda i: (i, 2)), row(C), vec, vec],
        out_specs=(row(C), row(C), vec, vec, vec),
        out_shape=(jax.ShapeDtypeStruct((S, C), BF16), jax.ShapeDtypeStruct((S, C), F32),
                   jax.ShapeDtypeStruct((1, C), F32), jax.ShapeDtypeStruct((1, C), F32),
                   jax.ShapeDtypeStruct((1, C), F32)),
        compiler_params=_params(("arbitrary",)),
    )(dxb, w_out, gates, conv_out, ln_g, ln_b)


def _conv_bwd(d_conv, gates, d_c_gate, conv_w, hosted=None, tt=256):
    S, C = d_conv.shape
    hb = tt // CONV_HALO
    nt = S // tt
    hn = hosted.n if hosted is not None else 0

    def body(*refs):
        dc_ref, dnext_ref, val_ref, glu_ref, dg_ref, w_ref = refs[:6]
        h_ins = refs[6:6 + hn]
        out_ref, gw_ref = refs[6 + hn:8 + hn]
        h_outs = refs[8 + hn:8 + 2 * hn]
        hbuf, dbuf, dhbuf, dph, wb = refs[8 + 2 * hn:13 + 2 * hn]
        h_sems = refs[13 + 2 * hn:]
        i = pl.program_id(0)

        @pl.when(i == 0)
        def _():
            gw_ref[...] = jnp.zeros_like(gw_ref)
            _broadcast_taps(w_ref, wb)
            if hosted is not None:
                hosted.start(h_ins, h_outs, h_sems)

        val = val_ref[...]
        sg = _sigmoid(glu_ref[...])
        hbuf[...] = val * sg
        dbuf[0:tt, :] = dc_ref[...]
        dbuf[tt:, :] = jnp.where(i < nt - 1, dnext_ref[...], 0.0)
        _shifted_copies(dbuf, dph)
        for cb in range(C // LANES):
            cols = slice(cb * LANES, (cb + 1) * LANES)
            gacc = [jnp.zeros((8, LANES), F32) for _ in range(CONV_K)]
            group = 2
            for rc0 in range(0, tt // 8, group):
                hcur = [hbuf[(rc0 + r) * 8:(rc0 + r + 1) * 8, cols] for r in range(group)]
                accs = [jnp.zeros((8, LANES), F32) for _ in range(group)]
                for j in range(CONV_K):
                    wj = wb[j, :, cols]
                    for r in range(group):
                        dwin = _window(dbuf, dph, (rc0 + r) * 8 + (CONV_K - 1) - j, cols)
                        accs[r] = accs[r] + dwin * wj
                        gacc[j] = gacc[j] + dwin * hcur[r]
                for r in range(group):
                    dhbuf[(rc0 + r) * 8:(rc0 + r + 1) * 8, cols] = accs[r]
            for j in range(CONV_K):
                gw_ref[j:j + 1, cols] += jnp.sum(gacc[j], axis=0, keepdims=True)
        d_h = dhbuf[...]
        out_ref[:, 0:C] = (d_h * sg).astype(BF16)
        out_ref[:, C:2 * C] = (d_h * val * sg * (1.0 - sg)).astype(BF16)
        out_ref[:, 2 * C:3 * C] = dg_ref[...]

        if hosted is not None:
            @pl.when(i == nt - 1)
            def _():
                hosted.finish(h_ins, h_outs, h_sems)

    tile = lambda col: pl.BlockSpec((tt, C), lambda i: (i, col))
    in_specs = [tile(0),
                pl.BlockSpec((CONV_HALO, C), lambda i: (jnp.minimum((i + 1) * hb, S // CONV_HALO - 1), 0)),
                tile(0), tile(1), tile(0),
                pl.BlockSpec((CONV_HALO, C), lambda i: (0, 0))]
    args = [d_conv, d_conv, gates, gates, d_c_gate, conv_w]
    out_specs = [pl.BlockSpec((tt, 3 * C), lambda i: (i, 0)), pl.BlockSpec((CONV_HALO, C), lambda i: (0, 0))]
    out_shape = [jax.ShapeDtypeStruct((S, 3 * C), BF16), jax.ShapeDtypeStruct((CONV_HALO, C), F32)]
    scratch = [pltpu.VMEM((tt, C), F32), pltpu.VMEM((tt + CONV_HALO, C), F32), pltpu.VMEM((tt, C), F32),
               pltpu.VMEM((7, tt + CONV_HALO - 8, C), F32), pltpu.VMEM((CONV_K, 8, C), F32)]
    if hosted is not None:
        in_specs += [ANY_SPEC] * hn
        args += hosted.arrays
        out_specs += [ANY_SPEC] * hn
        out_shape += hosted.out_shapes()
        scratch += hosted.sem_shapes()
    res = pl.pallas_call(
        body, name="conv_bwd", grid=(nt,),
        in_specs=in_specs, out_specs=tuple(out_specs), out_shape=tuple(out_shape), scratch_shapes=scratch,
        compiler_params=_params(("arbitrary",)),
    )(*args)
    return res[0], res[1], list(res[2:])


def _attn_bwd(q, kv, d_o, lse, delta, dil, prev, final, name, hosted=None):
    S = q.shape[0]
    rows = _Rows(dil, S)
    nb = rows.nb
    steps = dil * nb
    out_dt = BF16 if final else F32
    have_prev = prev is not None
    hn = hosted.n if hosted is not None else 0

    def body(*refs):
        refs = list(refs)
        q_ref, do_ref, lse_ref, dl_ref, kvc_ref, kvp_ref = refs[:6]
        del refs[:6]
        if have_prev:
            pdq_ref, pdkv_ref = refs[:2]
            del refs[:2]
        h_ins = refs[:hn]
        dq_ref, dkv_ref = refs[hn:hn + 2]
        h_outs = refs[hn + 2:2 * hn + 2]
        carry, tbl = refs[2 * hn + 2:2 * hn + 4]
        h_sems = refs[2 * hn + 4:]
        t = pl.program_id(0)
        n = t % nb

        @pl.when(t == 0)
        def _():
            if hosted is not None:
                hosted.start(h_ins, h_outs, h_sems)
            _fill_bias_table(tbl, rows, keys_first=True)
            carry[...] = jnp.zeros_like(carry)

        @pl.when(t < steps)
        def _():
            kv2 = jnp.concatenate([_ld(kvp_ref), _ld(kvc_ref)], axis=0)
            lse_t, dl_t = _ld(lse_ref).T, _ld(dl_ref).T
            lo_mask = lax.broadcasted_iota(jnp.int32, (2 * BLK, LANES), 1) < HEAD_DIM
            halves = [jnp.zeros((2 * BLK, LANES), F32) for _ in range(4)]
            for hk in range(N_KV_HEADS):
                k_lo, k_hi, v_lo, v_hi = _head_operands(kv2, hk, lo_mask)
                cols = [slice(b * LANES, (b + 1) * LANES) for b in (2 * hk, 2 * hk + 1)]
                q2 = jnp.concatenate([_ld(q_ref, cols[0]), _ld(q_ref, cols[1])], axis=0).astype(BF16)
                do2 = jnp.concatenate([_ld(do_ref, cols[0]), _ld(do_ref, cols[1])], axis=0).astype(BF16)
                dq2 = jnp.zeros((2 * BLK, LANES), F32)
                dks, dvs = [], []
                for which, (kk, vv) in enumerate(((k_lo, v_lo), (k_hi, v_hi))):
                    h0, h1 = 4 * hk + which, 4 * hk + 2 + which
                    s = _nt(kk, q2) + _bias2(tbl, n, h0, h1, axis=1)
                    lse2 = jnp.concatenate([lse_t[h0:h0 + 1, :], lse_t[h1:h1 + 1, :]], axis=1)
                    dl2 = jnp.concatenate([dl_t[h0:h0 + 1, :], dl_t[h1:h1 + 1, :]], axis=1)
                    p = jnp.exp(s - lse2)
                    ds = (p * (_nt(vv, do2) - dl2)).astype(BF16)
                    dq2 = dq2 + _tn(ds, kk)
                    dks.append(jnp.dot(ds, q2, preferred_element_type=F32))
                    dvs.append(jnp.dot(p.astype(BF16), do2, preferred_element_type=F32))
                dk_sum = jnp.where(lo_mask, dks[0], dks[1])
                dv_sum = jnp.where(lo_mask, dvs[0], dvs[1])
                for jp in range(2):
                    dq_blk = dq2[jp * BLK:(jp + 1) * BLK]
                    if have_prev:
                        dq_blk = dq_blk + _ld(pdq_ref, cols[jp])
                    if final:
                        dq_blk = dq_blk * (HEAD_DIM ** -0.5)
                    _st(dq_ref, dq_blk.astype(out_dt), cols[jp])
                half, pos = hk // 2, hk % 2
                here = lo_mask if pos == 0 else jnp.logical_not(lo_mask)
                dk_tot = dk_sum + pltpu.roll(dk_sum, HEAD_DIM, axis=1)
                dv_tot = dv_sum + pltpu.roll(dv_sum, HEAD_DIM, axis=1)
                halves[half] = halves[half] + jnp.where(here, dk_tot, 0.0)
                halves[2 + half] = halves[2 + half] + jnp.where(here, dv_tot, 0.0)
            for b in range(4):
                cols = slice(b * LANES, (b + 1) * LANES)
                done = carry[:, cols] + halves[b][0:BLK, :]
                if have_prev:
                    done = done + _ld(pdkv_ref, cols)
                _st(dkv_ref, done.astype(out_dt), cols)
                carry[:, cols] = halves[b][BLK:, :]

        @pl.when(t == steps)
        def _():
            done = carry[...]
            if have_prev:
                done = done + _ld(pdkv_ref)
            _st(dkv_ref, done.astype(out_dt))
            if hosted is not None:
                hosted.finish(h_ins, h_outs, h_sems)

    def spec(width, lag):
        def index(t):
            u = jnp.clip(t - lag, 0, steps - 1)
            return rows.index(u // nb, u % nb)
        return pl.BlockSpec(rows.block + (width,), index)

    def key_prev(t):
        u = jnp.minimum(t, steps - 1)
        return rows.index(u // nb, jnp.maximum(u % nb - 1, 0))

    in_specs = [spec(ATT_W, 0), spec(ATT_W, 0), spec(LANES, 0), spec(LANES, 0), spec(2 * KV_W, 0),
                pl.BlockSpec(rows.block + (2 * KV_W,), key_prev)]
    args = [rows.of(q), rows.of(d_o), rows.of(lse), rows.of(delta), rows.of(kv), rows.of(kv)]
    if have_prev:
        in_specs += [spec(ATT_W, 0), spec(2 * KV_W, 1)]
        args += [rows.of(prev[0]), rows.of(prev[1])]
    out_specs = [spec(ATT_W, 0), spec(2 * KV_W, 1)]
    out_shape = [jax.ShapeDtypeStruct(rows.view + (ATT_W,), out_dt),
                 jax.ShapeDtypeStruct(rows.view + (2 * KV_W,), out_dt)]
    scratch = [pltpu.VMEM((BLK, 2 * KV_W), F32), pltpu.VMEM((2 * N_Q_HEADS, 2 * BLK, BLK), F32)]
    if hosted is not None:
        in_specs += [ANY_SPEC] * hn
        args += hosted.arrays
        out_specs += [ANY_SPEC] * hn
        out_shape += hosted.out_shapes()
        scratch += hosted.sem_shapes()
    res = pl.pallas_call(
        body, name=name, grid=(steps + 1,),
        in_specs=in_specs, out_specs=tuple(out_specs), out_shape=tuple(out_shape), scratch_shapes=scratch,
        compiler_params=_params(("arbitrary",)),
    )(*args)
    return (res[0].reshape(S, ATT_W), res[1].reshape(S, 2 * KV_W)), list(res[2:])


def _dh(segments, w_in, x, dx2, g, hosted=None, tm=1024, tk=512):
    S, D = x.shape
    ns = len(segments)
    counts = [a.shape[1] // tk for a, _ in segments]
    starts = [sum(counts[:s]) for s in range(ns)]
    nk = sum(counts)
    hn = hosted.n if hosted is not None else 0

    ni = S // tm
    steps = nk * ni

    def body(*refs):
        seg_refs = refs[:ns]
        w_ref, x_ref, dx2_ref, g_ref = refs[ns:ns + 4]
        h_ins = refs[ns + 4:ns + 4 + hn]
        gx_ref, gng_ref = refs[ns + 4 + hn:ns + 6 + hn]
        h_outs = refs[ns + 6 + hn:ns + 6 + 2 * hn]
        acc, tbuf, wbuf, tsem, wsem = refs[ns + 6 + 2 * hn:ns + 11 + 2 * hn]
        h_sems = refs[ns + 11 + 2 * hn:]
        k, i = pl.program_id(0), pl.program_id(1)
        step = k * ni + i

        def tile_copy(at, start):
            kk, ii = at // ni, at % ni
            for s in range(ns):
                @pl.when((kk >= starts[s]) & (kk < starts[s] + counts[s]))
                def _(s=s):
                    cp = pltpu.make_async_copy(
                        seg_refs[s].at[pl.ds(ii * tm, tm), pl.ds((kk - starts[s]) * tk, tk)],
                        tbuf.at[at % 3], tsem.at[at % 3])
                    cp.start() if start else cp.wait()

        def weight_copy(kk, start):
            cp = pltpu.make_async_copy(w_ref.at[pl.ds(kk * tk, tk), :], wbuf.at[kk % 2], wsem.at[kk % 2])
            cp.start() if start else cp.wait()

        @pl.when(step == 0)
        def _():
            gng_ref[...] = jnp.zeros_like(gng_ref)
            if hosted is not None:
                hosted.start(h_ins, h_outs, h_sems)
            weight_copy(0, True)
            tile_copy(0, True)
            tile_copy(1, True)

        pl.when(step + 2 < steps)(lambda: tile_copy(step + 2, True))
        pl.when((i == 0) & (k + 1 < nk))(lambda: weight_copy(k + 1, True))
        pl.when(i == 0)(lambda: weight_copy(k, False))
        tile_copy(step, False)

        @pl.when(k == 0)
        def _():
            acc[i] = jnp.zeros(acc.shape[1:], F32)

        for s in range(ns):
            @pl.when((k >= starts[s]) & (k < starts[s] + counts[s]))
            def _(s=s):
                t = tbuf[step % 3]
                if segments[s][1]:
                    t = _perm_rows(t, True)
                acc[i] += jnp.dot(t, wbuf[k % 2], preferred_element_type=F32)

        @pl.when(k == nk - 1)
        def _():
            dh = acc[i]
            xf = x_ref[...]
            r = lax.rsqrt(jnp.mean(xf * xf, axis=-1, keepdims=True) + NORM_EPS)
            nrm = xf * r
            gng_ref[...] += jnp.sum(dh * nrm, axis=0, keepdims=True)
            dn = dh * g_ref[...]
            gx_ref[...] = dx2_ref[...] + r * (dn - nrm * jnp.mean(dn * nrm, axis=-1, keepdims=True))

        if hosted is not None:
            @pl.when((i == S // tm - 1) & (k == nk - 1))
            def _():
                hosted.finish(h_ins, h_outs, h_sems)

    row = pl.BlockSpec((tm, D), lambda k, i: (jnp.where(k == nk - 1, i, 0), 0))
    vec = pl.BlockSpec((1, D), lambda k, i: (0, 0))
    in_specs = [ANY_SPEC] * (ns + 1) + [row, row, vec]
    args = [a for a, _ in segments] + [w_in, x, dx2, g]
    out_specs = [row, vec]
    out_shape = [jax.ShapeDtypeStruct((S, D), F32), jax.ShapeDtypeStruct((1, D), F32)]
    scratch = [pltpu.VMEM((ni, tm, D), F32), pltpu.VMEM((3, tm, tk), BF16), pltpu.VMEM((2, tk, D), BF16),
               pltpu.SemaphoreType.DMA((3,)), pltpu.SemaphoreType.DMA((2,))]
    if hosted is not None:
        in_specs += [ANY_SPEC] * hn
        args += hosted.arrays
        out_specs += [ANY_SPEC] * hn
        out_shape += hosted.out_shapes()
        scratch += hosted.sem_shapes()
    res = pl.pallas_call(
        body, name="dh", grid=(nk, S // tm),
        in_specs=in_specs, out_specs=tuple(out_specs), out_shape=tuple(out_shape), scratch_shapes=scratch,
        compiler_params=_params(("arbitrary", "arbitrary"), BIG_VMEM_LIMIT),
    )(*args)
    return res[0], res[1], list(res[2:])


def _tn_matmul(a, bs, name, b_first=False, tm=512):
    M, K = a.shape
    nb = len(bs)
    shapes = [(b.shape[1], K) if b_first else (K, b.shape[1]) for b in bs]

    def body(a_ref, *refs):
        @pl.when(pl.program_id(0) == 0)
        def _():
            for o_ref in refs[nb:]:
                o_ref[...] = jnp.zeros_like(o_ref)

        at = a_ref[...]
        for b_ref, o_ref in zip(refs[:nb], refs[nb:]):
            for c in range(0, b_ref.shape[1], 512):
                if b_first:
                    o_ref[c:c + 512, :] += _tn(b_ref[:, c:c + 512], at)
                else:
                    o_ref[:, c:c + 512] += _tn(at, b_ref[:, c:c + 512])

    return pl.pallas_call(
        body, name=name, grid=(M // tm,),
        in_specs=[pl.BlockSpec((tm, K), lambda m: (m, 0))] + [pl.BlockSpec((tm, b.shape[1]), lambda m: (m, 0))
                                                              for b in bs],
        out_specs=tuple(pl.BlockSpec(s, lambda m: (0, 0)) for s in shapes),
        out_shape=tuple(jax.ShapeDtypeStruct(s, F32) for s in shapes),
        compiler_params=_params(("arbitrary",)),
    )(a, *bs)


def _adamw(parts, w, m, v, name, tr=None, split=None, by_chip=False):
    R, C = w.shape
    tr = R if tr is None else tr
    parts = [parts] if split is None else list(parts)
    npar = len(parts)

    def total(p_ref):
        if by_chip:
            c = lax.axis_index("c")
            g = p_ref[c].astype(F32)
            for chip in range(1, N_DEV // 2):
                g = g + p_ref[2 * chip + c].astype(F32)
            return g
        g = p_ref[0].astype(F32)
        for dev in range(1, N_DEV):
            g = g + p_ref[dev].astype(F32)
        return g

    def body(*refs):
        w_ref, m_ref, v_ref, g_out, d_out, m_out, v_out = refs[npar:]
        if split is None:
            g = total(refs[0])
        else:
            g = jnp.where(_mesh_pos()[3] < split, total(refs[0]), total(refs[1]))
        mn = ADAM_B1 * m_ref[...] + (1.0 - ADAM_B1) * g
        vn = ADAM_B2 * v_ref[...] + (1.0 - ADAM_B2) * (g * g)
        m_hat = mn / (1.0 - ADAM_B1 ** ADAM_STEP)
        v_hat = vn / (1.0 - ADAM_B2 ** ADAM_STEP)
        g_out[...] = g
        d_out[...] = -ADAM_LR * (m_hat / (jnp.sqrt(v_hat) + ADAM_EPS) + ADAM_WD * w_ref[...])
        m_out[...] = mn
        v_out[...] = vn

    blk = pl.BlockSpec((tr, C), lambda i: (i, 0))
    shp = jax.ShapeDtypeStruct((R, C), F32)
    return pl.pallas_call(
        body, name=name, grid=(R // tr,),
        in_specs=[pl.BlockSpec((N_DEV, tr, C), lambda i: (0, i, 0))] * npar + [blk, blk, blk],
        out_specs=(blk, blk, blk, blk), out_shape=(shp, shp, shp, shp),
        compiler_params=_params(("parallel",)),
    )(*parts, w, m, v)


def _local_step(x, target, norm_g, w_in, conv_w, conv_b, ln_g, ln_b, w_out, gf, exchanges=None, first_weights=None,
                late_weights=None):
    ex_out, ex_att, ex_conv = exchanges if exchanges is not None else (None, None, None)
    h_rm, h, *first = _norm_rows(x, norm_g, first_weights[0] if first_weights is not None else None)
    if first_weights is not None:
        w_in = first_weights[1](first)
    conv_cols = w_in.shape[0] - 2 * ATT_W - 2 * KV_W
    q, kv, a_gate, gates, *gathered = _inproj(
        h_rm, h, w_in,
        [(ATT_W, HEAD_DIM ** -0.5, True), (2 * KV_W, 1.0, True), (ATT_W, 1.0, True), (conv_cols, 1.0, False)],
        late_weights[0] if late_weights is not None else None)
    if late_weights is not None:
        conv_w, w_out = late_weights[1](gathered)

    alone = [_attn_fwd(q, kv, dil, "attn_fwd_d%d" % dil) for _, dil in PATTERNS[1:]]
    o, lse, y_att = _attn_fwd(q, kv, PATTERNS[0][1], "attn_fwd_d%d" % PATTERNS[0][1], alone, a_gate)
    conv_out, y_conv = _conv_fwd(gates, conv_w, conv_b, ln_g, ln_b)
    dx2, dxb, loss_cols, g_gf = _outproj_loss(x, y_att, y_conv, w_out, gf, target)

    d_o, d_a_gate, delta, dxb_rm = _dy_att(dxb, w_out, a_gate, o)
    g_w_out = jnp.concatenate([_tn_matmul(y_att, [dxb_rm], "gw_out_att")[0],
                               _tn_matmul(y_conv, [dxb], "gw_out_conv")[0]], axis=0)
    acc, out_parts = None, []
    for idx, (_, dil) in enumerate(reversed(PATTERNS)):
        hosted = ex_out(g_w_out) if (idx == 0 and ex_out is not None) else None
        acc, outs = _attn_bwd(q, kv, d_o, lse, delta, dil, acc, idx == len(PATTERNS) - 1, "attn_bwd_d%d" % dil,
                              hosted)
        out_parts += outs
    dq, dkv = acc
    g_q, g_kv, g_a = _tn_matmul(h_rm, [dq, dkv, d_a_gate], "gw_in_att", b_first=True)

    d_c_gate, d_conv, g_ln_g, g_ln_b, g_conv_b = _dy_conv(dxb, w_out, gates, conv_out, ln_g, ln_b)
    dgates, g_conv_w, att_parts = _conv_bwd(d_conv, gates, d_c_gate, conv_w,
                                            ex_att(g_q, g_kv, g_a) if ex_att is not None else None)
    g_c, = _tn_matmul(h, [dgates], "gw_in_conv", b_first=True)
    grad_x, g_norm_g, conv_parts = _dh(
        [(dq, True), (dkv, True), (d_a_gate, True), (dgates, False)], w_in, x, dx2, norm_g,
        ex_conv(g_a, g_c, g_conv_w) if ex_conv is not None else None)
    small = (g_norm_g, g_conv_b, g_ln_g, g_ln_b, g_gf, loss_cols)
    return grad_x, (g_q, g_kv, g_a, g_c), g_w_out, g_conv_w, small, (out_parts, att_parts, conv_parts)


def kernel(x, norm_g, w_in, conv_w, conv_b, conv_ln_g, conv_ln_b, w_out, final_norm_g, loss_target, m_norm_g, m_w_in, m_conv_w, m_conv_b, m_conv_ln_g, m_conv_ln_b, m_w_out, m_final_norm_g, v_norm_g, v_w_in, v_conv_w, v_conv_b, v_conv_ln_g, v_conv_ln_b, v_w_out, v_final_norm_g):
    S, D = x.shape[1], x.shape[2]
    win_sh, wout_sh, cw_sh = w_in[0].T, w_out[0], conv_w[0]
    cols_sh, rows_sh, ch_sh = win_sh.shape[0], wout_sh.shape[0], cw_sh.shape[1]

    def first_weights(gathered):
        return gathered[0].reshape(N_DEV * cols_sh, D)

    def late_weights(gathered):
        wout_all, cw_all = gathered
        conv_w_full = cw_all.transpose(1, 0, 2).reshape(CONV_K, N_DEV * ch_sh)
        return jnp.pad(conv_w_full, ((0, CONV_HALO - CONV_K), (0, 0))), wout_all.reshape(N_DEV * rows_sh, D)

    gf = final_norm_g.reshape(1, D)

    first = -(-(ATT_W + 2 * KV_W) // cols_sh)
    a_off = first * cols_sh - (ATT_W + 2 * KV_W)
    assert 0 <= a_off <= ATT_W

    def pieces(parts, n):
        return jnp.concatenate([p.astype(BF16) for p in parts], axis=0).reshape(n, cols_sh, D)

    def ex_out(g_w_out):
        return _Exchange([g_w_out.reshape(N_DEV, rows_sh, D).astype(BF16)], [(0, N_DEV)])

    same_core = (2, 4, 6)

    def ex_att(g_q, g_kv, g_a):
        mine = _chip_sum(pieces([g_q, g_kv, g_a[:a_off]], first), 0, "rs_att")
        return _Exchange([mine], [(0, first)], [same_core])

    def ex_conv(g_a, g_c, g_conv_w):
        mine = _chip_sum(pieces([g_a[a_off:], g_c], N_DEV - first), first, "rs_conv")
        return _Exchange(
            [mine, g_conv_w[:CONV_K].reshape(CONV_K, N_DEV, ch_sh).transpose(1, 0, 2)],
            [(first, N_DEV), (0, N_DEV)], [same_core, None])

    grad_x, _, _, _, small, parts = _local_step(
        x[0], loss_target[0], norm_g, None, None, conv_b, conv_ln_g, conv_ln_b, None, gf,
        (ex_out, ex_att, ex_conv), (_Gather([win_sh.astype(BF16)]), first_weights),
        (_Gather([wout_sh.astype(BF16), cw_sh]), late_weights))
    (wout_parts,), (win_parts_lo,), (win_parts_hi, cw_parts) = parts

    small_pack = jnp.concatenate(list(small) + [jnp.zeros((2, D), F32)], axis=0)
    small_parts, = _exchange(_Exchange([small_pack], [None]), "gather_small")

    upd_win = _adamw((win_parts_lo, win_parts_hi), win_sh, m_w_in[0].T, v_w_in[0].T, "adamw_w_in",
                     tr=cols_sh // 2, split=first, by_chip=True)
    upd_wout = _adamw(wout_parts, wout_sh, m_w_out[0], v_w_out[0], "adamw_w_out", tr=128)
    upd_cw = _adamw(cw_parts, cw_sh, m_conv_w[0], v_conv_w[0], "adamw_conv_w")
    zeros3 = jnp.zeros((3, D), F32)
    stack = lambda a, b, c, d_, e: jnp.concatenate([a, b, c, d_, e.reshape(1, D), zeros3], axis=0)
    upd_small = _adamw(
        small_parts,
        stack(norm_g, conv_b, conv_ln_g, conv_ln_b, final_norm_g),
        stack(m_norm_g, m_conv_b, m_conv_ln_g, m_conv_ln_b, m_final_norm_g),
        stack(v_norm_g, v_conv_b, v_conv_ln_g, v_conv_ln_b, v_final_norm_g) + jnp.concatenate(
            [jnp.zeros((5, D), F32), jnp.ones((3, D), F32)], axis=0),
        "adamw_small")

    loss = 0.5 / D * jnp.sum(upd_small[0][5])

    def outputs(kind):
        sm = upd_small[kind]
        return [sm[0:1], upd_win[kind].T[None], upd_cw[kind][None], sm[1:2], sm[2:3], sm[3:4],
                upd_wout[kind][None], sm[4]]

    return (loss, grad_x[None], *outputs(0), *outputs(1), *outputs(2), *outputs(3))
```

```python
import jax
import jax.numpy as jnp
from jax import lax
from jax.experimental import pallas as pl
from jax.experimental.pallas import tpu as pltpu

F32 = jnp.float32
BF16 = jnp.bfloat16

HEAD_DIM = 64
N_KV_HEADS = 4
N_Q_HEADS = 16
ATT_W = 1024
KV_W = 256
CONV_K = 31
CONV_HALO = 32
PATTERNS = ((128, 1), (512, 4), (2048, 16))
BLK = 128
LANES = 128
NORM_EPS = 1e-6
LN_EPS = 1e-5
NEG = -1e30
N_DEV = 8
ADAM_LR, ADAM_B1, ADAM_B2, ADAM_EPS, ADAM_WD, ADAM_STEP = 0.001, 0.9, 0.999, 1e-08, 0.01, 10
VMEM_LIMIT = 48 * 1024 * 1024
BIG_VMEM_LIMIT = 58 * 1024 * 1024
SLOPES = tuple(2.0 ** (-8.0 * (h + 1) / N_Q_HEADS) for h in range(N_Q_HEADS))
MESH = pl.DeviceIdType.MESH


def _params(sem, vmem_limit=VMEM_LIMIT):
    return pltpu.CompilerParams(dimension_semantics=sem, vmem_limit_bytes=vmem_limit)


def _sigmoid(v):
    return 1.0 / (1.0 + jnp.exp(-v))


def _silu_and_grad(v):
    s = _sigmoid(v)
    return v * s, s * (1.0 + v * (1.0 - s))


ANY_SPEC = pl.BlockSpec(memory_space=pl.ANY)


def _mesh_pos():
    x, y, c = lax.axis_index("x"), lax.axis_index("y"), lax.axis_index("c")
    return x, y, c, 4 * x + 2 * y + c


def _flipped(k, x, y, c):
    px = 1 - x if k & 4 else x
    py = 1 - y if k & 2 else y
    pc = 1 - c if k & 1 else c
    return (px, py, pc), 4 * px + 2 * py + pc


class _Exchange:
    def __init__(self, arrays, dests, flips=None):
        self.arrays, self.dests, self.n = list(arrays), list(dests), len(arrays)
        self.flips = [tuple(range(1, N_DEV)) if f is None else tuple(f)
                      for f in (flips if flips is not None else [None] * self.n)]

    def out_shapes(self):
        return [jax.ShapeDtypeStruct((N_DEV,) + a.shape[-2:], a.dtype) for a in self.arrays]

    def sem_shapes(self):
        return [pltpu.SemaphoreType.DMA((self.n, N_DEV - 1)), pltpu.SemaphoreType.DMA((self.n, N_DEV - 1)),
                pltpu.SemaphoreType.DMA((self.n,))]

    def _when(self, a, dev, fn):
        if self.dests[a] is None:
            fn()
        else:
            lo, hi = self.dests[a]
            pl.when((dev >= lo) & (dev < hi))(fn)

    def _mine(self, ins, a, dev):
        return ins[a] if self.dests[a] is None else ins[a].at[dev - self.dests[a][0]]

    def _copy(self, ins, outs, sems, a, k, src_dev, slot, target):
        return pltpu.make_async_remote_copy(
            src_ref=self._mine(ins, a, src_dev), dst_ref=outs[a].at[slot],
            send_sem=sems[0].at[a, k - 1], recv_sem=sems[1].at[a, k - 1],
            device_id=target, device_id_type=MESH)

    def start(self, ins, outs, sems):
        x, y, c, me = _mesh_pos()
        for a in range(self.n):
            self._when(a, me, lambda a=a: pltpu.make_async_copy(
                self._mine(ins, a, me), outs[a].at[me], sems[2].at[a]).start())
            for k in self.flips[a]:
                target, peer = _flipped(k, x, y, c)
                self._when(a, peer, lambda a=a, k=k, target=target, peer=peer: self._copy(
                    ins, outs, sems, a, k, peer, me, target).start())

    def finish(self, ins, outs, sems):
        x, y, c, me = _mesh_pos()
        lo0 = [0 if d is None else d[0] for d in self.dests]
        for a in range(self.n):
            for k in self.flips[a]:
                target, peer = _flipped(k, x, y, c)
                self._when(a, me, lambda a=a, k=k, peer=peer: self._copy(
                    ins, outs, sems, a, k, lo0[a], peer, (x, y, c)).wait_recv())
            for k in self.flips[a]:
                target, peer = _flipped(k, x, y, c)
                self._when(a, peer, lambda a=a, k=k, target=target, peer=peer: self._copy(
                    ins, outs, sems, a, k, peer, me, target).wait_send())
            self._when(a, me, lambda a=a: pltpu.make_async_copy(
                self._mine(ins, a, me), outs[a].at[me], sems[2].at[a]).wait())


def _exchange(ex, name):
    na = ex.n

    def body(*refs):
        ins, outs, sems = refs[:na], refs[na:2 * na], refs[2 * na:]
        ex.start(ins, outs, sems)
        ex.finish(ins, outs, sems)

    return pl.pallas_call(
        body, name=name, out_shape=tuple(ex.out_shapes()),
        in_specs=[ANY_SPEC] * na, out_specs=tuple([ANY_SPEC] * na), scratch_shapes=ex.sem_shapes(),
    )(*ex.arrays)


def _chip_sum(pieces, lo, name):
    n, R, C = pieces.shape

    def swap(p_ref, t_ref, send_sems, recv_sems):
        x, y, c, me = _mesh_pos()
        for i in range(n):
            mine = (lo + i) % 2
            cp = pltpu.make_async_remote_copy(
                src_ref=p_ref.at[i], dst_ref=t_ref.at[i], send_sem=send_sems.at[i], recv_sem=recv_sems.at[i],
                device_id=(x, y, 1 - c), device_id_type=MESH)
            pl.when(c != mine)(cp.start)
        for i in range(n):
            mine = (lo + i) % 2
            cp = pltpu.make_async_remote_copy(
                src_ref=p_ref.at[i], dst_ref=t_ref.at[i], send_sem=send_sems.at[i], recv_sem=recv_sems.at[i],
                device_id=(x, y, 1 - c), device_id_type=MESH)
            pl.when(c == mine)(cp.wait_recv)
            pl.when(c != mine)(cp.wait_send)

    other = pl.pallas_call(
        swap, name=name + "_swap", out_shape=jax.ShapeDtypeStruct(pieces.shape, pieces.dtype),
        in_specs=[ANY_SPEC], out_specs=ANY_SPEC,
        scratch_shapes=[pltpu.SemaphoreType.DMA((n,)), pltpu.SemaphoreType.DMA((n,))],
    )(pieces)

    def add(p_ref, t_ref, o_ref):
        o_ref[...] = (p_ref[...].astype(F32) + t_ref[...].astype(F32)).astype(o_ref.dtype)

    tr = R // 2
    blk = pl.BlockSpec((None, tr, C), lambda i, r: (i, r, 0))
    return pl.pallas_call(
        add, name=name + "_add", grid=(n, R // tr), in_specs=[blk, blk], out_specs=blk,
        out_shape=jax.ShapeDtypeStruct(pieces.shape, pieces.dtype),
        compiler_params=_params(("parallel", "parallel")),
    )(pieces, other)


class _Gather:
    def __init__(self, arrays):
        self.arrays, self.n = list(arrays), len(arrays)

    def out_shapes(self):
        return [jax.ShapeDtypeStruct((N_DEV,) + a.shape, a.dtype) for a in self.arrays]

    def sem_shapes(self):
        return [pltpu.SemaphoreType.DMA((self.n, N_DEV - 1)), pltpu.SemaphoreType.DMA((self.n, N_DEV - 1)),
                pltpu.SemaphoreType.DMA((self.n,))]

    def _plan(self, ins, outs, sems):
        x, y, c, me = _mesh_pos()
        chips = [(1 - x, y), (x, 1 - y), (1 - x, 1 - y)]

        def copy(a, k, src, block, to):
            px, py, pc = block
            return pltpu.make_async_remote_copy(
                src_ref=src, dst_ref=outs[a].at[4 * px + 2 * py + pc], send_sem=sems[0].at[a, k],
                recv_sem=sems[1].at[a, k], device_id=to, device_id_type=MESH)

        def landed(a, block):
            px, py, pc = block
            return outs[a].at[4 * px + 2 * py + pc]

        local = [pltpu.make_async_copy(ins[a], outs[a].at[me], sems[2].at[a]) for a in range(self.n)]
        first = []
        for a in range(self.n):
            first.append(copy(a, 0, ins[a], (x, y, c), (x, y, 1 - c)))
            first += [copy(a, 1 + j, ins[a], (x, y, c), (*chip, c)) for j, chip in enumerate(chips[:2])]
        return (x, y, c), chips, copy, landed, local, first

    def start(self, ins, outs, sems):
        *_, local, first = self._plan(ins, outs, sems)
        for cp in local + first:
            cp.start()

    def finish(self, ins, outs, sems):
        (x, y, c), chips, copy, landed, local, first = self._plan(ins, outs, sems)
        south = c == 0
        came = (jnp.where(south, 1 - x, x), jnp.where(south, y, 1 - y), c)
        goes = (jnp.where(south, x, 1 - x), jnp.where(south, 1 - y, y), c)
        passed = []
        for a in range(self.n):
            for j, chip in enumerate(chips[:2]):
                copy(a, 1 + j, ins[a], (*chip, c), (x, y, c)).wait_recv()
            passed.append(copy(a, 3, landed(a, came), came, goes))
            passed += [copy(a, 4 + j, landed(a, (*chip, c)), (*chip, c), (x, y, 1 - c))
                       for j, chip in enumerate(chips[:2])]
        for cp in passed:
            cp.start()
        for a in range(self.n):
            diagonal = (*chips[2], c)
            copy(a, 3, ins[a], diagonal, (x, y, c)).wait_recv()
            cp = copy(a, 6, landed(a, diagonal), diagonal, (x, y, 1 - c))
            cp.start()
            passed.append(cp)
        for a in range(self.n):
            copy(a, 0, ins[a], (x, y, 1 - c), (x, y, c)).wait_recv()
            for j, chip in enumerate(chips):
                copy(a, 4 + j, ins[a], (*chip, 1 - c), (x, y, c)).wait_recv()
        for cp in first + passed:
            cp.wait_send()
        for cp in local:
            cp.wait()


CHUNK = 128
RESIDUES = 16
PER_RES = CHUNK // RESIDUES


def _perm_rows(tile, inverse):
    a = lax.broadcasted_iota(jnp.int32, (CHUNK, CHUNK), 0)
    b = lax.broadcasted_iota(jnp.int32, (CHUNK, CHUNK), 1)
    if inverse:
        a, b = b, a
    p = jnp.where(a == PER_RES * (b % RESIDUES) + b // RESIDUES, 1.0, 0.0).astype(BF16)
    parts = [jnp.dot(p, tile[c * CHUNK:(c + 1) * CHUNK], preferred_element_type=F32)
             for c in range(tile.shape[0] // CHUNK)]
    return jnp.concatenate(parts, axis=0).astype(BF16)


class _Rows:
    def __init__(self, dil, S):
        nc = S // CHUNK
        self.dil = dil
        if dil == 1:
            self.view, self.block, self.nb = (nc, CHUNK), (None, CHUNK), nc
            self.index = lambda r, b: (b, 0, 0)
        elif dil == 4:
            self.view, self.block, self.nb = (nc, 4, 4, PER_RES), (4, 4, None, PER_RES), nc // 4
            self.index = lambda r, b: (b, 0, r, 0, 0)
        elif dil == RESIDUES:
            self.view, self.block, self.nb = (nc, RESIDUES, PER_RES), (RESIDUES, None, PER_RES), nc // RESIDUES
            self.index = lambda r, b: (b, r, 0, 0)
        else:
            raise NotImplementedError(dil)

    def of(self, a):
        return a.reshape(self.view + (a.shape[-1],))

    def spec(self, width, which_block):
        return pl.BlockSpec(self.block + (width,), lambda r, n: self.index(r, which_block(n)))

    def pos(self, row):
        if self.dil == 1:
            return (row % PER_RES) * RESIDUES + row // PER_RES
        if self.dil == 4:
            return (row // 32) * 32 + (row % PER_RES) * 4 + (row % 32) // PER_RES
        return row


def _ld(ref, cols=slice(None)):
    v = ref[(slice(None),) * (len(ref.shape) - 1) + (cols,)]
    return v.reshape(BLK, v.shape[-1])


def _st(ref, val, cols=slice(None)):
    ref[(slice(None),) * (len(ref.shape) - 1) + (cols,)] = val.reshape(ref.shape[:-1] + (val.shape[-1],))


def _norm_rows(x, g, hosted=None, tm=512):
    S, D = x.shape
    hn = hosted.n if hosted is not None else 0

    def body(x_ref, g_ref, *rest):
        h_ins = rest[:hn]
        hrm_out, h_out = rest[hn:hn + 2]
        h_outs = rest[hn + 2:2 * hn + 2]
        h_sems = rest[2 * hn + 2:]
        i = pl.program_id(0)
        if hosted is not None:
            pl.when(i == 0)(lambda: hosted.start(h_ins, h_outs, h_sems))
        xf = x_ref[...]
        r = lax.rsqrt(jnp.mean(xf * xf, axis=-1, keepdims=True) + NORM_EPS)
        h = (xf * r * g_ref[...]).astype(BF16)
        h_out[...] = h
        hrm_out[...] = _perm_rows(h, False)
        if hosted is not None:
            pl.when(i == S // tm - 1)(lambda: hosted.finish(h_ins, h_outs, h_sems))

    row = pl.BlockSpec((tm, D), lambda i: (i, 0))
    in_specs, args = [row, pl.BlockSpec((1, D), lambda i: (0, 0))], [x, g]
    out_specs, out_shape, scratch = [row, row], [jax.ShapeDtypeStruct((S, D), BF16)] * 2, []
    if hosted is not None:
        in_specs += [ANY_SPEC] * hn
        args += hosted.arrays
        out_specs += [ANY_SPEC] * hn
        out_shape += hosted.out_shapes()
        scratch += hosted.sem_shapes()
    return pl.pallas_call(
        body, name="norm_rows", grid=(S // tm,),
        in_specs=in_specs, out_specs=tuple(out_specs), out_shape=tuple(out_shape), scratch_shapes=scratch,
        compiler_params=_params(("arbitrary",)),
    )(*args)


def _inproj(h_rm, h, w_t, segments, hosted=None, tm=1024, tn=512):
    S, D = h.shape
    ns = len(segments)
    ni = S // tm
    counts = [seg[0] // tn for seg in segments]
    starts = [sum(counts[:s]) for s in range(ns)]

    hn = hosted.n if hosted is not None else 0
    last_p = sum(counts)

    def body(hrm_ref, h_ref, w_ref, *rest):
        h_ins, rest = rest[:hn], rest[hn:]
        outs = rest[:ns]
        h_outs = rest[ns:ns + hn]
        hrm_scr, h_scr = rest[ns + hn:ns + 2 + hn]
        h_sems = rest[ns + 2 + hn:]
        p, i = pl.program_id(0), pl.program_id(1)

        if hosted is not None:
            @pl.when((p == 0) & (i == 0))
            def _():
                hosted.start(h_ins, h_outs, h_sems)

            @pl.when((p == last_p) & (i == ni - 1))
            def _():
                hosted.finish(h_ins, h_outs, h_sems)

        @pl.when(p == 0)
        def _():
            h_scr[i] = h_ref[...]
            hrm_scr[i] = hrm_ref[...]

        for s, (_, scale, rm) in enumerate(segments):
            @pl.when((p > starts[s]) & (p <= starts[s] + counts[s]))
            def _(s=s, scale=scale, rm=rm):
                acc = _nt((hrm_scr if rm else h_scr)[i], w_ref[...])
                outs[s][...] = acc * scale if scale != 1.0 else acc

    def out_index(s):
        def index(p, i):
            j = p - 1 - starts[s]
            row = jnp.where(j < 0, 0, jnp.where(j >= counts[s], ni - 1, i))
            return row, jnp.clip(j, 0, counts[s] - 1)
        return index

    first_pass = pl.BlockSpec((tm, D), lambda p, i: (jnp.where(p == 0, i, ni - 1), 0))
    out_specs = [pl.BlockSpec((tm, tn), out_index(s)) for s in range(ns)]
    out_shape = [jax.ShapeDtypeStruct((S, seg[0]), F32) for seg in segments]
    in_specs = [first_pass, first_pass, pl.BlockSpec((tn, D), lambda p, i: (jnp.maximum(p - 1, 0), 0))]
    args = [h_rm, h, w_t]
    scratch = [pltpu.VMEM((ni, tm, D), BF16), pltpu.VMEM((ni, tm, D), BF16)]
    if hosted is not None:
        in_specs += [ANY_SPEC] * hn
        args += hosted.arrays
        out_specs += [ANY_SPEC] * hn
        out_shape += hosted.out_shapes()
        scratch += hosted.sem_shapes()
    return pl.pallas_call(
        body, name="inproj", grid=(1 + last_p, ni),
        in_specs=in_specs, out_specs=tuple(out_specs), out_shape=tuple(out_shape), scratch_shapes=scratch,
        compiler_params=_params(("arbitrary", "arbitrary"), BIG_VMEM_LIMIT),
    )(*args)


def _fill_bias_table(tbl, rows, keys_first=False):
    shape = (2 * BLK, BLK) if keys_first else (BLK, 2 * BLK)
    qi = lax.broadcasted_iota(jnp.int32, shape, 1 if keys_first else 0)
    kj = lax.broadcasted_iota(jnp.int32, shape, 0 if keys_first else 1)
    dist = rows.pos(qi) - rows.pos(kj % BLK) + jnp.where(kj < BLK, BLK, 0)
    inside = (dist >= 0) & (dist <= BLK)
    negd = (dist * (-rows.dil)).astype(F32)
    for f, valid in enumerate((inside & (kj >= BLK), inside)):
        for h in range(N_Q_HEADS):
            tbl[f * N_Q_HEADS + h] = jnp.where(valid, SLOPES[h] * negd, NEG)


def _bias2(tbl, n, h0, h1, axis=0):
    base = jnp.where(n == 0, 0, N_Q_HEADS)
    return jnp.concatenate([tbl[base + h0], tbl[base + h1]], axis=axis)


def _head_operands(kv2, hk, lo_mask):
    half, pos = hk // 2, hk % 2
    out = []
    for base in (0, KV_W):
        t = kv2[:, base + half * LANES: base + (half + 1) * LANES]
        sw = pltpu.roll(t, HEAD_DIM, axis=1)
        at_lo, at_hi = (t, sw) if pos == 0 else (sw, t)
        out.append(jnp.where(lo_mask, at_lo, 0.0).astype(BF16))
        out.append(jnp.where(lo_mask, 0.0, at_hi).astype(BF16))
    return out


def _nt(a, b):
    return lax.dot_general(a, b, (((1,), (1,)), ((), ())), preferred_element_type=F32)


def _tn(a, b):
    return lax.dot_general(a, b, (((0,), (0,)), ((), ())), preferred_element_type=F32)


def _attn_fwd(q, kv, dil, name, prev=(), gate=None):
    S = q.shape[0]
    rows = _Rows(dil, S)
    nb = rows.nb
    have_prev, last = len(prev) > 0, gate is not None

    def body(*refs):
        refs = list(refs)
        q_ref, kvc_ref, kvp_ref = refs[:3]
        del refs[:3]
        po_refs, pl_refs = refs[0:2 * len(prev):2], refs[1:2 * len(prev):2]
        del refs[:2 * len(prev)]
        if last:
            gate_ref = refs.pop(0)
        o_ref, lse_ref = refs[:2]
        y_ref = refs[2] if last else None
        tbl = refs[-1]
        n = pl.program_id(1)

        @pl.when((pl.program_id(0) == 0) & (n == 0))
        def _():
            _fill_bias_table(tbl, rows)

        kv2 = jnp.concatenate([_ld(kvp_ref), _ld(kvc_ref)], axis=0)
        lo_mask = lax.broadcasted_iota(jnp.int32, (2 * BLK, LANES), 1) < HEAD_DIM
        lane = lax.broadcasted_iota(jnp.int32, (BLK, LANES), 1)
        stats = jnp.zeros((BLK, LANES), F32)
        for hk in range(N_KV_HEADS):
            k_lo, k_hi, v_lo, v_hi = _head_operands(kv2, hk, lo_mask)
            cols = [slice(b * LANES, (b + 1) * LANES) for b in (2 * hk, 2 * hk + 1)]
            q2 = jnp.concatenate([_ld(q_ref, cols[0]), _ld(q_ref, cols[1])], axis=0).astype(BF16)
            o2 = jnp.zeros((2 * BLK, LANES), F32)
            for which, (kk, vv) in enumerate(((k_lo, v_lo), (k_hi, v_hi))):
                h0, h1 = 4 * hk + which, 4 * hk + 2 + which
                s = _nt(q2, kk) + _bias2(tbl, n, h0, h1)
                m = jnp.max(s, axis=1, keepdims=True)
                p = jnp.exp(s - m)
                l = jnp.sum(p, axis=1, keepdims=True)
                o2 = o2 + jnp.dot(p.astype(BF16), vv, preferred_element_type=F32) * (1.0 / l)
                lse = m + jnp.log(l)
                stats = jnp.where(lane == h0, lse[0:BLK], stats)
                stats = jnp.where(lane == h1, lse[BLK:], stats)
            _st(o_ref, o2[0:BLK], cols[0])
            _st(o_ref, o2[BLK:], cols[1])
        if have_prev:
            others = [_ld(r) for r in pl_refs]
            top = stats
            for b in others:
                top = jnp.maximum(top, b)
            e_new = jnp.exp(stats - top)
            e_old = [jnp.exp(b - top) for b in others]
            total = e_new
            for e in e_old:
                total = total + e
            stats = top + jnp.log(total)
            inv = 1.0 / total
            w_new, w_old = e_new * inv, [e * inv for e in e_old]
        if have_prev or last:
            lo = lane < HEAD_DIM
            for blk in range(ATT_W // LANES):
                cols = slice(blk * LANES, (blk + 1) * LANES)
                o_blk = _ld(o_ref, cols)
                if have_prev:
                    pick = lambda w: jnp.where(lo, w[:, 2 * blk:2 * blk + 1], w[:, 2 * blk + 1:2 * blk + 2])
                    o_blk = o_blk * pick(w_new)
                    for po_ref, w in zip(po_refs, w_old):
                        o_blk = o_blk + _ld(po_ref, cols) * pick(w)
                    _st(o_ref, o_blk, cols)
                if last:
                    a = _ld(gate_ref, cols)
                    _st(y_ref, (o_blk * (a * _sigmoid(a))).astype(BF16), cols)
        _st(lse_ref, stats)

    here = lambda n: n
    before_n = lambda n: jnp.maximum(n - 1, 0)
    in_specs = [rows.spec(ATT_W, here), rows.spec(2 * KV_W, here), rows.spec(2 * KV_W, before_n)]
    args = [rows.of(q), rows.of(kv), rows.of(kv)]
    for o_other, lse_other in prev:
        in_specs += [rows.spec(ATT_W, here), rows.spec(LANES, here)]
        args += [rows.of(o_other), rows.of(lse_other)]
    out_specs = [rows.spec(ATT_W, here), rows.spec(LANES, here)]
    out_shape = [jax.ShapeDtypeStruct(rows.view + (ATT_W,), F32), jax.ShapeDtypeStruct(rows.view + (LANES,), F32)]
    if last:
        in_specs.append(rows.spec(ATT_W, here))
        args.append(rows.of(gate))
        out_specs.append(rows.spec(ATT_W, here))
        out_shape.append(jax.ShapeDtypeStruct(rows.view + (ATT_W,), BF16))
    res = pl.pallas_call(
        body, name=name, grid=(dil, nb),
        in_specs=in_specs, out_specs=tuple(out_specs), out_shape=tuple(out_shape),
        scratch_shapes=[pltpu.VMEM((2 * N_Q_HEADS, BLK, 2 * BLK), F32)],
        compiler_params=_params(("arbitrary", "arbitrary")),
    )(*args)
    return tuple(r.reshape(S, r.shape[-1]) for r in res)


def _shifted_copies(buf, phases):
    n = phases.shape[1]
    for b in range(1, 8):
        phases[b - 1] = buf[b:b + n, :]


def _window(buf, phases, start, cols):
    b = start % 8
    if b == 0:
        return buf[start:start + 8, cols]
    return phases[b - 1, start - b:start - b + 8, cols]


def _broadcast_taps(w_ref, wb):
    for j in range(CONV_K):
        wb[j] = jnp.broadcast_to(w_ref[j:j + 1, :], wb.shape[1:])


def _conv_fwd(gates, conv_w, conv_b, ln_g, ln_b, tt=256):
    S = gates.shape[0]
    C = conv_w.shape[1]
    hb = tt // CONV_HALO

    def body(val_ref, glu_ref, hval_ref, hglu_ref, gate_ref, w_ref, b_ref, g_ref, beta_ref,
             conv_ref, y_ref, hbuf, hph):
        i = pl.program_id(0)
        halo = hval_ref[...] * _sigmoid(hglu_ref[...])
        hbuf[0:CONV_HALO, :] = jnp.where(i > 0, halo, 0.0)
        hbuf[CONV_HALO:, :] = val_ref[...] * _sigmoid(glu_ref[...])
        _shifted_copies(hbuf, hph)
        for cb in range(C // LANES):
            cols = slice(cb * LANES, (cb + 1) * LANES)
            wj = [jnp.broadcast_to(w_ref[j:j + 1, cols], (8, LANES)) for j in range(CONV_K)]
            for rc in range(tt // 8):
                acc = jnp.zeros((8, LANES), F32)
                for j in range(CONV_K):
                    start = rc * 8 + CONV_HALO - (CONV_K - 1) + j
                    acc = acc + _window(hbuf, hph, start, cols) * wj[j]
                conv_ref[rc * 8:(rc + 1) * 8, cols] = acc
        cv = conv_ref[...] + b_ref[...]
        conv_ref[...] = cv
        mu = jnp.mean(cv, axis=-1, keepdims=True)
        xc = cv - mu
        var = jnp.mean(xc * xc, axis=-1, keepdims=True)
        ln = xc * lax.rsqrt(var + LN_EPS) * g_ref[...] + beta_ref[...]
        gt = gate_ref[...]
        y_ref[...] = (ln * _sigmoid(ln) * (gt * _sigmoid(gt))).astype(BF16)

    vec = pl.BlockSpec((1, C), lambda i: (0, 0))
    return pl.pallas_call(
        body, name="conv_fwd", grid=(S // tt,),
        in_specs=[pl.BlockSpec((tt, C), lambda i: (i, 0)),
                  pl.BlockSpec((tt, C), lambda i: (i, 1)),
                  pl.BlockSpec((CONV_HALO, C), lambda i: (jnp.maximum(i * hb - 1, 0), 0)),
                  pl.BlockSpec((CONV_HALO, C), lambda i: (jnp.maximum(i * hb - 1, 0), 1)),
                  pl.BlockSpec((tt, C), lambda i: (i, 2)),
                  pl.BlockSpec((CONV_HALO, C), lambda i: (0, 0)), vec, vec, vec],
        out_specs=(pl.BlockSpec((tt, C), lambda i: (i, 0)), pl.BlockSpec((tt, C), lambda i: (i, 0))),
        out_shape=(jax.ShapeDtypeStruct((S, C), F32), jax.ShapeDtypeStruct((S, C), BF16)),
        scratch_shapes=[pltpu.VMEM((tt + CONV_HALO, C), F32), pltpu.VMEM((7, tt + CONV_HALO - 8, C), F32)],
        compiler_params=_params(("parallel",)),
    )(gates, gates, gates, gates, gates, conv_w, conv_b, ln_g, ln_b)


def _outproj_loss(x, y_att, y_conv, w_out, gf, target, tm=512):
    S, D = x.shape
    E = y_att.shape[1]

    def body(x_ref, ya_ref, yc_ref, w_ref, gf_ref, t_ref, dx_ref, dxb_ref, loss_ref, ggf_ref):
        @pl.when(pl.program_id(0) == 0)
        def _():
            loss_ref[...] = jnp.zeros_like(loss_ref)
            ggf_ref[...] = jnp.zeros_like(ggf_ref)

        x2 = (x_ref[...] + jnp.dot(_perm_rows(ya_ref[...], True), w_ref[0:E, :], preferred_element_type=F32)
              + jnp.dot(yc_ref[...], w_ref[E:, :], preferred_element_type=F32))
        r = lax.rsqrt(jnp.mean(x2 * x2, axis=-1, keepdims=True) + NORM_EPS)
        nrm = x2 * r
        gfv = gf_ref[...]
        err = nrm * gfv - t_ref[...]
        loss_ref[...] += jnp.sum(err * err, axis=0, keepdims=True)
        dout = err * (1.0 / D)
        ggf_ref[...] += jnp.sum(dout * nrm, axis=0, keepdims=True)
        dn = dout * gfv
        dx2 = r * (dn - nrm * jnp.mean(dn * nrm, axis=-1, keepdims=True))
        dx_ref[...] = dx2
        dxb_ref[...] = dx2.astype(BF16)

    row = lambda w: pl.BlockSpec((tm, w), lambda i: (i, 0))
    vec = pl.BlockSpec((1, D), lambda i: (0, 0))
    return pl.pallas_call(
        body, name="outproj_loss", grid=(S // tm,),
        in_specs=[row(D), row(E), row(E), pl.BlockSpec((2 * E, D), lambda i: (0, 0)), vec, row(D)],
        out_specs=(row(D), row(D), vec, vec),
        out_shape=(jax.ShapeDtypeStruct((S, D), F32), jax.ShapeDtypeStruct((S, D), BF16),
                   jax.ShapeDtypeStruct((1, D), F32), jax.ShapeDtypeStruct((1, D), F32)),
        compiler_params=_params(("arbitrary",)),
    )(x, y_att, y_conv, w_out, gf, target)


def _split3(v):
    hi = v.astype(BF16)
    r1 = v - hi.astype(F32)
    mid = r1.astype(BF16)
    lo = (r1 - mid.astype(F32)).astype(BF16)
    return hi, mid, lo


def _dy_att(dxb, w_out, gates, o, tm=512):
    S, D = dxb.shape
    E = ATT_W

    def body(dx_ref, w_ref, a_ref, o_ref, do_ref, da_ref, dl_ref, dxr_ref):
        dxr = _perm_rows(dx_ref[...], False)
        dxr_ref[...] = dxr
        dya = _nt(dxr, w_ref[...])
        a = a_ref[...]
        ov = o_ref[...]
        sl, dsl = _silu_and_grad(a)
        d_o = dya * sl
        do_ref[...] = d_o
        da_ref[...] = (dya * ov * dsl).astype(BF16)
        ci = lax.broadcasted_iota(jnp.int32, (E, LANES), 0) // HEAD_DIM
        hi = lax.broadcasted_iota(jnp.int32, (E, LANES), 1)
        sel = jnp.where(ci == hi, 1.0, 0.0).astype(BF16)
        acc = jnp.zeros((tm, LANES), F32)
        for part in _split3(d_o * ov):
            acc = acc + jnp.dot(part, sel, preferred_element_type=F32)
        dl_ref[...] = acc

    row = lambda w: pl.BlockSpec((tm, w), lambda i: (i, 0))
    return pl.pallas_call(
        body, name="dy_att", grid=(S // tm,),
        in_specs=[row(D), pl.BlockSpec((E, D), lambda i: (0, 0)), row(E), row(E)],
        out_specs=(row(E), row(E), row(LANES), row(D)),
        out_shape=(jax.ShapeDtypeStruct((S, E), F32), jax.ShapeDtypeStruct((S, E), BF16),
                   jax.ShapeDtypeStruct((S, LANES), F32), jax.ShapeDtypeStruct((S, D), BF16)),
        compiler_params=_params(("parallel",)),
    )(dxb, w_out, gates, o)


def _dy_conv(dxb, w_out, gates, conv_out, ln_g, ln_b, tm=512):
    S, D = dxb.shape
    C = conv_out.shape[1]

    def body(dx_ref, w_ref, gate_ref, cv_ref, g_ref, beta_ref, dgate_ref, dconv_ref, gg_ref, gb_ref, gcb_ref):
        @pl.when(pl.program_id(0) == 0)
        def _():
            gg_ref[...] = jnp.zeros_like(gg_ref)
            gb_ref[...] = jnp.zeros_like(gb_ref)
            gcb_ref[...] = jnp.zeros_like(gcb_ref)

        dyc = _nt(dx_ref[...], w_ref[...])
        cv = cv_ref[...]
        mu = jnp.mean(cv, axis=-1, keepdims=True)
        xc = cv - mu
        rstd = lax.rsqrt(jnp.mean(xc * xc, axis=-1, keepdims=True) + LN_EPS)
        nrm = xc * rstd
        gv = g_ref[...]
        ln = nrm * gv + beta_ref[...]
        u, du = _silu_and_grad(ln)
        gt = gate_ref[...]
        g2, dg2 = _silu_and_grad(gt)
        dgate_ref[...] = (dyc * u * dg2).astype(BF16)
        d_ln = dyc * g2 * du
        gb_ref[...] += jnp.sum(d_ln, axis=0, keepdims=True)
        gg_ref[...] += jnp.sum(d_ln * nrm, axis=0, keepdims=True)
        dn = d_ln * gv
        d_conv = rstd * (dn - jnp.mean(dn, axis=-1, keepdims=True)
                         - nrm * jnp.mean(dn * nrm, axis=-1, keepdims=True))
        dconv_ref[...] = d_conv
        gcb_ref[...] += jnp.sum(d_conv, axis=0, keepdims=True)

    row = lambda w: pl.BlockSpec((tm, w), lambda i: (i, 0))
    vec = pl.BlockSpec((1, C), lambda i: (0, 0))
    return pl.pallas_call(
        body, name="dy_conv", grid=(S // tm,),
        in_specs=[row(D), pl.BlockSpec((C, D), lambda i: (1, 0)),
                  pl.BlockSpec((tm, C), lambda i: (i, 2)), row(C), vec, vec],
        out_specs=(row(C), row(C), vec, vec, vec),
        out_shape=(jax.ShapeDtypeStruct((S, C), BF16), jax.ShapeDtypeStruct((S, C), F32),
                   jax.ShapeDtypeStruct((1, C), F32), jax.ShapeDtypeStruct((1, C), F32),
                   jax.ShapeDtypeStruct((1, C), F32)),
        compiler_params=_params(("arbitrary",)),
    )(dxb, w_out, gates, conv_out, ln_g, ln_b)


def _conv_bwd(d_conv, gates, d_c_gate, conv_w, hosted=None, tt=256):
    S, C = d_conv.shape
    hb = tt // CONV_HALO
    nt = S // tt
    hn = hosted.n if hosted is not None else 0

    def body(*refs):
        dc_ref, dnext_ref, val_ref, glu_ref, dg_ref, w_ref = refs[:6]
        h_ins = refs[6:6 + hn]
        out_ref, gw_ref = refs[6 + hn:8 + hn]
        h_outs = refs[8 + hn:8 + 2 * hn]
        hbuf, dbuf, dhbuf, dph, wb = refs[8 + 2 * hn:13 + 2 * hn]
        h_sems = refs[13 + 2 * hn:]
        i = pl.program_id(0)

        @pl.when(i == 0)
        def _():
            gw_ref[...] = jnp.zeros_like(gw_ref)
            _broadcast_taps(w_ref, wb)
            if hosted is not None:
                hosted.start(h_ins, h_outs, h_sems)

        val = val_ref[...]
        sg = _sigmoid(glu_ref[...])
        hbuf[...] = val * sg
        dbuf[0:tt, :] = dc_ref[...]
        dbuf[tt:, :] = jnp.where(i < nt - 1, dnext_ref[...], 0.0)
        _shifted_copies(dbuf, dph)
        for cb in range(C // LANES):
            cols = slice(cb * LANES, (cb + 1) * LANES)
            gacc = [jnp.zeros((8, LANES), F32) for _ in range(CONV_K)]
            group = 2
            for rc0 in range(0, tt // 8, group):
                hcur = [hbuf[(rc0 + r) * 8:(rc0 + r + 1) * 8, cols] for r in range(group)]
                accs = [jnp.zeros((8, LANES), F32) for _ in range(group)]
                for j in range(CONV_K):
                    wj = wb[j, :, cols]
                    for r in range(group):
                        dwin = _window(dbuf, dph, (rc0 + r) * 8 + (CONV_K - 1) - j, cols)
                        accs[r] = accs[r] + dwin * wj
                        gacc[j] = gacc[j] + dwin * hcur[r]
                for r in range(group):
                    dhbuf[(rc0 + r) * 8:(rc0 + r + 1) * 8, cols] = accs[r]
            for j in range(CONV_K):
                gw_ref[j:j + 1, cols] += jnp.sum(gacc[j], axis=0, keepdims=True)
        d_h = dhbuf[...]
        out_ref[:, 0:C] = (d_h * sg).astype(BF16)
        out_ref[:, C:2 * C] = (d_h * val * sg * (1.0 - sg)).astype(BF16)
        out_ref[:, 2 * C:3 * C] = dg_ref[...]

        if hosted is not None:
            @pl.when(i == nt - 1)
            def _():
                hosted.finish(h_ins, h_outs, h_sems)

    tile = lambda col: pl.BlockSpec((tt, C), lambda i: (i, col))
    in_specs = [tile(0),
                pl.BlockSpec((CONV_HALO, C), lambda i: (jnp.minimum((i + 1) * hb, S // CONV_HALO - 1), 0)),
                tile(0), tile(1), tile(0),
                pl.BlockSpec((CONV_HALO, C), lambda i: (0, 0))]
    args = [d_conv, d_conv, gates, gates, d_c_gate, conv_w]
    out_specs = [pl.BlockSpec((tt, 3 * C), lambda i: (i, 0)), pl.BlockSpec((CONV_HALO, C), lambda i: (0, 0))]
    out_shape = [jax.ShapeDtypeStruct((S, 3 * C), BF16), jax.ShapeDtypeStruct((CONV_HALO, C), F32)]
    scratch = [pltpu.VMEM((tt, C), F32), pltpu.VMEM((tt + CONV_HALO, C), F32), pltpu.VMEM((tt, C), F32),
               pltpu.VMEM((7, tt + CONV_HALO - 8, C), F32), pltpu.VMEM((CONV_K, 8, C), F32)]
    if hosted is not None:
        in_specs += [ANY_SPEC] * hn
        args += hosted.arrays
        out_specs += [ANY_SPEC] * hn
        out_shape += hosted.out_shapes()
        scratch += hosted.sem_shapes()
    res = pl.pallas_call(
        body, name="conv_bwd", grid=(nt,),
        in_specs=in_specs, out_specs=tuple(out_specs), out_shape=tuple(out_shape), scratch_shapes=scratch,
        compiler_params=_params(("arbitrary",)),
    )(*args)
    return res[0], res[1], list(res[2:])


def _attn_bwd(q, kv, d_o, lse, delta, dil, prev, final, name, hosted=None):
    S = q.shape[0]
    rows = _Rows(dil, S)
    nb = rows.nb
    steps = dil * nb
    out_dt = BF16 if final else F32
    have_prev = prev is not None
    hn = hosted.n if hosted is not None else 0

    def body(*refs):
        refs = list(refs)
        q_ref, do_ref, lse_ref, dl_ref, kvc_ref, kvp_ref = refs[:6]
        del refs[:6]
        if have_prev:
            pdq_ref, pdkv_ref = refs[:2]
            del refs[:2]
        h_ins = refs[:hn]
        dq_ref, dkv_ref = refs[hn:hn + 2]
        h_outs = refs[hn + 2:2 * hn + 2]
        carry, tbl = refs[2 * hn + 2:2 * hn + 4]
        h_sems = refs[2 * hn + 4:]
        t = pl.program_id(0)
        n = t % nb

        @pl.when(t == 0)
        def _():
            if hosted is not None:
                hosted.start(h_ins, h_outs, h_sems)
            _fill_bias_table(tbl, rows, keys_first=True)
            carry[...] = jnp.zeros_like(carry)

        @pl.when(t < steps)
        def _():
            kv2 = jnp.concatenate([_ld(kvp_ref), _ld(kvc_ref)], axis=0)
            lse_t, dl_t = _ld(lse_ref).T, _ld(dl_ref).T
            lo_mask = lax.broadcasted_iota(jnp.int32, (2 * BLK, LANES), 1) < HEAD_DIM
            halves = [jnp.zeros((2 * BLK, LANES), F32) for _ in range(4)]
            for hk in range(N_KV_HEADS):
                k_lo, k_hi, v_lo, v_hi = _head_operands(kv2, hk, lo_mask)
                cols = [slice(b * LANES, (b + 1) * LANES) for b in (2 * hk, 2 * hk + 1)]
                q2 = jnp.concatenate([_ld(q_ref, cols[0]), _ld(q_ref, cols[1])], axis=0).astype(BF16)
                do2 = jnp.concatenate([_ld(do_ref, cols[0]), _ld(do_ref, cols[1])], axis=0).astype(BF16)
                dq2 = jnp.zeros((2 * BLK, LANES), F32)
                dks, dvs = [], []
                for which, (kk, vv) in enumerate(((k_lo, v_lo), (k_hi, v_hi))):
                    h0, h1 = 4 * hk + which, 4 * hk + 2 + which
                    s = _nt(kk, q2) + _bias2(tbl, n, h0, h1, axis=1)
                    lse2 = jnp.concatenate([lse_t[h0:h0 + 1, :], lse_t[h1:h1 + 1, :]], axis=1)
                    dl2 = jnp.concatenate([dl_t[h0:h0 + 1, :], dl_t[h1:h1 + 1, :]], axis=1)
                    p = jnp.exp(s - lse2)
                    ds = (p * (_nt(vv, do2) - dl2)).astype(BF16)
                    dq2 = dq2 + _tn(ds, kk)
                    dks.append(jnp.dot(ds, q2, preferred_element_type=F32))
                    dvs.append(jnp.dot(p.astype(BF16), do2, preferred_element_type=F32))
                dk_sum = jnp.where(lo_mask, dks[0], dks[1])
                dv_sum = jnp.where(lo_mask, dvs[0], dvs[1])
                for jp in range(2):
                    dq_blk = dq2[jp * BLK:(jp + 1) * BLK]
                    if have_prev:
                        dq_blk = dq_blk + _ld(pdq_ref, cols[jp])
                    if final:
                        dq_blk = dq_blk * (HEAD_DIM ** -0.5)
                    _st(dq_ref, dq_blk.astype(out_dt), cols[jp])
                half, pos = hk // 2, hk % 2
                here = lo_mask if pos == 0 else jnp.logical_not(lo_mask)
                dk_tot = dk_sum + pltpu.roll(dk_sum, HEAD_DIM, axis=1)
                dv_tot = dv_sum + pltpu.roll(dv_sum, HEAD_DIM, axis=1)
                halves[half] = halves[half] + jnp.where(here, dk_tot, 0.0)
                halves[2 + half] = halves[2 + half] + jnp.where(here, dv_tot, 0.0)
            for b in range(4):
                cols = slice(b * LANES, (b + 1) * LANES)
                done = carry[:, cols] + halves[b][0:BLK, :]
                if have_prev:
                    done = done + _ld(pdkv_ref, cols)
                _st(dkv_ref, done.astype(out_dt), cols)
                carry[:, cols] = halves[b][BLK:, :]

        @pl.when(t == steps)
        def _():
            done = carry[...]
            if have_prev:
                done = done + _ld(pdkv_ref)
            _st(dkv_ref, done.astype(out_dt))
            if hosted is not None:
                hosted.finish(h_ins, h_outs, h_sems)

    def spec(width, lag):
        def index(t):
            u = jnp.clip(t - lag, 0, steps - 1)
            return rows.index(u // nb, u % nb)
        return pl.BlockSpec(rows.block + (width,), index)

    def key_prev(t):
        u = jnp.minimum(t, steps - 1)
        return rows.index(u // nb, jnp.maximum(u % nb - 1, 0))

    in_specs = [spec(ATT_W, 0), spec(ATT_W, 0), spec(LANES, 0), spec(LANES, 0), spec(2 * KV_W, 0),
                pl.BlockSpec(rows.block + (2 * KV_W,), key_prev)]
    args = [rows.of(q), rows.of(d_o), rows.of(lse), rows.of(delta), rows.of(kv), rows.of(kv)]
    if have_prev:
        in_specs += [spec(ATT_W, 0), spec(2 * KV_W, 1)]
        args += [rows.of(prev[0]), rows.of(prev[1])]
    out_specs = [spec(ATT_W, 0), spec(2 * KV_W, 1)]
    out_shape = [jax.ShapeDtypeStruct(rows.view + (ATT_W,), out_dt),
                 jax.ShapeDtypeStruct(rows.view + (2 * KV_W,), out_dt)]
    scratch = [pltpu.VMEM((BLK, 2 * KV_W), F32), pltpu.VMEM((2 * N_Q_HEADS, 2 * BLK, BLK), F32)]
    if hosted is not None:
        in_specs += [ANY_SPEC] * hn
        args += hosted.arrays
        out_specs += [ANY_SPEC] * hn
        out_shape += hosted.out_shapes()
        scratch += hosted.sem_shapes()
    res = pl.pallas_call(
        body, name=name, grid=(steps + 1,),
        in_specs=in_specs, out_specs=tuple(out_specs), out_shape=tuple(out_shape), scratch_shapes=scratch,
        compiler_params=_params(("arbitrary",)),
    )(*args)
    return (res[0].reshape(S, ATT_W), res[1].reshape(S, 2 * KV_W)), list(res[2:])


def _dh(segments, w_in, x, dx2, g, hosted=None, tm=1024, tk=512):
    S, D = x.shape
    ns = len(segments)
    counts = [a.shape[1] // tk for a, _ in segments]
    starts = [sum(counts[:s]) for s in range(ns)]
    nk = sum(counts)
    hn = hosted.n if hosted is not None else 0

    def body(*refs):
        seg_refs = refs[:ns]
        w_ref, x_ref, dx2_ref, g_ref = refs[ns:ns + 4]
        h_ins = refs[ns + 4:ns + 4 + hn]
        gx_ref, gng_ref = refs[ns + 4 + hn:ns + 6 + hn]
        h_outs = refs[ns + 6 + hn:ns + 6 + 2 * hn]
        acc = refs[ns + 6 + 2 * hn]
        h_sems = refs[ns + 7 + 2 * hn:]
        k, i = pl.program_id(0), pl.program_id(1)

        @pl.when((i == 0) & (k == 0))
        def _():
            gng_ref[...] = jnp.zeros_like(gng_ref)
            if hosted is not None:
                hosted.start(h_ins, h_outs, h_sems)

        @pl.when(k == 0)
        def _():
            acc[i] = jnp.zeros(acc.shape[1:], F32)

        for s in range(ns):
            @pl.when((k >= starts[s]) & (k < starts[s] + counts[s]))
            def _(s=s):
                t = seg_refs[s][...]
                if segments[s][1]:
                    t = _perm_rows(t, True)
                acc[i] += jnp.dot(t, w_ref[...], preferred_element_type=F32)

        @pl.when(k == nk - 1)
        def _():
            dh = acc[i]
            xf = x_ref[...]
            r = lax.rsqrt(jnp.mean(xf * xf, axis=-1, keepdims=True) + NORM_EPS)
            nrm = xf * r
            gng_ref[...] += jnp.sum(dh * nrm, axis=0, keepdims=True)
            dn = dh * g_ref[...]
            gx_ref[...] = dx2_ref[...] + r * (dn - nrm * jnp.mean(dn * nrm, axis=-1, keepdims=True))

        if hosted is not None:
            @pl.when((i == S // tm - 1) & (k == nk - 1))
            def _():
                hosted.finish(h_ins, h_outs, h_sems)

    ni = S // tm
    row = pl.BlockSpec((tm, D), lambda k, i: (jnp.where(k == nk - 1, i, 0), 0))
    vec = pl.BlockSpec((1, D), lambda k, i: (0, 0))

    def seg_index(s):
        def index(k, i):
            j = k - starts[s]
            return jnp.where(j < 0, 0, jnp.where(j >= counts[s], ni - 1, i)), jnp.clip(j, 0, counts[s] - 1)
        return index

    in_specs = [pl.BlockSpec((tm, tk), seg_index(s)) for s in range(ns)]
    in_specs += [pl.BlockSpec((tk, D), lambda k, i: (k, 0)), row, row, vec]
    args = [a for a, _ in segments] + [w_in, x, dx2, g]
    out_specs = [row, vec]
    out_shape = [jax.ShapeDtypeStruct((S, D), F32), jax.ShapeDtypeStruct((1, D), F32)]
    scratch = [pltpu.VMEM((ni, tm, D), F32)]
    if hosted is not None:
        in_specs += [ANY_SPEC] * hn
        args += hosted.arrays
        out_specs += [ANY_SPEC] * hn
        out_shape += hosted.out_shapes()
        scratch += hosted.sem_shapes()
    res = pl.pallas_call(
        body, name="dh", grid=(nk, S // tm),
        in_specs=in_specs, out_specs=tuple(out_specs), out_shape=tuple(out_shape), scratch_shapes=scratch,
        compiler_params=_params(("arbitrary", "arbitrary"), BIG_VMEM_LIMIT),
    )(*args)
    return res[0], res[1], list(res[2:])


def _tn_matmul(a, bs, name, b_first=False, tm=512):
    M, K = a.shape
    nb = len(bs)
    shapes = [(b.shape[1], K) if b_first else (K, b.shape[1]) for b in bs]
    last = M // tm - 1

    def body(a_ref, *refs):
        b_refs, o_refs, accs = refs[:nb], refs[nb:2 * nb], refs[2 * nb:]

        @pl.when(pl.program_id(0) == 0)
        def _():
            for acc in accs:
                acc[...] = jnp.zeros_like(acc)

        at = a_ref[...]
        for b_ref, acc in zip(b_refs, accs):
            for c in range(0, b_ref.shape[1], 512):
                if b_first:
                    acc[c:c + 512, :] += _tn(b_ref[:, c:c + 512], at)
                else:
                    acc[:, c:c + 512] += _tn(at, b_ref[:, c:c + 512])

        @pl.when(pl.program_id(0) == last)
        def _():
            for o_ref, acc in zip(o_refs, accs):
                o_ref[...] = acc[...].astype(o_ref.dtype)

    return pl.pallas_call(
        body, name=name, grid=(M // tm,),
        in_specs=[pl.BlockSpec((tm, K), lambda m: (m, 0))] + [pl.BlockSpec((tm, b.shape[1]), lambda m: (m, 0))
                                                              for b in bs],
        out_specs=tuple(pl.BlockSpec(s, lambda m: (0, 0)) for s in shapes),
        out_shape=tuple(jax.ShapeDtypeStruct(s, BF16) for s in shapes),
        scratch_shapes=[pltpu.VMEM(s, F32) for s in shapes],
        compiler_params=_params(("arbitrary",)),
    )(a, *bs)


def _adamw(parts, w, m, v, name, tr=None, split=None, by_chip=False):
    R, C = w.shape
    tr = R if tr is None else tr
    parts = [parts] if split is None else list(parts)
    npar = len(parts)

    def total(p_ref):
        if by_chip:
            c = lax.axis_index("c")
            g = p_ref[c].astype(F32)
            for chip in range(1, N_DEV // 2):
                g = g + p_ref[2 * chip + c].astype(F32)
            return g
        g = p_ref[0].astype(F32)
        for dev in range(1, N_DEV):
            g = g + p_ref[dev].astype(F32)
        return g

    def body(*refs):
        w_ref, m_ref, v_ref, g_out, d_out, m_out, v_out = refs[npar:]
        if split is None:
            g = total(refs[0])
        else:
            g = jnp.where(_mesh_pos()[3] < split, total(refs[0]), total(refs[1]))
        mn = ADAM_B1 * m_ref[...] + (1.0 - ADAM_B1) * g
        vn = ADAM_B2 * v_ref[...] + (1.0 - ADAM_B2) * (g * g)
        m_hat = mn / (1.0 - ADAM_B1 ** ADAM_STEP)
        v_hat = vn / (1.0 - ADAM_B2 ** ADAM_STEP)
        g_out[...] = g
        d_out[...] = -ADAM_LR * (m_hat / (jnp.sqrt(v_hat) + ADAM_EPS) + ADAM_WD * w_ref[...])
        m_out[...] = mn
        v_out[...] = vn

    blk = pl.BlockSpec((tr, C), lambda i: (i, 0))
    shp = jax.ShapeDtypeStruct((R, C), F32)
    return pl.pallas_call(
        body, name=name, grid=(R // tr,),
        in_specs=[pl.BlockSpec((N_DEV, tr, C), lambda i: (0, i, 0))] * npar + [blk, blk, blk],
        out_specs=(blk, blk, blk, blk), out_shape=(shp, shp, shp, shp),
        compiler_params=_params(("parallel",)),
    )(*parts, w, m, v)


def _local_step(x, target, norm_g, w_in, conv_w, conv_b, ln_g, ln_b, w_out, gf, exchanges=None, first_weights=None,
                late_weights=None):
    ex_out, ex_att, ex_conv = exchanges if exchanges is not None else (None, None, None)
    h_rm, h, *first = _norm_rows(x, norm_g, first_weights[0] if first_weights is not None else None)
    if first_weights is not None:
        w_in = first_weights[1](first)
    conv_cols = w_in.shape[0] - 2 * ATT_W - 2 * KV_W
    q, kv, a_gate, gates, *gathered = _inproj(
        h_rm, h, w_in,
        [(ATT_W, HEAD_DIM ** -0.5, True), (2 * KV_W, 1.0, True), (ATT_W, 1.0, True), (conv_cols, 1.0, False)],
        late_weights[0] if late_weights is not None else None)
    if late_weights is not None:
        conv_w, w_out = late_weights[1](gathered)

    alone = [_attn_fwd(q, kv, dil, "attn_fwd_d%d" % dil) for _, dil in PATTERNS[1:]]
    o, lse, y_att = _attn_fwd(q, kv, PATTERNS[0][1], "attn_fwd_d%d" % PATTERNS[0][1], alone, a_gate)
    conv_out, y_conv = _conv_fwd(gates, conv_w, conv_b, ln_g, ln_b)
    dx2, dxb, loss_cols, g_gf = _outproj_loss(x, y_att, y_conv, w_out, gf, target)

    d_o, d_a_gate, delta, dxb_rm = _dy_att(dxb, w_out, a_gate, o)
    g_w_out = jnp.concatenate([_tn_matmul(y_att, [dxb_rm], "gw_out_att")[0],
                               _tn_matmul(y_conv, [dxb], "gw_out_conv")[0]], axis=0)
    acc, out_parts = None, []
    for idx, (_, dil) in enumerate(reversed(PATTERNS)):
        hosted = ex_out(g_w_out) if (idx == 0 and ex_out is not None) else None
        acc, outs = _attn_bwd(q, kv, d_o, lse, delta, dil, acc, idx == len(PATTERNS) - 1, "attn_bwd_d%d" % dil,
                              hosted)
        out_parts += outs
    dq, dkv = acc
    g_q, g_kv, g_a = _tn_matmul(h_rm, [dq, dkv, d_a_gate], "gw_in_att", b_first=True)

    d_c_gate, d_conv, g_ln_g, g_ln_b, g_conv_b = _dy_conv(dxb, w_out, gates, conv_out, ln_g, ln_b)
    dgates, g_conv_w, att_parts = _conv_bwd(d_conv, gates, d_c_gate, conv_w,
                                            ex_att(g_q, g_kv, g_a) if ex_att is not None else None)
    g_c, = _tn_matmul(h, [dgates], "gw_in_conv", b_first=True)
    grad_x, g_norm_g, conv_parts = _dh(
        [(dq, True), (dkv, True), (d_a_gate, True), (dgates, False)], w_in, x, dx2, norm_g,
        ex_conv(g_a, g_c, g_conv_w) if ex_conv is not None else None)
    small = (g_norm_g, g_conv_b, g_ln_g, g_ln_b, g_gf, loss_cols)
    return grad_x, (g_q, g_kv, g_a, g_c), g_w_out, g_conv_w, small, (out_parts, att_parts, conv_parts)


def kernel(x, norm_g, w_in, conv_w, conv_b, conv_ln_g, conv_ln_b, w_out, final_norm_g, loss_target, m_norm_g, m_w_in, m_conv_w, m_conv_b, m_conv_ln_g, m_conv_ln_b, m_w_out, m_final_norm_g, v_norm_g, v_w_in, v_conv_w, v_conv_b, v_conv_ln_g, v_conv_ln_b, v_w_out, v_final_norm_g):
    S, D = x.shape[1], x.shape[2]
    win_sh, wout_sh, cw_sh = w_in[0].T, w_out[0], conv_w[0]
    cols_sh, rows_sh, ch_sh = win_sh.shape[0], wout_sh.shape[0], cw_sh.shape[1]

    def first_weights(gathered):
        return gathered[0].reshape(N_DEV * cols_sh, D)

    def late_weights(gathered):
        wout_all, cw_all = gathered
        conv_w_full = cw_all.transpose(1, 0, 2).reshape(CONV_K, N_DEV * ch_sh)
        return jnp.pad(conv_w_full, ((0, CONV_HALO - CONV_K), (0, 0))), wout_all.reshape(N_DEV * rows_sh, D)

    gf = final_norm_g.reshape(1, D)

    first = -(-(ATT_W + 2 * KV_W) // cols_sh)
    a_off = first * cols_sh - (ATT_W + 2 * KV_W)
    assert 0 <= a_off <= ATT_W

    def pieces(parts, n):
        return jnp.concatenate(parts, axis=0).reshape(n, cols_sh, D)

    def ex_out(g_w_out):
        return _Exchange([g_w_out.reshape(N_DEV, rows_sh, D)], [(0, N_DEV)])

    same_core = (2, 4, 6)

    def ex_att(g_q, g_kv, g_a):
        mine = _chip_sum(pieces([g_q, g_kv, g_a[:a_off]], first), 0, "rs_att")
        return _Exchange([mine], [(0, first)], [same_core])

    def ex_conv(g_a, g_c, g_conv_w):
        mine = _chip_sum(pieces([g_a[a_off:], g_c], N_DEV - first), first, "rs_conv")
        return _Exchange(
            [mine, g_conv_w[:CONV_K].reshape(CONV_K, N_DEV, ch_sh).transpose(1, 0, 2)],
            [(first, N_DEV), (0, N_DEV)], [same_core, None])

    grad_x, _, _, _, small, parts = _local_step(
        x[0], loss_target[0], norm_g, None, None, conv_b, conv_ln_g, conv_ln_b, None, gf,
        (ex_out, ex_att, ex_conv), (_Gather([win_sh.astype(BF16)]), first_weights),
        (_Gather([wout_sh.astype(BF16), cw_sh]), late_weights))
    (wout_parts,), (win_parts_lo,), (win_parts_hi, cw_parts) = parts

    small_pack = jnp.concatenate(list(small) + [jnp.zeros((2, D), F32)], axis=0)
    small_parts, = _exchange(_Exchange([small_pack], [None]), "gather_small")

    upd_win = _adamw((win_parts_lo, win_parts_hi), win_sh, m_w_in[0].T, v_w_in[0].T, "adamw_w_in",
                     tr=cols_sh // 2, split=first, by_chip=True)
    upd_wout = _adamw(wout_parts, wout_sh, m_w_out[0], v_w_out[0], "adamw_w_out", tr=128)
    upd_cw = _adamw(cw_parts, cw_sh, m_conv_w[0], v_conv_w[0], "adamw_conv_w")
    zeros3 = jnp.zeros((3, D), F32)
    stack = lambda a, b, c, d_, e: jnp.concatenate([a, b, c, d_, e.reshape(1, D), zeros3], axis=0)
    upd_small = _adamw(
        small_parts,
        stack(norm_g, conv_b, conv_ln_g, conv_ln_b, final_norm_g),
        stack(m_norm_g, m_conv_b, m_conv_ln_g, m_conv_ln_b, m_final_norm_g),
        stack(v_norm_g, v_conv_b, v_conv_ln_g, v_conv_ln_b, v_final_norm_g) + jnp.concatenate(
            [jnp.zeros((5, D), F32), jnp.ones((3, D), F32)], axis=0),
        "adamw_small")

    loss = 0.5 / D * jnp.sum(upd_small[0][5])

    def outputs(kind):
        sm = upd_small[kind]
        return [sm[0:1], upd_win[kind].T[None], upd_cw[kind][None], sm[1:2], sm[2:3], sm[3:4],
                upd_wout[kind][None], sm[4]]

    return (loss, grad_x[None], *outputs(0), *outputs(1), *outputs(2), *outputs(3))
```

```python
import jax
import jax.numpy as jnp
from jax import lax
from jax.experimental import pallas as pl
from jax.experimental.pallas import tpu as pltpu

F32 = jnp.float32
BF16 = jnp.bfloat16

HEAD_DIM = 64
N_KV_HEADS = 4
N_Q_HEADS = 16
ATT_W = 1024
KV_W = 256
CONV_K = 31
CONV_HALO = 32
PATTERNS = ((128, 1), (512, 4), (2048, 16))
BLK = 128
LANES = 128
NORM_EPS = 1e-6
LN_EPS = 1e-5
NEG = -1e30
N_DEV = 8
ADAM_LR, ADAM_B1, ADAM_B2, ADAM_EPS, ADAM_WD, ADAM_STEP = 0.001, 0.9, 0.999, 1e-08, 0.01, 10
VMEM_LIMIT = 48 * 1024 * 1024
BIG_VMEM_LIMIT = 58 * 1024 * 1024
SLOPES = tuple(2.0 ** (-8.0 * (h + 1) / N_Q_HEADS) for h in range(N_Q_HEADS))
MESH = pl.DeviceIdType.MESH


def _params(sem, vmem_limit=VMEM_LIMIT):
    return pltpu.CompilerParams(dimension_semantics=sem, vmem_limit_bytes=vmem_limit)


def _sigmoid(v):
    return 1.0 / (1.0 + jnp.exp(-v))


def _silu_and_grad(v):
    s = _sigmoid(v)
    return v * s, s * (1.0 + v * (1.0 - s))


ANY_SPEC = pl.BlockSpec(memory_space=pl.ANY)


def _mesh_pos():
    x, y, c = lax.axis_index("x"), lax.axis_index("y"), lax.axis_index("c")
    return x, y, c, 4 * x + 2 * y + c


def _flipped(k, x, y, c):
    px = 1 - x if k & 4 else x
    py = 1 - y if k & 2 else y
    pc = 1 - c if k & 1 else c
    return (px, py, pc), 4 * px + 2 * py + pc


class _Exchange:
    def __init__(self, arrays, dests, flips=None):
        self.arrays, self.dests, self.n = list(arrays), list(dests), len(arrays)
        self.flips = [tuple(range(1, N_DEV)) if f is None else tuple(f)
                      for f in (flips if flips is not None else [None] * self.n)]

    def out_shapes(self):
        return [jax.ShapeDtypeStruct((N_DEV,) + a.shape[-2:], a.dtype) for a in self.arrays]

    def sem_shapes(self):
        return [pltpu.SemaphoreType.DMA((self.n, N_DEV - 1)), pltpu.SemaphoreType.DMA((self.n, N_DEV - 1)),
                pltpu.SemaphoreType.DMA((self.n,))]

    def _when(self, a, dev, fn):
        if self.dests[a] is None:
            fn()
        else:
            lo, hi = self.dests[a]
            pl.when((dev >= lo) & (dev < hi))(fn)

    def _mine(self, ins, a, dev):
        return ins[a] if self.dests[a] is None else ins[a].at[dev - self.dests[a][0]]

    def _copy(self, ins, outs, sems, a, k, src_dev, slot, target):
        return pltpu.make_async_remote_copy(
            src_ref=self._mine(ins, a, src_dev), dst_ref=outs[a].at[slot],
            send_sem=sems[0].at[a, k - 1], recv_sem=sems[1].at[a, k - 1],
            device_id=target, device_id_type=MESH)

    def start(self, ins, outs, sems):
        x, y, c, me = _mesh_pos()
        for a in range(self.n):
            self._when(a, me, lambda a=a: pltpu.make_async_copy(
                self._mine(ins, a, me), outs[a].at[me], sems[2].at[a]).start())
            for k in self.flips[a]:
                target, peer = _flipped(k, x, y, c)
                self._when(a, peer, lambda a=a, k=k, target=target, peer=peer: self._copy(
                    ins, outs, sems, a, k, peer, me, target).start())

    def finish(self, ins, outs, sems):
        x, y, c, me = _mesh_pos()
        lo0 = [0 if d is None else d[0] for d in self.dests]
        for a in range(self.n):
            for k in self.flips[a]:
                target, peer = _flipped(k, x, y, c)
                self._when(a, me, lambda a=a, k=k, peer=peer: self._copy(
                    ins, outs, sems, a, k, lo0[a], peer, (x, y, c)).wait_recv())
            for k in self.flips[a]:
                target, peer = _flipped(k, x, y, c)
                self._when(a, peer, lambda a=a, k=k, target=target, peer=peer: self._copy(
                    ins, outs, sems, a, k, peer, me, target).wait_send())
            self._when(a, me, lambda a=a: pltpu.make_async_copy(
                self._mine(ins, a, me), outs[a].at[me], sems[2].at[a]).wait())


def _exchange(ex, name):
    na = ex.n

    def body(*refs):
        ins, outs, sems = refs[:na], refs[na:2 * na], refs[2 * na:]
        ex.start(ins, outs, sems)
        ex.finish(ins, outs, sems)

    return pl.pallas_call(
        body, name=name, out_shape=tuple(ex.out_shapes()),
        in_specs=[ANY_SPEC] * na, out_specs=tuple([ANY_SPEC] * na), scratch_shapes=ex.sem_shapes(),
    )(*ex.arrays)


def _chip_sum(pieces, lo, name):
    n, R, C = pieces.shape

    def swap(p_ref, t_ref, send_sems, recv_sems):
        x, y, c, me = _mesh_pos()
        for i in range(n):
            mine = (lo + i) % 2
            cp = pltpu.make_async_remote_copy(
                src_ref=p_ref.at[i], dst_ref=t_ref.at[i], send_sem=send_sems.at[i], recv_sem=recv_sems.at[i],
                device_id=(x, y, 1 - c), device_id_type=MESH)
            pl.when(c != mine)(cp.start)
        for i in range(n):
            mine = (lo + i) % 2
            cp = pltpu.make_async_remote_copy(
                src_ref=p_ref.at[i], dst_ref=t_ref.at[i], send_sem=send_sems.at[i], recv_sem=recv_sems.at[i],
                device_id=(x, y, 1 - c), device_id_type=MESH)
            pl.when(c == mine)(cp.wait_recv)
            pl.when(c != mine)(cp.wait_send)

    other = pl.pallas_call(
        swap, name=name + "_swap", out_shape=jax.ShapeDtypeStruct(pieces.shape, pieces.dtype),
        in_specs=[ANY_SPEC], out_specs=ANY_SPEC,
        scratch_shapes=[pltpu.SemaphoreType.DMA((n,)), pltpu.SemaphoreType.DMA((n,))],
    )(pieces)

    def add(p_ref, t_ref, o_ref):
        o_ref[...] = (p_ref[...].astype(F32) + t_ref[...].astype(F32)).astype(o_ref.dtype)

    tr = R // 2
    blk = pl.BlockSpec((None, tr, C), lambda i, r: (i, r, 0))
    return pl.pallas_call(
        add, name=name + "_add", grid=(n, R // tr), in_specs=[blk, blk], out_specs=blk,
        out_shape=jax.ShapeDtypeStruct(pieces.shape, pieces.dtype),
        compiler_params=_params(("parallel", "parallel")),
    )(pieces, other)


class _Gather:
    def __init__(self, arrays):
        self.arrays, self.n = list(arrays), len(arrays)

    def out_shapes(self):
        return [jax.ShapeDtypeStruct((N_DEV,) + a.shape, a.dtype) for a in self.arrays]

    def sem_shapes(self):
        return [pltpu.SemaphoreType.DMA((self.n, N_DEV - 1)), pltpu.SemaphoreType.DMA((self.n, N_DEV - 1)),
                pltpu.SemaphoreType.DMA((self.n,))]

    def _plan(self, ins, outs, sems):
        x, y, c, me = _mesh_pos()
        chips = [(1 - x, y), (x, 1 - y), (1 - x, 1 - y)]

        def copy(a, k, src, block, to):
            px, py, pc = block
            return pltpu.make_async_remote_copy(
                src_ref=src, dst_ref=outs[a].at[4 * px + 2 * py + pc], send_sem=sems[0].at[a, k],
                recv_sem=sems[1].at[a, k], device_id=to, device_id_type=MESH)

        def landed(a, block):
            px, py, pc = block
            return outs[a].at[4 * px + 2 * py + pc]

        local = [pltpu.make_async_copy(ins[a], outs[a].at[me], sems[2].at[a]) for a in range(self.n)]
        first = []
        for a in range(self.n):
            first.append(copy(a, 0, ins[a], (x, y, c), (x, y, 1 - c)))
            first += [copy(a, 1 + j, ins[a], (x, y, c), (*chip, c)) for j, chip in enumerate(chips[:2])]
        return (x, y, c), chips, copy, landed, local, first

    def start(self, ins, outs, sems):
        *_, local, first = self._plan(ins, outs, sems)
        for cp in local + first:
            cp.start()

    def finish(self, ins, outs, sems):
        (x, y, c), chips, copy, landed, local, first = self._plan(ins, outs, sems)
        south = c == 0
        came = (jnp.where(south, 1 - x, x), jnp.where(south, y, 1 - y), c)
        goes = (jnp.where(south, x, 1 - x), jnp.where(south, 1 - y, y), c)
        passed = []
        for a in range(self.n):
            for j, chip in enumerate(chips[:2]):
                copy(a, 1 + j, ins[a], (*chip, c), (x, y, c)).wait_recv()
            passed.append(copy(a, 3, landed(a, came), came, goes))
            passed += [copy(a, 4 + j, landed(a, (*chip, c)), (*chip, c), (x, y, 1 - c))
                       for j, chip in enumerate(chips[:2])]
        for cp in passed:
            cp.start()
        for a in range(self.n):
            diagonal = (*chips[2], c)
            copy(a, 3, ins[a], diagonal, (x, y, c)).wait_recv()
            cp = copy(a, 6, landed(a, diagonal), diagonal, (x, y, 1 - c))
            cp.start()
            passed.append(cp)
        for a in range(self.n):
            copy(a, 0, ins[a], (x, y, 1 - c), (x, y, c)).wait_recv()
            for j, chip in enumerate(chips):
                copy(a, 4 + j, ins[a], (*chip, 1 - c), (x, y, c)).wait_recv()
        for cp in first + passed:
            cp.wait_send()
        for cp in local:
            cp.wait()


CHUNK = 128
RESIDUES = 16
PER_RES = CHUNK // RESIDUES


def _perm_rows(tile, inverse):
    a = lax.broadcasted_iota(jnp.int32, (CHUNK, CHUNK), 0)
    b = lax.broadcasted_iota(jnp.int32, (CHUNK, CHUNK), 1)
    if inverse:
        a, b = b, a
    p = jnp.where(a == PER_RES * (b % RESIDUES) + b // RESIDUES, 1.0, 0.0).astype(BF16)
    parts = [jnp.dot(p, tile[c * CHUNK:(c + 1) * CHUNK], preferred_element_type=F32)
             for c in range(tile.shape[0] // CHUNK)]
    return jnp.concatenate(parts, axis=0).astype(BF16)


class _Rows:
    def __init__(self, dil, S):
        nc = S // CHUNK
        self.dil = dil
        if dil == 1:
            self.view, self.block, self.nb = (nc, CHUNK), (None, CHUNK), nc
            self.index = lambda r, b: (b, 0, 0)
        elif dil == 4:
            self.view, self.block, self.nb = (nc, 4, 4, PER_RES), (4, 4, None, PER_RES), nc // 4
            self.index = lambda r, b: (b, 0, r, 0, 0)
        elif dil == RESIDUES:
            self.view, self.block, self.nb = (nc, RESIDUES, PER_RES), (RESIDUES, None, PER_RES), nc // RESIDUES
            self.index = lambda r, b: (b, r, 0, 0)
        else:
            raise NotImplementedError(dil)

    def of(self, a):
        return a.reshape(self.view + (a.shape[-1],))

    def spec(self, width, which_block):
        return pl.BlockSpec(self.block + (width,), lambda r, n: self.index(r, which_block(n)))

    def pos(self, row):
        if self.dil == 1:
            return (row % PER_RES) * RESIDUES + row // PER_RES
        if self.dil == 4:
            return (row // 32) * 32 + (row % PER_RES) * 4 + (row % 32) // PER_RES
        return row


def _ld(ref, cols=slice(None)):
    v = ref[(slice(None),) * (len(ref.shape) - 1) + (cols,)]
    return v.reshape(BLK, v.shape[-1])


def _st(ref, val, cols=slice(None)):
    ref[(slice(None),) * (len(ref.shape) - 1) + (cols,)] = val.reshape(ref.shape[:-1] + (val.shape[-1],))


def _norm_rows(x, g, hosted=None, tm=512):
    S, D = x.shape
    hn = hosted.n if hosted is not None else 0

    def body(x_ref, g_ref, *rest):
        h_ins = rest[:hn]
        hrm_out, h_out = rest[hn:hn + 2]
        h_outs = rest[hn + 2:2 * hn + 2]
        h_sems = rest[2 * hn + 2:]
        i = pl.program_id(0)
        if hosted is not None:
            pl.when(i == 0)(lambda: hosted.start(h_ins, h_outs, h_sems))
        xf = x_ref[...]
        r = lax.rsqrt(jnp.mean(xf * xf, axis=-1, keepdims=True) + NORM_EPS)
        h = (xf * r * g_ref[...]).astype(BF16)
        h_out[...] = h
        hrm_out[...] = _perm_rows(h, False)
        if hosted is not None:
            pl.when(i == S // tm - 1)(lambda: hosted.finish(h_ins, h_outs, h_sems))

    row = pl.BlockSpec((tm, D), lambda i: (i, 0))
    in_specs, args = [row, pl.BlockSpec((1, D), lambda i: (0, 0))], [x, g]
    out_specs, out_shape, scratch = [row, row], [jax.ShapeDtypeStruct((S, D), BF16)] * 2, []
    if hosted is not None:
        in_specs += [ANY_SPEC] * hn
        args += hosted.arrays
        out_specs += [ANY_SPEC] * hn
        out_shape += hosted.out_shapes()
        scratch += hosted.sem_shapes()
    return pl.pallas_call(
        body, name="norm_rows", grid=(S // tm,),
        in_specs=in_specs, out_specs=tuple(out_specs), out_shape=tuple(out_shape), scratch_shapes=scratch,
        compiler_params=_params(("arbitrary",)),
    )(*args)


def _inproj(h_rm, h, w_t, segments, hosted=None, tm=1024, tn=512):
    S, D = h.shape
    ns = len(segments)
    ni = S // tm
    counts = [seg[0] // tn for seg in segments]
    starts = [sum(counts[:s]) for s in range(ns)]

    hn = hosted.n if hosted is not None else 0
    last_p = sum(counts)

    def body(hrm_ref, h_ref, w_ref, *rest):
        h_ins, rest = rest[:hn], rest[hn:]
        outs = rest[:ns]
        h_outs = rest[ns:ns + hn]
        hrm_scr, h_scr = rest[ns + hn:ns + 2 + hn]
        h_sems = rest[ns + 2 + hn:]
        p, i = pl.program_id(0), pl.program_id(1)

        if hosted is not None:
            @pl.when((p == 0) & (i == 0))
            def _():
                hosted.start(h_ins, h_outs, h_sems)

            @pl.when((p == last_p) & (i == ni - 1))
            def _():
                hosted.finish(h_ins, h_outs, h_sems)

        @pl.when(p == 0)
        def _():
            h_scr[i] = h_ref[...]
            hrm_scr[i] = hrm_ref[...]

        for s, (_, scale, rm) in enumerate(segments):
            @pl.when((p > starts[s]) & (p <= starts[s] + counts[s]))
            def _(s=s, scale=scale, rm=rm):
                acc = _nt((hrm_scr if rm else h_scr)[i], w_ref[...])
                outs[s][...] = acc * scale if scale != 1.0 else acc

    def out_index(s):
        def index(p, i):
            j = p - 1 - starts[s]
            row = jnp.where(j < 0, 0, jnp.where(j >= counts[s], ni - 1, i))
            return row, jnp.clip(j, 0, counts[s] - 1)
        return index

    first_pass = pl.BlockSpec((tm, D), lambda p, i: (jnp.where(p == 0, i, ni - 1), 0))
    out_specs = [pl.BlockSpec((tm, tn), out_index(s)) for s in range(ns)]
    out_shape = [jax.ShapeDtypeStruct((S, seg[0]), F32) for seg in segments]
    in_specs = [first_pass, first_pass, pl.BlockSpec((tn, D), lambda p, i: (jnp.maximum(p - 1, 0), 0))]
    args = [h_rm, h, w_t]
    scratch = [pltpu.VMEM((ni, tm, D), BF16), pltpu.VMEM((ni, tm, D), BF16)]
    if hosted is not None:
        in_specs += [ANY_SPEC] * hn
        args += hosted.arrays
        out_specs += [ANY_SPEC] * hn
        out_shape += hosted.out_shapes()
        scratch += hosted.sem_shapes()
    return pl.pallas_call(
        body, name="inproj", grid=(1 + last_p, ni),
        in_specs=in_specs, out_specs=tuple(out_specs), out_shape=tuple(out_shape), scratch_shapes=scratch,
        compiler_params=_params(("arbitrary", "arbitrary"), BIG_VMEM_LIMIT),
    )(*args)


def _fill_bias_table(tbl, rows, keys_first=False):
    shape = (2 * BLK, BLK) if keys_first else (BLK, 2 * BLK)
    qi = lax.broadcasted_iota(jnp.int32, shape, 1 if keys_first else 0)
    kj = lax.broadcasted_iota(jnp.int32, shape, 0 if keys_first else 1)
    dist = rows.pos(qi) - rows.pos(kj % BLK) + jnp.where(kj < BLK, BLK, 0)
    inside = (dist >= 0) & (dist <= BLK)
    negd = (dist * (-rows.dil)).astype(F32)
    for f, valid in enumerate((inside & (kj >= BLK), inside)):
        for h in range(N_Q_HEADS):
            tbl[f * N_Q_HEADS + h] = jnp.where(valid, SLOPES[h] * negd, NEG)


def _bias2(tbl, n, h0, h1, axis=0):
    base = jnp.where(n == 0, 0, N_Q_HEADS)
    return jnp.concatenate([tbl[base + h0], tbl[base + h1]], axis=axis)


def _head_operands(kv2, hk, lo_mask):
    half, pos = hk // 2, hk % 2
    out = []
    for base in (0, KV_W):
        t = kv2[:, base + half * LANES: base + (half + 1) * LANES]
        sw = pltpu.roll(t, HEAD_DIM, axis=1)
        at_lo, at_hi = (t, sw) if pos == 0 else (sw, t)
        out.append(jnp.where(lo_mask, at_lo, 0.0).astype(BF16))
        out.append(jnp.where(lo_mask, 0.0, at_hi).astype(BF16))
    return out


def _nt(a, b):
    return lax.dot_general(a, b, (((1,), (1,)), ((), ())), preferred_element_type=F32)


def _tn(a, b):
    return lax.dot_general(a, b, (((0,), (0,)), ((), ())), preferred_element_type=F32)


def _attn_fwd(q, kv, dil, name, prev=(), gate=None):
    S = q.shape[0]
    rows = _Rows(dil, S)
    nb = rows.nb
    have_prev, last = len(prev) > 0, gate is not None

    def body(*refs):
        refs = list(refs)
        q_ref, kvc_ref, kvp_ref = refs[:3]
        del refs[:3]
        po_refs, pl_refs = refs[0:2 * len(prev):2], refs[1:2 * len(prev):2]
        del refs[:2 * len(prev)]
        if last:
            gate_ref = refs.pop(0)
        o_ref, lse_ref = refs[:2]
        y_ref = refs[2] if last else None
        tbl = refs[-1]
        n = pl.program_id(1)

        @pl.when((pl.program_id(0) == 0) & (n == 0))
        def _():
            _fill_bias_table(tbl, rows)

        kv2 = jnp.concatenate([_ld(kvp_ref), _ld(kvc_ref)], axis=0)
        lo_mask = lax.broadcasted_iota(jnp.int32, (2 * BLK, LANES), 1) < HEAD_DIM
        lane = lax.broadcasted_iota(jnp.int32, (BLK, LANES), 1)
        stats = jnp.zeros((BLK, LANES), F32)
        for hk in range(N_KV_HEADS):
            k_lo, k_hi, v_lo, v_hi = _head_operands(kv2, hk, lo_mask)
            cols = [slice(b * LANES, (b + 1) * LANES) for b in (2 * hk, 2 * hk + 1)]
            q2 = jnp.concatenate([_ld(q_ref, cols[0]), _ld(q_ref, cols[1])], axis=0).astype(BF16)
            o2 = jnp.zeros((2 * BLK, LANES), F32)
            for which, (kk, vv) in enumerate(((k_lo, v_lo), (k_hi, v_hi))):
                h0, h1 = 4 * hk + which, 4 * hk + 2 + which
                s = _nt(q2, kk) + _bias2(tbl, n, h0, h1)
                m = jnp.max(s, axis=1, keepdims=True)
                p = jnp.exp(s - m)
                l = jnp.sum(p, axis=1, keepdims=True)
                o2 = o2 + jnp.dot(p.astype(BF16), vv, preferred_element_type=F32) * (1.0 / l)
                lse = m + jnp.log(l)
                stats = jnp.where(lane == h0, lse[0:BLK], stats)
                stats = jnp.where(lane == h1, lse[BLK:], stats)
            _st(o_ref, o2[0:BLK], cols[0])
            _st(o_ref, o2[BLK:], cols[1])
        if have_prev:
            others = [_ld(r) for r in pl_refs]
            top = stats
            for b in others:
                top = jnp.maximum(top, b)
            e_new = jnp.exp(stats - top)
            e_old = [jnp.exp(b - top) for b in others]
            total = e_new
            for e in e_old:
                total = total + e
            stats = top + jnp.log(total)
            inv = 1.0 / total
            w_new, w_old = e_new * inv, [e * inv for e in e_old]
        if have_prev or last:
            lo = lane < HEAD_DIM
            for blk in range(ATT_W // LANES):
                cols = slice(blk * LANES, (blk + 1) * LANES)
                o_blk = _ld(o_ref, cols)
                if have_prev:
                    pick = lambda w: jnp.where(lo, w[:, 2 * blk:2 * blk + 1], w[:, 2 * blk + 1:2 * blk + 2])
                    o_blk = o_blk * pick(w_new)
                    for po_ref, w in zip(po_refs, w_old):
                        o_blk = o_blk + _ld(po_ref, cols) * pick(w)
                    _st(o_ref, o_blk, cols)
                if last:
                    a = _ld(gate_ref, cols)
                    _st(y_ref, (o_blk * (a * _sigmoid(a))).astype(BF16), cols)
        _st(lse_ref, stats)

    here = lambda n: n
    before_n = lambda n: jnp.maximum(n - 1, 0)
    in_specs = [rows.spec(ATT_W, here), rows.spec(2 * KV_W, here), rows.spec(2 * KV_W, before_n)]
    args = [rows.of(q), rows.of(kv), rows.of(kv)]
    for o_other, lse_other in prev:
        in_specs += [rows.spec(ATT_W, here), rows.spec(LANES, here)]
        args += [rows.of(o_other), rows.of(lse_other)]
    out_specs = [rows.spec(ATT_W, here), rows.spec(LANES, here)]
    out_shape = [jax.ShapeDtypeStruct(rows.view + (ATT_W,), F32), jax.ShapeDtypeStruct(rows.view + (LANES,), F32)]
    if last:
        in_specs.append(rows.spec(ATT_W, here))
        args.append(rows.of(gate))
        out_specs.append(rows.spec(ATT_W, here))
        out_shape.append(jax.ShapeDtypeStruct(rows.view + (ATT_W,), BF16))
    res = pl.pallas_call(
        body, name=name, grid=(dil, nb),
        in_specs=in_specs, out_specs=tuple(out_specs), out_shape=tuple(out_shape),
        scratch_shapes=[pltpu.VMEM((2 * N_Q_HEADS, BLK, 2 * BLK), F32)],
        compiler_params=_params(("arbitrary", "arbitrary")),
    )(*args)
    return tuple(r.reshape(S, r.shape[-1]) for r in res)


def _shifted_copies(buf, phases):
    n = phases.shape[1]
    for b in range(1, 8):
        phases[b - 1] = buf[b:b + n, :]


def _window(buf, phases, start, cols):
    b = start % 8
    if b == 0:
        return buf[start:start + 8, cols]
    return phases[b - 1, start - b:start - b + 8, cols]


def _broadcast_taps(w_ref, wb):
    for j in range(CONV_K):
        wb[j] = jnp.broadcast_to(w_ref[j:j + 1, :], wb.shape[1:])


def _conv_fwd(gates, conv_w, conv_b, ln_g, ln_b, tt=256):
    S = gates.shape[0]
    C = conv_w.shape[1]
    hb = tt // CONV_HALO

    def body(val_ref, glu_ref, hval_ref, hglu_ref, gate_ref, w_ref, b_ref, g_ref, beta_ref,
             conv_ref, y_ref, hbuf, hph):
        i = pl.program_id(0)
        halo = hval_ref[...] * _sigmoid(hglu_ref[...])
        hbuf[0:CONV_HALO, :] = jnp.where(i > 0, halo, 0.0)
        hbuf[CONV_HALO:, :] = val_ref[...] * _sigmoid(glu_ref[...])
        _shifted_copies(hbuf, hph)
        for cb in range(C // LANES):
            cols = slice(cb * LANES, (cb + 1) * LANES)
            wj = [jnp.broadcast_to(w_ref[j:j + 1, cols], (8, LANES)) for j in range(CONV_K)]
            for rc in range(tt // 8):
                acc = jnp.zeros((8, LANES), F32)
                for j in range(CONV_K):
                    start = rc * 8 + CONV_HALO - (CONV_K - 1) + j
                    acc = acc + _window(hbuf, hph, start, cols) * wj[j]
                conv_ref[rc * 8:(rc + 1) * 8, cols] = acc
        cv = conv_ref[...] + b_ref[...]
        conv_ref[...] = cv
        mu = jnp.mean(cv, axis=-1, keepdims=True)
        xc = cv - mu
        var = jnp.mean(xc * xc, axis=-1, keepdims=True)
        ln = xc * lax.rsqrt(var + LN_EPS) * g_ref[...] + beta_ref[...]
        gt = gate_ref[...]
        y_ref[...] = (ln * _sigmoid(ln) * (gt * _sigmoid(gt))).astype(BF16)

    vec = pl.BlockSpec((1, C), lambda i: (0, 0))
    return pl.pallas_call(
        body, name="conv_fwd", grid=(S // tt,),
        in_specs=[pl.BlockSpec((tt, C), lambda i: (i, 0)),
                  pl.BlockSpec((tt, C), lambda i: (i, 1)),
                  pl.BlockSpec((CONV_HALO, C), lambda i: (jnp.maximum(i * hb - 1, 0), 0)),
                  pl.BlockSpec((CONV_HALO, C), lambda i: (jnp.maximum(i * hb - 1, 0), 1)),
                  pl.BlockSpec((tt, C), lambda i: (i, 2)),
                  pl.BlockSpec((CONV_HALO, C), lambda i: (0, 0)), vec, vec, vec],
        out_specs=(pl.BlockSpec((tt, C), lambda i: (i, 0)), pl.BlockSpec((tt, C), lambda i: (i, 0))),
        out_shape=(jax.ShapeDtypeStruct((S, C), F32), jax.ShapeDtypeStruct((S, C), BF16)),
        scratch_shapes=[pltpu.VMEM((tt + CONV_HALO, C), F32), pltpu.VMEM((7, tt + CONV_HALO - 8, C), F32)],
        compiler_params=_params(("parallel",)),
    )(gates, gates, gates, gates, gates, conv_w, conv_b, ln_g, ln_b)


def _outproj_loss(x, y_att, y_conv, w_out, gf, target, tm=512):
    S, D = x.shape
    E = y_att.shape[1]

    def body(x_ref, ya_ref, yc_ref, w_ref, gf_ref, t_ref, dx_ref, dxb_ref, loss_ref, ggf_ref):
        @pl.when(pl.program_id(0) == 0)
        def _():
            loss_ref[...] = jnp.zeros_like(loss_ref)
            ggf_ref[...] = jnp.zeros_like(ggf_ref)

        x2 = (x_ref[...] + jnp.dot(_perm_rows(ya_ref[...], True), w_ref[0:E, :], preferred_element_type=F32)
              + jnp.dot(yc_ref[...], w_ref[E:, :], preferred_element_type=F32))
        r = lax.rsqrt(jnp.mean(x2 * x2, axis=-1, keepdims=True) + NORM_EPS)
        nrm = x2 * r
        gfv = gf_ref[...]
        err = nrm * gfv - t_ref[...]
        loss_ref[...] += jnp.sum(err * err, axis=0, keepdims=True)
        dout = err * (1.0 / D)
        ggf_ref[...] += jnp.sum(dout * nrm, axis=0, keepdims=True)
        dn = dout * gfv
        dx2 = r * (dn - nrm * jnp.mean(dn * nrm, axis=-1, keepdims=True))
        dx_ref[...] = dx2
        dxb_ref[...] = dx2.astype(BF16)

    row = lambda w: pl.BlockSpec((tm, w), lambda i: (i, 0))
    vec = pl.BlockSpec((1, D), lambda i: (0, 0))
    return pl.pallas_call(
        body, name="outproj_loss", grid=(S // tm,),
        in_specs=[row(D), row(E), row(E), pl.BlockSpec((2 * E, D), lambda i: (0, 0)), vec, row(D)],
        out_specs=(row(D), row(D), vec, vec),
        out_shape=(jax.ShapeDtypeStruct((S, D), F32), jax.ShapeDtypeStruct((S, D), BF16),
                   jax.ShapeDtypeStruct((1, D), F32), jax.ShapeDtypeStruct((1, D), F32)),
        compiler_params=_params(("arbitrary",)),
    )(x, y_att, y_conv, w_out, gf, target)


def _split3(v):
    hi = v.astype(BF16)
    r1 = v - hi.astype(F32)
    mid = r1.astype(BF16)
    lo = (r1 - mid.astype(F32)).astype(BF16)
    return hi, mid, lo


def _dy_att(dxb, w_out, gates, o, tm=512):
    S, D = dxb.shape
    E = ATT_W

    def body(dx_ref, w_ref, a_ref, o_ref, do_ref, da_ref, dl_ref, dxr_ref):
        dxr = _perm_rows(dx_ref[...], False)
        dxr_ref[...] = dxr
        dya = _nt(dxr, w_ref[...])
        a = a_ref[...]
        ov = o_ref[...]
        sl, dsl = _silu_and_grad(a)
        d_o = dya * sl
        do_ref[...] = d_o
        da_ref[...] = (dya * ov * dsl).astype(BF16)
        ci = lax.broadcasted_iota(jnp.int32, (E, LANES), 0) // HEAD_DIM
        hi = lax.broadcasted_iota(jnp.int32, (E, LANES), 1)
        sel = jnp.where(ci == hi, 1.0, 0.0).astype(BF16)
        acc = jnp.zeros((tm, LANES), F32)
        for part in _split3(d_o * ov):
            acc = acc + jnp.dot(part, sel, preferred_element_type=F32)
        dl_ref[...] = acc

    row = lambda w: pl.BlockSpec((tm, w), lambda i: (i, 0))
    return pl.pallas_call(
        body, name="dy_att", grid=(S // tm,),
        in_specs=[row(D), pl.BlockSpec((E, D), lambda i: (0, 0)), row(E), row(E)],
        out_specs=(row(E), row(E), row(LANES), row(D)),
        out_shape=(jax.ShapeDtypeStruct((S, E), F32), jax.ShapeDtypeStruct((S, E), BF16),
                   jax.ShapeDtypeStruct((S, LANES), F32), jax.ShapeDtypeStruct((S, D), BF16)),
        compiler_params=_params(("parallel",)),
    )(dxb, w_out, gates, o)


def _dy_conv(dxb, w_out, gates, conv_out, ln_g, ln_b, tm=512):
    S, D = dxb.shape
    C = conv_out.shape[1]

    def body(dx_ref, w_ref, gate_ref, cv_ref, g_ref, beta_ref, dgate_ref, dconv_ref, gg_ref, gb_ref, gcb_ref):
        @pl.when(pl.program_id(0) == 0)
        def _():
            gg_ref[...] = jnp.zeros_like(gg_ref)
            gb_ref[...] = jnp.zeros_like(gb_ref)
            gcb_ref[...] = jnp.zeros_like(gcb_ref)

        dyc = _nt(dx_ref[...], w_ref[...])
        cv = cv_ref[...]
        mu = jnp.mean(cv, axis=-1, keepdims=True)
        xc = cv - mu
        rstd = lax.rsqrt(jnp.mean(xc * xc, axis=-1, keepdims=True) + LN_EPS)
        nrm = xc * rstd
        gv = g_ref[...]
        ln = nrm * gv + beta_ref[...]
        u, du = _silu_and_grad(ln)
        gt = gate_ref[...]
        g2, dg2 = _silu_and_grad(gt)
        dgate_ref[...] = (dyc * u * dg2).astype(BF16)
        d_ln = dyc * g2 * du
        gb_ref[...] += jnp.sum(d_ln, axis=0, keepdims=True)
        gg_ref[...] += jnp.sum(d_ln * nrm, axis=0, keepdims=True)
        dn = d_ln * gv
        d_conv = rstd * (dn - jnp.mean(dn, axis=-1, keepdims=True)
                         - nrm * jnp.mean(dn * nrm, axis=-1, keepdims=True))
        dconv_ref[...] = d_conv
        gcb_ref[...] += jnp.sum(d_conv, axis=0, keepdims=True)

    row = lambda w: pl.BlockSpec((tm, w), lambda i: (i, 0))
    vec = pl.BlockSpec((1, C), lambda i: (0, 0))
    return pl.pallas_call(
        body, name="dy_conv", grid=(S // tm,),
        in_specs=[row(D), pl.BlockSpec((C, D), lambda i: (1, 0)),
                  pl.BlockSpec((tm, C), lambda i: (i, 2)), row(C), vec, vec],
        out_specs=(row(C), row(C), vec, vec, vec),
        out_shape=(jax.ShapeDtypeStruct((S, C), BF16), jax.ShapeDtypeStruct((S, C), F32),
                   jax.ShapeDtypeStruct((1, C), F32), jax.ShapeDtypeStruct((1, C), F32),
                   jax.ShapeDtypeStruct((1, C), F32)),
        compiler_params=_params(("arbitrary",)),
    )(dxb, w_out, gates, conv_out, ln_g, ln_b)


def _conv_bwd(d_conv, gates, d_c_gate, conv_w, hosted=None, tt=256):
    S, C = d_conv.shape
    hb = tt // CONV_HALO
    nt = S // tt
    hn = hosted.n if hosted is not None else 0

    def body(*refs):
        dc_ref, dnext_ref, val_ref, glu_ref, dg_ref, w_ref = refs[:6]
        h_ins = refs[6:6 + hn]
        out_ref, gw_ref = refs[6 + hn:8 + hn]
        h_outs = refs[8 + hn:8 + 2 * hn]
        hbuf, dbuf, dhbuf, dph, wb = refs[8 + 2 * hn:13 + 2 * hn]
        h_sems = refs[13 + 2 * hn:]
        i = pl.program_id(0)

        @pl.when(i == 0)
        def _():
            gw_ref[...] = jnp.zeros_like(gw_ref)
            _broadcast_taps(w_ref, wb)
            if hosted is not None:
                hosted.start(h_ins, h_outs, h_sems)

        val = val_ref[...]
        sg = _sigmoid(glu_ref[...])
        hbuf[...] = val * sg
        dbuf[0:tt, :] = dc_ref[...]
        dbuf[tt:, :] = jnp.where(i < nt - 1, dnext_ref[...], 0.0)
        _shifted_copies(dbuf, dph)
        for cb in range(C // LANES):
            cols = slice(cb * LANES, (cb + 1) * LANES)
            gacc = [jnp.zeros((8, LANES), F32) for _ in range(CONV_K)]
            group = 2
            for rc0 in range(0, tt // 8, group):
                hcur = [hbuf[(rc0 + r) * 8:(rc0 + r + 1) * 8, cols] for r in range(group)]
                accs = [jnp.zeros((8, LANES), F32) for _ in range(group)]
                for j in range(CONV_K):
                    wj = wb[j, :, cols]
                    for r in range(group):
                        dwin = _window(dbuf, dph, (rc0 + r) * 8 + (CONV_K - 1) - j, cols)
                        accs[r] = accs[r] + dwin * wj
                        gacc[j] = gacc[j] + dwin * hcur[r]
                for r in range(group):
                    dhbuf[(rc0 + r) * 8:(rc0 + r + 1) * 8, cols] = accs[r]
            for j in range(CONV_K):
                gw_ref[j:j + 1, cols] += jnp.sum(gacc[j], axis=0, keepdims=True)
        d_h = dhbuf[...]
        out_ref[:, 0:C] = (d_h * sg).astype(BF16)
        out_ref[:, C:2 * C] = (d_h * val * sg * (1.0 - sg)).astype(BF16)
        out_ref[:, 2 * C:3 * C] = dg_ref[...]

        if hosted is not None:
            @pl.when(i == nt - 1)
            def _():
                hosted.finish(h_ins, h_outs, h_sems)

    tile = lambda col: pl.BlockSpec((tt, C), lambda i: (i, col))
    in_specs = [tile(0),
                pl.BlockSpec((CONV_HALO, C), lambda i: (jnp.minimum((i + 1) * hb, S // CONV_HALO - 1), 0)),
                tile(0), tile(1), tile(0),
                pl.BlockSpec((CONV_HALO, C), lambda i: (0, 0))]
    args = [d_conv, d_conv, gates, gates, d_c_gate, conv_w]
    out_specs = [pl.BlockSpec((tt, 3 * C), lambda i: (i, 0)), pl.BlockSpec((CONV_HALO, C), lambda i: (0, 0))]
    out_shape = [jax.ShapeDtypeStruct((S, 3 * C), BF16), jax.ShapeDtypeStruct((CONV_HALO, C), F32)]
    scratch = [pltpu.VMEM((tt, C), F32), pltpu.VMEM((tt + CONV_HALO, C), F32), pltpu.VMEM((tt, C), F32),
               pltpu.VMEM((7, tt + CONV_HALO - 8, C), F32), pltpu.VMEM((CONV_K, 8, C), F32)]
    if hosted is not None:
        in_specs += [ANY_SPEC] * hn
        args += hosted.arrays
        out_specs += [ANY_SPEC] * hn
        out_shape += hosted.out_shapes()
        scratch += hosted.sem_shapes()
    res = pl.pallas_call(
        body, name="conv_bwd", grid=(nt,),
        in_specs=in_specs, out_specs=tuple(out_specs), out_shape=tuple(out_shape), scratch_shapes=scratch,
        compiler_params=_params(("arbitrary",)),
    )(*args)
    return res[0], res[1], list(res[2:])


def _attn_bwd(q, kv, d_o, lse, delta, dil, prev, final, name, hosted=None):
    S = q.shape[0]
    rows = _Rows(dil, S)
    nb = rows.nb
    steps = dil * nb
    out_dt = BF16 if final else F32
    have_prev = prev is not None
    hn = hosted.n if hosted is not None else 0

    def body(*refs):
        refs = list(refs)
        q_ref, do_ref, lse_ref, dl_ref, kvc_ref, kvp_ref = refs[:6]
        del refs[:6]
        if have_prev:
            pdq_ref, pdkv_ref = refs[:2]
            del refs[:2]
        h_ins = refs[:hn]
        dq_ref, dkv_ref = refs[hn:hn + 2]
        h_outs = refs[hn + 2:2 * hn + 2]
        carry, tbl = refs[2 * hn + 2:2 * hn + 4]
        h_sems = refs[2 * hn + 4:]
        t = pl.program_id(0)
        n = t % nb

        @pl.when(t == 0)
        def _():
            if hosted is not None:
                hosted.start(h_ins, h_outs, h_sems)
            _fill_bias_table(tbl, rows, keys_first=True)
            carry[...] = jnp.zeros_like(carry)

        @pl.when(t < steps)
        def _():
            kv2 = jnp.concatenate([_ld(kvp_ref), _ld(kvc_ref)], axis=0)
            lse_t, dl_t = _ld(lse_ref).T, _ld(dl_ref).T
            lo_mask = lax.broadcasted_iota(jnp.int32, (2 * BLK, LANES), 1) < HEAD_DIM
            halves = [jnp.zeros((2 * BLK, LANES), F32) for _ in range(4)]
            for hk in range(N_KV_HEADS):
                k_lo, k_hi, v_lo, v_hi = _head_operands(kv2, hk, lo_mask)
                cols = [slice(b * LANES, (b + 1) * LANES) for b in (2 * hk, 2 * hk + 1)]
                q2 = jnp.concatenate([_ld(q_ref, cols[0]), _ld(q_ref, cols[1])], axis=0).astype(BF16)
                do2 = jnp.concatenate([_ld(do_ref, cols[0]), _ld(do_ref, cols[1])], axis=0).astype(BF16)
                dq2 = jnp.zeros((2 * BLK, LANES), F32)
                dks, dvs = [], []
                for which, (kk, vv) in enumerate(((k_lo, v_lo), (k_hi, v_hi))):
                    h0, h1 = 4 * hk + which, 4 * hk + 2 + which
                    s = _nt(kk, q2) + _bias2(tbl, n, h0, h1, axis=1)
                    lse2 = jnp.concatenate([lse_t[h0:h0 + 1, :], lse_t[h1:h1 + 1, :]], axis=1)
                    dl2 = jnp.concatenate([dl_t[h0:h0 + 1, :], dl_t[h1:h1 + 1, :]], axis=1)
                    p = jnp.exp(s - lse2)
                    ds = (p * (_nt(vv, do2) - dl2)).astype(BF16)
                    dq2 = dq2 + _tn(ds, kk)
                    dks.append(jnp.dot(ds, q2, preferred_element_type=F32))
                    dvs.append(jnp.dot(p.astype(BF16), do2, preferred_element_type=F32))
                dk_sum = jnp.where(lo_mask, dks[0], dks[1])
                dv_sum = jnp.where(lo_mask, dvs[0], dvs[1])
                for jp in range(2):
                    dq_blk = dq2[jp * BLK:(jp + 1) * BLK]
                    if have_prev:
                        dq_blk = dq_blk + _ld(pdq_ref, cols[jp])
                    if final:
                        dq_blk = dq_blk * (HEAD_DIM ** -0.5)
                    _st(dq_ref, dq_blk.astype(out_dt), cols[jp])
                half, pos = hk // 2, hk % 2
                here = lo_mask if pos == 0 else jnp.logical_not(lo_mask)
                dk_tot = dk_sum + pltpu.roll(dk_sum, HEAD_DIM, axis=1)
                dv_tot = dv_sum + pltpu.roll(dv_sum, HEAD_DIM, axis=1)
                halves[half] = halves[half] + jnp.where(here, dk_tot, 0.0)
                halves[2 + half] = halves[2 + half] + jnp.where(here, dv_tot, 0.0)
            for b in range(4):
                cols = slice(b * LANES, (b + 1) * LANES)
                done = carry[:, cols] + halves[b][0:BLK, :]
                if have_prev:
                    done = done + _ld(pdkv_ref, cols)
                _st(dkv_ref, done.astype(out_dt), cols)
                carry[:, cols] = halves[b][BLK:, :]

        @pl.when(t == steps)
        def _():
            done = carry[...]
            if have_prev:
                done = done + _ld(pdkv_ref)
            _st(dkv_ref, done.astype(out_dt))
            if hosted is not None:
                hosted.finish(h_ins, h_outs, h_sems)

    def spec(width, lag):
        def index(t):
            u = jnp.clip(t - lag, 0, steps - 1)
            return rows.index(u // nb, u % nb)
        return pl.BlockSpec(rows.block + (width,), index)

    def key_prev(t):
        u = jnp.minimum(t, steps - 1)
        return rows.index(u // nb, jnp.maximum(u % nb - 1, 0))

    in_specs = [spec(ATT_W, 0), spec(ATT_W, 0), spec(LANES, 0), spec(LANES, 0), spec(2 * KV_W, 0),
                pl.BlockSpec(rows.block + (2 * KV_W,), key_prev)]
    args = [rows.of(q), rows.of(d_o), rows.of(lse), rows.of(delta), rows.of(kv), rows.of(kv)]
    if have_prev:
        in_specs += [spec(ATT_W, 0), spec(2 * KV_W, 1)]
        args += [rows.of(prev[0]), rows.of(prev[1])]
    out_specs = [spec(ATT_W, 0), spec(2 * KV_W, 1)]
    out_shape = [jax.ShapeDtypeStruct(rows.view + (ATT_W,), out_dt),
                 jax.ShapeDtypeStruct(rows.view + (2 * KV_W,), out_dt)]
    scratch = [pltpu.VMEM((BLK, 2 * KV_W), F32), pltpu.VMEM((2 * N_Q_HEADS, 2 * BLK, BLK), F32)]
    if hosted is not None:
        in_specs += [ANY_SPEC] * hn
        args += hosted.arrays
        out_specs += [ANY_SPEC] * hn
        out_shape += hosted.out_shapes()
        scratch += hosted.sem_shapes()
    res = pl.pallas_call(
        body, name=name, grid=(steps + 1,),
        in_specs=in_specs, out_specs=tuple(out_specs), out_shape=tuple(out_shape), scratch_shapes=scratch,
        compiler_params=_params(("arbitrary",)),
    )(*args)
    return (res[0].reshape(S, ATT_W), res[1].reshape(S, 2 * KV_W)), list(res[2:])


def _dh(segments, w_in, x, dx2, g, hosted=None, tm=1024, tk=512):
    S, D = x.shape
    ns = len(segments)
    counts = [a.shape[1] // tk for a, _ in segments]
    starts = [sum(counts[:s]) for s in range(ns)]
    nk = sum(counts)
    hn = hosted.n if hosted is not None else 0

    def body(*refs):
        seg_refs = refs[:ns]
        w_ref, x_ref, dx2_ref, g_ref = refs[ns:ns + 4]
        h_ins = refs[ns + 4:ns + 4 + hn]
        gx_ref, gng_ref = refs[ns + 4 + hn:ns + 6 + hn]
        h_outs = refs[ns + 6 + hn:ns + 6 + 2 * hn]
        acc = refs[ns + 6 + 2 * hn]
        h_sems = refs[ns + 7 + 2 * hn:]
        k, i = pl.program_id(0), pl.program_id(1)

        @pl.when((i == 0) & (k == 0))
        def _():
            gng_ref[...] = jnp.zeros_like(gng_ref)
            if hosted is not None:
                hosted.start(h_ins, h_outs, h_sems)

        @pl.when(k == 0)
        def _():
            acc[i] = jnp.zeros(acc.shape[1:], F32)

        for s in range(ns):
            @pl.when((k >= starts[s]) & (k < starts[s] + counts[s]))
            def _(s=s):
                t = seg_refs[s][...]
                if segments[s][1]:
                    t = _perm_rows(t, True)
                acc[i] += jnp.dot(t, w_ref[...], preferred_element_type=F32)

        @pl.when(k == nk - 1)
        def _():
            dh = acc[i]
            xf = x_ref[...]
            r = lax.rsqrt(jnp.mean(xf * xf, axis=-1, keepdims=True) + NORM_EPS)
            nrm = xf * r
            gng_ref[...] += jnp.sum(dh * nrm, axis=0, keepdims=True)
            dn = dh * g_ref[...]
            gx_ref[...] = dx2_ref[...] + r * (dn - nrm * jnp.mean(dn * nrm, axis=-1, keepdims=True))

        if hosted is not None:
            @pl.when((i == S // tm - 1) & (k == nk - 1))
            def _():
                hosted.finish(h_ins, h_outs, h_sems)

    ni = S // tm
    row = pl.BlockSpec((tm, D), lambda k, i: (jnp.where(k == nk - 1, i, 0), 0))
    vec = pl.BlockSpec((1, D), lambda k, i: (0, 0))

    def seg_index(s):
        def index(k, i):
            j = k - starts[s]
            return jnp.where(j < 0, 0, jnp.where(j >= counts[s], ni - 1, i)), jnp.clip(j, 0, counts[s] - 1)
        return index

    in_specs = [pl.BlockSpec((tm, tk), seg_index(s)) for s in range(ns)]
    in_specs += [pl.BlockSpec((tk, D), lambda k, i: (k, 0)), row, row, vec]
    args = [a for a, _ in segments] + [w_in, x, dx2, g]
    out_specs = [row, vec]
    out_shape = [jax.ShapeDtypeStruct((S, D), F32), jax.ShapeDtypeStruct((1, D), F32)]
    scratch = [pltpu.VMEM((ni, tm, D), F32)]
    if hosted is not None:
        in_specs += [ANY_SPEC] * hn
        args += hosted.arrays
        out_specs += [ANY_SPEC] * hn
        out_shape += hosted.out_shapes()
        scratch += hosted.sem_shapes()
    res = pl.pallas_call(
        body, name="dh", grid=(nk, S // tm),
        in_specs=in_specs, out_specs=tuple(out_specs), out_shape=tuple(out_shape), scratch_shapes=scratch,
        compiler_params=_params(("arbitrary", "arbitrary"), BIG_VMEM_LIMIT),
    )(*args)
    return res[0], res[1], list(res[2:])


def _tn_matmul(pairs, layout, name, tm=512):
    arrays = []

    def slot(a):
        for i, b in enumerate(arrays):
            if b is a:
                return i
        arrays.append(a)
        return len(arrays) - 1

    slots = [(slot(u), slot(v)) for u, v in pairs]
    ready = [src for blocks in layout for src, _, _ in blocks if not isinstance(src, int)]
    M, K = pairs[0][1].shape
    n_in, n_ready, n_out = len(arrays), len(ready), len(layout)
    last = M // tm - 1

    def body(*refs):
        in_refs, ready_refs = refs[:n_in], refs[n_in:n_in + n_ready]
        o_refs, accs = refs[n_in + n_ready:n_in + n_ready + n_out], refs[n_in + n_ready + n_out:]

        @pl.when(pl.program_id(0) == 0)
        def _():
            for acc in accs:
                acc[...] = jnp.zeros_like(acc)

        for (iu, iv), acc in zip(slots, accs):
            vt = in_refs[iv][...]
            for c in range(0, acc.shape[0], 512):
                acc[c:c + 512, :] += _tn(in_refs[iu][:, c:c + 512], vt)

        @pl.when(pl.program_id(0) == last)
        def _():
            taken = 0
            for o_ref, blocks in zip(o_refs, layout):
                row = 0
                for src, r0, n in blocks:
                    if isinstance(src, int):
                        o_ref[row:row + n, :] = accs[src][r0:r0 + n, :].astype(o_ref.dtype)
                    else:
                        o_ref[row:row + n, :] = ready_refs[taken][r0:r0 + n, :]
                        taken += 1
                    row += n

    out_rows = [sum(n for _, _, n in blocks) for blocks in layout]
    return pl.pallas_call(
        body, name=name, grid=(M // tm,),
        in_specs=[pl.BlockSpec((tm, a.shape[1]), lambda m: (m, 0)) for a in arrays]
        + [pl.BlockSpec(r.shape, lambda m: (0, 0)) for r in ready],
        out_specs=tuple(pl.BlockSpec((rows, K), lambda m: (0, 0)) for rows in out_rows),
        out_shape=tuple(jax.ShapeDtypeStruct((rows, K), BF16) for rows in out_rows),
        scratch_shapes=[pltpu.VMEM((u.shape[1], K), F32) for u, _ in pairs],
        compiler_params=_params(("arbitrary",)),
    )(*arrays, *ready)


def _adamw(parts, w, m, v, name, tr=None, split=None, by_chip=False):
    R, C = w.shape
    tr = R if tr is None else tr
    parts = [parts] if split is None else list(parts)
    npar = len(parts)

    def total(p_ref):
        if by_chip:
            c = lax.axis_index("c")
            g = p_ref[c].astype(F32)
            for chip in range(1, N_DEV // 2):
                g = g + p_ref[2 * chip + c].astype(F32)
            return g
        g = p_ref[0].astype(F32)
        for dev in range(1, N_DEV):
            g = g + p_ref[dev].astype(F32)
        return g

    def body(*refs):
        w_ref, m_ref, v_ref, g_out, d_out, m_out, v_out = refs[npar:]
        if split is None:
            g = total(refs[0])
        else:
            g = jnp.where(_mesh_pos()[3] < split, total(refs[0]), total(refs[1]))
        mn = ADAM_B1 * m_ref[...] + (1.0 - ADAM_B1) * g
        vn = ADAM_B2 * v_ref[...] + (1.0 - ADAM_B2) * (g * g)
        m_hat = mn / (1.0 - ADAM_B1 ** ADAM_STEP)
        v_hat = vn / (1.0 - ADAM_B2 ** ADAM_STEP)
        g_out[...] = g
        d_out[...] = -ADAM_LR * (m_hat / (jnp.sqrt(v_hat) + ADAM_EPS) + ADAM_WD * w_ref[...])
        m_out[...] = mn
        v_out[...] = vn

    blk = pl.BlockSpec((tr, C), lambda i: (i, 0))
    shp = jax.ShapeDtypeStruct((R, C), F32)
    return pl.pallas_call(
        body, name=name, grid=(R // tr,),
        in_specs=[pl.BlockSpec((N_DEV, tr, C), lambda i: (0, i, 0))] * npar + [blk, blk, blk],
        out_specs=(blk, blk, blk, blk), out_shape=(shp, shp, shp, shp),
        compiler_params=_params(("parallel",)),
    )(*parts, w, m, v)


def _local_step(x, target, norm_g, w_in, conv_w, conv_b, ln_g, ln_b, w_out, gf, exchanges=None, first_weights=None,
                late_weights=None, first_rows=ATT_W + 2 * KV_W + ATT_W // 2):
    ex_out, ex_att, ex_conv = exchanges if exchanges is not None else (None, None, None)
    h_rm, h, *first = _norm_rows(x, norm_g, first_weights[0] if first_weights is not None else None)
    if first_weights is not None:
        w_in = first_weights[1](first)
    conv_cols = w_in.shape[0] - 2 * ATT_W - 2 * KV_W
    q, kv, a_gate, gates, *gathered = _inproj(
        h_rm, h, w_in,
        [(ATT_W, HEAD_DIM ** -0.5, True), (2 * KV_W, 1.0, True), (ATT_W, 1.0, True), (conv_cols, 1.0, False)],
        late_weights[0] if late_weights is not None else None)
    if late_weights is not None:
        conv_w, w_out = late_weights[1](gathered)

    alone = [_attn_fwd(q, kv, dil, "attn_fwd_d%d" % dil) for _, dil in PATTERNS[1:]]
    o, lse, y_att = _attn_fwd(q, kv, PATTERNS[0][1], "attn_fwd_d%d" % PATTERNS[0][1], alone, a_gate)
    conv_out, y_conv = _conv_fwd(gates, conv_w, conv_b, ln_g, ln_b)
    dx2, dxb, loss_cols, g_gf = _outproj_loss(x, y_att, y_conv, w_out, gf, target)

    d_o, d_a_gate, delta, dxb_rm = _dy_att(dxb, w_out, a_gate, o)
    g_w_out, = _tn_matmul([(y_att, dxb_rm), (y_conv, dxb)], [[(0, 0, ATT_W), (1, 0, y_conv.shape[1])]], "gw_out")
    acc, out_parts = None, []
    for idx, (_, dil) in enumerate(reversed(PATTERNS)):
        hosted = ex_out(g_w_out) if (idx == 0 and ex_out is not None) else None
        acc, outs = _attn_bwd(q, kv, d_o, lse, delta, dil, acc, idx == len(PATTERNS) - 1, "attn_bwd_d%d" % dil,
                              hosted)
        out_parts += outs
    dq, dkv = acc
    a_lo = first_rows - (ATT_W + 2 * KV_W)
    assert 0 < a_lo < ATT_W
    g_first, g_a_rest = _tn_matmul(
        [(dq, h_rm), (dkv, h_rm), (d_a_gate, h_rm)],
        [[(0, 0, ATT_W), (1, 0, 2 * KV_W), (2, 0, a_lo)], [(2, a_lo, ATT_W - a_lo)]], "gw_in_att")

    d_c_gate, d_conv, g_ln_g, g_ln_b, g_conv_b = _dy_conv(dxb, w_out, gates, conv_out, ln_g, ln_b)
    dgates, g_conv_w, att_parts = _conv_bwd(d_conv, gates, d_c_gate, conv_w,
                                            ex_att(g_first) if ex_att is not None else None)
    g_rest, = _tn_matmul([(dgates, h)], [[(g_a_rest, 0, ATT_W - a_lo), (0, 0, conv_cols)]], "gw_in_conv")
    grad_x, g_norm_g, conv_parts = _dh(
        [(dq, True), (dkv, True), (d_a_gate, True), (dgates, False)], w_in, x, dx2, norm_g,
        ex_conv(g_rest, g_conv_w) if ex_conv is not None else None)
    small = (g_norm_g, g_conv_b, g_ln_g, g_ln_b, g_gf, loss_cols)
    return grad_x, (g_first, g_rest), g_w_out, g_conv_w, small, (out_parts, att_parts, conv_parts)


def kernel(x, norm_g, w_in, conv_w, conv_b, conv_ln_g, conv_ln_b, w_out, final_norm_g, loss_target, m_norm_g, m_w_in, m_conv_w, m_conv_b, m_conv_ln_g, m_conv_ln_b, m_w_out, m_final_norm_g, v_norm_g, v_w_in, v_conv_w, v_conv_b, v_conv_ln_g, v_conv_ln_b, v_w_out, v_final_norm_g):
    S, D = x.shape[1], x.shape[2]
    win_sh, wout_sh, cw_sh = w_in[0].T, w_out[0], conv_w[0]
    cols_sh, rows_sh, ch_sh = win_sh.shape[0], wout_sh.shape[0], cw_sh.shape[1]

    def first_weights(gathered):
        return gathered[0].reshape(N_DEV * cols_sh, D)

    def late_weights(gathered):
        wout_all, cw_all = gathered
        conv_w_full = cw_all.transpose(1, 0, 2).reshape(CONV_K, N_DEV * ch_sh)
        return jnp.pad(conv_w_full, ((0, CONV_HALO - CONV_K), (0, 0))), wout_all.reshape(N_DEV * rows_sh, D)

    gf = final_norm_g.reshape(1, D)

    first = -(-(ATT_W + 2 * KV_W) // cols_sh)

    def ex_out(g_w_out):
        return _Exchange([g_w_out.reshape(N_DEV, rows_sh, D)], [(0, N_DEV)])

    same_core = (2, 4, 6)

    def ex_att(g_first):
        mine = _chip_sum(g_first.reshape(first, cols_sh, D), 0, "rs_att")
        return _Exchange([mine], [(0, first)], [same_core])

    def ex_conv(g_rest, g_conv_w):
        mine = _chip_sum(g_rest.reshape(N_DEV - first, cols_sh, D), first, "rs_conv")
        return _Exchange(
            [mine, g_conv_w[:CONV_K].reshape(CONV_K, N_DEV, ch_sh).transpose(1, 0, 2)],
            [(first, N_DEV), (0, N_DEV)], [same_core, None])

    grad_x, _, _, _, small, parts = _local_step(
        x[0], loss_target[0], norm_g, None, None, conv_b, conv_ln_g, conv_ln_b, None, gf,
        (ex_out, ex_att, ex_conv), (_Gather([win_sh.astype(BF16)]), first_weights),
        (_Gather([wout_sh.astype(BF16), cw_sh]), late_weights), first * cols_sh)
    (wout_parts,), (win_parts_lo,), (win_parts_hi, cw_parts) = parts

    small_pack = jnp.concatenate(list(small) + [jnp.zeros((2, D), F32)], axis=0)
    small_parts, = _exchange(_Exchange([small_pack], [None]), "gather_small")

    upd_win = _adamw((win_parts_lo, win_parts_hi), win_sh, m_w_in[0].T, v_w_in[0].T, "adamw_w_in",
                     tr=cols_sh // 2, split=first, by_chip=True)
    upd_wout = _adamw(wout_parts, wout_sh, m_w_out[0], v_w_out[0], "adamw_w_out", tr=128)
    upd_cw = _adamw(cw_parts, cw_sh, m_conv_w[0], v_conv_w[0], "adamw_conv_w")
    zeros3 = jnp.zeros((3, D), F32)
    stack = lambda a, b, c, d_, e: jnp.concatenate([a, b, c, d_, e.reshape(1, D), zeros3], axis=0)
    upd_small = _adamw(
        small_parts,
        stack(norm_g, conv_b, conv_ln_g, conv_ln_b, final_norm_g),
        stack(m_norm_g, m_conv_b, m_conv_ln_g, m_conv_ln_b, m_final_norm_g),
        stack(v_norm_g, v_conv_b, v_conv_ln_g, v_conv_ln_b, v_final_norm_g) + jnp.concatenate(
            [jnp.zeros((5, D), F32), jnp.ones((3, D), F32)], axis=0),
        "adamw_small")

    loss = 0.5 / D * jnp.sum(upd_small[0][5])

    def outputs(kind):
        sm = upd_small[kind]
        return [sm[0:1], upd_win[kind].T[None], upd_cw[kind][None], sm[1:2], sm[2:3], sm[3:4],
                upd_wout[kind][None], sm[4]]

    return (loss, grad_x[None], *outputs(0), *outputs(1), *outputs(2), *outputs(3))
```

```python
import jax
import jax.numpy as jnp
from jax import lax
from jax.experimental import pallas as pl
from jax.experimental.pallas import tpu as pltpu

F32 = jnp.float32
BF16 = jnp.bfloat16

HEAD_DIM = 64
N_KV_HEADS = 4
N_Q_HEADS = 16
ATT_W = 1024
KV_W = 256
CONV_K = 31
CONV_HALO = 32
PATTERNS = ((128, 1), (512, 4), (2048, 16))
BLK = 128
LANES = 128
NORM_EPS = 1e-6
LN_EPS = 1e-5
NEG = -1e30
N_DEV = 8
ADAM_LR, ADAM_B1, ADAM_B2, ADAM_EPS, ADAM_WD, ADAM_STEP = 0.001, 0.9, 0.999, 1e-08, 0.01, 10
VMEM_LIMIT = 48 * 1024 * 1024
BIG_VMEM_LIMIT = 58 * 1024 * 1024
SLOPES = tuple(2.0 ** (-8.0 * (h + 1) / N_Q_HEADS) for h in range(N_Q_HEADS))
MESH = pl.DeviceIdType.MESH


def _params(sem, vmem_limit=VMEM_LIMIT):
    return pltpu.CompilerParams(dimension_semantics=sem, vmem_limit_bytes=vmem_limit)


def _sigmoid(v):
    return 1.0 / (1.0 + jnp.exp(-v))


def _silu_and_grad(v):
    s = _sigmoid(v)
    return v * s, s * (1.0 + v * (1.0 - s))


ANY_SPEC = pl.BlockSpec(memory_space=pl.ANY)


def _mesh_pos():
    x, y, c = lax.axis_index("x"), lax.axis_index("y"), lax.axis_index("c")
    return x, y, c, 4 * x + 2 * y + c


def _flipped(k, x, y, c):
    px = 1 - x if k & 4 else x
    py = 1 - y if k & 2 else y
    pc = 1 - c if k & 1 else c
    return (px, py, pc), 4 * px + 2 * py + pc


class _Exchange:
    def __init__(self, arrays, dests, flips=None):
        self.arrays, self.dests, self.n = list(arrays), list(dests), len(arrays)
        self.flips = [tuple(range(1, N_DEV)) if f is None else tuple(f)
                      for f in (flips if flips is not None else [None] * self.n)]

    def out_shapes(self):
        return [jax.ShapeDtypeStruct((N_DEV,) + a.shape[-2:], a.dtype) for a in self.arrays]

    def sem_shapes(self):
        return [pltpu.SemaphoreType.DMA((self.n, N_DEV - 1)), pltpu.SemaphoreType.DMA((self.n, N_DEV - 1)),
                pltpu.SemaphoreType.DMA((self.n,))]

    def _when(self, a, dev, fn):
        if self.dests[a] is None:
            fn()
        else:
            lo, hi = self.dests[a]
            pl.when((dev >= lo) & (dev < hi))(fn)

    def _mine(self, ins, a, dev):
        return ins[a] if self.dests[a] is None else ins[a].at[dev - self.dests[a][0]]

    def _copy(self, ins, outs, sems, a, k, src_dev, slot, target):
        return pltpu.make_async_remote_copy(
            src_ref=self._mine(ins, a, src_dev), dst_ref=outs[a].at[slot],
            send_sem=sems[0].at[a, k - 1], recv_sem=sems[1].at[a, k - 1],
            device_id=target, device_id_type=MESH)

    def start(self, ins, outs, sems):
        x, y, c, me = _mesh_pos()
        for a in range(self.n):
            self._when(a, me, lambda a=a: pltpu.make_async_copy(
                self._mine(ins, a, me), outs[a].at[me], sems[2].at[a]).start())
            for k in self.flips[a]:
                target, peer = _flipped(k, x, y, c)
                self._when(a, peer, lambda a=a, k=k, target=target, peer=peer: self._copy(
                    ins, outs, sems, a, k, peer, me, target).start())

    def finish(self, ins, outs, sems):
        x, y, c, me = _mesh_pos()
        lo0 = [0 if d is None else d[0] for d in self.dests]
        for a in range(self.n):
            for k in self.flips[a]:
                target, peer = _flipped(k, x, y, c)
                self._when(a, me, lambda a=a, k=k, peer=peer: self._copy(
                    ins, outs, sems, a, k, lo0[a], peer, (x, y, c)).wait_recv())
            for k in self.flips[a]:
                target, peer = _flipped(k, x, y, c)
                self._when(a, peer, lambda a=a, k=k, target=target, peer=peer: self._copy(
                    ins, outs, sems, a, k, peer, me, target).wait_send())
            self._when(a, me, lambda a=a: pltpu.make_async_copy(
                self._mine(ins, a, me), outs[a].at[me], sems[2].at[a]).wait())


def _exchange(ex, name):
    na = ex.n

    def body(*refs):
        ins, outs, sems = refs[:na], refs[na:2 * na], refs[2 * na:]
        ex.start(ins, outs, sems)
        ex.finish(ins, outs, sems)

    return pl.pallas_call(
        body, name=name, out_shape=tuple(ex.out_shapes()),
        in_specs=[ANY_SPEC] * na, out_specs=tuple([ANY_SPEC] * na), scratch_shapes=ex.sem_shapes(),
    )(*ex.arrays)


def _chip_sum(pieces, lo, name):
    n, R, C = pieces.shape
    rows = 64
    assert R % rows == 0

    def body(p_ref, o_ref, mine_buf, other_buf, sum_buf, send_sems, recv_sems, local_sems, out_sems):
        x, y, c, me = _mesh_pos()

        def remote(i):
            return pltpu.make_async_remote_copy(
                src_ref=p_ref.at[i], dst_ref=other_buf.at[i], send_sem=send_sems.at[i], recv_sem=recv_sems.at[i],
                device_id=(x, y, 1 - c), device_id_type=MESH)

        def local(i):
            return pltpu.make_async_copy(p_ref.at[i], mine_buf.at[i], local_sems.at[i])

        def out(i):
            return pltpu.make_async_copy(sum_buf.at[i], o_ref.at[i], out_sems.at[i])

        summed_by = [(lo + i) % 2 for i in range(n)]
        for i in range(n):
            pl.when(c != summed_by[i])(remote(i).start)
            pl.when(c == summed_by[i])(local(i).start)
        for i in range(n):
            @pl.when(c == summed_by[i])
            def _(i=i):
                local(i).wait()
                remote(i).wait_recv()

                def chunk(j, carry):
                    r = pl.ds(pl.multiple_of(j * rows, rows), rows)
                    sum_buf[i, r, :] = (mine_buf[i, r, :].astype(F32) + other_buf[i, r, :].astype(F32)
                                        ).astype(sum_buf.dtype)
                    return carry

                lax.fori_loop(0, R // rows, chunk, 0)
                out(i).start()
        for i in range(n):
            pl.when(c == summed_by[i])(out(i).wait)
            pl.when(c != summed_by[i])(remote(i).wait_send)

    buf = pltpu.VMEM(pieces.shape, pieces.dtype)
    return pl.pallas_call(
        body, name=name, out_shape=jax.ShapeDtypeStruct(pieces.shape, pieces.dtype),
        in_specs=[ANY_SPEC], out_specs=ANY_SPEC,
        scratch_shapes=[buf, buf, buf] + [pltpu.SemaphoreType.DMA((n,))] * 4,
        compiler_params=pltpu.CompilerParams(vmem_limit_bytes=VMEM_LIMIT),
    )(pieces)


class _Gather:
    def __init__(self, arrays):
        self.arrays, self.n = list(arrays), len(arrays)

    def out_shapes(self):
        return [jax.ShapeDtypeStruct((N_DEV,) + a.shape, a.dtype) for a in self.arrays]

    def sem_shapes(self):
        return [pltpu.SemaphoreType.DMA((self.n, N_DEV - 1)), pltpu.SemaphoreType.DMA((self.n, N_DEV - 1)),
                pltpu.SemaphoreType.DMA((self.n,))]

    def _plan(self, ins, outs, sems):
        x, y, c, me = _mesh_pos()
        chips = [(1 - x, y), (x, 1 - y), (1 - x, 1 - y)]

        def copy(a, k, src, block, to):
            px, py, pc = block
            return pltpu.make_async_remote_copy(
                src_ref=src, dst_ref=outs[a].at[4 * px + 2 * py + pc], send_sem=sems[0].at[a, k],
                recv_sem=sems[1].at[a, k], device_id=to, device_id_type=MESH)

        def landed(a, block):
            px, py, pc = block
            return outs[a].at[4 * px + 2 * py + pc]

        local = [pltpu.make_async_copy(ins[a], outs[a].at[me], sems[2].at[a]) for a in range(self.n)]
        first = []
        for a in range(self.n):
            first.append(copy(a, 0, ins[a], (x, y, c), (x, y, 1 - c)))
            first += [copy(a, 1 + j, ins[a], (x, y, c), (*chip, c)) for j, chip in enumerate(chips[:2])]
        return (x, y, c), chips, copy, landed, local, first

    def start(self, ins, outs, sems):
        *_, local, first = self._plan(ins, outs, sems)
        for cp in local + first:
            cp.start()

    def finish(self, ins, outs, sems):
        (x, y, c), chips, copy, landed, local, first = self._plan(ins, outs, sems)
        south = c == 0
        came = (jnp.where(south, 1 - x, x), jnp.where(south, y, 1 - y), c)
        goes = (jnp.where(south, x, 1 - x), jnp.where(south, 1 - y, y), c)
        passed = []
        for a in range(self.n):
            for j, chip in enumerate(chips[:2]):
                copy(a, 1 + j, ins[a], (*chip, c), (x, y, c)).wait_recv()
            passed.append(copy(a, 3, landed(a, came), came, goes))
            passed += [copy(a, 4 + j, landed(a, (*chip, c)), (*chip, c), (x, y, 1 - c))
                       for j, chip in enumerate(chips[:2])]
        for cp in passed:
            cp.start()
        for a in range(self.n):
            diagonal = (*chips[2], c)
            copy(a, 3, ins[a], diagonal, (x, y, c)).wait_recv()
            cp = copy(a, 6, landed(a, diagonal), diagonal, (x, y, 1 - c))
            cp.start()
            passed.append(cp)
        for a in range(self.n):
            copy(a, 0, ins[a], (x, y, 1 - c), (x, y, c)).wait_recv()
            for j, chip in enumerate(chips):
                copy(a, 4 + j, ins[a], (*chip, 1 - c), (x, y, c)).wait_recv()
        for cp in first + passed:
            cp.wait_send()
        for cp in local:
            cp.wait()


CHUNK = 128
RESIDUES = 16
PER_RES = CHUNK // RESIDUES


def _perm_rows(tile, inverse):
    a = lax.broadcasted_iota(jnp.int32, (CHUNK, CHUNK), 0)
    b = lax.broadcasted_iota(jnp.int32, (CHUNK, CHUNK), 1)
    if inverse:
        a, b = b, a
    p = jnp.where(a == PER_RES * (b % RESIDUES) + b // RESIDUES, 1.0, 0.0).astype(BF16)
    parts = [jnp.dot(p, tile[c * CHUNK:(c + 1) * CHUNK], preferred_element_type=F32)
             for c in range(tile.shape[0] // CHUNK)]
    return jnp.concatenate(parts, axis=0).astype(BF16)


class _Rows:
    def __init__(self, dil, S):
        nc = S // CHUNK
        self.dil = dil
        if dil == 1:
            self.view, self.block, self.nb = (nc, CHUNK), (None, CHUNK), nc
            self.index = lambda r, b: (b, 0, 0)
        elif dil == 4:
            self.view, self.block, self.nb = (nc, 4, 4, PER_RES), (4, 4, None, PER_RES), nc // 4
            self.index = lambda r, b: (b, 0, r, 0, 0)
        elif dil == RESIDUES:
            self.view, self.block, self.nb = (nc, RESIDUES, PER_RES), (RESIDUES, None, PER_RES), nc // RESIDUES
            self.index = lambda r, b: (b, r, 0, 0)
        else:
            raise NotImplementedError(dil)

    def of(self, a):
        return a.reshape(self.view + (a.shape[-1],))

    def spec(self, width, which_block):
        return pl.BlockSpec(self.block + (width,), lambda r, n: self.index(r, which_block(n)))

    def pos(self, row):
        if self.dil == 1:
            return (row % PER_RES) * RESIDUES + row // PER_RES
        if self.dil == 4:
            return (row // 32) * 32 + (row % PER_RES) * 4 + (row % 32) // PER_RES
        return row


def _ld(ref, cols=slice(None)):
    v = ref[(slice(None),) * (len(ref.shape) - 1) + (cols,)]
    return v.reshape(BLK, v.shape[-1])


def _st(ref, val, cols=slice(None)):
    ref[(slice(None),) * (len(ref.shape) - 1) + (cols,)] = val.reshape(ref.shape[:-1] + (val.shape[-1],))


def _norm_rows(x, g, hosted=None, tm=512):
    S, D = x.shape
    hn = hosted.n if hosted is not None else 0

    def body(x_ref, g_ref, *rest):
        h_ins = rest[:hn]
        hrm_out, h_out = rest[hn:hn + 2]
        h_outs = rest[hn + 2:2 * hn + 2]
        h_sems = rest[2 * hn + 2:]
        i = pl.program_id(0)
        if hosted is not None:
            pl.when(i == 0)(lambda: hosted.start(h_ins, h_outs, h_sems))
        xf = x_ref[...]
        r = lax.rsqrt(jnp.mean(xf * xf, axis=-1, keepdims=True) + NORM_EPS)
        h = (xf * r * g_ref[...]).astype(BF16)
        h_out[...] = h
        hrm_out[...] = _perm_rows(h, False)
        if hosted is not None:
            pl.when(i == S // tm - 1)(lambda: hosted.finish(h_ins, h_outs, h_sems))

    row = pl.BlockSpec((tm, D), lambda i: (i, 0))
    in_specs, args = [row, pl.BlockSpec((1, D), lambda i: (0, 0))], [x, g]
    out_specs, out_shape, scratch = [row, row], [jax.ShapeDtypeStruct((S, D), BF16)] * 2, []
    if hosted is not None:
        in_specs += [ANY_SPEC] * hn
        args += hosted.arrays
        out_specs += [ANY_SPEC] * hn
        out_shape += hosted.out_shapes()
        scratch += hosted.sem_shapes()
    return pl.pallas_call(
        body, name="norm_rows", grid=(S // tm,),
        in_specs=in_specs, out_specs=tuple(out_specs), out_shape=tuple(out_shape), scratch_shapes=scratch,
        compiler_params=_params(("arbitrary",)),
    )(*args)


def _inproj(h_rm, h, w_t, segments, hosted=None, tm=1024, tn=512):
    S, D = h.shape
    ns = len(segments)
    ni = S // tm
    counts = [seg[0] // tn for seg in segments]
    starts = [sum(counts[:s]) for s in range(ns)]

    hn = hosted.n if hosted is not None else 0
    last_p = sum(counts)

    def body(hrm_ref, h_ref, w_ref, *rest):
        h_ins, rest = rest[:hn], rest[hn:]
        outs = rest[:ns]
        h_outs = rest[ns:ns + hn]
        hrm_scr, h_scr = rest[ns + hn:ns + 2 + hn]
        h_sems = rest[ns + 2 + hn:]
        p, i = pl.program_id(0), pl.program_id(1)

        if hosted is not None:
            @pl.when((p == 0) & (i == 0))
            def _():
                hosted.start(h_ins, h_outs, h_sems)

            @pl.when((p == last_p) & (i == ni - 1))
            def _():
                hosted.finish(h_ins, h_outs, h_sems)

        @pl.when(p == 0)
        def _():
            h_scr[i] = h_ref[...]
            hrm_scr[i] = hrm_ref[...]

        for s, (_, scale, rm) in enumerate(segments):
            @pl.when((p > starts[s]) & (p <= starts[s] + counts[s]))
            def _(s=s, scale=scale, rm=rm):
                acc = _nt((hrm_scr if rm else h_scr)[i], w_ref[...])
                outs[s][...] = acc * scale if scale != 1.0 else acc

    def out_index(s):
        def index(p, i):
            j = p - 1 - starts[s]
            row = jnp.where(j < 0, 0, jnp.where(j >= counts[s], ni - 1, i))
            return row, jnp.clip(j, 0, counts[s] - 1)
        return index

    first_pass = pl.BlockSpec((tm, D), lambda p, i: (jnp.where(p == 0, i, ni - 1), 0))
    out_specs = [pl.BlockSpec((tm, tn), out_index(s)) for s in range(ns)]
    out_shape = [jax.ShapeDtypeStruct((S, seg[0]), F32) for seg in segments]
    in_specs = [first_pass, first_pass, pl.BlockSpec((tn, D), lambda p, i: (jnp.maximum(p - 1, 0), 0))]
    args = [h_rm, h, w_t]
    scratch = [pltpu.VMEM((ni, tm, D), BF16), pltpu.VMEM((ni, tm, D), BF16)]
    if hosted is not None:
        in_specs += [ANY_SPEC] * hn
        args += hosted.arrays
        out_specs += [ANY_SPEC] * hn
        out_shape += hosted.out_shapes()
        scratch += hosted.sem_shapes()
    return pl.pallas_call(
        body, name="inproj", grid=(1 + last_p, ni),
        in_specs=in_specs, out_specs=tuple(out_specs), out_shape=tuple(out_shape), scratch_shapes=scratch,
        compiler_params=_params(("arbitrary", "arbitrary"), BIG_VMEM_LIMIT),
    )(*args)


def _fill_bias_table(tbl, rows, keys_first=False):
    shape = (2 * BLK, BLK) if keys_first else (BLK, 2 * BLK)
    qi = lax.broadcasted_iota(jnp.int32, shape, 1 if keys_first else 0)
    kj = lax.broadcasted_iota(jnp.int32, shape, 0 if keys_first else 1)
    dist = rows.pos(qi) - rows.pos(kj % BLK) + jnp.where(kj < BLK, BLK, 0)
    inside = (dist >= 0) & (dist <= BLK)
    negd = (dist * (-rows.dil)).astype(F32)
    for f, valid in enumerate((inside & (kj >= BLK), inside)):
        for h in range(N_Q_HEADS):
            tbl[f * N_Q_HEADS + h] = jnp.where(valid, SLOPES[h] * negd, NEG)


def _bias2(tbl, n, h0, h1, axis=0):
    base = jnp.where(n == 0, 0, N_Q_HEADS)
    return jnp.concatenate([tbl[base + h0], tbl[base + h1]], axis=axis)


def _head_operands(kv2, hk, lo_mask):
    half, pos = hk // 2, hk % 2
    out = []
    for base in (0, KV_W):
        t = kv2[:, base + half * LANES: base + (half + 1) * LANES]
        sw = pltpu.roll(t, HEAD_DIM, axis=1)
        at_lo, at_hi = (t, sw) if pos == 0 else (sw, t)
        out.append(jnp.where(lo_mask, at_lo, 0.0).astype(BF16))
        out.append(jnp.where(lo_mask, 0.0, at_hi).astype(BF16))
    return out


def _nt(a, b):
    return lax.dot_general(a, b, (((1,), (1,)), ((), ())), preferred_element_type=F32)


def _tn(a, b):
    return lax.dot_general(a, b, (((0,), (0,)), ((), ())), preferred_element_type=F32)


def _attn_fwd(q, kv, dil, name, prev=(), gate=None):
    S = q.shape[0]
    rows = _Rows(dil, S)
    nb = rows.nb
    have_prev, last = len(prev) > 0, gate is not None

    def body(*refs):
        refs = list(refs)
        q_ref, kvc_ref, kvp_ref = refs[:3]
        del refs[:3]
        po_refs, pl_refs = refs[0:2 * len(prev):2], refs[1:2 * len(prev):2]
        del refs[:2 * len(prev)]
        if last:
            gate_ref = refs.pop(0)
        o_ref, lse_ref = refs[:2]
        y_ref = refs[2] if last else None
        tbl = refs[-1]
        n = pl.program_id(1)

        @pl.when((pl.program_id(0) == 0) & (n == 0))
        def _():
            _fill_bias_table(tbl, rows)

        kv2 = jnp.concatenate([_ld(kvp_ref), _ld(kvc_ref)], axis=0)
        lo_mask = lax.broadcasted_iota(jnp.int32, (2 * BLK, LANES), 1) < HEAD_DIM
        lane = lax.broadcasted_iota(jnp.int32, (BLK, LANES), 1)
        stats = jnp.zeros((BLK, LANES), F32)
        for hk in range(N_KV_HEADS):
            k_lo, k_hi, v_lo, v_hi = _head_operands(kv2, hk, lo_mask)
            cols = [slice(b * LANES, (b + 1) * LANES) for b in (2 * hk, 2 * hk + 1)]
            q2 = jnp.concatenate([_ld(q_ref, cols[0]), _ld(q_ref, cols[1])], axis=0).astype(BF16)
            o2 = jnp.zeros((2 * BLK, LANES), F32)
            for which, (kk, vv) in enumerate(((k_lo, v_lo), (k_hi, v_hi))):
                h0, h1 = 4 * hk + which, 4 * hk + 2 + which
                s = _nt(q2, kk) + _bias2(tbl, n, h0, h1)
                m = jnp.max(s, axis=1, keepdims=True)
                p = jnp.exp(s - m)
                l = jnp.sum(p, axis=1, keepdims=True)
                o2 = o2 + jnp.dot(p.astype(BF16), vv, preferred_element_type=F32) * (1.0 / l)
                lse = m + jnp.log(l)
                stats = jnp.where(lane == h0, lse[0:BLK], stats)
                stats = jnp.where(lane == h1, lse[BLK:], stats)
            _st(o_ref, o2[0:BLK], cols[0])
            _st(o_ref, o2[BLK:], cols[1])
        if have_prev:
            others = [_ld(r) for r in pl_refs]
            top = stats
            for b in others:
                top = jnp.maximum(top, b)
            e_new = jnp.exp(stats - top)
            e_old = [jnp.exp(b - top) for b in others]
            total = e_new
            for e in e_old:
                total = total + e
            stats = top + jnp.log(total)
            inv = 1.0 / total
            w_new, w_old = e_new * inv, [e * inv for e in e_old]
        if have_prev or last:
            lo = lane < HEAD_DIM
            for blk in range(ATT_W // LANES):
                cols = slice(blk * LANES, (blk + 1) * LANES)
                o_blk = _ld(o_ref, cols)
                if have_prev:
                    pick = lambda w: jnp.where(lo, w[:, 2 * blk:2 * blk + 1], w[:, 2 * blk + 1:2 * blk + 2])
                    o_blk = o_blk * pick(w_new)
                    for po_ref, w in zip(po_refs, w_old):
                        o_blk = o_blk + _ld(po_ref, cols) * pick(w)
                    _st(o_ref, o_blk, cols)
                if last:
                    a = _ld(gate_ref, cols)
                    _st(y_ref, (o_blk * (a * _sigmoid(a))).astype(BF16), cols)
        _st(lse_ref, stats)

    here = lambda n: n
    before_n = lambda n: jnp.maximum(n - 1, 0)
    in_specs = [rows.spec(ATT_W, here), rows.spec(2 * KV_W, here), rows.spec(2 * KV_W, before_n)]
    args = [rows.of(q), rows.of(kv), rows.of(kv)]
    for o_other, lse_other in prev:
        in_specs += [rows.spec(ATT_W, here), rows.spec(LANES, here)]
        args += [rows.of(o_other), rows.of(lse_other)]
    out_specs = [rows.spec(ATT_W, here), rows.spec(LANES, here)]
    out_shape = [jax.ShapeDtypeStruct(rows.view + (ATT_W,), F32), jax.ShapeDtypeStruct(rows.view + (LANES,), F32)]
    if last:
        in_specs.append(rows.spec(ATT_W, here))
        args.append(rows.of(gate))
        out_specs.append(rows.spec(ATT_W, here))
        out_shape.append(jax.ShapeDtypeStruct(rows.view + (ATT_W,), BF16))
    res = pl.pallas_call(
        body, name=name, grid=(dil, nb),
        in_specs=in_specs, out_specs=tuple(out_specs), out_shape=tuple(out_shape),
        scratch_shapes=[pltpu.VMEM((2 * N_Q_HEADS, BLK, 2 * BLK), F32)],
        compiler_params=_params(("arbitrary", "arbitrary")),
    )(*args)
    return tuple(r.reshape(S, r.shape[-1]) for r in res)


def _shifted_copies(buf, phases):
    n = phases.shape[1]
    for b in range(1, 8):
        phases[b - 1] = buf[b:b + n, :]


def _window(buf, phases, start, cols):
    b = start % 8
    if b == 0:
        return buf[start:start + 8, cols]
    return phases[b - 1, start - b:start - b + 8, cols]


def _broadcast_taps(w_ref, wb):
    for j in range(CONV_K):
        wb[j] = jnp.broadcast_to(w_ref[j:j + 1, :], wb.shape[1:])


def _conv_fwd(gates, conv_w, conv_b, ln_g, ln_b, tt=256):
    S = gates.shape[0]
    C = conv_w.shape[1]
    hb = tt // CONV_HALO

    def body(val_ref, glu_ref, hval_ref, hglu_ref, gate_ref, w_ref, b_ref, g_ref, beta_ref,
             conv_ref, y_ref, hbuf, hph):
        i = pl.program_id(0)
        halo = hval_ref[...] * _sigmoid(hglu_ref[...])
        hbuf[0:CONV_HALO, :] = jnp.where(i > 0, halo, 0.0)
        hbuf[CONV_HALO:, :] = val_ref[...] * _sigmoid(glu_ref[...])
        _shifted_copies(hbuf, hph)
        for cb in range(C // LANES):
            cols = slice(cb * LANES, (cb + 1) * LANES)
            wj = [jnp.broadcast_to(w_ref[j:j + 1, cols], (8, LANES)) for j in range(CONV_K)]
            for rc in range(tt // 8):
                acc = jnp.zeros((8, LANES), F32)
                for j in range(CONV_K):
                    start = rc * 8 + CONV_HALO - (CONV_K - 1) + j
                    acc = acc + _window(hbuf, hph, start, cols) * wj[j]
                conv_ref[rc * 8:(rc + 1) * 8, cols] = acc
        cv = conv_ref[...] + b_ref[...]
        conv_ref[...] = cv
        mu = jnp.mean(cv, axis=-1, keepdims=True)
        xc = cv - mu
        var = jnp.mean(xc * xc, axis=-1, keepdims=True)
        ln = xc * lax.rsqrt(var + LN_EPS) * g_ref[...] + beta_ref[...]
        gt = gate_ref[...]
        y_ref[...] = (ln * _sigmoid(ln) * (gt * _sigmoid(gt))).astype(BF16)

    vec = pl.BlockSpec((1, C), lambda i: (0, 0))
    return pl.pallas_call(
        body, name="conv_fwd", grid=(S // tt,),
        in_specs=[pl.BlockSpec((tt, C), lambda i: (i, 0)),
                  pl.BlockSpec((tt, C), lambda i: (i, 1)),
                  pl.BlockSpec((CONV_HALO, C), lambda i: (jnp.maximum(i * hb - 1, 0), 0)),
                  pl.BlockSpec((CONV_HALO, C), lambda i: (jnp.maximum(i * hb - 1, 0), 1)),
                  pl.BlockSpec((tt, C), lambda i: (i, 2)),
                  pl.BlockSpec((CONV_HALO, C), lambda i: (0, 0)), vec, vec, vec],
        out_specs=(pl.BlockSpec((tt, C), lambda i: (i, 0)), pl.BlockSpec((tt, C), lambda i: (i, 0))),
        out_shape=(jax.ShapeDtypeStruct((S, C), F32), jax.ShapeDtypeStruct((S, C), BF16)),
        scratch_shapes=[pltpu.VMEM((tt + CONV_HALO, C), F32), pltpu.VMEM((7, tt + CONV_HALO - 8, C), F32)],
        compiler_params=_params(("parallel",)),
    )(gates, gates, gates, gates, gates, conv_w, conv_b, ln_g, ln_b)


def _outproj_loss(x, y_att, y_conv, w_out, gf, target, tm=512):
    S, D = x.shape
    E = y_att.shape[1]

    def body(x_ref, ya_ref, yc_ref, w_ref, gf_ref, t_ref, dx_ref, dxb_ref, loss_ref, ggf_ref):
        @pl.when(pl.program_id(0) == 0)
        def _():
            loss_ref[...] = jnp.zeros_like(loss_ref)
            ggf_ref[...] = jnp.zeros_like(ggf_ref)

        x2 = (x_ref[...] + jnp.dot(_perm_rows(ya_ref[...], True), w_ref[0:E, :], preferred_element_type=F32)
              + jnp.dot(yc_ref[...], w_ref[E:, :], preferred_element_type=F32))
        r = lax.rsqrt(jnp.mean(x2 * x2, axis=-1, keepdims=True) + NORM_EPS)
        nrm = x2 * r
        gfv = gf_ref[...]
        err = nrm * gfv - t_ref[...]
        loss_ref[...] += jnp.sum(err * err, axis=0, keepdims=True)
        dout = err * (1.0 / D)
        ggf_ref[...] += jnp.sum(dout * nrm, axis=0, keepdims=True)
        dn = dout * gfv
        dx2 = r * (dn - nrm * jnp.mean(dn * nrm, axis=-1, keepdims=True))
        dx_ref[...] = dx2
        dxb_ref[...] = dx2.astype(BF16)

    row = lambda w: pl.BlockSpec((tm, w), lambda i: (i, 0))
    vec = pl.BlockSpec((1, D), lambda i: (0, 0))
    return pl.pallas_call(
        body, name="outproj_loss", grid=(S // tm,),
        in_specs=[row(D), row(E), row(E), pl.BlockSpec((2 * E, D), lambda i: (0, 0)), vec, row(D)],
        out_specs=(row(D), row(D), vec, vec),
        out_shape=(jax.ShapeDtypeStruct((S, D), F32), jax.ShapeDtypeStruct((S, D), BF16),
                   jax.ShapeDtypeStruct((1, D), F32), jax.ShapeDtypeStruct((1, D), F32)),
        compiler_params=_params(("arbitrary",)),
    )(x, y_att, y_conv, w_out, gf, target)


def _split3(v):
    hi = v.astype(BF16)
    r1 = v - hi.astype(F32)
    mid = r1.astype(BF16)
    lo = (r1 - mid.astype(F32)).astype(BF16)
    return hi, mid, lo


def _dy_att(dxb, w_out, gates, o, tm=512):
    S, D = dxb.shape
    E = ATT_W

    def body(dx_ref, w_ref, a_ref, o_ref, do_ref, da_ref, dl_ref, dxr_ref):
        dxr = _perm_rows(dx_ref[...], False)
        dxr_ref[...] = dxr
        dya = _nt(dxr, w_ref[...])
        a = a_ref[...]
        ov = o_ref[...]
        sl, dsl = _silu_and_grad(a)
        d_o = dya * sl
        do_ref[...] = d_o
        da_ref[...] = (dya * ov * dsl).astype(BF16)
        ci = lax.broadcasted_iota(jnp.int32, (E, LANES), 0) // HEAD_DIM
        hi = lax.broadcasted_iota(jnp.int32, (E, LANES), 1)
        sel = jnp.where(ci == hi, 1.0, 0.0).astype(BF16)
        acc = jnp.zeros((tm, LANES), F32)
        for part in _split3(d_o * ov):
            acc = acc + jnp.dot(part, sel, preferred_element_type=F32)
        dl_ref[...] = acc

    row = lambda w: pl.BlockSpec((tm, w), lambda i: (i, 0))
    return pl.pallas_call(
        body, name="dy_att", grid=(S // tm,),
        in_specs=[row(D), pl.BlockSpec((E, D), lambda i: (0, 0)), row(E), row(E)],
        out_specs=(row(E), row(E), row(LANES), row(D)),
        out_shape=(jax.ShapeDtypeStruct((S, E), F32), jax.ShapeDtypeStruct((S, E), BF16),
                   jax.ShapeDtypeStruct((S, LANES), F32), jax.ShapeDtypeStruct((S, D), BF16)),
        compiler_params=_params(("parallel",)),
    )(dxb, w_out, gates, o)


def _dy_conv(dxb, w_out, gates, conv_out, ln_g, ln_b, tm=512):
    S, D = dxb.shape
    C = conv_out.shape[1]

    def body(dx_ref, w_ref, gate_ref, cv_ref, g_ref, beta_ref, dgate_ref, dconv_ref, gg_ref, gb_ref, gcb_ref):
        @pl.when(pl.program_id(0) == 0)
        def _():
            gg_ref[...] = jnp.zeros_like(gg_ref)
            gb_ref[...] = jnp.zeros_like(gb_ref)
            gcb_ref[...] = jnp.zeros_like(gcb_ref)

        dyc = _nt(dx_ref[...], w_ref[...])
        cv = cv_ref[...]
        mu = jnp.mean(cv, axis=-1, keepdims=True)
        xc = cv - mu
        rstd = lax.rsqrt(jnp.mean(xc * xc, axis=-1, keepdims=True) + LN_EPS)
        nrm = xc * rstd
        gv = g_ref[...]
        ln = nrm * gv + beta_ref[...]
        u, du = _silu_and_grad(ln)
        gt = gate_ref[...]
        g2, dg2 = _silu_and_grad(gt)
        dgate_ref[...] = (dyc * u * dg2).astype(BF16)
        d_ln = dyc * g2 * du
        gb_ref[...] += jnp.sum(d_ln, axis=0, keepdims=True)
        gg_ref[...] += jnp.sum(d_ln * nrm, axis=0, keepdims=True)
        dn = d_ln * gv
        d_conv = rstd * (dn - jnp.mean(dn, axis=-1, keepdims=True)
                         - nrm * jnp.mean(dn * nrm, axis=-1, keepdims=True))
        dconv_ref[...] = d_conv
        gcb_ref[...] += jnp.sum(d_conv, axis=0, keepdims=True)

    row = lambda w: pl.BlockSpec((tm, w), lambda i: (i, 0))
    vec = pl.BlockSpec((1, C), lambda i: (0, 0))
    return pl.pallas_call(
        body, name="dy_conv", grid=(S // tm,),
        in_specs=[row(D), pl.BlockSpec((C, D), lambda i: (1, 0)),
                  pl.BlockSpec((tm, C), lambda i: (i, 2)), row(C), vec, vec],
        out_specs=(row(C), row(C), vec, vec, vec),
        out_shape=(jax.ShapeDtypeStruct((S, C), BF16), jax.ShapeDtypeStruct((S, C), F32),
                   jax.ShapeDtypeStruct((1, C), F32), jax.ShapeDtypeStruct((1, C), F32),
                   jax.ShapeDtypeStruct((1, C), F32)),
        compiler_params=_params(("arbitrary",)),
    )(dxb, w_out, gates, conv_out, ln_g, ln_b)


def _conv_bwd(d_conv, gates, d_c_gate, conv_w, hosted=None, tt=256):
    S, C = d_conv.shape
    hb = tt // CONV_HALO
    nt = S // tt
    hn = hosted.n if hosted is not None else 0

    def body(*refs):
        dc_ref, dnext_ref, val_ref, glu_ref, dg_ref, w_ref = refs[:6]
        h_ins = refs[6:6 + hn]
        out_ref, gw_ref = refs[6 + hn:8 + hn]
        h_outs = refs[8 + hn:8 + 2 * hn]
        hbuf, dbuf, dhbuf, dph, wb = refs[8 + 2 * hn:13 + 2 * hn]
        h_sems = refs[13 + 2 * hn:]
        i = pl.program_id(0)

        @pl.when(i == 0)
        def _():
            gw_ref[...] = jnp.zeros_like(gw_ref)
            _broadcast_taps(w_ref, wb)
            if hosted is not None:
                hosted.start(h_ins, h_outs, h_sems)

        val = val_ref[...]
        sg = _sigmoid(glu_ref[...])
        hbuf[...] = val * sg
        dbuf[0:tt, :] = dc_ref[...]
        dbuf[tt:, :] = jnp.where(i < nt - 1, dnext_ref[...], 0.0)
        _shifted_copies(dbuf, dph)
        for cb in range(C // LANES):
            cols = slice(cb * LANES, (cb + 1) * LANES)
            gacc = [jnp.zeros((8, LANES), F32) for _ in range(CONV_K)]
            group = 2
            for rc0 in range(0, tt // 8, group):
                hcur = [hbuf[(rc0 + r) * 8:(rc0 + r + 1) * 8, cols] for r in range(group)]
                accs = [jnp.zeros((8, LANES), F32) for _ in range(group)]
                for j in range(CONV_K):
                    wj = wb[j, :, cols]
                    for r in range(group):
                        dwin = _window(dbuf, dph, (rc0 + r) * 8 + (CONV_K - 1) - j, cols)
                        accs[r] = accs[r] + dwin * wj
                        gacc[j] = gacc[j] + dwin * hcur[r]
                for r in range(group):
                    dhbuf[(rc0 + r) * 8:(rc0 + r + 1) * 8, cols] = accs[r]
            for j in range(CONV_K):
                gw_ref[j:j + 1, cols] += jnp.sum(gacc[j], axis=0, keepdims=True)
        d_h = dhbuf[...]
        out_ref[:, 0:C] = (d_h * sg).astype(BF16)
        out_ref[:, C:2 * C] = (d_h * val * sg * (1.0 - sg)).astype(BF16)
        out_ref[:, 2 * C:3 * C] = dg_ref[...]

        if hosted is not None:
            @pl.when(i == nt - 1)
            def _():
                hosted.finish(h_ins, h_outs, h_sems)

    tile = lambda col: pl.BlockSpec((tt, C), lambda i: (i, col))
    in_specs = [tile(0),
                pl.BlockSpec((CONV_HALO, C), lambda i: (jnp.minimum((i + 1) * hb, S // CONV_HALO - 1), 0)),
                tile(0), tile(1), tile(0),
                pl.BlockSpec((CONV_HALO, C), lambda i: (0, 0))]
    args = [d_conv, d_conv, gates, gates, d_c_gate, conv_w]
    out_specs = [pl.BlockSpec((tt, 3 * C), lambda i: (i, 0)), pl.BlockSpec((CONV_HALO, C), lambda i: (0, 0))]
    out_shape = [jax.ShapeDtypeStruct((S, 3 * C), BF16), jax.ShapeDtypeStruct((CONV_HALO, C), F32)]
    scratch = [pltpu.VMEM((tt, C), F32), pltpu.VMEM((tt + CONV_HALO, C), F32), pltpu.VMEM((tt, C), F32),
               pltpu.VMEM((7, tt + CONV_HALO - 8, C), F32), pltpu.VMEM((CONV_K, 8, C), F32)]
    if hosted is not None:
        in_specs += [ANY_SPEC] * hn
        args += hosted.arrays
        out_specs += [ANY_SPEC] * hn
        out_shape += hosted.out_shapes()
        scratch += hosted.sem_shapes()
    res = pl.pallas_call(
        body, name="conv_bwd", grid=(nt,),
        in_specs=in_specs, out_specs=tuple(out_specs), out_shape=tuple(out_shape), scratch_shapes=scratch,
        compiler_params=_params(("arbitrary",)),
    )(*args)
    return res[0], res[1], list(res[2:])


def _attn_bwd(q, kv, d_o, lse, delta, dil, prev, final, name, hosted=None):
    S = q.shape[0]
    rows = _Rows(dil, S)
    nb = rows.nb
    steps = dil * nb
    out_dt = BF16 if final else F32
    have_prev = prev is not None
    hn = hosted.n if hosted is not None else 0

    def body(*refs):
        refs = list(refs)
        q_ref, do_ref, lse_ref, dl_ref, kvc_ref, kvp_ref = refs[:6]
        del refs[:6]
        if have_prev:
            pdq_ref, pdkv_ref = refs[:2]
            del refs[:2]
        h_ins = refs[:hn]
        dq_ref, dkv_ref = refs[hn:hn + 2]
        h_outs = refs[hn + 2:2 * hn + 2]
        carry, tbl = refs[2 * hn + 2:2 * hn + 4]
        h_sems = refs[2 * hn + 4:]
        t = pl.program_id(0)
        n = t % nb

        @pl.when(t == 0)
        def _():
            if hosted is not None:
                hosted.start(h_ins, h_outs, h_sems)
            _fill_bias_table(tbl, rows, keys_first=True)
            carry[...] = jnp.zeros_like(carry)

        @pl.when(t < steps)
        def _():
            kv2 = jnp.concatenate([_ld(kvp_ref), _ld(kvc_ref)], axis=0)
            lse_t, dl_t = _ld(lse_ref).T, _ld(dl_ref).T
            lo_mask = lax.broadcasted_iota(jnp.int32, (2 * BLK, LANES), 1) < HEAD_DIM
            halves = [jnp.zeros((2 * BLK, LANES), F32) for _ in range(4)]
            for hk in range(N_KV_HEADS):
                k_lo, k_hi, v_lo, v_hi = _head_operands(kv2, hk, lo_mask)
                cols = [slice(b * LANES, (b + 1) * LANES) for b in (2 * hk, 2 * hk + 1)]
                q2 = jnp.concatenate([_ld(q_ref, cols[0]), _ld(q_ref, cols[1])], axis=0).astype(BF16)
                do2 = jnp.concatenate([_ld(do_ref, cols[0]), _ld(do_ref, cols[1])], axis=0).astype(BF16)
                dq2 = jnp.zeros((2 * BLK, LANES), F32)
                dks, dvs = [], []
                for which, (kk, vv) in enumerate(((k_lo, v_lo), (k_hi, v_hi))):
                    h0, h1 = 4 * hk + which, 4 * hk + 2 + which
                    s = _nt(kk, q2) + _bias2(tbl, n, h0, h1, axis=1)
                    lse2 = jnp.concatenate([lse_t[h0:h0 + 1, :], lse_t[h1:h1 + 1, :]], axis=1)
                    dl2 = jnp.concatenate([dl_t[h0:h0 + 1, :], dl_t[h1:h1 + 1, :]], axis=1)
                    p = jnp.exp(s - lse2)
                    ds = (p * (_nt(vv, do2) - dl2)).astype(BF16)
                    dq2 = dq2 + _tn(ds, kk)
                    dks.append(jnp.dot(ds, q2, preferred_element_type=F32))
                    dvs.append(jnp.dot(p.astype(BF16), do2, preferred_element_type=F32))
                dk_sum = jnp.where(lo_mask, dks[0], dks[1])
                dv_sum = jnp.where(lo_mask, dvs[0], dvs[1])
                for jp in range(2):
                    dq_blk = dq2[jp * BLK:(jp + 1) * BLK]
                    if have_prev:
                        dq_blk = dq_blk + _ld(pdq_ref, cols[jp])
                    if final:
                        dq_blk = dq_blk * (HEAD_DIM ** -0.5)
                    _st(dq_ref, dq_blk.astype(out_dt), cols[jp])
                half, pos = hk // 2, hk % 2
                here = lo_mask if pos == 0 else jnp.logical_not(lo_mask)
                dk_tot = dk_sum + pltpu.roll(dk_sum, HEAD_DIM, axis=1)
                dv_tot = dv_sum + pltpu.roll(dv_sum, HEAD_DIM, axis=1)
                halves[half] = halves[half] + jnp.where(here, dk_tot, 0.0)
                halves[2 + half] = halves[2 + half] + jnp.where(here, dv_tot, 0.0)
            for b in range(4):
                cols = slice(b * LANES, (b + 1) * LANES)
                done = carry[:, cols] + halves[b][0:BLK, :]
                if have_prev:
                    done = done + _ld(pdkv_ref, cols)
                _st(dkv_ref, done.astype(out_dt), cols)
                carry[:, cols] = halves[b][BLK:, :]

        @pl.when(t == steps)
        def _():
            done = carry[...]
            if have_prev:
                done = done + _ld(pdkv_ref)
            _st(dkv_ref, done.astype(out_dt))
            if hosted is not None:
                hosted.finish(h_ins, h_outs, h_sems)

    def spec(width, lag):
        def index(t):
            u = jnp.clip(t - lag, 0, steps - 1)
            return rows.index(u // nb, u % nb)
        return pl.BlockSpec(rows.block + (width,), index)

    def key_prev(t):
        u = jnp.minimum(t, steps - 1)
        return rows.index(u // nb, jnp.maximum(u % nb - 1, 0))

    in_specs = [spec(ATT_W, 0), spec(ATT_W, 0), spec(LANES, 0), spec(LANES, 0), spec(2 * KV_W, 0),
                pl.BlockSpec(rows.block + (2 * KV_W,), key_prev)]
    args = [rows.of(q), rows.of(d_o), rows.of(lse), rows.of(delta), rows.of(kv), rows.of(kv)]
    if have_prev:
        in_specs += [spec(ATT_W, 0), spec(2 * KV_W, 1)]
        args += [rows.of(prev[0]), rows.of(prev[1])]
    out_specs = [spec(ATT_W, 0), spec(2 * KV_W, 1)]
    out_shape = [jax.ShapeDtypeStruct(rows.view + (ATT_W,), out_dt),
                 jax.ShapeDtypeStruct(rows.view + (2 * KV_W,), out_dt)]
    scratch = [pltpu.VMEM((BLK, 2 * KV_W), F32), pltpu.VMEM((2 * N_Q_HEADS, 2 * BLK, BLK), F32)]
    if hosted is not None:
        in_specs += [ANY_SPEC] * hn
        args += hosted.arrays
        out_specs += [ANY_SPEC] * hn
        out_shape += hosted.out_shapes()
        scratch += hosted.sem_shapes()
    res = pl.pallas_call(
        body, name=name, grid=(steps + 1,),
        in_specs=in_specs, out_specs=tuple(out_specs), out_shape=tuple(out_shape), scratch_shapes=scratch,
        compiler_params=_params(("arbitrary",)),
    )(*args)
    return (res[0].reshape(S, ATT_W), res[1].reshape(S, 2 * KV_W)), list(res[2:])


def _dh(segments, w_in, x, dx2, g, hosted=None, tm=1024, tk=512):
    S, D = x.shape
    ns = len(segments)
    counts = [a.shape[1] // tk for a, _ in segments]
    starts = [sum(counts[:s]) for s in range(ns)]
    nk = sum(counts)
    hn = hosted.n if hosted is not None else 0

    def body(*refs):
        seg_refs = refs[:ns]
        w_ref, x_ref, dx2_ref, g_ref = refs[ns:ns + 4]
        h_ins = refs[ns + 4:ns + 4 + hn]
        gx_ref, gng_ref = refs[ns + 4 + hn:ns + 6 + hn]
        h_outs = refs[ns + 6 + hn:ns + 6 + 2 * hn]
        acc = refs[ns + 6 + 2 * hn]
        h_sems = refs[ns + 7 + 2 * hn:]
        k, i = pl.program_id(0), pl.program_id(1)

        @pl.when((i == 0) & (k == 0))
        def _():
            gng_ref[...] = jnp.zeros_like(gng_ref)
            if hosted is not None:
                hosted.start(h_ins, h_outs, h_sems)

        @pl.when(k == 0)
        def _():
            acc[i] = jnp.zeros(acc.shape[1:], F32)

        for s in range(ns):
            @pl.when((k >= starts[s]) & (k < starts[s] + counts[s]))
            def _(s=s):
                t = seg_refs[s][...]
                if segments[s][1]:
                    t = _perm_rows(t, True)
                acc[i] += jnp.dot(t, w_ref[...], preferred_element_type=F32)

        @pl.when(k == nk - 1)
        def _():
            dh = acc[i]
            xf = x_ref[...]
            r = lax.rsqrt(jnp.mean(xf * xf, axis=-1, keepdims=True) + NORM_EPS)
            nrm = xf * r
            gng_ref[...] += jnp.sum(dh * nrm, axis=0, keepdims=True)
            dn = dh * g_ref[...]
            gx_ref[...] = dx2_ref[...] + r * (dn - nrm * jnp.mean(dn * nrm, axis=-1, keepdims=True))

        if hosted is not None:
            @pl.when((i == S // tm - 1) & (k == nk - 1))
            def _():
                hosted.finish(h_ins, h_outs, h_sems)

    ni = S // tm
    row = pl.BlockSpec((tm, D), lambda k, i: (jnp.where(k == nk - 1, i, 0), 0))
    vec = pl.BlockSpec((1, D), lambda k, i: (0, 0))

    def seg_index(s):
        def index(k, i):
            j = k - starts[s]
            return jnp.where(j < 0, 0, jnp.where(j >= counts[s], ni - 1, i)), jnp.clip(j, 0, counts[s] - 1)
        return index

    in_specs = [pl.BlockSpec((tm, tk), seg_index(s)) for s in range(ns)]
    in_specs += [pl.BlockSpec((tk, D), lambda k, i: (k, 0)), row, row, vec]
    args = [a for a, _ in segments] + [w_in, x, dx2, g]
    out_specs = [row, vec]
    out_shape = [jax.ShapeDtypeStruct((S, D), F32), jax.ShapeDtypeStruct((1, D), F32)]
    scratch = [pltpu.VMEM((ni, tm, D), F32)]
    if hosted is not None:
        in_specs += [ANY_SPEC] * hn
        args += hosted.arrays
        out_specs += [ANY_SPEC] * hn
        out_shape += hosted.out_shapes()
        scratch += hosted.sem_shapes()
    res = pl.pallas_call(
        body, name="dh", grid=(nk, S // tm),
        in_specs=in_specs, out_specs=tuple(out_specs), out_shape=tuple(out_shape), scratch_shapes=scratch,
        compiler_params=_params(("arbitrary", "arbitrary"), BIG_VMEM_LIMIT),
    )(*args)
    return res[0], res[1], list(res[2:])


def _tn_matmul(pairs, layout, name, tm=512):
    arrays = []

    def slot(a):
        for i, b in enumerate(arrays):
            if b is a:
                return i
        arrays.append(a)
        return len(arrays) - 1

    slots = [(slot(u), slot(v)) for u, v in pairs]
    ready = [src for blocks in layout for src, _, _ in blocks if not isinstance(src, int)]
    M, K = pairs[0][1].shape
    n_in, n_ready, n_out = len(arrays), len(ready), len(layout)
    last = M // tm - 1

    def body(*refs):
        in_refs, ready_refs = refs[:n_in], refs[n_in:n_in + n_ready]
        o_refs, accs = refs[n_in + n_ready:n_in + n_ready + n_out], refs[n_in + n_ready + n_out:]

        @pl.when(pl.program_id(0) == 0)
        def _():
            for acc in accs:
                acc[...] = jnp.zeros_like(acc)

        for (iu, iv), acc in zip(slots, accs):
            vt = in_refs[iv][...]
            for c in range(0, acc.shape[0], 512):
                acc[c:c + 512, :] += _tn(in_refs[iu][:, c:c + 512], vt)

        @pl.when(pl.program_id(0) == last)
        def _():
            taken = 0
            for o_ref, blocks in zip(o_refs, layout):
                row = 0
                for src, r0, n in blocks:
                    if isinstance(src, int):
                        o_ref[row:row + n, :] = accs[src][r0:r0 + n, :].astype(o_ref.dtype)
                    else:
                        o_ref[row:row + n, :] = ready_refs[taken][r0:r0 + n, :]
                        taken += 1
                    row += n

    out_rows = [sum(n for _, _, n in blocks) for blocks in layout]
    return pl.pallas_call(
        body, name=name, grid=(M // tm,),
        in_specs=[pl.BlockSpec((tm, a.shape[1]), lambda m: (m, 0)) for a in arrays]
        + [pl.BlockSpec(r.shape, lambda m: (0, 0)) for r in ready],
        out_specs=tuple(pl.BlockSpec((rows, K), lambda m: (0, 0)) for rows in out_rows),
        out_shape=tuple(jax.ShapeDtypeStruct((rows, K), BF16) for rows in out_rows),
        scratch_shapes=[pltpu.VMEM((u.shape[1], K), F32) for u, _ in pairs],
        compiler_params=_params(("arbitrary",)),
    )(*arrays, *ready)


def _adamw(parts, w, m, v, name, tr=None, split=None, by_chip=False):
    R, C = w.shape
    tr = R if tr is None else tr
    parts = [parts] if split is None else list(parts)
    npar = len(parts)

    def total(p_ref):
        if by_chip:
            c = lax.axis_index("c")
            g = p_ref[c].astype(F32)
            for chip in range(1, N_DEV // 2):
                g = g + p_ref[2 * chip + c].astype(F32)
            return g
        g = p_ref[0].astype(F32)
        for dev in range(1, N_DEV):
            g = g + p_ref[dev].astype(F32)
        return g

    def body(*refs):
        w_ref, m_ref, v_ref, g_out, d_out, m_out, v_out = refs[npar:]
        if split is None:
            g = total(refs[0])
        else:
            g = jnp.where(_mesh_pos()[3] < split, total(refs[0]), total(refs[1]))
        mn = ADAM_B1 * m_ref[...] + (1.0 - ADAM_B1) * g
        vn = ADAM_B2 * v_ref[...] + (1.0 - ADAM_B2) * (g * g)
        m_hat = mn / (1.0 - ADAM_B1 ** ADAM_STEP)
        v_hat = vn / (1.0 - ADAM_B2 ** ADAM_STEP)
        g_out[...] = g
        d_out[...] = -ADAM_LR * (m_hat / (jnp.sqrt(v_hat) + ADAM_EPS) + ADAM_WD * w_ref[...])
        m_out[...] = mn
        v_out[...] = vn

    blk = pl.BlockSpec((tr, C), lambda i: (i, 0))
    shp = jax.ShapeDtypeStruct((R, C), F32)
    return pl.pallas_call(
        body, name=name, grid=(R // tr,),
        in_specs=[pl.BlockSpec((N_DEV, tr, C), lambda i: (0, i, 0))] * npar + [blk, blk, blk],
        out_specs=(blk, blk, blk, blk), out_shape=(shp, shp, shp, shp),
        compiler_params=_params(("parallel",)),
    )(*parts, w, m, v)


def _local_step(x, target, norm_g, w_in, conv_w, conv_b, ln_g, ln_b, w_out, gf, exchanges=None, first_weights=None,
                late_weights=None, first_rows=ATT_W + 2 * KV_W + ATT_W // 2):
    ex_out, ex_att, ex_conv = exchanges if exchanges is not None else (None, None, None)
    h_rm, h, *first = _norm_rows(x, norm_g, first_weights[0] if first_weights is not None else None)
    if first_weights is not None:
        w_in = first_weights[1](first)
    conv_cols = w_in.shape[0] - 2 * ATT_W - 2 * KV_W
    q, kv, a_gate, gates, *gathered = _inproj(
        h_rm, h, w_in,
        [(ATT_W, HEAD_DIM ** -0.5, True), (2 * KV_W, 1.0, True), (ATT_W, 1.0, True), (conv_cols, 1.0, False)],
        late_weights[0] if late_weights is not None else None)
    if late_weights is not None:
        conv_w, w_out = late_weights[1](gathered)

    alone = [_attn_fwd(q, kv, dil, "attn_fwd_d%d" % dil) for _, dil in PATTERNS[1:]]
    o, lse, y_att = _attn_fwd(q, kv, PATTERNS[0][1], "attn_fwd_d%d" % PATTERNS[0][1], alone, a_gate)
    conv_out, y_conv = _conv_fwd(gates, conv_w, conv_b, ln_g, ln_b)
    dx2, dxb, loss_cols, g_gf = _outproj_loss(x, y_att, y_conv, w_out, gf, target)

    d_o, d_a_gate, delta, dxb_rm = _dy_att(dxb, w_out, a_gate, o)
    g_w_out, = _tn_matmul([(y_att, dxb_rm), (y_conv, dxb)], [[(0, 0, ATT_W), (1, 0, y_conv.shape[1])]], "gw_out")
    acc, out_parts = None, []
    for idx, (_, dil) in enumerate(reversed(PATTERNS)):
        hosted = ex_out(g_w_out) if (idx == 0 and ex_out is not None) else None
        acc, outs = _attn_bwd(q, kv, d_o, lse, delta, dil, acc, idx == len(PATTERNS) - 1, "attn_bwd_d%d" % dil,
                              hosted)
        out_parts += outs
    dq, dkv = acc
    a_lo = first_rows - (ATT_W + 2 * KV_W)
    assert 0 < a_lo < ATT_W
    g_first, g_a_rest = _tn_matmul(
        [(dq, h_rm), (dkv, h_rm), (d_a_gate, h_rm)],
        [[(0, 0, ATT_W), (1, 0, 2 * KV_W), (2, 0, a_lo)], [(2, a_lo, ATT_W - a_lo)]], "gw_in_att")

    d_c_gate, d_conv, g_ln_g, g_ln_b, g_conv_b = _dy_conv(dxb, w_out, gates, conv_out, ln_g, ln_b)
    dgates, g_conv_w, att_parts = _conv_bwd(d_conv, gates, d_c_gate, conv_w,
                                            ex_att(g_first) if ex_att is not None else None)
    g_rest, = _tn_matmul([(dgates, h)], [[(g_a_rest, 0, ATT_W - a_lo), (0, 0, conv_cols)]], "gw_in_conv")
    grad_x, g_norm_g, conv_parts = _dh(
        [(dq, True), (dkv, True), (d_a_gate, True), (dgates, False)], w_in, x, dx2, norm_g,
        ex_conv(g_rest, g_conv_w) if ex_conv is not None else None)
    small = (g_norm_g, g_conv_b, g_ln_g, g_ln_b, g_gf, loss_cols)
    return grad_x, (g_first, g_rest), g_w_out, g_conv_w, small, (out_parts, att_parts, conv_parts)


def kernel(x, norm_g, w_in, conv_w, conv_b, conv_ln_g, conv_ln_b, w_out, final_norm_g, loss_target, m_norm_g, m_w_in, m_conv_w, m_conv_b, m_conv_ln_g, m_conv_ln_b, m_w_out, m_final_norm_g, v_norm_g, v_w_in, v_conv_w, v_conv_b, v_conv_ln_g, v_conv_ln_b, v_w_out, v_final_norm_g):
    S, D = x.shape[1], x.shape[2]
    win_sh, wout_sh, cw_sh = w_in[0].T, w_out[0], conv_w[0]
    cols_sh, rows_sh, ch_sh = win_sh.shape[0], wout_sh.shape[0], cw_sh.shape[1]

    def first_weights(gathered):
        return gathered[0].reshape(N_DEV * cols_sh, D)

    def late_weights(gathered):
        wout_all, cw_all = gathered
        conv_w_full = cw_all.transpose(1, 0, 2).reshape(CONV_K, N_DEV * ch_sh)
        return jnp.pad(conv_w_full, ((0, CONV_HALO - CONV_K), (0, 0))), wout_all.reshape(N_DEV * rows_sh, D)

    gf = final_norm_g.reshape(1, D)

    first = -(-(ATT_W + 2 * KV_W) // cols_sh)

    def ex_out(g_w_out):
        return _Exchange([g_w_out.reshape(N_DEV, rows_sh, D)], [(0, N_DEV)])

    same_core = (2, 4, 6)

    def ex_att(g_first):
        mine = _chip_sum(g_first.reshape(first, cols_sh, D), 0, "rs_att")
        return _Exchange([mine], [(0, first)], [same_core])

    def ex_conv(g_rest, g_conv_w):
        mine = _chip_sum(g_rest.reshape(N_DEV - first, cols_sh, D), first, "rs_conv")
        return _Exchange(
            [mine, g_conv_w[:CONV_K].reshape(CONV_K, N_DEV, ch_sh).transpose(1, 0, 2)],
            [(first, N_DEV), (0, N_DEV)], [same_core, None])

    grad_x, _, _, _, small, parts = _local_step(
        x[0], loss_target[0], norm_g, None, None, conv_b, conv_ln_g, conv_ln_b, None, gf,
        (ex_out, ex_att, ex_conv), (_Gather([win_sh.astype(BF16)]), first_weights),
        (_Gather([wout_sh.astype(BF16), cw_sh]), late_weights), first * cols_sh)
    (wout_parts,), (win_parts_lo,), (win_parts_hi, cw_parts) = parts

    small_pack = jnp.concatenate(list(small) + [jnp.zeros((2, D), F32)], axis=0)
    small_parts, = _exchange(_Exchange([small_pack], [None]), "gather_small")

    upd_win = _adamw((win_parts_lo, win_parts_hi), win_sh, m_w_in[0].T, v_w_in[0].T, "adamw_w_in",
                     tr=cols_sh // 2, split=first, by_chip=True)
    upd_wout = _adamw(wout_parts, wout_sh, m_w_out[0], v_w_out[0], "adamw_w_out", tr=128)
    upd_cw = _adamw(cw_parts, cw_sh, m_conv_w[0], v_conv_w[0], "adamw_conv_w")
    zeros3 = jnp.zeros((3, D), F32)
    stack = lambda a, b, c, d_, e: jnp.concatenate([a, b, c, d_, e.reshape(1, D), zeros3], axis=0)
    upd_small = _adamw(
        small_parts,
        stack(norm_g, conv_b, conv_ln_g, conv_ln_b, final_norm_g),
        stack(m_norm_g, m_conv_b, m_conv_ln_g, m_conv_ln_b, m_final_norm_g),
        stack(v_norm_g, v_conv_b, v_conv_ln_g, v_conv_ln_b, v_final_norm_g) + jnp.concatenate(
            [jnp.zeros((5, D), F32), jnp.ones((3, D), F32)], axis=0),
        "adamw_small")

    loss = 0.5 / D * jnp.sum(upd_small[0][5])

    def outputs(kind):
        sm = upd_small[kind]
        return [sm[0:1], upd_win[kind].T[None], upd_cw[kind][None], sm[1:2], sm[2:3], sm[3:4],
                upd_wout[kind][None], sm[4]]

    return (loss, grad_x[None], *outputs(0), *outputs(1), *outputs(2), *outputs(3))
```

```python
import jax
import jax.numpy as jnp
from jax import lax
from jax.experimental import pallas as pl
from jax.experimental.pallas import tpu as pltpu

F32 = jnp.float32
BF16 = jnp.bfloat16

HEAD_DIM = 64
N_KV_HEADS = 4
N_Q_HEADS = 16
ATT_W = 1024
KV_W = 256
CONV_K = 31
CONV_HALO = 32
PATTERNS = ((128, 1), (512, 4), (2048, 16))
BLK = 128
LANES = 128
NORM_EPS = 1e-6
LN_EPS = 1e-5
NEG = -1e30
N_DEV = 8
ADAM_LR, ADAM_B1, ADAM_B2, ADAM_EPS, ADAM_WD, ADAM_STEP = 0.001, 0.9, 0.999, 1e-08, 0.01, 10
VMEM_LIMIT = 48 * 1024 * 1024
BIG_VMEM_LIMIT = 58 * 1024 * 1024
SLOPES = tuple(2.0 ** (-8.0 * (h + 1) / N_Q_HEADS) for h in range(N_Q_HEADS))
MESH = pl.DeviceIdType.MESH


def _params(sem, vmem_limit=VMEM_LIMIT):
    return pltpu.CompilerParams(dimension_semantics=sem, vmem_limit_bytes=vmem_limit)


def _sigmoid(v):
    return 1.0 / (1.0 + jnp.exp(-v))


def _silu_and_grad(v):
    s = _sigmoid(v)
    return v * s, s * (1.0 + v * (1.0 - s))


ANY_SPEC = pl.BlockSpec(memory_space=pl.ANY)


def _mesh_pos():
    x, y, c = lax.axis_index("x"), lax.axis_index("y"), lax.axis_index("c")
    return x, y, c, 4 * x + 2 * y + c


def _flipped(k, x, y, c):
    px = 1 - x if k & 4 else x
    py = 1 - y if k & 2 else y
    pc = 1 - c if k & 1 else c
    return (px, py, pc), 4 * px + 2 * py + pc


class _Exchange:
    def __init__(self, arrays, dests, flips=None):
        self.arrays, self.dests, self.n = list(arrays), list(dests), len(arrays)
        self.flips = [tuple(range(1, N_DEV)) if f is None else tuple(f)
                      for f in (flips if flips is not None else [None] * self.n)]

    def out_shapes(self):
        return [jax.ShapeDtypeStruct((N_DEV,) + a.shape[-2:], a.dtype) for a in self.arrays]

    def sem_shapes(self):
        return [pltpu.SemaphoreType.DMA((self.n, N_DEV - 1)), pltpu.SemaphoreType.DMA((self.n, N_DEV - 1)),
                pltpu.SemaphoreType.DMA((self.n,))]

    def _when(self, a, dev, fn):
        if self.dests[a] is None:
            fn()
        else:
            lo, hi = self.dests[a]
            pl.when((dev >= lo) & (dev < hi))(fn)

    def _mine(self, ins, a, dev):
        return ins[a] if self.dests[a] is None else ins[a].at[dev - self.dests[a][0]]

    def _copy(self, ins, outs, sems, a, k, src_dev, slot, target):
        return pltpu.make_async_remote_copy(
            src_ref=self._mine(ins, a, src_dev), dst_ref=outs[a].at[slot],
            send_sem=sems[0].at[a, k - 1], recv_sem=sems[1].at[a, k - 1],
            device_id=target, device_id_type=MESH)

    def start(self, ins, outs, sems):
        x, y, c, me = _mesh_pos()
        for a in range(self.n):
            self._when(a, me, lambda a=a: pltpu.make_async_copy(
                self._mine(ins, a, me), outs[a].at[me], sems[2].at[a]).start())
            for k in self.flips[a]:
                target, peer = _flipped(k, x, y, c)
                self._when(a, peer, lambda a=a, k=k, target=target, peer=peer: self._copy(
                    ins, outs, sems, a, k, peer, me, target).start())

    def finish(self, ins, outs, sems):
        x, y, c, me = _mesh_pos()
        lo0 = [0 if d is None else d[0] for d in self.dests]
        for a in range(self.n):
            for k in self.flips[a]:
                target, peer = _flipped(k, x, y, c)
                self._when(a, me, lambda a=a, k=k, peer=peer: self._copy(
                    ins, outs, sems, a, k, lo0[a], peer, (x, y, c)).wait_recv())
            for k in self.flips[a]:
                target, peer = _flipped(k, x, y, c)
                self._when(a, peer, lambda a=a, k=k, target=target, peer=peer: self._copy(
                    ins, outs, sems, a, k, peer, me, target).wait_send())
            self._when(a, me, lambda a=a: pltpu.make_async_copy(
                self._mine(ins, a, me), outs[a].at[me], sems[2].at[a]).wait())


def _exchange(ex, name):
    na = ex.n

    def body(*refs):
        ins, outs, sems = refs[:na], refs[na:2 * na], refs[2 * na:]
        ex.start(ins, outs, sems)
        ex.finish(ins, outs, sems)

    return pl.pallas_call(
        body, name=name, out_shape=tuple(ex.out_shapes()),
        in_specs=[ANY_SPEC] * na, out_specs=tuple([ANY_SPEC] * na), scratch_shapes=ex.sem_shapes(),
    )(*ex.arrays)


def _chip_sum(pieces, lo, name):
    n, R, C = pieces.shape
    rows = 64
    assert R % rows == 0

    def body(p_ref, o_ref, mine_buf, other_buf, sum_buf, send_sems, recv_sems, local_sems, out_sems):
        x, y, c, me = _mesh_pos()

        def remote(i):
            return pltpu.make_async_remote_copy(
                src_ref=p_ref.at[i], dst_ref=other_buf.at[i], send_sem=send_sems.at[i], recv_sem=recv_sems.at[i],
                device_id=(x, y, 1 - c), device_id_type=MESH)

        def local(i):
            return pltpu.make_async_copy(p_ref.at[i], mine_buf.at[i], local_sems.at[i])

        def out(i):
            return pltpu.make_async_copy(sum_buf.at[i], o_ref.at[i], out_sems.at[i])

        summed_by = [(lo + i) % 2 for i in range(n)]
        for i in range(n):
            pl.when(c != summed_by[i])(remote(i).start)
            pl.when(c == summed_by[i])(local(i).start)
        for i in range(n):
            @pl.when(c == summed_by[i])
            def _(i=i):
                local(i).wait()
                remote(i).wait_recv()

                def chunk(j, carry):
                    r = pl.ds(pl.multiple_of(j * rows, rows), rows)
                    sum_buf[i, r, :] = (mine_buf[i, r, :].astype(F32) + other_buf[i, r, :].astype(F32)
                                        ).astype(sum_buf.dtype)
                    return carry

                lax.fori_loop(0, R // rows, chunk, 0)
                out(i).start()
        for i in range(n):
            pl.when(c == summed_by[i])(out(i).wait)
            pl.when(c != summed_by[i])(remote(i).wait_send)

    buf = pltpu.VMEM(pieces.shape, pieces.dtype)
    return pl.pallas_call(
        body, name=name, out_shape=jax.ShapeDtypeStruct(pieces.shape, pieces.dtype),
        in_specs=[ANY_SPEC], out_specs=ANY_SPEC,
        scratch_shapes=[buf, buf, buf] + [pltpu.SemaphoreType.DMA((n,))] * 4,
        compiler_params=pltpu.CompilerParams(vmem_limit_bytes=VMEM_LIMIT),
    )(pieces)


class _Gather:
    def __init__(self, arrays):
        self.arrays, self.n = list(arrays), len(arrays)

    def out_shapes(self):
        return [jax.ShapeDtypeStruct((N_DEV,) + a.shape, a.dtype) for a in self.arrays]

    def sem_shapes(self):
        return [pltpu.SemaphoreType.DMA((self.n, N_DEV - 1)), pltpu.SemaphoreType.DMA((self.n, N_DEV - 1)),
                pltpu.SemaphoreType.DMA((self.n,))]

    def _plan(self, ins, outs, sems):
        x, y, c, me = _mesh_pos()
        chips = [(1 - x, y), (x, 1 - y), (1 - x, 1 - y)]

        def copy(a, k, src, block, to):
            px, py, pc = block
            return pltpu.make_async_remote_copy(
                src_ref=src, dst_ref=outs[a].at[4 * px + 2 * py + pc], send_sem=sems[0].at[a, k],
                recv_sem=sems[1].at[a, k], device_id=to, device_id_type=MESH)

        def landed(a, block):
            px, py, pc = block
            return outs[a].at[4 * px + 2 * py + pc]

        local = [pltpu.make_async_copy(ins[a], outs[a].at[me], sems[2].at[a]) for a in range(self.n)]
        first = []
        for a in range(self.n):
            first.append(copy(a, 0, ins[a], (x, y, c), (x, y, 1 - c)))
            first += [copy(a, 1 + j, ins[a], (x, y, c), (*chip, c)) for j, chip in enumerate(chips[:2])]
        return (x, y, c), chips, copy, landed, local, first

    def start(self, ins, outs, sems):
        *_, local, first = self._plan(ins, outs, sems)
        for cp in local + first:
            cp.start()

    def finish(self, ins, outs, sems):
        (x, y, c), chips, copy, landed, local, first = self._plan(ins, outs, sems)
        south = c == 0
        came = (jnp.where(south, 1 - x, x), jnp.where(south, y, 1 - y), c)
        goes = (jnp.where(south, x, 1 - x), jnp.where(south, 1 - y, y), c)
        passed = []
        for a in range(self.n):
            for j, chip in enumerate(chips[:2]):
                copy(a, 1 + j, ins[a], (*chip, c), (x, y, c)).wait_recv()
            passed.append(copy(a, 3, landed(a, came), came, goes))
            passed += [copy(a, 4 + j, landed(a, (*chip, c)), (*chip, c), (x, y, 1 - c))
                       for j, chip in enumerate(chips[:2])]
        for cp in passed:
            cp.start()
        for a in range(self.n):
            diagonal = (*chips[2], c)
            copy(a, 3, ins[a], diagonal, (x, y, c)).wait_recv()
            cp = copy(a, 6, landed(a, diagonal), diagonal, (x, y, 1 - c))
            cp.start()
            passed.append(cp)
        for a in range(self.n):
            copy(a, 0, ins[a], (x, y, 1 - c), (x, y, c)).wait_recv()
            for j, chip in enumerate(chips):
                copy(a, 4 + j, ins[a], (*chip, 1 - c), (x, y, c)).wait_recv()
        for cp in first + passed:
            cp.wait_send()
        for cp in local:
            cp.wait()


CHUNK = 128
RESIDUES = 16
PER_RES = CHUNK // RESIDUES


def _perm_rows(tile, inverse):
    a = lax.broadcasted_iota(jnp.int32, (CHUNK, CHUNK), 0)
    b = lax.broadcasted_iota(jnp.int32, (CHUNK, CHUNK), 1)
    if inverse:
        a, b = b, a
    p = jnp.where(a == PER_RES * (b % RESIDUES) + b // RESIDUES, 1.0, 0.0).astype(BF16)
    parts = [jnp.dot(p, tile[c * CHUNK:(c + 1) * CHUNK], preferred_element_type=F32)
             for c in range(tile.shape[0] // CHUNK)]
    return jnp.concatenate(parts, axis=0).astype(BF16)


class _Rows:
    def __init__(self, dil, S):
        nc = S // CHUNK
        self.dil = dil
        if dil == 1:
            self.view, self.block, self.nb = (nc, CHUNK), (None, CHUNK), nc
            self.index = lambda r, b: (b, 0, 0)
        elif dil == 4:
            self.view, self.block, self.nb = (nc, 4, 4, PER_RES), (4, 4, None, PER_RES), nc // 4
            self.index = lambda r, b: (b, 0, r, 0, 0)
        elif dil == RESIDUES:
            self.view, self.block, self.nb = (nc, RESIDUES, PER_RES), (RESIDUES, None, PER_RES), nc // RESIDUES
            self.index = lambda r, b: (b, r, 0, 0)
        else:
            raise NotImplementedError(dil)

    def of(self, a):
        return a.reshape(self.view + (a.shape[-1],))

    def spec(self, width, which_block):
        return pl.BlockSpec(self.block + (width,), lambda r, n: self.index(r, which_block(n)))

    def pos(self, row):
        if self.dil == 1:
            return (row % PER_RES) * RESIDUES + row // PER_RES
        if self.dil == 4:
            return (row // 32) * 32 + (row % PER_RES) * 4 + (row % 32) // PER_RES
        return row


def _ld(ref, cols=slice(None)):
    v = ref[(slice(None),) * (len(ref.shape) - 1) + (cols,)]
    return v.reshape(BLK, v.shape[-1])


def _st(ref, val, cols=slice(None)):
    ref[(slice(None),) * (len(ref.shape) - 1) + (cols,)] = val.reshape(ref.shape[:-1] + (val.shape[-1],))


def _norm_rows(x, g, hosted=None, tm=512):
    S, D = x.shape
    hn = hosted.n if hosted is not None else 0

    def body(x_ref, g_ref, *rest):
        h_ins = rest[:hn]
        hrm_out, h_out = rest[hn:hn + 2]
        h_outs = rest[hn + 2:2 * hn + 2]
        h_sems = rest[2 * hn + 2:]
        i = pl.program_id(0)
        if hosted is not None:
            pl.when(i == 0)(lambda: hosted.start(h_ins, h_outs, h_sems))
        xf = x_ref[...]
        r = lax.rsqrt(jnp.mean(xf * xf, axis=-1, keepdims=True) + NORM_EPS)
        h = (xf * r * g_ref[...]).astype(BF16)
        h_out[...] = h
        hrm_out[...] = _perm_rows(h, False)
        if hosted is not None:
            pl.when(i == S // tm - 1)(lambda: hosted.finish(h_ins, h_outs, h_sems))

    row = pl.BlockSpec((tm, D), lambda i: (i, 0))
    in_specs, args = [row, pl.BlockSpec((1, D), lambda i: (0, 0))], [x, g]
    out_specs, out_shape, scratch = [row, row], [jax.ShapeDtypeStruct((S, D), BF16)] * 2, []
    if hosted is not None:
        in_specs += [ANY_SPEC] * hn
        args += hosted.arrays
        out_specs += [ANY_SPEC] * hn
        out_shape += hosted.out_shapes()
        scratch += hosted.sem_shapes()
    return pl.pallas_call(
        body, name="norm_rows", grid=(S // tm,),
        in_specs=in_specs, out_specs=tuple(out_specs), out_shape=tuple(out_shape), scratch_shapes=scratch,
        compiler_params=_params(("arbitrary",)),
    )(*args)


def _inproj(h_rm, h, w_t, segments, hosted=None, tm=1024, tn=512):
    S, D = h.shape
    ns = len(segments)
    ni = S // tm
    counts = [seg[0] // tn for seg in segments]
    starts = [sum(counts[:s]) for s in range(ns)]

    hn = hosted.n if hosted is not None else 0
    last_p = sum(counts)

    def body(hrm_ref, h_ref, w_ref, *rest):
        h_ins, rest = rest[:hn], rest[hn:]
        outs = rest[:ns]
        h_outs = rest[ns:ns + hn]
        hrm_scr, h_scr = rest[ns + hn:ns + 2 + hn]
        h_sems = rest[ns + 2 + hn:]
        p, i = pl.program_id(0), pl.program_id(1)

        if hosted is not None:
            @pl.when((p == 0) & (i == 0))
            def _():
                hosted.start(h_ins, h_outs, h_sems)

            @pl.when((p == last_p) & (i == ni - 1))
            def _():
                hosted.finish(h_ins, h_outs, h_sems)

        @pl.when(p == 0)
        def _():
            h_scr[i] = h_ref[...]
            hrm_scr[i] = hrm_ref[...]

        for s, (_, scale, rm) in enumerate(segments):
            @pl.when((p > starts[s]) & (p <= starts[s] + counts[s]))
            def _(s=s, scale=scale, rm=rm):
                acc = _nt((hrm_scr if rm else h_scr)[i], w_ref[...])
                outs[s][...] = acc * scale if scale != 1.0 else acc

    def out_index(s):
        def index(p, i):
            j = p - 1 - starts[s]
            row = jnp.where(j < 0, 0, jnp.where(j >= counts[s], ni - 1, i))
            return row, jnp.clip(j, 0, counts[s] - 1)
        return index

    first_pass = pl.BlockSpec((tm, D), lambda p, i: (jnp.where(p == 0, i, ni - 1), 0))
    out_specs = [pl.BlockSpec((tm, tn), out_index(s)) for s in range(ns)]
    out_shape = [jax.ShapeDtypeStruct((S, seg[0]), F32) for seg in segments]
    in_specs = [first_pass, first_pass, pl.BlockSpec((tn, D), lambda p, i: (jnp.maximum(p - 1, 0), 0))]
    args = [h_rm, h, w_t]
    scratch = [pltpu.VMEM((ni, tm, D), BF16), pltpu.VMEM((ni, tm, D), BF16)]
    if hosted is not None:
        in_specs += [ANY_SPEC] * hn
        args += hosted.arrays
        out_specs += [ANY_SPEC] * hn
        out_shape += hosted.out_shapes()
        scratch += hosted.sem_shapes()
    return pl.pallas_call(
        body, name="inproj", grid=(1 + last_p, ni),
        in_specs=in_specs, out_specs=tuple(out_specs), out_shape=tuple(out_shape), scratch_shapes=scratch,
        compiler_params=_params(("arbitrary", "arbitrary"), BIG_VMEM_LIMIT),
    )(*args)


def _fill_bias_table(tbl, rows, keys_first=False):
    shape = (2 * BLK, BLK) if keys_first else (BLK, 2 * BLK)
    qi = lax.broadcasted_iota(jnp.int32, shape, 1 if keys_first else 0)
    kj = lax.broadcasted_iota(jnp.int32, shape, 0 if keys_first else 1)
    dist = rows.pos(qi) - rows.pos(kj % BLK) + jnp.where(kj < BLK, BLK, 0)
    inside = (dist >= 0) & (dist <= BLK)
    negd = (dist * (-rows.dil)).astype(F32)
    for f, valid in enumerate((inside & (kj >= BLK), inside)):
        for h in range(N_Q_HEADS):
            tbl[f * N_Q_HEADS + h] = jnp.where(valid, SLOPES[h] * negd, NEG)


def _bias2(tbl, n, h0, h1, axis=0):
    base = jnp.where(n == 0, 0, N_Q_HEADS)
    return jnp.concatenate([tbl[base + h0], tbl[base + h1]], axis=axis)


def _head_operands(kv2, hk, lo_mask):
    half, pos = hk // 2, hk % 2
    out = []
    for base in (0, KV_W):
        t = kv2[:, base + half * LANES: base + (half + 1) * LANES]
        sw = pltpu.roll(t, HEAD_DIM, axis=1)
        at_lo, at_hi = (t, sw) if pos == 0 else (sw, t)
        out.append(jnp.where(lo_mask, at_lo, 0.0).astype(BF16))
        out.append(jnp.where(lo_mask, 0.0, at_hi).astype(BF16))
    return out


def _nt(a, b):
    return lax.dot_general(a, b, (((1,), (1,)), ((), ())), preferred_element_type=F32)


def _tn(a, b):
    return lax.dot_general(a, b, (((0,), (0,)), ((), ())), preferred_element_type=F32)


def _attn_fwd(q, kv, dil, name, prev=(), gate=None):
    S = q.shape[0]
    rows = _Rows(dil, S)
    nb = rows.nb
    have_prev, last = len(prev) > 0, gate is not None

    def body(*refs):
        refs = list(refs)
        q_ref, kvc_ref, kvp_ref = refs[:3]
        del refs[:3]
        po_refs, pl_refs = refs[0:2 * len(prev):2], refs[1:2 * len(prev):2]
        del refs[:2 * len(prev)]
        if last:
            gate_ref = refs.pop(0)
        o_ref, lse_ref = refs[:2]
        y_ref = refs[2] if last else None
        tbl = refs[-1]
        n = pl.program_id(1)

        @pl.when((pl.program_id(0) == 0) & (n == 0))
        def _():
            _fill_bias_table(tbl, rows)

        kv2 = jnp.concatenate([_ld(kvp_ref), _ld(kvc_ref)], axis=0)
        lo_mask = lax.broadcasted_iota(jnp.int32, (2 * BLK, LANES), 1) < HEAD_DIM
        lane = lax.broadcasted_iota(jnp.int32, (BLK, LANES), 1)
        stats = jnp.zeros((BLK, LANES), F32)
        for hk in range(N_KV_HEADS):
            k_lo, k_hi, v_lo, v_hi = _head_operands(kv2, hk, lo_mask)
            cols = [slice(b * LANES, (b + 1) * LANES) for b in (2 * hk, 2 * hk + 1)]
            q2 = jnp.concatenate([_ld(q_ref, cols[0]), _ld(q_ref, cols[1])], axis=0).astype(BF16)
            o2 = jnp.zeros((2 * BLK, LANES), F32)
            for which, (kk, vv) in enumerate(((k_lo, v_lo), (k_hi, v_hi))):
                h0, h1 = 4 * hk + which, 4 * hk + 2 + which
                s = _nt(q2, kk) + _bias2(tbl, n, h0, h1)
                m = jnp.max(s, axis=1, keepdims=True)
                p = jnp.exp(s - m)
                l = jnp.sum(p, axis=1, keepdims=True)
                o2 = o2 + jnp.dot(p.astype(BF16), vv, preferred_element_type=F32) * (1.0 / l)
                lse = m + jnp.log(l)
                stats = jnp.where(lane == h0, lse[0:BLK], stats)
                stats = jnp.where(lane == h1, lse[BLK:], stats)
            _st(o_ref, o2[0:BLK], cols[0])
            _st(o_ref, o2[BLK:], cols[1])
        if have_prev:
            others = [_ld(r) for r in pl_refs]
            top = stats
            for b in others:
                top = jnp.maximum(top, b)
            e_new = jnp.exp(stats - top)
            e_old = [jnp.exp(b - top) for b in others]
            total = e_new
            for e in e_old:
                total = total + e
            stats = top + jnp.log(total)
            inv = 1.0 / total
            w_new, w_old = e_new * inv, [e * inv for e in e_old]
        if have_prev or last:
            lo = lane < HEAD_DIM
            for blk in range(ATT_W // LANES):
                cols = slice(blk * LANES, (blk + 1) * LANES)
                o_blk = _ld(o_ref, cols)
                if have_prev:
                    pick = lambda w: jnp.where(lo, w[:, 2 * blk:2 * blk + 1], w[:, 2 * blk + 1:2 * blk + 2])
                    o_blk = o_blk * pick(w_new)
                    for po_ref, w in zip(po_refs, w_old):
                        o_blk = o_blk + _ld(po_ref, cols) * pick(w)
                    _st(o_ref, o_blk, cols)
                if last:
                    a = _ld(gate_ref, cols)
                    _st(y_ref, (o_blk * (a * _sigmoid(a))).astype(BF16), cols)
        _st(lse_ref, stats)

    here = lambda n: n
    before_n = lambda n: jnp.maximum(n - 1, 0)
    in_specs = [rows.spec(ATT_W, here), rows.spec(2 * KV_W, here), rows.spec(2 * KV_W, before_n)]
    args = [rows.of(q), rows.of(kv), rows.of(kv)]
    for o_other, lse_other in prev:
        in_specs += [rows.spec(ATT_W, here), rows.spec(LANES, here)]
        args += [rows.of(o_other), rows.of(lse_other)]
    out_specs = [rows.spec(ATT_W, here), rows.spec(LANES, here)]
    out_shape = [jax.ShapeDtypeStruct(rows.view + (ATT_W,), F32), jax.ShapeDtypeStruct(rows.view + (LANES,), F32)]
    if last:
        in_specs.append(rows.spec(ATT_W, here))
        args.append(rows.of(gate))
        out_specs.append(rows.spec(ATT_W, here))
        out_shape.append(jax.ShapeDtypeStruct(rows.view + (ATT_W,), BF16))
    res = pl.pallas_call(
        body, name=name, grid=(dil, nb),
        in_specs=in_specs, out_specs=tuple(out_specs), out_shape=tuple(out_shape),
        scratch_shapes=[pltpu.VMEM((2 * N_Q_HEADS, BLK, 2 * BLK), F32)],
        compiler_params=_params(("arbitrary", "arbitrary")),
    )(*args)
    return tuple(r.reshape(S, r.shape[-1]) for r in res)


def _shifted_copies(buf, phases):
    n = phases.shape[1]
    for b in range(1, 8):
        phases[b - 1] = buf[b:b + n, :]


def _window(buf, phases, start, cols):
    b = start % 8
    if b == 0:
        return buf[start:start + 8, cols]
    return phases[b - 1, start - b:start - b + 8, cols]


def _broadcast_taps(w_ref, wb):
    for j in range(CONV_K):
        wb[j] = jnp.broadcast_to(w_ref[j:j + 1, :], wb.shape[1:])


def _conv_fwd(gates, conv_w, conv_b, ln_g, ln_b, tt=256):
    S = gates.shape[0]
    C = conv_w.shape[1]
    hb = tt // CONV_HALO

    def body(val_ref, glu_ref, hval_ref, hglu_ref, gate_ref, w_ref, b_ref, g_ref, beta_ref,
             conv_ref, y_ref, hbuf, hph):
        i = pl.program_id(0)
        halo = hval_ref[...] * _sigmoid(hglu_ref[...])
        hbuf[0:CONV_HALO, :] = jnp.where(i > 0, halo, 0.0)
        hbuf[CONV_HALO:, :] = val_ref[...] * _sigmoid(glu_ref[...])
        _shifted_copies(hbuf, hph)
        for cb in range(C // LANES):
            cols = slice(cb * LANES, (cb + 1) * LANES)
            wj = [jnp.broadcast_to(w_ref[j:j + 1, cols], (8, LANES)) for j in range(CONV_K)]
            for rc in range(tt // 8):
                acc = jnp.zeros((8, LANES), F32)
                for j in range(CONV_K):
                    start = rc * 8 + CONV_HALO - (CONV_K - 1) + j
                    acc = acc + _window(hbuf, hph, start, cols) * wj[j]
                conv_ref[rc * 8:(rc + 1) * 8, cols] = acc
        cv = conv_ref[...] + b_ref[...]
        conv_ref[...] = cv
        mu = jnp.mean(cv, axis=-1, keepdims=True)
        xc = cv - mu
        var = jnp.mean(xc * xc, axis=-1, keepdims=True)
        ln = xc * lax.rsqrt(var + LN_EPS) * g_ref[...] + beta_ref[...]
        gt = gate_ref[...]
        y_ref[...] = (ln * _sigmoid(ln) * (gt * _sigmoid(gt))).astype(BF16)

    vec = pl.BlockSpec((1, C), lambda i: (0, 0))
    return pl.pallas_call(
        body, name="conv_fwd", grid=(S // tt,),
        in_specs=[pl.BlockSpec((tt, C), lambda i: (i, 0)),
                  pl.BlockSpec((tt, C), lambda i: (i, 1)),
                  pl.BlockSpec((CONV_HALO, C), lambda i: (jnp.maximum(i * hb - 1, 0), 0)),
                  pl.BlockSpec((CONV_HALO, C), lambda i: (jnp.maximum(i * hb - 1, 0), 1)),
                  pl.BlockSpec((tt, C), lambda i: (i, 2)),
                  pl.BlockSpec((CONV_HALO, C), lambda i: (0, 0)), vec, vec, vec],
        out_specs=(pl.BlockSpec((tt, C), lambda i: (i, 0)), pl.BlockSpec((tt, C), lambda i: (i, 0))),
        out_shape=(jax.ShapeDtypeStruct((S, C), F32), jax.ShapeDtypeStruct((S, C), BF16)),
        scratch_shapes=[pltpu.VMEM((tt + CONV_HALO, C), F32), pltpu.VMEM((7, tt + CONV_HALO - 8, C), F32)],
        compiler_params=_params(("parallel",)),
    )(gates, gates, gates, gates, gates, conv_w, conv_b, ln_g, ln_b)


def _outproj_loss(x, y_att, y_conv, w_out, gf, target, tm=512):
    S, D = x.shape
    E = y_att.shape[1]

    def body(x_ref, ya_ref, yc_ref, w_ref, gf_ref, t_ref, dx_ref, dxb_ref, loss_ref, ggf_ref):
        @pl.when(pl.program_id(0) == 0)
        def _():
            loss_ref[...] = jnp.zeros_like(loss_ref)
            ggf_ref[...] = jnp.zeros_like(ggf_ref)

        x2 = (x_ref[...] + jnp.dot(_perm_rows(ya_ref[...], True), w_ref[0:E, :], preferred_element_type=F32)
              + jnp.dot(yc_ref[...], w_ref[E:, :], preferred_element_type=F32))
        r = lax.rsqrt(jnp.mean(x2 * x2, axis=-1, keepdims=True) + NORM_EPS)
        nrm = x2 * r
        gfv = gf_ref[...]
        err = nrm * gfv - t_ref[...]
        loss_ref[...] += jnp.sum(err * err, axis=0, keepdims=True)
        dout = err * (1.0 / D)
        ggf_ref[...] += jnp.sum(dout * nrm, axis=0, keepdims=True)
        dn = dout * gfv
        dx2 = r * (dn - nrm * jnp.mean(dn * nrm, axis=-1, keepdims=True))
        dx_ref[...] = dx2
        dxb_ref[...] = dx2.astype(BF16)

    row = lambda w: pl.BlockSpec((tm, w), lambda i: (i, 0))
    vec = pl.BlockSpec((1, D), lambda i: (0, 0))
    return pl.pallas_call(
        body, name="outproj_loss", grid=(S // tm,),
        in_specs=[row(D), row(E), row(E), pl.BlockSpec((2 * E, D), lambda i: (0, 0)), vec, row(D)],
        out_specs=(row(D), row(D), vec, vec),
        out_shape=(jax.ShapeDtypeStruct((S, D), F32), jax.ShapeDtypeStruct((S, D), BF16),
                   jax.ShapeDtypeStruct((1, D), F32), jax.ShapeDtypeStruct((1, D), F32)),
        compiler_params=_params(("arbitrary",)),
    )(x, y_att, y_conv, w_out, gf, target)


def _split3(v):
    hi = v.astype(BF16)
    r1 = v - hi.astype(F32)
    mid = r1.astype(BF16)
    lo = (r1 - mid.astype(F32)).astype(BF16)
    return hi, mid, lo


def _dy_att(dxb, w_out, gates, o, tm=512):
    S, D = dxb.shape
    E = ATT_W

    def body(dx_ref, w_ref, a_ref, o_ref, do_ref, da_ref, dl_ref, dxr_ref):
        dxr = _perm_rows(dx_ref[...], False)
        dxr_ref[...] = dxr
        dya = _nt(dxr, w_ref[...])
        a = a_ref[...]
        ov = o_ref[...]
        sl, dsl = _silu_and_grad(a)
        d_o = dya * sl
        do_ref[...] = d_o
        da_ref[...] = (dya * ov * dsl).astype(BF16)
        ci = lax.broadcasted_iota(jnp.int32, (E, LANES), 0) // HEAD_DIM
        hi = lax.broadcasted_iota(jnp.int32, (E, LANES), 1)
        sel = jnp.where(ci == hi, 1.0, 0.0).astype(BF16)
        acc = jnp.zeros((tm, LANES), F32)
        for part in _split3(d_o * ov):
            acc = acc + jnp.dot(part, sel, preferred_element_type=F32)
        dl_ref[...] = acc

    row = lambda w: pl.BlockSpec((tm, w), lambda i: (i, 0))
    return pl.pallas_call(
        body, name="dy_att", grid=(S // tm,),
        in_specs=[row(D), pl.BlockSpec((E, D), lambda i: (0, 0)), row(E), row(E)],
        out_specs=(row(E), row(E), row(LANES), row(D)),
        out_shape=(jax.ShapeDtypeStruct((S, E), F32), jax.ShapeDtypeStruct((S, E), BF16),
                   jax.ShapeDtypeStruct((S, LANES), F32), jax.ShapeDtypeStruct((S, D), BF16)),
        compiler_params=_params(("parallel",)),
    )(dxb, w_out, gates, o)


def _dy_conv(dxb, w_out, gates, conv_out, ln_g, ln_b, tm=512):
    S, D = dxb.shape
    C = conv_out.shape[1]

    def body(dx_ref, w_ref, gate_ref, cv_ref, g_ref, beta_ref, dgate_ref, dconv_ref, gg_ref, gb_ref, gcb_ref):
        @pl.when(pl.program_id(0) == 0)
        def _():
            gg_ref[...] = jnp.zeros_like(gg_ref)
            gb_ref[...] = jnp.zeros_like(gb_ref)
            gcb_ref[...] = jnp.zeros_like(gcb_ref)

        dyc = _nt(dx_ref[...], w_ref[...])
        cv = cv_ref[...]
        mu = jnp.mean(cv, axis=-1, keepdims=True)
        xc = cv - mu
        rstd = lax.rsqrt(jnp.mean(xc * xc, axis=-1, keepdims=True) + LN_EPS)
        nrm = xc * rstd
        gv = g_ref[...]
        ln = nrm * gv + beta_ref[...]
        u, du = _silu_and_grad(ln)
        gt = gate_ref[...]
        g2, dg2 = _silu_and_grad(gt)
        dgate_ref[...] = (dyc * u * dg2).astype(BF16)
        d_ln = dyc * g2 * du
        gb_ref[...] += jnp.sum(d_ln, axis=0, keepdims=True)
        gg_ref[...] += jnp.sum(d_ln * nrm, axis=0, keepdims=True)
        dn = d_ln * gv
        d_conv = rstd * (dn - jnp.mean(dn, axis=-1, keepdims=True)
                         - nrm * jnp.mean(dn * nrm, axis=-1, keepdims=True))
        dconv_ref[...] = d_conv
        gcb_ref[...] += jnp.sum(d_conv, axis=0, keepdims=True)

    row = lambda w: pl.BlockSpec((tm, w), lambda i: (i, 0))
    vec = pl.BlockSpec((1, C), lambda i: (0, 0))
    return pl.pallas_call(
        body, name="dy_conv", grid=(S // tm,),
        in_specs=[row(D), pl.BlockSpec((C, D), lambda i: (1, 0)),
                  pl.BlockSpec((tm, C), lambda i: (i, 2)), row(C), vec, vec],
        out_specs=(row(C), row(C), vec, vec, vec),
        out_shape=(jax.ShapeDtypeStruct((S, C), BF16), jax.ShapeDtypeStruct((S, C), F32),
                   jax.ShapeDtypeStruct((1, C), F32), jax.ShapeDtypeStruct((1, C), F32),
                   jax.ShapeDtypeStruct((1, C), F32)),
        compiler_params=_params(("arbitrary",)),
    )(dxb, w_out, gates, conv_out, ln_g, ln_b)


def _conv_bwd(d_conv, gates, d_c_gate, conv_w, hosted=None, tt=256):
    S, C = d_conv.shape
    hb = tt // CONV_HALO
    nt = S // tt
    hn = hosted.n if hosted is not None else 0

    def body(*refs):
        dc_ref, dnext_ref, val_ref, glu_ref, dg_ref, w_ref = refs[:6]
        h_ins = refs[6:6 + hn]
        out_ref, gw_ref = refs[6 + hn:8 + hn]
        h_outs = refs[8 + hn:8 + 2 * hn]
        hbuf, dbuf, dhbuf, dph, wb = refs[8 + 2 * hn:13 + 2 * hn]
        h_sems = refs[13 + 2 * hn:]
        i = pl.program_id(0)

        @pl.when(i == 0)
        def _():
            gw_ref[...] = jnp.zeros_like(gw_ref)
            _broadcast_taps(w_ref, wb)
            if hosted is not None:
                hosted.start(h_ins, h_outs, h_sems)

        val = val_ref[...]
        sg = _sigmoid(glu_ref[...])
        hbuf[...] = val * sg
        dbuf[0:tt, :] = dc_ref[...]
        dbuf[tt:, :] = jnp.where(i < nt - 1, dnext_ref[...], 0.0)
        _shifted_copies(dbuf, dph)
        for cb in range(C // LANES):
            cols = slice(cb * LANES, (cb + 1) * LANES)
            gacc = [jnp.zeros((8, LANES), F32) for _ in range(CONV_K)]
            group = 2
            for rc0 in range(0, tt // 8, group):
                hcur = [hbuf[(rc0 + r) * 8:(rc0 + r + 1) * 8, cols] for r in range(group)]
                accs = [jnp.zeros((8, LANES), F32) for _ in range(group)]
                for j in range(CONV_K):
                    wj = wb[j, :, cols]
                    for r in range(group):
                        dwin = _window(dbuf, dph, (rc0 + r) * 8 + (CONV_K - 1) - j, cols)
                        accs[r] = accs[r] + dwin * wj
                        gacc[j] = gacc[j] + dwin * hcur[r]
                for r in range(group):
                    dhbuf[(rc0 + r) * 8:(rc0 + r + 1) * 8, cols] = accs[r]
            for j in range(CONV_K):
                gw_ref[j:j + 1, cols] += jnp.sum(gacc[j], axis=0, keepdims=True)
        d_h = dhbuf[...]
        out_ref[:, 0:C] = (d_h * sg).astype(BF16)
        out_ref[:, C:2 * C] = (d_h * val * sg * (1.0 - sg)).astype(BF16)
        out_ref[:, 2 * C:3 * C] = dg_ref[...]

        if hosted is not None:
            @pl.when(i == nt - 1)
            def _():
                hosted.finish(h_ins, h_outs, h_sems)

    tile = lambda col: pl.BlockSpec((tt, C), lambda i: (i, col))
    in_specs = [tile(0),
                pl.BlockSpec((CONV_HALO, C), lambda i: (jnp.minimum((i + 1) * hb, S // CONV_HALO - 1), 0)),
                tile(0), tile(1), tile(0),
                pl.BlockSpec((CONV_HALO, C), lambda i: (0, 0))]
    args = [d_conv, d_conv, gates, gates, d_c_gate, conv_w]
    out_specs = [pl.BlockSpec((tt, 3 * C), lambda i: (i, 0)), pl.BlockSpec((CONV_HALO, C), lambda i: (0, 0))]
    out_shape = [jax.ShapeDtypeStruct((S, 3 * C), BF16), jax.ShapeDtypeStruct((CONV_HALO, C), F32)]
    scratch = [pltpu.VMEM((tt, C), F32), pltpu.VMEM((tt + CONV_HALO, C), F32), pltpu.VMEM((tt, C), F32),
               pltpu.VMEM((7, tt + CONV_HALO - 8, C), F32), pltpu.VMEM((CONV_K, 8, C), F32)]
    if hosted is not None:
        in_specs += [ANY_SPEC] * hn
        args += hosted.arrays
        out_specs += [ANY_SPEC] * hn
        out_shape += hosted.out_shapes()
        scratch += hosted.sem_shapes()
    res = pl.pallas_call(
        body, name="conv_bwd", grid=(nt,),
        in_specs=in_specs, out_specs=tuple(out_specs), out_shape=tuple(out_shape), scratch_shapes=scratch,
        compiler_params=_params(("arbitrary",)),
    )(*args)
    return res[0], res[1], list(res[2:])


def _attn_bwd(q, kv, d_o, lse, delta, dil, prev, final, name, hosted=None):
    S = q.shape[0]
    rows = _Rows(dil, S)
    nb = rows.nb
    steps = dil * nb
    out_dt = BF16 if final else F32
    have_prev = prev is not None
    hn = hosted.n if hosted is not None else 0

    def body(*refs):
        refs = list(refs)
        q_ref, do_ref, lse_ref, dl_ref, kvc_ref, kvp_ref = refs[:6]
        del refs[:6]
        if have_prev:
            pdq_ref, pdkv_ref = refs[:2]
            del refs[:2]
        h_ins = refs[:hn]
        dq_ref, dkv_ref = refs[hn:hn + 2]
        h_outs = refs[hn + 2:2 * hn + 2]
        carry, tbl = refs[2 * hn + 2:2 * hn + 4]
        h_sems = refs[2 * hn + 4:]
        t = pl.program_id(0)
        n = t % nb

        @pl.when(t == 0)
        def _():
            if hosted is not None:
                hosted.start(h_ins, h_outs, h_sems)
            _fill_bias_table(tbl, rows, keys_first=True)
            carry[...] = jnp.zeros_like(carry)

        @pl.when(t < steps)
        def _():
            kv2 = jnp.concatenate([_ld(kvp_ref), _ld(kvc_ref)], axis=0)
            lse_t, dl_t = _ld(lse_ref).T, _ld(dl_ref).T
            lo_mask = lax.broadcasted_iota(jnp.int32, (2 * BLK, LANES), 1) < HEAD_DIM
            halves = [jnp.zeros((2 * BLK, LANES), F32) for _ in range(4)]
            for hk in range(N_KV_HEADS):
                k_lo, k_hi, v_lo, v_hi = _head_operands(kv2, hk, lo_mask)
                cols = [slice(b * LANES, (b + 1) * LANES) for b in (2 * hk, 2 * hk + 1)]
                q2 = jnp.concatenate([_ld(q_ref, cols[0]), _ld(q_ref, cols[1])], axis=0).astype(BF16)
                do2 = jnp.concatenate([_ld(do_ref, cols[0]), _ld(do_ref, cols[1])], axis=0).astype(BF16)
                dq2 = jnp.zeros((2 * BLK, LANES), F32)
                dks, dvs = [], []
                for which, (kk, vv) in enumerate(((k_lo, v_lo), (k_hi, v_hi))):
                    h0, h1 = 4 * hk + which, 4 * hk + 2 + which
                    s = _nt(kk, q2) + _bias2(tbl, n, h0, h1, axis=1)
                    lse2 = jnp.concatenate([lse_t[h0:h0 + 1, :], lse_t[h1:h1 + 1, :]], axis=1)
                    dl2 = jnp.concatenate([dl_t[h0:h0 + 1, :], dl_t[h1:h1 + 1, :]], axis=1)
                    p = jnp.exp(s - lse2)
                    ds = (p * (_nt(vv, do2) - dl2)).astype(BF16)
                    dq2 = dq2 + _tn(ds, kk)
                    dks.append(jnp.dot(ds, q2, preferred_element_type=F32))
                    dvs.append(jnp.dot(p.astype(BF16), do2, preferred_element_type=F32))
                dk_sum = jnp.where(lo_mask, dks[0], dks[1])
                dv_sum = jnp.where(lo_mask, dvs[0], dvs[1])
                for jp in range(2):
                    dq_blk = dq2[jp * BLK:(jp + 1) * BLK]
                    if have_prev:
                        dq_blk = dq_blk + _ld(pdq_ref, cols[jp])
                    if final:
                        dq_blk = dq_blk * (HEAD_DIM ** -0.5)
                    _st(dq_ref, dq_blk.astype(out_dt), cols[jp])
                half, pos = hk // 2, hk % 2
                here = lo_mask if pos == 0 else jnp.logical_not(lo_mask)
                dk_tot = dk_sum + pltpu.roll(dk_sum, HEAD_DIM, axis=1)
                dv_tot = dv_sum + pltpu.roll(dv_sum, HEAD_DIM, axis=1)
                halves[half] = halves[half] + jnp.where(here, dk_tot, 0.0)
                halves[2 + half] = halves[2 + half] + jnp.where(here, dv_tot, 0.0)
            for b in range(4):
                cols = slice(b * LANES, (b + 1) * LANES)
                done = carry[:, cols] + halves[b][0:BLK, :]
                if have_prev:
                    done = done + _ld(pdkv_ref, cols)
                _st(dkv_ref, done.astype(out_dt), cols)
                carry[:, cols] = halves[b][BLK:, :]

        @pl.when(t == steps)
        def _():
            done = carry[...]
            if have_prev:
                done = done + _ld(pdkv_ref)
            _st(dkv_ref, done.astype(out_dt))
            if hosted is not None:
                hosted.finish(h_ins, h_outs, h_sems)

    def spec(width, lag):
        def index(t):
            u = jnp.clip(t - lag, 0, steps - 1)
            return rows.index(u // nb, u % nb)
        return pl.BlockSpec(rows.block + (width,), index)

    def key_prev(t):
        u = jnp.minimum(t, steps - 1)
        return rows.index(u // nb, jnp.maximum(u % nb - 1, 0))

    in_specs = [spec(ATT_W, 0), spec(ATT_W, 0), spec(LANES, 0), spec(LANES, 0), spec(2 * KV_W, 0),
                pl.BlockSpec(rows.block + (2 * KV_W,), key_prev)]
    args = [rows.of(q), rows.of(d_o), rows.of(lse), rows.of(delta), rows.of(kv), rows.of(kv)]
    if have_prev:
        in_specs += [spec(ATT_W, 0), spec(2 * KV_W, 1)]
        args += [rows.of(prev[0]), rows.of(prev[1])]
    out_specs = [spec(ATT_W, 0), spec(2 * KV_W, 1)]
    out_shape = [jax.ShapeDtypeStruct(rows.view + (ATT_W,), out_dt),
                 jax.ShapeDtypeStruct(rows.view + (2 * KV_W,), out_dt)]
    scratch = [pltpu.VMEM((BLK, 2 * KV_W), F32), pltpu.VMEM((2 * N_Q_HEADS, 2 * BLK, BLK), F32)]
    if hosted is not None:
        in_specs += [ANY_SPEC] * hn
        args += hosted.arrays
        out_specs += [ANY_SPEC] * hn
        out_shape += hosted.out_shapes()
        scratch += hosted.sem_shapes()
    res = pl.pallas_call(
        body, name=name, grid=(steps + 1,),
        in_specs=in_specs, out_specs=tuple(out_specs), out_shape=tuple(out_shape), scratch_shapes=scratch,
        compiler_params=_params(("arbitrary",)),
    )(*args)
    return (res[0].reshape(S, ATT_W), res[1].reshape(S, 2 * KV_W)), list(res[2:])


def _dh(segments, w_in, x, dx2, g, hosted=None, tm=1024, tk=512):
    S, D = x.shape
    ns = len(segments)
    counts = [a.shape[1] // tk for a, _ in segments]
    starts = [sum(counts[:s]) for s in range(ns)]
    nk = sum(counts)
    hn = hosted.n if hosted is not None else 0

    def body(*refs):
        seg_refs = refs[:ns]
        w_ref, x_ref, dx2_ref, g_ref = refs[ns:ns + 4]
        h_ins = refs[ns + 4:ns + 4 + hn]
        gx_ref, gng_ref = refs[ns + 4 + hn:ns + 6 + hn]
        h_outs = refs[ns + 6 + hn:ns + 6 + 2 * hn]
        acc = refs[ns + 6 + 2 * hn]
        h_sems = refs[ns + 7 + 2 * hn:]
        k, i = pl.program_id(0), pl.program_id(1)

        @pl.when((i == 0) & (k == 0))
        def _():
            gng_ref[...] = jnp.zeros_like(gng_ref)
            if hosted is not None:
                hosted.start(h_ins, h_outs, h_sems)

        @pl.when(k == 0)
        def _():
            acc[i] = jnp.zeros(acc.shape[1:], F32)

        for s in range(ns):
            @pl.when((k >= starts[s]) & (k < starts[s] + counts[s]))
            def _(s=s):
                t = seg_refs[s][...]
                if segments[s][1]:
                    t = _perm_rows(t, True)
                acc[i] += jnp.dot(t, w_ref[...], preferred_element_type=F32)

        @pl.when(k == nk - 1)
        def _():
            dh = acc[i]
            xf = x_ref[...]
            r = lax.rsqrt(jnp.mean(xf * xf, axis=-1, keepdims=True) + NORM_EPS)
            nrm = xf * r
            gng_ref[...] += jnp.sum(dh * nrm, axis=0, keepdims=True)
            dn = dh * g_ref[...]
            gx_ref[...] = dx2_ref[...] + r * (dn - nrm * jnp.mean(dn * nrm, axis=-1, keepdims=True))

        if hosted is not None:
            @pl.when((i == S // tm - 1) & (k == nk - 1))
            def _():
                hosted.finish(h_ins, h_outs, h_sems)

    ni = S // tm
    row = pl.BlockSpec((tm, D), lambda k, i: (jnp.where(k == nk - 1, i, 0), 0))
    vec = pl.BlockSpec((1, D), lambda k, i: (0, 0))

    def seg_index(s):
        def index(k, i):
            j = k - starts[s]
            return jnp.where(j < 0, 0, jnp.where(j >= counts[s], ni - 1, i)), jnp.clip(j, 0, counts[s] - 1)
        return index

    in_specs = [pl.BlockSpec((tm, tk), seg_index(s)) for s in range(ns)]
    in_specs += [pl.BlockSpec((tk, D), lambda k, i: (k, 0)), row, row, vec]
    args = [a for a, _ in segments] + [w_in, x, dx2, g]
    out_specs = [row, vec]
    out_shape = [jax.ShapeDtypeStruct((S, D), F32), jax.ShapeDtypeStruct((1, D), F32)]
    scratch = [pltpu.VMEM((ni, tm, D), F32)]
    if hosted is not None:
        in_specs += [ANY_SPEC] * hn
        args += hosted.arrays
        out_specs += [ANY_SPEC] * hn
        out_shape += hosted.out_shapes()
        scratch += hosted.sem_shapes()
    res = pl.pallas_call(
        body, name="dh", grid=(nk, S // tm),
        in_specs=in_specs, out_specs=tuple(out_specs), out_shape=tuple(out_shape), scratch_shapes=scratch,
        compiler_params=_params(("arbitrary", "arbitrary"), BIG_VMEM_LIMIT),
    )(*args)
    return res[0], res[1], list(res[2:])


def _tn_matmul(pairs, layout, name, tm=1024):
    arrays = []

    def slot(a):
        for i, b in enumerate(arrays):
            if b is a:
                return i
        arrays.append(a)
        return len(arrays) - 1

    slots = [(slot(u), slot(v)) for u, v in pairs]
    ready = [src for blocks in layout for src, _, _ in blocks if not isinstance(src, int)]
    M, K = pairs[0][1].shape
    n_in, n_ready, n_out = len(arrays), len(ready), len(layout)
    last = M // tm - 1

    def body(*refs):
        in_refs, ready_refs = refs[:n_in], refs[n_in:n_in + n_ready]
        o_refs, accs = refs[n_in + n_ready:n_in + n_ready + n_out], refs[n_in + n_ready + n_out:]

        @pl.when(pl.program_id(0) == 0)
        def _():
            for acc in accs:
                acc[...] = jnp.zeros_like(acc)

        for (iu, iv), acc in zip(slots, accs):
            vt = in_refs[iv][...]
            for c in range(0, acc.shape[0], 512):
                acc[c:c + 512, :] += _tn(in_refs[iu][:, c:c + 512], vt)

        @pl.when(pl.program_id(0) == last)
        def _():
            taken = 0
            for o_ref, blocks in zip(o_refs, layout):
                row = 0
                for src, r0, n in blocks:
                    if isinstance(src, int):
                        o_ref[row:row + n, :] = accs[src][r0:r0 + n, :].astype(o_ref.dtype)
                    else:
                        o_ref[row:row + n, :] = ready_refs[taken][r0:r0 + n, :]
                        taken += 1
                    row += n

    out_rows = [sum(n for _, _, n in blocks) for blocks in layout]
    return pl.pallas_call(
        body, name=name, grid=(M // tm,),
        in_specs=[pl.BlockSpec((tm, a.shape[1]), lambda m: (m, 0)) for a in arrays]
        + [pl.BlockSpec(r.shape, lambda m: (0, 0)) for r in ready],
        out_specs=tuple(pl.BlockSpec((rows, K), lambda m: (0, 0)) for rows in out_rows),
        out_shape=tuple(jax.ShapeDtypeStruct((rows, K), BF16) for rows in out_rows),
        scratch_shapes=[pltpu.VMEM((u.shape[1], K), F32) for u, _ in pairs],
        compiler_params=_params(("arbitrary",)),
    )(*arrays, *ready)


def _adamw(parts, w, m, v, name, tr=None, split=None, by_chip=False):
    R, C = w.shape
    tr = R if tr is None else tr
    parts = [parts] if split is None else list(parts)
    npar = len(parts)

    def total(p_ref):
        if by_chip:
            c = lax.axis_index("c")
            g = p_ref[c].astype(F32)
            for chip in range(1, N_DEV // 2):
                g = g + p_ref[2 * chip + c].astype(F32)
            return g
        g = p_ref[0].astype(F32)
        for dev in range(1, N_DEV):
            g = g + p_ref[dev].astype(F32)
        return g

    def body(*refs):
        w_ref, m_ref, v_ref, g_out, d_out, m_out, v_out = refs[npar:]
        if split is None:
            g = total(refs[0])
        else:
            g = jnp.where(_mesh_pos()[3] < split, total(refs[0]), total(refs[1]))
        mn = ADAM_B1 * m_ref[...] + (1.0 - ADAM_B1) * g
        vn = ADAM_B2 * v_ref[...] + (1.0 - ADAM_B2) * (g * g)
        m_hat = mn / (1.0 - ADAM_B1 ** ADAM_STEP)
        v_hat = vn / (1.0 - ADAM_B2 ** ADAM_STEP)
        g_out[...] = g
        d_out[...] = -ADAM_LR * (m_hat / (jnp.sqrt(v_hat) + ADAM_EPS) + ADAM_WD * w_ref[...])
        m_out[...] = mn
        v_out[...] = vn

    blk = pl.BlockSpec((tr, C), lambda i: (i, 0))
    shp = jax.ShapeDtypeStruct((R, C), F32)
    return pl.pallas_call(
        body, name=name, grid=(R // tr,),
        in_specs=[pl.BlockSpec((N_DEV, tr, C), lambda i: (0, i, 0))] * npar + [blk, blk, blk],
        out_specs=(blk, blk, blk, blk), out_shape=(shp, shp, shp, shp),
        compiler_params=_params(("parallel",)),
    )(*parts, w, m, v)


def _local_step(x, target, norm_g, w_in, conv_w, conv_b, ln_g, ln_b, w_out, gf, exchanges=None, first_weights=None,
                late_weights=None, first_rows=ATT_W + 2 * KV_W + ATT_W // 2):
    ex_out, ex_att, ex_conv = exchanges if exchanges is not None else (None, None, None)
    h_rm, h, *first = _norm_rows(x, norm_g, first_weights[0] if first_weights is not None else None)
    if first_weights is not None:
        w_in = first_weights[1](first)
    conv_cols = w_in.shape[0] - 2 * ATT_W - 2 * KV_W
    q, kv, a_gate, gates, *gathered = _inproj(
        h_rm, h, w_in,
        [(ATT_W, HEAD_DIM ** -0.5, True), (2 * KV_W, 1.0, True), (ATT_W, 1.0, True), (conv_cols, 1.0, False)],
        late_weights[0] if late_weights is not None else None)
    if late_weights is not None:
        conv_w, w_out = late_weights[1](gathered)

    alone = [_attn_fwd(q, kv, dil, "attn_fwd_d%d" % dil) for _, dil in PATTERNS[1:]]
    o, lse, y_att = _attn_fwd(q, kv, PATTERNS[0][1], "attn_fwd_d%d" % PATTERNS[0][1], alone, a_gate)
    conv_out, y_conv = _conv_fwd(gates, conv_w, conv_b, ln_g, ln_b)
    dx2, dxb, loss_cols, g_gf = _outproj_loss(x, y_att, y_conv, w_out, gf, target)

    d_o, d_a_gate, delta, dxb_rm = _dy_att(dxb, w_out, a_gate, o)
    g_w_out, = _tn_matmul([(y_att, dxb_rm), (y_conv, dxb)], [[(0, 0, ATT_W), (1, 0, y_conv.shape[1])]], "gw_out")
    acc, out_parts = None, []
    for idx, (_, dil) in enumerate(reversed(PATTERNS)):
        hosted = ex_out(g_w_out) if (idx == 0 and ex_out is not None) else None
        acc, outs = _attn_bwd(q, kv, d_o, lse, delta, dil, acc, idx == len(PATTERNS) - 1, "attn_bwd_d%d" % dil,
                              hosted)
        out_parts += outs
    dq, dkv = acc
    a_lo = first_rows - (ATT_W + 2 * KV_W)
    assert 0 < a_lo < ATT_W
    g_first, g_a_rest = _tn_matmul(
        [(dq, h_rm), (dkv, h_rm), (d_a_gate, h_rm)],
        [[(0, 0, ATT_W), (1, 0, 2 * KV_W), (2, 0, a_lo)], [(2, a_lo, ATT_W - a_lo)]], "gw_in_att")

    d_c_gate, d_conv, g_ln_g, g_ln_b, g_conv_b = _dy_conv(dxb, w_out, gates, conv_out, ln_g, ln_b)
    dgates, g_conv_w, att_parts = _conv_bwd(d_conv, gates, d_c_gate, conv_w,
                                            ex_att(g_first) if ex_att is not None else None)
    g_rest, = _tn_matmul([(dgates, h)], [[(g_a_rest, 0, ATT_W - a_lo), (0, 0, conv_cols)]], "gw_in_conv")
    grad_x, g_norm_g, conv_parts = _dh(
        [(dq, True), (dkv, True), (d_a_gate, True), (dgates, False)], w_in, x, dx2, norm_g,
        ex_conv(g_rest, g_conv_w) if ex_conv is not None else None)
    small = (g_norm_g, g_conv_b, g_ln_g, g_ln_b, g_gf, loss_cols)
    return grad_x, (g_first, g_rest), g_w_out, g_conv_w, small, (out_parts, att_parts, conv_parts)


def kernel(x, norm_g, w_in, conv_w, conv_b, conv_ln_g, conv_ln_b, w_out, final_norm_g, loss_target, m_norm_g, m_w_in, m_conv_w, m_conv_b, m_conv_ln_g, m_conv_ln_b, m_w_out, m_final_norm_g, v_norm_g, v_w_in, v_conv_w, v_conv_b, v_conv_ln_g, v_conv_ln_b, v_w_out, v_final_norm_g):
    S, D = x.shape[1], x.shape[2]
    win_sh, wout_sh, cw_sh = w_in[0].T, w_out[0], conv_w[0]
    cols_sh, rows_sh, ch_sh = win_sh.shape[0], wout_sh.shape[0], cw_sh.shape[1]

    def first_weights(gathered):
        return gathered[0].reshape(N_DEV * cols_sh, D)

    def late_weights(gathered):
        wout_all, cw_all = gathered
        conv_w_full = cw_all.transpose(1, 0, 2).reshape(CONV_K, N_DEV * ch_sh)
        return jnp.pad(conv_w_full, ((0, CONV_HALO - CONV_K), (0, 0))), wout_all.reshape(N_DEV * rows_sh, D)

    gf = final_norm_g.reshape(1, D)

    first = -(-(ATT_W + 2 * KV_W) // cols_sh)

    def ex_out(g_w_out):
        return _Exchange([g_w_out.reshape(N_DEV, rows_sh, D)], [(0, N_DEV)])

    same_core = (2, 4, 6)

    def ex_att(g_first):
        mine = _chip_sum(g_first.reshape(first, cols_sh, D), 0, "rs_att")
        return _Exchange([mine], [(0, first)], [same_core])

    def ex_conv(g_rest, g_conv_w):
        mine = _chip_sum(g_rest.reshape(N_DEV - first, cols_sh, D), first, "rs_conv")
        return _Exchange(
            [mine, g_conv_w[:CONV_K].reshape(CONV_K, N_DEV, ch_sh).transpose(1, 0, 2)],
            [(first, N_DEV), (0, N_DEV)], [same_core, None])

    grad_x, _, _, _, small, parts = _local_step(
        x[0], loss_target[0], norm_g, None, None, conv_b, conv_ln_g, conv_ln_b, None, gf,
        (ex_out, ex_att, ex_conv), (_Gather([win_sh.astype(BF16)]), first_weights),
        (_Gather([wout_sh.astype(BF16), cw_sh]), late_weights), first * cols_sh)
    (wout_parts,), (win_parts_lo,), (win_parts_hi, cw_parts) = parts

    small_pack = jnp.concatenate(list(small) + [jnp.zeros((2, D), F32)], axis=0)
    small_parts, = _exchange(_Exchange([small_pack], [None]), "gather_small")

    upd_win = _adamw((win_parts_lo, win_parts_hi), win_sh, m_w_in[0].T, v_w_in[0].T, "adamw_w_in",
                     tr=cols_sh // 2, split=first, by_chip=True)
    upd_wout = _adamw(wout_parts, wout_sh, m_w_out[0], v_w_out[0], "adamw_w_out", tr=128)
    upd_cw = _adamw(cw_parts, cw_sh, m_conv_w[0], v_conv_w[0], "adamw_conv_w")
    zeros3 = jnp.zeros((3, D), F32)
    stack = lambda a, b, c, d_, e: jnp.concatenate([a, b, c, d_, e.reshape(1, D), zeros3], axis=0)
    upd_small = _adamw(
        small_parts,
        stack(norm_g, conv_b, conv_ln_g, conv_ln_b, final_norm_g),
        stack(m_norm_g, m_conv_b, m_conv_ln_g, m_conv_ln_b, m_final_norm_g),
        stack(v_norm_g, v_conv_b, v_conv_ln_g, v_conv_ln_b, v_final_norm_g) + jnp.concatenate(
            [jnp.zeros((5, D), F32), jnp.ones((3, D), F32)], axis=0),
        "adamw_small")

    loss = 0.5 / D * jnp.sum(upd_small[0][5])

    def outputs(kind):
        sm = upd_small[kind]
        return [sm[0:1], upd_win[kind].T[None], upd_cw[kind][None], sm[1:2], sm[2:3], sm[3:4],
                upd_wout[kind][None], sm[4]]

    return (loss, grad_x[None], *outputs(0), *outputs(1), *outputs(2), *outputs(3))
```

```python
import jax
import jax.numpy as jnp
from jax import lax
from jax.experimental import pallas as pl
from jax.experimental.pallas import tpu as pltpu

F32 = jnp.float32
BF16 = jnp.bfloat16

HEAD_DIM = 64
N_KV_HEADS = 4
N_Q_HEADS = 16
ATT_W = 1024
KV_W = 256
CONV_K = 31
CONV_HALO = 32
PATTERNS = ((128, 1), (512, 4), (2048, 16))
BLK = 128
LANES = 128
NORM_EPS = 1e-6
LN_EPS = 1e-5
NEG = -1e30
N_DEV = 8
ADAM_LR, ADAM_B1, ADAM_B2, ADAM_EPS, ADAM_WD, ADAM_STEP = 0.001, 0.9, 0.999, 1e-08, 0.01, 10
VMEM_LIMIT = 48 * 1024 * 1024
BIG_VMEM_LIMIT = 58 * 1024 * 1024
SLOPES = tuple(2.0 ** (-8.0 * (h + 1) / N_Q_HEADS) for h in range(N_Q_HEADS))
MESH = pl.DeviceIdType.MESH


def _params(sem, vmem_limit=VMEM_LIMIT):
    return pltpu.CompilerParams(dimension_semantics=sem, vmem_limit_bytes=vmem_limit)


def _sigmoid(v):
    return 1.0 / (1.0 + jnp.exp(-v))


def _silu_and_grad(v):
    s = _sigmoid(v)
    return v * s, s * (1.0 + v * (1.0 - s))


ANY_SPEC = pl.BlockSpec(memory_space=pl.ANY)


def _mesh_pos():
    x, y, c = lax.axis_index("x"), lax.axis_index("y"), lax.axis_index("c")
    return x, y, c, 4 * x + 2 * y + c


def _flipped(k, x, y, c):
    px = 1 - x if k & 4 else x
    py = 1 - y if k & 2 else y
    pc = 1 - c if k & 1 else c
    return (px, py, pc), 4 * px + 2 * py + pc


class _Exchange:
    def __init__(self, arrays, dests, flips=None):
        self.arrays, self.dests, self.n = list(arrays), list(dests), len(arrays)
        self.flips = [tuple(range(1, N_DEV)) if f is None else tuple(f)
                      for f in (flips if flips is not None else [None] * self.n)]

    def out_shapes(self):
        return [jax.ShapeDtypeStruct((N_DEV,) + a.shape[-2:], a.dtype) for a in self.arrays]

    def sem_shapes(self):
        return [pltpu.SemaphoreType.DMA((self.n, N_DEV - 1)), pltpu.SemaphoreType.DMA((self.n, N_DEV - 1)),
                pltpu.SemaphoreType.DMA((self.n,))]

    def _when(self, a, dev, fn):
        if self.dests[a] is None:
            fn()
        else:
            lo, hi = self.dests[a]
            pl.when((dev >= lo) & (dev < hi))(fn)

    def _mine(self, ins, a, dev):
        return ins[a] if self.dests[a] is None else ins[a].at[dev - self.dests[a][0]]

    def _copy(self, ins, outs, sems, a, k, src_dev, slot, target):
        return pltpu.make_async_remote_copy(
            src_ref=self._mine(ins, a, src_dev), dst_ref=outs[a].at[slot],
            send_sem=sems[0].at[a, k - 1], recv_sem=sems[1].at[a, k - 1],
            device_id=target, device_id_type=MESH)

    def start(self, ins, outs, sems):
        x, y, c, me = _mesh_pos()
        for a in range(self.n):
            self._when(a, me, lambda a=a: pltpu.make_async_copy(
                self._mine(ins, a, me), outs[a].at[me], sems[2].at[a]).start())
            for k in self.flips[a]:
                target, peer = _flipped(k, x, y, c)
                self._when(a, peer, lambda a=a, k=k, target=target, peer=peer: self._copy(
                    ins, outs, sems, a, k, peer, me, target).start())

    def finish(self, ins, outs, sems):
        x, y, c, me = _mesh_pos()
        lo0 = [0 if d is None else d[0] for d in self.dests]
        for a in range(self.n):
            for k in self.flips[a]:
                target, peer = _flipped(k, x, y, c)
                self._when(a, me, lambda a=a, k=k, peer=peer: self._copy(
                    ins, outs, sems, a, k, lo0[a], peer, (x, y, c)).wait_recv())
            for k in self.flips[a]:
                target, peer = _flipped(k, x, y, c)
                self._when(a, peer, lambda a=a, k=k, target=target, peer=peer: self._copy(
                    ins, outs, sems, a, k, peer, me, target).wait_send())
            self._when(a, me, lambda a=a: pltpu.make_async_copy(
                self._mine(ins, a, me), outs[a].at[me], sems[2].at[a]).wait())


def _exchange(ex, name):
    na = ex.n

    def body(*refs):
        ins, outs, sems = refs[:na], refs[na:2 * na], refs[2 * na:]
        ex.start(ins, outs, sems)
        ex.finish(ins, outs, sems)

    return pl.pallas_call(
        body, name=name, out_shape=tuple(ex.out_shapes()),
        in_specs=[ANY_SPEC] * na, out_specs=tuple([ANY_SPEC] * na), scratch_shapes=ex.sem_shapes(),
    )(*ex.arrays)


def _chip_sum(pieces, lo, name):
    n, R, C = pieces.shape
    rows = 64
    assert R % rows == 0

    def body(p_ref, o_ref, mine_buf, other_buf, sum_buf, send_sems, recv_sems, local_sems, out_sems):
        x, y, c, me = _mesh_pos()

        def remote(i):
            return pltpu.make_async_remote_copy(
                src_ref=p_ref.at[i], dst_ref=other_buf.at[i], send_sem=send_sems.at[i], recv_sem=recv_sems.at[i],
                device_id=(x, y, 1 - c), device_id_type=MESH)

        def local(i):
            return pltpu.make_async_copy(p_ref.at[i], mine_buf.at[i], local_sems.at[i])

        def out(i):
            return pltpu.make_async_copy(sum_buf.at[i], o_ref.at[i], out_sems.at[i])

        summed_by = [(lo + i) % 2 for i in range(n)]
        for i in range(n):
            pl.when(c != summed_by[i])(remote(i).start)
            pl.when(c == summed_by[i])(local(i).start)
        for i in range(n):
            @pl.when(c == summed_by[i])
            def _(i=i):
                local(i).wait()
                remote(i).wait_recv()

                def chunk(j, carry):
                    r = pl.ds(pl.multiple_of(j * rows, rows), rows)
                    sum_buf[i, r, :] = (mine_buf[i, r, :].astype(F32) + other_buf[i, r, :].astype(F32)
                                        ).astype(sum_buf.dtype)
                    return carry

                lax.fori_loop(0, R // rows, chunk, 0)
                out(i).start()
        for i in range(n):
            pl.when(c == summed_by[i])(out(i).wait)
            pl.when(c != summed_by[i])(remote(i).wait_send)

    buf = pltpu.VMEM(pieces.shape, pieces.dtype)
    return pl.pallas_call(
        body, name=name, out_shape=jax.ShapeDtypeStruct(pieces.shape, pieces.dtype),
        in_specs=[ANY_SPEC], out_specs=ANY_SPEC,
        scratch_shapes=[buf, buf, buf] + [pltpu.SemaphoreType.DMA((n,))] * 4,
        compiler_params=pltpu.CompilerParams(vmem_limit_bytes=VMEM_LIMIT),
    )(pieces)


class _Gather:
    def __init__(self, arrays):
        self.arrays, self.n = list(arrays), len(arrays)

    def out_shapes(self):
        return [jax.ShapeDtypeStruct((N_DEV,) + a.shape, a.dtype) for a in self.arrays]

    def sem_shapes(self):
        return [pltpu.SemaphoreType.DMA((self.n, N_DEV - 1)), pltpu.SemaphoreType.DMA((self.n, N_DEV - 1)),
                pltpu.SemaphoreType.DMA((self.n,))]

    def _plan(self, ins, outs, sems):
        x, y, c, me = _mesh_pos()
        chips = [(1 - x, y), (x, 1 - y), (1 - x, 1 - y)]

        def copy(a, k, src, block, to):
            px, py, pc = block
            return pltpu.make_async_remote_copy(
                src_ref=src, dst_ref=outs[a].at[4 * px + 2 * py + pc], send_sem=sems[0].at[a, k],
                recv_sem=sems[1].at[a, k], device_id=to, device_id_type=MESH)

        def landed(a, block):
            px, py, pc = block
            return outs[a].at[4 * px + 2 * py + pc]

        local = [pltpu.make_async_copy(ins[a], outs[a].at[me], sems[2].at[a]) for a in range(self.n)]
        first = []
        for a in range(self.n):
            first.append(copy(a, 0, ins[a], (x, y, c), (x, y, 1 - c)))
            first += [copy(a, 1 + j, ins[a], (x, y, c), (*chip, c)) for j, chip in enumerate(chips[:2])]
        return (x, y, c), chips, copy, landed, local, first

    def start(self, ins, outs, sems):
        *_, local, first = self._plan(ins, outs, sems)
        for cp in local + first:
            cp.start()

    def finish(self, ins, outs, sems):
        (x, y, c), chips, copy, landed, local, first = self._plan(ins, outs, sems)
        south = c == 0
        came = (jnp.where(south, 1 - x, x), jnp.where(south, y, 1 - y), c)
        goes = (jnp.where(south, x, 1 - x), jnp.where(south, 1 - y, y), c)
        passed = []
        for a in range(self.n):
            for j, chip in enumerate(chips[:2]):
                copy(a, 1 + j, ins[a], (*chip, c), (x, y, c)).wait_recv()
            passed.append(copy(a, 3, landed(a, came), came, goes))
            passed += [copy(a, 4 + j, landed(a, (*chip, c)), (*chip, c), (x, y, 1 - c))
                       for j, chip in enumerate(chips[:2])]
        for cp in passed:
            cp.start()
        for a in range(self.n):
            diagonal = (*chips[2], c)
            copy(a, 3, ins[a], diagonal, (x, y, c)).wait_recv()
            cp = copy(a, 6, landed(a, diagonal), diagonal, (x, y, 1 - c))
            cp.start()
            passed.append(cp)
        for a in range(self.n):
            copy(a, 0, ins[a], (x, y, 1 - c), (x, y, c)).wait_recv()
            for j, chip in enumerate(chips):
                copy(a, 4 + j, ins[a], (*chip, 1 - c), (x, y, c)).wait_recv()
        for cp in first + passed:
            cp.wait_send()
        for cp in local:
            cp.wait()


CHUNK = 128
RESIDUES = 16
PER_RES = CHUNK // RESIDUES


def _perm_rows(tile, inverse):
    a = lax.broadcasted_iota(jnp.int32, (CHUNK, CHUNK), 0)
    b = lax.broadcasted_iota(jnp.int32, (CHUNK, CHUNK), 1)
    if inverse:
        a, b = b, a
    p = jnp.where(a == PER_RES * (b % RESIDUES) + b // RESIDUES, 1.0, 0.0).astype(BF16)
    parts = [jnp.dot(p, tile[c * CHUNK:(c + 1) * CHUNK], preferred_element_type=F32)
             for c in range(tile.shape[0] // CHUNK)]
    return jnp.concatenate(parts, axis=0).astype(BF16)


class _Rows:
    def __init__(self, dil, S):
        nc = S // CHUNK
        self.dil = dil
        if dil == 1:
            self.view, self.block, self.nb = (nc, CHUNK), (None, CHUNK), nc
            self.index = lambda r, b: (b, 0, 0)
        elif dil == 4:
            self.view, self.block, self.nb = (nc, 4, 4, PER_RES), (4, 4, None, PER_RES), nc // 4
            self.index = lambda r, b: (b, 0, r, 0, 0)
        elif dil == RESIDUES:
            self.view, self.block, self.nb = (nc, RESIDUES, PER_RES), (RESIDUES, None, PER_RES), nc // RESIDUES
            self.index = lambda r, b: (b, r, 0, 0)
        else:
            raise NotImplementedError(dil)

    def of(self, a):
        return a.reshape(self.view + (a.shape[-1],))

    def spec(self, width, which_block):
        return pl.BlockSpec(self.block + (width,), lambda r, n: self.index(r, which_block(n)))

    def pos(self, row):
        if self.dil == 1:
            return (row % PER_RES) * RESIDUES + row // PER_RES
        if self.dil == 4:
            return (row // 32) * 32 + (row % PER_RES) * 4 + (row % 32) // PER_RES
        return row


def _ld(ref, cols=slice(None)):
    v = ref[(slice(None),) * (len(ref.shape) - 1) + (cols,)]
    return v.reshape(BLK, v.shape[-1])


def _st(ref, val, cols=slice(None)):
    ref[(slice(None),) * (len(ref.shape) - 1) + (cols,)] = val.reshape(ref.shape[:-1] + (val.shape[-1],))


def _norm_rows(x, g, hosted=None, tm=512):
    S, D = x.shape
    hn = hosted.n if hosted is not None else 0

    def body(x_ref, g_ref, *rest):
        h_ins = rest[:hn]
        hrm_out, h_out = rest[hn:hn + 2]
        h_outs = rest[hn + 2:2 * hn + 2]
        h_sems = rest[2 * hn + 2:]
        i = pl.program_id(0)
        if hosted is not None:
            pl.when(i == 0)(lambda: hosted.start(h_ins, h_outs, h_sems))
        xf = x_ref[...]
        r = lax.rsqrt(jnp.mean(xf * xf, axis=-1, keepdims=True) + NORM_EPS)
        h = (xf * r * g_ref[...]).astype(BF16)
        h_out[...] = h
        hrm_out[...] = _perm_rows(h, False)
        if hosted is not None:
            pl.when(i == S // tm - 1)(lambda: hosted.finish(h_ins, h_outs, h_sems))

    row = pl.BlockSpec((tm, D), lambda i: (i, 0))
    in_specs, args = [row, pl.BlockSpec((1, D), lambda i: (0, 0))], [x, g]
    out_specs, out_shape, scratch = [row, row], [jax.ShapeDtypeStruct((S, D), BF16)] * 2, []
    if hosted is not None:
        in_specs += [ANY_SPEC] * hn
        args += hosted.arrays
        out_specs += [ANY_SPEC] * hn
        out_shape += hosted.out_shapes()
        scratch += hosted.sem_shapes()
    return pl.pallas_call(
        body, name="norm_rows", grid=(S // tm,),
        in_specs=in_specs, out_specs=tuple(out_specs), out_shape=tuple(out_shape), scratch_shapes=scratch,
        compiler_params=_params(("arbitrary",)),
    )(*args)


def _inproj(h_rm, h, w_t, segments, hosted=None, tm=1024, tn=512):
    S, D = h.shape
    ns = len(segments)
    ni = S // tm
    counts = [seg[0] // tn for seg in segments]
    starts = [sum(counts[:s]) for s in range(ns)]

    hn = hosted.n if hosted is not None else 0
    last_p = sum(counts)

    def body(hrm_ref, h_ref, w_ref, *rest):
        h_ins, rest = rest[:hn], rest[hn:]
        outs = rest[:ns]
        h_outs = rest[ns:ns + hn]
        hrm_scr, h_scr = rest[ns + hn:ns + 2 + hn]
        h_sems = rest[ns + 2 + hn:]
        p, i = pl.program_id(0), pl.program_id(1)

        if hosted is not None:
            @pl.when((p == 0) & (i == 0))
            def _():
                hosted.start(h_ins, h_outs, h_sems)

            @pl.when((p == last_p - 1) & (i == ni - 1))
            def _():
                hosted.finish(h_ins, h_outs, h_sems)

        @pl.when(p == 0)
        def _():
            h_scr[i] = h_ref[...]
            hrm_scr[i] = hrm_ref[...]

        for s, (_, scale, rm) in enumerate(segments):
            @pl.when((p >= starts[s]) & (p < starts[s] + counts[s]))
            def _(s=s, scale=scale, rm=rm):
                acc = _nt((hrm_scr if rm else h_scr)[i], w_ref[...])
                outs[s][...] = acc * scale if scale != 1.0 else acc

    def out_index(s):
        def index(p, i):
            j = p - starts[s]
            row = jnp.where(j < 0, 0, jnp.where(j >= counts[s], ni - 1, i))
            return row, jnp.clip(j, 0, counts[s] - 1)
        return index

    first_pass = pl.BlockSpec((tm, D), lambda p, i: (jnp.where(p == 0, i, ni - 1), 0))
    out_specs = [pl.BlockSpec((tm, tn), out_index(s)) for s in range(ns)]
    out_shape = [jax.ShapeDtypeStruct((S, seg[0]), F32) for seg in segments]
    in_specs = [first_pass, first_pass, pl.BlockSpec((tn, D), lambda p, i: (p, 0))]
    args = [h_rm, h, w_t]
    scratch = [pltpu.VMEM((ni, tm, D), BF16), pltpu.VMEM((ni, tm, D), BF16)]
    if hosted is not None:
        in_specs += [ANY_SPEC] * hn
        args += hosted.arrays
        out_specs += [ANY_SPEC] * hn
        out_shape += hosted.out_shapes()
        scratch += hosted.sem_shapes()
    return pl.pallas_call(
        body, name="inproj", grid=(last_p, ni),
        in_specs=in_specs, out_specs=tuple(out_specs), out_shape=tuple(out_shape), scratch_shapes=scratch,
        compiler_params=_params(("arbitrary", "arbitrary"), BIG_VMEM_LIMIT),
    )(*args)


def _fill_bias_table(tbl, rows, keys_first=False):
    shape = (2 * BLK, BLK) if keys_first else (BLK, 2 * BLK)
    qi = lax.broadcasted_iota(jnp.int32, shape, 1 if keys_first else 0)
    kj = lax.broadcasted_iota(jnp.int32, shape, 0 if keys_first else 1)
    dist = rows.pos(qi) - rows.pos(kj % BLK) + jnp.where(kj < BLK, BLK, 0)
    inside = (dist >= 0) & (dist <= BLK)
    negd = (dist * (-rows.dil)).astype(F32)
    for f, valid in enumerate((inside & (kj >= BLK), inside)):
        for h in range(N_Q_HEADS):
            tbl[f * N_Q_HEADS + h] = jnp.where(valid, SLOPES[h] * negd, NEG)


def _bias2(tbl, n, h0, h1, axis=0):
    base = jnp.where(n == 0, 0, N_Q_HEADS)
    return jnp.concatenate([tbl[base + h0], tbl[base + h1]], axis=axis)


def _head_operands(kv2, hk, lo_mask):
    half, pos = hk // 2, hk % 2
    out = []
    for base in (0, KV_W):
        t = kv2[:, base + half * LANES: base + (half + 1) * LANES]
        sw = pltpu.roll(t, HEAD_DIM, axis=1)
        at_lo, at_hi = (t, sw) if pos == 0 else (sw, t)
        out.append(jnp.where(lo_mask, at_lo, 0.0).astype(BF16))
        out.append(jnp.where(lo_mask, 0.0, at_hi).astype(BF16))
    return out


def _nt(a, b):
    return lax.dot_general(a, b, (((1,), (1,)), ((), ())), preferred_element_type=F32)


def _tn(a, b):
    return lax.dot_general(a, b, (((0,), (0,)), ((), ())), preferred_element_type=F32)


def _attn_fwd(q, kv, dil, name, prev=(), gate=None):
    S = q.shape[0]
    rows = _Rows(dil, S)
    nb = rows.nb
    have_prev, last = len(prev) > 0, gate is not None

    def body(*refs):
        refs = list(refs)
        q_ref, kvc_ref, kvp_ref = refs[:3]
        del refs[:3]
        po_refs, pl_refs = refs[0:2 * len(prev):2], refs[1:2 * len(prev):2]
        del refs[:2 * len(prev)]
        if last:
            gate_ref = refs.pop(0)
        o_ref, lse_ref = refs[:2]
        y_ref = refs[2] if last else None
        tbl = refs[-1]
        n = pl.program_id(1)

        @pl.when((pl.program_id(0) == 0) & (n == 0))
        def _():
            _fill_bias_table(tbl, rows)

        kv2 = jnp.concatenate([_ld(kvp_ref), _ld(kvc_ref)], axis=0)
        lo_mask = lax.broadcasted_iota(jnp.int32, (2 * BLK, LANES), 1) < HEAD_DIM
        lane = lax.broadcasted_iota(jnp.int32, (BLK, LANES), 1)
        stats = jnp.zeros((BLK, LANES), F32)
        for hk in range(N_KV_HEADS):
            k_lo, k_hi, v_lo, v_hi = _head_operands(kv2, hk, lo_mask)
            cols = [slice(b * LANES, (b + 1) * LANES) for b in (2 * hk, 2 * hk + 1)]
            q2 = jnp.concatenate([_ld(q_ref, cols[0]), _ld(q_ref, cols[1])], axis=0).astype(BF16)
            o2 = jnp.zeros((2 * BLK, LANES), F32)
            for which, (kk, vv) in enumerate(((k_lo, v_lo), (k_hi, v_hi))):
                h0, h1 = 4 * hk + which, 4 * hk + 2 + which
                s = _nt(q2, kk) + _bias2(tbl, n, h0, h1)
                m = jnp.max(s, axis=1, keepdims=True)
                p = jnp.exp(s - m)
                l = jnp.sum(p, axis=1, keepdims=True)
                o2 = o2 + jnp.dot(p.astype(BF16), vv, preferred_element_type=F32) * (1.0 / l)
                lse = m + jnp.log(l)
                stats = jnp.where(lane == h0, lse[0:BLK], stats)
                stats = jnp.where(lane == h1, lse[BLK:], stats)
            _st(o_ref, o2[0:BLK], cols[0])
            _st(o_ref, o2[BLK:], cols[1])
        if have_prev:
            others = [_ld(r) for r in pl_refs]
            top = stats
            for b in others:
                top = jnp.maximum(top, b)
            e_new = jnp.exp(stats - top)
            e_old = [jnp.exp(b - top) for b in others]
            total = e_new
            for e in e_old:
                total = total + e
            stats = top + jnp.log(total)
            inv = 1.0 / total
            w_new, w_old = e_new * inv, [e * inv for e in e_old]
        if have_prev or last:
            lo = lane < HEAD_DIM
            for blk in range(ATT_W // LANES):
                cols = slice(blk * LANES, (blk + 1) * LANES)
                o_blk = _ld(o_ref, cols)
                if have_prev:
                    pick = lambda w: jnp.where(lo, w[:, 2 * blk:2 * blk + 1], w[:, 2 * blk + 1:2 * blk + 2])
                    o_blk = o_blk * pick(w_new)
                    for po_ref, w in zip(po_refs, w_old):
                        o_blk = o_blk + _ld(po_ref, cols) * pick(w)
                    _st(o_ref, o_blk, cols)
                if last:
                    a = _ld(gate_ref, cols)
                    _st(y_ref, (o_blk * (a * _sigmoid(a))).astype(BF16), cols)
        _st(lse_ref, stats)

    here = lambda n: n
    before_n = lambda n: jnp.maximum(n - 1, 0)
    in_specs = [rows.spec(ATT_W, here), rows.spec(2 * KV_W, here), rows.spec(2 * KV_W, before_n)]
    args = [rows.of(q), rows.of(kv), rows.of(kv)]
    for o_other, lse_other in prev:
        in_specs += [rows.spec(ATT_W, here), rows.spec(LANES, here)]
        args += [rows.of(o_other), rows.of(lse_other)]
    out_specs = [rows.spec(ATT_W, here), rows.spec(LANES, here)]
    out_shape = [jax.ShapeDtypeStruct(rows.view + (ATT_W,), F32), jax.ShapeDtypeStruct(rows.view + (LANES,), F32)]
    if last:
        in_specs.append(rows.spec(ATT_W, here))
        args.append(rows.of(gate))
        out_specs.append(rows.spec(ATT_W, here))
        out_shape.append(jax.ShapeDtypeStruct(rows.view + (ATT_W,), BF16))
    res = pl.pallas_call(
        body, name=name, grid=(dil, nb),
        in_specs=in_specs, out_specs=tuple(out_specs), out_shape=tuple(out_shape),
        scratch_shapes=[pltpu.VMEM((2 * N_Q_HEADS, BLK, 2 * BLK), F32)],
        compiler_params=_params(("arbitrary", "arbitrary")),
    )(*args)
    return tuple(r.reshape(S, r.shape[-1]) for r in res)


def _shifted_copies(buf, phases):
    n = phases.shape[1]
    for b in range(1, 8):
        phases[b - 1] = buf[b:b + n, :]


def _window(buf, phases, start, cols):
    b = start % 8
    if b == 0:
        return buf[start:start + 8, cols]
    return phases[b - 1, start - b:start - b + 8, cols]


def _broadcast_taps(w_ref, wb):
    for j in range(CONV_K):
        wb[j] = jnp.broadcast_to(w_ref[j:j + 1, :], wb.shape[1:])


def _conv_fwd(gates, conv_w, conv_b, ln_g, ln_b, tt=256):
    S = gates.shape[0]
    C = conv_w.shape[1]
    hb = tt // CONV_HALO

    def body(val_ref, glu_ref, hval_ref, hglu_ref, gate_ref, w_ref, b_ref, g_ref, beta_ref,
             conv_ref, y_ref, hbuf, hph):
        i = pl.program_id(0)
        halo = hval_ref[...] * _sigmoid(hglu_ref[...])
        hbuf[0:CONV_HALO, :] = jnp.where(i > 0, halo, 0.0)
        hbuf[CONV_HALO:, :] = val_ref[...] * _sigmoid(glu_ref[...])
        _shifted_copies(hbuf, hph)
        for cb in range(C // LANES):
            cols = slice(cb * LANES, (cb + 1) * LANES)
            wj = [jnp.broadcast_to(w_ref[j:j + 1, cols], (8, LANES)) for j in range(CONV_K)]
            for rc in range(tt // 8):
                acc = jnp.zeros((8, LANES), F32)
                for j in range(CONV_K):
                    start = rc * 8 + CONV_HALO - (CONV_K - 1) + j
                    acc = acc + _window(hbuf, hph, start, cols) * wj[j]
                conv_ref[rc * 8:(rc + 1) * 8, cols] = acc
        cv = conv_ref[...] + b_ref[...]
        conv_ref[...] = cv
        mu = jnp.mean(cv, axis=-1, keepdims=True)
        xc = cv - mu
        var = jnp.mean(xc * xc, axis=-1, keepdims=True)
        ln = xc * lax.rsqrt(var + LN_EPS) * g_ref[...] + beta_ref[...]
        gt = gate_ref[...]
        y_ref[...] = (ln * _sigmoid(ln) * (gt * _sigmoid(gt))).astype(BF16)

    vec = pl.BlockSpec((1, C), lambda i: (0, 0))
    return pl.pallas_call(
        body, name="conv_fwd", grid=(S // tt,),
        in_specs=[pl.BlockSpec((tt, C), lambda i: (i, 0)),
                  pl.BlockSpec((tt, C), lambda i: (i, 1)),
                  pl.BlockSpec((CONV_HALO, C), lambda i: (jnp.maximum(i * hb - 1, 0), 0)),
                  pl.BlockSpec((CONV_HALO, C), lambda i: (jnp.maximum(i * hb - 1, 0), 1)),
                  pl.BlockSpec((tt, C), lambda i: (i, 2)),
                  pl.BlockSpec((CONV_HALO, C), lambda i: (0, 0)), vec, vec, vec],
        out_specs=(pl.BlockSpec((tt, C), lambda i: (i, 0)), pl.BlockSpec((tt, C), lambda i: (i, 0))),
        out_shape=(jax.ShapeDtypeStruct((S, C), F32), jax.ShapeDtypeStruct((S, C), BF16)),
        scratch_shapes=[pltpu.VMEM((tt + CONV_HALO, C), F32), pltpu.VMEM((7, tt + CONV_HALO - 8, C), F32)],
        compiler_params=_params(("parallel",)),
    )(gates, gates, gates, gates, gates, conv_w, conv_b, ln_g, ln_b)


def _outproj_loss(x, y_att, y_conv, w_out, gf, target, tm=512):
    S, D = x.shape
    E = y_att.shape[1]

    def body(x_ref, ya_ref, yc_ref, w_ref, gf_ref, t_ref, dx_ref, dxb_ref, loss_ref, ggf_ref):
        @pl.when(pl.program_id(0) == 0)
        def _():
            loss_ref[...] = jnp.zeros_like(loss_ref)
            ggf_ref[...] = jnp.zeros_like(ggf_ref)

        x2 = (x_ref[...] + jnp.dot(_perm_rows(ya_ref[...], True), w_ref[0:E, :], preferred_element_type=F32)
              + jnp.dot(yc_ref[...], w_ref[E:, :], preferred_element_type=F32))
        r = lax.rsqrt(jnp.mean(x2 * x2, axis=-1, keepdims=True) + NORM_EPS)
        nrm = x2 * r
        gfv = gf_ref[...]
        err = nrm * gfv - t_ref[...]
        loss_ref[...] += jnp.sum(err * err, axis=0, keepdims=True)
        dout = err * (1.0 / D)
        ggf_ref[...] += jnp.sum(dout * nrm, axis=0, keepdims=True)
        dn = dout * gfv
        dx2 = r * (dn - nrm * jnp.mean(dn * nrm, axis=-1, keepdims=True))
        dx_ref[...] = dx2
        dxb_ref[...] = dx2.astype(BF16)

    row = lambda w: pl.BlockSpec((tm, w), lambda i: (i, 0))
    vec = pl.BlockSpec((1, D), lambda i: (0, 0))
    return pl.pallas_call(
        body, name="outproj_loss", grid=(S // tm,),
        in_specs=[row(D), row(E), row(E), pl.BlockSpec((2 * E, D), lambda i: (0, 0)), vec, row(D)],
        out_specs=(row(D), row(D), vec, vec),
        out_shape=(jax.ShapeDtypeStruct((S, D), F32), jax.ShapeDtypeStruct((S, D), BF16),
                   jax.ShapeDtypeStruct((1, D), F32), jax.ShapeDtypeStruct((1, D), F32)),
        compiler_params=_params(("arbitrary",)),
    )(x, y_att, y_conv, w_out, gf, target)


def _split3(v):
    hi = v.astype(BF16)
    r1 = v - hi.astype(F32)
    mid = r1.astype(BF16)
    lo = (r1 - mid.astype(F32)).astype(BF16)
    return hi, mid, lo


def _dy_att(dxb, w_out, gates, o, tm=512):
    S, D = dxb.shape
    E = ATT_W

    def body(dx_ref, w_ref, a_ref, o_ref, do_ref, da_ref, dl_ref, dxr_ref):
        dxr = _perm_rows(dx_ref[...], False)
        dxr_ref[...] = dxr
        dya = _nt(dxr, w_ref[...])
        a = a_ref[...]
        ov = o_ref[...]
        sl, dsl = _silu_and_grad(a)
        d_o = dya * sl
        do_ref[...] = d_o
        da_ref[...] = (dya * ov * dsl).astype(BF16)
        ci = lax.broadcasted_iota(jnp.int32, (E, LANES), 0) // HEAD_DIM
        hi = lax.broadcasted_iota(jnp.int32, (E, LANES), 1)
        sel = jnp.where(ci == hi, 1.0, 0.0).astype(BF16)
        acc = jnp.zeros((tm, LANES), F32)
        for part in _split3(d_o * ov):
            acc = acc + jnp.dot(part, sel, preferred_element_type=F32)
        dl_ref[...] = acc

    row = lambda w: pl.BlockSpec((tm, w), lambda i: (i, 0))
    return pl.pallas_call(
        body, name="dy_att", grid=(S // tm,),
        in_specs=[row(D), pl.BlockSpec((E, D), lambda i: (0, 0)), row(E), row(E)],
        out_specs=(row(E), row(E), row(LANES), row(D)),
        out_shape=(jax.ShapeDtypeStruct((S, E), F32), jax.ShapeDtypeStruct((S, E), BF16),
                   jax.ShapeDtypeStruct((S, LANES), F32), jax.ShapeDtypeStruct((S, D), BF16)),
        compiler_params=_params(("parallel",)),
    )(dxb, w_out, gates, o)


def _dy_conv(dxb, w_out, gates, conv_out, ln_g, ln_b, tm=512):
    S, D = dxb.shape
    C = conv_out.shape[1]

    def body(dx_ref, w_ref, gate_ref, cv_ref, g_ref, beta_ref, dgate_ref, dconv_ref, gg_ref, gb_ref, gcb_ref):
        @pl.when(pl.program_id(0) == 0)
        def _():
            gg_ref[...] = jnp.zeros_like(gg_ref)
            gb_ref[...] = jnp.zeros_like(gb_ref)
            gcb_ref[...] = jnp.zeros_like(gcb_ref)

        dyc = _nt(dx_ref[...], w_ref[...])
        cv = cv_ref[...]
        mu = jnp.mean(cv, axis=-1, keepdims=True)
        xc = cv - mu
        rstd = lax.rsqrt(jnp.mean(xc * xc, axis=-1, keepdims=True) + LN_EPS)
        nrm = xc * rstd
        gv = g_ref[...]
        ln = nrm * gv + beta_ref[...]
        u, du = _silu_and_grad(ln)
        gt = gate_ref[...]
        g2, dg2 = _silu_and_grad(gt)
        dgate_ref[...] = (dyc * u * dg2).astype(BF16)
        d_ln = dyc * g2 * du
        gb_ref[...] += jnp.sum(d_ln, axis=0, keepdims=True)
        gg_ref[...] += jnp.sum(d_ln * nrm, axis=0, keepdims=True)
        dn = d_ln * gv
        d_conv = rstd * (dn - jnp.mean(dn, axis=-1, keepdims=True)
                         - nrm * jnp.mean(dn * nrm, axis=-1, keepdims=True))
        dconv_ref[...] = d_conv
        gcb_ref[...] += jnp.sum(d_conv, axis=0, keepdims=True)

    row = lambda w: pl.BlockSpec((tm, w), lambda i: (i, 0))
    vec = pl.BlockSpec((1, C), lambda i: (0, 0))
    return pl.pallas_call(
        body, name="dy_conv", grid=(S // tm,),
        in_specs=[row(D), pl.BlockSpec((C, D), lambda i: (1, 0)),
                  pl.BlockSpec((tm, C), lambda i: (i, 2)), row(C), vec, vec],
        out_specs=(row(C), row(C), vec, vec, vec),
        out_shape=(jax.ShapeDtypeStruct((S, C), BF16), jax.ShapeDtypeStruct((S, C), F32),
                   jax.ShapeDtypeStruct((1, C), F32), jax.ShapeDtypeStruct((1, C), F32),
                   jax.ShapeDtypeStruct((1, C), F32)),
        compiler_params=_params(("arbitrary",)),
    )(dxb, w_out, gates, conv_out, ln_g, ln_b)


def _conv_bwd(d_conv, gates, d_c_gate, conv_w, hosted=None, tt=256):
    S, C = d_conv.shape
    hb = tt // CONV_HALO
    nt = S // tt
    hn = hosted.n if hosted is not None else 0

    def body(*refs):
        dc_ref, dnext_ref, val_ref, glu_ref, dg_ref, w_ref = refs[:6]
        h_ins = refs[6:6 + hn]
        out_ref, gw_ref = refs[6 + hn:8 + hn]
        h_outs = refs[8 + hn:8 + 2 * hn]
        hbuf, dbuf, dhbuf, dph, wb = refs[8 + 2 * hn:13 + 2 * hn]
        h_sems = refs[13 + 2 * hn:]
        i = pl.program_id(0)

        @pl.when(i == 0)
        def _():
            gw_ref[...] = jnp.zeros_like(gw_ref)
            _broadcast_taps(w_ref, wb)
            if hosted is not None:
                hosted.start(h_ins, h_outs, h_sems)

        val = val_ref[...]
        sg = _sigmoid(glu_ref[...])
        hbuf[...] = val * sg
        dbuf[0:tt, :] = dc_ref[...]
        dbuf[tt:, :] = jnp.where(i < nt - 1, dnext_ref[...], 0.0)
        _shifted_copies(dbuf, dph)
        for cb in range(C // LANES):
            cols = slice(cb * LANES, (cb + 1) * LANES)
            gacc = [jnp.zeros((8, LANES), F32) for _ in range(CONV_K)]
            group = 2
            for rc0 in range(0, tt // 8, group):
                hcur = [hbuf[(rc0 + r) * 8:(rc0 + r + 1) * 8, cols] for r in range(group)]
                accs = [jnp.zeros((8, LANES), F32) for _ in range(group)]
                for j in range(CONV_K):
                    wj = wb[j, :, cols]
                    for r in range(group):
                        dwin = _window(dbuf, dph, (rc0 + r) * 8 + (CONV_K - 1) - j, cols)
                        accs[r] = accs[r] + dwin * wj
                        gacc[j] = gacc[j] + dwin * hcur[r]
                for r in range(group):
                    dhbuf[(rc0 + r) * 8:(rc0 + r + 1) * 8, cols] = accs[r]
            for j in range(CONV_K):
                gw_ref[j:j + 1, cols] += jnp.sum(gacc[j], axis=0, keepdims=True)
        d_h = dhbuf[...]
        out_ref[:, 0:C] = (d_h * sg).astype(BF16)
        out_ref[:, C:2 * C] = (d_h * val * sg * (1.0 - sg)).astype(BF16)
        out_ref[:, 2 * C:3 * C] = dg_ref[...]

        if hosted is not None:
            @pl.when(i == nt - 1)
            def _():
                hosted.finish(h_ins, h_outs, h_sems)

    tile = lambda col: pl.BlockSpec((tt, C), lambda i: (i, col))
    in_specs = [tile(0),
                pl.BlockSpec((CONV_HALO, C), lambda i: (jnp.minimum((i + 1) * hb, S // CONV_HALO - 1), 0)),
                tile(0), tile(1), tile(0),
                pl.BlockSpec((CONV_HALO, C), lambda i: (0, 0))]
    args = [d_conv, d_conv, gates, gates, d_c_gate, conv_w]
    out_specs = [pl.BlockSpec((tt, 3 * C), lambda i: (i, 0)), pl.BlockSpec((CONV_HALO, C), lambda i: (0, 0))]
    out_shape = [jax.ShapeDtypeStruct((S, 3 * C), BF16), jax.ShapeDtypeStruct((CONV_HALO, C), F32)]
    scratch = [pltpu.VMEM((tt, C), F32), pltpu.VMEM((tt + CONV_HALO, C), F32), pltpu.VMEM((tt, C), F32),
               pltpu.VMEM((7, tt + CONV_HALO - 8, C), F32), pltpu.VMEM((CONV_K, 8, C), F32)]
    if hosted is not None:
        in_specs += [ANY_SPEC] * hn
        args += hosted.arrays
        out_specs += [ANY_SPEC] * hn
        out_shape += hosted.out_shapes()
        scratch += hosted.sem_shapes()
    res = pl.pallas_call(
        body, name="conv_bwd", grid=(nt,),
        in_specs=in_specs, out_specs=tuple(out_specs), out_shape=tuple(out_shape), scratch_shapes=scratch,
        compiler_params=_params(("arbitrary",)),
    )(*args)
    return res[0], res[1], list(res[2:])


def _attn_bwd(q, kv, d_o, lse, delta, dil, prev, final, name, hosted=None):
    S = q.shape[0]
    rows = _Rows(dil, S)
    nb = rows.nb
    steps = dil * nb
    out_dt = BF16 if final else F32
    have_prev = prev is not None
    hn = hosted.n if hosted is not None else 0

    def body(*refs):
        refs = list(refs)
        q_ref, do_ref, lse_ref, dl_ref, kvc_ref, kvp_ref = refs[:6]
        del refs[:6]
        if have_prev:
            pdq_ref, pdkv_ref = refs[:2]
            del refs[:2]
        h_ins = refs[:hn]
        dq_ref, dkv_ref = refs[hn:hn + 2]
        h_outs = refs[hn + 2:2 * hn + 2]
        carry, tbl = refs[2 * hn + 2:2 * hn + 4]
        h_sems = refs[2 * hn + 4:]
        t = pl.program_id(0)
        n = t % nb

        @pl.when(t == 0)
        def _():
            if hosted is not None:
                hosted.start(h_ins, h_outs, h_sems)
            _fill_bias_table(tbl, rows, keys_first=True)
            carry[...] = jnp.zeros_like(carry)

        @pl.when(t < steps)
        def _():
            kv2 = jnp.concatenate([_ld(kvp_ref), _ld(kvc_ref)], axis=0)
            lse_t, dl_t = _ld(lse_ref).T, _ld(dl_ref).T
            lo_mask = lax.broadcasted_iota(jnp.int32, (2 * BLK, LANES), 1) < HEAD_DIM
            halves = [jnp.zeros((2 * BLK, LANES), F32) for _ in range(4)]
            for hk in range(N_KV_HEADS):
                k_lo, k_hi, v_lo, v_hi = _head_operands(kv2, hk, lo_mask)
                cols = [slice(b * LANES, (b + 1) * LANES) for b in (2 * hk, 2 * hk + 1)]
                q2 = jnp.concatenate([_ld(q_ref, cols[0]), _ld(q_ref, cols[1])], axis=0).astype(BF16)
                do2 = jnp.concatenate([_ld(do_ref, cols[0]), _ld(do_ref, cols[1])], axis=0).astype(BF16)
                dq2 = jnp.zeros((2 * BLK, LANES), F32)
                dks, dvs = [], []
                for which, (kk, vv) in enumerate(((k_lo, v_lo), (k_hi, v_hi))):
                    h0, h1 = 4 * hk + which, 4 * hk + 2 + which
                    s = _nt(kk, q2) + _bias2(tbl, n, h0, h1, axis=1)
                    lse2 = jnp.concatenate([lse_t[h0:h0 + 1, :], lse_t[h1:h1 + 1, :]], axis=1)
                    dl2 = jnp.concatenate([dl_t[h0:h0 + 1, :], dl_t[h1:h1 + 1, :]], axis=1)
                    p = jnp.exp(s - lse2)
                    ds = (p * (_nt(vv, do2) - dl2)).astype(BF16)
                    dq2 = dq2 + _tn(ds, kk)
                    dks.append(jnp.dot(ds, q2, preferred_element_type=F32))
                    dvs.append(jnp.dot(p.astype(BF16), do2, preferred_element_type=F32))
                dk_sum = jnp.where(lo_mask, dks[0], dks[1])
                dv_sum = jnp.where(lo_mask, dvs[0], dvs[1])
                for jp in range(2):
                    dq_blk = dq2[jp * BLK:(jp + 1) * BLK]
                    if have_prev:
                        dq_blk = dq_blk + _ld(pdq_ref, cols[jp])
                    if final:
                        dq_blk = dq_blk * (HEAD_DIM ** -0.5)
                    _st(dq_ref, dq_blk.astype(out_dt), cols[jp])
                half, pos = hk // 2, hk % 2
                here = lo_mask if pos == 0 else jnp.logical_not(lo_mask)
                dk_tot = dk_sum + pltpu.roll(dk_sum, HEAD_DIM, axis=1)
                dv_tot = dv_sum + pltpu.roll(dv_sum, HEAD_DIM, axis=1)
                halves[half] = halves[half] + jnp.where(here, dk_tot, 0.0)
                halves[2 + half] = halves[2 + half] + jnp.where(here, dv_tot, 0.0)
            for b in range(4):
                cols = slice(b * LANES, (b + 1) * LANES)
                done = carry[:, cols] + halves[b][0:BLK, :]
                if have_prev:
                    done = done + _ld(pdkv_ref, cols)
                _st(dkv_ref, done.astype(out_dt), cols)
                carry[:, cols] = halves[b][BLK:, :]

        @pl.when(t == steps)
        def _():
            done = carry[...]
            if have_prev:
                done = done + _ld(pdkv_ref)
            _st(dkv_ref, done.astype(out_dt))
            if hosted is not None:
                hosted.finish(h_ins, h_outs, h_sems)

    def spec(width, lag):
        def index(t):
            u = jnp.clip(t - lag, 0, steps - 1)
            return rows.index(u // nb, u % nb)
        return pl.BlockSpec(rows.block + (width,), index)

    def key_prev(t):
        u = jnp.minimum(t, steps - 1)
        return rows.index(u // nb, jnp.maximum(u % nb - 1, 0))

    in_specs = [spec(ATT_W, 0), spec(ATT_W, 0), spec(LANES, 0), spec(LANES, 0), spec(2 * KV_W, 0),
                pl.BlockSpec(rows.block + (2 * KV_W,), key_prev)]
    args = [rows.of(q), rows.of(d_o), rows.of(lse), rows.of(delta), rows.of(kv), rows.of(kv)]
    if have_prev:
        in_specs += [spec(ATT_W, 0), spec(2 * KV_W, 1)]
        args += [rows.of(prev[0]), rows.of(prev[1])]
    out_specs = [spec(ATT_W, 0), spec(2 * KV_W, 1)]
    out_shape = [jax.ShapeDtypeStruct(rows.view + (ATT_W,), out_dt),
                 jax.ShapeDtypeStruct(rows.view + (2 * KV_W,), out_dt)]
    scratch = [pltpu.VMEM((BLK, 2 * KV_W), F32), pltpu.VMEM((2 * N_Q_HEADS, 2 * BLK, BLK), F32)]
    if hosted is not None:
        in_specs += [ANY_SPEC] * hn
        args += hosted.arrays
        out_specs += [ANY_SPEC] * hn
        out_shape += hosted.out_shapes()
        scratch += hosted.sem_shapes()
    res = pl.pallas_call(
        body, name=name, grid=(steps + 1,),
        in_specs=in_specs, out_specs=tuple(out_specs), out_shape=tuple(out_shape), scratch_shapes=scratch,
        compiler_params=_params(("arbitrary",)),
    )(*args)
    return (res[0].reshape(S, ATT_W), res[1].reshape(S, 2 * KV_W)), list(res[2:])


def _dh(segments, w_in, x, dx2, g, hosted=None, tm=1024, tk=512):
    S, D = x.shape
    ns = len(segments)
    counts = [a.shape[1] // tk for a, _ in segments]
    starts = [sum(counts[:s]) for s in range(ns)]
    nk = sum(counts)
    hn = hosted.n if hosted is not None else 0

    def body(*refs):
        seg_refs = refs[:ns]
        w_ref, x_ref, dx2_ref, g_ref = refs[ns:ns + 4]
        h_ins = refs[ns + 4:ns + 4 + hn]
        gx_ref, gng_ref = refs[ns + 4 + hn:ns + 6 + hn]
        h_outs = refs[ns + 6 + hn:ns + 6 + 2 * hn]
        acc = refs[ns + 6 + 2 * hn]
        h_sems = refs[ns + 7 + 2 * hn:]
        k, i = pl.program_id(0), pl.program_id(1)

        @pl.when((i == 0) & (k == 0))
        def _():
            gng_ref[...] = jnp.zeros_like(gng_ref)
            if hosted is not None:
                hosted.start(h_ins, h_outs, h_sems)

        @pl.when(k == 0)
        def _():
            acc[i] = jnp.zeros(acc.shape[1:], F32)

        for s in range(ns):
            @pl.when((k >= starts[s]) & (k < starts[s] + counts[s]))
            def _(s=s):
                t = seg_refs[s][...]
                if segments[s][1]:
                    t = _perm_rows(t, True)
                acc[i] += jnp.dot(t, w_ref[...], preferred_element_type=F32)

        @pl.when(k == nk - 1)
        def _():
            dh = acc[i]
            xf = x_ref[...]
            r = lax.rsqrt(jnp.mean(xf * xf, axis=-1, keepdims=True) + NORM_EPS)
            nrm = xf * r
            gng_ref[...] += jnp.sum(dh * nrm, axis=0, keepdims=True)
            dn = dh * g_ref[...]
            gx_ref[...] = dx2_ref[...] + r * (dn - nrm * jnp.mean(dn * nrm, axis=-1, keepdims=True))

        if hosted is not None:
            @pl.when((i == S // tm - 1) & (k == nk - 1))
            def _():
                hosted.finish(h_ins, h_outs, h_sems)

    ni = S // tm
    row = pl.BlockSpec((tm, D), lambda k, i: (jnp.where(k == nk - 1, i, 0), 0))
    vec = pl.BlockSpec((1, D), lambda k, i: (0, 0))

    def seg_index(s):
        def index(k, i):
            j = k - starts[s]
            return jnp.where(j < 0, 0, jnp.where(j >= counts[s], ni - 1, i)), jnp.clip(j, 0, counts[s] - 1)
        return index

    in_specs = [pl.BlockSpec((tm, tk), seg_index(s)) for s in range(ns)]
    in_specs += [pl.BlockSpec((tk, D), lambda k, i: (k, 0)), row, row, vec]
    args = [a for a, _ in segments] + [w_in, x, dx2, g]
    out_specs = [row, vec]
    out_shape = [jax.ShapeDtypeStruct((S, D), F32), jax.ShapeDtypeStruct((1, D), F32)]
    scratch = [pltpu.VMEM((ni, tm, D), F32)]
    if hosted is not None:
        in_specs += [ANY_SPEC] * hn
        args += hosted.arrays
        out_specs += [ANY_SPEC] * hn
        out_shape += hosted.out_shapes()
        scratch += hosted.sem_shapes()
    res = pl.pallas_call(
        body, name="dh", grid=(nk, S // tm),
        in_specs=in_specs, out_specs=tuple(out_specs), out_shape=tuple(out_shape), scratch_shapes=scratch,
        compiler_params=_params(("arbitrary", "arbitrary"), BIG_VMEM_LIMIT),
    )(*args)
    return res[0], res[1], list(res[2:])


def _tn_matmul(pairs, layout, name, tm=512):
    arrays = []

    def slot(a):
        for i, b in enumerate(arrays):
            if b is a:
                return i
        arrays.append(a)
        return len(arrays) - 1

    slots = [(slot(u), slot(v)) for u, v in pairs]
    ready = [src for blocks in layout for src, _, _ in blocks if not isinstance(src, int)]
    M, K = pairs[0][1].shape
    n_in, n_ready, n_out = len(arrays), len(ready), len(layout)
    last = M // tm - 1

    def body(*refs):
        in_refs, ready_refs = refs[:n_in], refs[n_in:n_in + n_ready]
        o_refs, accs = refs[n_in + n_ready:n_in + n_ready + n_out], refs[n_in + n_ready + n_out:]

        @pl.when(pl.program_id(0) == 0)
        def _():
            for acc in accs:
                acc[...] = jnp.zeros_like(acc)

        for (iu, iv), acc in zip(slots, accs):
            vt = in_refs[iv][...]
            for c in range(0, acc.shape[0], 512):
                acc[c:c + 512, :] += _tn(in_refs[iu][:, c:c + 512], vt)

        @pl.when(pl.program_id(0) == last)
        def _():
            taken = 0
            for o_ref, blocks in zip(o_refs, layout):
                row = 0
                for src, r0, n in blocks:
                    if isinstance(src, int):
                        o_ref[row:row + n, :] = accs[src][r0:r0 + n, :].astype(o_ref.dtype)
                    else:
                        o_ref[row:row + n, :] = ready_refs[taken][r0:r0 + n, :]
                        taken += 1
                    row += n

    out_rows = [sum(n for _, _, n in blocks) for blocks in layout]
    return pl.pallas_call(
        body, name=name, grid=(M // tm,),
        in_specs=[pl.BlockSpec((tm, a.shape[1]), lambda m: (m, 0)) for a in arrays]
        + [pl.BlockSpec(r.shape, lambda m: (0, 0)) for r in ready],
        out_specs=tuple(pl.BlockSpec((rows, K), lambda m: (0, 0)) for rows in out_rows),
        out_shape=tuple(jax.ShapeDtypeStruct((rows, K), BF16) for rows in out_rows),
        scratch_shapes=[pltpu.VMEM((u.shape[1], K), F32) for u, _ in pairs],
        compiler_params=_params(("arbitrary",)),
    )(*arrays, *ready)


def _adamw(parts, w, m, v, name, tr=None, split=None, by_chip=False):
    R, C = w.shape
    tr = R if tr is None else tr
    parts = [parts] if split is None else list(parts)
    npar = len(parts)

    def total(p_ref):
        if by_chip:
            c = lax.axis_index("c")
            g = p_ref[c].astype(F32)
            for chip in range(1, N_DEV // 2):
                g = g + p_ref[2 * chip + c].astype(F32)
            return g
        g = p_ref[0].astype(F32)
        for dev in range(1, N_DEV):
            g = g + p_ref[dev].astype(F32)
        return g

    def body(*refs):
        w_ref, m_ref, v_ref, g_out, d_out, m_out, v_out = refs[npar:]
        if split is None:
            g = total(refs[0])
        else:
            g = jnp.where(_mesh_pos()[3] < split, total(refs[0]), total(refs[1]))
        mn = ADAM_B1 * m_ref[...] + (1.0 - ADAM_B1) * g
        vn = ADAM_B2 * v_ref[...] + (1.0 - ADAM_B2) * (g * g)
        m_hat = mn / (1.0 - ADAM_B1 ** ADAM_STEP)
        v_hat = vn / (1.0 - ADAM_B2 ** ADAM_STEP)
        g_out[...] = g
        d_out[...] = -ADAM_LR * (m_hat / (jnp.sqrt(v_hat) + ADAM_EPS) + ADAM_WD * w_ref[...])
        m_out[...] = mn
        v_out[...] = vn

    blk = pl.BlockSpec((tr, C), lambda i: (i, 0))
    shp = jax.ShapeDtypeStruct((R, C), F32)
    return pl.pallas_call(
        body, name=name, grid=(R // tr,),
        in_specs=[pl.BlockSpec((N_DEV, tr, C), lambda i: (0, i, 0))] * npar + [blk, blk, blk],
        out_specs=(blk, blk, blk, blk), out_shape=(shp, shp, shp, shp),
        compiler_params=_params(("parallel",)),
    )(*parts, w, m, v)


def _local_step(x, target, norm_g, w_in, conv_w, conv_b, ln_g, ln_b, w_out, gf, exchanges=None, first_weights=None,
                late_weights=None, first_rows=ATT_W + 2 * KV_W + ATT_W // 2):
    ex_out, ex_att, ex_conv = exchanges if exchanges is not None else (None, None, None)
    h_rm, h, *first = _norm_rows(x, norm_g, first_weights[0] if first_weights is not None else None)
    if first_weights is not None:
        w_in = first_weights[1](first)
    conv_cols = w_in.shape[0] - 2 * ATT_W - 2 * KV_W
    q, kv, a_gate, gates, *gathered = _inproj(
        h_rm, h, w_in,
        [(ATT_W, HEAD_DIM ** -0.5, True), (2 * KV_W, 1.0, True), (ATT_W, 1.0, True), (conv_cols, 1.0, False)],
        late_weights[0] if late_weights is not None else None)
    if late_weights is not None:
        conv_w, w_out = late_weights[1](gathered)

    alone = [_attn_fwd(q, kv, dil, "attn_fwd_d%d" % dil) for _, dil in PATTERNS[1:]]
    o, lse, y_att = _attn_fwd(q, kv, PATTERNS[0][1], "attn_fwd_d%d" % PATTERNS[0][1], alone, a_gate)
    conv_out, y_conv = _conv_fwd(gates, conv_w, conv_b, ln_g, ln_b)
    dx2, dxb, loss_cols, g_gf = _outproj_loss(x, y_att, y_conv, w_out, gf, target)

    d_o, d_a_gate, delta, dxb_rm = _dy_att(dxb, w_out, a_gate, o)
    g_w_out, = _tn_matmul([(y_att, dxb_rm), (y_conv, dxb)], [[(0, 0, ATT_W), (1, 0, y_conv.shape[1])]], "gw_out")
    acc, out_parts = None, []
    for idx, (_, dil) in enumerate(reversed(PATTERNS)):
        hosted = ex_out(g_w_out) if (idx == 0 and ex_out is not None) else None
        acc, outs = _attn_bwd(q, kv, d_o, lse, delta, dil, acc, idx == len(PATTERNS) - 1, "attn_bwd_d%d" % dil,
                              hosted)
        out_parts += outs
    dq, dkv = acc
    a_lo = first_rows - (ATT_W + 2 * KV_W)
    assert 0 < a_lo < ATT_W
    g_first, g_a_rest = _tn_matmul(
        [(dq, h_rm), (dkv, h_rm), (d_a_gate, h_rm)],
        [[(0, 0, ATT_W), (1, 0, 2 * KV_W), (2, 0, a_lo)], [(2, a_lo, ATT_W - a_lo)]], "gw_in_att")

    d_c_gate, d_conv, g_ln_g, g_ln_b, g_conv_b = _dy_conv(dxb, w_out, gates, conv_out, ln_g, ln_b)
    dgates, g_conv_w, att_parts = _conv_bwd(d_conv, gates, d_c_gate, conv_w,
                                            ex_att(g_first) if ex_att is not None else None)
    g_rest, = _tn_matmul([(dgates, h)], [[(g_a_rest, 0, ATT_W - a_lo), (0, 0, conv_cols)]], "gw_in_conv")
    grad_x, g_norm_g, conv_parts = _dh(
        [(dq, True), (dkv, True), (d_a_gate, True), (dgates, False)], w_in, x, dx2, norm_g,
        ex_conv(g_rest, g_conv_w) if ex_conv is not None else None)
    small = (g_norm_g, g_conv_b, g_ln_g, g_ln_b, g_gf, loss_cols)
    return grad_x, (g_first, g_rest), g_w_out, g_conv_w, small, (out_parts, att_parts, conv_parts)


def kernel(x, norm_g, w_in, conv_w, conv_b, conv_ln_g, conv_ln_b, w_out, final_norm_g, loss_target, m_norm_g, m_w_in, m_conv_w, m_conv_b, m_conv_ln_g, m_conv_ln_b, m_w_out, m_final_norm_g, v_norm_g, v_w_in, v_conv_w, v_conv_b, v_conv_ln_g, v_conv_ln_b, v_w_out, v_final_norm_g):
    S, D = x.shape[1], x.shape[2]
    win_sh, wout_sh, cw_sh = w_in[0].T, w_out[0], conv_w[0]
    cols_sh, rows_sh, ch_sh = win_sh.shape[0], wout_sh.shape[0], cw_sh.shape[1]

    def first_weights(gathered):
        return gathered[0].reshape(N_DEV * cols_sh, D)

    def late_weights(gathered):
        wout_all, cw_all = gathered
        conv_w_full = cw_all.transpose(1, 0, 2).reshape(CONV_K, N_DEV * ch_sh)
        return jnp.pad(conv_w_full, ((0, CONV_HALO - CONV_K), (0, 0))), wout_all.reshape(N_DEV * rows_sh, D)

    gf = final_norm_g.reshape(1, D)

    first = -(-(ATT_W + 2 * KV_W) // cols_sh)

    def ex_out(g_w_out):
        return _Exchange([g_w_out.reshape(N_DEV, rows_sh, D)], [(0, N_DEV)])

    same_core = (2, 4, 6)

    def ex_att(g_first):
        mine = _chip_sum(g_first.reshape(first, cols_sh, D), 0, "rs_att")
        return _Exchange([mine], [(0, first)], [same_core])

    def ex_conv(g_rest, g_conv_w):
        mine = _chip_sum(g_rest.reshape(N_DEV - first, cols_sh, D), first, "rs_conv")
        return _Exchange(
            [mine, g_conv_w[:CONV_K].reshape(CONV_K, N_DEV, ch_sh).transpose(1, 0, 2)],
            [(first, N_DEV), (0, N_DEV)], [same_core, None])

    grad_x, _, _, _, small, parts = _local_step(
        x[0], loss_target[0], norm_g, None, None, conv_b, conv_ln_g, conv_ln_b, None, gf,
        (ex_out, ex_att, ex_conv), (_Gather([win_sh.astype(BF16)]), first_weights),
        (_Gather([wout_sh.astype(BF16), cw_sh]), late_weights), first * cols_sh)
    (wout_parts,), (win_parts_lo,), (win_parts_hi, cw_parts) = parts

    small_pack = jnp.concatenate(list(small) + [jnp.zeros((2, D), F32)], axis=0)
    small_parts, = _exchange(_Exchange([small_pack], [None]), "gather_small")

    upd_win = _adamw((win_parts_lo, win_parts_hi), win_sh, m_w_in[0].T, v_w_in[0].T, "adamw_w_in",
                     tr=cols_sh // 2, split=first, by_chip=True)
    upd_wout = _adamw(wout_parts, wout_sh, m_w_out[0], v_w_out[0], "adamw_w_out", tr=128)
    upd_cw = _adamw(cw_parts, cw_sh, m_conv_w[0], v_conv_w[0], "adamw_conv_w")
    zeros3 = jnp.zeros((3, D), F32)
    stack = lambda a, b, c, d_, e: jnp.concatenate([a, b, c, d_, e.reshape(1, D), zeros3], axis=0)
    upd_small = _adamw(
        small_parts,
        stack(norm_g, conv_b, conv_ln_g, conv_ln_b, final_norm_g),
        stack(m_norm_g, m_conv_b, m_conv_ln_g, m_conv_ln_b, m_final_norm_g),
        stack(v_norm_g, v_conv_b, v_conv_ln_g, v_conv_ln_b, v_final_norm_g) + jnp.concatenate(
            [jnp.zeros((5, D), F32), jnp.ones((3, D), F32)], axis=0),
        "adamw_small")

    loss = 0.5 / D * jnp.sum(upd_small[0][5])

    def outputs(kind):
        sm = upd_small[kind]
        return [sm[0:1], upd_win[kind].T[None], upd_cw[kind][None], sm[1:2], sm[2:3], sm[3:4],
                upd_wout[kind][None], sm[4]]

    return (loss, grad_x[None], *outputs(0), *outputs(1), *outputs(2), *outputs(3))
```

```python
import jax
import jax.numpy as jnp
from jax import lax
from jax.experimental import pallas as pl
from jax.experimental.pallas import tpu as pltpu

F32 = jnp.float32
BF16 = jnp.bfloat16

HEAD_DIM = 64
N_KV_HEADS = 4
N_Q_HEADS = 16
ATT_W = 1024
KV_W = 256
CONV_K = 31
CONV_HALO = 32
PATTERNS = ((128, 1), (512, 4), (2048, 16))
BLK = 128
LANES = 128
NORM_EPS = 1e-6
LN_EPS = 1e-5
NEG = -1e30
N_DEV = 8
ADAM_LR, ADAM_B1, ADAM_B2, ADAM_EPS, ADAM_WD, ADAM_STEP = 0.001, 0.9, 0.999, 1e-08, 0.01, 10
VMEM_LIMIT = 48 * 1024 * 1024
BIG_VMEM_LIMIT = 58 * 1024 * 1024
SLOPES = tuple(2.0 ** (-8.0 * (h + 1) / N_Q_HEADS) for h in range(N_Q_HEADS))
MESH = pl.DeviceIdType.MESH


def _params(sem, vmem_limit=VMEM_LIMIT):
    return pltpu.CompilerParams(dimension_semantics=sem, vmem_limit_bytes=vmem_limit)


def _sigmoid(v):
    return 1.0 / (1.0 + jnp.exp(-v))


def _silu_and_grad(v):
    s = _sigmoid(v)
    return v * s, s * (1.0 + v * (1.0 - s))


ANY_SPEC = pl.BlockSpec(memory_space=pl.ANY)


def _mesh_pos():
    x, y, c = lax.axis_index("x"), lax.axis_index("y"), lax.axis_index("c")
    return x, y, c, 4 * x + 2 * y + c


def _flipped(k, x, y, c):
    px = 1 - x if k & 4 else x
    py = 1 - y if k & 2 else y
    pc = 1 - c if k & 1 else c
    return (px, py, pc), 4 * px + 2 * py + pc


class _Exchange:
    def __init__(self, arrays, dests, flips=None):
        self.arrays, self.dests, self.n = list(arrays), list(dests), len(arrays)
        self.flips = [tuple(range(1, N_DEV)) if f is None else tuple(f)
                      for f in (flips if flips is not None else [None] * self.n)]

    def out_shapes(self):
        return [jax.ShapeDtypeStruct((N_DEV,) + a.shape[-2:], a.dtype) for a in self.arrays]

    def sem_shapes(self):
        return [pltpu.SemaphoreType.DMA((self.n, N_DEV - 1)), pltpu.SemaphoreType.DMA((self.n, N_DEV - 1)),
                pltpu.SemaphoreType.DMA((self.n,))]

    def _when(self, a, dev, fn):
        if self.dests[a] is None:
            fn()
        else:
            lo, hi = self.dests[a]
            pl.when((dev >= lo) & (dev < hi))(fn)

    def _mine(self, ins, a, dev):
        return ins[a] if self.dests[a] is None else ins[a].at[dev - self.dests[a][0]]

    def _copy(self, ins, outs, sems, a, k, src_dev, slot, target):
        return pltpu.make_async_remote_copy(
            src_ref=self._mine(ins, a, src_dev), dst_ref=outs[a].at[slot],
            send_sem=sems[0].at[a, k - 1], recv_sem=sems[1].at[a, k - 1],
            device_id=target, device_id_type=MESH)

    def start(self, ins, outs, sems):
        x, y, c, me = _mesh_pos()
        for a in range(self.n):
            self._when(a, me, lambda a=a: pltpu.make_async_copy(
                self._mine(ins, a, me), outs[a].at[me], sems[2].at[a]).start())
            for k in self.flips[a]:
                target, peer = _flipped(k, x, y, c)
                self._when(a, peer, lambda a=a, k=k, target=target, peer=peer: self._copy(
                    ins, outs, sems, a, k, peer, me, target).start())

    def finish(self, ins, outs, sems):
        x, y, c, me = _mesh_pos()
        lo0 = [0 if d is None else d[0] for d in self.dests]
        for a in range(self.n):
            for k in self.flips[a]:
                target, peer = _flipped(k, x, y, c)
                self._when(a, me, lambda a=a, k=k, peer=peer: self._copy(
                    ins, outs, sems, a, k, lo0[a], peer, (x, y, c)).wait_recv())
            for k in self.flips[a]:
                target, peer = _flipped(k, x, y, c)
                self._when(a, peer, lambda a=a, k=k, target=target, peer=peer: self._copy(
                    ins, outs, sems, a, k, peer, me, target).wait_send())
            self._when(a, me, lambda a=a: pltpu.make_async_copy(
                self._mine(ins, a, me), outs[a].at[me], sems[2].at[a]).wait())


def _exchange(ex, name):
    na = ex.n

    def body(*refs):
        ins, outs, sems = refs[:na], refs[na:2 * na], refs[2 * na:]
        ex.start(ins, outs, sems)
        ex.finish(ins, outs, sems)

    return pl.pallas_call(
        body, name=name, out_shape=tuple(ex.out_shapes()),
        in_specs=[ANY_SPEC] * na, out_specs=tuple([ANY_SPEC] * na), scratch_shapes=ex.sem_shapes(),
    )(*ex.arrays)


def _chip_sum(pieces, lo, name):
    n, R, C = pieces.shape
    rows = 64
    assert R % rows == 0

    def body(p_ref, o_ref, mine_buf, other_buf, sum_buf, send_sems, recv_sems, local_sems, out_sems):
        x, y, c, me = _mesh_pos()

        def remote(i):
            return pltpu.make_async_remote_copy(
                src_ref=p_ref.at[i], dst_ref=other_buf.at[i], send_sem=send_sems.at[i], recv_sem=recv_sems.at[i],
                device_id=(x, y, 1 - c), device_id_type=MESH)

        def local(i):
            return pltpu.make_async_copy(p_ref.at[i], mine_buf.at[i], local_sems.at[i])

        def out(i):
            return pltpu.make_async_copy(sum_buf.at[i], o_ref.at[i], out_sems.at[i])

        summed_by = [(lo + i) % 2 for i in range(n)]
        for i in range(n):
            pl.when(c != summed_by[i])(remote(i).start)
            pl.when(c == summed_by[i])(local(i).start)
        for i in range(n):
            @pl.when(c == summed_by[i])
            def _(i=i):
                local(i).wait()
                remote(i).wait_recv()

                def chunk(j, carry):
                    r = pl.ds(pl.multiple_of(j * rows, rows), rows)
                    sum_buf[i, r, :] = (mine_buf[i, r, :].astype(F32) + other_buf[i, r, :].astype(F32)
                                        ).astype(sum_buf.dtype)
                    return carry

                lax.fori_loop(0, R // rows, chunk, 0)
                out(i).start()
        for i in range(n):
            pl.when(c == summed_by[i])(out(i).wait)
            pl.when(c != summed_by[i])(remote(i).wait_send)

    buf = pltpu.VMEM(pieces.shape, pieces.dtype)
    return pl.pallas_call(
        body, name=name, out_shape=jax.ShapeDtypeStruct(pieces.shape, pieces.dtype),
        in_specs=[ANY_SPEC], out_specs=ANY_SPEC,
        scratch_shapes=[buf, buf, buf] + [pltpu.SemaphoreType.DMA((n,))] * 4,
        compiler_params=pltpu.CompilerParams(vmem_limit_bytes=VMEM_LIMIT),
    )(pieces)


class _Gather:
    def __init__(self, arrays):
        self.arrays, self.n = list(arrays), len(arrays)

    def out_shapes(self):
        return [jax.ShapeDtypeStruct((N_DEV,) + a.shape, a.dtype) for a in self.arrays]

    def sem_shapes(self):
        return [pltpu.SemaphoreType.DMA((self.n, N_DEV - 1)), pltpu.SemaphoreType.DMA((self.n, N_DEV - 1)),
                pltpu.SemaphoreType.DMA((self.n,))]

    def _plan(self, ins, outs, sems):
        x, y, c, me = _mesh_pos()
        chips = [(1 - x, y), (x, 1 - y), (1 - x, 1 - y)]

        def copy(a, k, src, block, to):
            px, py, pc = block
            return pltpu.make_async_remote_copy(
                src_ref=src, dst_ref=outs[a].at[4 * px + 2 * py + pc], send_sem=sems[0].at[a, k],
                recv_sem=sems[1].at[a, k], device_id=to, device_id_type=MESH)

        def landed(a, block):
            px, py, pc = block
            return outs[a].at[4 * px + 2 * py + pc]

        local = [pltpu.make_async_copy(ins[a], outs[a].at[me], sems[2].at[a]) for a in range(self.n)]
        first = []
        for a in range(self.n):
            first.append(copy(a, 0, ins[a], (x, y, c), (x, y, 1 - c)))
            first += [copy(a, 1 + j, ins[a], (x, y, c), (*chip, c)) for j, chip in enumerate(chips[:2])]
        return (x, y, c), chips, copy, landed, local, first

    def start(self, ins, outs, sems):
        *_, local, first = self._plan(ins, outs, sems)
        for cp in local + first:
            cp.start()

    def finish(self, ins, outs, sems):
        (x, y, c), chips, copy, landed, local, first = self._plan(ins, outs, sems)
        south = c == 0
        came = (jnp.where(south, 1 - x, x), jnp.where(south, y, 1 - y), c)
        goes = (jnp.where(south, x, 1 - x), jnp.where(south, 1 - y, y), c)
        passed = []
        for a in range(self.n):
            for j, chip in enumerate(chips[:2]):
                copy(a, 1 + j, ins[a], (*chip, c), (x, y, c)).wait_recv()
            passed.append(copy(a, 3, landed(a, came), came, goes))
            passed += [copy(a, 4 + j, landed(a, (*chip, c)), (*chip, c), (x, y, 1 - c))
                       for j, chip in enumerate(chips[:2])]
        for cp in passed:
            cp.start()
        for a in range(self.n):
            diagonal = (*chips[2], c)
            copy(a, 3, ins[a], diagonal, (x, y, c)).wait_recv()
            cp = copy(a, 6, landed(a, diagonal), diagonal, (x, y, 1 - c))
            cp.start()
            passed.append(cp)
        for a in range(self.n):
            copy(a, 0, ins[a], (x, y, 1 - c), (x, y, c)).wait_recv()
            for j, chip in enumerate(chips):
                copy(a, 4 + j, ins[a], (*chip, 1 - c), (x, y, c)).wait_recv()
        for cp in first + passed:
            cp.wait_send()
        for cp in local:
            cp.wait()


CHUNK = 128
RESIDUES = 16
PER_RES = CHUNK // RESIDUES


def _perm_rows(tile, inverse):
    a = lax.broadcasted_iota(jnp.int32, (CHUNK, CHUNK), 0)
    b = lax.broadcasted_iota(jnp.int32, (CHUNK, CHUNK), 1)
    if inverse:
        a, b = b, a
    p = jnp.where(a == PER_RES * (b % RESIDUES) + b // RESIDUES, 1.0, 0.0).astype(BF16)
    parts = [jnp.dot(p, tile[c * CHUNK:(c + 1) * CHUNK], preferred_element_type=F32)
             for c in range(tile.shape[0] // CHUNK)]
    return jnp.concatenate(parts, axis=0).astype(BF16)


class _Rows:
    def __init__(self, dil, S):
        nc = S // CHUNK
        self.dil = dil
        if dil == 1:
            self.view, self.block, self.nb = (nc, CHUNK), (None, CHUNK), nc
            self.index = lambda r, b: (b, 0, 0)
        elif dil == 4:
            self.view, self.block, self.nb = (nc, 4, 4, PER_RES), (4, 4, None, PER_RES), nc // 4
            self.index = lambda r, b: (b, 0, r, 0, 0)
        elif dil == RESIDUES:
            self.view, self.block, self.nb = (nc, RESIDUES, PER_RES), (RESIDUES, None, PER_RES), nc // RESIDUES
            self.index = lambda r, b: (b, r, 0, 0)
        else:
            raise NotImplementedError(dil)

    def of(self, a):
        return a.reshape(self.view + (a.shape[-1],))

    def spec(self, width, which_block):
        return pl.BlockSpec(self.block + (width,), lambda r, n: self.index(r, which_block(n)))

    def pos(self, row):
        if self.dil == 1:
            return (row % PER_RES) * RESIDUES + row // PER_RES
        if self.dil == 4:
            return (row // 32) * 32 + (row % PER_RES) * 4 + (row % 32) // PER_RES
        return row


def _ld(ref, cols=slice(None)):
    v = ref[(slice(None),) * (len(ref.shape) - 1) + (cols,)]
    return v.reshape(BLK, v.shape[-1])


def _st(ref, val, cols=slice(None)):
    ref[(slice(None),) * (len(ref.shape) - 1) + (cols,)] = val.reshape(ref.shape[:-1] + (val.shape[-1],))


def _norm_rows(x, g, hosted=None, tm=512):
    S, D = x.shape
    hn = hosted.n if hosted is not None else 0

    def body(x_ref, g_ref, *rest):
        h_ins = rest[:hn]
        hrm_out, h_out = rest[hn:hn + 2]
        h_outs = rest[hn + 2:2 * hn + 2]
        h_sems = rest[2 * hn + 2:]
        i = pl.program_id(0)
        if hosted is not None:
            pl.when(i == 0)(lambda: hosted.start(h_ins, h_outs, h_sems))
        xf = x_ref[...]
        r = lax.rsqrt(jnp.mean(xf * xf, axis=-1, keepdims=True) + NORM_EPS)
        h = (xf * r * g_ref[...]).astype(BF16)
        h_out[...] = h
        hrm_out[...] = _perm_rows(h, False)
        if hosted is not None:
            pl.when(i == S // tm - 1)(lambda: hosted.finish(h_ins, h_outs, h_sems))

    row = pl.BlockSpec((tm, D), lambda i: (i, 0))
    in_specs, args = [row, pl.BlockSpec((1, D), lambda i: (0, 0))], [x, g]
    out_specs, out_shape, scratch = [row, row], [jax.ShapeDtypeStruct((S, D), BF16)] * 2, []
    if hosted is not None:
        in_specs += [ANY_SPEC] * hn
        args += hosted.arrays
        out_specs += [ANY_SPEC] * hn
        out_shape += hosted.out_shapes()
        scratch += hosted.sem_shapes()
    return pl.pallas_call(
        body, name="norm_rows", grid=(S // tm,),
        in_specs=in_specs, out_specs=tuple(out_specs), out_shape=tuple(out_shape), scratch_shapes=scratch,
        compiler_params=_params(("arbitrary",)),
    )(*args)


def _inproj(h_rm, h, w_t, segments, hosted=None, tm=1024, tn=512):
    S, D = h.shape
    ns = len(segments)
    ni = S // tm
    counts = [seg[0] // tn for seg in segments]
    starts = [sum(counts[:s]) for s in range(ns)]

    hn = hosted.n if hosted is not None else 0
    last_p = sum(counts)

    def body(hrm_ref, h_ref, w_ref, *rest):
        h_ins, rest = rest[:hn], rest[hn:]
        outs = rest[:ns]
        h_outs = rest[ns:ns + hn]
        hrm_scr, h_scr = rest[ns + hn:ns + 2 + hn]
        h_sems = rest[ns + 2 + hn:]
        p, i = pl.program_id(0), pl.program_id(1)

        if hosted is not None:
            @pl.when((p == 0) & (i == 0))
            def _():
                hosted.start(h_ins, h_outs, h_sems)

            @pl.when((p == last_p - 1) & (i == ni - 1))
            def _():
                hosted.finish(h_ins, h_outs, h_sems)

        @pl.when(p == 0)
        def _():
            h_scr[i] = h_ref[...]
            hrm_scr[i] = hrm_ref[...]

        for s, (_, scale, rm) in enumerate(segments):
            @pl.when((p >= starts[s]) & (p < starts[s] + counts[s]))
            def _(s=s, scale=scale, rm=rm):
                acc = _nt((hrm_scr if rm else h_scr)[i], w_ref[...])
                outs[s][...] = acc * scale if scale != 1.0 else acc

    def out_index(s):
        def index(p, i):
            j = p - starts[s]
            row = jnp.where(j < 0, 0, jnp.where(j >= counts[s], ni - 1, i))
            return row, jnp.clip(j, 0, counts[s] - 1)
        return index

    first_pass = pl.BlockSpec((tm, D), lambda p, i: (jnp.where(p == 0, i, ni - 1), 0))
    out_specs = [pl.BlockSpec((tm, tn), out_index(s)) for s in range(ns)]
    out_shape = [jax.ShapeDtypeStruct((S, seg[0]), F32) for seg in segments]
    in_specs = [first_pass, first_pass, pl.BlockSpec((tn, D), lambda p, i: (p, 0))]
    args = [h_rm, h, w_t]
    scratch = [pltpu.VMEM((ni, tm, D), BF16), pltpu.VMEM((ni, tm, D), BF16)]
    if hosted is not None:
        in_specs += [ANY_SPEC] * hn
        args += hosted.arrays
        out_specs += [ANY_SPEC] * hn
        out_shape += hosted.out_shapes()
        scratch += hosted.sem_shapes()
    return pl.pallas_call(
        body, name="inproj", grid=(last_p, ni),
        in_specs=in_specs, out_specs=tuple(out_specs), out_shape=tuple(out_shape), scratch_shapes=scratch,
        compiler_params=_params(("arbitrary", "arbitrary"), BIG_VMEM_LIMIT),
    )(*args)


def _fill_bias_table(tbl, rows, keys_first=False):
    shape = (2 * BLK, BLK) if keys_first else (BLK, 2 * BLK)
    qi = lax.broadcasted_iota(jnp.int32, shape, 1 if keys_first else 0)
    kj = lax.broadcasted_iota(jnp.int32, shape, 0 if keys_first else 1)
    dist = rows.pos(qi) - rows.pos(kj % BLK) + jnp.where(kj < BLK, BLK, 0)
    inside = (dist >= 0) & (dist <= BLK)
    negd = (dist * (-rows.dil)).astype(F32)
    for f, valid in enumerate((inside & (kj >= BLK), inside)):
        for h in range(N_Q_HEADS):
            tbl[f * N_Q_HEADS + h] = jnp.where(valid, SLOPES[h] * negd, NEG)


def _bias2(tbl, n, h0, h1, axis=0):
    base = jnp.where(n == 0, 0, N_Q_HEADS)
    return jnp.concatenate([tbl[base + h0], tbl[base + h1]], axis=axis)


def _head_operands(kv2, hk, lo_mask):
    half, pos = hk // 2, hk % 2
    out = []
    for base in (0, KV_W):
        t = kv2[:, base + half * LANES: base + (half + 1) * LANES]
        sw = pltpu.roll(t, HEAD_DIM, axis=1)
        at_lo, at_hi = (t, sw) if pos == 0 else (sw, t)
        out.append(jnp.where(lo_mask, at_lo, 0.0).astype(BF16))
        out.append(jnp.where(lo_mask, 0.0, at_hi).astype(BF16))
    return out


def _nt(a, b):
    return lax.dot_general(a, b, (((1,), (1,)), ((), ())), preferred_element_type=F32)


def _tn(a, b):
    return lax.dot_general(a, b, (((0,), (0,)), ((), ())), preferred_element_type=F32)


def _attn_fwd(q, kv, dil, name, prev=(), gate=None):
    S = q.shape[0]
    rows = _Rows(dil, S)
    nb = rows.nb
    have_prev, last = len(prev) > 0, gate is not None

    def body(*refs):
        refs = list(refs)
        q_ref, kvc_ref, kvp_ref = refs[:3]
        del refs[:3]
        po_refs, pl_refs = refs[0:2 * len(prev):2], refs[1:2 * len(prev):2]
        del refs[:2 * len(prev)]
        if last:
            gate_ref = refs.pop(0)
        o_ref, lse_ref = refs[:2]
        y_ref = refs[2] if last else None
        tbl = refs[-1]
        n = pl.program_id(1)

        @pl.when((pl.program_id(0) == 0) & (n == 0))
        def _():
            _fill_bias_table(tbl, rows)

        kv2 = jnp.concatenate([_ld(kvp_ref), _ld(kvc_ref)], axis=0)
        lo_mask = lax.broadcasted_iota(jnp.int32, (2 * BLK, LANES), 1) < HEAD_DIM
        lane = lax.broadcasted_iota(jnp.int32, (BLK, LANES), 1)
        stats = jnp.zeros((BLK, LANES), F32)
        for hk in range(N_KV_HEADS):
            k_lo, k_hi, v_lo, v_hi = _head_operands(kv2, hk, lo_mask)
            cols = [slice(b * LANES, (b + 1) * LANES) for b in (2 * hk, 2 * hk + 1)]
            q2 = jnp.concatenate([_ld(q_ref, cols[0]), _ld(q_ref, cols[1])], axis=0).astype(BF16)
            o2 = jnp.zeros((2 * BLK, LANES), F32)
            for which, (kk, vv) in enumerate(((k_lo, v_lo), (k_hi, v_hi))):
                h0, h1 = 4 * hk + which, 4 * hk + 2 + which
                s = _nt(q2, kk) + _bias2(tbl, n, h0, h1)
                m = jnp.max(s, axis=1, keepdims=True)
                p = jnp.exp(s - m)
                l = jnp.sum(p, axis=1, keepdims=True)
                o2 = o2 + jnp.dot(p.astype(BF16), vv, preferred_element_type=F32) * (1.0 / l)
                lse = m + jnp.log(l)
                stats = jnp.where(lane == h0, lse[0:BLK], stats)
                stats = jnp.where(lane == h1, lse[BLK:], stats)
            _st(o_ref, o2[0:BLK], cols[0])
            _st(o_ref, o2[BLK:], cols[1])
        if have_prev:
            others = [_ld(r) for r in pl_refs]
            top = stats
            for b in others:
                top = jnp.maximum(top, b)
            e_new = jnp.exp(stats - top)
            e_old = [jnp.exp(b - top) for b in others]
            total = e_new
            for e in e_old:
                total = total + e
            stats = top + jnp.log(total)
            inv = 1.0 / total
            w_new, w_old = e_new * inv, [e * inv for e in e_old]
        if have_prev or last:
            lo = lane < HEAD_DIM
            for blk in range(ATT_W // LANES):
                cols = slice(blk * LANES, (blk + 1) * LANES)
                o_blk = _ld(o_ref, cols)
                if have_prev:
                    pick = lambda w: jnp.where(lo, w[:, 2 * blk:2 * blk + 1], w[:, 2 * blk + 1:2 * blk + 2])
                    o_blk = o_blk * pick(w_new)
                    for po_ref, w in zip(po_refs, w_old):
                        o_blk = o_blk + _ld(po_ref, cols) * pick(w)
                    _st(o_ref, o_blk, cols)
                if last:
                    a = _ld(gate_ref, cols)
                    _st(y_ref, (o_blk * (a * _sigmoid(a))).astype(BF16), cols)
        _st(lse_ref, stats)

    here = lambda n: n
    before_n = lambda n: jnp.maximum(n - 1, 0)
    in_specs = [rows.spec(ATT_W, here), rows.spec(2 * KV_W, here), rows.spec(2 * KV_W, before_n)]
    args = [rows.of(q), rows.of(kv), rows.of(kv)]
    for o_other, lse_other in prev:
        in_specs += [rows.spec(ATT_W, here), rows.spec(LANES, here)]
        args += [rows.of(o_other), rows.of(lse_other)]
    out_specs = [rows.spec(ATT_W, here), rows.spec(LANES, here)]
    out_shape = [jax.ShapeDtypeStruct(rows.view + (ATT_W,), F32), jax.ShapeDtypeStruct(rows.view + (LANES,), F32)]
    if last:
        in_specs.append(rows.spec(ATT_W, here))
        args.append(rows.of(gate))
        out_specs.append(rows.spec(ATT_W, here))
        out_shape.append(jax.ShapeDtypeStruct(rows.view + (ATT_W,), BF16))
    res = pl.pallas_call(
        body, name=name, grid=(dil, nb),
        in_specs=in_specs, out_specs=tuple(out_specs), out_shape=tuple(out_shape),
        scratch_shapes=[pltpu.VMEM((2 * N_Q_HEADS, BLK, 2 * BLK), F32)],
        compiler_params=_params(("arbitrary", "arbitrary")),
    )(*args)
    return tuple(r.reshape(S, r.shape[-1]) for r in res)


def _shifted_copies(buf, phases):
    n = phases.shape[1]
    for b in range(1, 8):
        phases[b - 1] = buf[b:b + n, :]


def _window(buf, phases, start, cols):
    b = start % 8
    if b == 0:
        return buf[start:start + 8, cols]
    return phases[b - 1, start - b:start - b + 8, cols]


def _broadcast_taps(w_ref, wb):
    for j in range(CONV_K):
        wb[j] = jnp.broadcast_to(w_ref[j:j + 1, :], wb.shape[1:])


def _conv_fwd(gates, conv_w, conv_b, ln_g, ln_b, tt=256):
    S = gates.shape[0]
    C = conv_w.shape[1]
    hb = tt // CONV_HALO

    def body(val_ref, glu_ref, hval_ref, hglu_ref, gate_ref, w_ref, b_ref, g_ref, beta_ref,
             conv_ref, y_ref, hbuf, hph):
        i = pl.program_id(0)
        halo = hval_ref[...] * _sigmoid(hglu_ref[...])
        hbuf[0:CONV_HALO, :] = jnp.where(i > 0, halo, 0.0)
        hbuf[CONV_HALO:, :] = val_ref[...] * _sigmoid(glu_ref[...])
        _shifted_copies(hbuf, hph)
        for cb in range(C // LANES):
            cols = slice(cb * LANES, (cb + 1) * LANES)
            wj = [jnp.broadcast_to(w_ref[j:j + 1, cols], (8, LANES)) for j in range(CONV_K)]
            for rc in range(tt // 8):
                acc = jnp.zeros((8, LANES), F32)
                for j in range(CONV_K):
                    start = rc * 8 + CONV_HALO - (CONV_K - 1) + j
                    acc = acc + _window(hbuf, hph, start, cols) * wj[j]
                conv_ref[rc * 8:(rc + 1) * 8, cols] = acc
        cv = conv_ref[...] + b_ref[...]
        conv_ref[...] = cv
        mu = jnp.mean(cv, axis=-1, keepdims=True)
        xc = cv - mu
        var = jnp.mean(xc * xc, axis=-1, keepdims=True)
        ln = xc * lax.rsqrt(var + LN_EPS) * g_ref[...] + beta_ref[...]
        gt = gate_ref[...]
        y_ref[...] = (ln * _sigmoid(ln) * (gt * _sigmoid(gt))).astype(BF16)

    vec = pl.BlockSpec((1, C), lambda i: (0, 0))
    return pl.pallas_call(
        body, name="conv_fwd", grid=(S // tt,),
        in_specs=[pl.BlockSpec((tt, C), lambda i: (i, 0)),
                  pl.BlockSpec((tt, C), lambda i: (i, 1)),
                  pl.BlockSpec((CONV_HALO, C), lambda i: (jnp.maximum(i * hb - 1, 0), 0)),
                  pl.BlockSpec((CONV_HALO, C), lambda i: (jnp.maximum(i * hb - 1, 0), 1)),
                  pl.BlockSpec((tt, C), lambda i: (i, 2)),
                  pl.BlockSpec((CONV_HALO, C), lambda i: (0, 0)), vec, vec, vec],
        out_specs=(pl.BlockSpec((tt, C), lambda i: (i, 0)), pl.BlockSpec((tt, C), lambda i: (i, 0))),
        out_shape=(jax.ShapeDtypeStruct((S, C), F32), jax.ShapeDtypeStruct((S, C), BF16)),
        scratch_shapes=[pltpu.VMEM((tt + CONV_HALO, C), F32), pltpu.VMEM((7, tt + CONV_HALO - 8, C), F32)],
        compiler_params=_params(("parallel",)),
    )(gates, gates, gates, gates, gates, conv_w, conv_b, ln_g, ln_b)


def _outproj_loss(x, y_att, y_conv, w_out, gf, target, tm=512):
    S, D = x.shape
    E = y_att.shape[1]

    def body(x_ref, ya_ref, yc_ref, w_ref, gf_ref, t_ref, dx_ref, dxb_ref, loss_ref, ggf_ref):
        @pl.when(pl.program_id(0) == 0)
        def _():
            loss_ref[...] = jnp.zeros_like(loss_ref)
            ggf_ref[...] = jnp.zeros_like(ggf_ref)

        x2 = (x_ref[...] + jnp.dot(_perm_rows(ya_ref[...], True), w_ref[0:E, :], preferred_element_type=F32)
              + jnp.dot(yc_ref[...], w_ref[E:, :], preferred_element_type=F32))
        r = lax.rsqrt(jnp.mean(x2 * x2, axis=-1, keepdims=True) + NORM_EPS)
        nrm = x2 * r
        gfv = gf_ref[...]
        err = nrm * gfv - t_ref[...]
        loss_ref[...] += jnp.sum(err * err, axis=0, keepdims=True)
        dout = err * (1.0 / D)
        ggf_ref[...] += jnp.sum(dout * nrm, axis=0, keepdims=True)
        dn = dout * gfv
        dx2 = r * (dn - nrm * jnp.mean(dn * nrm, axis=-1, keepdims=True))
        dx_ref[...] = dx2
        dxb_ref[...] = dx2.astype(BF16)

    row = lambda w: pl.BlockSpec((tm, w), lambda i: (i, 0))
    vec = pl.BlockSpec((1, D), lambda i: (0, 0))
    return pl.pallas_call(
        body, name="outproj_loss", grid=(S // tm,),
        in_specs=[row(D), row(E), row(E), pl.BlockSpec((2 * E, D), lambda i: (0, 0)), vec, row(D)],
        out_specs=(row(D), row(D), vec, vec),
        out_shape=(jax.ShapeDtypeStruct((S, D), F32), jax.ShapeDtypeStruct((S, D), BF16),
                   jax.ShapeDtypeStruct((1, D), F32), jax.ShapeDtypeStruct((1, D), F32)),
        compiler_params=_params(("arbitrary",)),
    )(x, y_att, y_conv, w_out, gf, target)


def _split3(v):
    hi = v.astype(BF16)
    r1 = v - hi.astype(F32)
    mid = r1.astype(BF16)
    lo = (r1 - mid.astype(F32)).astype(BF16)
    return hi, mid, lo


def _dy_att(dxb, w_out, gates, o, tm=512):
    S, D = dxb.shape
    E = ATT_W

    def body(dx_ref, w_ref, a_ref, o_ref, do_ref, da_ref, dl_ref, dxr_ref):
        dxr = _perm_rows(dx_ref[...], False)
        dxr_ref[...] = dxr
        dya = _nt(dxr, w_ref[...])
        a = a_ref[...]
        ov = o_ref[...]
        sl, dsl = _silu_and_grad(a)
        d_o = dya * sl
        do_ref[...] = d_o
        da_ref[...] = (dya * ov * dsl).astype(BF16)
        ci = lax.broadcasted_iota(jnp.int32, (E, LANES), 0) // HEAD_DIM
        hi = lax.broadcasted_iota(jnp.int32, (E, LANES), 1)
        sel = jnp.where(ci == hi, 1.0, 0.0).astype(BF16)
        acc = jnp.zeros((tm, LANES), F32)
        for part in _split3(d_o * ov):
            acc = acc + jnp.dot(part, sel, preferred_element_type=F32)
        dl_ref[...] = acc

    row = lambda w: pl.BlockSpec((tm, w), lambda i: (i, 0))
    return pl.pallas_call(
        body, name="dy_att", grid=(S // tm,),
        in_specs=[row(D), pl.BlockSpec((E, D), lambda i: (0, 0)), row(E), row(E)],
        out_specs=(row(E), row(E), row(LANES), row(D)),
        out_shape=(jax.ShapeDtypeStruct((S, E), F32), jax.ShapeDtypeStruct((S, E), BF16),
                   jax.ShapeDtypeStruct((S, LANES), F32), jax.ShapeDtypeStruct((S, D), BF16)),
        compiler_params=_params(("parallel",)),
    )(dxb, w_out, gates, o)


def _dy_conv(dxb, w_out, gates, conv_out, ln_g, ln_b, tm=512):
    S, D = dxb.shape
    C = conv_out.shape[1]

    def body(dx_ref, w_ref, gate_ref, cv_ref, g_ref, beta_ref, dgate_ref, dconv_ref, gg_ref, gb_ref, gcb_ref):
        @pl.when(pl.program_id(0) == 0)
        def _():
            gg_ref[...] = jnp.zeros_like(gg_ref)
            gb_ref[...] = jnp.zeros_like(gb_ref)
            gcb_ref[...] = jnp.zeros_like(gcb_ref)

        dyc = _nt(dx_ref[...], w_ref[...])
        cv = cv_ref[...]
        mu = jnp.mean(cv, axis=-1, keepdims=True)
        xc = cv - mu
        rstd = lax.rsqrt(jnp.mean(xc * xc, axis=-1, keepdims=True) + LN_EPS)
        nrm = xc * rstd
        gv = g_ref[...]
        ln = nrm * gv + beta_ref[...]
        u, du = _silu_and_grad(ln)
        gt = gate_ref[...]
        g2, dg2 = _silu_and_grad(gt)
        dgate_ref[...] = (dyc * u * dg2).astype(BF16)
        d_ln = dyc * g2 * du
        gb_ref[...] += jnp.sum(d_ln, axis=0, keepdims=True)
        gg_ref[...] += jnp.sum(d_ln * nrm, axis=0, keepdims=True)
        dn = d_ln * gv
        d_conv = rstd * (dn - jnp.mean(dn, axis=-1, keepdims=True)
                         - nrm * jnp.mean(dn * nrm, axis=-1, keepdims=True))
        dconv_ref[...] = d_conv
        gcb_ref[...] += jnp.sum(d_conv, axis=0, keepdims=True)

    row = lambda w: pl.BlockSpec((tm, w), lambda i: (i, 0))
    vec = pl.BlockSpec((1, C), lambda i: (0, 0))
    return pl.pallas_call(
        body, name="dy_conv", grid=(S // tm,),
        in_specs=[row(D), pl.BlockSpec((C, D), lambda i: (1, 0)),
                  pl.BlockSpec((tm, C), lambda i: (i, 2)), row(C), vec, vec],
        out_specs=(row(C), row(C), vec, vec, vec),
        out_shape=(jax.ShapeDtypeStruct((S, C), BF16), jax.ShapeDtypeStruct((S, C), F32),
                   jax.ShapeDtypeStruct((1, C), F32), jax.ShapeDtypeStruct((1, C), F32),
                   jax.ShapeDtypeStruct((1, C), F32)),
        compiler_params=_params(("arbitrary",)),
    )(dxb, w_out, gates, conv_out, ln_g, ln_b)


def _conv_bwd(d_conv, gates, d_c_gate, conv_w, hosted=None, tt=256):
    S, C = d_conv.shape
    hb = tt // CONV_HALO
    nt = S // tt
    hn = hosted.n if hosted is not None else 0

    def body(*refs):
        dc_ref, dnext_ref, val_ref, glu_ref, dg_ref, w_ref = refs[:6]
        h_ins = refs[6:6 + hn]
        out_ref, gw_ref = refs[6 + hn:8 + hn]
        h_outs = refs[8 + hn:8 + 2 * hn]
        hbuf, dbuf, dhbuf, dph, wb = refs[8 + 2 * hn:13 + 2 * hn]
        h_sems = refs[13 + 2 * hn:]
        i = pl.program_id(0)

        @pl.when(i == 0)
        def _():
            gw_ref[...] = jnp.zeros_like(gw_ref)
            _broadcast_taps(w_ref, wb)
            if hosted is not None:
                hosted.start(h_ins, h_outs, h_sems)

        val = val_ref[...]
        sg = _sigmoid(glu_ref[...])
        hbuf[...] = val * sg
        dbuf[0:tt, :] = dc_ref[...]
        dbuf[tt:, :] = jnp.where(i < nt - 1, dnext_ref[...], 0.0)
        _shifted_copies(dbuf, dph)
        for cb in range(C // LANES):
            cols = slice(cb * LANES, (cb + 1) * LANES)
            gacc = [jnp.zeros((8, LANES), F32) for _ in range(CONV_K)]
            group = 2
            for rc0 in range(0, tt // 8, group):
                hcur = [hbuf[(rc0 + r) * 8:(rc0 + r + 1) * 8, cols] for r in range(group)]
                accs = [jnp.zeros((8, LANES), F32) for _ in range(group)]
                for j in range(CONV_K):
                    wj = wb[j, :, cols]
                    for r in range(group):
                        dwin = _window(dbuf, dph, (rc0 + r) * 8 + (CONV_K - 1) - j, cols)
                        accs[r] = accs[r] + dwin * wj
                        gacc[j] = gacc[j] + dwin * hcur[r]
                for r in range(group):
                    dhbuf[(rc0 + r) * 8:(rc0 + r + 1) * 8, cols] = accs[r]
            for j in range(CONV_K):
                gw_ref[j:j + 1, cols] += jnp.sum(gacc[j], axis=0, keepdims=True)
        d_h = dhbuf[...]
        out_ref[:, 0:C] = (d_h * sg).astype(BF16)
        out_ref[:, C:2 * C] = (d_h * val * sg * (1.0 - sg)).astype(BF16)
        out_ref[:, 2 * C:3 * C] = dg_ref[...]

        if hosted is not None:
            @pl.when(i == nt - 1)
            def _():
                hosted.finish(h_ins, h_outs, h_sems)

    tile = lambda col: pl.BlockSpec((tt, C), lambda i: (i, col))
    in_specs = [tile(0),
                pl.BlockSpec((CONV_HALO, C), lambda i: (jnp.minimum((i + 1) * hb, S // CONV_HALO - 1), 0)),
                tile(0), tile(1), tile(0),
                pl.BlockSpec((CONV_HALO, C), lambda i: (0, 0))]
    args = [d_conv, d_conv, gates, gates, d_c_gate, conv_w]
    out_specs = [pl.BlockSpec((tt, 3 * C), lambda i: (i, 0)), pl.BlockSpec((CONV_HALO, C), lambda i: (0, 0))]
    out_shape = [jax.ShapeDtypeStruct((S, 3 * C), BF16), jax.ShapeDtypeStruct((CONV_HALO, C), F32)]
    scratch = [pltpu.VMEM((tt, C), F32), pltpu.VMEM((tt + CONV_HALO, C), F32), pltpu.VMEM((tt, C), F32),
               pltpu.VMEM((7, tt + CONV_HALO - 8, C), F32), pltpu.VMEM((CONV_K, 8, C), F32)]
    if hosted is not None:
        in_specs += [ANY_SPEC] * hn
        args += hosted.arrays
        out_specs += [ANY_SPEC] * hn
        out_shape += hosted.out_shapes()
        scratch += hosted.sem_shapes()
    res = pl.pallas_call(
        body, name="conv_bwd", grid=(nt,),
        in_specs=in_specs, out_specs=tuple(out_specs), out_shape=tuple(out_shape), scratch_shapes=scratch,
        compiler_params=_params(("arbitrary",)),
    )(*args)
    return res[0], res[1], list(res[2:])


def _attn_bwd(q, kv, d_o, lse, delta, dil, prev, final, name, hosted=None):
    S = q.shape[0]
    rows = _Rows(dil, S)
    nb = rows.nb
    steps = dil * nb
    out_dt = BF16 if final else F32
    have_prev = prev is not None
    hn = hosted.n if hosted is not None else 0

    def body(*refs):
        refs = list(refs)
        q_ref, do_ref, lse_ref, dl_ref, kvc_ref, kvp_ref = refs[:6]
        del refs[:6]
        if have_prev:
            pdq_ref, pdkv_ref = refs[:2]
            del refs[:2]
        h_ins = refs[:hn]
        dq_ref, dkv_ref = refs[hn:hn + 2]
        h_outs = refs[hn + 2:2 * hn + 2]
        carry, tbl = refs[2 * hn + 2:2 * hn + 4]
        h_sems = refs[2 * hn + 4:]
        t = pl.program_id(0)
        n = t % nb

        @pl.when(t == 0)
        def _():
            if hosted is not None:
                hosted.start(h_ins, h_outs, h_sems)
            _fill_bias_table(tbl, rows, keys_first=True)
            carry[...] = jnp.zeros_like(carry)

        @pl.when(t < steps)
        def _():
            kv2 = jnp.concatenate([_ld(kvp_ref), _ld(kvc_ref)], axis=0)
            lse_t, dl_t = _ld(lse_ref).T, _ld(dl_ref).T
            lo_mask = lax.broadcasted_iota(jnp.int32, (2 * BLK, LANES), 1) < HEAD_DIM
            halves = [jnp.zeros((2 * BLK, LANES), F32) for _ in range(4)]
            for hk in range(N_KV_HEADS):
                k_lo, k_hi, v_lo, v_hi = _head_operands(kv2, hk, lo_mask)
                cols = [slice(b * LANES, (b + 1) * LANES) for b in (2 * hk, 2 * hk + 1)]
                q2 = jnp.concatenate([_ld(q_ref, cols[0]), _ld(q_ref, cols[1])], axis=0).astype(BF16)
                do2 = jnp.concatenate([_ld(do_ref, cols[0]), _ld(do_ref, cols[1])], axis=0).astype(BF16)
                dq2 = jnp.zeros((2 * BLK, LANES), F32)
                dks, dvs = [], []
                for which, (kk, vv) in enumerate(((k_lo, v_lo), (k_hi, v_hi))):
                    h0, h1 = 4 * hk + which, 4 * hk + 2 + which
                    s = _nt(kk, q2) + _bias2(tbl, n, h0, h1, axis=1)
                    lse2 = jnp.concatenate([lse_t[h0:h0 + 1, :], lse_t[h1:h1 + 1, :]], axis=1)
                    dl2 = jnp.concatenate([dl_t[h0:h0 + 1, :], dl_t[h1:h1 + 1, :]], axis=1)
                    p = jnp.exp(s - lse2)
                    ds = (p * (_nt(vv, do2) - dl2)).astype(BF16)
                    dq2 = dq2 + _tn(ds, kk)
                    dks.append(jnp.dot(ds, q2, preferred_element_type=F32))
                    dvs.append(jnp.dot(p.astype(BF16), do2, preferred_element_type=F32))
                dk_sum = jnp.where(lo_mask, dks[0], dks[1])
                dv_sum = jnp.where(lo_mask, dvs[0], dvs[1])
                for jp in range(2):
                    dq_blk = dq2[jp * BLK:(jp + 1) * BLK]
                    if have_prev:
                        dq_blk = dq_blk + _ld(pdq_ref, cols[jp])
                    if final:
                        dq_blk = dq_blk * (HEAD_DIM ** -0.5)
                    _st(dq_ref, dq_blk.astype(out_dt), cols[jp])
                half, pos = hk // 2, hk % 2
                here = lo_mask if pos == 0 else jnp.logical_not(lo_mask)
                dk_tot = dk_sum + pltpu.roll(dk_sum, HEAD_DIM, axis=1)
                dv_tot = dv_sum + pltpu.roll(dv_sum, HEAD_DIM, axis=1)
                halves[half] = halves[half] + jnp.where(here, dk_tot, 0.0)
                halves[2 + half] = halves[2 + half] + jnp.where(here, dv_tot, 0.0)
            for b in range(4):
                cols = slice(b * LANES, (b + 1) * LANES)
                done = carry[:, cols] + halves[b][0:BLK, :]
                if have_prev:
                    done = done + _ld(pdkv_ref, cols)
                _st(dkv_ref, done.astype(out_dt), cols)
                carry[:, cols] = halves[b][BLK:, :]

        @pl.when(t == steps)
        def _():
            done = carry[...]
            if have_prev:
                done = done + _ld(pdkv_ref)
            _st(dkv_ref, done.astype(out_dt))
            if hosted is not None:
                hosted.finish(h_ins, h_outs, h_sems)

    def spec(width, lag):
        def index(t):
            u = jnp.clip(t - lag, 0, steps - 1)
            return rows.index(u // nb, u % nb)
        return pl.BlockSpec(rows.block + (width,), index)

    def key_prev(t):
        u = jnp.minimum(t, steps - 1)
        return rows.index(u // nb, jnp.maximum(u % nb - 1, 0))

    in_specs = [spec(ATT_W, 0), spec(ATT_W, 0), spec(LANES, 0), spec(LANES, 0), spec(2 * KV_W, 0),
                pl.BlockSpec(rows.block + (2 * KV_W,), key_prev)]
    args = [rows.of(q), rows.of(d_o), rows.of(lse), rows.of(delta), rows.of(kv), rows.of(kv)]
    if have_prev:
        in_specs += [spec(ATT_W, 0), spec(2 * KV_W, 1)]
        args += [rows.of(prev[0]), rows.of(prev[1])]
    out_specs = [spec(ATT_W, 0), spec(2 * KV_W, 1)]
    out_shape = [jax.ShapeDtypeStruct(rows.view + (ATT_W,), out_dt),
                 jax.ShapeDtypeStruct(rows.view + (2 * KV_W,), out_dt)]
    scratch = [pltpu.VMEM((BLK, 2 * KV_W), F32), pltpu.VMEM((2 * N_Q_HEADS, 2 * BLK, BLK), F32)]
    if hosted is not None:
        in_specs += [ANY_SPEC] * hn
        args += hosted.arrays
        out_specs += [ANY_SPEC] * hn
        out_shape += hosted.out_shapes()
        scratch += hosted.sem_shapes()
    res = pl.pallas_call(
        body, name=name, grid=(steps + 1,),
        in_specs=in_specs, out_specs=tuple(out_specs), out_shape=tuple(out_shape), scratch_shapes=scratch,
        compiler_params=_params(("arbitrary",)),
    )(*args)
    return (res[0].reshape(S, ATT_W), res[1].reshape(S, 2 * KV_W)), list(res[2:])


def _dh(segments, w_in, x, dx2, g, hosted=None, tm=1024, tk=512):
    S, D = x.shape
    ns = len(segments)
    counts = [a.shape[1] // tk for a, _ in segments]
    starts = [sum(counts[:s]) for s in range(ns)]
    nk = sum(counts)
    hn = hosted.n if hosted is not None else 0

    def body(*refs):
        seg_refs = refs[:ns]
        w_ref, x_ref, dx2_ref, g_ref = refs[ns:ns + 4]
        h_ins = refs[ns + 4:ns + 4 + hn]
        gx_ref, gng_ref = refs[ns + 4 + hn:ns + 6 + hn]
        h_outs = refs[ns + 6 + hn:ns + 6 + 2 * hn]
        acc = refs[ns + 6 + 2 * hn]
        h_sems = refs[ns + 7 + 2 * hn:]
        k, i = pl.program_id(0), pl.program_id(1)

        @pl.when((i == 0) & (k == 0))
        def _():
            gng_ref[...] = jnp.zeros_like(gng_ref)
            if hosted is not None:
                hosted.start(h_ins, h_outs, h_sems)

        @pl.when(k == 0)
        def _():
            acc[i] = jnp.zeros(acc.shape[1:], F32)

        for s in range(ns):
            @pl.when((k >= starts[s]) & (k < starts[s] + counts[s]))
            def _(s=s):
                t = seg_refs[s][...]
                if segments[s][1]:
                    t = _perm_rows(t, True)
                acc[i] += jnp.dot(t, w_ref[...], preferred_element_type=F32)

        @pl.when(k == nk - 1)
        def _():
            dh = acc[i]
            xf = x_ref[...]
            r = lax.rsqrt(jnp.mean(xf * xf, axis=-1, keepdims=True) + NORM_EPS)
            nrm = xf * r
            gng_ref[...] += jnp.sum(dh * nrm, axis=0, keepdims=True)
            dn = dh * g_ref[...]
            gx_ref[...] = dx2_ref[...] + r * (dn - nrm * jnp.mean(dn * nrm, axis=-1, keepdims=True))

        if hosted is not None:
            @pl.when((i == S // tm - 1) & (k == nk - 1))
            def _():
                hosted.finish(h_ins, h_outs, h_sems)

    ni = S // tm
    row = pl.BlockSpec((tm, D), lambda k, i: (jnp.where(k == nk - 1, i, 0), 0))
    vec = pl.BlockSpec((1, D), lambda k, i: (0, 0))

    def seg_index(s):
        def index(k, i):
            j = k - starts[s]
            return jnp.where(j < 0, 0, jnp.where(j >= counts[s], ni - 1, i)), jnp.clip(j, 0, counts[s] - 1)
        return index

    in_specs = [pl.BlockSpec((tm, tk), seg_index(s)) for s in range(ns)]
    in_specs += [pl.BlockSpec((tk, D), lambda k, i: (k, 0)), row, row, vec]
    args = [a for a, _ in segments] + [w_in, x, dx2, g]
    out_specs = [row, vec]
    out_shape = [jax.ShapeDtypeStruct((S, D), F32), jax.ShapeDtypeStruct((1, D), F32)]
    scratch = [pltpu.VMEM((ni, tm, D), F32)]
    if hosted is not None:
        in_specs += [ANY_SPEC] * hn
        args += hosted.arrays
        out_specs += [ANY_SPEC] * hn
        out_shape += hosted.out_shapes()
        scratch += hosted.sem_shapes()
    res = pl.pallas_call(
        body, name="dh", grid=(nk, S // tm),
        in_specs=in_specs, out_specs=tuple(out_specs), out_shape=tuple(out_shape), scratch_shapes=scratch,
        compiler_params=_params(("arbitrary", "arbitrary"), BIG_VMEM_LIMIT),
    )(*args)
    return res[0], res[1], list(res[2:])


def _tn_matmul(pairs, layout, name, tm=512):
    arrays = []

    def slot(a):
        for i, b in enumerate(arrays):
            if b is a:
                return i
        arrays.append(a)
        return len(arrays) - 1

    slots = [(slot(u), slot(v)) for u, v in pairs]
    ready = [src for blocks in layout for src, _, _ in blocks if not isinstance(src, int)]
    M, K = pairs[0][1].shape
    n_in, n_ready, n_out = len(arrays), len(ready), len(layout)
    last = M // tm - 1

    def body(*refs):
        in_refs, ready_refs = refs[:n_in], refs[n_in:n_in + n_ready]
        o_refs, accs = refs[n_in + n_ready:n_in + n_ready + n_out], refs[n_in + n_ready + n_out:]

        @pl.when(pl.program_id(0) == 0)
        def _():
            for acc in accs:
                acc[...] = jnp.zeros_like(acc)

        for (iu, iv), acc in zip(slots, accs):
            vt = in_refs[iv][...]
            for c in range(0, acc.shape[0], 512):
                acc[c:c + 512, :] += _tn(in_refs[iu][:, c:c + 512], vt)

        @pl.when(pl.program_id(0) == last)
        def _():
            taken = 0
            for o_ref, blocks in zip(o_refs, layout):
                row = 0
                for src, r0, n in blocks:
                    if isinstance(src, int):
                        o_ref[row:row + n, :] = accs[src][r0:r0 + n, :].astype(o_ref.dtype)
                    else:
                        o_ref[row:row + n, :] = ready_refs[taken][r0:r0 + n, :]
                        taken += 1
                    row += n

    out_rows = [sum(n for _, _, n in blocks) for blocks in layout]
    return pl.pallas_call(
        body, name=name, grid=(M // tm,),
        in_specs=[pl.BlockSpec((tm, a.shape[1]), lambda m: (m, 0)) for a in arrays]
        + [pl.BlockSpec(r.shape, lambda m: (0, 0)) for r in ready],
        out_specs=tuple(pl.BlockSpec((rows, K), lambda m: (0, 0)) for rows in out_rows),
        out_shape=tuple(jax.ShapeDtypeStruct((rows, K), BF16) for rows in out_rows),
        scratch_shapes=[pltpu.VMEM((u.shape[1], K), F32) for u, _ in pairs],
        compiler_params=_params(("arbitrary",)),
    )(*arrays, *ready)


def _adamw(parts, w, m, v, name, tr=None, split=None, by_chip=False):
    R, C = w.shape
    tr = R if tr is None else tr
    parts = [parts] if split is None else list(parts)
    npar = len(parts)

    def total(p_ref):
        g = p_ref[0].astype(F32)
        for slot in range(1, p_ref.shape[0]):
            g = g + p_ref[slot].astype(F32)
        return g

    def body(*refs):
        w_ref, m_ref, v_ref, g_out, d_out, m_out, v_out = refs[npar:]
        if split is None:
            g = total(refs[0])
        else:
            g = jnp.where(_mesh_pos()[3] < split, total(refs[0]), total(refs[1]))
        mn = ADAM_B1 * m_ref[...] + (1.0 - ADAM_B1) * g
        vn = ADAM_B2 * v_ref[...] + (1.0 - ADAM_B2) * (g * g)
        m_hat = mn / (1.0 - ADAM_B1 ** ADAM_STEP)
        v_hat = vn / (1.0 - ADAM_B2 ** ADAM_STEP)
        g_out[...] = g
        d_out[...] = -ADAM_LR * (m_hat / (jnp.sqrt(v_hat) + ADAM_EPS) + ADAM_WD * w_ref[...])
        m_out[...] = mn
        v_out[...] = vn

    def used(k):
        return True if split is None else (_mesh_pos()[3] < split) == (k == 0)

    if by_chip:
        parts = [p.reshape(N_DEV // 2, 2, R, C) for p in parts]
        part_specs = [pl.BlockSpec((N_DEV // 2, None, tr, C),
                                   lambda i, k=k: (0, lax.axis_index("c"), jnp.where(used(k), i, 0), 0))
                      for k in range(npar)]
    else:
        part_specs = [pl.BlockSpec((N_DEV, tr, C), lambda i, k=k: (0, jnp.where(used(k), i, 0), 0))
                      for k in range(npar)]
    blk = pl.BlockSpec((tr, C), lambda i: (i, 0))
    shp = jax.ShapeDtypeStruct((R, C), F32)
    return pl.pallas_call(
        body, name=name, grid=(R // tr,),
        in_specs=part_specs + [blk, blk, blk],
        out_specs=(blk, blk, blk, blk), out_shape=(shp, shp, shp, shp),
        compiler_params=_params(("parallel",)),
    )(*parts, w, m, v)


def _local_step(x, target, norm_g, w_in, conv_w, conv_b, ln_g, ln_b, w_out, gf, exchanges=None, first_weights=None,
                late_weights=None, first_rows=ATT_W + 2 * KV_W + ATT_W // 2):
    ex_out, ex_att, ex_conv = exchanges if exchanges is not None else (None, None, None)
    h_rm, h, *first = _norm_rows(x, norm_g, first_weights[0] if first_weights is not None else None)
    if first_weights is not None:
        w_in = first_weights[1](first)
    conv_cols = w_in.shape[0] - 2 * ATT_W - 2 * KV_W
    q, kv, a_gate, gates, *gathered = _inproj(
        h_rm, h, w_in,
        [(ATT_W, HEAD_DIM ** -0.5, True), (2 * KV_W, 1.0, True), (ATT_W, 1.0, True), (conv_cols, 1.0, False)],
        late_weights[0] if late_weights is not None else None)
    if late_weights is not None:
        conv_w, w_out = late_weights[1](gathered)

    alone = [_attn_fwd(q, kv, dil, "attn_fwd_d%d" % dil) for _, dil in PATTERNS[1:]]
    o, lse, y_att = _attn_fwd(q, kv, PATTERNS[0][1], "attn_fwd_d%d" % PATTERNS[0][1], alone, a_gate)
    conv_out, y_conv = _conv_fwd(gates, conv_w, conv_b, ln_g, ln_b)
    dx2, dxb, loss_cols, g_gf = _outproj_loss(x, y_att, y_conv, w_out, gf, target)

    d_o, d_a_gate, delta, dxb_rm = _dy_att(dxb, w_out, a_gate, o)
    g_w_out, = _tn_matmul([(y_att, dxb_rm), (y_conv, dxb)], [[(0, 0, ATT_W), (1, 0, y_conv.shape[1])]], "gw_out")
    acc, out_parts = None, []
    for idx, (_, dil) in enumerate(reversed(PATTERNS)):
        hosted = ex_out(g_w_out) if (idx == 0 and ex_out is not None) else None
        acc, outs = _attn_bwd(q, kv, d_o, lse, delta, dil, acc, idx == len(PATTERNS) - 1, "attn_bwd_d%d" % dil,
                              hosted)
        out_parts += outs
    dq, dkv = acc
    a_lo = first_rows - (ATT_W + 2 * KV_W)
    assert 0 < a_lo < ATT_W
    g_first, g_a_rest = _tn_matmul(
        [(dq, h_rm), (dkv, h_rm), (d_a_gate, h_rm)],
        [[(0, 0, ATT_W), (1, 0, 2 * KV_W), (2, 0, a_lo)], [(2, a_lo, ATT_W - a_lo)]], "gw_in_att")

    d_c_gate, d_conv, g_ln_g, g_ln_b, g_conv_b = _dy_conv(dxb, w_out, gates, conv_out, ln_g, ln_b)
    dgates, g_conv_w, att_parts = _conv_bwd(d_conv, gates, d_c_gate, conv_w,
                                            ex_att(g_first) if ex_att is not None else None)
    g_rest, = _tn_matmul([(dgates, h)], [[(g_a_rest, 0, ATT_W - a_lo), (0, 0, conv_cols)]], "gw_in_conv")
    grad_x, g_norm_g, conv_parts = _dh(
        [(dq, True), (dkv, True), (d_a_gate, True), (dgates, False)], w_in, x, dx2, norm_g,
        ex_conv(g_rest, g_conv_w) if ex_conv is not None else None)
    small = (g_norm_g, g_conv_b, g_ln_g, g_ln_b, g_gf, loss_cols)
    return grad_x, (g_first, g_rest), g_w_out, g_conv_w, small, (out_parts, att_parts, conv_parts)


def kernel(x, norm_g, w_in, conv_w, conv_b, conv_ln_g, conv_ln_b, w_out, final_norm_g, loss_target, m_norm_g, m_w_in, m_conv_w, m_conv_b, m_conv_ln_g, m_conv_ln_b, m_w_out, m_final_norm_g, v_norm_g, v_w_in, v_conv_w, v_conv_b, v_conv_ln_g, v_conv_ln_b, v_w_out, v_final_norm_g):
    S, D = x.shape[1], x.shape[2]
    win_sh, wout_sh, cw_sh = w_in[0].T, w_out[0], conv_w[0]
    cols_sh, rows_sh, ch_sh = win_sh.shape[0], wout_sh.shape[0], cw_sh.shape[1]

    def first_weights(gathered):
        return gathered[0].reshape(N_DEV * cols_sh, D)

    def late_weights(gathered):
        wout_all, cw_all = gathered
        conv_w_full = cw_all.transpose(1, 0, 2).reshape(CONV_K, N_DEV * ch_sh)
        return jnp.pad(conv_w_full, ((0, CONV_HALO - CONV_K), (0, 0))), wout_all.reshape(N_DEV * rows_sh, D)

    gf = final_norm_g.reshape(1, D)

    first = -(-(ATT_W + 2 * KV_W) // cols_sh)

    def ex_out(g_w_out):
        return _Exchange([g_w_out.reshape(N_DEV, rows_sh, D)], [(0, N_DEV)])

    same_core = (2, 4, 6)

    def ex_att(g_first):
        mine = _chip_sum(g_first.reshape(first, cols_sh, D), 0, "rs_att")
        return _Exchange([mine], [(0, first)], [same_core])

    def ex_conv(g_rest, g_conv_w):
        mine = _chip_sum(g_rest.reshape(N_DEV - first, cols_sh, D), first, "rs_conv")
        return _Exchange(
            [mine, g_conv_w[:CONV_K].reshape(CONV_K, N_DEV, ch_sh).transpose(1, 0, 2)],
            [(first, N_DEV), (0, N_DEV)], [same_core, None])

    grad_x, _, _, _, small, parts = _local_step(
        x[0], loss_target[0], norm_g, None, None, conv_b, conv_ln_g, conv_ln_b, None, gf,
        (ex_out, ex_att, ex_conv), (_Gather([win_sh.astype(BF16)]), first_weights),
        (_Gather([wout_sh.astype(BF16), cw_sh]), late_weights), first * cols_sh)
    (wout_parts,), (win_parts_lo,), (win_parts_hi, cw_parts) = parts

    small_pack = jnp.concatenate(list(small) + [jnp.zeros((2, D), F32)], axis=0)
    small_parts, = _exchange(_Exchange([small_pack], [None]), "gather_small")

    upd_win = _adamw((win_parts_lo, win_parts_hi), win_sh, m_w_in[0].T, v_w_in[0].T, "adamw_w_in",
                     tr=cols_sh // 2, split=first, by_chip=True)
    upd_wout = _adamw(wout_parts, wout_sh, m_w_out[0], v_w_out[0], "adamw_w_out", tr=128)
    upd_cw = _adamw(cw_parts, cw_sh, m_conv_w[0], v_conv_w[0], "adamw_conv_w")
    zeros3 = jnp.zeros((3, D), F32)
    stack = lambda a, b, c, d_, e: jnp.concatenate([a, b, c, d_, e.reshape(1, D), zeros3], axis=0)
    upd_small = _adamw(
        small_parts,
        stack(norm_g, conv_b, conv_ln_g, conv_ln_b, final_norm_g),
        stack(m_norm_g, m_conv_b, m_conv_ln_g, m_conv_ln_b, m_final_norm_g),
        stack(v_norm_g, v_conv_b, v_conv_ln_g, v_conv_ln_b, v_final_norm_g) + jnp.concatenate(
            [jnp.zeros((5, D), F32), jnp.ones((3, D), F32)], axis=0),
        "adamw_small")

    loss = 0.5 / D * jnp.sum(upd_small[0][5])

    def outputs(kind):
        sm = upd_small[kind]
        return [sm[0:1], upd_win[kind].T[None], upd_cw[kind][None], sm[1:2], sm[2:3], sm[3:4],
                upd_wout[kind][None], sm[4]]

    return (loss, grad_x[None], *outputs(0), *outputs(1), *outputs(2), *outputs(3))
```

```python
import jax
import jax.numpy as jnp
from jax import lax
from jax.experimental import pallas as pl
from jax.experimental.pallas import tpu as pltpu

F32 = jnp.float32
BF16 = jnp.bfloat16

HEAD_DIM = 64
N_KV_HEADS = 4
N_Q_HEADS = 16
ATT_W = 1024
KV_W = 256
CONV_K = 31
CONV_HALO = 32
PATTERNS = ((128, 1), (512, 4), (2048, 16))
BLK = 128
LANES = 128
NORM_EPS = 1e-6
LN_EPS = 1e-5
NEG = -1e30
N_DEV = 8
ADAM_LR, ADAM_B1, ADAM_B2, ADAM_EPS, ADAM_WD, ADAM_STEP = 0.001, 0.9, 0.999, 1e-08, 0.01, 10
VMEM_LIMIT = 48 * 1024 * 1024
BIG_VMEM_LIMIT = 58 * 1024 * 1024
SLOPES = tuple(2.0 ** (-8.0 * (h + 1) / N_Q_HEADS) for h in range(N_Q_HEADS))
MESH = pl.DeviceIdType.MESH


def _params(sem, vmem_limit=VMEM_LIMIT):
    return pltpu.CompilerParams(dimension_semantics=sem, vmem_limit_bytes=vmem_limit)


def _sigmoid(v):
    return 1.0 / (1.0 + jnp.exp(-v))


def _silu_and_grad(v):
    s = _sigmoid(v)
    return v * s, s * (1.0 + v * (1.0 - s))


ANY_SPEC = pl.BlockSpec(memory_space=pl.ANY)


def _mesh_pos():
    x, y, c = lax.axis_index("x"), lax.axis_index("y"), lax.axis_index("c")
    return x, y, c, 4 * x + 2 * y + c


def _flipped(k, x, y, c):
    px = 1 - x if k & 4 else x
    py = 1 - y if k & 2 else y
    pc = 1 - c if k & 1 else c
    return (px, py, pc), 4 * px + 2 * py + pc


class _Exchange:
    def __init__(self, arrays, dests, flips=None):
        self.arrays, self.dests, self.n = list(arrays), list(dests), len(arrays)
        self.flips = [tuple(range(1, N_DEV)) if f is None else tuple(f)
                      for f in (flips if flips is not None else [None] * self.n)]

    def out_shapes(self):
        return [jax.ShapeDtypeStruct((N_DEV,) + a.shape[-2:], a.dtype) for a in self.arrays]

    def sem_shapes(self):
        return [pltpu.SemaphoreType.DMA((self.n, N_DEV - 1)), pltpu.SemaphoreType.DMA((self.n, N_DEV - 1)),
                pltpu.SemaphoreType.DMA((self.n,))]

    def _when(self, a, dev, fn):
        if self.dests[a] is None:
            fn()
        else:
            lo, hi = self.dests[a]
            pl.when((dev >= lo) & (dev < hi))(fn)

    def _mine(self, ins, a, dev):
        return ins[a] if self.dests[a] is None else ins[a].at[dev - self.dests[a][0]]

    def _copy(self, ins, outs, sems, a, k, src_dev, slot, target):
        return pltpu.make_async_remote_copy(
            src_ref=self._mine(ins, a, src_dev), dst_ref=outs[a].at[slot],
            send_sem=sems[0].at[a, k - 1], recv_sem=sems[1].at[a, k - 1],
            device_id=target, device_id_type=MESH)

    def start(self, ins, outs, sems):
        x, y, c, me = _mesh_pos()
        for a in range(self.n):
            self._when(a, me, lambda a=a: pltpu.make_async_copy(
                self._mine(ins, a, me), outs[a].at[me], sems[2].at[a]).start())
            for k in self.flips[a]:
                target, peer = _flipped(k, x, y, c)
                self._when(a, peer, lambda a=a, k=k, target=target, peer=peer: self._copy(
                    ins, outs, sems, a, k, peer, me, target).start())

    def finish(self, ins, outs, sems):
        x, y, c, me = _mesh_pos()
        lo0 = [0 if d is None else d[0] for d in self.dests]
        for a in range(self.n):
            for k in self.flips[a]:
                target, peer = _flipped(k, x, y, c)
                self._when(a, me, lambda a=a, k=k, peer=peer: self._copy(
                    ins, outs, sems, a, k, lo0[a], peer, (x, y, c)).wait_recv())
            for k in self.flips[a]:
                target, peer = _flipped(k, x, y, c)
                self._when(a, peer, lambda a=a, k=k, target=target, peer=peer: self._copy(
                    ins, outs, sems, a, k, peer, me, target).wait_send())
            self._when(a, me, lambda a=a: pltpu.make_async_copy(
                self._mine(ins, a, me), outs[a].at[me], sems[2].at[a]).wait())


def _exchange(ex, name):
    na = ex.n

    def body(*refs):
        ins, outs, sems = refs[:na], refs[na:2 * na], refs[2 * na:]
        ex.start(ins, outs, sems)
        ex.finish(ins, outs, sems)

    return pl.pallas_call(
        body, name=name, out_shape=tuple(ex.out_shapes()),
        in_specs=[ANY_SPEC] * na, out_specs=tuple([ANY_SPEC] * na), scratch_shapes=ex.sem_shapes(),
    )(*ex.arrays)


def _chip_sum(pieces, lo, name):
    n, R, C = pieces.shape
    rows = 64
    assert R % rows == 0

    def body(p_ref, o_ref, mine_buf, other_buf, sum_buf, send_sems, recv_sems, local_sems, out_sems):
        x, y, c, me = _mesh_pos()

        def remote(i):
            return pltpu.make_async_remote_copy(
                src_ref=p_ref.at[i], dst_ref=other_buf.at[i], send_sem=send_sems.at[i], recv_sem=recv_sems.at[i],
                device_id=(x, y, 1 - c), device_id_type=MESH)

        def local(i):
            return pltpu.make_async_copy(p_ref.at[i], mine_buf.at[i], local_sems.at[i])

        def out(i):
            return pltpu.make_async_copy(sum_buf.at[i], o_ref.at[i], out_sems.at[i])

        summed_by = [(lo + i) % 2 for i in range(n)]
        for i in range(n):
            pl.when(c != summed_by[i])(remote(i).start)
            pl.when(c == summed_by[i])(local(i).start)
        for i in range(n):
            @pl.when(c == summed_by[i])
            def _(i=i):
                local(i).wait()
                remote(i).wait_recv()

                def chunk(j, carry):
                    r = pl.ds(pl.multiple_of(j * rows, rows), rows)
                    sum_buf[i, r, :] = (mine_buf[i, r, :].astype(F32) + other_buf[i, r, :].astype(F32)
                                        ).astype(sum_buf.dtype)
                    return carry

                lax.fori_loop(0, R // rows, chunk, 0)
                out(i).start()
        for i in range(n):
            pl.when(c == summed_by[i])(out(i).wait)
            pl.when(c != summed_by[i])(remote(i).wait_send)

    buf = pltpu.VMEM(pieces.shape, pieces.dtype)
    return pl.pallas_call(
        body, name=name, out_shape=jax.ShapeDtypeStruct(pieces.shape, pieces.dtype),
        in_specs=[ANY_SPEC], out_specs=ANY_SPEC,
        scratch_shapes=[buf, buf, buf] + [pltpu.SemaphoreType.DMA((n,))] * 4,
        compiler_params=pltpu.CompilerParams(vmem_limit_bytes=VMEM_LIMIT),
    )(pieces)


class _Gather:
    def __init__(self, arrays):
        self.arrays, self.n = list(arrays), len(arrays)

    def out_shapes(self):
        return [jax.ShapeDtypeStruct((N_DEV,) + a.shape, a.dtype) for a in self.arrays]

    def sem_shapes(self):
        return [pltpu.SemaphoreType.DMA((self.n, N_DEV - 1)), pltpu.SemaphoreType.DMA((self.n, N_DEV - 1)),
                pltpu.SemaphoreType.DMA((self.n,))]

    def _plan(self, ins, outs, sems):
        x, y, c, me = _mesh_pos()
        chips = [(1 - x, y), (x, 1 - y), (1 - x, 1 - y)]

        def copy(a, k, src, block, to):
            px, py, pc = block
            return pltpu.make_async_remote_copy(
                src_ref=src, dst_ref=outs[a].at[4 * px + 2 * py + pc], send_sem=sems[0].at[a, k],
                recv_sem=sems[1].at[a, k], device_id=to, device_id_type=MESH)

        def landed(a, block):
            px, py, pc = block
            return outs[a].at[4 * px + 2 * py + pc]

        local = [pltpu.make_async_copy(ins[a], outs[a].at[me], sems[2].at[a]) for a in range(self.n)]
        first = []
        for a in range(self.n):
            first.append(copy(a, 0, ins[a], (x, y, c), (x, y, 1 - c)))
            first += [copy(a, 1 + j, ins[a], (x, y, c), (*chip, c)) for j, chip in enumerate(chips[:2])]
        return (x, y, c), chips, copy, landed, local, first

    def start(self, ins, outs, sems):
        *_, local, first = self._plan(ins, outs, sems)
        for cp in local + first:
            cp.start()

    def finish(self, ins, outs, sems):
        (x, y, c), chips, copy, landed, local, first = self._plan(ins, outs, sems)
        south = c == 0
        came = (jnp.where(south, 1 - x, x), jnp.where(south, y, 1 - y), c)
        goes = (jnp.where(south, x, 1 - x), jnp.where(south, 1 - y, y), c)
        passed = []
        for a in range(self.n):
            for j, chip in enumerate(chips[:2]):
                copy(a, 1 + j, ins[a], (*chip, c), (x, y, c)).wait_recv()
            passed.append(copy(a, 3, landed(a, came), came, goes))
            passed += [copy(a, 4 + j, landed(a, (*chip, c)), (*chip, c), (x, y, 1 - c))
                       for j, chip in enumerate(chips[:2])]
        for cp in passed:
            cp.start()
        for a in range(self.n):
            diagonal = (*chips[2], c)
            copy(a, 3, ins[a], diagonal, (x, y, c)).wait_recv()
            cp = copy(a, 6, landed(a, diagonal), diagonal, (x, y, 1 - c))
            cp.start()
            passed.append(cp)
        for a in range(self.n):
            copy(a, 0, ins[a], (x, y, 1 - c), (x, y, c)).wait_recv()
            for j, chip in enumerate(chips):
                copy(a, 4 + j, ins[a], (*chip, 1 - c), (x, y, c)).wait_recv()
        for cp in first + passed:
            cp.wait_send()
        for cp in local:
            cp.wait()


CHUNK = 128
RESIDUES = 16
PER_RES = CHUNK // RESIDUES


def _perm_rows(tile, inverse):
    a = lax.broadcasted_iota(jnp.int32, (CHUNK, CHUNK), 0)
    b = lax.broadcasted_iota(jnp.int32, (CHUNK, CHUNK), 1)
    if inverse:
        a, b = b, a
    p = jnp.where(a == PER_RES * (b % RESIDUES) + b // RESIDUES, 1.0, 0.0).astype(BF16)
    parts = [jnp.dot(p, tile[c * CHUNK:(c + 1) * CHUNK], preferred_element_type=F32)
             for c in range(tile.shape[0] // CHUNK)]
    return jnp.concatenate(parts, axis=0).astype(BF16)


class _Rows:
    def __init__(self, dil, S):
        nc = S // CHUNK
        self.dil = dil
        if dil == 1:
            self.view, self.block, self.nb = (nc, CHUNK), (None, CHUNK), nc
            self.index = lambda r, b: (b, 0, 0)
        elif dil == 4:
            self.view, self.block, self.nb = (nc, 4, 4, PER_RES), (4, 4, None, PER_RES), nc // 4
            self.index = lambda r, b: (b, 0, r, 0, 0)
        elif dil == RESIDUES:
            self.view, self.block, self.nb = (nc, RESIDUES, PER_RES), (RESIDUES, None, PER_RES), nc // RESIDUES
            self.index = lambda r, b: (b, r, 0, 0)
        else:
            raise NotImplementedError(dil)

    def of(self, a):
        return a.reshape(self.view + (a.shape[-1],))

    def spec(self, width, which_block):
        return pl.BlockSpec(self.block + (width,), lambda r, n: self.index(r, which_block(n)))

    def pos(self, row):
        if self.dil == 1:
            return (row % PER_RES) * RESIDUES + row // PER_RES
        if self.dil == 4:
            return (row // 32) * 32 + (row % PER_RES) * 4 + (row % 32) // PER_RES
        return row


def _ld(ref, cols=slice(None)):
    v = ref[(slice(None),) * (len(ref.shape) - 1) + (cols,)]
    return v.reshape(BLK, v.shape[-1])


def _st(ref, val, cols=slice(None)):
    ref[(slice(None),) * (len(ref.shape) - 1) + (cols,)] = val.reshape(ref.shape[:-1] + (val.shape[-1],))


def _norm_rows(x, g, hosted=None, sources=None, tm=512):
    S, D = x.shape
    hn = hosted.n if hosted is not None else 0
    nb = hn if sources is not None else 0

    def body(x_ref, g_ref, *rest):
        h_ins = rest[:hn]
        hrm_out, h_out = rest[hn:hn + 2]
        h_outs = rest[hn + 2:2 * hn + 2]
        bufs = rest[2 * hn + 2:2 * hn + 2 + nb]
        h_sems = rest[2 * hn + 2 + nb:]
        i = pl.program_id(0)
        if sources is not None:
            @pl.when(i == 0)
            def _():
                for src, buf in zip(h_ins, bufs):
                    buf[...] = src[...].astype(buf.dtype)
            h_ins = bufs
        if hosted is not None:
            pl.when(i == 0)(lambda: hosted.start(h_ins, h_outs, h_sems))
        xf = x_ref[...]
        r = lax.rsqrt(jnp.mean(xf * xf, axis=-1, keepdims=True) + NORM_EPS)
        h = (xf * r * g_ref[...]).astype(BF16)
        h_out[...] = h
        hrm_out[...] = _perm_rows(h, False)
        if hosted is not None:
            pl.when(i == S // tm - 1)(lambda: hosted.finish(h_ins, h_outs, h_sems))

    row = pl.BlockSpec((tm, D), lambda i: (i, 0))
    in_specs, args = [row, pl.BlockSpec((1, D), lambda i: (0, 0))], [x, g]
    out_specs, out_shape, scratch = [row, row], [jax.ShapeDtypeStruct((S, D), BF16)] * 2, []
    if hosted is not None:
        if sources is not None:
            in_specs += [pl.BlockSpec(a.shape, lambda i: (0, 0)) for a in sources]
            args += list(sources)
            scratch += [pltpu.VMEM(a.shape, a.dtype) for a in hosted.arrays]
        else:
            in_specs += [ANY_SPEC] * hn
            args += hosted.arrays
        out_specs += [ANY_SPEC] * hn
        out_shape += hosted.out_shapes()
        scratch += hosted.sem_shapes()
    return pl.pallas_call(
        body, name="norm_rows", grid=(S // tm,),
        in_specs=in_specs, out_specs=tuple(out_specs), out_shape=tuple(out_shape), scratch_shapes=scratch,
        compiler_params=_params(("arbitrary",)),
    )(*args)


def _inproj(h_rm, h, w_t, segments, hosted=None, tm=1024, tn=512):
    S, D = h.shape
    ns = len(segments)
    ni = S // tm
    counts = [seg[0] // tn for seg in segments]
    starts = [sum(counts[:s]) for s in range(ns)]

    hn = hosted.n if hosted is not None else 0
    last_p = sum(counts)

    def body(hrm_ref, h_ref, w_ref, *rest):
        h_ins, rest = rest[:hn], rest[hn:]
        outs = rest[:ns]
        h_outs = rest[ns:ns + hn]
        hrm_scr, h_scr = rest[ns + hn:ns + 2 + hn]
        h_sems = rest[ns + 2 + hn:]
        p, i = pl.program_id(0), pl.program_id(1)

        if hosted is not None:
            @pl.when((p == 0) & (i == 0))
            def _():
                hosted.start(h_ins, h_outs, h_sems)

            @pl.when((p == last_p - 1) & (i == ni - 1))
            def _():
                hosted.finish(h_ins, h_outs, h_sems)

        @pl.when(p == 0)
        def _():
            h_scr[i] = h_ref[...]
            hrm_scr[i] = hrm_ref[...]

        for s, (_, scale, rm) in enumerate(segments):
            @pl.when((p >= starts[s]) & (p < starts[s] + counts[s]))
            def _(s=s, scale=scale, rm=rm):
                acc = _nt((hrm_scr if rm else h_scr)[i], w_ref[...])
                outs[s][...] = acc * scale if scale != 1.0 else acc

    def out_index(s):
        def index(p, i):
            j = p - starts[s]
            row = jnp.where(j < 0, 0, jnp.where(j >= counts[s], ni - 1, i))
            return row, jnp.clip(j, 0, counts[s] - 1)
        return index

    first_pass = pl.BlockSpec((tm, D), lambda p, i: (jnp.where(p == 0, i, ni - 1), 0))
    out_specs = [pl.BlockSpec((tm, tn), out_index(s)) for s in range(ns)]
    out_shape = [jax.ShapeDtypeStruct((S, seg[0]), F32) for seg in segments]
    in_specs = [first_pass, first_pass, pl.BlockSpec((tn, D), lambda p, i: (p, 0))]
    args = [h_rm, h, w_t]
    scratch = [pltpu.VMEM((ni, tm, D), BF16), pltpu.VMEM((ni, tm, D), BF16)]
    if hosted is not None:
        in_specs += [ANY_SPEC] * hn
        args += hosted.arrays
        out_specs += [ANY_SPEC] * hn
        out_shape += hosted.out_shapes()
        scratch += hosted.sem_shapes()
    return pl.pallas_call(
        body, name="inproj", grid=(last_p, ni),
        in_specs=in_specs, out_specs=tuple(out_specs), out_shape=tuple(out_shape), scratch_shapes=scratch,
        compiler_params=_params(("arbitrary", "arbitrary"), BIG_VMEM_LIMIT),
    )(*args)


def _fill_bias_table(tbl, rows, keys_first=False):
    shape = (2 * BLK, BLK) if keys_first else (BLK, 2 * BLK)
    qi = lax.broadcasted_iota(jnp.int32, shape, 1 if keys_first else 0)
    kj = lax.broadcasted_iota(jnp.int32, shape, 0 if keys_first else 1)
    dist = rows.pos(qi) - rows.pos(kj % BLK) + jnp.where(kj < BLK, BLK, 0)
    inside = (dist >= 0) & (dist <= BLK)
    negd = (dist * (-rows.dil)).astype(F32)
    for f, valid in enumerate((inside & (kj >= BLK), inside)):
        for h in range(N_Q_HEADS):
            tbl[f * N_Q_HEADS + h] = jnp.where(valid, SLOPES[h] * negd, NEG)


def _bias2(tbl, n, h0, h1, axis=0):
    base = jnp.where(n == 0, 0, N_Q_HEADS)
    return jnp.concatenate([tbl[base + h0], tbl[base + h1]], axis=axis)


def _head_operands(kv2, hk, lo_mask):
    half, pos = hk // 2, hk % 2
    out = []
    for base in (0, KV_W):
        t = kv2[:, base + half * LANES: base + (half + 1) * LANES]
        sw = pltpu.roll(t, HEAD_DIM, axis=1)
        at_lo, at_hi = (t, sw) if pos == 0 else (sw, t)
        out.append(jnp.where(lo_mask, at_lo, 0.0).astype(BF16))
        out.append(jnp.where(lo_mask, 0.0, at_hi).astype(BF16))
    return out


def _nt(a, b):
    return lax.dot_general(a, b, (((1,), (1,)), ((), ())), preferred_element_type=F32)


def _tn(a, b):
    return lax.dot_general(a, b, (((0,), (0,)), ((), ())), preferred_element_type=F32)


def _attn_fwd(q, kv, dil, name, prev=(), gate=None):
    S = q.shape[0]
    rows = _Rows(dil, S)
    nb = rows.nb
    have_prev, last = len(prev) > 0, gate is not None

    def body(*refs):
        refs = list(refs)
        q_ref, kvc_ref, kvp_ref = refs[:3]
        del refs[:3]
        po_refs, pl_refs = refs[0:2 * len(prev):2], refs[1:2 * len(prev):2]
        del refs[:2 * len(prev)]
        if last:
            gate_ref = refs.pop(0)
        o_ref, lse_ref = refs[:2]
        y_ref = refs[2] if last else None
        tbl = refs[-1]
        n = pl.program_id(1)

        @pl.when((pl.program_id(0) == 0) & (n == 0))
        def _():
            _fill_bias_table(tbl, rows)

        kv2 = jnp.concatenate([_ld(kvp_ref), _ld(kvc_ref)], axis=0)
        lo_mask = lax.broadcasted_iota(jnp.int32, (2 * BLK, LANES), 1) < HEAD_DIM
        lane = lax.broadcasted_iota(jnp.int32, (BLK, LANES), 1)
        stats = jnp.zeros((BLK, LANES), F32)
        for hk in range(N_KV_HEADS):
            k_lo, k_hi, v_lo, v_hi = _head_operands(kv2, hk, lo_mask)
            cols = [slice(b * LANES, (b + 1) * LANES) for b in (2 * hk, 2 * hk + 1)]
            q2 = jnp.concatenate([_ld(q_ref, cols[0]), _ld(q_ref, cols[1])], axis=0).astype(BF16)
            o2 = jnp.zeros((2 * BLK, LANES), F32)
            for which, (kk, vv) in enumerate(((k_lo, v_lo), (k_hi, v_hi))):
                h0, h1 = 4 * hk + which, 4 * hk + 2 + which
                s = _nt(q2, kk) + _bias2(tbl, n, h0, h1)
                m = jnp.max(s, axis=1, keepdims=True)
                p = jnp.exp(s - m)
                l = jnp.sum(p, axis=1, keepdims=True)
                o2 = o2 + jnp.dot(p.astype(BF16), vv, preferred_element_type=F32) * (1.0 / l)
                lse = m + jnp.log(l)
                stats = jnp.where(lane == h0, lse[0:BLK], stats)
                stats = jnp.where(lane == h1, lse[BLK:], stats)
            _st(o_ref, o2[0:BLK], cols[0])
            _st(o_ref, o2[BLK:], cols[1])
        if have_prev:
            others = [_ld(r) for r in pl_refs]
            top = stats
            for b in others:
                top = jnp.maximum(top, b)
            e_new = jnp.exp(stats - top)
            e_old = [jnp.exp(b - top) for b in others]
            total = e_new
            for e in e_old:
                total = total + e
            stats = top + jnp.log(total)
            inv = 1.0 / total
            w_new, w_old = e_new * inv, [e * inv for e in e_old]
        if have_prev or last:
            lo = lane < HEAD_DIM
            for blk in range(ATT_W // LANES):
                cols = slice(blk * LANES, (blk + 1) * LANES)
                o_blk = _ld(o_ref, cols)
                if have_prev:
                    pick = lambda w: jnp.where(lo, w[:, 2 * blk:2 * blk + 1], w[:, 2 * blk + 1:2 * blk + 2])
                    o_blk = o_blk * pick(w_new)
                    for po_ref, w in zip(po_refs, w_old):
                        o_blk = o_blk + _ld(po_ref, cols) * pick(w)
                    _st(o_ref, o_blk, cols)
                if last:
                    a = _ld(gate_ref, cols)
                    _st(y_ref, (o_blk * (a * _sigmoid(a))).astype(BF16), cols)
        _st(lse_ref, stats)

    here = lambda n: n
    before_n = lambda n: jnp.maximum(n - 1, 0)
    in_specs = [rows.spec(ATT_W, here), rows.spec(2 * KV_W, here), rows.spec(2 * KV_W, before_n)]
    args = [rows.of(q), rows.of(kv), rows.of(kv)]
    for o_other, lse_other in prev:
        in_specs += [rows.spec(ATT_W, here), rows.spec(LANES, here)]
        args += [rows.of(o_other), rows.of(lse_other)]
    out_specs = [rows.spec(ATT_W, here), rows.spec(LANES, here)]
    out_shape = [jax.ShapeDtypeStruct(rows.view + (ATT_W,), F32), jax.ShapeDtypeStruct(rows.view + (LANES,), F32)]
    if last:
        in_specs.append(rows.spec(ATT_W, here))
        args.append(rows.of(gate))
        out_specs.append(rows.spec(ATT_W, here))
        out_shape.append(jax.ShapeDtypeStruct(rows.view + (ATT_W,), BF16))
    res = pl.pallas_call(
        body, name=name, grid=(dil, nb),
        in_specs=in_specs, out_specs=tuple(out_specs), out_shape=tuple(out_shape),
        scratch_shapes=[pltpu.VMEM((2 * N_Q_HEADS, BLK, 2 * BLK), F32)],
        compiler_params=_params(("arbitrary", "arbitrary")),
    )(*args)
    return tuple(r.reshape(S, r.shape[-1]) for r in res)


def _shifted_copies(buf, phases):
    n = phases.shape[1]
    for b in range(1, 8):
        phases[b - 1] = buf[b:b + n, :]


def _window(buf, phases, start, cols):
    b = start % 8
    if b == 0:
        return buf[start:start + 8, cols]
    return phases[b - 1, start - b:start - b + 8, cols]


def _broadcast_taps(w_ref, wb):
    for j in range(CONV_K):
        wb[j] = jnp.broadcast_to(w_ref[j:j + 1, :], wb.shape[1:])


def _conv_fwd(gates, conv_w, conv_b, ln_g, ln_b, tt=256):
    S = gates.shape[0]
    C = conv_w.shape[1]
    hb = tt // CONV_HALO

    def body(val_ref, glu_ref, hval_ref, hglu_ref, gate_ref, w_ref, b_ref, g_ref, beta_ref,
             conv_ref, y_ref, hbuf, hph):
        i = pl.program_id(0)
        halo = hval_ref[...] * _sigmoid(hglu_ref[...])
        hbuf[0:CONV_HALO, :] = jnp.where(i > 0, halo, 0.0)
        hbuf[CONV_HALO:, :] = val_ref[...] * _sigmoid(glu_ref[...])
        _shifted_copies(hbuf, hph)
        for cb in range(C // LANES):
            cols = slice(cb * LANES, (cb + 1) * LANES)
            wj = [jnp.broadcast_to(w_ref[j:j + 1, cols], (8, LANES)) for j in range(CONV_K)]
            for rc in range(tt // 8):
                acc = jnp.zeros((8, LANES), F32)
                for j in range(CONV_K):
                    start = rc * 8 + CONV_HALO - (CONV_K - 1) + j
                    acc = acc + _window(hbuf, hph, start, cols) * wj[j]
                conv_ref[rc * 8:(rc + 1) * 8, cols] = acc
        cv = conv_ref[...] + b_ref[...]
        conv_ref[...] = cv
        mu = jnp.mean(cv, axis=-1, keepdims=True)
        xc = cv - mu
        var = jnp.mean(xc * xc, axis=-1, keepdims=True)
        ln = xc * lax.rsqrt(var + LN_EPS) * g_ref[...] + beta_ref[...]
        gt = gate_ref[...]
        y_ref[...] = (ln * _sigmoid(ln) * (gt * _sigmoid(gt))).astype(BF16)

    vec = pl.BlockSpec((1, C), lambda i: (0, 0))
    return pl.pallas_call(
        body, name="conv_fwd", grid=(S // tt,),
        in_specs=[pl.BlockSpec((tt, C), lambda i: (i, 0)),
                  pl.BlockSpec((tt, C), lambda i: (i, 1)),
                  pl.BlockSpec((CONV_HALO, C), lambda i: (jnp.maximum(i * hb - 1, 0), 0)),
                  pl.BlockSpec((CONV_HALO, C), lambda i: (jnp.maximum(i * hb - 1, 0), 1)),
                  pl.BlockSpec((tt, C), lambda i: (i, 2)),
                  pl.BlockSpec((CONV_HALO, C), lambda i: (0, 0)), vec, vec, vec],
        out_specs=(pl.BlockSpec((tt, C), lambda i: (i, 0)), pl.BlockSpec((tt, C), lambda i: (i, 0))),
        out_shape=(jax.ShapeDtypeStruct((S, C), F32), jax.ShapeDtypeStruct((S, C), BF16)),
        scratch_shapes=[pltpu.VMEM((tt + CONV_HALO, C), F32), pltpu.VMEM((7, tt + CONV_HALO - 8, C), F32)],
        compiler_params=_params(("parallel",)),
    )(gates, gates, gates, gates, gates, conv_w, conv_b, ln_g, ln_b)


def _outproj_loss(x, y_att, y_conv, w_out, gf, target, tm=512):
    S, D = x.shape
    E = y_att.shape[1]

    def body(x_ref, ya_ref, yc_ref, w_ref, gf_ref, t_ref, dx_ref, dxb_ref, loss_ref, ggf_ref):
        @pl.when(pl.program_id(0) == 0)
        def _():
            loss_ref[...] = jnp.zeros_like(loss_ref)
            ggf_ref[...] = jnp.zeros_like(ggf_ref)

        x2 = (x_ref[...] + jnp.dot(_perm_rows(ya_ref[...], True), w_ref[0:E, :], preferred_element_type=F32)
              + jnp.dot(yc_ref[...], w_ref[E:, :], preferred_element_type=F32))
        r = lax.rsqrt(jnp.mean(x2 * x2, axis=-1, keepdims=True) + NORM_EPS)
        nrm = x2 * r
        gfv = gf_ref[...]
        err = nrm * gfv - t_ref[...]
        loss_ref[...] += jnp.sum(err * err, axis=0, keepdims=True)
        dout = err * (1.0 / D)
        ggf_ref[...] += jnp.sum(dout * nrm, axis=0, keepdims=True)
        dn = dout * gfv
        dx2 = r * (dn - nrm * jnp.mean(dn * nrm, axis=-1, keepdims=True))
        dx_ref[...] = dx2
        dxb_ref[...] = dx2.astype(BF16)

    row = lambda w: pl.BlockSpec((tm, w), lambda i: (i, 0))
    vec = pl.BlockSpec((1, D), lambda i: (0, 0))
    return pl.pallas_call(
        body, name="outproj_loss", grid=(S // tm,),
        in_specs=[row(D), row(E), row(E), pl.BlockSpec((2 * E, D), lambda i: (0, 0)), vec, row(D)],
        out_specs=(row(D), row(D), vec, vec),
        out_shape=(jax.ShapeDtypeStruct((S, D), F32), jax.ShapeDtypeStruct((S, D), BF16),
                   jax.ShapeDtypeStruct((1, D), F32), jax.ShapeDtypeStruct((1, D), F32)),
        compiler_params=_params(("arbitrary",)),
    )(x, y_att, y_conv, w_out, gf, target)


def _split3(v):
    hi = v.astype(BF16)
    r1 = v - hi.astype(F32)
    mid = r1.astype(BF16)
    lo = (r1 - mid.astype(F32)).astype(BF16)
    return hi, mid, lo


def _dy_att(dxb, w_out, gates, o, tm=512):
    S, D = dxb.shape
    E = ATT_W

    def body(dx_ref, w_ref, a_ref, o_ref, do_ref, da_ref, dl_ref, dxr_ref):
        dxr = _perm_rows(dx_ref[...], False)
        dxr_ref[...] = dxr
        dya = _nt(dxr, w_ref[...])
        a = a_ref[...]
        ov = o_ref[...]
        sl, dsl = _silu_and_grad(a)
        d_o = dya * sl
        do_ref[...] = d_o
        da_ref[...] = (dya * ov * dsl).astype(BF16)
        ci = lax.broadcasted_iota(jnp.int32, (E, LANES), 0) // HEAD_DIM
        hi = lax.broadcasted_iota(jnp.int32, (E, LANES), 1)
        sel = jnp.where(ci == hi, 1.0, 0.0).astype(BF16)
        acc = jnp.zeros((tm, LANES), F32)
        for part in _split3(d_o * ov):
            acc = acc + jnp.dot(part, sel, preferred_element_type=F32)
        dl_ref[...] = acc

    row = lambda w: pl.BlockSpec((tm, w), lambda i: (i, 0))
    return pl.pallas_call(
        body, name="dy_att", grid=(S // tm,),
        in_specs=[row(D), pl.BlockSpec((E, D), lambda i: (0, 0)), row(E), row(E)],
        out_specs=(row(E), row(E), row(LANES), row(D)),
        out_shape=(jax.ShapeDtypeStruct((S, E), F32), jax.ShapeDtypeStruct((S, E), BF16),
                   jax.ShapeDtypeStruct((S, LANES), F32), jax.ShapeDtypeStruct((S, D), BF16)),
        compiler_params=_params(("parallel",)),
    )(dxb, w_out, gates, o)


def _dy_conv(dxb, w_out, gates, conv_out, ln_g, ln_b, tm=512):
    S, D = dxb.shape
    C = conv_out.shape[1]

    def body(dx_ref, w_ref, gate_ref, cv_ref, g_ref, beta_ref, dgate_ref, dconv_ref, gg_ref, gb_ref, gcb_ref):
        @pl.when(pl.program_id(0) == 0)
        def _():
            gg_ref[...] = jnp.zeros_like(gg_ref)
            gb_ref[...] = jnp.zeros_like(gb_ref)
            gcb_ref[...] = jnp.zeros_like(gcb_ref)

        dyc = _nt(dx_ref[...], w_ref[...])
        cv = cv_ref[...]
        mu = jnp.mean(cv, axis=-1, keepdims=True)
        xc = cv - mu
        rstd = lax.rsqrt(jnp.mean(xc * xc, axis=-1, keepdims=True) + LN_EPS)
        nrm = xc * rstd
        gv = g_ref[...]
        ln = nrm * gv + beta_ref[...]
        u, du = _silu_and_grad(ln)
        gt = gate_ref[...]
        g2, dg2 = _silu_and_grad(gt)
        dgate_ref[...] = (dyc * u * dg2).astype(BF16)
        d_ln = dyc * g2 * du
        gb_ref[...] += jnp.sum(d_ln, axis=0, keepdims=True)
        gg_ref[...] += jnp.sum(d_ln * nrm, axis=0, keepdims=True)
        dn = d_ln * gv
        d_conv = rstd * (dn - jnp.mean(dn, axis=-1, keepdims=True)
                         - nrm * jnp.mean(dn * nrm, axis=-1, keepdims=True))
        dconv_ref[...] = d_conv
        gcb_ref[...] += jnp.sum(d_conv, axis=0, keepdims=True)

    row = lambda w: pl.BlockSpec((tm, w), lambda i: (i, 0))
    vec = pl.BlockSpec((1, C), lambda i: (0, 0))
    return pl.pallas_call(
        body, name="dy_conv", grid=(S // tm,),
        in_specs=[row(D), pl.BlockSpec((C, D), lambda i: (1, 0)),
                  pl.BlockSpec((tm, C), lambda i: (i, 2)), row(C), vec, vec],
        out_specs=(row(C), row(C), vec, vec, vec),
        out_shape=(jax.ShapeDtypeStruct((S, C), BF16), jax.ShapeDtypeStruct((S, C), F32),
                   jax.ShapeDtypeStruct((1, C), F32), jax.ShapeDtypeStruct((1, C), F32),
                   jax.ShapeDtypeStruct((1, C), F32)),
        compiler_params=_params(("arbitrary",)),
    )(dxb, w_out, gates, conv_out, ln_g, ln_b)


def _conv_bwd(d_conv, gates, d_c_gate, conv_w, hosted=None, tt=256):
    S, C = d_conv.shape
    hb = tt // CONV_HALO
    nt = S // tt
    hn = hosted.n if hosted is not None else 0

    def body(*refs):
        dc_ref, dnext_ref, val_ref, glu_ref, dg_ref, w_ref = refs[:6]
        h_ins = refs[6:6 + hn]
        out_ref, gw_ref = refs[6 + hn:8 + hn]
        h_outs = refs[8 + hn:8 + 2 * hn]
        hbuf, dbuf, dhbuf, dph, wb = refs[8 + 2 * hn:13 + 2 * hn]
        h_sems = refs[13 + 2 * hn:]
        i = pl.program_id(0)

        @pl.when(i == 0)
        def _():
            gw_ref[...] = jnp.zeros_like(gw_ref)
            _broadcast_taps(w_ref, wb)
            if hosted is not None:
                hosted.start(h_ins, h_outs, h_sems)

        val = val_ref[...]
        sg = _sigmoid(glu_ref[...])
        hbuf[...] = val * sg
        dbuf[0:tt, :] = dc_ref[...]
        dbuf[tt:, :] = jnp.where(i < nt - 1, dnext_ref[...], 0.0)
        _shifted_copies(dbuf, dph)
        for cb in range(C // LANES):
            cols = slice(cb * LANES, (cb + 1) * LANES)
            gacc = [jnp.zeros((8, LANES), F32) for _ in range(CONV_K)]
            group = 2
            for rc0 in range(0, tt // 8, group):
                hcur = [hbuf[(rc0 + r) * 8:(rc0 + r + 1) * 8, cols] for r in range(group)]
                accs = [jnp.zeros((8, LANES), F32) for _ in range(group)]
                for j in range(CONV_K):
                    wj = wb[j, :, cols]
                    for r in range(group):
                        dwin = _window(dbuf, dph, (rc0 + r) * 8 + (CONV_K - 1) - j, cols)
                        accs[r] = accs[r] + dwin * wj
                        gacc[j] = gacc[j] + dwin * hcur[r]
                for r in range(group):
                    dhbuf[(rc0 + r) * 8:(rc0 + r + 1) * 8, cols] = accs[r]
            for j in range(CONV_K):
                gw_ref[j:j + 1, cols] += jnp.sum(gacc[j], axis=0, keepdims=True)
        d_h = dhbuf[...]
        out_ref[:, 0:C] = (d_h * sg).astype(BF16)
        out_ref[:, C:2 * C] = (d_h * val * sg * (1.0 - sg)).astype(BF16)
        out_ref[:, 2 * C:3 * C] = dg_ref[...]

        if hosted is not None:
            @pl.when(i == nt - 1)
            def _():
                hosted.finish(h_ins, h_outs, h_sems)

    tile = lambda col: pl.BlockSpec((tt, C), lambda i: (i, col))
    in_specs = [tile(0),
                pl.BlockSpec((CONV_HALO, C), lambda i: (jnp.minimum((i + 1) * hb, S // CONV_HALO - 1), 0)),
                tile(0), tile(1), tile(0),
                pl.BlockSpec((CONV_HALO, C), lambda i: (0, 0))]
    args = [d_conv, d_conv, gates, gates, d_c_gate, conv_w]
    out_specs = [pl.BlockSpec((tt, 3 * C), lambda i: (i, 0)), pl.BlockSpec((CONV_HALO, C), lambda i: (0, 0))]
    out_shape = [jax.ShapeDtypeStruct((S, 3 * C), BF16), jax.ShapeDtypeStruct((CONV_HALO, C), F32)]
    scratch = [pltpu.VMEM((tt, C), F32), pltpu.VMEM((tt + CONV_HALO, C), F32), pltpu.VMEM((tt, C), F32),
               pltpu.VMEM((7, tt + CONV_HALO - 8, C), F32), pltpu.VMEM((CONV_K, 8, C), F32)]
    if hosted is not None:
        in_specs += [ANY_SPEC] * hn
        args += hosted.arrays
        out_specs += [ANY_SPEC] * hn
        out_shape += hosted.out_shapes()
        scratch += hosted.sem_shapes()
    res = pl.pallas_call(
        body, name="conv_bwd", grid=(nt,),
        in_specs=in_specs, out_specs=tuple(out_specs), out_shape=tuple(out_shape), scratch_shapes=scratch,
        compiler_params=_params(("arbitrary",)),
    )(*args)
    return res[0], res[1], list(res[2:])


def _attn_bwd(q, kv, d_o, lse, delta, dil, prev, final, name, hosted=None):
    S = q.shape[0]
    rows = _Rows(dil, S)
    nb = rows.nb
    steps = dil * nb
    out_dt = BF16 if final else F32
    have_prev = prev is not None
    hn = hosted.n if hosted is not None else 0

    def body(*refs):
        refs = list(refs)
        q_ref, do_ref, lse_ref, dl_ref, kvc_ref, kvp_ref = refs[:6]
        del refs[:6]
        if have_prev:
            pdq_ref, pdkv_ref = refs[:2]
            del refs[:2]
        h_ins = refs[:hn]
        dq_ref, dkv_ref = refs[hn:hn + 2]
        h_outs = refs[hn + 2:2 * hn + 2]
        carry, tbl = refs[2 * hn + 2:2 * hn + 4]
        h_sems = refs[2 * hn + 4:]
        t = pl.program_id(0)
        n = t % nb

        @pl.when(t == 0)
        def _():
            if hosted is not None:
                hosted.start(h_ins, h_outs, h_sems)
            _fill_bias_table(tbl, rows, keys_first=True)
            carry[...] = jnp.zeros_like(carry)

        @pl.when(t < steps)
        def _():
            kv2 = jnp.concatenate([_ld(kvp_ref), _ld(kvc_ref)], axis=0)
            lse_t, dl_t = _ld(lse_ref).T, _ld(dl_ref).T
            lo_mask = lax.broadcasted_iota(jnp.int32, (2 * BLK, LANES), 1) < HEAD_DIM
            halves = [jnp.zeros((2 * BLK, LANES), F32) for _ in range(4)]
            for hk in range(N_KV_HEADS):
                k_lo, k_hi, v_lo, v_hi = _head_operands(kv2, hk, lo_mask)
                cols = [slice(b * LANES, (b + 1) * LANES) for b in (2 * hk, 2 * hk + 1)]
                q2 = jnp.concatenate([_ld(q_ref, cols[0]), _ld(q_ref, cols[1])], axis=0).astype(BF16)
                do2 = jnp.concatenate([_ld(do_ref, cols[0]), _ld(do_ref, cols[1])], axis=0).astype(BF16)
                dq2 = jnp.zeros((2 * BLK, LANES), F32)
                dks, dvs = [], []
                for which, (kk, vv) in enumerate(((k_lo, v_lo), (k_hi, v_hi))):
                    h0, h1 = 4 * hk + which, 4 * hk + 2 + which
                    s = _nt(kk, q2) + _bias2(tbl, n, h0, h1, axis=1)
                    lse2 = jnp.concatenate([lse_t[h0:h0 + 1, :], lse_t[h1:h1 + 1, :]], axis=1)
                    dl2 = jnp.concatenate([dl_t[h0:h0 + 1, :], dl_t[h1:h1 + 1, :]], axis=1)
                    p = jnp.exp(s - lse2)
                    ds = (p * (_nt(vv, do2) - dl2)).astype(BF16)
                    dq2 = dq2 + _tn(ds, kk)
                    dks.append(jnp.dot(ds, q2, preferred_element_type=F32))
                    dvs.append(jnp.dot(p.astype(BF16), do2, preferred_element_type=F32))
                dk_sum = jnp.where(lo_mask, dks[0], dks[1])
                dv_sum = jnp.where(lo_mask, dvs[0], dvs[1])
                for jp in range(2):
                    dq_blk = dq2[jp * BLK:(jp + 1) * BLK]
                    if have_prev:
                        dq_blk = dq_blk + _ld(pdq_ref, cols[jp])
                    if final:
                        dq_blk = dq_blk * (HEAD_DIM ** -0.5)
                    _st(dq_ref, dq_blk.astype(out_dt), cols[jp])
                half, pos = hk // 2, hk % 2
                here = lo_mask if pos == 0 else jnp.logical_not(lo_mask)
                dk_tot = dk_sum + pltpu.roll(dk_sum, HEAD_DIM, axis=1)
                dv_tot = dv_sum + pltpu.roll(dv_sum, HEAD_DIM, axis=1)
                halves[half] = halves[half] + jnp.where(here, dk_tot, 0.0)
                halves[2 + half] = halves[2 + half] + jnp.where(here, dv_tot, 0.0)
            for b in range(4):
                cols = slice(b * LANES, (b + 1) * LANES)
                done = carry[:, cols] + halves[b][0:BLK, :]
                if have_prev:
                    done = done + _ld(pdkv_ref, cols)
                _st(dkv_ref, done.astype(out_dt), cols)
                carry[:, cols] = halves[b][BLK:, :]

        @pl.when(t == steps)
        def _():
            done = carry[...]
            if have_prev:
                done = done + _ld(pdkv_ref)
            _st(dkv_ref, done.astype(out_dt))
            if hosted is not None:
                hosted.finish(h_ins, h_outs, h_sems)

    def spec(width, lag):
        def index(t):
            u = jnp.clip(t - lag, 0, steps - 1)
            return rows.index(u // nb, u % nb)
        return pl.BlockSpec(rows.block + (width,), index)

    def key_prev(t):
        u = jnp.minimum(t, steps - 1)
        return rows.index(u // nb, jnp.maximum(u % nb - 1, 0))

    in_specs = [spec(ATT_W, 0), spec(ATT_W, 0), spec(LANES, 0), spec(LANES, 0), spec(2 * KV_W, 0),
                pl.BlockSpec(rows.block + (2 * KV_W,), key_prev)]
    args = [rows.of(q), rows.of(d_o), rows.of(lse), rows.of(delta), rows.of(kv), rows.of(kv)]
    if have_prev:
        in_specs += [spec(ATT_W, 0), spec(2 * KV_W, 1)]
        args += [rows.of(prev[0]), rows.of(prev[1])]
    out_specs = [spec(ATT_W, 0), spec(2 * KV_W, 1)]
    out_shape = [jax.ShapeDtypeStruct(rows.view + (ATT_W,), out_dt),
                 jax.ShapeDtypeStruct(rows.view + (2 * KV_W,), out_dt)]
    scratch = [pltpu.VMEM((BLK, 2 * KV_W), F32), pltpu.VMEM((2 * N_Q_HEADS, 2 * BLK, BLK), F32)]
    if hosted is not None:
        in_specs += [ANY_SPEC] * hn
        args += hosted.arrays
        out_specs += [ANY_SPEC] * hn
        out_shape += hosted.out_shapes()
        scratch += hosted.sem_shapes()
    res = pl.pallas_call(
        body, name=name, grid=(steps + 1,),
        in_specs=in_specs, out_specs=tuple(out_specs), out_shape=tuple(out_shape), scratch_shapes=scratch,
        compiler_params=_params(("arbitrary",)),
    )(*args)
    return (res[0].reshape(S, ATT_W), res[1].reshape(S, 2 * KV_W)), list(res[2:])


def _dh(segments, w_in, x, dx2, g, hosted=None, tm=1024, tk=512):
    S, D = x.shape
    ns = len(segments)
    counts = [a.shape[1] // tk for a, _ in segments]
    starts = [sum(counts[:s]) for s in range(ns)]
    nk = sum(counts)
    hn = hosted.n if hosted is not None else 0

    def body(*refs):
        seg_refs = refs[:ns]
        w_ref, x_ref, dx2_ref, g_ref = refs[ns:ns + 4]
        h_ins = refs[ns + 4:ns + 4 + hn]
        gx_ref, gng_ref = refs[ns + 4 + hn:ns + 6 + hn]
        h_outs = refs[ns + 6 + hn:ns + 6 + 2 * hn]
        acc = refs[ns + 6 + 2 * hn]
        h_sems = refs[ns + 7 + 2 * hn:]
        k, i = pl.program_id(0), pl.program_id(1)

        @pl.when((i == 0) & (k == 0))
        def _():
            gng_ref[...] = jnp.zeros_like(gng_ref)
            if hosted is not None:
                hosted.start(h_ins, h_outs, h_sems)

        @pl.when(k == 0)
        def _():
            acc[i] = jnp.zeros(acc.shape[1:], F32)

        for s in range(ns):
            @pl.when((k >= starts[s]) & (k < starts[s] + counts[s]))
            def _(s=s):
                t = seg_refs[s][...]
                if segments[s][1]:
                    t = _perm_rows(t, True)
                acc[i] += jnp.dot(t, w_ref[...], preferred_element_type=F32)

        @pl.when(k == nk - 1)
        def _():
            dh = acc[i]
            xf = x_ref[...]
            r = lax.rsqrt(jnp.mean(xf * xf, axis=-1, keepdims=True) + NORM_EPS)
            nrm = xf * r
            gng_ref[...] += jnp.sum(dh * nrm, axis=0, keepdims=True)
            dn = dh * g_ref[...]
            gx_ref[...] = dx2_ref[...] + r * (dn - nrm * jnp.mean(dn * nrm, axis=-1, keepdims=True))

        if hosted is not None:
            @pl.when((i == S // tm - 1) & (k == nk - 1))
            def _():
                hosted.finish(h_ins, h_outs, h_sems)

    ni = S // tm
    row = pl.BlockSpec((tm, D), lambda k, i: (jnp.where(k == nk - 1, i, 0), 0))
    vec = pl.BlockSpec((1, D), lambda k, i: (0, 0))

    def seg_index(s):
        def index(k, i):
            j = k - starts[s]
            return jnp.where(j < 0, 0, jnp.where(j >= counts[s], ni - 1, i)), jnp.clip(j, 0, counts[s] - 1)
        return index

    in_specs = [pl.BlockSpec((tm, tk), seg_index(s)) for s in range(ns)]
    in_specs += [pl.BlockSpec((tk, D), lambda k, i: (k, 0)), row, row, vec]
    args = [a for a, _ in segments] + [w_in, x, dx2, g]
    out_specs = [row, vec]
    out_shape = [jax.ShapeDtypeStruct((S, D), F32), jax.ShapeDtypeStruct((1, D), F32)]
    scratch = [pltpu.VMEM((ni, tm, D), F32)]
    if hosted is not None:
        in_specs += [ANY_SPEC] * hn
        args += hosted.arrays
        out_specs += [ANY_SPEC] * hn
        out_shape += hosted.out_shapes()
        scratch += hosted.sem_shapes()
    res = pl.pallas_call(
        body, name="dh", grid=(nk, S // tm),
        in_specs=in_specs, out_specs=tuple(out_specs), out_shape=tuple(out_shape), scratch_shapes=scratch,
        compiler_params=_params(("arbitrary", "arbitrary"), BIG_VMEM_LIMIT),
    )(*args)
    return res[0], res[1], list(res[2:])


def _tn_matmul(pairs, layout, name, tm=512):
    arrays = []

    def slot(a):
        for i, b in enumerate(arrays):
            if b is a:
                return i
        arrays.append(a)
        return len(arrays) - 1

    slots = [(slot(u), slot(v)) for u, v in pairs]
    ready = [src for blocks in layout for src, _, _ in blocks if not isinstance(src, int)]
    M, K = pairs[0][1].shape
    n_in, n_ready, n_out = len(arrays), len(ready), len(layout)
    last = M // tm - 1

    def body(*refs):
        in_refs, ready_refs = refs[:n_in], refs[n_in:n_in + n_ready]
        o_refs, accs = refs[n_in + n_ready:n_in + n_ready + n_out], refs[n_in + n_ready + n_out:]

        @pl.when(pl.program_id(0) == 0)
        def _():
            for acc in accs:
                acc[...] = jnp.zeros_like(acc)

        for (iu, iv), acc in zip(slots, accs):
            vt = in_refs[iv][...]
            for c in range(0, acc.shape[0], 512):
                acc[c:c + 512, :] += _tn(in_refs[iu][:, c:c + 512], vt)

        @pl.when(pl.program_id(0) == last)
        def _():
            taken = 0
            for o_ref, blocks in zip(o_refs, layout):
                row = 0
                for src, r0, n in blocks:
                    if isinstance(src, int):
                        o_ref[row:row + n, :] = accs[src][r0:r0 + n, :].astype(o_ref.dtype)
                    else:
                        o_ref[row:row + n, :] = ready_refs[taken][r0:r0 + n, :]
                        taken += 1
                    row += n

    out_rows = [sum(n for _, _, n in blocks) for blocks in layout]
    return pl.pallas_call(
        body, name=name, grid=(M // tm,),
        in_specs=[pl.BlockSpec((tm, a.shape[1]), lambda m: (m, 0)) for a in arrays]
        + [pl.BlockSpec(r.shape, lambda m: (0, 0)) for r in ready],
        out_specs=tuple(pl.BlockSpec((rows, K), lambda m: (0, 0)) for rows in out_rows),
        out_shape=tuple(jax.ShapeDtypeStruct((rows, K), BF16) for rows in out_rows),
        scratch_shapes=[pltpu.VMEM((u.shape[1], K), F32) for u, _ in pairs],
        compiler_params=_params(("arbitrary",)),
    )(*arrays, *ready)


def _adamw(parts, w, m, v, name, tr=None, split=None, by_chip=False):
    R, C = w.shape
    tr = R if tr is None else tr
    parts = [parts] if split is None else list(parts)
    npar = len(parts)

    def total(p_ref):
        g = p_ref[0].astype(F32)
        for slot in range(1, p_ref.shape[0]):
            g = g + p_ref[slot].astype(F32)
        return g

    def body(*refs):
        w_ref, m_ref, v_ref, g_out, d_out, m_out, v_out = refs[npar:]
        if split is None:
            g = total(refs[0])
        else:
            g = jnp.where(_mesh_pos()[3] < split, total(refs[0]), total(refs[1]))
        mn = ADAM_B1 * m_ref[...] + (1.0 - ADAM_B1) * g
        vn = ADAM_B2 * v_ref[...] + (1.0 - ADAM_B2) * (g * g)
        m_hat = mn / (1.0 - ADAM_B1 ** ADAM_STEP)
        v_hat = vn / (1.0 - ADAM_B2 ** ADAM_STEP)
        g_out[...] = g
        d_out[...] = -ADAM_LR * (m_hat / (jnp.sqrt(v_hat) + ADAM_EPS) + ADAM_WD * w_ref[...])
        m_out[...] = mn
        v_out[...] = vn

    def used(k):
        return True if split is None else (_mesh_pos()[3] < split) == (k == 0)

    if by_chip:
        parts = [p.reshape(N_DEV // 2, 2, R, C) for p in parts]
        part_specs = [pl.BlockSpec((N_DEV // 2, None, tr, C),
                                   lambda i, k=k: (0, lax.axis_index("c"), jnp.where(used(k), i, 0), 0))
                      for k in range(npar)]
    else:
        part_specs = [pl.BlockSpec((N_DEV, tr, C), lambda i, k=k: (0, jnp.where(used(k), i, 0), 0))
                      for k in range(npar)]
    blk = pl.BlockSpec((tr, C), lambda i: (i, 0))
    shp = jax.ShapeDtypeStruct((R, C), F32)
    return pl.pallas_call(
        body, name=name, grid=(R // tr,),
        in_specs=part_specs + [blk, blk, blk],
        out_specs=(blk, blk, blk, blk), out_shape=(shp, shp, shp, shp),
        compiler_params=_params(("parallel",)),
    )(*parts, w, m, v)


def _local_step(x, target, norm_g, w_in, conv_w, conv_b, ln_g, ln_b, w_out, gf, exchanges=None, first_weights=None,
                late_weights=None, first_rows=ATT_W + 2 * KV_W + ATT_W // 2):
    ex_out, ex_att, ex_conv = exchanges if exchanges is not None else (None, None, None)
    h_rm, h, *first = _norm_rows(x, norm_g, *(first_weights[::2] if first_weights is not None else ()))
    if first_weights is not None:
        w_in = first_weights[1](first)
    conv_cols = w_in.shape[0] - 2 * ATT_W - 2 * KV_W
    q, kv, a_gate, gates, *gathered = _inproj(
        h_rm, h, w_in,
        [(ATT_W, HEAD_DIM ** -0.5, True), (2 * KV_W, 1.0, True), (ATT_W, 1.0, True), (conv_cols, 1.0, False)],
        late_weights[0] if late_weights is not None else None)
    if late_weights is not None:
        conv_w, w_out = late_weights[1](gathered)

    alone = [_attn_fwd(q, kv, dil, "attn_fwd_d%d" % dil) for _, dil in PATTERNS[1:]]
    o, lse, y_att = _attn_fwd(q, kv, PATTERNS[0][1], "attn_fwd_d%d" % PATTERNS[0][1], alone, a_gate)
    conv_out, y_conv = _conv_fwd(gates, conv_w, conv_b, ln_g, ln_b)
    dx2, dxb, loss_cols, g_gf = _outproj_loss(x, y_att, y_conv, w_out, gf, target)

    d_o, d_a_gate, delta, dxb_rm = _dy_att(dxb, w_out, a_gate, o)
    g_w_out, = _tn_matmul([(y_att, dxb_rm), (y_conv, dxb)], [[(0, 0, ATT_W), (1, 0, y_conv.shape[1])]], "gw_out")
    acc, out_parts = None, []
    for idx, (_, dil) in enumerate(reversed(PATTERNS)):
        hosted = ex_out(g_w_out) if (idx == 0 and ex_out is not None) else None
        acc, outs = _attn_bwd(q, kv, d_o, lse, delta, dil, acc, idx == len(PATTERNS) - 1, "attn_bwd_d%d" % dil,
                              hosted)
        out_parts += outs
    dq, dkv = acc
    a_lo = first_rows - (ATT_W + 2 * KV_W)
    assert 0 < a_lo < ATT_W
    g_first, g_a_rest = _tn_matmul(
        [(dq, h_rm), (dkv, h_rm), (d_a_gate, h_rm)],
        [[(0, 0, ATT_W), (1, 0, 2 * KV_W), (2, 0, a_lo)], [(2, a_lo, ATT_W - a_lo)]], "gw_in_att")

    d_c_gate, d_conv, g_ln_g, g_ln_b, g_conv_b = _dy_conv(dxb, w_out, gates, conv_out, ln_g, ln_b)
    dgates, g_conv_w, att_parts = _conv_bwd(d_conv, gates, d_c_gate, conv_w,
                                            ex_att(g_first) if ex_att is not None else None)
    g_rest, = _tn_matmul([(dgates, h)], [[(g_a_rest, 0, ATT_W - a_lo), (0, 0, conv_cols)]], "gw_in_conv")
    grad_x, g_norm_g, conv_parts = _dh(
        [(dq, True), (dkv, True), (d_a_gate, True), (dgates, False)], w_in, x, dx2, norm_g,
        ex_conv(g_rest, g_conv_w) if ex_conv is not None else None)
    small = (g_norm_g, g_conv_b, g_ln_g, g_ln_b, g_gf, loss_cols)
    return grad_x, (g_first, g_rest), g_w_out, g_conv_w, small, (out_parts, att_parts, conv_parts)


def kernel(x, norm_g, w_in, conv_w, conv_b, conv_ln_g, conv_ln_b, w_out, final_norm_g, loss_target, m_norm_g, m_w_in, m_conv_w, m_conv_b, m_conv_ln_g, m_conv_ln_b, m_w_out, m_final_norm_g, v_norm_g, v_w_in, v_conv_w, v_conv_b, v_conv_ln_g, v_conv_ln_b, v_w_out, v_final_norm_g):
    S, D = x.shape[1], x.shape[2]
    win_sh, wout_sh, cw_sh = w_in[0].T, w_out[0], conv_w[0]
    cols_sh, rows_sh, ch_sh = win_sh.shape[0], wout_sh.shape[0], cw_sh.shape[1]

    def first_weights(gathered):
        return gathered[0].reshape(N_DEV * cols_sh, D)

    def late_weights(gathered):
        wout_all, cw_all = gathered
        conv_w_full = cw_all.transpose(1, 0, 2).reshape(CONV_K, N_DEV * ch_sh)
        return jnp.pad(conv_w_full, ((0, CONV_HALO - CONV_K), (0, 0))), wout_all.reshape(N_DEV * rows_sh, D)

    gf = final_norm_g.reshape(1, D)

    first = -(-(ATT_W + 2 * KV_W) // cols_sh)

    def ex_out(g_w_out):
        return _Exchange([g_w_out.reshape(N_DEV, rows_sh, D)], [(0, N_DEV)])

    same_core = (2, 4, 6)

    def ex_att(g_first):
        mine = _chip_sum(g_first.reshape(first, cols_sh, D), 0, "rs_att")
        return _Exchange([mine], [(0, first)], [same_core])

    def ex_conv(g_rest, g_conv_w):
        mine = _chip_sum(g_rest.reshape(N_DEV - first, cols_sh, D), first, "rs_conv")
        return _Exchange(
            [mine, g_conv_w[:CONV_K].reshape(CONV_K, N_DEV, ch_sh).transpose(1, 0, 2)],
            [(first, N_DEV), (0, N_DEV)], [same_core, None])

    grad_x, _, _, _, small, parts = _local_step(
        x[0], loss_target[0], norm_g, None, None, conv_b, conv_ln_g, conv_ln_b, None, gf,
        (ex_out, ex_att, ex_conv), (_Gather([jax.ShapeDtypeStruct(win_sh.shape, BF16)]), first_weights, [win_sh]),
        (_Gather([wout_sh.astype(BF16), cw_sh]), late_weights), first * cols_sh)
    (wout_parts,), (win_parts_lo,), (win_parts_hi, cw_parts) = parts

    small_pack = jnp.concatenate(list(small) + [jnp.zeros((2, D), F32)], axis=0)
    small_parts, = _exchange(_Exchange([small_pack], [None]), "gather_small")

    upd_win = _adamw((win_parts_lo, win_parts_hi), win_sh, m_w_in[0].T, v_w_in[0].T, "adamw_w_in",
                     tr=cols_sh // 2, split=first, by_chip=True)
    upd_wout = _adamw(wout_parts, wout_sh, m_w_out[0], v_w_out[0], "adamw_w_out", tr=128)
    upd_cw = _adamw(cw_parts, cw_sh, m_conv_w[0], v_conv_w[0], "adamw_conv_w")
    zeros3 = jnp.zeros((3, D), F32)
    stack = lambda a, b, c, d_, e: jnp.concatenate([a, b, c, d_, e.reshape(1, D), zeros3], axis=0)
    upd_small = _adamw(
        small_parts,
        stack(norm_g, conv_b, conv_ln_g, conv_ln_b, final_norm_g),
        stack(m_norm_g, m_conv_b, m_conv_ln_g, m_conv_ln_b, m_final_norm_g),
        stack(v_norm_g, v_conv_b, v_conv_ln_g, v_conv_ln_b, v_final_norm_g) + jnp.concatenate(
            [jnp.zeros((5, D), F32), jnp.ones((3, D), F32)], axis=0),
        "adamw_small")

    loss = 0.5 / D * jnp.sum(upd_small[0][5])

    def outputs(kind):
        sm = upd_small[kind]
        return [sm[0:1], upd_win[kind].T[None], upd_cw[kind][None], sm[1:2], sm[2:3], sm[3:4],
                upd_wout[kind][None], sm[4]]

    return (loss, grad_x[None], *outputs(0), *outputs(1), *outputs(2), *outputs(3))
```
